```python
import jax, jax.numpy as jnp
from jax import lax
import numpy as np

D_MODEL = 2048
BATCH = 8
SEQ = 8192
DEPTH = 1

MEM_LEN = 256
D_CONV = 2048
CONV_WIDTH = 31
ML_HEADS = 4
D_ML = 2048
ML_HEAD_DIM = D_ML // ML_HEADS
ML_CHUNK = 64
QK_CONV_WIDTH = 4
XA_HEADS = 4
D_XA = 2048
XA_HEAD_DIM = D_XA // XA_HEADS
N_BRANCH = 3
EPS = 1e-6

IN_GROUPS = (
    ("glu_a", D_CONV), ("glu_b", D_CONV), ("z_conv", D_CONV),
    ("qk_ml", 2 * D_ML), ("v_ml", D_ML), ("o_ml", D_ML), ("z_ml", D_ML),
    ("if_ml", 2 * ML_HEADS),
    ("q_xa", D_XA), ("z_xa", D_XA),
    ("gates", N_BRANCH * D_MODEL),
)
N_IN = 3 * D_CONV + 5 * D_ML + 2 * ML_HEADS + 2 * D_XA + N_BRANCH * D_MODEL

kernel_name = "hybrid_conformer_mlstm_memxattn_gated"


def rms_norm(x, g):
    xf = x.astype(jnp.float32)
    y = xf * lax.rsqrt(jnp.mean(xf * xf, axis=-1, keepdims=True) + EPS)
    return (y * g.astype(jnp.float32)).astype(x.dtype)


def layer_norm(x, g, b):
    xf = x.astype(jnp.float32)
    mu = jnp.mean(xf, axis=-1, keepdims=True)
    var = jnp.mean(jnp.square(xf - mu), axis=-1, keepdims=True)
    y = (xf - mu) * lax.rsqrt(var + EPS)
    return (y * g.astype(jnp.float32) + b.astype(jnp.float32)).astype(x.dtype)


def causal_depthwise_conv(x, w):
    width, ch = w.shape
    return lax.conv_general_dilated(
        x, w[:, None, :].astype(x.dtype), window_strides=(1,),
        padding=((width - 1, 0),), dimension_numbers=("NWC", "WIO", "NWC"),
        feature_group_count=ch)


def in_cols(h, w_in, name):
    start = 0
    for n, width in IN_GROUPS:
        if n == name:
            return h @ w_in[:, start:start + width]
        start += width
    raise ValueError(name)


def mlstm_chunkwise(q, k, v, log_i, log_f):
    B, H, S, Dh = q.shape
    L = ML_CHUNK
    nc = S // L

    def to_chunks(a):
        return jnp.moveaxis(a.reshape(B, H, nc, L, *a.shape[3:]), 2, 0)

    xs = (to_chunks(q), to_chunks(k), to_chunks(v), to_chunks(log_i), to_chunks(log_f))
    causal = jnp.tril(jnp.ones((L, L), dtype=bool))

    def step(carry, chunk):
        C, n, m = carry
        qj, kj, vj, li, lf = chunk
        b = jnp.cumsum(lf, axis=-1)
        d = b[..., :, None] - b[..., None, :] + li[..., None, :]
        d = jnp.where(causal, d, -jnp.inf)
        inter = b + m[..., None]
        m_row = jnp.maximum(inter, jnp.max(d, axis=-1))
        w_intra = jnp.exp(d - m_row[..., None])
        w_inter = jnp.exp(inter - m_row)
        s = jnp.einsum("bhid,bhjd->bhij", qj, kj) * w_intra
        num = (jnp.einsum("bhij,bhje->bhie", s, vj)
               + w_inter[..., None] * jnp.einsum("bhid,bhde->bhie", qj, C))
        den = jnp.sum(s, axis=-1) + w_inter * jnp.einsum("bhid,bhd->bhi", qj, n)
        h = num / jnp.maximum(jnp.abs(den), jnp.exp(-m_row))[..., None]
        b_last = b[..., -1]
        g = b_last[..., None] - b + li
        m_new = jnp.maximum(b_last + m, jnp.max(g, axis=-1))
        decay = jnp.exp(b_last + m - m_new)
        wk = jnp.exp(g - m_new[..., None])
        C = decay[..., None, None] * C + jnp.einsum("bhj,bhjd,bhje->bhde", wk, kj, vj)
        n = decay[..., None] * n + jnp.einsum("bhj,bhjd->bhd", wk, kj)
        return (C, n, m_new), h

    init = (jnp.zeros((B, H, Dh, Dh), jnp.float32),
            jnp.zeros((B, H, Dh), jnp.float32),
            jnp.zeros((B, H), jnp.float32))
    _, hs = lax.scan(step, init, xs)
    return jnp.moveaxis(hs, 0, 2).reshape(B, H, S, Dh)


def _fwd_setup_inputs(seed: int = 0) -> dict:
    key = jax.random.key(seed)
    ks = jax.random.split(key, 24)
    f32 = jnp.float32

    def nrm(k, shape, scale):
        return jax.random.normal(k, shape, f32) * scale

    def gain(k, n):
        return 1.0 + 0.02 * jax.random.normal(k, (n,), f32)

    b_i = 0.1 * jax.random.normal(ks[3], (ML_HEADS,), f32)
    b_f = jnp.linspace(3.0, 6.0, ML_HEADS, dtype=f32) + 0.1 * jax.random.normal(ks[4], (ML_HEADS,), f32)
    return {
        "x": nrm(ks[0], (BATCH, SEQ, D_MODEL), 1.0),
        "mem": nrm(ks[1], (BATCH, MEM_LEN, D_MODEL), 1.0),
        "g_pre": gain(ks[2], D_MODEL),
        "w_in": nrm(ks[5], (D_MODEL, N_IN), D_MODEL ** -0.5),
        "b_if": jnp.concatenate([b_i, b_f]),
        "w_qk_conv": nrm(ks[6], (QK_CONV_WIDTH, 2 * D_ML), QK_CONV_WIDTH ** -0.5),
        "w_dw": nrm(ks[7], (CONV_WIDTH, D_CONV), CONV_WIDTH ** -0.5),
        "b_dw": nrm(ks[8], (D_CONV,), 0.02),
        "g_ln": gain(ks[9], D_CONV),
        "b_ln": nrm(ks[10], (D_CONV,), 0.02),
        "w_conv_out": nrm(ks[11], (D_CONV, D_MODEL), D_CONV ** -0.5),
        "g_ml_head": gain(ks[12], D_ML),
        "w_ml_out": nrm(ks[13], (D_ML, D_MODEL), D_ML ** -0.5),
        "g_mem": gain(ks[14], D_MODEL),
        "w_mem_kv": nrm(ks[15], (D_MODEL, 2 * D_XA), D_MODEL ** -0.5),
        "w_xa_out": nrm(ks[16], (D_XA, D_MODEL), D_XA ** -0.5),
        "w_out": nrm(ks[17], (D_MODEL, D_MODEL), D_MODEL ** -0.5),
        "g_post": gain(ks[18], D_MODEL),
    }


def _fwd_reference(x, mem, g_pre, w_in, b_if, w_qk_conv, w_dw, b_dw, g_ln, b_ln,
              w_conv_out, g_ml_head, w_ml_out, g_mem, w_mem_kv, w_xa_out,
              w_out, g_post):
    B, S, _ = x.shape
    f32 = jnp.float32
    for _layer in range(DEPTH):
        h = rms_norm(x, g_pre)

        u = in_cols(h, w_in, "glu_a") * jax.nn.sigmoid(in_cols(h, w_in, "glu_b"))
        u = causal_depthwise_conv(u, w_dw) + b_dw.astype(u.dtype)
        u = jax.nn.silu(layer_norm(u, g_ln, b_ln))
        y_conv = (u * jax.nn.silu(in_cols(h, w_in, "z_conv"))) @ w_conv_out

        qk = jax.nn.silu(causal_depthwise_conv(in_cols(h, w_in, "qk_ml"), w_qk_conv))
        q_ml, k_ml = jnp.split(qk, 2, axis=-1)
        v_ml = in_cols(h, w_in, "v_ml")

        def heads(t):
            return t.reshape(B, S, ML_HEADS, ML_HEAD_DIM).transpose(0, 2, 1, 3).astype(f32)

        gif = in_cols(h, w_in, "if_ml").astype(f32) + b_if.astype(f32)
        log_i = gif[..., :ML_HEADS].transpose(0, 2, 1)
        log_f = jax.nn.log_sigmoid(gif[..., ML_HEADS:]).transpose(0, 2, 1)
        hm = mlstm_chunkwise(heads(q_ml), heads(k_ml) * (ML_HEAD_DIM ** -0.5),
                             heads(v_ml), log_i, log_f)
        hm = hm.transpose(0, 2, 1, 3).reshape(B, S, D_ML)
        hm = jax.nn.sigmoid(in_cols(h, w_in, "o_ml").astype(f32)) * hm
        hm = hm.reshape(B, S, ML_HEADS, ML_HEAD_DIM)
        hm = hm * lax.rsqrt(jnp.mean(hm * hm, axis=-1, keepdims=True) + EPS)
        hm = (hm * g_ml_head.astype(f32).reshape(ML_HEADS, ML_HEAD_DIM)).reshape(B, S, D_ML)
        hm = hm.astype(x.dtype)
        y_ml = (hm * jax.nn.silu(in_cols(h, w_in, "z_ml"))) @ w_ml_out

        kv = rms_norm(mem, g_mem) @ w_mem_kv
        k_m, v_m = jnp.split(kv, 2, axis=-1)
        k_m = k_m.reshape(B, -1, XA_HEADS, XA_HEAD_DIM)
        v_m = v_m.reshape(B, -1, XA_HEADS, XA_HEAD_DIM)
        q_x = in_cols(h, w_in, "q_xa").reshape(B, S, XA_HEADS, XA_HEAD_DIM)
        scores = jnp.einsum("bshd,bmhd->bhsm", q_x, k_m).astype(f32) * (XA_HEAD_DIM ** -0.5)
        p = jax.nn.softmax(scores, axis=-1).astype(x.dtype)
        o_x = jnp.einsum("bhsm,bmhd->bshd", p, v_m).reshape(B, S, D_XA)
        y_xa = (o_x * jax.nn.silu(in_cols(h, w_in, "z_xa"))) @ w_xa_out

        g_c, g_m, g_x = jnp.split(jax.nn.sigmoid(in_cols(h, w_in, "gates")), N_BRANCH, axis=-1)
        merged = g_c * y_conv + g_m * y_ml.astype(x.dtype) + g_x * y_xa
        x = x + rms_norm(merged @ w_out, g_post)
    return x


import jax as _jax
import jax.numpy as _jnp

TWIN_FORMAT = 'train_step'
FWD_PARAMS = ['x', 'mem', 'g_pre', 'w_in', 'b_if', 'w_qk_conv', 'w_dw', 'b_dw', 'g_ln', 'b_ln', 'w_conv_out', 'g_ml_head', 'w_ml_out', 'g_mem', 'w_mem_kv', 'w_xa_out', 'w_out', 'g_post']
TWIN_WEIGHTS = ['g_pre', 'w_in', 'b_if', 'w_qk_conv', 'w_dw', 'b_dw', 'g_ln', 'b_ln', 'w_conv_out', 'g_ml_head', 'w_ml_out', 'g_mem', 'w_mem_kv', 'w_xa_out', 'w_out', 'g_post']
TWIN_DIFF_INPUT = 'x'
TWIN_INPUTS = ['x', 'mem', 'g_pre', 'w_in', 'b_if', 'w_qk_conv', 'w_dw', 'b_dw', 'g_ln', 'b_ln', 'w_conv_out', 'g_ml_head', 'w_ml_out', 'g_mem', 'w_mem_kv', 'w_xa_out', 'w_out', 'g_post', 'loss_target', 'm_g_pre', 'm_w_in', 'm_b_if', 'm_w_qk_conv', 'm_w_dw', 'm_b_dw', 'm_g_ln', 'm_b_ln', 'm_w_conv_out', 'm_g_ml_head', 'm_w_ml_out', 'm_g_mem', 'm_w_mem_kv', 'm_w_xa_out', 'm_w_out', 'm_g_post', 'v_g_pre', 'v_w_in', 'v_b_if', 'v_w_qk_conv', 'v_w_dw', 'v_b_dw', 'v_g_ln', 'v_b_ln', 'v_w_conv_out', 'v_g_ml_head', 'v_w_ml_out', 'v_g_mem', 'v_w_mem_kv', 'v_w_xa_out', 'v_w_out', 'v_g_post']
TWIN_OUTPUTS = ['loss', 'grad_x', 'grad_g_pre', 'grad_w_in', 'grad_b_if', 'grad_w_qk_conv', 'grad_w_dw', 'grad_b_dw', 'grad_g_ln', 'grad_b_ln', 'grad_w_conv_out', 'grad_g_ml_head', 'grad_w_ml_out', 'grad_g_mem', 'grad_w_mem_kv', 'grad_w_xa_out', 'grad_w_out', 'grad_g_post', 'delta_g_pre', 'delta_w_in', 'delta_b_if', 'delta_w_qk_conv', 'delta_w_dw', 'delta_b_dw', 'delta_g_ln', 'delta_b_ln', 'delta_w_conv_out', 'delta_g_ml_head', 'delta_w_ml_out', 'delta_g_mem', 'delta_w_mem_kv', 'delta_w_xa_out', 'delta_w_out', 'delta_g_post', 'new_m_g_pre', 'new_m_w_in', 'new_m_b_if', 'new_m_w_qk_conv', 'new_m_w_dw', 'new_m_b_dw', 'new_m_g_ln', 'new_m_b_ln', 'new_m_w_conv_out', 'new_m_g_ml_head', 'new_m_w_ml_out', 'new_m_g_mem', 'new_m_w_mem_kv', 'new_m_w_xa_out', 'new_m_w_out', 'new_m_g_post', 'new_v_g_pre', 'new_v_w_in', 'new_v_b_if', 'new_v_w_qk_conv', 'new_v_w_dw', 'new_v_b_dw', 'new_v_g_ln', 'new_v_b_ln', 'new_v_w_conv_out', 'new_v_g_ml_head', 'new_v_w_ml_out', 'new_v_g_mem', 'new_v_w_mem_kv', 'new_v_w_xa_out', 'new_v_w_out', 'new_v_g_post']
TWIN_LEAF_KINDS = {'loss': 'loss', 'grad_x': 'grad_x', 'grad_g_pre': 'grad_w', 'grad_w_in': 'grad_w', 'grad_b_if': 'grad_w', 'grad_w_qk_conv': 'grad_w', 'grad_w_dw': 'grad_w', 'grad_b_dw': 'grad_w', 'grad_g_ln': 'grad_w', 'grad_b_ln': 'grad_w', 'grad_w_conv_out': 'grad_w', 'grad_g_ml_head': 'grad_w', 'grad_w_ml_out': 'grad_w', 'grad_g_mem': 'grad_w', 'grad_w_mem_kv': 'grad_w', 'grad_w_xa_out': 'grad_w', 'grad_w_out': 'grad_w', 'grad_g_post': 'grad_w', 'delta_g_pre': 'delta_w', 'delta_w_in': 'delta_w', 'delta_b_if': 'delta_w', 'delta_w_qk_conv': 'delta_w', 'delta_w_dw': 'delta_w', 'delta_b_dw': 'delta_w', 'delta_g_ln': 'delta_w', 'delta_b_ln': 'delta_w', 'delta_w_conv_out': 'delta_w', 'delta_g_ml_head': 'delta_w', 'delta_w_ml_out': 'delta_w', 'delta_g_mem': 'delta_w', 'delta_w_mem_kv': 'delta_w', 'delta_w_xa_out': 'delta_w', 'delta_w_out': 'delta_w', 'delta_g_post': 'delta_w', 'new_m_g_pre': 'new_m', 'new_m_w_in': 'new_m', 'new_m_b_if': 'new_m', 'new_m_w_qk_conv': 'new_m', 'new_m_w_dw': 'new_m', 'new_m_b_dw': 'new_m', 'new_m_g_ln': 'new_m', 'new_m_b_ln': 'new_m', 'new_m_w_conv_out': 'new_m', 'new_m_g_ml_head': 'new_m', 'new_m_w_ml_out': 'new_m', 'new_m_g_mem': 'new_m', 'new_m_w_mem_kv': 'new_m', 'new_m_w_xa_out': 'new_m', 'new_m_w_out': 'new_m', 'new_m_g_post': 'new_m', 'new_v_g_pre': 'new_v', 'new_v_w_in': 'new_v', 'new_v_b_if': 'new_v', 'new_v_w_qk_conv': 'new_v', 'new_v_w_dw': 'new_v', 'new_v_b_dw': 'new_v', 'new_v_g_ln': 'new_v', 'new_v_b_ln': 'new_v', 'new_v_w_conv_out': 'new_v', 'new_v_g_ml_head': 'new_v', 'new_v_w_ml_out': 'new_v', 'new_v_g_mem': 'new_v', 'new_v_w_mem_kv': 'new_v', 'new_v_w_xa_out': 'new_v', 'new_v_w_out': 'new_v', 'new_v_g_post': 'new_v'}


def _forward(args):
    return _fwd_reference(*[args[k] for k in FWD_PARAMS])


def _output_shape():
    def fwd():
        inp = _fwd_setup_inputs(0)
        return _fwd_reference(*[inp[k] for k in FWD_PARAMS])
    out = _jax.eval_shape(fwd)
    return out.shape, out.dtype

N_MICROBATCH = 1
ADAM_LR = 0.001
ADAM_B1 = 0.9
ADAM_B2 = 0.999
ADAM_EPS = 1e-08
ADAM_WD = 0.01
ADAM_STEP = 10
PER_EXAMPLE_BATCH_AXIS = {'x': 0, 'mem': 0, 'loss_target': 0}
SHARED_INPUTS = []
_WEIGHT_DTYPES = {'g_pre': _jnp.float32, 'w_in': _jnp.float32, 'b_if': _jnp.float32, 'w_qk_conv': _jnp.float32, 'w_dw': _jnp.float32, 'b_dw': _jnp.float32, 'g_ln': _jnp.float32, 'b_ln': _jnp.float32, 'w_conv_out': _jnp.float32, 'g_ml_head': _jnp.float32, 'w_ml_out': _jnp.float32, 'g_mem': _jnp.float32, 'w_mem_kv': _jnp.float32, 'w_xa_out': _jnp.float32, 'w_out': _jnp.float32, 'g_post': _jnp.float32}
MOMENT_SCALE = {'g_pre': 3.525222e-01, 'w_in': 9.327991e-02, 'b_if': 2.614875e+00, 'w_qk_conv': 6.689818e-02, 'w_dw': 9.499041e-02, 'b_dw': 2.598480e-01, 'g_ln': 1.248316e-01, 'b_ln': 1.577813e-01, 'w_conv_out': 1.054468e-01, 'g_ml_head': 1.513552e-01, 'w_ml_out': 1.584762e-01, 'g_mem': 2.325681e-02, 'w_mem_kv': 1.634918e-02, 'w_xa_out': 1.719357e-02, 'w_out': 1.914983e-01, 'g_post': 3.199397e+01}


def _to_microbatches(a, axis):
    t = _jnp.moveaxis(a, axis, 0)
    t = t.reshape((N_MICROBATCH, t.shape[0] // N_MICROBATCH) + t.shape[1:])
    return _jnp.moveaxis(t, 1, axis + 1)


def setup_inputs(seed: int = 0) -> dict:
    inp = _fwd_setup_inputs(seed)
    key = _jax.random.fold_in(_jax.random.key(seed), 7919)
    shape, _ = _output_shape()
    out = dict(inp)
    out["loss_target"] = _jax.random.normal(_jax.random.fold_in(key, 0), shape, _jnp.float32)
    for i, name in enumerate(TWIN_WEIGHTS):
        w = inp[name].astype(_jnp.float32)
        if MOMENT_SCALE is None:
            s = _jnp.sqrt(_jnp.mean(_jnp.square(w)) + 1e-30)
        else:
            s = MOMENT_SCALE[name]
        km, kv = _jax.random.split(_jax.random.fold_in(key, i + 1))
        out[name] = w
        out["m_" + name] = s * _jax.random.normal(km, w.shape, _jnp.float32)
        out["v_" + name] = (s * s) * _jax.random.uniform(kv, w.shape, _jnp.float32, 0.5, 1.5)
    if N_MICROBATCH > 1:
        for name, axis in PER_EXAMPLE_BATCH_AXIS.items():
            out[name] = _to_microbatches(out[name], axis)
    return {'x': out['x'], 'mem': out['mem'], 'g_pre': out['g_pre'], 'w_in': out['w_in'], 'b_if': out['b_if'], 'w_qk_conv': out['w_qk_conv'], 'w_dw': out['w_dw'], 'b_dw': out['b_dw'], 'g_ln': out['g_ln'], 'b_ln': out['b_ln'], 'w_conv_out': out['w_conv_out'], 'g_ml_head': out['g_ml_head'], 'w_ml_out': out['w_ml_out'], 'g_mem': out['g_mem'], 'w_mem_kv': out['w_mem_kv'], 'w_xa_out': out['w_xa_out'], 'w_out': out['w_out'], 'g_post': out['g_post'], 'loss_target': out['loss_target'], 'm_g_pre': out['m_g_pre'], 'm_w_in': out['m_w_in'], 'm_b_if': out['m_b_if'], 'm_w_qk_conv': out['m_w_qk_conv'], 'm_w_dw': out['m_w_dw'], 'm_b_dw': out['m_b_dw'], 'm_g_ln': out['m_g_ln'], 'm_b_ln': out['m_b_ln'], 'm_w_conv_out': out['m_w_conv_out'], 'm_g_ml_head': out['m_g_ml_head'], 'm_w_ml_out': out['m_w_ml_out'], 'm_g_mem': out['m_g_mem'], 'm_w_mem_kv': out['m_w_mem_kv'], 'm_w_xa_out': out['m_w_xa_out'], 'm_w_out': out['m_w_out'], 'm_g_post': out['m_g_post'], 'v_g_pre': out['v_g_pre'], 'v_w_in': out['v_w_in'], 'v_b_if': out['v_b_if'], 'v_w_qk_conv': out['v_w_qk_conv'], 'v_w_dw': out['v_w_dw'], 'v_b_dw': out['v_b_dw'], 'v_g_ln': out['v_g_ln'], 'v_b_ln': out['v_b_ln'], 'v_w_conv_out': out['v_w_conv_out'], 'v_g_ml_head': out['v_g_ml_head'], 'v_w_ml_out': out['v_w_ml_out'], 'v_g_mem': out['v_g_mem'], 'v_w_mem_kv': out['v_w_mem_kv'], 'v_w_xa_out': out['v_w_xa_out'], 'v_w_out': out['v_w_out'], 'v_g_post': out['v_g_post']}


def _loss(weights, diff, rest, loss_target):
    with _jax.named_scope("forward"):
        args = {**rest, TWIN_DIFF_INPUT: diff, **{k: w.astype(_WEIGHT_DTYPES[k]) for k, w in weights.items()}}
        y = _forward(args)
    with _jax.named_scope("loss_head"):
        err = _jnp.square(y.astype(_jnp.float32) - loss_target)
        return 0.5 * _jnp.sum(_jnp.mean(err, axis=-1)) if err.ndim else 0.5 * err


def _adamw(w, g, m, v):
    m = ADAM_B1 * m + (1.0 - ADAM_B1) * g
    v = ADAM_B2 * v + (1.0 - ADAM_B2) * _jnp.square(g)
    m_hat = m / (1.0 - ADAM_B1 ** ADAM_STEP)
    v_hat = v / (1.0 - ADAM_B2 ** ADAM_STEP)
    delta = -ADAM_LR * (m_hat / (_jnp.sqrt(v_hat) + ADAM_EPS) + ADAM_WD * w)
    return delta, m, v


def reference(x, mem, g_pre, w_in, b_if, w_qk_conv, w_dw, b_dw, g_ln, b_ln, w_conv_out, g_ml_head, w_ml_out, g_mem, w_mem_kv, w_xa_out, w_out, g_post, loss_target, m_g_pre, m_w_in, m_b_if, m_w_qk_conv, m_w_dw, m_b_dw, m_g_ln, m_b_ln, m_w_conv_out, m_g_ml_head, m_w_ml_out, m_g_mem, m_w_mem_kv, m_w_xa_out, m_w_out, m_g_post, v_g_pre, v_w_in, v_b_if, v_w_qk_conv, v_w_dw, v_b_dw, v_g_ln, v_b_ln, v_w_conv_out, v_g_ml_head, v_w_ml_out, v_g_mem, v_w_mem_kv, v_w_xa_out, v_w_out, v_g_post):
    given = dict(x=x, mem=mem, g_pre=g_pre, w_in=w_in, b_if=b_if, w_qk_conv=w_qk_conv, w_dw=w_dw, b_dw=b_dw, g_ln=g_ln, b_ln=b_ln, w_conv_out=w_conv_out, g_ml_head=g_ml_head, w_ml_out=w_ml_out, g_mem=g_mem, w_mem_kv=w_mem_kv, w_xa_out=w_xa_out, w_out=w_out, g_post=g_post, loss_target=loss_target, m_g_pre=m_g_pre, m_w_in=m_w_in, m_b_if=m_b_if, m_w_qk_conv=m_w_qk_conv, m_w_dw=m_w_dw, m_b_dw=m_b_dw, m_g_ln=m_g_ln, m_b_ln=m_b_ln, m_w_conv_out=m_w_conv_out, m_g_ml_head=m_g_ml_head, m_w_ml_out=m_w_ml_out, m_g_mem=m_g_mem, m_w_mem_kv=m_w_mem_kv, m_w_xa_out=m_w_xa_out, m_w_out=m_w_out, m_g_post=m_g_post, v_g_pre=v_g_pre, v_w_in=v_w_in, v_b_if=v_b_if, v_w_qk_conv=v_w_qk_conv, v_w_dw=v_w_dw, v_b_dw=v_b_dw, v_g_ln=v_g_ln, v_b_ln=v_b_ln, v_w_conv_out=v_w_conv_out, v_g_ml_head=v_g_ml_head, v_w_ml_out=v_w_ml_out, v_g_mem=v_g_mem, v_w_mem_kv=v_w_mem_kv, v_w_xa_out=v_w_xa_out, v_w_out=v_w_out, v_g_post=v_g_post)
    weights = {n: given[n] for n in TWIN_WEIGHTS}
    shared = {n: given[n] for n in SHARED_INPUTS}
    per_example = {n: given[n] for n in ['x', 'mem']}
    grad_fn = _jax.value_and_grad(_loss, argnums=(0, 1))

    def one_microbatch(ex, loss_target):
        ex = dict(ex)
        diff = ex.pop(TWIN_DIFF_INPUT)
        return grad_fn(weights, diff, {**shared, **ex}, loss_target)

    if N_MICROBATCH == 1:
        loss, (grad_w, grad_x) = one_microbatch(per_example, given["loss_target"])
    else:
        def body(carry, xs):
            loss_sum, grad_sum = carry
            l_k, (gw_k, gx_k) = one_microbatch(xs[0], xs[1])
            with _jax.named_scope("update"):
                return (loss_sum + l_k, _jax.tree.map(_jnp.add, grad_sum, gw_k)), gx_k

        init = (_jnp.zeros((), _jnp.float32), _jax.tree.map(_jnp.zeros_like, weights))
        (loss, grad_w), grad_x = _jax.lax.scan(body, init, (per_example, given["loss_target"]))
    with _jax.named_scope("update"):
        delta_w, new_m, new_v = {}, {}, {}
        for n in TWIN_WEIGHTS:
            delta_w[n], new_m[n], new_v[n] = _adamw(weights[n], grad_w[n], given["m_" + n], given["v_" + n])
    return (loss, grad_x, *[grad_w[n] for n in TWIN_WEIGHTS], *[delta_w[n] for n in TWIN_WEIGHTS],
            *[new_m[n] for n in TWIN_WEIGHTS], *[new_v[n] for n in TWIN_WEIGHTS])
```

```python
import functools
import math

import jax
import jax.numpy as jnp
from jax import lax
from jax.experimental import pallas as pl
from jax.experimental.pallas import tpu as pltpu

F32 = jnp.float32
BF16 = jnp.bfloat16

N_DEV = 8
N_HEADS = 4
CONV_W = 31
QK_W = 4
N_IF = 2 * N_HEADS
IF_PAD = 128
EPS = 1e-6
NEG = -1e30

ADAM_LR = 0.001
ADAM_B1 = 0.9
ADAM_B2 = 0.999
ADAM_EPS = 1e-08
ADAM_WD = 0.01
ADAM_STEP = 10

TOK_TILE = 256
ML_CHUNK = 256
HALO = 32
VMEM_LIMIT = 56 * 1024 * 1024


def _cparams(sem=None):
    kw = dict(vmem_limit_bytes=VMEM_LIMIT)
    if sem is not None:
        kw["dimension_semantics"] = sem
    return pltpu.CompilerParams(**kw)


def _sig(x):
    return jax.nn.sigmoid(x)


def _silu(x):
    return x * _sig(x)


def _dsilu(x):
    s = _sig(x)
    return s * (1.0 + x * (1.0 - s))


def _logsig(x):
    return jnp.minimum(x, 0.0) - jnp.log(1.0 + jnp.exp(-jnp.abs(x)))


def _dot(a, b, dims):
    return lax.dot_general(a, b, (dims, ((), ())), preferred_element_type=F32)


def _dot_nn(a, b):
    return _dot(a, b, ((1,), (0,)))


def _dot_nt(a, b):
    return _dot(a, b, ((1,), (1,)))


def _dot_tn(a, b):
    return _dot(a, b, ((0,), (0,)))


def _split3(a):
    hi = a.astype(BF16)
    r1 = a - hi.astype(F32)
    mid = r1.astype(BF16)
    lo = (r1 - mid.astype(F32)).astype(BF16)
    return hi, mid, lo


def _tri_left(tri, a):
    hi, mid, lo = _split3(a)
    return _dot_nn(tri, hi) + _dot_nn(tri, mid) + _dot_nn(tri, lo)


def _tri_right(a, tri):
    hi, mid, lo = _split3(a)
    return _dot_nn(hi, tri) + _dot_nn(mid, tri) + _dot_nn(lo, tri)


def _fit(n, t):
    if n <= t:
        return n
    return max(c for c in range(128, t + 1, 128) if n % c == 0)


def _mm(a, b, *, mode, out_dtype, tm, tn, tk, name, n_outer=False, acc_in=None):
    if mode == "nn":
        (M, K), (K2, N) = a.shape, b.shape
    elif mode == "nt":
        (M, K), (N, K2) = a.shape, b.shape
    else:
        (K, M), (K2, N) = a.shape, b.shape
    assert K == K2, (a.shape, b.shape, mode)
    tm, tn, tk = _fit(M, tm), _fit(N, tn), _fit(K, tk)
    assert M % tm == 0 and N % tn == 0 and K % tk == 0, (a.shape, b.shape, tm, tn, tk)
    nk = K // tk
    if n_outer:
        grid = (N // tn, M // tm, nk)
        ij = lambda g0, g1: (g1, g0)
    else:
        grid = (M // tm, N // tn, nk)
        ij = lambda g0, g1: (g0, g1)

    if mode == "nn":
        a_spec = pl.BlockSpec((tm, tk), lambda g0, g1, k: (ij(g0, g1)[0], k))
        b_spec = pl.BlockSpec((tk, tn), lambda g0, g1, k: (k, ij(g0, g1)[1]))
        dot = _dot_nn
    elif mode == "nt":
        a_spec = pl.BlockSpec((tm, tk), lambda g0, g1, k: (ij(g0, g1)[0], k))
        b_spec = pl.BlockSpec((tn, tk), lambda g0, g1, k: (ij(g0, g1)[1], k))
        dot = _dot_nt
    else:
        a_spec = pl.BlockSpec((tk, tm), lambda g0, g1, k: (k, ij(g0, g1)[0]))
        b_spec = pl.BlockSpec((tk, tn), lambda g0, g1, k: (k, ij(g0, g1)[1]))
        dot = _dot_tn
    o_spec = pl.BlockSpec((tm, tn), lambda g0, g1, k: ij(g0, g1))
    has_acc = acc_in is not None

    def body(*refs):
        if has_acc:
            a_ref, b_ref, c_ref, o_ref, acc = refs
        else:
            a_ref, b_ref, o_ref, acc = refs
        k = pl.program_id(2)
        p = dot(a_ref[...], b_ref[...])

        @pl.when(k == 0)
        def _():
            acc[...] = p

        @pl.when(k > 0)
        def _():
            acc[...] += p

        @pl.when(k == nk - 1)
        def _():
            r = acc[...]
            if has_acc:
                r = r + c_ref[...].astype(F32)
            o_ref[...] = r.astype(out_dtype)

    in_specs = [a_spec, b_spec]
    args = [a, b]
    if has_acc:
        in_specs.append(o_spec)
        args.append(acc_in)
    return pl.pallas_call(
        body, name=name, grid=grid, in_specs=in_specs, out_specs=o_spec,
        out_shape=jax.ShapeDtypeStruct((M, N), out_dtype),
        scratch_shapes=[pltpu.VMEM((tm, tn), F32)],
        compiler_params=_cparams(("parallel", "parallel", "arbitrary")),
    )(*args)


def _rmsnorm_fwd(x, g, name):
    S, D = x.shape
    T = min(TOK_TILE, S)

    def body(x_ref, g_ref, h_ref, r_ref):
        xv = x_ref[...]
        r = lax.rsqrt(jnp.mean(xv * xv, axis=-1, keepdims=True) + EPS)
        h_ref[...] = (xv * r * g_ref[...]).astype(BF16)
        r_ref[...] = r

    return pl.pallas_call(
        body, name=name, grid=(S // T,),
        in_specs=[pl.BlockSpec((T, D), lambda i: (i, 0)), pl.BlockSpec((1, D), lambda i: (0, 0))],
        out_specs=[pl.BlockSpec((T, D), lambda i: (i, 0)), pl.BlockSpec((T, 1), lambda i: (i, 0))],
        out_shape=[jax.ShapeDtypeStruct((S, D), BF16), jax.ShapeDtypeStruct((S, 1), F32)],
        compiler_params=_cparams(("parallel",)),
    )(x, g)


def _conv_fwd(z, w_dw, b_dw, g_ln, b_ln, D):
    S = z.shape[0]
    T = min(TOK_TILE, S)
    hb = T // HALO

    def body(a_ref, b_ref, zc_ref, ah_ref, bh_ref, w_ref, bdw_ref, gln_ref, bln_ref, u1_ref, pc_ref, u0s):
        i = pl.program_id(0)
        a = a_ref[...].astype(F32)
        b = b_ref[...].astype(F32)
        u0s[pl.ds(HALO, T), :] = a * _sig(b)
        halo = ah_ref[...].astype(F32) * _sig(bh_ref[...].astype(F32))
        u0s[pl.ds(0, HALO), :] = jnp.where(i > 0, halo, 0.0)
        acc = jnp.zeros((T, D), F32) + bdw_ref[...]
        for j in range(CONV_W):
            acc = acc + w_ref[pl.ds(j, 1), :] * u0s[pl.ds(HALO - (CONV_W - 1) + j, T), :]
        u1_ref[...] = acc
        mu = jnp.mean(acc, axis=-1, keepdims=True)
        xc = acc - mu
        var = jnp.mean(xc * xc, axis=-1, keepdims=True)
        ln = xc * lax.rsqrt(var + EPS) * gln_ref[...] + bln_ref[...]
        pc_ref[...] = (_silu(ln) * _silu(zc_ref[...].astype(F32))).astype(BF16)

    prev = lambda i: (jnp.maximum(i * hb - 1, 0), 0)
    prev_b = lambda i: (jnp.maximum(i * hb - 1, 0), D // D)
    vec = pl.BlockSpec((1, D), lambda i: (0, 0))
    return pl.pallas_call(
        body, name="conv_fwd", grid=(S // T,),
        in_specs=[pl.BlockSpec((T, D), lambda i: (i, 0)), pl.BlockSpec((T, D), lambda i: (i, 1)),
                  pl.BlockSpec((T, D), lambda i: (i, 2)),
                  pl.BlockSpec((HALO, D), prev), pl.BlockSpec((HALO, D), prev_b),
                  pl.BlockSpec((32, D), lambda i: (0, 0)), vec, vec, vec],
        out_specs=[pl.BlockSpec((T, D), lambda i: (i, 0)), pl.BlockSpec((T, D), lambda i: (i, 0))],
        out_shape=[jax.ShapeDtypeStruct((S, D), F32), jax.ShapeDtypeStruct((S, D), BF16)],
        scratch_shapes=[pltpu.VMEM((T + HALO, D), F32)],
        compiler_params=_cparams(("parallel",)),
    )(z, z, z, z, z, w_dw, b_dw, g_ln, b_ln)


def _qk_fwd(z, w_qk, D):
    S = z.shape[0]
    T = min(TOK_TILE, S)
    hb = T // HALO
    scale = (D // N_HEADS) ** -0.5
    W2 = 2 * D

    def body(p0_ref, p1_ref, h0_ref, h1_ref, w_ref, o_ref, ps):
        i = pl.program_id(0)
        ps[pl.ds(HALO, T), pl.ds(0, D)] = p0_ref[...].astype(F32)
        ps[pl.ds(HALO, T), pl.ds(D, D)] = p1_ref[...].astype(F32)
        ps[pl.ds(0, HALO), pl.ds(0, D)] = jnp.where(i > 0, h0_ref[...].astype(F32), 0.0)
        ps[pl.ds(0, HALO), pl.ds(D, D)] = jnp.where(i > 0, h1_ref[...].astype(F32), 0.0)
        acc = jnp.zeros((T, W2), F32)
        for j in range(QK_W):
            acc = acc + w_ref[pl.ds(j, 1), :] * ps[pl.ds(HALO - (QK_W - 1) + j, T), :]
        qk = _silu(acc)
        o_ref[:, pl.ds(0, D)] = qk[:, :D].astype(BF16)
        o_ref[:, pl.ds(D, D)] = (qk[:, D:] * scale).astype(BF16)

    prev0 = lambda i: (jnp.maximum(i * hb - 1, 0), 3)
    prev1 = lambda i: (jnp.maximum(i * hb - 1, 0), 4)
    return pl.pallas_call(
        body, name="qk_fwd", grid=(S // T,),
        in_specs=[pl.BlockSpec((T, D), lambda i: (i, 3)), pl.BlockSpec((T, D), lambda i: (i, 4)),
                  pl.BlockSpec((HALO, D), prev0), pl.BlockSpec((HALO, D), prev1),
                  pl.BlockSpec((8, W2), lambda i: (0, 0))],
        out_specs=pl.BlockSpec((T, W2), lambda i: (i, 0)),
        out_shape=jax.ShapeDtypeStruct((S, W2), BF16),
        scratch_shapes=[pltpu.VMEM((T + HALO, W2), F32)],
        compiler_params=_cparams(("parallel",)),
    )(z, z, z, z, w_qk)


def _ml_gates(gc_ref, gr_ref, bifr_ref, bifc_ref, L):
    gc = gc_ref[...] + bifr_ref[...]
    gr = gr_ref[...] + bifc_ref[...]
    ii = lax.broadcasted_iota(jnp.int32, (L, L), 0)
    jj = lax.broadcasted_iota(jnp.int32, (L, L), 1)
    tri = (jj <= ii).astype(BF16)
    triu = (ii <= jj).astype(BF16)
    bcol_all = _tri_left(tri, _logsig(gc))
    brow_all = _tri_right(_logsig(gr), triu)
    return gc, gr, bcol_all, brow_all, ii, jj


def _ml_intra(q, k, b_c, b_r, li_r, m, n_row, ii, jj):
    d = b_c - b_r + li_r
    dm = jnp.where(jj <= ii, d, NEG)
    inter = b_c + m
    m_row = jnp.maximum(inter, jnp.max(dm, axis=1, keepdims=True))
    w_intra = jnp.exp(dm - m_row)
    w_inter = jnp.exp(inter - m_row)
    qk = _dot_nt(q, k)
    s = qk * w_intra
    qn = jnp.sum(q.astype(F32) * n_row, axis=-1, keepdims=True)
    den = jnp.sum(s, axis=-1, keepdims=True) + w_inter * qn
    hdiv = jnp.maximum(jnp.abs(den), jnp.exp(-m_row))
    return qk, w_intra, w_inter, s, qn, den, hdiv, m_row


def _ml_state(b_c, li_c, m, L):
    b_last = b_c[L - 1:L, :]
    g_c = b_last - b_c + li_c
    m_new = jnp.maximum(b_last + m, jnp.max(g_c, axis=0, keepdims=True))
    decay = jnp.exp(b_last + m - m_new)
    wk = jnp.exp(g_c - m_new)
    return m_new, decay, wk


def _mlstm_fwd(qks, z, gcol, grow, bif_row, bif_col, D):
    S = qks.shape[0]
    L = min(ML_CHUNK, S)
    NC = S // L
    H = N_HEADS
    dh = D // H

    def body(q_ref, k_ref, v_ref, gc_ref, gr_ref, bifr_ref, bifc_ref, hm_ref, cst_ref, nst_ref, C, NM):
        c = pl.program_id(0)

        @pl.when(c == 0)
        def _():
            C[...] = jnp.zeros_like(C)
            NM[...] = jnp.zeros_like(NM)

        gc, gr, bcol_all, brow_all, ii, jj = _ml_gates(gc_ref, gr_ref.at[0], bifr_ref, bifc_ref, L)
        for h in range(H):
            hs = pl.ds(h * dh, dh)
            q = q_ref[:, hs]
            k = k_ref[:, hs]
            v = v_ref[:, hs]
            b_c = bcol_all[:, H + h:H + h + 1]
            li_c = gc[:, h:h + 1]
            b_r = brow_all[H + h:H + h + 1, :]
            li_r = gr[h:h + 1, :]
            n_row = NM[h, 0:1, :]
            m = NM[h, 1:2, 0:1]
            Cb = C[h].astype(BF16)
            cst_ref[0, h] = Cb
            nst_ref[0, h] = NM[h]
            qk, w_intra, w_inter, s, qn, den, hdiv, m_row = _ml_intra(q, k, b_c, b_r, li_r, m, n_row, ii, jj)
            num = _dot_nn(s.astype(BF16), v) + w_inter * _dot_nn(q, Cb)
            hm_ref[:, hs] = num / hdiv
            m_new, decay, wk = _ml_state(b_c, li_c, m, L)
            wkk = wk * k.astype(F32)
            C[h] = decay * C[h] + _dot_tn(wkk.astype(BF16), v)
            NM[h, 0:1, :] = decay * n_row + jnp.sum(wkk, axis=0, keepdims=True)
            NM[h, 1:2, :] = jnp.broadcast_to(m_new, (1, dh))

    return pl.pallas_call(
        body, name="mlstm_fwd", grid=(NC,),
        in_specs=[pl.BlockSpec((L, D), lambda c: (c, 0)), pl.BlockSpec((L, D), lambda c: (c, 1)),
                  pl.BlockSpec((L, D), lambda c: (c, 5)),
                  pl.BlockSpec((L, IF_PAD), lambda c: (c, 0)),
                  pl.BlockSpec((1, 16, L), lambda c: (c, 0, 0)),
                  pl.BlockSpec((1, IF_PAD), lambda c: (0, 0)), pl.BlockSpec((16, 1), lambda c: (0, 0))],
        out_specs=[pl.BlockSpec((L, D), lambda c: (c, 0)),
                   pl.BlockSpec((1, H, dh, dh), lambda c: (c, 0, 0, 0)),
                   pl.BlockSpec((1, H, 8, dh), lambda c: (c, 0, 0, 0))],
        out_shape=[jax.ShapeDtypeStruct((S, D), F32), jax.ShapeDtypeStruct((NC, H, dh, dh), BF16),
                   jax.ShapeDtypeStruct((NC, H, 8, dh), F32)],
        scratch_shapes=[pltpu.VMEM((H, dh, dh), F32), pltpu.VMEM((H, 8, dh), F32)],
        compiler_params=_cparams(("arbitrary",)),
    )(qks, qks, z, gcol, grow, bif_row, bif_col)


def _mlstm_bwd(qks, z, gcol, grow, bif_row, bif_col, hm, dhm, cst, nst, dz, D):
    S = qks.shape[0]
    L = min(ML_CHUNK, S)
    NC = S // L
    H = N_HEADS
    dh = D // H

    def body(q_ref, k_ref, v_ref, gc_ref, gr_ref, bifr_ref, bifc_ref, hm_ref, dhm_ref, cst_ref, nst_ref, dz_any,
             dqk_ref, dg_ref, dv_ref, dC, dN):
        del dz_any
        c = pl.program_id(0)

        @pl.when(c == 0)
        def _():
            dC[...] = jnp.zeros_like(dC)
            dN[...] = jnp.zeros_like(dN)

        gc, gr, bcol_all, brow_all, ii, jj = _ml_gates(gc_ref, gr_ref.at[0], bifr_ref, bifc_ref, L)
        eye = ii == jj
        lane = lax.broadcasted_iota(jnp.int32, (L, IF_PAD), 1)
        db_all = jnp.zeros((L, IF_PAD), F32)
        dli_all = jnp.zeros((L, IF_PAD), F32)
        last_row = lax.broadcasted_iota(jnp.int32, (L, 1), 0) == L - 1
        for h in range(H):
            hs = pl.ds(h * dh, dh)
            q = q_ref[:, hs]
            k = k_ref[:, hs]
            v = v_ref[:, hs]
            qf = q.astype(F32)
            kf = k.astype(F32)
            b_c = bcol_all[:, H + h:H + h + 1]
            li_c = gc[:, h:h + 1]
            b_r = brow_all[H + h:H + h + 1, :]
            li_r = gr[h:h + 1, :]
            n_row = nst_ref[0, h, 0:1, :]
            m = nst_ref[0, h, 1:2, 0:1]
            Cb = cst_ref[0, h]
            qk, w_intra, w_inter, s, qn, den, hdiv, m_row = _ml_intra(q, k, b_c, b_r, li_r, m, n_row, ii, jj)
            dh_ = dhm_ref[:, hs]
            hh = hm_ref[:, hs]
            dnum = dh_ / hdiv
            dhdiv = -jnp.sum(dh_ * hh, axis=-1, keepdims=True) / hdiv
            dden = jnp.where(jnp.abs(den) > jnp.exp(-m_row), dhdiv * jnp.sign(den), 0.0)
            dnum_b = dnum.astype(BF16)
            ds = _dot_nt(dnum_b, v) + dden
            s_b = s.astype(BF16)
            dv = _dot_tn(s_b, dnum_b)
            dnC = _dot_nt(dnum_b, Cb)
            dwi = dden * w_inter
            dw_inter = jnp.sum(qf * dnC, axis=-1, keepdims=True) + dden * qn
            dq = w_inter * dnC + dwi * n_row
            dC_out = _dot_tn((w_inter * qf).astype(BF16), dnum_b)
            dn_out = jnp.sum(dwi * qf, axis=0, keepdims=True)
            dqk = (ds * w_intra).astype(BF16)
            dq = dq + _dot_nn(dqk, k)
            dk = _dot_tn(dqk, q)
            dd = ds * s
            rs_c = jnp.sum(dd, axis=1, keepdims=True)
            cs_r = jnp.sum(dd, axis=0, keepdims=True)
            cs_c = jnp.sum(jnp.where(eye, cs_r, 0.0), axis=1, keepdims=True)
            dinter = dw_inter * w_inter
            m_new, decay, wk = _ml_state(b_c, li_c, m, L)
            dCn = dC[h]
            dCn_b = dCn.astype(BF16)
            dn_new = dN[h, 0:1, :]
            ddecay = (jnp.sum(jnp.sum(dCn * Cb.astype(F32), axis=1, keepdims=True), axis=0, keepdims=True)
                      + jnp.sum(dn_new * n_row, axis=1, keepdims=True))
            dwkk = _dot_nt(v, dCn_b) + dn_new
            wkk_b = (wk * kf).astype(BF16)
            dv = dv + _dot_nn(wkk_b, dCn_b)
            dk = dk + wk * dwkk
            dg = jnp.sum(dwkk * kf, axis=-1, keepdims=True) * wk
            db_last = ddecay * decay + jnp.sum(dg, axis=0, keepdims=True)
            dC[h] = decay * dCn + dC_out
            dN[h, 0:1, :] = decay * dn_new + dn_out
            db_c = rs_c - cs_c + dinter - dg + jnp.where(last_row, db_last, 0.0)
            dli_c = cs_c + dg
            db_all = jnp.where(lane == H + h, db_c, db_all)
            dli_all = jnp.where(lane == h, dli_c, dli_all)
            dqk_ref[:, hs] = dq.astype(BF16)
            dqk_ref[:, pl.ds(D + h * dh, dh)] = dk.astype(BF16)
            dv_ref[:, hs] = dv.astype(BF16)
        triu = (ii <= jj).astype(BF16)
        dlf = _tri_left(triu, db_all)
        dg_ref[...] = dli_all + dlf * (1.0 - _sig(gc))

    r = lambda c: NC - 1 - c
    n_in = 12
    return pl.pallas_call(
        body, name="mlstm_bwd", grid=(NC,),
        in_specs=[pl.BlockSpec((L, D), lambda c: (r(c), 0)), pl.BlockSpec((L, D), lambda c: (r(c), 1)),
                  pl.BlockSpec((L, D), lambda c: (r(c), 5)),
                  pl.BlockSpec((L, IF_PAD), lambda c: (r(c), 0)),
                  pl.BlockSpec((1, 16, L), lambda c: (r(c), 0, 0)),
                  pl.BlockSpec((1, IF_PAD), lambda c: (0, 0)), pl.BlockSpec((16, 1), lambda c: (0, 0)),
                  pl.BlockSpec((L, D), lambda c: (r(c), 0)), pl.BlockSpec((L, D), lambda c: (r(c), 0)),
                  pl.BlockSpec((1, H, dh, dh), lambda c: (r(c), 0, 0, 0)),
                  pl.BlockSpec((1, H, 8, dh), lambda c: (r(c), 0, 0, 0)),
                  pl.BlockSpec(memory_space=pl.ANY)],
        out_specs=[pl.BlockSpec((L, 2 * D), lambda c: (r(c), 0)),
                   pl.BlockSpec((L, IF_PAD), lambda c: (r(c), 0)),
                   pl.BlockSpec((L, D), lambda c: (r(c), 5))],
        out_shape=[jax.ShapeDtypeStruct((S, 2 * D), BF16),
                   jax.ShapeDtypeStruct((S, IF_PAD), F32), jax.ShapeDtypeStruct(dz.shape, dz.dtype)],
        input_output_aliases={n_in - 1: 2},
        scratch_shapes=[pltpu.VMEM((H, dh, dh), F32), pltpu.VMEM((H, 8, dh), F32)],
        compiler_params=_cparams(("arbitrary",)),
    )(qks, qks, z, gcol, grow, bif_row, bif_col, hm, dhm, cst, nst, dz)


def _ml_post_fwd(hm, z, g_head, D):
    S = hm.shape[0]
    T = min(TOK_TILE, S)
    dh = D // N_HEADS

    def body(hm_ref, o_ref, zm_ref, g_ref, pm_ref):
        for h in range(N_HEADS):
            hs = pl.ds(h * dh, dh)
            hg = _sig(o_ref[:, hs].astype(F32)) * hm_ref[:, hs]
            rs = lax.rsqrt(jnp.mean(hg * hg, axis=-1, keepdims=True) + EPS)
            pm_ref[:, hs] = (hg * rs * g_ref[:, hs] * _silu(zm_ref[:, hs].astype(F32))).astype(BF16)

    return pl.pallas_call(
        body, name="ml_post_fwd", grid=(S // T,),
        in_specs=[pl.BlockSpec((T, D), lambda i: (i, 0)), pl.BlockSpec((T, D), lambda i: (i, 6)),
                  pl.BlockSpec((T, D), lambda i: (i, 7)), pl.BlockSpec((1, D), lambda i: (0, 0))],
        out_specs=pl.BlockSpec((T, D), lambda i: (i, 0)),
        out_shape=jax.ShapeDtypeStruct((S, D), BF16),
        compiler_params=_cparams(("parallel",)),
    )(hm, z, z, g_head)


def _ml_post_bwd(dpm, hm, z, g_head, dz, D):
    S = hm.shape[0]
    T = min(TOK_TILE, S)
    dh = D // N_HEADS

    def body(dpm_ref, hm_ref, o_ref, zm_ref, g_ref, dz_any, dhm_ref, dg_ref, dozm_ref):
        del dz_any
        do_ref = dozm_ref.at[:, pl.ds(0, D)]
        dzm_ref = dozm_ref.at[:, pl.ds(D, D)]
        i = pl.program_id(0)

        @pl.when(i == 0)
        def _():
            dg_ref[...] = jnp.zeros_like(dg_ref)

        for h in range(N_HEADS):
            hs = pl.ds(h * dh, dh)
            o = o_ref[:, hs].astype(F32)
            zm = zm_ref[:, hs].astype(F32)
            hmv = hm_ref[:, hs]
            dpm_ = dpm_ref[:, hs]
            g = g_ref[:, hs]
            so = _sig(o)
            hg = so * hmv
            rs = lax.rsqrt(jnp.mean(hg * hg, axis=-1, keepdims=True) + EPS)
            xn = hg * rs
            dzm_ref[:, hs] = (dpm_ * xn * g * _dsilu(zm)).astype(BF16)
            dhn = dpm_ * _silu(zm)
            dg_ref[0:1, hs] += jnp.sum(dhn * xn, axis=0, keepdims=True)
            dxn = dhn * g
            dhg = rs * (dxn - xn * jnp.mean(dxn * xn, axis=-1, keepdims=True))
            do_ref[:, hs] = (dhg * hmv * so * (1.0 - so)).astype(BF16)
            dhm_ref[:, hs] = dhg * so

    return pl.pallas_call(
        body, name="ml_post_bwd", grid=(S // T,),
        in_specs=[pl.BlockSpec((T, D), lambda i: (i, 0)), pl.BlockSpec((T, D), lambda i: (i, 0)),
                  pl.BlockSpec((T, D), lambda i: (i, 6)), pl.BlockSpec((T, D), lambda i: (i, 7)),
                  pl.BlockSpec((1, D), lambda i: (0, 0)), pl.BlockSpec(memory_space=pl.ANY)],
        out_specs=[pl.BlockSpec((T, D), lambda i: (i, 0)), pl.BlockSpec((8, D), lambda i: (0, 0)),
                   pl.BlockSpec((T, 2 * D), lambda i: (i, 3))],
        out_shape=[jax.ShapeDtypeStruct((S, D), F32), jax.ShapeDtypeStruct((8, D), F32),
                   jax.ShapeDtypeStruct(dz.shape, dz.dtype)],
        input_output_aliases={5: 2},
        compiler_params=_cparams(("arbitrary",)),
    )(dpm, hm, z, z, g_head, dz)


def _xattn_probs(q, km_h, scale):
    s = _dot_nt(q, km_h) * scale
    e = jnp.exp(s - jnp.max(s, axis=-1, keepdims=True))
    return e / jnp.sum(e, axis=-1, keepdims=True)


def _xattn_fwd(z, kv, D):
    S = z.shape[0]
    M = kv.shape[0]
    T = min(TOK_TILE, S)
    dh = D // N_HEADS
    scale = dh ** -0.5

    def body(q_ref, zx_ref, km_ref, vm_ref, px_ref):
        for h in range(N_HEADS):
            hs = pl.ds(h * dh, dh)
            p = _xattn_probs(q_ref[:, hs], km_ref[:, hs], scale)
            o = _dot_nn(p.astype(BF16), vm_ref[:, hs])
            px_ref[:, hs] = (o * _silu(zx_ref[:, hs].astype(F32))).astype(BF16)

    return pl.pallas_call(
        body, name="xattn_fwd", grid=(S // T,),
        in_specs=[pl.BlockSpec((T, D), lambda i: (i, 8)), pl.BlockSpec((T, D), lambda i: (i, 9)),
                  pl.BlockSpec((M, D), lambda i: (0, 0)), pl.BlockSpec((M, D), lambda i: (0, 1))],
        out_specs=pl.BlockSpec((T, D), lambda i: (i, 0)),
        out_shape=jax.ShapeDtypeStruct((S, D), BF16),
        compiler_params=_cparams(("parallel",)),
    )(z, z, kv, kv)


def _xattn_bwd(dpx, z, kv, dz, D):
    S = z.shape[0]
    M = kv.shape[0]
    T = min(TOK_TILE, S)
    dh = D // N_HEADS
    scale = dh ** -0.5

    def body(dpx_ref, q_ref, zx_ref, km_ref, vm_ref, dz_any, dkv_ref, dqz_ref):
        del dz_any
        i = pl.program_id(0)

        @pl.when(i == 0)
        def _():
            dkv_ref[...] = jnp.zeros_like(dkv_ref)

        for h in range(N_HEADS):
            hs = pl.ds(h * dh, dh)
            q = q_ref[:, hs]
            zx = zx_ref[:, hs].astype(F32)
            dpx_ = dpx_ref[:, hs]
            p = _xattn_probs(q, km_ref[:, hs], scale)
            p_b = p.astype(BF16)
            o = _dot_nn(p_b, vm_ref[:, hs])
            dqz_ref[:, pl.ds(D + h * dh, dh)] = (dpx_ * o * _dsilu(zx)).astype(BF16)
            do = (dpx_ * _silu(zx)).astype(BF16)
            dp = _dot_nt(do, vm_ref[:, hs])
            dsc = (p * (dp - jnp.sum(p * dp, axis=-1, keepdims=True)) * scale).astype(BF16)
            dqz_ref[:, hs] = _dot_nn(dsc, km_ref[:, hs]).astype(BF16)
            dkv_ref[:, hs] += _dot_tn(dsc, q)
            dkv_ref[:, pl.ds(D + h * dh, dh)] += _dot_tn(p_b, do)

    return pl.pallas_call(
        body, name="xattn_bwd", grid=(S // T,),
        in_specs=[pl.BlockSpec((T, D), lambda i: (i, 0)),
                  pl.BlockSpec((T, D), lambda i: (i, 8)), pl.BlockSpec((T, D), lambda i: (i, 9)),
                  pl.BlockSpec((M, D), lambda i: (0, 0)), pl.BlockSpec((M, D), lambda i: (0, 1)),
                  pl.BlockSpec(memory_space=pl.ANY)],
        out_specs=[pl.BlockSpec((M, 2 * D), lambda i: (0, 0)), pl.BlockSpec((T, 2 * D), lambda i: (i, 4))],
        out_shape=[jax.ShapeDtypeStruct((M, 2 * D), F32), jax.ShapeDtypeStruct(dz.shape, dz.dtype)],
        input_output_aliases={5: 1},
        compiler_params=_cparams(("arbitrary",)),
    )(dpx, z, z, kv, kv, dz)


def _gain_grad(dn, xin, rstd):
    R, D = dn.shape

    def body(dn_ref, x_ref, r_ref, o_ref):
        o_ref[...] = jnp.zeros_like(o_ref)
        o_ref[0:1, :] = jnp.sum(dn_ref[...] * x_ref[...] * r_ref[...], axis=0, keepdims=True)

    return pl.pallas_call(
        body, name="gain_grad", out_shape=jax.ShapeDtypeStruct((8, D), F32),
        compiler_params=_cparams(),
    )(dn, xin, rstd)


def _merge_fwd(z, yc, ym, yx, D):
    S = z.shape[0]
    T = min(TOK_TILE, S)

    def body(g0, g1, g2, yc_ref, ym_ref, yx_ref, o_ref):
        o_ref[...] = (_sig(g0[...].astype(F32)) * yc_ref[...] + _sig(g1[...].astype(F32)) * ym_ref[...]
                      + _sig(g2[...].astype(F32)) * yx_ref[...]).astype(BF16)

    tok = pl.BlockSpec((T, D), lambda i: (i, 0))
    return pl.pallas_call(
        body, name="merge_fwd", grid=(S // T,),
        in_specs=[pl.BlockSpec((T, D), lambda i: (i, 10)), pl.BlockSpec((T, D), lambda i: (i, 11)),
                  pl.BlockSpec((T, D), lambda i: (i, 12)), tok, tok, tok],
        out_specs=tok, out_shape=jax.ShapeDtypeStruct((S, D), BF16),
        compiler_params=_cparams(("parallel",)),
    )(z, z, z, yc, ym, yx)


def _merge_bwd(dmg, z, yc, ym, yx, D):
    S = z.shape[0]
    T = min(TOK_TILE, S)

    def body(dm_ref, g_ref, yc_ref, ym_ref, yx_ref, dyc_ref, dym_ref, dyx_ref, dg_ref):
        j = pl.program_id(1)
        for jj, (y_ref, dy_ref) in enumerate(((yc_ref, dyc_ref), (ym_ref, dym_ref), (yx_ref, dyx_ref))):
            @pl.when(j == jj)
            def _():
                sg = _sig(g_ref[...].astype(F32))
                dm = dm_ref[...]
                dy_ref[...] = (dm * sg).astype(BF16)
                dg_ref[...] = (dm * y_ref[...] * sg * (1.0 - sg)).astype(BF16)

    tok = pl.BlockSpec((T, D), lambda i, j: (i, 0))
    return pl.pallas_call(
        body, name="merge_bwd", grid=(S // T, 3),
        in_specs=[tok, pl.BlockSpec((T, D), lambda i, j: (i, 10 + j)), tok, tok, tok],
        out_specs=[tok, tok, tok, pl.BlockSpec((T, D), lambda i, j: (i, 10 + j))],
        out_shape=[jax.ShapeDtypeStruct((S, D), BF16)] * 3 + [jax.ShapeDtypeStruct((S, 13 * D), BF16)],
        compiler_params=_cparams(("arbitrary", "arbitrary")),
    )(dmg, z, yc, ym, yx)


def _post_loss(r, x, tgt, g_post):
    S, D = r.shape
    T = min(TOK_TILE, S)

    def body(r_ref, x_ref, t_ref, g_ref, dy_ref, dr_ref, dg_ref, loss_ref):
        i = pl.program_id(0)

        @pl.when(i == 0)
        def _():
            dg_ref[...] = jnp.zeros_like(dg_ref)
            loss_ref[...] = jnp.zeros_like(loss_ref)

        rv = r_ref[...]
        g = g_ref[...]
        rs = lax.rsqrt(jnp.mean(rv * rv, axis=-1, keepdims=True) + EPS)
        nrm = rv * rs
        e = x_ref[...] + nrm * g - t_ref[...]
        part = jnp.sum(jnp.sum(e * e, axis=-1, keepdims=True), axis=0, keepdims=True) * (0.5 / D)
        loss_ref[0:1, 0:1] += part
        dy = e * (1.0 / D)
        dy_ref[...] = dy
        dg_ref[0:1, :] += jnp.sum(dy * nrm, axis=0, keepdims=True)
        dn = dy * g
        dr_ref[...] = (rs * (dn - nrm * jnp.mean(dn * nrm, axis=-1, keepdims=True))).astype(BF16)

    tok = pl.BlockSpec((T, D), lambda i: (i, 0))
    return pl.pallas_call(
        body, name="post_loss", grid=(S // T,),
        in_specs=[tok, tok, tok, pl.BlockSpec((1, D), lambda i: (0, 0))],
        out_specs=[tok, tok, pl.BlockSpec((8, D), lambda i: (0, 0)), pl.BlockSpec((8, 128), lambda i: (0, 0))],
        out_shape=[jax.ShapeDtypeStruct((S, D), F32), jax.ShapeDtypeStruct((S, D), BF16),
                   jax.ShapeDtypeStruct((8, D), F32), jax.ShapeDtypeStruct((8, 128), F32)],
        compiler_params=_cparams(("arbitrary",)),
    )(r, x, tgt, g_post)


def _prenorm_bwd(dh, x, rstd, dy, g):
    S, D = x.shape
    T = min(TOK_TILE, S)

    def body(dh_ref, x_ref, r_ref, dy_ref, g_ref, dx_ref, dg_ref):
        i = pl.program_id(0)

        @pl.when(i == 0)
        def _():
            dg_ref[...] = jnp.zeros_like(dg_ref)

        rs = r_ref[...]
        xn = x_ref[...] * rs
        dhv = dh_ref[...]
        dg_ref[0:1, :] += jnp.sum(dhv * xn, axis=0, keepdims=True)
        dxn = dhv * g_ref[...]
        dx_ref[...] = dy_ref[...] + rs * (dxn - xn * jnp.mean(dxn * xn, axis=-1, keepdims=True))

    tok = pl.BlockSpec((T, D), lambda i: (i, 0))
    return pl.pallas_call(
        body, name="prenorm_bwd", grid=(S // T,),
        in_specs=[tok, tok, pl.BlockSpec((T, 1), lambda i: (i, 0)), tok, pl.BlockSpec((1, D), lambda i: (0, 0))],
        out_specs=[tok, pl.BlockSpec((8, D), lambda i: (0, 0))],
        out_shape=[jax.ShapeDtypeStruct((S, D), F32), jax.ShapeDtypeStruct((8, D), F32)],
        compiler_params=_cparams(("arbitrary",)),
    )(dh, x, rstd, dy, g)


def _conv_bwd1(dpc, z, u1, g_ln, b_ln, dz, D):
    S = z.shape[0]
    T = min(TOK_TILE, S)

    def body(dpc_ref, zc_ref, u1_ref, gln_ref, bln_ref, dz_any, du1_ref, st_ref, dzc_ref):
        del dz_any
        i = pl.program_id(0)

        @pl.when(i == 0)
        def _():
            st_ref[...] = jnp.zeros_like(st_ref)

        u1v = u1_ref[...]
        mu = jnp.mean(u1v, axis=-1, keepdims=True)
        xc = u1v - mu
        rs = lax.rsqrt(jnp.mean(xc * xc, axis=-1, keepdims=True) + EPS)
        xh = xc * rs
        g = gln_ref[...]
        ln = xh * g + bln_ref[...]
        zc = zc_ref[...].astype(F32)
        dpc_ = dpc_ref[...]
        dzc_ref[...] = (dpc_ * _silu(ln) * _dsilu(zc)).astype(BF16)
        dln = dpc_ * _silu(zc) * _dsilu(ln)
        st_ref[0:1, :] += jnp.sum(dln * xh, axis=0, keepdims=True)
        st_ref[1:2, :] += jnp.sum(dln, axis=0, keepdims=True)
        dxh = dln * g
        du1 = rs * (dxh - jnp.mean(dxh, axis=-1, keepdims=True) - xh * jnp.mean(dxh * xh, axis=-1, keepdims=True))
        du1_ref[...] = du1
        st_ref[2:3, :] += jnp.sum(du1, axis=0, keepdims=True)

    tok = pl.BlockSpec((T, D), lambda i: (i, 0))
    vec = pl.BlockSpec((1, D), lambda i: (0, 0))
    return pl.pallas_call(
        body, name="conv_bwd1", grid=(S // T,),
        in_specs=[tok, pl.BlockSpec((T, D), lambda i: (i, 2)), tok, vec, vec, pl.BlockSpec(memory_space=pl.ANY)],
        out_specs=[tok, pl.BlockSpec((8, D), lambda i: (0, 0)), pl.BlockSpec((T, D), lambda i: (i, 2))],
        out_shape=[jax.ShapeDtypeStruct((S, D), F32), jax.ShapeDtypeStruct((8, D), F32),
                   jax.ShapeDtypeStruct(dz.shape, dz.dtype)],
        input_output_aliases={5: 2},
        compiler_params=_cparams(("arbitrary",)),
    )(dpc, z, u1, g_ln, b_ln, dz)


def _conv_bwd2(du1, z, w_dw, dz, D):
    S = z.shape[0]
    T = min(TOK_TILE, S)
    hb = T // HALO
    last = S // HALO - 1

    def body(du_ref, dun_ref, a_ref, b_ref, ah_ref, bh_ref, w_ref, dz_any, dw_ref, dab_ref, dus, u0s):
        del dz_any
        i = pl.program_id(0)
        n = pl.num_programs(0)

        @pl.when(i == 0)
        def _():
            dw_ref[...] = jnp.zeros_like(dw_ref)

        du1 = du_ref[...]
        dus[pl.ds(0, T), :] = du1
        dus[pl.ds(T, HALO), :] = jnp.where(i < n - 1, dun_ref[...], 0.0)
        a = a_ref[...].astype(F32)
        sb = _sig(b_ref[...].astype(F32))
        u0s[pl.ds(HALO, T), :] = a * sb
        halo = ah_ref[...].astype(F32) * _sig(bh_ref[...].astype(F32))
        u0s[pl.ds(0, HALO), :] = jnp.where(i > 0, halo, 0.0)
        du0 = jnp.zeros((T, D), F32)
        for j in range(CONV_W):
            du0 = du0 + w_ref[pl.ds(j, 1), :] * dus[pl.ds(CONV_W - 1 - j, T), :]
            dw_ref[pl.ds(j, 1), :] += jnp.sum(du1 * u0s[pl.ds(HALO - (CONV_W - 1) + j, T), :], axis=0, keepdims=True)
        dab_ref[:, pl.ds(0, D)] = (du0 * sb).astype(BF16)
        dab_ref[:, pl.ds(D, D)] = (du0 * a * sb * (1.0 - sb)).astype(BF16)

    tok = pl.BlockSpec((T, D), lambda i: (i, 0))
    nxt = lambda i: (jnp.minimum((i + 1) * hb, last), 0)
    prev = lambda i: (jnp.maximum(i * hb - 1, 0), 0)
    prev_b = lambda i: (jnp.maximum(i * hb - 1, 0), 1)
    return pl.pallas_call(
        body, name="conv_bwd2", grid=(S // T,),
        in_specs=[tok, pl.BlockSpec((HALO, D), nxt), tok, pl.BlockSpec((T, D), lambda i: (i, 1)),
                  pl.BlockSpec((HALO, D), prev), pl.BlockSpec((HALO, D), prev_b),
                  pl.BlockSpec((32, D), lambda i: (0, 0)), pl.BlockSpec(memory_space=pl.ANY)],
        out_specs=[pl.BlockSpec((32, D), lambda i: (0, 0)), pl.BlockSpec((T, 2 * D), lambda i: (i, 0))],
        out_shape=[jax.ShapeDtypeStruct((32, D), F32), jax.ShapeDtypeStruct(dz.shape, dz.dtype)],
        input_output_aliases={7: 1},
        scratch_shapes=[pltpu.VMEM((T + HALO, D), F32), pltpu.VMEM((T + HALO, D), F32)],
        compiler_params=_cparams(("arbitrary",)),
    )(du1, du1, z, z, z, z, w_dw, dz)


def _qk_bwd1(dqks, z, w_qk, D):
    S = z.shape[0]
    T = min(TOK_TILE, S)
    hb = T // HALO
    scale = (D // N_HEADS) ** -0.5
    W2 = 2 * D

    def body(d_ref, p0_ref, p1_ref, h0_ref, h1_ref, w_ref, dc_ref, dw_ref, ps):
        i = pl.program_id(0)

        @pl.when(i == 0)
        def _():
            dw_ref[...] = jnp.zeros_like(dw_ref)

        ps[pl.ds(HALO, T), pl.ds(0, D)] = p0_ref[...].astype(F32)
        ps[pl.ds(HALO, T), pl.ds(D, D)] = p1_ref[...].astype(F32)
        ps[pl.ds(0, HALO), pl.ds(0, D)] = jnp.where(i > 0, h0_ref[...].astype(F32), 0.0)
        ps[pl.ds(0, HALO), pl.ds(D, D)] = jnp.where(i > 0, h1_ref[...].astype(F32), 0.0)
        acc = jnp.zeros((T, W2), F32)
        for j in range(QK_W):
            acc = acc + w_ref[pl.ds(j, 1), :] * ps[pl.ds(HALO - (QK_W - 1) + j, T), :]
        col = lax.broadcasted_iota(jnp.int32, (1, W2), 1)
        dc = d_ref[...].astype(F32) * _dsilu(acc) * jnp.where(col < D, 1.0, scale)
        dc_ref[...] = dc.astype(BF16)
        for j in range(QK_W):
            dw_ref[pl.ds(j, 1), :] += jnp.sum(dc * ps[pl.ds(HALO - (QK_W - 1) + j, T), :], axis=0, keepdims=True)

    prev0 = lambda i: (jnp.maximum(i * hb - 1, 0), 3)
    prev1 = lambda i: (jnp.maximum(i * hb - 1, 0), 4)
    return pl.pallas_call(
        body, name="qk_bwd1", grid=(S // T,),
        in_specs=[pl.BlockSpec((T, W2), lambda i: (i, 0)),
                  pl.BlockSpec((T, D), lambda i: (i, 3)), pl.BlockSpec((T, D), lambda i: (i, 4)),
                  pl.BlockSpec((HALO, D), prev0), pl.BlockSpec((HALO, D), prev1),
                  pl.BlockSpec((8, W2), lambda i: (0, 0))],
        out_specs=[pl.BlockSpec((T, W2), lambda i: (i, 0)), pl.BlockSpec((8, W2), lambda i: (0, 0))],
        out_shape=[jax.ShapeDtypeStruct((S, W2), BF16), jax.ShapeDtypeStruct((8, W2), F32)],
        scratch_shapes=[pltpu.VMEM((T + HALO, W2), F32)],
        compiler_params=_cparams(("arbitrary",)),
    )(dqks, z, z, z, z, w_qk)


def _qk_bwd2(dc, w_qk, dz, D):
    S = dc.shape[0]
    T = min(TOK_TILE, S)
    hb = T // HALO
    last = S // HALO - 1

    def body(d_ref, dn_ref, w_ref, dz_any, o_ref, ds):
        del dz_any
        i = pl.program_id(0)
        n = pl.num_programs(0)
        ds[pl.ds(0, T), :] = d_ref[...].astype(F32)
        ds[pl.ds(T, HALO), :] = jnp.where(i < n - 1, dn_ref[...].astype(F32), 0.0)
        acc = jnp.zeros((T, D), F32)
        for j in range(QK_W):
            acc = acc + w_ref[pl.ds(j, 1), :] * ds[pl.ds(QK_W - 1 - j, T), :]
        o_ref[...] = acc.astype(BF16)

    return pl.pallas_call(
        body, name="qk_bwd2", grid=(S // T, 2),
        in_specs=[pl.BlockSpec((T, D), lambda i, j: (i, j)),
                  pl.BlockSpec((HALO, D), lambda i, j: (jnp.minimum((i + 1) * hb, last), j)),
                  pl.BlockSpec((8, D), lambda i, j: (0, j)), pl.BlockSpec(memory_space=pl.ANY)],
        out_specs=pl.BlockSpec((T, D), lambda i, j: (i, 3 + j)),
        out_shape=jax.ShapeDtypeStruct(dz.shape, dz.dtype),
        input_output_aliases={3: 0},
        scratch_shapes=[pltpu.VMEM((T + HALO, D), F32)],
        compiler_params=_cparams(("arbitrary", "arbitrary")),
    )(dc, dc, w_qk, dz)


def _row_tile(R, row_bytes, budget=3 * 1024 * 1024):
    cands = [t for t in range(16, R + 1, 16) if R % t == 0 and t * row_bytes <= budget]
    return max(cands) if cands else R


def _adamw(parts, w, m, v, name):
    R, C = w.shape
    tr = _row_tile(R, C * (4 * 7 + 8 * parts.dtype.itemsize))
    c1 = 1.0 / (1.0 - ADAM_B1 ** ADAM_STEP)
    c2 = 1.0 / (1.0 - ADAM_B2 ** ADAM_STEP)

    def body(p_ref, w_ref, m_ref, v_ref, g_ref, d_ref, nm_ref, nv_ref):
        g = p_ref[0].astype(F32)
        for s in range(1, N_DEV):
            g = g + p_ref[s].astype(F32)
        g_ref[...] = g
        mn = ADAM_B1 * m_ref[...] + (1.0 - ADAM_B1) * g
        vn = ADAM_B2 * v_ref[...] + (1.0 - ADAM_B2) * (g * g)
        nm_ref[...] = mn
        nv_ref[...] = vn
        d_ref[...] = -ADAM_LR * ((mn * c1) / (jnp.sqrt(vn * c2) + ADAM_EPS) + ADAM_WD * w_ref[...])

    blk = pl.BlockSpec((tr, C), lambda i: (i, 0))
    return pl.pallas_call(
        body, name=name, grid=(R // tr,),
        in_specs=[pl.BlockSpec((N_DEV, tr, C), lambda i: (0, i, 0)), blk, blk, blk],
        out_specs=[blk, blk, blk, blk],
        out_shape=[jax.ShapeDtypeStruct((R, C), F32)] * 4,
        compiler_params=_cparams(("parallel",)),
    )(parts, w, m, v)


def _exchange(operands, scatter, name):
    n = len(operands)
    outs = []
    for a in operands:
        shp = a.shape[1:] if scatter else a.shape
        outs.append(jax.ShapeDtypeStruct((N_DEV,) + tuple(shp), a.dtype))

    def body(*refs):
        srcs, dsts = refs[:n], refs[n:2 * n]
        send_sems, recv_sems, loc_sems = refs[2 * n:]
        x, y, c = lax.axis_index("x"), lax.axis_index("y"), lax.axis_index("c")
        me = 4 * x + 2 * y + c
        copies = []
        for t in range(n):
            loc = pltpu.make_async_copy(srcs[t].at[me] if scatter else srcs[t], dsts[t].at[me], loc_sems.at[t])
            loc.start()
            copies.append(loc)
            for k in range(1, N_DEV):
                px = 1 - x if (k >> 2) & 1 else x
                py = 1 - y if (k >> 1) & 1 else y
                pc = 1 - c if k & 1 else c
                peer = 4 * px + 2 * py + pc
                cp = pltpu.make_async_remote_copy(
                    src_ref=srcs[t].at[peer] if scatter else srcs[t], dst_ref=dsts[t].at[me],
                    send_sem=send_sems.at[t * (N_DEV - 1) + k - 1], recv_sem=recv_sems.at[t * (N_DEV - 1) + k - 1],
                    device_id=(px, py, pc), device_id_type=pl.DeviceIdType.MESH)
                cp.start()
                copies.append(cp)
        for cp in copies:
            cp.wait()

    hbm = pl.BlockSpec(memory_space=pltpu.HBM)
    return pl.pallas_call(
        body, name=name, in_specs=[hbm] * n, out_specs=[hbm] * n, out_shape=outs,
        scratch_shapes=[pltpu.SemaphoreType.DMA((n * (N_DEV - 1),)), pltpu.SemaphoreType.DMA((n * (N_DEV - 1),)),
                        pltpu.SemaphoreType.DMA((n,))],
        compiler_params=pltpu.CompilerParams(has_side_effects=True),
    )(*operands)


def _local_step(x, mem, tgt, g_pre, wp, wif, wifT, b_if, w_qk, w_dw, b_dw, g_ln, b_ln, w_co, g_head, w_mo,
                g_mem, w_kv, w_xo, w_out, g_post):
    S, D = x.shape
    L = min(ML_CHUNK, S)
    NC = S // L
    bif_row = jnp.zeros((1, IF_PAD), F32).at[0, :N_IF].set(b_if)
    bif_col = jnp.zeros((16, 1), F32).at[:N_IF, 0].set(b_if)

    h, rstd = _rmsnorm_fwd(x, g_pre, "prenorm_fwd")
    z = _mm(h, wp, mode="nn", out_dtype=BF16, tm=1024, tn=1024, tk=2048, name="in_proj", n_outer=True)
    gcol = _mm(h, wif, mode="nn", out_dtype=F32, tm=1024, tn=IF_PAD, tk=2048, name="gates_col")
    grow = _mm(wifT, h, mode="nt", out_dtype=F32, tm=16, tn=1024, tk=2048, name="gates_row")
    grow = grow.reshape(16, NC, L).transpose(1, 0, 2)

    u1, pc = _conv_fwd(z, w_dw, b_dw, g_ln, b_ln, D)
    qks = _qk_fwd(z, w_qk, D)
    hm, cst, nst = _mlstm_fwd(qks, z, gcol, grow, bif_row, bif_col, D)
    pm = _ml_post_fwd(hm, z, g_head, D)
    memn, rstd_mem = _rmsnorm_fwd(mem, g_mem, "memnorm_fwd")
    kv = _mm(memn, w_kv, mode="nn", out_dtype=BF16, tm=256, tn=1024, tk=2048, name="mem_kv")
    px = _xattn_fwd(z, kv, D)

    proj = functools.partial(_mm, mode="nn", out_dtype=F32, tm=1024, tn=1024, tk=2048)
    yc = proj(pc, w_co, name="conv_out")
    ym = proj(pm, w_mo, name="ml_out")
    yx = proj(px, w_xo, name="xa_out")
    mg = _merge_fwd(z, yc, ym, yx, D)
    r = proj(mg, w_out, name="out_proj")
    dy, dr, dg_post, loss = _post_loss(r, x, tgt, g_post)

    dgrad = functools.partial(_mm, mode="nt", out_dtype=F32, tm=1024, tn=1024, tk=2048)
    wgrad = functools.partial(_mm, mode="tn", out_dtype=F32, tm=1024, tn=1024, tk=1024)
    dmg = dgrad(dr, w_out, name="out_proj_dx")
    dw_out = wgrad(mg, dr, name="out_proj_dw")
    dyc, dym, dyx, dz = _merge_bwd(dmg, z, yc, ym, yx, D)
    dpc = dgrad(dyc, w_co, name="conv_out_dx")
    dw_co = wgrad(pc, dyc, name="conv_out_dw")
    dpm = dgrad(dym, w_mo, name="ml_out_dx")
    dw_mo = wgrad(pm, dym, name="ml_out_dw")
    dpx = dgrad(dyx, w_xo, name="xa_out_dx")
    dw_xo = wgrad(px, dyx, name="xa_out_dw")

    dkv, dz = _xattn_bwd(dpx, z, kv, dz, D)
    dkv_b = dkv.astype(BF16)
    dw_kv = wgrad(memn, dkv_b, name="mem_kv_dw")
    dmemn = _mm(dkv_b, w_kv, mode="nt", out_dtype=F32, tm=256, tn=1024, tk=2048, name="mem_kv_dx")
    dg_mem = _gain_grad(dmemn, mem, rstd_mem)

    du1, cstats, dz = _conv_bwd1(dpc, z, u1, g_ln, b_ln, dz, D)
    dw_dw, dz = _conv_bwd2(du1, z, w_dw, dz, D)

    dhm, dg_head, dz = _ml_post_bwd(dpm, hm, z, g_head, dz, D)
    dqks, dgates, dz = _mlstm_bwd(qks, z, gcol, grow, bif_row, bif_col, hm, dhm, cst, nst, dz, D)
    dc, dw_qk = _qk_bwd1(dqks, z, w_qk, D)
    dz = _qk_bwd2(dc, w_qk, dz, D)

    dgates_b = dgates.astype(BF16)
    dh = _mm(dz, wp, mode="nt", out_dtype=F32, tm=1024, tn=2048, tk=1024, name="in_proj_dx")
    dh = _mm(dgates_b, wif, mode="nt", out_dtype=F32, tm=1024, tn=1024, tk=IF_PAD, name="gates_dx", acc_in=dh)
    dwp = _mm(h, dz, mode="tn", out_dtype=F32, tm=2048, tn=1024, tk=512, name="in_proj_dw", n_outer=True)
    dwif = _mm(h, dgates_b, mode="tn", out_dtype=F32, tm=2048, tn=IF_PAD, tk=512, name="gates_dw")
    dx, dg_pre = _prenorm_bwd(dh, x, rstd, dy, g_pre)

    dw_in = jnp.concatenate([dwp[:, :8 * D], dwif[:, :N_IF], dwp[:, 8 * D:]], axis=1)
    db_if = jnp.sum(dgates[:, :N_IF], axis=0)
    grads = dict(g_pre=dg_pre[0], w_in=dw_in, b_if=db_if, w_qk_conv=dw_qk[:QK_W], w_dw=dw_dw[:CONV_W],
                 b_dw=cstats[2], g_ln=cstats[0], b_ln=cstats[1], w_conv_out=dw_co, g_ml_head=dg_head[0],
                 w_ml_out=dw_mo, g_mem=dg_mem[0], w_mem_kv=dw_kv, w_xa_out=dw_xo, w_out=dw_out,
                 g_post=dg_post[0])
    return loss[0, 0], dx, grads


WEIGHTS = ('g_pre', 'w_in', 'b_if', 'w_qk_conv', 'w_dw', 'b_dw', 'g_ln', 'b_ln', 'w_conv_out', 'g_ml_head',
           'w_ml_out', 'g_mem', 'w_mem_kv', 'w_xa_out', 'w_out', 'g_post')
VECTORS = ('g_pre', 'b_dw', 'g_ln', 'b_ln', 'g_ml_head', 'g_mem', 'g_post')
ROW_SHARDED = ('w_conv_out', 'w_ml_out', 'w_xa_out', 'w_out')
COL_SHARDED = ('w_in', 'w_mem_kv', 'w_qk_conv', 'w_dw')


def _cols_to_slabs(g):
    R, C8 = g.shape
    return g.reshape(R, N_DEV, C8 // N_DEV).transpose(1, 0, 2)


def _slabs_to_cols(a):
    n, R, C = a.shape
    return a.transpose(1, 0, 2).reshape(R, n * C)


def kernel(x, mem, g_pre, w_in, b_if, w_qk_conv, w_dw, b_dw, g_ln, b_ln, w_conv_out, g_ml_head, w_ml_out, g_mem, w_mem_kv, w_xa_out, w_out, g_post, loss_target, m_g_pre, m_w_in, m_b_if, m_w_qk_conv, m_w_dw, m_b_dw, m_g_ln, m_b_ln, m_w_conv_out, m_g_ml_head, m_w_ml_out, m_g_mem, m_w_mem_kv, m_w_xa_out, m_w_out, m_g_post, v_g_pre, v_w_in, v_b_if, v_w_qk_conv, v_w_dw, v_b_dw, v_g_ln, v_b_ln, v_w_conv_out, v_g_ml_head, v_w_ml_out, v_g_mem, v_w_mem_kv, v_w_xa_out, v_w_out, v_g_post):
    S, D = x.shape[1], x.shape[2]
    w = dict(g_pre=g_pre, w_in=w_in, b_if=b_if, w_qk_conv=w_qk_conv, w_dw=w_dw, b_dw=b_dw, g_ln=g_ln, b_ln=b_ln,
             w_conv_out=w_conv_out, g_ml_head=g_ml_head, w_ml_out=w_ml_out, g_mem=g_mem, w_mem_kv=w_mem_kv,
             w_xa_out=w_xa_out, w_out=w_out, g_post=g_post)
    mom = dict(g_pre=m_g_pre, w_in=m_w_in, b_if=m_b_if, w_qk_conv=m_w_qk_conv, w_dw=m_w_dw, b_dw=m_b_dw,
               g_ln=m_g_ln, b_ln=m_b_ln, w_conv_out=m_w_conv_out, g_ml_head=m_g_ml_head, w_ml_out=m_w_ml_out,
               g_mem=m_g_mem, w_mem_kv=m_w_mem_kv, w_xa_out=m_w_xa_out, w_out=m_w_out, g_post=m_g_post)
    var = dict(g_pre=v_g_pre, w_in=v_w_in, b_if=v_b_if, w_qk_conv=v_w_qk_conv, w_dw=v_w_dw, b_dw=v_b_dw,
               g_ln=v_g_ln, b_ln=v_b_ln, w_conv_out=v_w_conv_out, g_ml_head=v_g_ml_head, w_ml_out=v_w_ml_out,
               g_mem=v_g_mem, w_mem_kv=v_w_mem_kv, w_xa_out=v_w_xa_out, w_out=v_w_out, g_post=v_g_post)

    big = ('w_in', 'w_conv_out', 'w_ml_out', 'w_xa_out', 'w_out', 'w_mem_kv')
    small = ('w_qk_conv', 'w_dw')
    gathered = _exchange([w[n].astype(BF16) for n in big] + [w[n] for n in small], False, "gather_weights")
    full = dict(zip(big + small, gathered))
    for n in COL_SHARDED:
        full[n] = _slabs_to_cols(full[n])
    for n in ROW_SHARDED:
        full[n] = full[n].reshape(D, D)
    w_in_f = full['w_in']
    wp = jnp.concatenate([w_in_f[:, :8 * D], w_in_f[:, 8 * D + N_IF:]], axis=1)
    wif = jnp.zeros((D, IF_PAD), BF16).at[:, :N_IF].set(w_in_f[:, 8 * D:8 * D + N_IF])
    wifT = jnp.zeros((16, D), BF16).at[:N_IF, :].set(w_in_f[:, 8 * D:8 * D + N_IF].T)
    w_qk_f = jnp.zeros((8, 2 * D), F32).at[:QK_W].set(full['w_qk_conv'])
    w_dw_f = jnp.zeros((32, D), F32).at[:CONV_W].set(full['w_dw'])
    row = lambda a: a.reshape(1, D)

    loss, dx, g = _local_step(
        x[0], mem[0], loss_target[0], row(g_pre), wp, wif, wifT, b_if, w_qk_f, w_dw_f, row(b_dw), row(g_ln),
        row(b_ln), full['w_conv_out'], row(g_ml_head), full['w_ml_out'], row(g_mem), full['w_mem_kv'],
        full['w_xa_out'], full['w_out'], row(g_post))

    send = []
    for n in big:
        if n in COL_SHARDED:
            send.append(_cols_to_slabs(g[n]).astype(BF16))
        else:
            send.append(g[n].reshape(N_DEV, D // N_DEV, D).astype(BF16))
    for n in small:
        send.append(_cols_to_slabs(g[n]))
    vec = jnp.stack([g[n] for n in VECTORS] + [jnp.zeros((D,), F32).at[:N_IF].set(g['b_if'])])
    send.append(jnp.broadcast_to(vec[None], (N_DEV,) + vec.shape))
    recv = _exchange(send, True, "scatter_grads")
    parts = dict(zip(big + small + ('vec',), recv))

    out = {}
    for n in big + small:
        out[n] = _adamw(parts[n], w[n], mom[n], var[n], "adamw_" + n)
    pad = lambda a: jnp.zeros((D,), F32).at[:N_IF].set(a)
    stack = lambda d: jnp.stack([d[n] for n in VECTORS] + [pad(d['b_if'])])
    vres = _adamw(parts['vec'], stack(w), stack(mom), stack(var), "adamw_vectors")
    for i, n in enumerate(VECTORS):
        out[n] = [r[i] for r in vres]
    out['b_if'] = [r[len(VECTORS), :N_IF] for r in vres]

    loss = lax.psum(loss, ("x", "y", "c"))
    res = [loss, dx.reshape(1, S, D)]
    for k in range(4):
        res += [out[n][k] for n in WEIGHTS]
    return tuple(res)
```

```python
import functools
import math

import jax
import jax.numpy as jnp
from jax import lax
from jax.experimental import pallas as pl
from jax.experimental.pallas import tpu as pltpu

F32 = jnp.float32
BF16 = jnp.bfloat16

N_DEV = 8
N_HEADS = 4
CONV_W = 31
QK_W = 4
N_IF = 2 * N_HEADS
IF_PAD = 128
EPS = 1e-6
NEG = -1e30

ADAM_LR = 0.001
ADAM_B1 = 0.9
ADAM_B2 = 0.999
ADAM_EPS = 1e-08
ADAM_WD = 0.01
ADAM_STEP = 10

TOK_TILE = 256
ML_CHUNK = 256
HALO = 32
VMEM_LIMIT = 56 * 1024 * 1024


def _cparams(sem=None):
    kw = dict(vmem_limit_bytes=VMEM_LIMIT)
    if sem is not None:
        kw["dimension_semantics"] = sem
    return pltpu.CompilerParams(**kw)


def _sig(x):
    return jax.nn.sigmoid(x)


def _silu(x):
    return x * _sig(x)


def _dsilu(x):
    s = _sig(x)
    return s * (1.0 + x * (1.0 - s))


def _logsig(x):
    return jnp.minimum(x, 0.0) - jnp.log(1.0 + jnp.exp(-jnp.abs(x)))


def _dot(a, b, dims):
    return lax.dot_general(a, b, (dims, ((), ())), preferred_element_type=F32)


def _dot_nn(a, b):
    return _dot(a, b, ((1,), (0,)))


def _dot_nt(a, b):
    return _dot(a, b, ((1,), (1,)))


def _dot_tn(a, b):
    return _dot(a, b, ((0,), (0,)))


def _split3(a):
    hi = a.astype(BF16)
    r1 = a - hi.astype(F32)
    mid = r1.astype(BF16)
    lo = (r1 - mid.astype(F32)).astype(BF16)
    return hi, mid, lo


def _tri_left(tri, a):
    hi, mid, lo = _split3(a)
    return _dot_nn(tri, hi) + _dot_nn(tri, mid) + _dot_nn(tri, lo)


def _tri_right(a, tri):
    hi, mid, lo = _split3(a)
    return _dot_nn(hi, tri) + _dot_nn(mid, tri) + _dot_nn(lo, tri)


def _fit(n, t):
    if n <= t:
        return n
    return max(c for c in range(128, t + 1, 128) if n % c == 0)


def _peer(k):
    x, y, c = lax.axis_index("x"), lax.axis_index("y"), lax.axis_index("c")
    px = 1 - x if (k >> 2) & 1 else x
    py = 1 - y if (k >> 1) & 1 else y
    pc = 1 - c if k & 1 else c
    return (px, py, pc), 4 * px + 2 * py + pc


def _xchg_copies(srcs, dsts, send_sems, recv_sems, loc_sems, scatter):
    _, me = _peer(0)
    copies = []
    for t, (src, dst) in enumerate(zip(srcs, dsts)):
        copies.append(pltpu.make_async_copy(src.at[me] if scatter else src, dst.at[me], loc_sems.at[t]))
        for k in range(1, N_DEV):
            dev, p = _peer(k)
            s = t * (N_DEV - 1) + k - 1
            copies.append(pltpu.make_async_remote_copy(
                src_ref=src.at[p] if scatter else src, dst_ref=dst.at[me],
                send_sem=send_sems.at[s], recv_sem=recv_sems.at[s],
                device_id=dev, device_id_type=pl.DeviceIdType.MESH))
    return copies


def _xchg_out_shapes(operands, scatter):
    return [jax.ShapeDtypeStruct((N_DEV,) + tuple(a.shape[1:] if scatter else a.shape), a.dtype) for a in operands]


def _xchg_sems(n):
    return [pltpu.SemaphoreType.DMA((n * (N_DEV - 1),)), pltpu.SemaphoreType.DMA((n * (N_DEV - 1),)),
            pltpu.SemaphoreType.DMA((n,))]


def _mm(a, b, *, mode, out_dtype, tm, tn, tk, name, n_outer=False, acc_in=None, side=None):
    if mode == "nn":
        (M, K), (K2, N) = a.shape, b.shape
    elif mode == "nt":
        (M, K), (N, K2) = a.shape, b.shape
    else:
        (K, M), (K2, N) = a.shape, b.shape
    assert K == K2, (a.shape, b.shape, mode)
    tm, tn, tk = _fit(M, tm), _fit(N, tn), _fit(K, tk)
    assert M % tm == 0 and N % tn == 0 and K % tk == 0, (a.shape, b.shape, tm, tn, tk)
    nk = K // tk
    if n_outer:
        grid = (N // tn, M // tm, nk)
        ij = lambda g0, g1: (g1, g0)
    else:
        grid = (M // tm, N // tn, nk)
        ij = lambda g0, g1: (g0, g1)

    if mode == "nn":
        a_spec = pl.BlockSpec((tm, tk), lambda g0, g1, k: (ij(g0, g1)[0], k))
        b_spec = pl.BlockSpec((tk, tn), lambda g0, g1, k: (k, ij(g0, g1)[1]))
        dot = _dot_nn
    elif mode == "nt":
        a_spec = pl.BlockSpec((tm, tk), lambda g0, g1, k: (ij(g0, g1)[0], k))
        b_spec = pl.BlockSpec((tn, tk), lambda g0, g1, k: (ij(g0, g1)[1], k))
        dot = _dot_nt
    else:
        a_spec = pl.BlockSpec((tk, tm), lambda g0, g1, k: (k, ij(g0, g1)[0]))
        b_spec = pl.BlockSpec((tk, tn), lambda g0, g1, k: (k, ij(g0, g1)[1]))
        dot = _dot_tn
    o_spec = pl.BlockSpec((tm, tn), lambda g0, g1, k: ij(g0, g1))
    has_acc = acc_in is not None
    n_side = len(side[0]) if side is not None else 0
    scatter = side[1] if side is not None else False
    n_in = 2 + int(has_acc)

    def body(*refs):
        a_ref, b_ref = refs[0], refs[1]
        c_ref = refs[2] if has_acc else None
        srcs = refs[n_in:n_in + n_side]
        o_ref = refs[n_in + n_side]
        dsts = refs[n_in + n_side + 1:n_in + 2 * n_side + 1]
        scratch = refs[n_in + 2 * n_side + 1:]
        k = pl.program_id(2)
        if n_side:
            sems = scratch[-3:]
            first = (pl.program_id(0) == 0) & (pl.program_id(1) == 0) & (k == 0)
            last = (pl.program_id(0) == grid[0] - 1) & (pl.program_id(1) == grid[1] - 1) & (k == nk - 1)

            @pl.when(first)
            def _():
                for cp in _xchg_copies(srcs, dsts, *sems, scatter):
                    cp.start()

        p = dot(a_ref[...], b_ref[...])

        def finish(r):
            if has_acc:
                r = r + c_ref[...].astype(F32)
            o_ref[...] = r.astype(out_dtype)

        if nk == 1:
            finish(p)
        else:
            acc = scratch[0]

            @pl.when(k == 0)
            def _():
                acc[...] = p

            @pl.when((k > 0) & (k < nk - 1))
            def _():
                acc[...] += p

            @pl.when(k == nk - 1)
            def _():
                finish(acc[...] + p)

        if n_side:
            @pl.when(last)
            def _():
                for cp in _xchg_copies(srcs, dsts, *sems, scatter):
                    cp.wait()

    hbm = pl.BlockSpec(memory_space=pltpu.HBM)
    in_specs = [a_spec, b_spec]
    args = [a, b]
    if has_acc:
        in_specs.append(o_spec)
        args.append(acc_in)
    out_specs = [o_spec]
    out_shape = [jax.ShapeDtypeStruct((M, N), out_dtype)]
    scratch_shapes = [pltpu.VMEM((tm, tn), F32)] if nk > 1 else []
    if n_side:
        in_specs += [hbm] * n_side
        args += list(side[0])
        out_specs += [hbm] * n_side
        out_shape += _xchg_out_shapes(side[0], scatter)
        scratch_shapes += _xchg_sems(n_side)
        cp = pltpu.CompilerParams(vmem_limit_bytes=VMEM_LIMIT, has_side_effects=True,
                                  dimension_semantics=("arbitrary", "arbitrary", "arbitrary"))
    else:
        cp = _cparams(("parallel", "parallel", "arbitrary"))
    res = pl.pallas_call(
        body, name=name, grid=grid, in_specs=in_specs, out_specs=out_specs, out_shape=out_shape,
        scratch_shapes=scratch_shapes, compiler_params=cp,
    )(*args)
    return (res[0], list(res[1:])) if n_side else res[0]


def _rmsnorm_fwd(x, g, name):
    S, D = x.shape
    T = min(TOK_TILE, S)

    def body(x_ref, g_ref, h_ref, r_ref):
        xv = x_ref[...]
        r = lax.rsqrt(jnp.mean(xv * xv, axis=-1, keepdims=True) + EPS)
        h_ref[...] = (xv * r * g_ref[...]).astype(BF16)
        r_ref[...] = r

    return pl.pallas_call(
        body, name=name, grid=(S // T,),
        in_specs=[pl.BlockSpec((T, D), lambda i: (i, 0)), pl.BlockSpec((1, D), lambda i: (0, 0))],
        out_specs=[pl.BlockSpec((T, D), lambda i: (i, 0)), pl.BlockSpec((T, 1), lambda i: (i, 0))],
        out_shape=[jax.ShapeDtypeStruct((S, D), BF16), jax.ShapeDtypeStruct((S, 1), F32)],
        compiler_params=_cparams(("parallel",)),
    )(x, g)


LANE = 128
ROW_BLK = 64


def _shift_strip(src, sh, cs, nr):
    for r in range(1, 8):
        sh[r] = src[pl.ds(r, nr), cs]


def _tap(src, sh, cs, o, r0):
    r = o % 8
    if r == 0:
        return src[pl.ds(o + r0, ROW_BLK), cs]
    return sh[r, pl.ds(o - r + r0, ROW_BLK), :]


def _conv_fwd(z, w_dw, b_dw, g_ln, b_ln, D):
    S = z.shape[0]
    T = min(TOK_TILE, S)
    hb = T // HALO

    def body(a_ref, b_ref, zc_ref, ah_ref, bh_ref, w_ref, bdw_ref, gln_ref, bln_ref, u1_ref, pc_ref, u0s, sh):
        i = pl.program_id(0)
        a = a_ref[...].astype(F32)
        b = b_ref[...].astype(F32)
        u0s[pl.ds(HALO, T), :] = a * _sig(b)
        halo = ah_ref[...].astype(F32) * _sig(bh_ref[...].astype(F32))
        u0s[pl.ds(0, HALO), :] = jnp.where(i > 0, halo, 0.0)
        u0s[pl.ds(T + HALO, 8), :] = jnp.zeros((8, D), F32)
        for c0 in range(0, D, LANE):
            cs = pl.ds(c0, LANE)
            _shift_strip(u0s, sh, cs, T + HALO)
            for r0 in range(0, T, ROW_BLK):
                acc = jnp.broadcast_to(bdw_ref[:, cs], (ROW_BLK, LANE))
                for j in range(CONV_W):
                    acc = acc + w_ref[pl.ds(j, 1), cs] * _tap(u0s, sh, cs, HALO - (CONV_W - 1) + j, r0)
                u1_ref[pl.ds(r0, ROW_BLK), cs] = acc
        acc = u1_ref[...]
        mu = jnp.mean(acc, axis=-1, keepdims=True)
        xc = acc - mu
        var = jnp.mean(xc * xc, axis=-1, keepdims=True)
        ln = xc * lax.rsqrt(var + EPS) * gln_ref[...] + bln_ref[...]
        pc_ref[...] = (_silu(ln) * _silu(zc_ref[...].astype(F32))).astype(BF16)

    prev = lambda i: (jnp.maximum(i * hb - 1, 0), 0)
    prev_b = lambda i: (jnp.maximum(i * hb - 1, 0), D // D)
    vec = pl.BlockSpec((1, D), lambda i: (0, 0))
    return pl.pallas_call(
        body, name="conv_fwd", grid=(S // T,),
        in_specs=[pl.BlockSpec((T, D), lambda i: (i, 0)), pl.BlockSpec((T, D), lambda i: (i, 1)),
                  pl.BlockSpec((T, D), lambda i: (i, 2)),
                  pl.BlockSpec((HALO, D), prev), pl.BlockSpec((HALO, D), prev_b),
                  pl.BlockSpec((32, D), lambda i: (0, 0)), vec, vec, vec],
        out_specs=[pl.BlockSpec((T, D), lambda i: (i, 0)), pl.BlockSpec((T, D), lambda i: (i, 0))],
        out_shape=[jax.ShapeDtypeStruct((S, D), F32), jax.ShapeDtypeStruct((S, D), BF16)],
        scratch_shapes=[pltpu.VMEM((T + HALO + 8, D), F32), pltpu.VMEM((8, T + HALO, LANE), F32)],
        compiler_params=_cparams(("parallel",)),
    )(z, z, z, z, z, w_dw, b_dw, g_ln, b_ln)


def _qk_fwd(z, w_qk, D):
    S = z.shape[0]
    T = min(TOK_TILE, S)
    hb = T // HALO
    scale = (D // N_HEADS) ** -0.5
    W2 = 2 * D

    def body(p0_ref, p1_ref, h0_ref, h1_ref, w_ref, o_ref, ps):
        i = pl.program_id(0)
        ps[pl.ds(HALO, T), pl.ds(0, D)] = p0_ref[...].astype(F32)
        ps[pl.ds(HALO, T), pl.ds(D, D)] = p1_ref[...].astype(F32)
        ps[pl.ds(0, HALO), pl.ds(0, D)] = jnp.where(i > 0, h0_ref[...].astype(F32), 0.0)
        ps[pl.ds(0, HALO), pl.ds(D, D)] = jnp.where(i > 0, h1_ref[...].astype(F32), 0.0)
        acc = jnp.zeros((T, W2), F32)
        for j in range(QK_W):
            acc = acc + w_ref[pl.ds(j, 1), :] * ps[pl.ds(HALO - (QK_W - 1) + j, T), :]
        qk = _silu(acc)
        o_ref[:, pl.ds(0, D)] = qk[:, :D].astype(BF16)
        o_ref[:, pl.ds(D, D)] = (qk[:, D:] * scale).astype(BF16)

    prev0 = lambda i: (jnp.maximum(i * hb - 1, 0), 3)
    prev1 = lambda i: (jnp.maximum(i * hb - 1, 0), 4)
    return pl.pallas_call(
        body, name="qk_fwd", grid=(S // T,),
        in_specs=[pl.BlockSpec((T, D), lambda i: (i, 3)), pl.BlockSpec((T, D), lambda i: (i, 4)),
                  pl.BlockSpec((HALO, D), prev0), pl.BlockSpec((HALO, D), prev1),
                  pl.BlockSpec((8, W2), lambda i: (0, 0))],
        out_specs=pl.BlockSpec((T, W2), lambda i: (i, 0)),
        out_shape=jax.ShapeDtypeStruct((S, W2), BF16),
        scratch_shapes=[pltpu.VMEM((T + HALO, W2), F32)],
        compiler_params=_cparams(("parallel",)),
    )(z, z, z, z, w_qk)


def _ml_gates(gc_ref, gr_ref, bifr_ref, bifc_ref, L):
    gc = gc_ref[...] + bifr_ref[...]
    gr = gr_ref[...] + bifc_ref[...]
    ii = lax.broadcasted_iota(jnp.int32, (L, L), 0)
    jj = lax.broadcasted_iota(jnp.int32, (L, L), 1)
    tri = (jj <= ii).astype(BF16)
    triu = (ii <= jj).astype(BF16)
    bcol_all = _tri_left(tri, _logsig(gc))
    brow_all = _tri_right(_logsig(gr), triu)
    return gc, gr, bcol_all, brow_all, ii, jj


def _ml_intra(q, k, b_c, b_r, li_r, m, n_row, ii, jj):
    d = b_c - b_r + li_r
    dm = jnp.where(jj <= ii, d, NEG)
    inter = b_c + m
    m_row = jnp.maximum(inter, jnp.max(dm, axis=1, keepdims=True))
    w_intra = jnp.exp(dm - m_row)
    w_inter = jnp.exp(inter - m_row)
    qk = _dot_nt(q, k)
    s = qk * w_intra
    qn = jnp.sum(q.astype(F32) * n_row, axis=-1, keepdims=True)
    den = jnp.sum(s, axis=-1, keepdims=True) + w_inter * qn
    hdiv = jnp.maximum(jnp.abs(den), jnp.exp(-m_row))
    return qk, w_intra, w_inter, s, qn, den, hdiv, m_row


def _ml_state(b_c, li_c, m, L):
    b_last = b_c[L - 1:L, :]
    g_c = b_last - b_c + li_c
    m_new = jnp.maximum(b_last + m, jnp.max(g_c, axis=0, keepdims=True))
    decay = jnp.exp(b_last + m - m_new)
    wk = jnp.exp(g_c - m_new)
    return m_new, decay, wk


def _mlstm_fwd(qks, z, gcol, grow, bif_row, bif_col, D):
    S = qks.shape[0]
    L = min(ML_CHUNK, S)
    NC = S // L
    H = N_HEADS
    dh = D // H

    def body(q_ref, k_ref, v_ref, gc_ref, gr_ref, bifr_ref, bifc_ref, hm_ref, cst_ref, nst_ref, C, NM):
        c = pl.program_id(0)

        @pl.when(c == 0)
        def _():
            C[...] = jnp.zeros_like(C)
            NM[...] = jnp.zeros_like(NM)

        gc, gr, bcol_all, brow_all, ii, jj = _ml_gates(gc_ref, gr_ref.at[0], bifr_ref, bifc_ref, L)
        for h in range(H):
            hs = pl.ds(h * dh, dh)
            q = q_ref[:, hs]
            k = k_ref[:, hs]
            v = v_ref[:, hs]
            b_c = bcol_all[:, H + h:H + h + 1]
            li_c = gc[:, h:h + 1]
            b_r = brow_all[H + h:H + h + 1, :]
            li_r = gr[h:h + 1, :]
            n_row = NM[h, 0:1, :]
            m = NM[h, 1:2, 0:1]
            Cb = C[h].astype(BF16)
            cst_ref[0, h] = Cb
            nst_ref[0, h] = NM[h]
            qk, w_intra, w_inter, s, qn, den, hdiv, m_row = _ml_intra(q, k, b_c, b_r, li_r, m, n_row, ii, jj)
            num = _dot_nn(s.astype(BF16), v) + w_inter * _dot_nn(q, Cb)
            hm_ref[:, hs] = num / hdiv
            m_new, decay, wk = _ml_state(b_c, li_c, m, L)
            wkk = wk * k.astype(F32)
            C[h] = decay * C[h] + _dot_tn(wkk.astype(BF16), v)
            NM[h, 0:1, :] = decay * n_row + jnp.sum(wkk, axis=0, keepdims=True)
            NM[h, 1:2, :] = jnp.broadcast_to(m_new, (1, dh))

    return pl.pallas_call(
        body, name="mlstm_fwd", grid=(NC,),
        in_specs=[pl.BlockSpec((L, D), lambda c: (c, 0)), pl.BlockSpec((L, D), lambda c: (c, 1)),
                  pl.BlockSpec((L, D), lambda c: (c, 5)),
                  pl.BlockSpec((L, IF_PAD), lambda c: (c, 0)),
                  pl.BlockSpec((1, 16, L), lambda c: (c, 0, 0)),
                  pl.BlockSpec((1, IF_PAD), lambda c: (0, 0)), pl.BlockSpec((16, 1), lambda c: (0, 0))],
        out_specs=[pl.BlockSpec((L, D), lambda c: (c, 0)),
                   pl.BlockSpec((1, H, dh, dh), lambda c: (c, 0, 0, 0)),
                   pl.BlockSpec((1, H, 8, dh), lambda c: (c, 0, 0, 0))],
        out_shape=[jax.ShapeDtypeStruct((S, D), F32), jax.ShapeDtypeStruct((NC, H, dh, dh), BF16),
                   jax.ShapeDtypeStruct((NC, H, 8, dh), F32)],
        scratch_shapes=[pltpu.VMEM((H, dh, dh), F32), pltpu.VMEM((H, 8, dh), F32)],
        compiler_params=_cparams(("arbitrary",)),
    )(qks, qks, z, gcol, grow, bif_row, bif_col)


def _mlstm_bwd(qks, z, gcol, grow, bif_row, bif_col, hm, dhm, cst, nst, dz, D):
    S = qks.shape[0]
    L = min(ML_CHUNK, S)
    NC = S // L
    H = N_HEADS
    dh = D // H

    def body(q_ref, k_ref, v_ref, gc_ref, gr_ref, bifr_ref, bifc_ref, hm_ref, dhm_ref, cst_ref, nst_ref, dz_any,
             dqk_ref, dg_ref, dv_ref, dC, dN):
        del dz_any
        c = pl.program_id(0)

        @pl.when(c == 0)
        def _():
            dC[...] = jnp.zeros_like(dC)
            dN[...] = jnp.zeros_like(dN)

        gc, gr, bcol_all, brow_all, ii, jj = _ml_gates(gc_ref, gr_ref.at[0], bifr_ref, bifc_ref, L)
        eye = ii == jj
        lane = lax.broadcasted_iota(jnp.int32, (L, IF_PAD), 1)
        db_all = jnp.zeros((L, IF_PAD), F32)
        dli_all = jnp.zeros((L, IF_PAD), F32)
        last_row = lax.broadcasted_iota(jnp.int32, (L, 1), 0) == L - 1
        for h in range(H):
            hs = pl.ds(h * dh, dh)
            q = q_ref[:, hs]
            k = k_ref[:, hs]
            v = v_ref[:, hs]
            qf = q.astype(F32)
            kf = k.astype(F32)
            b_c = bcol_all[:, H + h:H + h + 1]
            li_c = gc[:, h:h + 1]
            b_r = brow_all[H + h:H + h + 1, :]
            li_r = gr[h:h + 1, :]
            n_row = nst_ref[0, h, 0:1, :]
            m = nst_ref[0, h, 1:2, 0:1]
            Cb = cst_ref[0, h]
            qk, w_intra, w_inter, s, qn, den, hdiv, m_row = _ml_intra(q, k, b_c, b_r, li_r, m, n_row, ii, jj)
            dh_ = dhm_ref[:, hs]
            hh = hm_ref[:, hs]
            dnum = dh_ / hdiv
            dhdiv = -jnp.sum(dh_ * hh, axis=-1, keepdims=True) / hdiv
            dden = jnp.where(jnp.abs(den) > jnp.exp(-m_row), dhdiv * jnp.sign(den), 0.0)
            dnum_b = dnum.astype(BF16)
            ds = _dot_nt(dnum_b, v) + dden
            s_b = s.astype(BF16)
            dv = _dot_tn(s_b, dnum_b)
            dnC = _dot_nt(dnum_b, Cb)
            dwi = dden * w_inter
            dw_inter = jnp.sum(qf * dnC, axis=-1, keepdims=True) + dden * qn
            dq = w_inter * dnC + dwi * n_row
            dC_out = _dot_tn((w_inter * qf).astype(BF16), dnum_b)
            dn_out = jnp.sum(dwi * qf, axis=0, keepdims=True)
            dqk = (ds * w_intra).astype(BF16)
            dq = dq + _dot_nn(dqk, k)
            dk = _dot_tn(dqk, q)
            dd = ds * s
            rs_c = jnp.sum(dd, axis=1, keepdims=True)
            cs_r = jnp.sum(dd, axis=0, keepdims=True)
            cs_c = jnp.sum(jnp.where(eye, cs_r, 0.0), axis=1, keepdims=True)
            dinter = dw_inter * w_inter
            m_new, decay, wk = _ml_state(b_c, li_c, m, L)
            dCn = dC[h]
            dCn_b = dCn.astype(BF16)
            dn_new = dN[h, 0:1, :]
            ddecay = (jnp.sum(jnp.sum(dCn * Cb.astype(F32), axis=1, keepdims=True), axis=0, keepdims=True)
                      + jnp.sum(dn_new * n_row, axis=1, keepdims=True))
            dwkk = _dot_nt(v, dCn_b) + dn_new
            wkk_b = (wk * kf).astype(BF16)
            dv = dv + _dot_nn(wkk_b, dCn_b)
            dk = dk + wk * dwkk
            dg = jnp.sum(dwkk * kf, axis=-1, keepdims=True) * wk
            db_last = ddecay * decay + jnp.sum(dg, axis=0, keepdims=True)
            dC[h] = decay * dCn + dC_out
            dN[h, 0:1, :] = decay * dn_new + dn_out
            db_c = rs_c - cs_c + dinter - dg + jnp.where(last_row, db_last, 0.0)
            dli_c = cs_c + dg
            db_all = jnp.where(lane == H + h, db_c, db_all)
            dli_all = jnp.where(lane == h, dli_c, dli_all)
            dqk_ref[:, hs] = dq.astype(BF16)
            dqk_ref[:, pl.ds(D + h * dh, dh)] = dk.astype(BF16)
            dv_ref[:, hs] = dv.astype(BF16)
        triu = (ii <= jj).astype(BF16)
        dlf = _tri_left(triu, db_all)
        dg_ref[...] = dli_all + dlf * (1.0 - _sig(gc))

    r = lambda c: NC - 1 - c
    n_in = 12
    return pl.pallas_call(
        body, name="mlstm_bwd", grid=(NC,),
        in_specs=[pl.BlockSpec((L, D), lambda c: (r(c), 0)), pl.BlockSpec((L, D), lambda c: (r(c), 1)),
                  pl.BlockSpec((L, D), lambda c: (r(c), 5)),
                  pl.BlockSpec((L, IF_PAD), lambda c: (r(c), 0)),
                  pl.BlockSpec((1, 16, L), lambda c: (r(c), 0, 0)),
                  pl.BlockSpec((1, IF_PAD), lambda c: (0, 0)), pl.BlockSpec((16, 1), lambda c: (0, 0)),
                  pl.BlockSpec((L, D), lambda c: (r(c), 0)), pl.BlockSpec((L, D), lambda c: (r(c), 0)),
                  pl.BlockSpec((1, H, dh, dh), lambda c: (r(c), 0, 0, 0)),
                  pl.BlockSpec((1, H, 8, dh), lambda c: (r(c), 0, 0, 0)),
                  pl.BlockSpec(memory_space=pl.ANY)],
        out_specs=[pl.BlockSpec((L, 2 * D), lambda c: (r(c), 0)),
                   pl.BlockSpec((L, IF_PAD), lambda c: (r(c), 0)),
                   pl.BlockSpec((L, D), lambda c: (r(c), 5))],
        out_shape=[jax.ShapeDtypeStruct((S, 2 * D), BF16),
                   jax.ShapeDtypeStruct((S, IF_PAD), F32), jax.ShapeDtypeStruct(dz.shape, dz.dtype)],
        input_output_aliases={n_in - 1: 2},
        scratch_shapes=[pltpu.VMEM((H, dh, dh), F32), pltpu.VMEM((H, 8, dh), F32)],
        compiler_params=_cparams(("arbitrary",)),
    )(qks, qks, z, gcol, grow, bif_row, bif_col, hm, dhm, cst, nst, dz)


def _ml_post_fwd(hm, z, g_head, D):
    S = hm.shape[0]
    T = min(TOK_TILE, S)
    dh = D // N_HEADS

    def body(hm_ref, o_ref, zm_ref, g_ref, pm_ref):
        for h in range(N_HEADS):
            hs = pl.ds(h * dh, dh)
            hg = _sig(o_ref[:, hs].astype(F32)) * hm_ref[:, hs]
            rs = lax.rsqrt(jnp.mean(hg * hg, axis=-1, keepdims=True) + EPS)
            pm_ref[:, hs] = (hg * rs * g_ref[:, hs] * _silu(zm_ref[:, hs].astype(F32))).astype(BF16)

    return pl.pallas_call(
        body, name="ml_post_fwd", grid=(S // T,),
        in_specs=[pl.BlockSpec((T, D), lambda i: (i, 0)), pl.BlockSpec((T, D), lambda i: (i, 6)),
                  pl.BlockSpec((T, D), lambda i: (i, 7)), pl.BlockSpec((1, D), lambda i: (0, 0))],
        out_specs=pl.BlockSpec((T, D), lambda i: (i, 0)),
        out_shape=jax.ShapeDtypeStruct((S, D), BF16),
        compiler_params=_cparams(("parallel",)),
    )(hm, z, z, g_head)


def _ml_post_bwd(dpm, hm, z, g_head, dz, D):
    S = hm.shape[0]
    T = min(TOK_TILE, S)
    dh = D // N_HEADS

    def body(dpm_ref, hm_ref, o_ref, zm_ref, g_ref, dz_any, dhm_ref, dg_ref, dozm_ref):
        del dz_any
        do_ref = dozm_ref.at[:, pl.ds(0, D)]
        dzm_ref = dozm_ref.at[:, pl.ds(D, D)]
        i = pl.program_id(0)

        @pl.when(i == 0)
        def _():
            dg_ref[...] = jnp.zeros_like(dg_ref)

        for h in range(N_HEADS):
            hs = pl.ds(h * dh, dh)
            o = o_ref[:, hs].astype(F32)
            zm = zm_ref[:, hs].astype(F32)
            hmv = hm_ref[:, hs]
            dpm_ = dpm_ref[:, hs]
            g = g_ref[:, hs]
            so = _sig(o)
            hg = so * hmv
            rs = lax.rsqrt(jnp.mean(hg * hg, axis=-1, keepdims=True) + EPS)
            xn = hg * rs
            dzm_ref[:, hs] = (dpm_ * xn * g * _dsilu(zm)).astype(BF16)
            dhn = dpm_ * _silu(zm)
            dg_ref[0:1, hs] += jnp.sum(dhn * xn, axis=0, keepdims=True)
            dxn = dhn * g
            dhg = rs * (dxn - xn * jnp.mean(dxn * xn, axis=-1, keepdims=True))
            do_ref[:, hs] = (dhg * hmv * so * (1.0 - so)).astype(BF16)
            dhm_ref[:, hs] = dhg * so

    return pl.pallas_call(
        body, name="ml_post_bwd", grid=(S // T,),
        in_specs=[pl.BlockSpec((T, D), lambda i: (i, 0)), pl.BlockSpec((T, D), lambda i: (i, 0)),
                  pl.BlockSpec((T, D), lambda i: (i, 6)), pl.BlockSpec((T, D), lambda i: (i, 7)),
                  pl.BlockSpec((1, D), lambda i: (0, 0)), pl.BlockSpec(memory_space=pl.ANY)],
        out_specs=[pl.BlockSpec((T, D), lambda i: (i, 0)), pl.BlockSpec((8, D), lambda i: (0, 0)),
                   pl.BlockSpec((T, 2 * D), lambda i: (i, 3))],
        out_shape=[jax.ShapeDtypeStruct((S, D), F32), jax.ShapeDtypeStruct((8, D), F32),
                   jax.ShapeDtypeStruct(dz.shape, dz.dtype)],
        input_output_aliases={5: 2},
        compiler_params=_cparams(("arbitrary",)),
    )(dpm, hm, z, z, g_head, dz)


def _xattn_probs(q, km_h, scale):
    s = _dot_nt(q, km_h) * scale
    e = jnp.exp(s - jnp.max(s, axis=-1, keepdims=True))
    return e / jnp.sum(e, axis=-1, keepdims=True)


def _xattn_fwd(z, kv, D):
    S = z.shape[0]
    M = kv.shape[0]
    T = min(TOK_TILE, S)
    dh = D // N_HEADS
    scale = dh ** -0.5

    def body(q_ref, zx_ref, km_ref, vm_ref, px_ref):
        for h in range(N_HEADS):
            hs = pl.ds(h * dh, dh)
            p = _xattn_probs(q_ref[:, hs], km_ref[:, hs], scale)
            o = _dot_nn(p.astype(BF16), vm_ref[:, hs])
            px_ref[:, hs] = (o * _silu(zx_ref[:, hs].astype(F32))).astype(BF16)

    return pl.pallas_call(
        body, name="xattn_fwd", grid=(S // T,),
        in_specs=[pl.BlockSpec((T, D), lambda i: (i, 8)), pl.BlockSpec((T, D), lambda i: (i, 9)),
                  pl.BlockSpec((M, D), lambda i: (0, 0)), pl.BlockSpec((M, D), lambda i: (0, 1))],
        out_specs=pl.BlockSpec((T, D), lambda i: (i, 0)),
        out_shape=jax.ShapeDtypeStruct((S, D), BF16),
        compiler_params=_cparams(("parallel",)),
    )(z, z, kv, kv)


def _xattn_bwd(dpx, z, kv, dz, D):
    S = z.shape[0]
    M = kv.shape[0]
    T = min(TOK_TILE, S)
    dh = D // N_HEADS
    scale = dh ** -0.5

    def body(dpx_ref, q_ref, zx_ref, km_ref, vm_ref, dz_any, dkv_ref, dqz_ref):
        del dz_any
        i = pl.program_id(0)

        @pl.when(i == 0)
        def _():
            dkv_ref[...] = jnp.zeros_like(dkv_ref)

        for h in range(N_HEADS):
            hs = pl.ds(h * dh, dh)
            q = q_ref[:, hs]
            zx = zx_ref[:, hs].astype(F32)
            dpx_ = dpx_ref[:, hs]
            p = _xattn_probs(q, km_ref[:, hs], scale)
            p_b = p.astype(BF16)
            o = _dot_nn(p_b, vm_ref[:, hs])
            dqz_ref[:, pl.ds(D + h * dh, dh)] = (dpx_ * o * _dsilu(zx)).astype(BF16)
            do = (dpx_ * _silu(zx)).astype(BF16)
            dp = _dot_nt(do, vm_ref[:, hs])
            dsc = (p * (dp - jnp.sum(p * dp, axis=-1, keepdims=True)) * scale).astype(BF16)
            dqz_ref[:, hs] = _dot_nn(dsc, km_ref[:, hs]).astype(BF16)
            dkv_ref[:, hs] += _dot_tn(dsc, q)
            dkv_ref[:, pl.ds(D + h * dh, dh)] += _dot_tn(p_b, do)

    return pl.pallas_call(
        body, name="xattn_bwd", grid=(S // T,),
        in_specs=[pl.BlockSpec((T, D), lambda i: (i, 0)),
                  pl.BlockSpec((T, D), lambda i: (i, 8)), pl.BlockSpec((T, D), lambda i: (i, 9)),
                  pl.BlockSpec((M, D), lambda i: (0, 0)), pl.BlockSpec((M, D), lambda i: (0, 1)),
                  pl.BlockSpec(memory_space=pl.ANY)],
        out_specs=[pl.BlockSpec((M, 2 * D), lambda i: (0, 0)), pl.BlockSpec((T, 2 * D), lambda i: (i, 4))],
        out_shape=[jax.ShapeDtypeStruct((M, 2 * D), F32), jax.ShapeDtypeStruct(dz.shape, dz.dtype)],
        input_output_aliases={5: 1},
        compiler_params=_cparams(("arbitrary",)),
    )(dpx, z, z, kv, kv, dz)


def _gain_grad(dn, xin, rstd):
    R, D = dn.shape

    def body(dn_ref, x_ref, r_ref, o_ref):
        o_ref[...] = jnp.zeros_like(o_ref)
        o_ref[0:1, :] = jnp.sum(dn_ref[...] * x_ref[...] * r_ref[...], axis=0, keepdims=True)

    return pl.pallas_call(
        body, name="gain_grad", out_shape=jax.ShapeDtypeStruct((8, D), F32),
        compiler_params=_cparams(),
    )(dn, xin, rstd)


def _merge_fwd(z, yc, ym, yx, D):
    S = z.shape[0]
    T = min(TOK_TILE, S)

    def body(g0, g1, g2, yc_ref, ym_ref, yx_ref, o_ref):
        o_ref[...] = (_sig(g0[...].astype(F32)) * yc_ref[...] + _sig(g1[...].astype(F32)) * ym_ref[...]
                      + _sig(g2[...].astype(F32)) * yx_ref[...]).astype(BF16)

    tok = pl.BlockSpec((T, D), lambda i: (i, 0))
    return pl.pallas_call(
        body, name="merge_fwd", grid=(S // T,),
        in_specs=[pl.BlockSpec((T, D), lambda i: (i, 10)), pl.BlockSpec((T, D), lambda i: (i, 11)),
                  pl.BlockSpec((T, D), lambda i: (i, 12)), tok, tok, tok],
        out_specs=tok, out_shape=jax.ShapeDtypeStruct((S, D), BF16),
        compiler_params=_cparams(("parallel",)),
    )(z, z, z, yc, ym, yx)


def _merge_bwd(dmg, z, yc, ym, yx, D):
    S = z.shape[0]
    T = min(TOK_TILE, S)

    def body(dm_ref, g_ref, yc_ref, ym_ref, yx_ref, dyc_ref, dym_ref, dyx_ref, dg_ref):
        j = pl.program_id(1)
        for jj, (y_ref, dy_ref) in enumerate(((yc_ref, dyc_ref), (ym_ref, dym_ref), (yx_ref, dyx_ref))):
            @pl.when(j == jj)
            def _():
                sg = _sig(g_ref[...].astype(F32))
                dm = dm_ref[...]
                dy_ref[...] = (dm * sg).astype(BF16)
                dg_ref[...] = (dm * y_ref[...] * sg * (1.0 - sg)).astype(BF16)

    tok = pl.BlockSpec((T, D), lambda i, j: (i, 0))
    return pl.pallas_call(
        body, name="merge_bwd", grid=(S // T, 3),
        in_specs=[tok, pl.BlockSpec((T, D), lambda i, j: (i, 10 + j)), tok, tok, tok],
        out_specs=[tok, tok, tok, pl.BlockSpec((T, D), lambda i, j: (i, 10 + j))],
        out_shape=[jax.ShapeDtypeStruct((S, D), BF16)] * 3 + [jax.ShapeDtypeStruct((S, 13 * D), BF16)],
        compiler_params=_cparams(("arbitrary", "arbitrary")),
    )(dmg, z, yc, ym, yx)


def _post_loss(r, x, tgt, g_post):
    S, D = r.shape
    T = min(TOK_TILE, S)

    def body(r_ref, x_ref, t_ref, g_ref, dy_ref, dr_ref, dg_ref, loss_ref):
        i = pl.program_id(0)

        @pl.when(i == 0)
        def _():
            dg_ref[...] = jnp.zeros_like(dg_ref)
            loss_ref[...] = jnp.zeros_like(loss_ref)

        rv = r_ref[...]
        g = g_ref[...]
        rs = lax.rsqrt(jnp.mean(rv * rv, axis=-1, keepdims=True) + EPS)
        nrm = rv * rs
        e = x_ref[...] + nrm * g - t_ref[...]
        part = jnp.sum(jnp.sum(e * e, axis=-1, keepdims=True), axis=0, keepdims=True) * (0.5 / D)
        loss_ref[0:1, 0:1] += part
        dy = e * (1.0 / D)
        dy_ref[...] = dy
        dg_ref[0:1, :] += jnp.sum(dy * nrm, axis=0, keepdims=True)
        dn = dy * g
        dr_ref[...] = (rs * (dn - nrm * jnp.mean(dn * nrm, axis=-1, keepdims=True))).astype(BF16)

    tok = pl.BlockSpec((T, D), lambda i: (i, 0))
    return pl.pallas_call(
        body, name="post_loss", grid=(S // T,),
        in_specs=[tok, tok, tok, pl.BlockSpec((1, D), lambda i: (0, 0))],
        out_specs=[tok, tok, pl.BlockSpec((8, D), lambda i: (0, 0)), pl.BlockSpec((8, 128), lambda i: (0, 0))],
        out_shape=[jax.ShapeDtypeStruct((S, D), F32), jax.ShapeDtypeStruct((S, D), BF16),
                   jax.ShapeDtypeStruct((8, D), F32), jax.ShapeDtypeStruct((8, 128), F32)],
        compiler_params=_cparams(("arbitrary",)),
    )(r, x, tgt, g_post)


def _prenorm_bwd(dh, x, rstd, dy, g):
    S, D = x.shape
    T = min(TOK_TILE, S)

    def body(dh_ref, x_ref, r_ref, dy_ref, g_ref, dx_ref, dg_ref):
        i = pl.program_id(0)

        @pl.when(i == 0)
        def _():
            dg_ref[...] = jnp.zeros_like(dg_ref)

        rs = r_ref[...]
        xn = x_ref[...] * rs
        dhv = dh_ref[...]
        dg_ref[0:1, :] += jnp.sum(dhv * xn, axis=0, keepdims=True)
        dxn = dhv * g_ref[...]
        dx_ref[...] = dy_ref[...] + rs * (dxn - xn * jnp.mean(dxn * xn, axis=-1, keepdims=True))

    tok = pl.BlockSpec((T, D), lambda i: (i, 0))
    return pl.pallas_call(
        body, name="prenorm_bwd", grid=(S // T,),
        in_specs=[tok, tok, pl.BlockSpec((T, 1), lambda i: (i, 0)), tok, pl.BlockSpec((1, D), lambda i: (0, 0))],
        out_specs=[tok, pl.BlockSpec((8, D), lambda i: (0, 0))],
        out_shape=[jax.ShapeDtypeStruct((S, D), F32), jax.ShapeDtypeStruct((8, D), F32)],
        compiler_params=_cparams(("arbitrary",)),
    )(dh, x, rstd, dy, g)


def _conv_bwd1(dpc, z, u1, g_ln, b_ln, dz, D):
    S = z.shape[0]
    T = min(TOK_TILE, S)

    def body(dpc_ref, zc_ref, u1_ref, gln_ref, bln_ref, dz_any, du1_ref, st_ref, dzc_ref):
        del dz_any
        i = pl.program_id(0)

        @pl.when(i == 0)
        def _():
            st_ref[...] = jnp.zeros_like(st_ref)

        u1v = u1_ref[...]
        mu = jnp.mean(u1v, axis=-1, keepdims=True)
        xc = u1v - mu
        rs = lax.rsqrt(jnp.mean(xc * xc, axis=-1, keepdims=True) + EPS)
        xh = xc * rs
        g = gln_ref[...]
        ln = xh * g + bln_ref[...]
        zc = zc_ref[...].astype(F32)
        dpc_ = dpc_ref[...]
        dzc_ref[...] = (dpc_ * _silu(ln) * _dsilu(zc)).astype(BF16)
        dln = dpc_ * _silu(zc) * _dsilu(ln)
        st_ref[0:1, :] += jnp.sum(dln * xh, axis=0, keepdims=True)
        st_ref[1:2, :] += jnp.sum(dln, axis=0, keepdims=True)
        dxh = dln * g
        du1 = rs * (dxh - jnp.mean(dxh, axis=-1, keepdims=True) - xh * jnp.mean(dxh * xh, axis=-1, keepdims=True))
        du1_ref[...] = du1
        st_ref[2:3, :] += jnp.sum(du1, axis=0, keepdims=True)

    tok = pl.BlockSpec((T, D), lambda i: (i, 0))
    vec = pl.BlockSpec((1, D), lambda i: (0, 0))
    return pl.pallas_call(
        body, name="conv_bwd1", grid=(S // T,),
        in_specs=[tok, pl.BlockSpec((T, D), lambda i: (i, 2)), tok, vec, vec, pl.BlockSpec(memory_space=pl.ANY)],
        out_specs=[tok, pl.BlockSpec((8, D), lambda i: (0, 0)), pl.BlockSpec((T, D), lambda i: (i, 2))],
        out_shape=[jax.ShapeDtypeStruct((S, D), F32), jax.ShapeDtypeStruct((8, D), F32),
                   jax.ShapeDtypeStruct(dz.shape, dz.dtype)],
        input_output_aliases={5: 2},
        compiler_params=_cparams(("arbitrary",)),
    )(dpc, z, u1, g_ln, b_ln, dz)


def _conv_bwd2(du1, z, w_dw, dz, D):
    S = z.shape[0]
    T = min(TOK_TILE, S)
    hb = T // HALO
    last = S // HALO - 1

    n_steps = S // T

    def body(du_ref, dun_ref, a_ref, b_ref, ah_ref, bh_ref, w_ref, dz_any, dw_ref, dab_ref,
             dus, u0s, shd, shu, dwp):
        del dz_any
        i = pl.program_id(0)

        @pl.when(i == 0)
        def _():
            dwp[...] = jnp.zeros_like(dwp)

        dus[pl.ds(0, T), :] = du_ref[...]
        dus[pl.ds(T, HALO), :] = jnp.where(i < n_steps - 1, dun_ref[...], 0.0)
        dus[pl.ds(T + HALO, 8), :] = jnp.zeros((8, D), F32)
        u0s[pl.ds(HALO, T), :] = a_ref[...].astype(F32) * _sig(b_ref[...].astype(F32))
        halo = ah_ref[...].astype(F32) * _sig(bh_ref[...].astype(F32))
        u0s[pl.ds(0, HALO), :] = jnp.where(i > 0, halo, 0.0)
        u0s[pl.ds(T + HALO, 8), :] = jnp.zeros((8, D), F32)
        for c0 in range(0, D, LANE):
            cs = pl.ds(c0, LANE)
            _shift_strip(dus, shd, cs, T + HALO)
            _shift_strip(u0s, shu, cs, T + HALO)
            for r0 in range(0, T, ROW_BLK):
                rows = pl.ds(r0, ROW_BLK)
                du0 = jnp.zeros((ROW_BLK, LANE), F32)
                for j in range(CONV_W):
                    du0 = du0 + w_ref[pl.ds(j, 1), cs] * _tap(dus, shd, cs, CONV_W - 1 - j, r0)
                a = a_ref[rows, cs].astype(F32)
                sb = _sig(b_ref[rows, cs].astype(F32))
                dab_ref[rows, cs] = (du0 * sb).astype(BF16)
                dab_ref[rows, pl.ds(D + c0, LANE)] = (du0 * a * sb * (1.0 - sb)).astype(BF16)
                d = dus[rows, cs]
                for j in range(CONV_W):
                    prod = d * _tap(u0s, shu, cs, HALO - (CONV_W - 1) + j, r0)
                    part = prod[0:8]
                    for q in range(8, ROW_BLK, 8):
                        part = part + prod[q:q + 8]
                    dwp[pl.ds(8 * j, 8), cs] += part

        @pl.when(i == n_steps - 1)
        def _():
            dw_ref[...] = jnp.zeros_like(dw_ref)
            for j in range(CONV_W):
                dw_ref[pl.ds(j, 1), :] = jnp.sum(dwp[pl.ds(8 * j, 8), :], axis=0, keepdims=True)

    tok = pl.BlockSpec((T, D), lambda i: (i, 0))
    nxt = lambda i: (jnp.minimum((i + 1) * hb, last), 0)
    prev = lambda i: (jnp.maximum(i * hb - 1, 0), 0)
    prev_b = lambda i: (jnp.maximum(i * hb - 1, 0), 1)
    return pl.pallas_call(
        body, name="conv_bwd2", grid=(S // T,),
        in_specs=[tok, pl.BlockSpec((HALO, D), nxt), tok, pl.BlockSpec((T, D), lambda i: (i, 1)),
                  pl.BlockSpec((HALO, D), prev), pl.BlockSpec((HALO, D), prev_b),
                  pl.BlockSpec((32, D), lambda i: (0, 0)), pl.BlockSpec(memory_space=pl.ANY)],
        out_specs=[pl.BlockSpec((32, D), lambda i: (0, 0)), pl.BlockSpec((T, 2 * D), lambda i: (i, 0))],
        out_shape=[jax.ShapeDtypeStruct((32, D), F32), jax.ShapeDtypeStruct(dz.shape, dz.dtype)],
        input_output_aliases={7: 1},
        scratch_shapes=[pltpu.VMEM((T + HALO + 8, D), F32), pltpu.VMEM((T + HALO + 8, D), F32),
                        pltpu.VMEM((8, T + HALO, LANE), F32), pltpu.VMEM((8, T + HALO, LANE), F32),
                        pltpu.VMEM((8 * 32, D), F32)],
        compiler_params=_cparams(("arbitrary",)),
    )(du1, du1, z, z, z, z, w_dw, dz)


def _qk_bwd1(dqks, z, w_qk, D):
    S = z.shape[0]
    T = min(TOK_TILE, S)
    hb = T // HALO
    scale = (D // N_HEADS) ** -0.5
    W2 = 2 * D

    def body(d_ref, p0_ref, p1_ref, h0_ref, h1_ref, w_ref, dc_ref, dw_ref, ps):
        i = pl.program_id(0)

        @pl.when(i == 0)
        def _():
            dw_ref[...] = jnp.zeros_like(dw_ref)

        ps[pl.ds(HALO, T), pl.ds(0, D)] = p0_ref[...].astype(F32)
        ps[pl.ds(HALO, T), pl.ds(D, D)] = p1_ref[...].astype(F32)
        ps[pl.ds(0, HALO), pl.ds(0, D)] = jnp.where(i > 0, h0_ref[...].astype(F32), 0.0)
        ps[pl.ds(0, HALO), pl.ds(D, D)] = jnp.where(i > 0, h1_ref[...].astype(F32), 0.0)
        acc = jnp.zeros((T, W2), F32)
        for j in range(QK_W):
            acc = acc + w_ref[pl.ds(j, 1), :] * ps[pl.ds(HALO - (QK_W - 1) + j, T), :]
        col = lax.broadcasted_iota(jnp.int32, (1, W2), 1)
        dc = d_ref[...].astype(F32) * _dsilu(acc) * jnp.where(col < D, 1.0, scale)
        dc_ref[...] = dc.astype(BF16)
        for j in range(QK_W):
            dw_ref[pl.ds(j, 1), :] += jnp.sum(dc * ps[pl.ds(HALO - (QK_W - 1) + j, T), :], axis=0, keepdims=True)

    prev0 = lambda i: (jnp.maximum(i * hb - 1, 0), 3)
    prev1 = lambda i: (jnp.maximum(i * hb - 1, 0), 4)
    return pl.pallas_call(
        body, name="qk_bwd1", grid=(S // T,),
        in_specs=[pl.BlockSpec((T, W2), lambda i: (i, 0)),
                  pl.BlockSpec((T, D), lambda i: (i, 3)), pl.BlockSpec((T, D), lambda i: (i, 4)),
                  pl.BlockSpec((HALO, D), prev0), pl.BlockSpec((HALO, D), prev1),
                  pl.BlockSpec((8, W2), lambda i: (0, 0))],
        out_specs=[pl.BlockSpec((T, W2), lambda i: (i, 0)), pl.BlockSpec((8, W2), lambda i: (0, 0))],
        out_shape=[jax.ShapeDtypeStruct((S, W2), BF16), jax.ShapeDtypeStruct((8, W2), F32)],
        scratch_shapes=[pltpu.VMEM((T + HALO, W2), F32)],
        compiler_params=_cparams(("arbitrary",)),
    )(dqks, z, z, z, z, w_qk)


def _qk_bwd2(dc, w_qk, dz, D):
    S = dc.shape[0]
    T = min(TOK_TILE, S)
    hb = T // HALO
    last = S // HALO - 1

    def body(d_ref, dn_ref, w_ref, dz_any, o_ref, ds):
        del dz_any
        i = pl.program_id(0)
        n = pl.num_programs(0)
        ds[pl.ds(0, T), :] = d_ref[...].astype(F32)
        ds[pl.ds(T, HALO), :] = jnp.where(i < n - 1, dn_ref[...].astype(F32), 0.0)
        acc = jnp.zeros((T, D), F32)
        for j in range(QK_W):
            acc = acc + w_ref[pl.ds(j, 1), :] * ds[pl.ds(QK_W - 1 - j, T), :]
        o_ref[...] = acc.astype(BF16)

    return pl.pallas_call(
        body, name="qk_bwd2", grid=(S // T, 2),
        in_specs=[pl.BlockSpec((T, D), lambda i, j: (i, j)),
                  pl.BlockSpec((HALO, D), lambda i, j: (jnp.minimum((i + 1) * hb, last), j)),
                  pl.BlockSpec((8, D), lambda i, j: (0, j)), pl.BlockSpec(memory_space=pl.ANY)],
        out_specs=pl.BlockSpec((T, D), lambda i, j: (i, 3 + j)),
        out_shape=jax.ShapeDtypeStruct(dz.shape, dz.dtype),
        input_output_aliases={3: 0},
        scratch_shapes=[pltpu.VMEM((T + HALO, D), F32)],
        compiler_params=_cparams(("arbitrary", "arbitrary")),
    )(dc, dc, w_qk, dz)


def _row_tile(R, row_bytes, budget=3 * 1024 * 1024):
    cands = [t for t in range(16, R + 1, 16) if R % t == 0 and t * row_bytes <= budget]
    return max(cands) if cands else R


def _adamw(parts, w, m, v, name):
    R, C = w.shape
    tr = _row_tile(R, C * (4 * 7 + 8 * parts.dtype.itemsize))
    c1 = 1.0 / (1.0 - ADAM_B1 ** ADAM_STEP)
    c2 = 1.0 / (1.0 - ADAM_B2 ** ADAM_STEP)

    def body(p_ref, w_ref, m_ref, v_ref, g_ref, d_ref, nm_ref, nv_ref):
        g = p_ref[0].astype(F32)
        for s in range(1, N_DEV):
            g = g + p_ref[s].astype(F32)
        g_ref[...] = g
        mn = ADAM_B1 * m_ref[...] + (1.0 - ADAM_B1) * g
        vn = ADAM_B2 * v_ref[...] + (1.0 - ADAM_B2) * (g * g)
        nm_ref[...] = mn
        nv_ref[...] = vn
        d_ref[...] = -ADAM_LR * ((mn * c1) / (jnp.sqrt(vn * c2) + ADAM_EPS) + ADAM_WD * w_ref[...])

    blk = pl.BlockSpec((tr, C), lambda i: (i, 0))
    return pl.pallas_call(
        body, name=name, grid=(R // tr,),
        in_specs=[pl.BlockSpec((N_DEV, tr, C), lambda i: (0, i, 0)), blk, blk, blk],
        out_specs=[blk, blk, blk, blk],
        out_shape=[jax.ShapeDtypeStruct((R, C), F32)] * 4,
        compiler_params=_cparams(("parallel",)),
    )(parts, w, m, v)


def _exchange(operands, scatter, name):
    n = len(operands)

    def body(*refs):
        copies = _xchg_copies(refs[:n], refs[n:2 * n], *refs[2 * n:], scatter)
        for cp in copies:
            cp.start()
        for cp in copies:
            cp.wait()

    hbm = pl.BlockSpec(memory_space=pltpu.HBM)
    return pl.pallas_call(
        body, name=name, in_specs=[hbm] * n, out_specs=[hbm] * n, out_shape=_xchg_out_shapes(operands, scatter),
        scratch_shapes=_xchg_sems(n), compiler_params=pltpu.CompilerParams(has_side_effects=True),
    )(*operands)


def _gather_two_level(blk, name):
    def body(src, out, send_sems, recv_sems, loc_sem):
        x, y, c = lax.axis_index("x"), lax.axis_index("y"), lax.axis_index("c")
        me, sibling = (x, y, c), (x, y, 1 - c)
        chips = [(1 - x, y), (x, 1 - y), (1 - x, 1 - y)]

        def slot(px, py, pc):
            return out.at[4 * px + 2 * py + pc]

        def copy(k, block, to, from_src=False):
            return pltpu.make_async_remote_copy(
                src_ref=src if from_src else slot(*block), dst_ref=slot(*block),
                send_sem=send_sems.at[k], recv_sem=recv_sems.at[k],
                device_id=to, device_id_type=pl.DeviceIdType.MESH)

        mine = pltpu.make_async_copy(src, slot(*me), loc_sem)
        mine.start()
        first = [copy(0, me, sibling, True)] + [copy(1 + j, me, (*chip, c), True) for j, chip in enumerate(chips)]
        for cp in first:
            cp.start()
        passed = [copy(4 + j, (*chip, c), sibling) for j, chip in enumerate(chips)]
        for j, chip in enumerate(chips):
            copy(1 + j, (*chip, c), me).wait_recv()
            passed[j].start()
        copy(0, sibling, me).wait_recv()
        for j, chip in enumerate(chips):
            copy(4 + j, (*chip, 1 - c), me).wait_recv()
        for cp in first + passed:
            cp.wait_send()
        mine.wait()

    hbm = pl.BlockSpec(memory_space=pltpu.HBM)
    return pl.pallas_call(
        body, name=name, in_specs=[hbm], out_specs=hbm,
        out_shape=jax.ShapeDtypeStruct((N_DEV,) + blk.shape, blk.dtype),
        scratch_shapes=[pltpu.SemaphoreType.DMA((7,)), pltpu.SemaphoreType.DMA((7,)), pltpu.SemaphoreType.DMA],
        compiler_params=pltpu.CompilerParams(has_side_effects=True),
    )(blk)


LATE = ('w_conv_out', 'w_ml_out', 'w_xa_out', 'w_out', 'w_mem_kv', 'w_qk_conv', 'w_dw')
ROW_SHARDED = ('w_conv_out', 'w_ml_out', 'w_xa_out', 'w_out')
COL_SHARDED = ('w_in', 'w_mem_kv', 'w_qk_conv', 'w_dw')


def _cols_to_slabs(g):
    R, C8 = g.shape
    return g.reshape(R, N_DEV, C8 // N_DEV).transpose(1, 0, 2)


def _slabs_to_cols(a):
    n, R, C = a.shape
    return a.transpose(1, 0, 2).reshape(R, n * C)


def _assemble(name, slabs):
    if name in ROW_SHARDED:
        return slabs.reshape(slabs.shape[0] * slabs.shape[1], slabs.shape[2])
    return _slabs_to_cols(slabs)


def _to_slabs(name, g, dtype):
    if name in ROW_SHARDED:
        return g.reshape(N_DEV, g.shape[0] // N_DEV, g.shape[1]).astype(dtype)
    return _cols_to_slabs(g).astype(dtype)


def _local_step(x, mem, tgt, g_pre, wp, wif, wifT, b_if, b_dw, g_ln, b_ln, g_head, g_mem, g_post, late, dist):
    S, D = x.shape
    L = min(ML_CHUNK, S)
    NC = S // L
    bif_row = jnp.zeros((1, IF_PAD), F32).at[0, :N_IF].set(b_if)
    bif_col = jnp.zeros((16, 1), F32).at[:N_IF, 0].set(b_if)

    h, rstd = _rmsnorm_fwd(x, g_pre, "prenorm_fwd")
    in_proj = functools.partial(_mm, h, wp, mode="nn", out_dtype=BF16, tm=1024, tn=1024, tk=2048, name="in_proj",
                                n_outer=True)
    if dist:
        z, gathered = in_proj(side=([late[n] for n in LATE], False))
        full = {n: _assemble(n, a) for n, a in zip(LATE, gathered)}
    else:
        z = in_proj()
        full = late
    w_co, w_mo, w_xo, w_out, w_kv = (full[n] for n in LATE[:5])
    w_qk = jnp.zeros((8, 2 * D), F32).at[:QK_W].set(full['w_qk_conv'])
    w_dw = jnp.zeros((32, D), F32).at[:CONV_W].set(full['w_dw'])
    gcol = _mm(h, wif, mode="nn", out_dtype=F32, tm=1024, tn=IF_PAD, tk=2048, name="gates_col")
    grow = _mm(wifT, h, mode="nt", out_dtype=F32, tm=16, tn=1024, tk=2048, name="gates_row")
    grow = grow.reshape(16, NC, L).transpose(1, 0, 2)

    u1, pc = _conv_fwd(z, w_dw, b_dw, g_ln, b_ln, D)
    qks = _qk_fwd(z, w_qk, D)
    hm, cst, nst = _mlstm_fwd(qks, z, gcol, grow, bif_row, bif_col, D)
    pm = _ml_post_fwd(hm, z, g_head, D)
    memn, rstd_mem = _rmsnorm_fwd(mem, g_mem, "memnorm_fwd")
    kv = _mm(memn, w_kv, mode="nn", out_dtype=BF16, tm=256, tn=1024, tk=2048, name="mem_kv")
    px = _xattn_fwd(z, kv, D)

    proj = functools.partial(_mm, mode="nn", out_dtype=F32, tm=1024, tn=1024, tk=2048)
    yc = proj(pc, w_co, name="conv_out")
    ym = proj(pm, w_mo, name="ml_out")
    yx = proj(px, w_xo, name="xa_out")
    mg = _merge_fwd(z, yc, ym, yx, D)
    r = proj(mg, w_out, name="out_proj")
    dy, dr, dg_post, loss = _post_loss(r, x, tgt, g_post)

    dgrad = functools.partial(_mm, mode="nt", out_dtype=F32, tm=1024, tn=1024, tk=2048)
    wgrad = functools.partial(_mm, mode="tn", out_dtype=F32, tm=1024, tn=1024, tk=2048)
    dmg = dgrad(dr, w_out, name="out_proj_dx")
    dw_out = wgrad(mg, dr, name="out_proj_dw")
    dyc, dym, dyx, dz = _merge_bwd(dmg, z, yc, ym, yx, D)
    dpc = dgrad(dyc, w_co, name="conv_out_dx")
    dw_co = wgrad(pc, dyc, name="conv_out_dw")
    dpm = dgrad(dym, w_mo, name="ml_out_dx")
    dw_mo = wgrad(pm, dym, name="ml_out_dw")
    dpx = dgrad(dyx, w_xo, name="xa_out_dx")
    dw_xo = wgrad(px, dyx, name="xa_out_dw")

    dkv, dz = _xattn_bwd(dpx, z, kv, dz, D)
    dkv_b = dkv.astype(BF16)
    dw_kv = wgrad(memn, dkv_b, name="mem_kv_dw")
    dmemn = _mm(dkv_b, w_kv, mode="nt", out_dtype=F32, tm=256, tn=1024, tk=2048, name="mem_kv_dx")
    dg_mem = _gain_grad(dmemn, mem, rstd_mem)

    du1, cstats, dz = _conv_bwd1(dpc, z, u1, g_ln, b_ln, dz, D)
    dw_dw, dz = _conv_bwd2(du1, z, w_dw, dz, D)

    dhm, dg_head, dz = _ml_post_bwd(dpm, hm, z, g_head, dz, D)
    dqks, dgates, dz = _mlstm_bwd(qks, z, gcol, grow, bif_row, bif_col, hm, dhm, cst, nst, dz, D)
    dc, dw_qk = _qk_bwd1(dqks, z, w_qk, D)
    dz = _qk_bwd2(dc, w_qk, dz, D)

    g = dict(w_conv_out=dw_co, w_ml_out=dw_mo, w_xa_out=dw_xo, w_out=dw_out, w_mem_kv=dw_kv,
             w_qk_conv=dw_qk[:QK_W], w_dw=dw_dw[:CONV_W])
    dgates_b = dgates.astype(BF16)
    in_proj_dw = functools.partial(_mm, h, dz, mode="tn", out_dtype=F32, tm=1024, tn=1024, tk=2048,
                                   name="in_proj_dw", n_outer=True)
    in_proj_dx = functools.partial(_mm, dz, wp, mode="nt", out_dtype=F32, tm=512, tn=2048, tk=2048,
                                   name="in_proj_dx")
    parts = {}
    if dist:
        send = [_to_slabs(n, g[n], BF16 if n in LATE[:5] else F32) for n in LATE]
        dwp, recv = in_proj_dw(side=(send, True))
        parts.update(zip(LATE, recv))
    else:
        dwp = in_proj_dw()
    dwif = _mm(h, dgates_b, mode="tn", out_dtype=F32, tm=2048, tn=IF_PAD, tk=2048, name="gates_dw")
    dw_in = jnp.concatenate([dwp[:, :8 * D], dwif[:, :N_IF], dwp[:, 8 * D:]], axis=1)
    if dist:
        dh, recv = in_proj_dx(side=([_to_slabs('w_in', dw_in, BF16)], True))
        parts['w_in'] = recv[0]
    else:
        dh = in_proj_dx()
        g['w_in'] = dw_in
    dh = _mm(dgates_b, wif, mode="nt", out_dtype=F32, tm=1024, tn=1024, tk=IF_PAD, name="gates_dx", acc_in=dh)
    dx, dg_pre = _prenorm_bwd(dh, x, rstd, dy, g_pre)

    db_if = jnp.sum(dgates[:, :N_IF], axis=0)
    g.update(g_pre=dg_pre[0], b_if=db_if, b_dw=cstats[2], g_ln=cstats[0], b_ln=cstats[1], g_ml_head=dg_head[0],
             g_mem=dg_mem[0], g_post=dg_post[0])
    return loss[0, 0], dx, g, parts


WEIGHTS = ('g_pre', 'w_in', 'b_if', 'w_qk_conv', 'w_dw', 'b_dw', 'g_ln', 'b_ln', 'w_conv_out', 'g_ml_head',
           'w_ml_out', 'g_mem', 'w_mem_kv', 'w_xa_out', 'w_out', 'g_post')
VECTORS = ('g_pre', 'b_dw', 'g_ln', 'b_ln', 'g_ml_head', 'g_mem', 'g_post')


def kernel(x, mem, g_pre, w_in, b_if, w_qk_conv, w_dw, b_dw, g_ln, b_ln, w_conv_out, g_ml_head, w_ml_out, g_mem, w_mem_kv, w_xa_out, w_out, g_post, loss_target, m_g_pre, m_w_in, m_b_if, m_w_qk_conv, m_w_dw, m_b_dw, m_g_ln, m_b_ln, m_w_conv_out, m_g_ml_head, m_w_ml_out, m_g_mem, m_w_mem_kv, m_w_xa_out, m_w_out, m_g_post, v_g_pre, v_w_in, v_b_if, v_w_qk_conv, v_w_dw, v_b_dw, v_g_ln, v_b_ln, v_w_conv_out, v_g_ml_head, v_w_ml_out, v_g_mem, v_w_mem_kv, v_w_xa_out, v_w_out, v_g_post):
    S, D = x.shape[1], x.shape[2]
    w = dict(g_pre=g_pre, w_in=w_in, b_if=b_if, w_qk_conv=w_qk_conv, w_dw=w_dw, b_dw=b_dw, g_ln=g_ln, b_ln=b_ln,
             w_conv_out=w_conv_out, g_ml_head=g_ml_head, w_ml_out=w_ml_out, g_mem=g_mem, w_mem_kv=w_mem_kv,
             w_xa_out=w_xa_out, w_out=w_out, g_post=g_post)
    mom = dict(g_pre=m_g_pre, w_in=m_w_in, b_if=m_b_if, w_qk_conv=m_w_qk_conv, w_dw=m_w_dw, b_dw=m_b_dw,
               g_ln=m_g_ln, b_ln=m_b_ln, w_conv_out=m_w_conv_out, g_ml_head=m_g_ml_head, w_ml_out=m_w_ml_out,
               g_mem=m_g_mem, w_mem_kv=m_w_mem_kv, w_xa_out=m_w_xa_out, w_out=m_w_out, g_post=m_g_post)
    var = dict(g_pre=v_g_pre, w_in=v_w_in, b_if=v_b_if, w_qk_conv=v_w_qk_conv, w_dw=v_w_dw, b_dw=v_b_dw,
               g_ln=v_g_ln, b_ln=v_b_ln, w_conv_out=v_w_conv_out, g_ml_head=v_g_ml_head, w_ml_out=v_w_ml_out,
               g_mem=v_g_mem, w_mem_kv=v_w_mem_kv, w_xa_out=v_w_xa_out, w_out=v_w_out, g_post=v_g_post)

    w_in_f = _slabs_to_cols(_gather_two_level(w_in.astype(BF16), "gather_w_in"))
    wp = jnp.concatenate([w_in_f[:, :8 * D], w_in_f[:, 8 * D + N_IF:]], axis=1)
    wif = jnp.zeros((D, IF_PAD), BF16).at[:, :N_IF].set(w_in_f[:, 8 * D:8 * D + N_IF])
    wifT = jnp.zeros((16, D), BF16).at[:N_IF, :].set(w_in_f[:, 8 * D:8 * D + N_IF].T)
    late = {n: w[n].astype(BF16) if n in LATE[:5] else w[n] for n in LATE}
    row = lambda a: a.reshape(1, D)

    loss, dx, g, parts = _local_step(
        x[0], mem[0], loss_target[0], row(g_pre), wp, wif, wifT, b_if, row(b_dw), row(g_ln), row(b_ln),
        row(g_ml_head), row(g_mem), row(g_post), late, True)

    vec = jnp.stack([g[n] for n in VECTORS] + [jnp.zeros((D,), F32).at[:N_IF].set(g['b_if'])])
    parts['vec'] = _exchange([vec], False, "gather_vector_grads")[0]

    out = {}
    for n in ('w_in',) + LATE:
        out[n] = _adamw(parts[n], w[n], mom[n], var[n], "adamw_" + n)
    pad = lambda a: jnp.zeros((D,), F32).at[:N_IF].set(a)
    stack = lambda d: jnp.stack([d[n] for n in VECTORS] + [pad(d['b_if'])])
    vres = _adamw(parts['vec'], stack(w), stack(mom), stack(var), "adamw_vectors")
    for i, n in enumerate(VECTORS):
        out[n] = [r[i] for r in vres]
    out['b_if'] = [r[len(VECTORS), :N_IF] for r in vres]

    loss = lax.psum(loss, ("x", "y", "c"))
    res = [loss, dx.reshape(1, S, D)]
    for k in range(4):
        res += [out[n][k] for n in WEIGHTS]
    return tuple(res)
```

```python
import functools
import math

import jax
import jax.numpy as jnp
from jax import lax
from jax.experimental import pallas as pl
from jax.experimental.pallas import tpu as pltpu

F32 = jnp.float32
BF16 = jnp.bfloat16

N_DEV = 8
N_HEADS = 4
CONV_W = 31
QK_W = 4
N_IF = 2 * N_HEADS
IF_PAD = 128
EPS = 1e-6
NEG = -1e30

ADAM_LR = 0.001
ADAM_B1 = 0.9
ADAM_B2 = 0.999
ADAM_EPS = 1e-08
ADAM_WD = 0.01
ADAM_STEP = 10

TOK_TILE = 256
ML_CHUNK = 256
HALO = 32
BF16_ROWS = 16
VMEM_LIMIT = 56 * 1024 * 1024


def _cparams(sem=None):
    kw = dict(vmem_limit_bytes=VMEM_LIMIT)
    if sem is not None:
        kw["dimension_semantics"] = sem
    return pltpu.CompilerParams(**kw)


def _sig(x):
    return jax.nn.sigmoid(x)


def _silu(x):
    return x * _sig(x)


def _dsilu(x):
    s = _sig(x)
    return s * (1.0 + x * (1.0 - s))


def _logsig(x):
    return jnp.minimum(x, 0.0) - jnp.log(1.0 + jnp.exp(-jnp.abs(x)))


def _dot(a, b, dims):
    return lax.dot_general(a, b, (dims, ((), ())), preferred_element_type=F32)


def _dot_nn(a, b):
    return _dot(a, b, ((1,), (0,)))


def _dot_nt(a, b):
    return _dot(a, b, ((1,), (1,)))


def _dot_tn(a, b):
    return _dot(a, b, ((0,), (0,)))


def _split3(a):
    hi = a.astype(BF16)
    r1 = a - hi.astype(F32)
    mid = r1.astype(BF16)
    lo = (r1 - mid.astype(F32)).astype(BF16)
    return hi, mid, lo


def _tri_left(tri, a):
    hi, mid, lo = _split3(a)
    return _dot_nn(tri, hi) + _dot_nn(tri, mid) + _dot_nn(tri, lo)


def _tri_right(a, tri):
    hi, mid, lo = _split3(a)
    return _dot_nn(hi, tri) + _dot_nn(mid, tri) + _dot_nn(lo, tri)


def _fit(n, t):
    if n <= t:
        return n
    return max(c for c in range(128, t + 1, 128) if n % c == 0)


def _peer(k):
    x, y, c = lax.axis_index("x"), lax.axis_index("y"), lax.axis_index("c")
    px = 1 - x if (k >> 2) & 1 else x
    py = 1 - y if (k >> 1) & 1 else y
    pc = 1 - c if k & 1 else c
    return (px, py, pc), 4 * px + 2 * py + pc


class _RowWindows:
    def __init__(self, start, height):
        self.start, self.height = start, height


def _xchg_copies(srcs, dsts, send_sems, recv_sems, loc_sems, scatter):
    _, me = _peer(0)

    def piece(src, p):
        if isinstance(scatter, _RowWindows):
            return src.at[pl.ds(pl.multiple_of(scatter.start(p), BF16_ROWS), scatter.height), :]
        return src.at[p] if scatter else src

    copies = []
    for t, (src, dst) in enumerate(zip(srcs, dsts)):
        copies.append(pltpu.make_async_copy(piece(src, me), dst.at[me], loc_sems.at[t]))
        for k in range(1, N_DEV):
            dev, p = _peer(k)
            s = t * (N_DEV - 1) + k - 1
            copies.append(pltpu.make_async_remote_copy(
                src_ref=piece(src, p), dst_ref=dst.at[me],
                send_sem=send_sems.at[s], recv_sem=recv_sems.at[s],
                device_id=dev, device_id_type=pl.DeviceIdType.MESH))
    return copies


def _xchg_out_shapes(operands, scatter):
    def one(a):
        if isinstance(scatter, _RowWindows):
            return (scatter.height, a.shape[1])
        return tuple(a.shape[1:] if scatter else a.shape)
    return [jax.ShapeDtypeStruct((N_DEV,) + one(a), a.dtype) for a in operands]


def _xchg_sems(n):
    return [pltpu.SemaphoreType.DMA((n * (N_DEV - 1),)), pltpu.SemaphoreType.DMA((n * (N_DEV - 1),)),
            pltpu.SemaphoreType.DMA((n,))]


def _mm(a, b, *, mode, out_dtype, tm, tn, tk, name, n_outer=False, acc_in=None, side=None):
    if mode == "nn":
        (M, K), (K2, N) = a.shape, b.shape
    elif mode == "nt":
        (M, K), (N, K2) = a.shape, b.shape
    else:
        (K, M), (K2, N) = a.shape, b.shape
    assert K == K2, (a.shape, b.shape, mode)
    tm, tn, tk = _fit(M, tm), _fit(N, tn), _fit(K, tk)
    assert M % tm == 0 and N % tn == 0 and K % tk == 0, (a.shape, b.shape, tm, tn, tk)
    nk = K // tk
    if n_outer:
        grid = (N // tn, M // tm, nk)
        ij = lambda g0, g1: (g1, g0)
    else:
        grid = (M // tm, N // tn, nk)
        ij = lambda g0, g1: (g0, g1)

    if mode == "nn":
        a_spec = pl.BlockSpec((tm, tk), lambda g0, g1, k: (ij(g0, g1)[0], k))
        b_spec = pl.BlockSpec((tk, tn), lambda g0, g1, k: (k, ij(g0, g1)[1]))
        dot = _dot_nn
    elif mode == "nt":
        a_spec = pl.BlockSpec((tm, tk), lambda g0, g1, k: (ij(g0, g1)[0], k))
        b_spec = pl.BlockSpec((tn, tk), lambda g0, g1, k: (ij(g0, g1)[1], k))
        dot = _dot_nt
    else:
        a_spec = pl.BlockSpec((tk, tm), lambda g0, g1, k: (k, ij(g0, g1)[0]))
        b_spec = pl.BlockSpec((tk, tn), lambda g0, g1, k: (k, ij(g0, g1)[1]))
        dot = _dot_tn
    o_spec = pl.BlockSpec((tm, tn), lambda g0, g1, k: ij(g0, g1))
    has_acc = acc_in is not None
    n_side = len(side[0]) if side is not None else 0
    scatter = side[1] if side is not None else False
    n_in = 2 + int(has_acc)

    def body(*refs):
        a_ref, b_ref = refs[0], refs[1]
        c_ref = refs[2] if has_acc else None
        srcs = refs[n_in:n_in + n_side]
        o_ref = refs[n_in + n_side]
        dsts = refs[n_in + n_side + 1:n_in + 2 * n_side + 1]
        scratch = refs[n_in + 2 * n_side + 1:]
        k = pl.program_id(2)
        if n_side:
            sems = scratch[-3:]
            first = (pl.program_id(0) == 0) & (pl.program_id(1) == 0) & (k == 0)
            last = (pl.program_id(0) == grid[0] - 1) & (pl.program_id(1) == grid[1] - 1) & (k == nk - 1)

            @pl.when(first)
            def _():
                for cp in _xchg_copies(srcs, dsts, *sems, scatter):
                    cp.start()

        p = dot(a_ref[...], b_ref[...])

        def finish(r):
            if has_acc:
                r = r + c_ref[...].astype(F32)
            o_ref[...] = r.astype(out_dtype)

        if nk == 1:
            finish(p)
        else:
            acc = scratch[0]

            @pl.when(k == 0)
            def _():
                acc[...] = p

            @pl.when((k > 0) & (k < nk - 1))
            def _():
                acc[...] += p

            @pl.when(k == nk - 1)
            def _():
                finish(acc[...] + p)

        if n_side:
            @pl.when(last)
            def _():
                for cp in _xchg_copies(srcs, dsts, *sems, scatter):
                    cp.wait()

    hbm = pl.BlockSpec(memory_space=pltpu.HBM)
    in_specs = [a_spec, b_spec]
    args = [a, b]
    if has_acc:
        in_specs.append(o_spec)
        args.append(acc_in)
    out_specs = [o_spec]
    out_shape = [jax.ShapeDtypeStruct((M, N), out_dtype)]
    scratch_shapes = [pltpu.VMEM((tm, tn), F32)] if nk > 1 else []
    if n_side:
        in_specs += [hbm] * n_side
        args += list(side[0])
        out_specs += [hbm] * n_side
        out_shape += _xchg_out_shapes(side[0], scatter)
        scratch_shapes += _xchg_sems(n_side)
        cp = pltpu.CompilerParams(vmem_limit_bytes=VMEM_LIMIT, has_side_effects=True,
                                  dimension_semantics=("arbitrary", "arbitrary", "arbitrary"))
    else:
        cp = _cparams(("parallel", "parallel", "arbitrary"))
    res = pl.pallas_call(
        body, name=name, grid=grid, in_specs=in_specs, out_specs=out_specs, out_shape=out_shape,
        scratch_shapes=scratch_shapes, compiler_params=cp,
    )(*args)
    return (res[0], list(res[1:])) if n_side else res[0]


def _rmsnorm_fwd(x, g, name):
    S, D = x.shape
    T = min(TOK_TILE, S)

    def body(x_ref, g_ref, h_ref, r_ref):
        xv = x_ref[...]
        r = lax.rsqrt(jnp.mean(xv * xv, axis=-1, keepdims=True) + EPS)
        h_ref[...] = (xv * r * g_ref[...]).astype(BF16)
        r_ref[...] = r

    return pl.pallas_call(
        body, name=name, grid=(S // T,),
        in_specs=[pl.BlockSpec((T, D), lambda i: (i, 0)), pl.BlockSpec((1, D), lambda i: (0, 0))],
        out_specs=[pl.BlockSpec((T, D), lambda i: (i, 0)), pl.BlockSpec((T, 1), lambda i: (i, 0))],
        out_shape=[jax.ShapeDtypeStruct((S, D), BF16), jax.ShapeDtypeStruct((S, 1), F32)],
        compiler_params=_cparams(("parallel",)),
    )(x, g)


LANE = 128
ROW_BLK = 64


def _shift_strip(src, sh, cs, nr, residues=range(1, 8)):
    for r in residues:
        sh[r] = src[pl.ds(r, nr), cs]


def _tap(src, sh, cs, o, r0):
    r = o % 8
    if r == 0:
        return src[pl.ds(o + r0, ROW_BLK), cs]
    return sh[r, pl.ds(o - r + r0, ROW_BLK), :]


def _conv_fwd(z, w_dw, b_dw, g_ln, b_ln, D):
    S = z.shape[0]
    T = min(TOK_TILE, S)
    hb = T // HALO

    def body(a_ref, b_ref, zc_ref, ah_ref, bh_ref, w_ref, bdw_ref, gln_ref, bln_ref, u1_ref, pc_ref, u0s, sh):
        i = pl.program_id(0)
        a = a_ref[...].astype(F32)
        b = b_ref[...].astype(F32)
        u0s[pl.ds(HALO, T), :] = a * _sig(b)
        halo = ah_ref[...].astype(F32) * _sig(bh_ref[...].astype(F32))
        u0s[pl.ds(0, HALO), :] = jnp.where(i > 0, halo, 0.0)
        u0s[pl.ds(T + HALO, 8), :] = jnp.zeros((8, D), F32)
        for c0 in range(0, D, LANE):
            cs = pl.ds(c0, LANE)
            _shift_strip(u0s, sh, cs, T + HALO)
            for r0 in range(0, T, ROW_BLK):
                acc = jnp.broadcast_to(bdw_ref[:, cs], (ROW_BLK, LANE))
                for j in range(CONV_W):
                    acc = acc + w_ref[pl.ds(j, 1), cs] * _tap(u0s, sh, cs, HALO - (CONV_W - 1) + j, r0)
                u1_ref[pl.ds(r0, ROW_BLK), cs] = acc
        acc = u1_ref[...]
        mu = jnp.mean(acc, axis=-1, keepdims=True)
        xc = acc - mu
        var = jnp.mean(xc * xc, axis=-1, keepdims=True)
        ln = xc * lax.rsqrt(var + EPS) * gln_ref[...] + bln_ref[...]
        pc_ref[...] = (_silu(ln) * _silu(zc_ref[...].astype(F32))).astype(BF16)

    prev = lambda i: (jnp.maximum(i * hb - 1, 0), 0)
    prev_b = lambda i: (jnp.maximum(i * hb - 1, 0), D // D)
    vec = pl.BlockSpec((1, D), lambda i: (0, 0))
    return pl.pallas_call(
        body, name="conv_fwd", grid=(S // T,),
        in_specs=[pl.BlockSpec((T, D), lambda i: (i, 0)), pl.BlockSpec((T, D), lambda i: (i, 1)),
                  pl.BlockSpec((T, D), lambda i: (i, 2)),
                  pl.BlockSpec((HALO, D), prev), pl.BlockSpec((HALO, D), prev_b),
                  pl.BlockSpec((32, D), lambda i: (0, 0)), vec, vec, vec],
        out_specs=[pl.BlockSpec((T, D), lambda i: (i, 0)), pl.BlockSpec((T, D), lambda i: (i, 0))],
        out_shape=[jax.ShapeDtypeStruct((S, D), F32), jax.ShapeDtypeStruct((S, D), BF16)],
        scratch_shapes=[pltpu.VMEM((T + HALO + 8, D), F32), pltpu.VMEM((8, T + HALO, LANE), F32)],
        compiler_params=_cparams(("parallel",)),
    )(z, z, z, z, z, w_dw, b_dw, g_ln, b_ln)


_QK_RES = sorted({(HALO - (QK_W - 1) + j) % 8 for j in range(QK_W)} - {0})


def _qk_fill(ps, p0_ref, p1_ref, h0_ref, h1_ref, T, D):
    i = pl.program_id(0)
    ps[pl.ds(HALO, T), pl.ds(0, D)] = p0_ref[...].astype(F32)
    ps[pl.ds(HALO, T), pl.ds(D, D)] = p1_ref[...].astype(F32)
    ps[pl.ds(0, HALO), pl.ds(0, D)] = jnp.where(i > 0, h0_ref[...].astype(F32), 0.0)
    ps[pl.ds(0, HALO), pl.ds(D, D)] = jnp.where(i > 0, h1_ref[...].astype(F32), 0.0)
    ps[pl.ds(T + HALO, 8), :] = jnp.zeros((8, 2 * D), F32)


def _qk_conv(ps, sh, w_ref, cs, r0):
    acc = jnp.zeros((ROW_BLK, LANE), F32)
    for j in range(QK_W):
        acc = acc + w_ref[pl.ds(j, 1), cs] * _tap(ps, sh, cs, HALO - (QK_W - 1) + j, r0)
    return acc


def _qk_fwd(z, w_qk, D):
    S = z.shape[0]
    T = min(TOK_TILE, S)
    hb = T // HALO
    scale = (D // N_HEADS) ** -0.5
    W2 = 2 * D

    def body(p0_ref, p1_ref, h0_ref, h1_ref, w_ref, o_ref, ps, sh):
        _qk_fill(ps, p0_ref, p1_ref, h0_ref, h1_ref, T, D)
        for c0 in range(0, W2, LANE):
            cs = pl.ds(c0, LANE)
            _shift_strip(ps, sh, cs, T + HALO, _QK_RES)
            for r0 in range(0, T, ROW_BLK):
                qk = _silu(_qk_conv(ps, sh, w_ref, cs, r0))
                o_ref[pl.ds(r0, ROW_BLK), cs] = (qk * scale if c0 >= D else qk).astype(BF16)

    prev0 = lambda i: (jnp.maximum(i * hb - 1, 0), 3)
    prev1 = lambda i: (jnp.maximum(i * hb - 1, 0), 4)
    return pl.pallas_call(
        body, name="qk_fwd", grid=(S // T,),
        in_specs=[pl.BlockSpec((T, D), lambda i: (i, 3)), pl.BlockSpec((T, D), lambda i: (i, 4)),
                  pl.BlockSpec((HALO, D), prev0), pl.BlockSpec((HALO, D), prev1),
                  pl.BlockSpec((8, W2), lambda i: (0, 0))],
        out_specs=pl.BlockSpec((T, W2), lambda i: (i, 0)),
        out_shape=jax.ShapeDtypeStruct((S, W2), BF16),
        scratch_shapes=[pltpu.VMEM((T + HALO + 8, W2), F32), pltpu.VMEM((8, T + HALO, LANE), F32)],
        compiler_params=_cparams(("parallel",)),
    )(z, z, z, z, w_qk)


def _ml_gates(gc_ref, gr_ref, bifr_ref, bifc_ref, L):
    gc = gc_ref[...] + bifr_ref[...]
    gr = gr_ref[...] + bifc_ref[...]
    ii = lax.broadcasted_iota(jnp.int32, (L, L), 0)
    jj = lax.broadcasted_iota(jnp.int32, (L, L), 1)
    tri = (jj <= ii).astype(BF16)
    triu = (ii <= jj).astype(BF16)
    bcol_all = _tri_left(tri, _logsig(gc))
    brow_all = _tri_right(_logsig(gr), triu)
    return gc, gr, bcol_all, brow_all, ii, jj


def _ml_intra(q, k, b_c, b_r, li_r, m, n_row, ii, jj):
    d = b_c - b_r + li_r
    dm = jnp.where(jj <= ii, d, NEG)
    inter = b_c + m
    m_row = jnp.maximum(inter, jnp.max(dm, axis=1, keepdims=True))
    w_intra = jnp.exp(dm - m_row)
    w_inter = jnp.exp(inter - m_row)
    qk = _dot_nt(q, k)
    s = qk * w_intra
    qn = jnp.sum(q.astype(F32) * n_row, axis=-1, keepdims=True)
    den = jnp.sum(s, axis=-1, keepdims=True) + w_inter * qn
    hdiv = jnp.maximum(jnp.abs(den), jnp.exp(-m_row))
    return qk, w_intra, w_inter, s, qn, den, hdiv, m_row


def _ml_state(b_c, li_c, m, L):
    b_last = b_c[L - 1:L, :]
    g_c = b_last - b_c + li_c
    m_new = jnp.maximum(b_last + m, jnp.max(g_c, axis=0, keepdims=True))
    decay = jnp.exp(b_last + m - m_new)
    wk = jnp.exp(g_c - m_new)
    return m_new, decay, wk


def _mlstm_fwd(qks, z, gcol, grow, bif_row, bif_col, D):
    S = qks.shape[0]
    L = min(ML_CHUNK, S)
    NC = S // L
    H = N_HEADS
    dh = D // H

    def body(q_ref, k_ref, v_ref, gc_ref, gr_ref, bifr_ref, bifc_ref, hm_ref, cst_ref, nst_ref, C, NM):
        c = pl.program_id(0)

        @pl.when(c == 0)
        def _():
            C[...] = jnp.zeros_like(C)
            NM[...] = jnp.zeros_like(NM)

        gc, gr, bcol_all, brow_all, ii, jj = _ml_gates(gc_ref, gr_ref.at[0], bifr_ref, bifc_ref, L)
        for h in range(H):
            hs = pl.ds(h * dh, dh)
            q = q_ref[:, hs]
            k = k_ref[:, hs]
            v = v_ref[:, hs]
            b_c = bcol_all[:, H + h:H + h + 1]
            li_c = gc[:, h:h + 1]
            b_r = brow_all[H + h:H + h + 1, :]
            li_r = gr[h:h + 1, :]
            n_row = NM[h, 0:1, :]
            m = NM[h, 1:2, 0:1]
            Cb = C[h].astype(BF16)
            cst_ref[0, h] = Cb
            nst_ref[0, h] = NM[h]
            qk, w_intra, w_inter, s, qn, den, hdiv, m_row = _ml_intra(q, k, b_c, b_r, li_r, m, n_row, ii, jj)
            num = _dot_nn(s.astype(BF16), v) + w_inter * _dot_nn(q, Cb)
            hm_ref[:, hs] = num / hdiv
            m_new, decay, wk = _ml_state(b_c, li_c, m, L)
            wkk = wk * k.astype(F32)
            C[h] = decay * C[h] + _dot_tn(wkk.astype(BF16), v)
            NM[h, 0:1, :] = decay * n_row + jnp.sum(wkk, axis=0, keepdims=True)
            NM[h, 1:2, :] = jnp.broadcast_to(m_new, (1, dh))

    return pl.pallas_call(
        body, name="mlstm_fwd", grid=(NC,),
        in_specs=[pl.BlockSpec((L, D), lambda c: (c, 0)), pl.BlockSpec((L, D), lambda c: (c, 1)),
                  pl.BlockSpec((L, D), lambda c: (c, 5)),
                  pl.BlockSpec((L, IF_PAD), lambda c: (c, 0)),
                  pl.BlockSpec((1, 16, L), lambda c: (c, 0, 0)),
                  pl.BlockSpec((1, IF_PAD), lambda c: (0, 0)), pl.BlockSpec((16, 1), lambda c: (0, 0))],
        out_specs=[pl.BlockSpec((L, D), lambda c: (c, 0)),
                   pl.BlockSpec((1, H, dh, dh), lambda c: (c, 0, 0, 0)),
                   pl.BlockSpec((1, H, 8, dh), lambda c: (c, 0, 0, 0))],
        out_shape=[jax.ShapeDtypeStruct((S, D), F32), jax.ShapeDtypeStruct((NC, H, dh, dh), BF16),
                   jax.ShapeDtypeStruct((NC, H, 8, dh), F32)],
        scratch_shapes=[pltpu.VMEM((H, dh, dh), F32), pltpu.VMEM((H, 8, dh), F32)],
        compiler_params=_cparams(("arbitrary",)),
    )(qks, qks, z, gcol, grow, bif_row, bif_col)


def _mlstm_bwd(qks, z, gcol, grow, bif_row, bif_col, hm, dhm, cst, nst, dz, D):
    S = qks.shape[0]
    L = min(ML_CHUNK, S)
    NC = S // L
    H = N_HEADS
    dh = D // H

    def body(q_ref, k_ref, v_ref, gc_ref, gr_ref, bifr_ref, bifc_ref, hm_ref, dhm_ref, cst_ref, nst_ref, dz_any,
             dqk_ref, dg_ref, dv_ref, dC, dN):
        del dz_any
        c = pl.program_id(0)

        @pl.when(c == 0)
        def _():
            dC[...] = jnp.zeros_like(dC)
            dN[...] = jnp.zeros_like(dN)

        gc, gr, bcol_all, brow_all, ii, jj = _ml_gates(gc_ref, gr_ref.at[0], bifr_ref, bifc_ref, L)
        eye = ii == jj
        lane = lax.broadcasted_iota(jnp.int32, (L, IF_PAD), 1)
        db_all = jnp.zeros((L, IF_PAD), F32)
        dli_all = jnp.zeros((L, IF_PAD), F32)
        last_row = lax.broadcasted_iota(jnp.int32, (L, 1), 0) == L - 1
        for h in range(H):
            hs = pl.ds(h * dh, dh)
            q = q_ref[:, hs]
            k = k_ref[:, hs]
            v = v_ref[:, hs]
            qf = q.astype(F32)
            kf = k.astype(F32)
            b_c = bcol_all[:, H + h:H + h + 1]
            li_c = gc[:, h:h + 1]
            b_r = brow_all[H + h:H + h + 1, :]
            li_r = gr[h:h + 1, :]
            n_row = nst_ref[0, h, 0:1, :]
            m = nst_ref[0, h, 1:2, 0:1]
            Cb = cst_ref[0, h]
            qk, w_intra, w_inter, s, qn, den, hdiv, m_row = _ml_intra(q, k, b_c, b_r, li_r, m, n_row, ii, jj)
            dh_ = dhm_ref[:, hs]
            hh = hm_ref[:, hs]
            dnum = dh_ / hdiv
            dhdiv = -jnp.sum(dh_ * hh, axis=-1, keepdims=True) / hdiv
            dden = jnp.where(jnp.abs(den) > jnp.exp(-m_row), dhdiv * jnp.sign(den), 0.0)
            dnum_b = dnum.astype(BF16)
            ds = _dot_nt(dnum_b, v) + dden
            s_b = s.astype(BF16)
            dv = _dot_tn(s_b, dnum_b)
            dnC = _dot_nt(dnum_b, Cb)
            dwi = dden * w_inter
            dw_inter = jnp.sum(qf * dnC, axis=-1, keepdims=True) + dden * qn
            dq = w_inter * dnC + dwi * n_row
            dC_out = _dot_tn((w_inter * qf).astype(BF16), dnum_b)
            dn_out = jnp.sum(dwi * qf, axis=0, keepdims=True)
            dqk = (ds * w_intra).astype(BF16)
            dq = dq + _dot_nn(dqk, k)
            dk = _dot_tn(dqk, q)
            dd = ds * s
            rs_c = jnp.sum(dd, axis=1, keepdims=True)
            cs_r = jnp.sum(dd, axis=0, keepdims=True)
            cs_c = jnp.sum(jnp.where(eye, cs_r, 0.0), axis=1, keepdims=True)
            dinter = dw_inter * w_inter
            m_new, decay, wk = _ml_state(b_c, li_c, m, L)
            dCn = dC[h]
            dCn_b = dCn.astype(BF16)
            dn_new = dN[h, 0:1, :]
            ddecay = (jnp.sum(jnp.sum(dCn * Cb.astype(F32), axis=1, keepdims=True), axis=0, keepdims=True)
                      + jnp.sum(dn_new * n_row, axis=1, keepdims=True))
            dwkk = _dot_nt(v, dCn_b) + dn_new
            wkk_b = (wk * kf).astype(BF16)
            dv = dv + _dot_nn(wkk_b, dCn_b)
            dk = dk + wk * dwkk
            dg = jnp.sum(dwkk * kf, axis=-1, keepdims=True) * wk
            db_last = ddecay * decay + jnp.sum(dg, axis=0, keepdims=True)
            dC[h] = decay * dCn + dC_out
            dN[h, 0:1, :] = decay * dn_new + dn_out
            db_c = rs_c - cs_c + dinter - dg + jnp.where(last_row, db_last, 0.0)
            dli_c = cs_c + dg
            db_all = jnp.where(lane == H + h, db_c, db_all)
            dli_all = jnp.where(lane == h, dli_c, dli_all)
            dqk_ref[:, hs] = dq.astype(BF16)
            dqk_ref[:, pl.ds(D + h * dh, dh)] = dk.astype(BF16)
            dv_ref[:, hs] = dv.astype(BF16)
        triu = (ii <= jj).astype(BF16)
        dlf = _tri_left(triu, db_all)
        dg_ref[...] = dli_all + dlf * (1.0 - _sig(gc))

    r = lambda c: NC - 1 - c
    n_in = 12
    return pl.pallas_call(
        body, name="mlstm_bwd", grid=(NC,),
        in_specs=[pl.BlockSpec((L, D), lambda c: (r(c), 0)), pl.BlockSpec((L, D), lambda c: (r(c), 1)),
                  pl.BlockSpec((L, D), lambda c: (r(c), 5)),
                  pl.BlockSpec((L, IF_PAD), lambda c: (r(c), 0)),
                  pl.BlockSpec((1, 16, L), lambda c: (r(c), 0, 0)),
                  pl.BlockSpec((1, IF_PAD), lambda c: (0, 0)), pl.BlockSpec((16, 1), lambda c: (0, 0)),
                  pl.BlockSpec((L, D), lambda c: (r(c), 0)), pl.BlockSpec((L, D), lambda c: (r(c), 0)),
                  pl.BlockSpec((1, H, dh, dh), lambda c: (r(c), 0, 0, 0)),
                  pl.BlockSpec((1, H, 8, dh), lambda c: (r(c), 0, 0, 0)),
                  pl.BlockSpec(memory_space=pl.ANY)],
        out_specs=[pl.BlockSpec((L, 2 * D), lambda c: (r(c), 0)),
                   pl.BlockSpec((L, IF_PAD), lambda c: (r(c), 0)),
                   pl.BlockSpec((L, D), lambda c: (r(c), 5))],
        out_shape=[jax.ShapeDtypeStruct((S, 2 * D), BF16),
                   jax.ShapeDtypeStruct((S, IF_PAD), F32), jax.ShapeDtypeStruct(dz.shape, dz.dtype)],
        input_output_aliases={n_in - 1: 2},
        scratch_shapes=[pltpu.VMEM((H, dh, dh), F32), pltpu.VMEM((H, 8, dh), F32)],
        compiler_params=_cparams(("arbitrary",)),
    )(qks, qks, z, gcol, grow, bif_row, bif_col, hm, dhm, cst, nst, dz)


def _ml_post_fwd(hm, z, g_head, D):
    S = hm.shape[0]
    T = min(TOK_TILE, S)
    dh = D // N_HEADS

    def body(hm_ref, o_ref, zm_ref, g_ref, pm_ref):
        for h in range(N_HEADS):
            hs = pl.ds(h * dh, dh)
            hg = _sig(o_ref[:, hs].astype(F32)) * hm_ref[:, hs]
            rs = lax.rsqrt(jnp.mean(hg * hg, axis=-1, keepdims=True) + EPS)
            pm_ref[:, hs] = (hg * rs * g_ref[:, hs] * _silu(zm_ref[:, hs].astype(F32))).astype(BF16)

    return pl.pallas_call(
        body, name="ml_post_fwd", grid=(S // T,),
        in_specs=[pl.BlockSpec((T, D), lambda i: (i, 0)), pl.BlockSpec((T, D), lambda i: (i, 6)),
                  pl.BlockSpec((T, D), lambda i: (i, 7)), pl.BlockSpec((1, D), lambda i: (0, 0))],
        out_specs=pl.BlockSpec((T, D), lambda i: (i, 0)),
        out_shape=jax.ShapeDtypeStruct((S, D), BF16),
        compiler_params=_cparams(("parallel",)),
    )(hm, z, z, g_head)


def _ml_post_bwd(dpm, hm, z, g_head, dz, D):
    S = hm.shape[0]
    T = min(TOK_TILE, S)
    dh = D // N_HEADS

    def body(dpm_ref, hm_ref, o_ref, zm_ref, g_ref, dz_any, dhm_ref, dg_ref, dozm_ref):
        del dz_any
        do_ref = dozm_ref.at[:, pl.ds(0, D)]
        dzm_ref = dozm_ref.at[:, pl.ds(D, D)]
        i = pl.program_id(0)

        @pl.when(i == 0)
        def _():
            dg_ref[...] = jnp.zeros_like(dg_ref)

        for h in range(N_HEADS):
            hs = pl.ds(h * dh, dh)
            o = o_ref[:, hs].astype(F32)
            zm = zm_ref[:, hs].astype(F32)
            hmv = hm_ref[:, hs]
            dpm_ = dpm_ref[:, hs]
            g = g_ref[:, hs]
            so = _sig(o)
            hg = so * hmv
            rs = lax.rsqrt(jnp.mean(hg * hg, axis=-1, keepdims=True) + EPS)
            xn = hg * rs
            dzm_ref[:, hs] = (dpm_ * xn * g * _dsilu(zm)).astype(BF16)
            dhn = dpm_ * _silu(zm)
            dg_ref[0:1, hs] += jnp.sum(dhn * xn, axis=0, keepdims=True)
            dxn = dhn * g
            dhg = rs * (dxn - xn * jnp.mean(dxn * xn, axis=-1, keepdims=True))
            do_ref[:, hs] = (dhg * hmv * so * (1.0 - so)).astype(BF16)
            dhm_ref[:, hs] = dhg * so

    return pl.pallas_call(
        body, name="ml_post_bwd", grid=(S // T,),
        in_specs=[pl.BlockSpec((T, D), lambda i: (i, 0)), pl.BlockSpec((T, D), lambda i: (i, 0)),
                  pl.BlockSpec((T, D), lambda i: (i, 6)), pl.BlockSpec((T, D), lambda i: (i, 7)),
                  pl.BlockSpec((1, D), lambda i: (0, 0)), pl.BlockSpec(memory_space=pl.ANY)],
        out_specs=[pl.BlockSpec((T, D), lambda i: (i, 0)), pl.BlockSpec((8, D), lambda i: (0, 0)),
                   pl.BlockSpec((T, 2 * D), lambda i: (i, 3))],
        out_shape=[jax.ShapeDtypeStruct((S, D), F32), jax.ShapeDtypeStruct((8, D), F32),
                   jax.ShapeDtypeStruct(dz.shape, dz.dtype)],
        input_output_aliases={5: 2},
        compiler_params=_cparams(("arbitrary",)),
    )(dpm, hm, z, z, g_head, dz)


def _xattn_probs(q, km_h, scale):
    s = _dot_nt(q, km_h) * scale
    e = jnp.exp(s - jnp.max(s, axis=-1, keepdims=True))
    return e / jnp.sum(e, axis=-1, keepdims=True)


def _xattn_fwd(z, kv, D):
    S = z.shape[0]
    M = kv.shape[0]
    T = min(TOK_TILE, S)
    dh = D // N_HEADS
    scale = dh ** -0.5

    def body(q_ref, zx_ref, km_ref, vm_ref, px_ref):
        for h in range(N_HEADS):
            hs = pl.ds(h * dh, dh)
            p = _xattn_probs(q_ref[:, hs], km_ref[:, hs], scale)
            o = _dot_nn(p.astype(BF16), vm_ref[:, hs])
            px_ref[:, hs] = (o * _silu(zx_ref[:, hs].astype(F32))).astype(BF16)

    return pl.pallas_call(
        body, name="xattn_fwd", grid=(S // T,),
        in_specs=[pl.BlockSpec((T, D), lambda i: (i, 8)), pl.BlockSpec((T, D), lambda i: (i, 9)),
                  pl.BlockSpec((M, D), lambda i: (0, 0)), pl.BlockSpec((M, D), lambda i: (0, 1))],
        out_specs=pl.BlockSpec((T, D), lambda i: (i, 0)),
        out_shape=jax.ShapeDtypeStruct((S, D), BF16),
        compiler_params=_cparams(("parallel",)),
    )(z, z, kv, kv)


def _xattn_bwd(dpx, z, kv, dz, D):
    S = z.shape[0]
    M = kv.shape[0]
    T = min(TOK_TILE, S)
    dh = D // N_HEADS
    scale = dh ** -0.5

    def body(dpx_ref, q_ref, zx_ref, km_ref, vm_ref, dz_any, dkv_ref, dqz_ref):
        del dz_any
        i = pl.program_id(0)

        @pl.when(i == 0)
        def _():
            dkv_ref[...] = jnp.zeros_like(dkv_ref)

        for h in range(N_HEADS):
            hs = pl.ds(h * dh, dh)
            q = q_ref[:, hs]
            zx = zx_ref[:, hs].astype(F32)
            dpx_ = dpx_ref[:, hs]
            p = _xattn_probs(q, km_ref[:, hs], scale)
            p_b = p.astype(BF16)
            o = _dot_nn(p_b, vm_ref[:, hs])
            dqz_ref[:, pl.ds(D + h * dh, dh)] = (dpx_ * o * _dsilu(zx)).astype(BF16)
            do = (dpx_ * _silu(zx)).astype(BF16)
            dp = _dot_nt(do, vm_ref[:, hs])
            dsc = (p * (dp - jnp.sum(p * dp, axis=-1, keepdims=True)) * scale).astype(BF16)
            dqz_ref[:, hs] = _dot_nn(dsc, km_ref[:, hs]).astype(BF16)
            dkv_ref[:, hs] += _dot_tn(dsc, q)
            dkv_ref[:, pl.ds(D + h * dh, dh)] += _dot_tn(p_b, do)

    return pl.pallas_call(
        body, name="xattn_bwd", grid=(S // T,),
        in_specs=[pl.BlockSpec((T, D), lambda i: (i, 0)),
                  pl.BlockSpec((T, D), lambda i: (i, 8)), pl.BlockSpec((T, D), lambda i: (i, 9)),
                  pl.BlockSpec((M, D), lambda i: (0, 0)), pl.BlockSpec((M, D), lambda i: (0, 1)),
                  pl.BlockSpec(memory_space=pl.ANY)],
        out_specs=[pl.BlockSpec((M, 2 * D), lambda i: (0, 0)), pl.BlockSpec((T, 2 * D), lambda i: (i, 4))],
        out_shape=[jax.ShapeDtypeStruct((M, 2 * D), F32), jax.ShapeDtypeStruct(dz.shape, dz.dtype)],
        input_output_aliases={5: 1},
        compiler_params=_cparams(("arbitrary",)),
    )(dpx, z, z, kv, kv, dz)


def _col_sum(a):
    S, C = a.shape
    T = min(1024, S)

    def body(a_ref, o_ref):
        @pl.when(pl.program_id(0) == 0)
        def _():
            o_ref[...] = jnp.zeros_like(o_ref)

        o_ref[0:1, :] += jnp.sum(a_ref[...], axis=0, keepdims=True)

    return pl.pallas_call(
        body, name="col_sum", grid=(S // T,),
        in_specs=[pl.BlockSpec((T, C), lambda i: (i, 0))], out_specs=pl.BlockSpec((8, C), lambda i: (0, 0)),
        out_shape=jax.ShapeDtypeStruct((8, C), F32), compiler_params=_cparams(("arbitrary",)),
    )(a)


def _gain_grad(dn, xin, rstd):
    R, D = dn.shape

    def body(dn_ref, x_ref, r_ref, o_ref):
        o_ref[...] = jnp.zeros_like(o_ref)
        o_ref[0:1, :] = jnp.sum(dn_ref[...] * x_ref[...] * r_ref[...], axis=0, keepdims=True)

    return pl.pallas_call(
        body, name="gain_grad", out_shape=jax.ShapeDtypeStruct((8, D), F32),
        compiler_params=_cparams(),
    )(dn, xin, rstd)


def _merge_fwd(z, yc, ym, yx, D):
    S = z.shape[0]
    T = min(TOK_TILE, S)

    def body(g0, g1, g2, yc_ref, ym_ref, yx_ref, o_ref):
        o_ref[...] = (_sig(g0[...].astype(F32)) * yc_ref[...] + _sig(g1[...].astype(F32)) * ym_ref[...]
                      + _sig(g2[...].astype(F32)) * yx_ref[...]).astype(BF16)

    tok = pl.BlockSpec((T, D), lambda i: (i, 0))
    return pl.pallas_call(
        body, name="merge_fwd", grid=(S // T,),
        in_specs=[pl.BlockSpec((T, D), lambda i: (i, 10)), pl.BlockSpec((T, D), lambda i: (i, 11)),
                  pl.BlockSpec((T, D), lambda i: (i, 12)), tok, tok, tok],
        out_specs=tok, out_shape=jax.ShapeDtypeStruct((S, D), BF16),
        compiler_params=_cparams(("parallel",)),
    )(z, z, z, yc, ym, yx)


def _merge_bwd(dmg, z, yc, ym, yx, D):
    S = z.shape[0]
    T = min(TOK_TILE, S)

    def body(dm_ref, g_ref, yc_ref, ym_ref, yx_ref, dyc_ref, dym_ref, dyx_ref, dg_ref):
        j = pl.program_id(1)
        for jj, (y_ref, dy_ref) in enumerate(((yc_ref, dyc_ref), (ym_ref, dym_ref), (yx_ref, dyx_ref))):
            @pl.when(j == jj)
            def _():
                sg = _sig(g_ref[...].astype(F32))
                dm = dm_ref[...]
                dy_ref[...] = (dm * sg).astype(BF16)
                dg_ref[...] = (dm * y_ref[...] * sg * (1.0 - sg)).astype(BF16)

    tok = pl.BlockSpec((T, D), lambda i, j: (i, 0))
    return pl.pallas_call(
        body, name="merge_bwd", grid=(S // T, 3),
        in_specs=[tok, pl.BlockSpec((T, D), lambda i, j: (i, 10 + j)), tok, tok, tok],
        out_specs=[tok, tok, tok, pl.BlockSpec((T, D), lambda i, j: (i, 10 + j))],
        out_shape=[jax.ShapeDtypeStruct((S, D), BF16)] * 3 + [jax.ShapeDtypeStruct((S, 13 * D), BF16)],
        compiler_params=_cparams(("arbitrary", "arbitrary")),
    )(dmg, z, yc, ym, yx)


def _post_loss(r, x, tgt, g_post):
    S, D = r.shape
    T = min(TOK_TILE, S)

    def body(r_ref, x_ref, t_ref, g_ref, dy_ref, dr_ref, dg_ref, loss_ref):
        i = pl.program_id(0)

        @pl.when(i == 0)
        def _():
            dg_ref[...] = jnp.zeros_like(dg_ref)
            loss_ref[...] = jnp.zeros_like(loss_ref)

        rv = r_ref[...]
        g = g_ref[...]
        rs = lax.rsqrt(jnp.mean(rv * rv, axis=-1, keepdims=True) + EPS)
        nrm = rv * rs
        e = x_ref[...] + nrm * g - t_ref[...]
        part = jnp.sum(jnp.sum(e * e, axis=-1, keepdims=True), axis=0, keepdims=True) * (0.5 / D)
        loss_ref[0:1, 0:1] += part
        dy = e * (1.0 / D)
        dy_ref[...] = dy
        dg_ref[0:1, :] += jnp.sum(dy * nrm, axis=0, keepdims=True)
        dn = dy * g
        dr_ref[...] = (rs * (dn - nrm * jnp.mean(dn * nrm, axis=-1, keepdims=True))).astype(BF16)

    tok = pl.BlockSpec((T, D), lambda i: (i, 0))
    return pl.pallas_call(
        body, name="post_loss", grid=(S // T,),
        in_specs=[tok, tok, tok, pl.BlockSpec((1, D), lambda i: (0, 0))],
        out_specs=[tok, tok, pl.BlockSpec((8, D), lambda i: (0, 0)), pl.BlockSpec((8, 128), lambda i: (0, 0))],
        out_shape=[jax.ShapeDtypeStruct((S, D), F32), jax.ShapeDtypeStruct((S, D), BF16),
                   jax.ShapeDtypeStruct((8, D), F32), jax.ShapeDtypeStruct((8, 128), F32)],
        compiler_params=_cparams(("arbitrary",)),
    )(r, x, tgt, g_post)


def _prenorm_bwd(dh, x, rstd, dy, g):
    S, D = x.shape
    T = min(TOK_TILE, S)

    def body(dh_ref, x_ref, r_ref, dy_ref, g_ref, dx_ref, dg_ref):
        i = pl.program_id(0)

        @pl.when(i == 0)
        def _():
            dg_ref[...] = jnp.zeros_like(dg_ref)

        rs = r_ref[...]
        xn = x_ref[...] * rs
        dhv = dh_ref[...]
        dg_ref[0:1, :] += jnp.sum(dhv * xn, axis=0, keepdims=True)
        dxn = dhv * g_ref[...]
        dx_ref[...] = dy_ref[...] + rs * (dxn - xn * jnp.mean(dxn * xn, axis=-1, keepdims=True))

    tok = pl.BlockSpec((T, D), lambda i: (i, 0))
    return pl.pallas_call(
        body, name="prenorm_bwd", grid=(S // T,),
        in_specs=[tok, tok, pl.BlockSpec((T, 1), lambda i: (i, 0)), tok, pl.BlockSpec((1, D), lambda i: (0, 0))],
        out_specs=[tok, pl.BlockSpec((8, D), lambda i: (0, 0))],
        out_shape=[jax.ShapeDtypeStruct((S, D), F32), jax.ShapeDtypeStruct((8, D), F32)],
        compiler_params=_cparams(("arbitrary",)),
    )(dh, x, rstd, dy, g)


def _conv_bwd1(dpc, z, u1, g_ln, b_ln, dz, D):
    S = z.shape[0]
    T = min(TOK_TILE, S)

    def body(dpc_ref, zc_ref, u1_ref, gln_ref, bln_ref, dz_any, du1_ref, st_ref, dzc_ref):
        del dz_any
        i = pl.program_id(0)

        @pl.when(i == 0)
        def _():
            st_ref[...] = jnp.zeros_like(st_ref)

        u1v = u1_ref[...]
        mu = jnp.mean(u1v, axis=-1, keepdims=True)
        xc = u1v - mu
        rs = lax.rsqrt(jnp.mean(xc * xc, axis=-1, keepdims=True) + EPS)
        xh = xc * rs
        g = gln_ref[...]
        ln = xh * g + bln_ref[...]
        zc = zc_ref[...].astype(F32)
        dpc_ = dpc_ref[...]
        dzc_ref[...] = (dpc_ * _silu(ln) * _dsilu(zc)).astype(BF16)
        dln = dpc_ * _silu(zc) * _dsilu(ln)
        st_ref[0:1, :] += jnp.sum(dln * xh, axis=0, keepdims=True)
        st_ref[1:2, :] += jnp.sum(dln, axis=0, keepdims=True)
        dxh = dln * g
        du1 = rs * (dxh - jnp.mean(dxh, axis=-1, keepdims=True) - xh * jnp.mean(dxh * xh, axis=-1, keepdims=True))
        du1_ref[...] = du1
        st_ref[2:3, :] += jnp.sum(du1, axis=0, keepdims=True)

    tok = pl.BlockSpec((T, D), lambda i: (i, 0))
    vec = pl.BlockSpec((1, D), lambda i: (0, 0))
    return pl.pallas_call(
        body, name="conv_bwd1", grid=(S // T,),
        in_specs=[tok, pl.BlockSpec((T, D), lambda i: (i, 2)), tok, vec, vec, pl.BlockSpec(memory_space=pl.ANY)],
        out_specs=[tok, pl.BlockSpec((8, D), lambda i: (0, 0)), pl.BlockSpec((T, D), lambda i: (i, 2))],
        out_shape=[jax.ShapeDtypeStruct((S, D), F32), jax.ShapeDtypeStruct((8, D), F32),
                   jax.ShapeDtypeStruct(dz.shape, dz.dtype)],
        input_output_aliases={5: 2},
        compiler_params=_cparams(("arbitrary",)),
    )(dpc, z, u1, g_ln, b_ln, dz)


def _conv_bwd2(du1, z, w_dw, dz, D):
    S = z.shape[0]
    T = min(TOK_TILE, S)
    hb = T // HALO
    last = S // HALO - 1

    n_steps = S // T

    def body(du_ref, dun_ref, a_ref, b_ref, ah_ref, bh_ref, w_ref, dz_any, dw_ref, dab_ref,
             dus, u0s, shd, shu, dwp):
        del dz_any
        i = pl.program_id(0)

        @pl.when(i == 0)
        def _():
            dwp[...] = jnp.zeros_like(dwp)

        dus[pl.ds(0, T), :] = du_ref[...]
        dus[pl.ds(T, HALO), :] = jnp.where(i < n_steps - 1, dun_ref[...], 0.0)
        dus[pl.ds(T + HALO, 8), :] = jnp.zeros((8, D), F32)
        u0s[pl.ds(HALO, T), :] = a_ref[...].astype(F32) * _sig(b_ref[...].astype(F32))
        halo = ah_ref[...].astype(F32) * _sig(bh_ref[...].astype(F32))
        u0s[pl.ds(0, HALO), :] = jnp.where(i > 0, halo, 0.0)
        u0s[pl.ds(T + HALO, 8), :] = jnp.zeros((8, D), F32)
        for c0 in range(0, D, LANE):
            cs = pl.ds(c0, LANE)
            _shift_strip(dus, shd, cs, T + HALO)
            _shift_strip(u0s, shu, cs, T + HALO)
            for r0 in range(0, T, ROW_BLK):
                rows = pl.ds(r0, ROW_BLK)
                du0 = jnp.zeros((ROW_BLK, LANE), F32)
                for j in range(CONV_W):
                    du0 = du0 + w_ref[pl.ds(j, 1), cs] * _tap(dus, shd, cs, CONV_W - 1 - j, r0)
                a = a_ref[rows, cs].astype(F32)
                sb = _sig(b_ref[rows, cs].astype(F32))
                dab_ref[rows, cs] = (du0 * sb).astype(BF16)
                dab_ref[rows, pl.ds(D + c0, LANE)] = (du0 * a * sb * (1.0 - sb)).astype(BF16)
                d = dus[rows, cs]
                for j in range(CONV_W):
                    prod = d * _tap(u0s, shu, cs, HALO - (CONV_W - 1) + j, r0)
                    part = prod[0:8]
                    for q in range(8, ROW_BLK, 8):
                        part = part + prod[q:q + 8]
                    dwp[pl.ds(8 * j, 8), cs] += part

        @pl.when(i == n_steps - 1)
        def _():
            dw_ref[...] = jnp.zeros_like(dw_ref)
            for j in range(CONV_W):
                dw_ref[pl.ds(j, 1), :] = jnp.sum(dwp[pl.ds(8 * j, 8), :], axis=0, keepdims=True)

    tok = pl.BlockSpec((T, D), lambda i: (i, 0))
    nxt = lambda i: (jnp.minimum((i + 1) * hb, last), 0)
    prev = lambda i: (jnp.maximum(i * hb - 1, 0), 0)
    prev_b = lambda i: (jnp.maximum(i * hb - 1, 0), 1)
    return pl.pallas_call(
        body, name="conv_bwd2", grid=(S // T,),
        in_specs=[tok, pl.BlockSpec((HALO, D), nxt), tok, pl.BlockSpec((T, D), lambda i: (i, 1)),
                  pl.BlockSpec((HALO, D), prev), pl.BlockSpec((HALO, D), prev_b),
                  pl.BlockSpec((32, D), lambda i: (0, 0)), pl.BlockSpec(memory_space=pl.ANY)],
        out_specs=[pl.BlockSpec((32, D), lambda i: (0, 0)), pl.BlockSpec((T, 2 * D), lambda i: (i, 0))],
        out_shape=[jax.ShapeDtypeStruct((32, D), F32), jax.ShapeDtypeStruct(dz.shape, dz.dtype)],
        input_output_aliases={7: 1},
        scratch_shapes=[pltpu.VMEM((T + HALO + 8, D), F32), pltpu.VMEM((T + HALO + 8, D), F32),
                        pltpu.VMEM((8, T + HALO, LANE), F32), pltpu.VMEM((8, T + HALO, LANE), F32),
                        pltpu.VMEM((8 * 32, D), F32)],
        compiler_params=_cparams(("arbitrary",)),
    )(du1, du1, z, z, z, z, w_dw, dz)


def _qk_bwd1(dqks, z, w_qk, D):
    S = z.shape[0]
    T = min(TOK_TILE, S)
    hb = T // HALO
    scale = (D // N_HEADS) ** -0.5
    W2 = 2 * D

    n_steps = S // T

    def body(d_ref, p0_ref, p1_ref, h0_ref, h1_ref, w_ref, dc_ref, dw_ref, ps, sh, dwp):
        i = pl.program_id(0)

        @pl.when(i == 0)
        def _():
            dwp[...] = jnp.zeros_like(dwp)

        _qk_fill(ps, p0_ref, p1_ref, h0_ref, h1_ref, T, D)
        for c0 in range(0, W2, LANE):
            cs = pl.ds(c0, LANE)
            _shift_strip(ps, sh, cs, T + HALO, _QK_RES)
            for r0 in range(0, T, ROW_BLK):
                rows = pl.ds(r0, ROW_BLK)
                dc = d_ref[rows, cs].astype(F32) * _dsilu(_qk_conv(ps, sh, w_ref, cs, r0))
                if c0 >= D:
                    dc = dc * scale
                dc_ref[rows, cs] = dc.astype(BF16)
                for j in range(QK_W):
                    prod = dc * _tap(ps, sh, cs, HALO - (QK_W - 1) + j, r0)
                    part = prod[0:8]
                    for q in range(8, ROW_BLK, 8):
                        part = part + prod[q:q + 8]
                    dwp[pl.ds(8 * j, 8), cs] += part

        @pl.when(i == n_steps - 1)
        def _():
            dw_ref[...] = jnp.zeros_like(dw_ref)
            for j in range(QK_W):
                dw_ref[pl.ds(j, 1), :] = jnp.sum(dwp[pl.ds(8 * j, 8), :], axis=0, keepdims=True)

    prev0 = lambda i: (jnp.maximum(i * hb - 1, 0), 3)
    prev1 = lambda i: (jnp.maximum(i * hb - 1, 0), 4)
    return pl.pallas_call(
        body, name="qk_bwd1", grid=(S // T,),
        in_specs=[pl.BlockSpec((T, W2), lambda i: (i, 0)),
                  pl.BlockSpec((T, D), lambda i: (i, 3)), pl.BlockSpec((T, D), lambda i: (i, 4)),
                  pl.BlockSpec((HALO, D), prev0), pl.BlockSpec((HALO, D), prev1),
                  pl.BlockSpec((8, W2), lambda i: (0, 0))],
        out_specs=[pl.BlockSpec((T, W2), lambda i: (i, 0)), pl.BlockSpec((8, W2), lambda i: (0, 0))],
        out_shape=[jax.ShapeDtypeStruct((S, W2), BF16), jax.ShapeDtypeStruct((8, W2), F32)],
        scratch_shapes=[pltpu.VMEM((T + HALO + 8, W2), F32), pltpu.VMEM((8, T + HALO, LANE), F32),
                        pltpu.VMEM((8 * QK_W, W2), F32)],
        compiler_params=_cparams(("arbitrary",)),
    )(dqks, z, z, z, z, w_qk)


def _qk_bwd2(dc, w_qk, dz, D):
    S = dc.shape[0]
    T = min(TOK_TILE, S)
    hb = T // HALO
    last = S // HALO - 1

    n_steps = S // T

    def body(d_ref, dn_ref, w_ref, dz_any, o_ref, ds, sh):
        del dz_any
        i = pl.program_id(0)
        ds[pl.ds(0, T), :] = d_ref[...].astype(F32)
        ds[pl.ds(T, HALO), :] = jnp.where(i < n_steps - 1, dn_ref[...].astype(F32), 0.0)
        ds[pl.ds(T + HALO, 8), :] = jnp.zeros((8, D), F32)
        for c0 in range(0, D, LANE):
            cs = pl.ds(c0, LANE)
            _shift_strip(ds, sh, cs, T + HALO, range(1, QK_W))
            for r0 in range(0, T, ROW_BLK):
                acc = jnp.zeros((ROW_BLK, LANE), F32)
                for j in range(QK_W):
                    acc = acc + w_ref[pl.ds(j, 1), cs] * _tap(ds, sh, cs, QK_W - 1 - j, r0)
                o_ref[pl.ds(r0, ROW_BLK), cs] = acc.astype(BF16)

    return pl.pallas_call(
        body, name="qk_bwd2", grid=(S // T, 2),
        in_specs=[pl.BlockSpec((T, D), lambda i, j: (i, j)),
                  pl.BlockSpec((HALO, D), lambda i, j: (jnp.minimum((i + 1) * hb, last), j)),
                  pl.BlockSpec((8, D), lambda i, j: (0, j)), pl.BlockSpec(memory_space=pl.ANY)],
        out_specs=pl.BlockSpec((T, D), lambda i, j: (i, 3 + j)),
        out_shape=jax.ShapeDtypeStruct(dz.shape, dz.dtype),
        input_output_aliases={3: 0},
        scratch_shapes=[pltpu.VMEM((T + HALO + 8, D), F32), pltpu.VMEM((8, T + HALO, LANE), F32)],
        compiler_params=_cparams(("arbitrary", "arbitrary")),
    )(dc, dc, w_qk, dz)


TILE_BUDGET = 6 * 1024 * 1024


def _tile_2d(R, C, elem_bytes):
    cands = [t for t in range(BF16_ROWS, R + 1, BF16_ROWS) if R % t == 0 and t * C * elem_bytes <= TILE_BUDGET]
    if cands:
        return max(cands), C
    if R * C * elem_bytes <= TILE_BUDGET or C % LANE:
        return R, C
    cands = [t for t in range(LANE, C + 1, LANE) if C % t == 0 and t * R * elem_bytes <= TILE_BUDGET]
    return R, (max(cands) if cands else LANE)


def _sum_parts(parts, name):
    n, R, C = parts.shape
    tr, tc = _tile_2d(R, C, n * parts.dtype.itemsize + 4)

    def body(p_ref, o_ref):
        g = p_ref[0].astype(F32)
        for s in range(1, n):
            g = g + p_ref[s].astype(F32)
        o_ref[...] = g

    return pl.pallas_call(
        body, name=name, grid=(R // tr, C // tc),
        in_specs=[pl.BlockSpec((n, tr, tc), lambda i, j: (0, i, j))],
        out_specs=pl.BlockSpec((tr, tc), lambda i, j: (i, j)),
        out_shape=jax.ShapeDtypeStruct((R, C), F32),
        compiler_params=_cparams(("parallel", "parallel")),
    )(parts)


def _adamw(parts, w, m, v, name):
    R, C = w.shape
    n = parts.shape[0]
    tr, tc = _tile_2d(R, C, 4 * 7 + n * parts.dtype.itemsize)
    c1 = 1.0 / (1.0 - ADAM_B1 ** ADAM_STEP)
    c2 = 1.0 / (1.0 - ADAM_B2 ** ADAM_STEP)

    def body(p_ref, w_ref, m_ref, v_ref, g_ref, d_ref, nm_ref, nv_ref):
        g = p_ref[0].astype(F32)
        for s in range(1, n):
            g = g + p_ref[s].astype(F32)
        g_ref[...] = g
        mn = ADAM_B1 * m_ref[...] + (1.0 - ADAM_B1) * g
        vn = ADAM_B2 * v_ref[...] + (1.0 - ADAM_B2) * (g * g)
        nm_ref[...] = mn
        nv_ref[...] = vn
        d_ref[...] = -ADAM_LR * ((mn * c1) / (jnp.sqrt(vn * c2) + ADAM_EPS) + ADAM_WD * w_ref[...])

    blk = pl.BlockSpec((tr, tc), lambda i, j: (i, j))
    return pl.pallas_call(
        body, name=name, grid=(R // tr, C // tc),
        in_specs=[pl.BlockSpec((n, tr, tc), lambda i, j: (0, i, j)), blk, blk, blk],
        out_specs=[blk, blk, blk, blk],
        out_shape=[jax.ShapeDtypeStruct((R, C), F32)] * 4,
        compiler_params=_cparams(("parallel", "parallel")),
    )(parts, w, m, v)


def _exchange(operands, scatter, name):
    n = len(operands)

    def body(*refs):
        copies = _xchg_copies(refs[:n], refs[n:2 * n], *refs[2 * n:], scatter)
        for cp in copies:
            cp.start()
        for cp in copies:
            cp.wait()

    hbm = pl.BlockSpec(memory_space=pltpu.HBM)
    return pl.pallas_call(
        body, name=name, in_specs=[hbm] * n, out_specs=[hbm] * n, out_shape=_xchg_out_shapes(operands, scatter),
        scratch_shapes=_xchg_sems(n), compiler_params=pltpu.CompilerParams(has_side_effects=True),
    )(*operands)


def _gather_two_level(blk, name):
    def body(src, out, send_sems, recv_sems, loc_sem):
        x, y, c = lax.axis_index("x"), lax.axis_index("y"), lax.axis_index("c")
        me, sibling = (x, y, c), (x, y, 1 - c)
        chips = [(1 - x, y), (x, 1 - y), (1 - x, 1 - y)]

        def slot(px, py, pc):
            return out.at[4 * px + 2 * py + pc]

        def copy(k, block, to, from_src=False):
            return pltpu.make_async_remote_copy(
                src_ref=src if from_src else slot(*block), dst_ref=slot(*block),
                send_sem=send_sems.at[k], recv_sem=recv_sems.at[k],
                device_id=to, device_id_type=pl.DeviceIdType.MESH)

        mine = pltpu.make_async_copy(src, slot(*me), loc_sem)
        mine.start()
        first = [copy(0, me, sibling, True)] + [copy(1 + j, me, (*chip, c), True) for j, chip in enumerate(chips)]
        for cp in first:
            cp.start()
        passed = [copy(4 + j, (*chip, c), sibling) for j, chip in enumerate(chips)]
        for j, chip in enumerate(chips):
            copy(1 + j, (*chip, c), me).wait_recv()
            passed[j].start()
        copy(0, sibling, me).wait_recv()
        for j, chip in enumerate(chips):
            copy(4 + j, (*chip, 1 - c), me).wait_recv()
        for cp in first + passed:
            cp.wait_send()
        mine.wait()

    hbm = pl.BlockSpec(memory_space=pltpu.HBM)
    return pl.pallas_call(
        body, name=name, in_specs=[hbm], out_specs=hbm,
        out_shape=jax.ShapeDtypeStruct((N_DEV,) + blk.shape, blk.dtype),
        scratch_shapes=[pltpu.SemaphoreType.DMA((7,)), pltpu.SemaphoreType.DMA((7,)), pltpu.SemaphoreType.DMA],
        compiler_params=pltpu.CompilerParams(has_side_effects=True),
    )(blk)


LATE = ('w_conv_out', 'w_ml_out', 'w_xa_out', 'w_out', 'w_mem_kv', 'w_qk_conv', 'w_dw')
ROW_SHARDED = ('w_conv_out', 'w_ml_out', 'w_xa_out', 'w_out')
COL_SHARDED = ('w_in', 'w_mem_kv', 'w_qk_conv', 'w_dw')


def _cols_to_slabs(g):
    R, C8 = g.shape
    return g.reshape(R, N_DEV, C8 // N_DEV).transpose(1, 0, 2)


def _slabs_to_cols(a):
    n, R, C = a.shape
    return a.transpose(1, 0, 2).reshape(R, n * C)


def _assemble(name, slabs):
    if name in ROW_SHARDED:
        return slabs.reshape(slabs.shape[0] * slabs.shape[1], slabs.shape[2])
    return _slabs_to_cols(slabs)


def _to_slabs(name, g, dtype):
    if name in ROW_SHARDED:
        return g.reshape(N_DEV, g.shape[0] // N_DEV, g.shape[1]).astype(dtype)
    return _cols_to_slabs(g).astype(dtype)


def _w_in_rows(D):
    n = (13 * D + N_IF) // N_DEV
    gate_dev, gate_row = (8 * D) // n, (8 * D) % n
    assert gate_row + N_IF <= n
    height = -(-(n + BF16_ROWS - 1) // BF16_ROWS) * BF16_ROWS

    def first(p):
        return n * p - jnp.where(n * p >= 8 * D + N_IF, N_IF, 0)

    def start(p):
        return jnp.minimum((first(p) // BF16_ROWS) * BF16_ROWS, 13 * D - height)

    return n, gate_dev, gate_row, height, first, start


def _local_step(x, mem, tgt, g_pre, wpT, wifT, b_if, b_dw, g_ln, b_ln, g_head, g_mem, g_post, late, dist):
    S, D = x.shape
    L = min(ML_CHUNK, S)
    NC = S // L
    bif_row = jnp.zeros((1, IF_PAD), F32).at[0, :N_IF].set(b_if)
    bif_col = jnp.zeros((16, 1), F32).at[:N_IF, 0].set(b_if)

    h, rstd = _rmsnorm_fwd(x, g_pre, "prenorm_fwd")
    in_proj = functools.partial(_mm, h, wpT, mode="nt", out_dtype=BF16, tm=1024, tn=1024, tk=2048, name="in_proj",
                                n_outer=True)
    if dist:
        z, gathered = in_proj(side=([late[n] for n in LATE], False))
        full = {n: _assemble(n, a) for n, a in zip(LATE, gathered)}
    else:
        z = in_proj()
        full = late
    w_co, w_mo, w_xo, w_out, w_kv = (full[n] for n in LATE[:5])
    w_qk = jnp.zeros((8, 2 * D), F32).at[:QK_W].set(full['w_qk_conv'])
    w_dw = jnp.zeros((32, D), F32).at[:CONV_W].set(full['w_dw'])
    gcol = _mm(h, wifT, mode="nt", out_dtype=F32, tm=1024, tn=IF_PAD, tk=2048, name="gates_col")
    grow = _mm(wifT[:16], h, mode="nt", out_dtype=F32, tm=16, tn=1024, tk=2048, name="gates_row")
    grow = grow.reshape(16, NC, L).transpose(1, 0, 2)

    u1, pc = _conv_fwd(z, w_dw, b_dw, g_ln, b_ln, D)
    qks = _qk_fwd(z, w_qk, D)
    hm, cst, nst = _mlstm_fwd(qks, z, gcol, grow, bif_row, bif_col, D)
    pm = _ml_post_fwd(hm, z, g_head, D)
    memn, rstd_mem = _rmsnorm_fwd(mem, g_mem, "memnorm_fwd")
    kv = _mm(memn, w_kv, mode="nn", out_dtype=BF16, tm=256, tn=1024, tk=2048, name="mem_kv")
    px = _xattn_fwd(z, kv, D)

    proj = functools.partial(_mm, mode="nn", out_dtype=F32, tm=1024, tn=1024, tk=2048)
    yc = proj(pc, w_co, name="conv_out")
    ym = proj(pm, w_mo, name="ml_out")
    yx = proj(px, w_xo, name="xa_out")
    mg = _merge_fwd(z, yc, ym, yx, D)
    r = proj(mg, w_out, name="out_proj")
    dy, dr, dg_post, loss = _post_loss(r, x, tgt, g_post)

    dgrad = functools.partial(_mm, mode="nt", out_dtype=F32, tm=1024, tn=1024, tk=2048)
    wgrad = functools.partial(_mm, mode="tn", out_dtype=F32, tm=1024, tn=1024, tk=2048)
    dmg = dgrad(dr, w_out, name="out_proj_dx")
    dw_out = wgrad(mg, dr, name="out_proj_dw")
    dyc, dym, dyx, dz = _merge_bwd(dmg, z, yc, ym, yx, D)
    dpc = dgrad(dyc, w_co, name="conv_out_dx")
    dw_co = wgrad(pc, dyc, name="conv_out_dw")
    dpm = dgrad(dym, w_mo, name="ml_out_dx")
    dw_mo = wgrad(pm, dym, name="ml_out_dw")
    dpx = dgrad(dyx, w_xo, name="xa_out_dx")
    dw_xo = wgrad(px, dyx, name="xa_out_dw")

    dkv, dz = _xattn_bwd(dpx, z, kv, dz, D)
    dkv_b = dkv.astype(BF16)
    dw_kv = wgrad(memn, dkv_b, name="mem_kv_dw")
    dmemn = _mm(dkv_b, w_kv, mode="nt", out_dtype=F32, tm=256, tn=1024, tk=2048, name="mem_kv_dx")
    dg_mem = _gain_grad(dmemn, mem, rstd_mem)

    du1, cstats, dz = _conv_bwd1(dpc, z, u1, g_ln, b_ln, dz, D)
    dw_dw, dz = _conv_bwd2(du1, z, w_dw, dz, D)

    dhm, dg_head, dz = _ml_post_bwd(dpm, hm, z, g_head, dz, D)
    dqks, dgates, dz = _mlstm_bwd(qks, z, gcol, grow, bif_row, bif_col, hm, dhm, cst, nst, dz, D)
    dc, dw_qk = _qk_bwd1(dqks, z, w_qk, D)
    dz = _qk_bwd2(dc, w_qk, dz, D)

    g = dict(w_conv_out=dw_co, w_ml_out=dw_mo, w_xa_out=dw_xo, w_out=dw_out, w_mem_kv=dw_kv,
             w_qk_conv=dw_qk[:QK_W], w_dw=dw_dw[:CONV_W])
    dgates_b = dgates.astype(BF16)
    in_proj_dw = functools.partial(_mm, dz, h, mode="tn", out_dtype=BF16 if dist else F32, tm=1024, tn=2048, tk=2048,
                                   name="in_proj_dw")
    in_proj_dx = functools.partial(_mm, dz, wpT, mode="nn", out_dtype=F32, tm=512, tn=2048, tk=2048,
                                   name="in_proj_dx")
    parts = {}
    if dist:
        send = [_to_slabs(n, g[n], BF16 if n in LATE[:5] else F32) for n in LATE]
        dwpT, recv = in_proj_dw(side=(send, True))
        parts.update(zip(LATE, recv))
        *_, height, _, start = _w_in_rows(D)
        dh, recv = in_proj_dx(side=([dwpT], _RowWindows(start, height)))
        parts['w_in'] = recv[0]
    else:
        dwpT = in_proj_dw()
        dh = in_proj_dx()
    dwifT = _mm(dgates_b, h, mode="tn", out_dtype=F32, tm=IF_PAD, tn=2048, tk=2048, name="gates_dw")
    if not dist:
        g['w_in'] = jnp.concatenate([dwpT[:8 * D], dwifT[:N_IF], dwpT[8 * D:]], axis=0).T
    dh = _mm(dgates_b, wifT, mode="nn", out_dtype=F32, tm=1024, tn=1024, tk=IF_PAD, name="gates_dx", acc_in=dh)
    dx, dg_pre = _prenorm_bwd(dh, x, rstd, dy, g_pre)

    db_if = _col_sum(dgates)[0, :N_IF]
    g.update(g_pre=dg_pre[0], b_if=db_if, b_dw=cstats[2], g_ln=cstats[0], b_ln=cstats[1], g_ml_head=dg_head[0],
             g_mem=dg_mem[0], g_post=dg_post[0], w_in_gates=dwifT[:N_IF])
    return loss[0, 0], dx, g, parts


WEIGHTS = ('g_pre', 'w_in', 'b_if', 'w_qk_conv', 'w_dw', 'b_dw', 'g_ln', 'b_ln', 'w_conv_out', 'g_ml_head',
           'w_ml_out', 'g_mem', 'w_mem_kv', 'w_xa_out', 'w_out', 'g_post')
VECTORS = ('g_pre', 'b_dw', 'g_ln', 'b_ln', 'g_ml_head', 'g_mem', 'g_post')


def kernel(x, mem, g_pre, w_in, b_if, w_qk_conv, w_dw, b_dw, g_ln, b_ln, w_conv_out, g_ml_head, w_ml_out, g_mem, w_mem_kv, w_xa_out, w_out, g_post, loss_target, m_g_pre, m_w_in, m_b_if, m_w_qk_conv, m_w_dw, m_b_dw, m_g_ln, m_b_ln, m_w_conv_out, m_g_ml_head, m_w_ml_out, m_g_mem, m_w_mem_kv, m_w_xa_out, m_w_out, m_g_post, v_g_pre, v_w_in, v_b_if, v_w_qk_conv, v_w_dw, v_b_dw, v_g_ln, v_b_ln, v_w_conv_out, v_g_ml_head, v_w_ml_out, v_g_mem, v_w_mem_kv, v_w_xa_out, v_w_out, v_g_post):
    S, D = x.shape[1], x.shape[2]
    w = dict(g_pre=g_pre, w_in=w_in, b_if=b_if, w_qk_conv=w_qk_conv, w_dw=w_dw, b_dw=b_dw, g_ln=g_ln, b_ln=b_ln,
             w_conv_out=w_conv_out, g_ml_head=g_ml_head, w_ml_out=w_ml_out, g_mem=g_mem, w_mem_kv=w_mem_kv,
             w_xa_out=w_xa_out, w_out=w_out, g_post=g_post)
    mom = dict(g_pre=m_g_pre, w_in=m_w_in, b_if=m_b_if, w_qk_conv=m_w_qk_conv, w_dw=m_w_dw, b_dw=m_b_dw,
               g_ln=m_g_ln, b_ln=m_b_ln, w_conv_out=m_w_conv_out, g_ml_head=m_g_ml_head, w_ml_out=m_w_ml_out,
               g_mem=m_g_mem, w_mem_kv=m_w_mem_kv, w_xa_out=m_w_xa_out, w_out=m_w_out, g_post=m_g_post)
    var = dict(g_pre=v_g_pre, w_in=v_w_in, b_if=v_b_if, w_qk_conv=v_w_qk_conv, w_dw=v_w_dw, b_dw=v_b_dw,
               g_ln=v_g_ln, b_ln=v_b_ln, w_conv_out=v_w_conv_out, g_ml_head=v_g_ml_head, w_ml_out=v_w_ml_out,
               g_mem=v_g_mem, w_mem_kv=v_w_mem_kv, w_xa_out=v_w_xa_out, w_out=v_w_out, g_post=v_g_post)

    n_rows, gate_dev, gate_row, height, first, start = _w_in_rows(D)
    me = 4 * lax.axis_index("x") + 2 * lax.axis_index("y") + lax.axis_index("c")

    rows = _gather_two_level(w_in.T.astype(BF16), "gather_w_in").reshape(N_DEV * n_rows, D)
    wpT = jnp.concatenate([rows[:8 * D], rows[8 * D + N_IF:]], axis=0)
    wifT = jnp.zeros((IF_PAD, D), BF16).at[:N_IF].set(rows[8 * D:8 * D + N_IF])
    late = {n: w[n].astype(BF16) if n in LATE[:5] else w[n] for n in LATE}
    row = lambda a: a.reshape(1, D)

    loss, dx, g, parts = _local_step(
        x[0], mem[0], loss_target[0], row(g_pre), wpT, wifT, b_if, row(b_dw), row(g_ln), row(b_ln),
        row(g_ml_head), row(g_mem), row(g_post), late, True)

    pad = lambda a: jnp.zeros((D,), F32).at[:N_IF].set(a)
    vec = jnp.concatenate([jnp.stack([g[n] for n in VECTORS] + [pad(g['b_if'])]), g['w_in_gates']])
    parts['vec'] = _exchange([vec], False, "gather_vector_grads")[0]

    out = {}
    for n in LATE:
        out[n] = _adamw(parts[n], w[n], mom[n], var[n], "adamw_" + n)
    zeros8 = jnp.zeros((N_IF, D), F32)
    stack = lambda d: jnp.concatenate([jnp.stack([d[n] for n in VECTORS] + [pad(d['b_if'])]), zeros8])
    vres = _adamw(parts['vec'], stack(w), stack(mom), stack(var), "adamw_vectors")
    for i, n in enumerate(VECTORS):
        out[n] = [r[i] for r in vres]
    out['b_if'] = [r[len(VECTORS), :N_IF] for r in vres]
    gate_grad = vres[0][len(VECTORS) + 1:]

    win = _sum_parts(parts['w_in'], "sum_w_in_grads")
    seg = lax.dynamic_slice(win, (first(me) - start(me), 0), (n_rows, D))
    with_gates = jnp.concatenate([seg[:gate_row], gate_grad, seg[gate_row:n_rows - N_IF]], axis=0)
    g_in = jnp.where(me == gate_dev, with_gates, seg)
    out['w_in'] = [r.T for r in _adamw(g_in[None], w_in.T, m_w_in.T, v_w_in.T, "adamw_w_in")]

    loss = lax.psum(loss, ("x", "y", "c"))
    res = [loss, dx.reshape(1, S, D)]
    for k in range(4):
        res += [out[n][k] for n in WEIGHTS]
    return tuple(res)
```

```python
import functools
import math

import jax
import jax.numpy as jnp
from jax import lax
from jax.experimental import pallas as pl
from jax.experimental.pallas import tpu as pltpu

F32 = jnp.float32
BF16 = jnp.bfloat16

N_DEV = 8
N_HEADS = 4
CONV_W = 31
QK_W = 4
N_IF = 2 * N_HEADS
IF_PAD = 128
EPS = 1e-6
NEG = -1e30

ADAM_LR = 0.001
ADAM_B1 = 0.9
ADAM_B2 = 0.999
ADAM_EPS = 1e-08
ADAM_WD = 0.01
ADAM_STEP = 10

TOK_TILE = 256
ML_CHUNK = 256
HALO = 32
BF16_ROWS = 16
VMEM_LIMIT = 56 * 1024 * 1024


def _cparams(sem=None):
    kw = dict(vmem_limit_bytes=VMEM_LIMIT)
    if sem is not None:
        kw["dimension_semantics"] = sem
    return pltpu.CompilerParams(**kw)


def _sig(x):
    return jax.nn.sigmoid(x)


def _silu(x):
    return x * _sig(x)


def _dsilu(x):
    s = _sig(x)
    return s * (1.0 + x * (1.0 - s))


def _logsig(x):
    return jnp.minimum(x, 0.0) - jnp.log(1.0 + jnp.exp(-jnp.abs(x)))


def _dot(a, b, dims):
    return lax.dot_general(a, b, (dims, ((), ())), preferred_element_type=F32)


def _dot_nn(a, b):
    return _dot(a, b, ((1,), (0,)))


def _dot_nt(a, b):
    return _dot(a, b, ((1,), (1,)))


def _dot_tn(a, b):
    return _dot(a, b, ((0,), (0,)))


def _split3(a):
    hi = a.astype(BF16)
    r1 = a - hi.astype(F32)
    mid = r1.astype(BF16)
    lo = (r1 - mid.astype(F32)).astype(BF16)
    return hi, mid, lo


def _tri_left(tri, a):
    hi, mid, lo = _split3(a)
    return _dot_nn(tri, hi) + _dot_nn(tri, mid) + _dot_nn(tri, lo)


def _tri_right(a, tri):
    hi, mid, lo = _split3(a)
    return _dot_nn(hi, tri) + _dot_nn(mid, tri) + _dot_nn(lo, tri)


def _fit(n, t):
    if n <= t:
        return n
    return max(c for c in range(128, t + 1, 128) if n % c == 0)


def _peer(k):
    x, y, c = lax.axis_index("x"), lax.axis_index("y"), lax.axis_index("c")
    px = 1 - x if (k >> 2) & 1 else x
    py = 1 - y if (k >> 1) & 1 else y
    pc = 1 - c if k & 1 else c
    return (px, py, pc), 4 * px + 2 * py + pc


class _RowWindows:
    def __init__(self, start, height):
        self.start, self.height = start, height


def _xchg_copies(srcs, dsts, send_sems, recv_sems, loc_sems, scatter):
    _, me = _peer(0)

    def piece(src, p):
        if isinstance(scatter, _RowWindows):
            return src.at[pl.ds(pl.multiple_of(scatter.start(p), BF16_ROWS), scatter.height), :]
        return src.at[p] if scatter else src

    copies = []
    for t, (src, dst) in enumerate(zip(srcs, dsts)):
        copies.append(pltpu.make_async_copy(piece(src, me), dst.at[me], loc_sems.at[t]))
        for k in range(1, N_DEV):
            dev, p = _peer(k)
            s = t * (N_DEV - 1) + k - 1
            copies.append(pltpu.make_async_remote_copy(
                src_ref=piece(src, p), dst_ref=dst.at[me],
                send_sem=send_sems.at[s], recv_sem=recv_sems.at[s],
                device_id=dev, device_id_type=pl.DeviceIdType.MESH))
    return copies


def _xchg_out_shapes(operands, scatter):
    def one(a):
        if isinstance(scatter, _RowWindows):
            return (scatter.height, a.shape[1])
        return tuple(a.shape[1:] if scatter else a.shape)
    return [jax.ShapeDtypeStruct((N_DEV,) + one(a), a.dtype) for a in operands]


def _xchg_sems(n):
    return [pltpu.SemaphoreType.DMA((n * (N_DEV - 1),)), pltpu.SemaphoreType.DMA((n * (N_DEV - 1),)),
            pltpu.SemaphoreType.DMA((n,))]


def _mm(a, b, *, mode, out_dtype, tm, tn, tk, name, n_outer=False, acc_in=None, side=None):
    if mode == "nn":
        (M, K), (K2, N) = a.shape, b.shape
    elif mode == "nt":
        (M, K), (N, K2) = a.shape, b.shape
    else:
        (K, M), (K2, N) = a.shape, b.shape
    assert K == K2, (a.shape, b.shape, mode)
    tm, tn, tk = _fit(M, tm), _fit(N, tn), _fit(K, tk)
    assert M % tm == 0 and N % tn == 0 and K % tk == 0, (a.shape, b.shape, tm, tn, tk)
    nk = K // tk
    if n_outer:
        grid = (N // tn, M // tm, nk)
        ij = lambda g0, g1: (g1, g0)
    else:
        grid = (M // tm, N // tn, nk)
        ij = lambda g0, g1: (g0, g1)

    if mode == "nn":
        a_spec = pl.BlockSpec((tm, tk), lambda g0, g1, k: (ij(g0, g1)[0], k))
        b_spec = pl.BlockSpec((tk, tn), lambda g0, g1, k: (k, ij(g0, g1)[1]))
        dot = _dot_nn
    elif mode == "nt":
        a_spec = pl.BlockSpec((tm, tk), lambda g0, g1, k: (ij(g0, g1)[0], k))
        b_spec = pl.BlockSpec((tn, tk), lambda g0, g1, k: (ij(g0, g1)[1], k))
        dot = _dot_nt
    else:
        a_spec = pl.BlockSpec((tk, tm), lambda g0, g1, k: (k, ij(g0, g1)[0]))
        b_spec = pl.BlockSpec((tk, tn), lambda g0, g1, k: (k, ij(g0, g1)[1]))
        dot = _dot_tn
    o_spec = pl.BlockSpec((tm, tn), lambda g0, g1, k: ij(g0, g1))
    has_acc = acc_in is not None
    n_side = len(side[0]) if side is not None else 0
    scatter = side[1] if side is not None else False
    n_in = 2 + int(has_acc)

    def body(*refs):
        a_ref, b_ref = refs[0], refs[1]
        c_ref = refs[2] if has_acc else None
        srcs = refs[n_in:n_in + n_side]
        o_ref = refs[n_in + n_side]
        dsts = refs[n_in + n_side + 1:n_in + 2 * n_side + 1]
        scratch = refs[n_in + 2 * n_side + 1:]
        k = pl.program_id(2)
        if n_side:
            sems = scratch[-3:]
            first = (pl.program_id(0) == 0) & (pl.program_id(1) == 0) & (k == 0)
            last = (pl.program_id(0) == grid[0] - 1) & (pl.program_id(1) == grid[1] - 1) & (k == nk - 1)

            @pl.when(first)
            def _():
                for cp in _xchg_copies(srcs, dsts, *sems, scatter):
                    cp.start()

        p = dot(a_ref[...], b_ref[...])

        def finish(r):
            if has_acc:
                r = r + c_ref[...].astype(F32)
            o_ref[...] = r.astype(out_dtype)

        if nk == 1:
            finish(p)
        else:
            acc = scratch[0]

            @pl.when(k == 0)
            def _():
                acc[...] = p

            @pl.when((k > 0) & (k < nk - 1))
            def _():
                acc[...] += p

            @pl.when(k == nk - 1)
            def _():
                finish(acc[...] + p)

        if n_side:
            @pl.when(last)
            def _():
                for cp in _xchg_copies(srcs, dsts, *sems, scatter):
                    cp.wait()

    hbm = pl.BlockSpec(memory_space=pltpu.HBM)
    in_specs = [a_spec, b_spec]
    args = [a, b]
    if has_acc:
        in_specs.append(o_spec)
        args.append(acc_in)
    out_specs = [o_spec]
    out_shape = [jax.ShapeDtypeStruct((M, N), out_dtype)]
    scratch_shapes = [pltpu.VMEM((tm, tn), F32)] if nk > 1 else []
    if n_side:
        in_specs += [hbm] * n_side
        args += list(side[0])
        out_specs += [hbm] * n_side
        out_shape += _xchg_out_shapes(side[0], scatter)
        scratch_shapes += _xchg_sems(n_side)
        cp = pltpu.CompilerParams(vmem_limit_bytes=VMEM_LIMIT, has_side_effects=True,
                                  dimension_semantics=("arbitrary", "arbitrary", "arbitrary"))
    else:
        cp = _cparams(("parallel", "parallel", "arbitrary"))
    res = pl.pallas_call(
        body, name=name, grid=grid, in_specs=in_specs, out_specs=out_specs, out_shape=out_shape,
        scratch_shapes=scratch_shapes, compiler_params=cp,
    )(*args)
    return (res[0], list(res[1:])) if n_side else res[0]


def _rmsnorm_fwd(x, g, name):
    S, D = x.shape
    T = min(TOK_TILE, S)

    def body(x_ref, g_ref, h_ref, r_ref):
        xv = x_ref[...]
        r = lax.rsqrt(jnp.mean(xv * xv, axis=-1, keepdims=True) + EPS)
        h_ref[...] = (xv * r * g_ref[...]).astype(BF16)
        r_ref[...] = r

    return pl.pallas_call(
        body, name=name, grid=(S // T,),
        in_specs=[pl.BlockSpec((T, D), lambda i: (i, 0)), pl.BlockSpec((1, D), lambda i: (0, 0))],
        out_specs=[pl.BlockSpec((T, D), lambda i: (i, 0)), pl.BlockSpec((T, 1), lambda i: (i, 0))],
        out_shape=[jax.ShapeDtypeStruct((S, D), BF16), jax.ShapeDtypeStruct((S, 1), F32)],
        compiler_params=_cparams(("parallel",)),
    )(x, g)


LANE = 128
ROW_BLK = 64


def _shift_strip(src, sh, cs, nr, residues=range(1, 8)):
    for r in residues:
        sh[r] = src[pl.ds(r, nr), cs]


def _tap(src, sh, cs, o, r0):
    r = o % 8
    if r == 0:
        return src[pl.ds(o + r0, ROW_BLK), cs]
    return sh[r, pl.ds(o - r + r0, ROW_BLK), :]


def _conv_fwd(z, w_dw, b_dw, g_ln, b_ln, D):
    S = z.shape[0]
    T = min(TOK_TILE, S)
    hb = T // HALO

    def body(a_ref, b_ref, zc_ref, ah_ref, bh_ref, w_ref, bdw_ref, gln_ref, bln_ref, u1_ref, pc_ref, u0s, sh):
        i = pl.program_id(0)
        a = a_ref[...].astype(F32)
        b = b_ref[...].astype(F32)
        u0s[pl.ds(HALO, T), :] = a * _sig(b)
        halo = ah_ref[...].astype(F32) * _sig(bh_ref[...].astype(F32))
        u0s[pl.ds(0, HALO), :] = jnp.where(i > 0, halo, 0.0)
        u0s[pl.ds(T + HALO, 8), :] = jnp.zeros((8, D), F32)
        for c0 in range(0, D, LANE):
            cs = pl.ds(c0, LANE)
            _shift_strip(u0s, sh, cs, T + HALO)
            for r0 in range(0, T, ROW_BLK):
                acc = jnp.broadcast_to(bdw_ref[:, cs], (ROW_BLK, LANE))
                for j in range(CONV_W):
                    acc = acc + w_ref[pl.ds(j, 1), cs] * _tap(u0s, sh, cs, HALO - (CONV_W - 1) + j, r0)
                u1_ref[pl.ds(r0, ROW_BLK), cs] = acc
        acc = u1_ref[...]
        mu = jnp.mean(acc, axis=-1, keepdims=True)
        xc = acc - mu
        var = jnp.mean(xc * xc, axis=-1, keepdims=True)
        ln = xc * lax.rsqrt(var + EPS) * gln_ref[...] + bln_ref[...]
        pc_ref[...] = (_silu(ln) * _silu(zc_ref[...].astype(F32))).astype(BF16)

    prev = lambda i: (jnp.maximum(i * hb - 1, 0), 0)
    prev_b = lambda i: (jnp.maximum(i * hb - 1, 0), D // D)
    vec = pl.BlockSpec((1, D), lambda i: (0, 0))
    return pl.pallas_call(
        body, name="conv_fwd", grid=(S // T,),
        in_specs=[pl.BlockSpec((T, D), lambda i: (i, 0)), pl.BlockSpec((T, D), lambda i: (i, 1)),
                  pl.BlockSpec((T, D), lambda i: (i, 2)),
                  pl.BlockSpec((HALO, D), prev), pl.BlockSpec((HALO, D), prev_b),
                  pl.BlockSpec((32, D), lambda i: (0, 0)), vec, vec, vec],
        out_specs=[pl.BlockSpec((T, D), lambda i: (i, 0)), pl.BlockSpec((T, D), lambda i: (i, 0))],
        out_shape=[jax.ShapeDtypeStruct((S, D), F32), jax.ShapeDtypeStruct((S, D), BF16)],
        scratch_shapes=[pltpu.VMEM((T + HALO + 8, D), F32), pltpu.VMEM((8, T + HALO, LANE), F32)],
        compiler_params=_cparams(("parallel",)),
    )(z, z, z, z, z, w_dw, b_dw, g_ln, b_ln)


_QK_RES = sorted({(HALO - (QK_W - 1) + j) % 8 for j in range(QK_W)} - {0})


def _qk_fill(ps, p0_ref, p1_ref, h0_ref, h1_ref, T, D):
    i = pl.program_id(0)
    ps[pl.ds(HALO, T), pl.ds(0, D)] = p0_ref[...].astype(F32)
    ps[pl.ds(HALO, T), pl.ds(D, D)] = p1_ref[...].astype(F32)
    ps[pl.ds(0, HALO), pl.ds(0, D)] = jnp.where(i > 0, h0_ref[...].astype(F32), 0.0)
    ps[pl.ds(0, HALO), pl.ds(D, D)] = jnp.where(i > 0, h1_ref[...].astype(F32), 0.0)
    ps[pl.ds(T + HALO, 8), :] = jnp.zeros((8, 2 * D), F32)


def _qk_conv(ps, sh, w_ref, cs, r0):
    acc = jnp.zeros((ROW_BLK, LANE), F32)
    for j in range(QK_W):
        acc = acc + w_ref[pl.ds(j, 1), cs] * _tap(ps, sh, cs, HALO - (QK_W - 1) + j, r0)
    return acc


def _qk_fwd(z, w_qk, D):
    S = z.shape[0]
    T = min(TOK_TILE, S)
    hb = T // HALO
    scale = (D // N_HEADS) ** -0.5
    W2 = 2 * D

    def body(p0_ref, p1_ref, h0_ref, h1_ref, w_ref, o_ref, ps, sh):
        _qk_fill(ps, p0_ref, p1_ref, h0_ref, h1_ref, T, D)
        for c0 in range(0, W2, LANE):
            cs = pl.ds(c0, LANE)
            _shift_strip(ps, sh, cs, T + HALO, _QK_RES)
            for r0 in range(0, T, ROW_BLK):
                qk = _silu(_qk_conv(ps, sh, w_ref, cs, r0))
                o_ref[pl.ds(r0, ROW_BLK), cs] = (qk * scale if c0 >= D else qk).astype(BF16)

    prev0 = lambda i: (jnp.maximum(i * hb - 1, 0), 3)
    prev1 = lambda i: (jnp.maximum(i * hb - 1, 0), 4)
    return pl.pallas_call(
        body, name="qk_fwd", grid=(S // T,),
        in_specs=[pl.BlockSpec((T, D), lambda i: (i, 3)), pl.BlockSpec((T, D), lambda i: (i, 4)),
                  pl.BlockSpec((HALO, D), prev0), pl.BlockSpec((HALO, D), prev1),
                  pl.BlockSpec((8, W2), lambda i: (0, 0))],
        out_specs=pl.BlockSpec((T, W2), lambda i: (i, 0)),
        out_shape=jax.ShapeDtypeStruct((S, W2), BF16),
        scratch_shapes=[pltpu.VMEM((T + HALO + 8, W2), F32), pltpu.VMEM((8, T + HALO, LANE), F32)],
        compiler_params=_cparams(("parallel",)),
    )(z, z, z, z, w_qk)


def _ml_gates(gc_ref, gr_ref, bifr_ref, bifc_ref, L):
    gc = gc_ref[...] + bifr_ref[...]
    gr = gr_ref[...] + bifc_ref[...]
    ii = lax.broadcasted_iota(jnp.int32, (L, L), 0)
    jj = lax.broadcasted_iota(jnp.int32, (L, L), 1)
    tri = (jj <= ii).astype(BF16)
    triu = (ii <= jj).astype(BF16)
    bcol_all = _tri_left(tri, _logsig(gc))
    brow_all = _tri_right(_logsig(gr), triu)
    return gc, gr, bcol_all, brow_all, ii, jj


def _ml_intra(q, k, b_c, b_r, li_r, m, n_row, ii, jj):
    d = b_c - b_r + li_r
    dm = jnp.where(jj <= ii, d, NEG)
    inter = b_c + m
    m_row = jnp.maximum(inter, jnp.max(dm, axis=1, keepdims=True))
    w_intra = jnp.exp(dm - m_row)
    w_inter = jnp.exp(inter - m_row)
    qk = _dot_nt(q, k)
    s = qk * w_intra
    qn = jnp.sum(q.astype(F32) * n_row, axis=-1, keepdims=True)
    den = jnp.sum(s, axis=-1, keepdims=True) + w_inter * qn
    hdiv = jnp.maximum(jnp.abs(den), jnp.exp(-m_row))
    return qk, w_intra, w_inter, s, qn, den, hdiv, m_row


def _ml_state(b_c, li_c, m, L):
    b_last = b_c[L - 1:L, :]
    g_c = b_last - b_c + li_c
    m_new = jnp.maximum(b_last + m, jnp.max(g_c, axis=0, keepdims=True))
    decay = jnp.exp(b_last + m - m_new)
    wk = jnp.exp(g_c - m_new)
    return m_new, decay, wk


def _mlstm_fwd(qks, z, gcol, grow, bif_row, bif_col, D):
    S = qks.shape[0]
    L = min(ML_CHUNK, S)
    NC = S // L
    H = N_HEADS
    dh = D // H

    def body(q_ref, k_ref, v_ref, gc_ref, gr_ref, bifr_ref, bifc_ref, hm_ref, cst_ref, nst_ref, C, NM):
        c = pl.program_id(0)

        @pl.when(c == 0)
        def _():
            C[...] = jnp.zeros_like(C)
            NM[...] = jnp.zeros_like(NM)

        gc, gr, bcol_all, brow_all, ii, jj = _ml_gates(gc_ref, gr_ref.at[0], bifr_ref, bifc_ref, L)
        for h in range(H):
            hs = pl.ds(h * dh, dh)
            q = q_ref[:, hs]
            k = k_ref[:, hs]
            v = v_ref[:, hs]
            b_c = bcol_all[:, H + h:H + h + 1]
            li_c = gc[:, h:h + 1]
            b_r = brow_all[H + h:H + h + 1, :]
            li_r = gr[h:h + 1, :]
            n_row = NM[h, 0:1, :]
            m = NM[h, 1:2, 0:1]
            Cb = C[h].astype(BF16)
            cst_ref[0, h] = Cb
            nst_ref[0, h] = NM[h]
            qk, w_intra, w_inter, s, qn, den, hdiv, m_row = _ml_intra(q, k, b_c, b_r, li_r, m, n_row, ii, jj)
            num = _dot_nn(s.astype(BF16), v) + w_inter * _dot_nn(q, Cb)
            hm_ref[:, hs] = num / hdiv
            m_new, decay, wk = _ml_state(b_c, li_c, m, L)
            wkk = wk * k.astype(F32)
            C[h] = decay * C[h] + _dot_tn(wkk.astype(BF16), v)
            NM[h, 0:1, :] = decay * n_row + jnp.sum(wkk, axis=0, keepdims=True)
            NM[h, 1:2, :] = jnp.broadcast_to(m_new, (1, dh))

    return pl.pallas_call(
        body, name="mlstm_fwd", grid=(NC,),
        in_specs=[pl.BlockSpec((L, D), lambda c: (c, 0)), pl.BlockSpec((L, D), lambda c: (c, 1)),
                  pl.BlockSpec((L, D), lambda c: (c, 5)),
                  pl.BlockSpec((L, IF_PAD), lambda c: (c, 0)),
                  pl.BlockSpec((1, 16, L), lambda c: (c, 0, 0)),
                  pl.BlockSpec((1, IF_PAD), lambda c: (0, 0)), pl.BlockSpec((16, 1), lambda c: (0, 0))],
        out_specs=[pl.BlockSpec((L, D), lambda c: (c, 0)),
                   pl.BlockSpec((1, H, dh, dh), lambda c: (c, 0, 0, 0)),
                   pl.BlockSpec((1, H, 8, dh), lambda c: (c, 0, 0, 0))],
        out_shape=[jax.ShapeDtypeStruct((S, D), F32), jax.ShapeDtypeStruct((NC, H, dh, dh), BF16),
                   jax.ShapeDtypeStruct((NC, H, 8, dh), F32)],
        scratch_shapes=[pltpu.VMEM((H, dh, dh), F32), pltpu.VMEM((H, 8, dh), F32)],
        compiler_params=_cparams(("arbitrary",)),
    )(qks, qks, z, gcol, grow, bif_row, bif_col)


def _mlstm_bwd(qks, z, gcol, grow, bif_row, bif_col, hm, dhm, cst, nst, dz, D):
    S = qks.shape[0]
    L = min(ML_CHUNK, S)
    NC = S // L
    H = N_HEADS
    dh = D // H

    def body(q_ref, k_ref, v_ref, gc_ref, gr_ref, bifr_ref, bifc_ref, hm_ref, dhm_ref, cst_ref, nst_ref, dz_any,
             dqk_ref, dg_ref, dv_ref, dC, dN):
        del dz_any
        c = pl.program_id(0)

        @pl.when(c == 0)
        def _():
            dC[...] = jnp.zeros_like(dC)
            dN[...] = jnp.zeros_like(dN)

        gc, gr, bcol_all, brow_all, ii, jj = _ml_gates(gc_ref, gr_ref.at[0], bifr_ref, bifc_ref, L)
        eye = ii == jj
        lane = lax.broadcasted_iota(jnp.int32, (L, IF_PAD), 1)
        db_all = jnp.zeros((L, IF_PAD), F32)
        dli_all = jnp.zeros((L, IF_PAD), F32)
        last_row = lax.broadcasted_iota(jnp.int32, (L, 1), 0) == L - 1
        for h in range(H):
            hs = pl.ds(h * dh, dh)
            q = q_ref[:, hs]
            k = k_ref[:, hs]
            v = v_ref[:, hs]
            qf = q.astype(F32)
            kf = k.astype(F32)
            b_c = bcol_all[:, H + h:H + h + 1]
            li_c = gc[:, h:h + 1]
            b_r = brow_all[H + h:H + h + 1, :]
            li_r = gr[h:h + 1, :]
            n_row = nst_ref[0, h, 0:1, :]
            m = nst_ref[0, h, 1:2, 0:1]
            Cb = cst_ref[0, h]
            qk, w_intra, w_inter, s, qn, den, hdiv, m_row = _ml_intra(q, k, b_c, b_r, li_r, m, n_row, ii, jj)
            dh_ = dhm_ref[:, hs]
            hh = hm_ref[:, hs]
            dnum = dh_ / hdiv
            dhdiv = -jnp.sum(dh_ * hh, axis=-1, keepdims=True) / hdiv
            dden = jnp.where(jnp.abs(den) > jnp.exp(-m_row), dhdiv * jnp.sign(den), 0.0)
            dnum_b = dnum.astype(BF16)
            ds = _dot_nt(dnum_b, v) + dden
            s_b = s.astype(BF16)
            dv = _dot_tn(s_b, dnum_b)
            dnC = _dot_nt(dnum_b, Cb)
            dwi = dden * w_inter
            dw_inter = jnp.sum(qf * dnC, axis=-1, keepdims=True) + dden * qn
            dq = w_inter * dnC + dwi * n_row
            dC_out = _dot_tn((w_inter * qf).astype(BF16), dnum_b)
            dn_out = jnp.sum(dwi * qf, axis=0, keepdims=True)
            dqk = (ds * w_intra).astype(BF16)
            dq = dq + _dot_nn(dqk, k)
            dk = _dot_tn(dqk, q)
            dd = ds * s
            rs_c = jnp.sum(dd, axis=1, keepdims=True)
            cs_r = jnp.sum(dd, axis=0, keepdims=True)
            cs_c = jnp.sum(jnp.where(eye, cs_r, 0.0), axis=1, keepdims=True)
            dinter = dw_inter * w_inter
            m_new, decay, wk = _ml_state(b_c, li_c, m, L)
            dCn = dC[h]
            dCn_b = dCn.astype(BF16)
            dn_new = dN[h, 0:1, :]
            ddecay = (jnp.sum(jnp.sum(dCn * Cb.astype(F32), axis=1, keepdims=True), axis=0, keepdims=True)
                      + jnp.sum(dn_new * n_row, axis=1, keepdims=True))
            dwkk = _dot_nt(v, dCn_b) + dn_new
            wkk_b = (wk * kf).astype(BF16)
            dv = dv + _dot_nn(wkk_b, dCn_b)
            dk = dk + wk * dwkk
            dg = jnp.sum(dwkk * kf, axis=-1, keepdims=True) * wk
            db_last = ddecay * decay + jnp.sum(dg, axis=0, keepdims=True)
            dC[h] = decay * dCn + dC_out
            dN[h, 0:1, :] = decay * dn_new + dn_out
            db_c = rs_c - cs_c + dinter - dg + jnp.where(last_row, db_last, 0.0)
            dli_c = cs_c + dg
            db_all = jnp.where(lane == H + h, db_c, db_all)
            dli_all = jnp.where(lane == h, dli_c, dli_all)
            dqk_ref[:, hs] = dq.astype(BF16)
            dqk_ref[:, pl.ds(D + h * dh, dh)] = dk.astype(BF16)
            dv_ref[:, hs] = dv.astype(BF16)
        triu = (ii <= jj).astype(BF16)
        dlf = _tri_left(triu, db_all)
        dg_ref[...] = dli_all + dlf * (1.0 - _sig(gc))

    r = lambda c: NC - 1 - c
    n_in = 12
    return pl.pallas_call(
        body, name="mlstm_bwd", grid=(NC,),
        in_specs=[pl.BlockSpec((L, D), lambda c: (r(c), 0)), pl.BlockSpec((L, D), lambda c: (r(c), 1)),
                  pl.BlockSpec((L, D), lambda c: (r(c), 5)),
                  pl.BlockSpec((L, IF_PAD), lambda c: (r(c), 0)),
                  pl.BlockSpec((1, 16, L), lambda c: (r(c), 0, 0)),
                  pl.BlockSpec((1, IF_PAD), lambda c: (0, 0)), pl.BlockSpec((16, 1), lambda c: (0, 0)),
                  pl.BlockSpec((L, D), lambda c: (r(c), 0)), pl.BlockSpec((L, D), lambda c: (r(c), 0)),
                  pl.BlockSpec((1, H, dh, dh), lambda c: (r(c), 0, 0, 0)),
                  pl.BlockSpec((1, H, 8, dh), lambda c: (r(c), 0, 0, 0)),
                  pl.BlockSpec(memory_space=pl.ANY)],
        out_specs=[pl.BlockSpec((L, 2 * D), lambda c: (r(c), 0)),
                   pl.BlockSpec((L, IF_PAD), lambda c: (r(c), 0)),
                   pl.BlockSpec((L, D), lambda c: (r(c), 5))],
        out_shape=[jax.ShapeDtypeStruct((S, 2 * D), BF16),
                   jax.ShapeDtypeStruct((S, IF_PAD), F32), jax.ShapeDtypeStruct(dz.shape, dz.dtype)],
        input_output_aliases={n_in - 1: 2},
        scratch_shapes=[pltpu.VMEM((H, dh, dh), F32), pltpu.VMEM((H, 8, dh), F32)],
        compiler_params=_cparams(("arbitrary",)),
    )(qks, qks, z, gcol, grow, bif_row, bif_col, hm, dhm, cst, nst, dz)


def _ml_post_fwd(hm, z, g_head, D):
    S = hm.shape[0]
    T = min(TOK_TILE, S)
    dh = D // N_HEADS

    def body(hm_ref, o_ref, zm_ref, g_ref, pm_ref):
        for h in range(N_HEADS):
            hs = pl.ds(h * dh, dh)
            hg = _sig(o_ref[:, hs].astype(F32)) * hm_ref[:, hs]
            rs = lax.rsqrt(jnp.mean(hg * hg, axis=-1, keepdims=True) + EPS)
            pm_ref[:, hs] = (hg * rs * g_ref[:, hs] * _silu(zm_ref[:, hs].astype(F32))).astype(BF16)

    return pl.pallas_call(
        body, name="ml_post_fwd", grid=(S // T,),
        in_specs=[pl.BlockSpec((T, D), lambda i: (i, 0)), pl.BlockSpec((T, D), lambda i: (i, 6)),
                  pl.BlockSpec((T, D), lambda i: (i, 7)), pl.BlockSpec((1, D), lambda i: (0, 0))],
        out_specs=pl.BlockSpec((T, D), lambda i: (i, 0)),
        out_shape=jax.ShapeDtypeStruct((S, D), BF16),
        compiler_params=_cparams(("parallel",)),
    )(hm, z, z, g_head)


def _ml_post_bwd(dpm, hm, z, g_head, dz, D):
    S = hm.shape[0]
    T = min(TOK_TILE, S)
    dh = D // N_HEADS

    def body(dpm_ref, hm_ref, o_ref, zm_ref, g_ref, dz_any, dhm_ref, dg_ref, dozm_ref):
        del dz_any
        do_ref = dozm_ref.at[:, pl.ds(0, D)]
        dzm_ref = dozm_ref.at[:, pl.ds(D, D)]
        i = pl.program_id(0)

        @pl.when(i == 0)
        def _():
            dg_ref[...] = jnp.zeros_like(dg_ref)

        for h in range(N_HEADS):
            hs = pl.ds(h * dh, dh)
            o = o_ref[:, hs].astype(F32)
            zm = zm_ref[:, hs].astype(F32)
            hmv = hm_ref[:, hs]
            dpm_ = dpm_ref[:, hs].astype(F32)
            g = g_ref[:, hs]
            so = _sig(o)
            hg = so * hmv
            rs = lax.rsqrt(jnp.mean(hg * hg, axis=-1, keepdims=True) + EPS)
            xn = hg * rs
            dzm_ref[:, hs] = (dpm_ * xn * g * _dsilu(zm)).astype(BF16)
            dhn = dpm_ * _silu(zm)
            dg_ref[0:1, hs] += jnp.sum(dhn * xn, axis=0, keepdims=True)
            dxn = dhn * g
            dhg = rs * (dxn - xn * jnp.mean(dxn * xn, axis=-1, keepdims=True))
            do_ref[:, hs] = (dhg * hmv * so * (1.0 - so)).astype(BF16)
            dhm_ref[:, hs] = dhg * so

    return pl.pallas_call(
        body, name="ml_post_bwd", grid=(S // T,),
        in_specs=[pl.BlockSpec((T, D), lambda i: (i, 0)), pl.BlockSpec((T, D), lambda i: (i, 0)),
                  pl.BlockSpec((T, D), lambda i: (i, 6)), pl.BlockSpec((T, D), lambda i: (i, 7)),
                  pl.BlockSpec((1, D), lambda i: (0, 0)), pl.BlockSpec(memory_space=pl.ANY)],
        out_specs=[pl.BlockSpec((T, D), lambda i: (i, 0)), pl.BlockSpec((8, D), lambda i: (0, 0)),
                   pl.BlockSpec((T, 2 * D), lambda i: (i, 3))],
        out_shape=[jax.ShapeDtypeStruct((S, D), F32), jax.ShapeDtypeStruct((8, D), F32),
                   jax.ShapeDtypeStruct(dz.shape, dz.dtype)],
        input_output_aliases={5: 2},
        compiler_params=_cparams(("arbitrary",)),
    )(dpm, hm, z, z, g_head, dz)


def _xattn_probs(q, km_h, scale):
    s = _dot_nt(q, km_h) * scale
    e = jnp.exp(s - jnp.max(s, axis=-1, keepdims=True))
    return e / jnp.sum(e, axis=-1, keepdims=True)


def _xattn_fwd(z, kv, D):
    S = z.shape[0]
    M = kv.shape[0]
    T = min(TOK_TILE, S)
    dh = D // N_HEADS
    scale = dh ** -0.5

    def body(q_ref, zx_ref, km_ref, vm_ref, px_ref):
        for h in range(N_HEADS):
            hs = pl.ds(h * dh, dh)
            p = _xattn_probs(q_ref[:, hs], km_ref[:, hs], scale)
            o = _dot_nn(p.astype(BF16), vm_ref[:, hs])
            px_ref[:, hs] = (o * _silu(zx_ref[:, hs].astype(F32))).astype(BF16)

    return pl.pallas_call(
        body, name="xattn_fwd", grid=(S // T,),
        in_specs=[pl.BlockSpec((T, D), lambda i: (i, 8)), pl.BlockSpec((T, D), lambda i: (i, 9)),
                  pl.BlockSpec((M, D), lambda i: (0, 0)), pl.BlockSpec((M, D), lambda i: (0, 1))],
        out_specs=pl.BlockSpec((T, D), lambda i: (i, 0)),
        out_shape=jax.ShapeDtypeStruct((S, D), BF16),
        compiler_params=_cparams(("parallel",)),
    )(z, z, kv, kv)


def _xattn_bwd(dpx, z, kv, dz, D):
    S = z.shape[0]
    M = kv.shape[0]
    T = min(TOK_TILE, S)
    dh = D // N_HEADS
    scale = dh ** -0.5

    def body(dpx_ref, q_ref, zx_ref, km_ref, vm_ref, dz_any, dkv_ref, dqz_ref):
        del dz_any
        i = pl.program_id(0)

        @pl.when(i == 0)
        def _():
            dkv_ref[...] = jnp.zeros_like(dkv_ref)

        for h in range(N_HEADS):
            hs = pl.ds(h * dh, dh)
            q = q_ref[:, hs]
            zx = zx_ref[:, hs].astype(F32)
            dpx_ = dpx_ref[:, hs].astype(F32)
            p = _xattn_probs(q, km_ref[:, hs], scale)
            p_b = p.astype(BF16)
            o = _dot_nn(p_b, vm_ref[:, hs])
            dqz_ref[:, pl.ds(D + h * dh, dh)] = (dpx_ * o * _dsilu(zx)).astype(BF16)
            do = (dpx_ * _silu(zx)).astype(BF16)
            dp = _dot_nt(do, vm_ref[:, hs])
            dsc = (p * (dp - jnp.sum(p * dp, axis=-1, keepdims=True)) * scale).astype(BF16)
            dqz_ref[:, hs] = _dot_nn(dsc, km_ref[:, hs]).astype(BF16)
            dkv_ref[:, hs] += _dot_tn(dsc, q)
            dkv_ref[:, pl.ds(D + h * dh, dh)] += _dot_tn(p_b, do)

    return pl.pallas_call(
        body, name="xattn_bwd", grid=(S // T,),
        in_specs=[pl.BlockSpec((T, D), lambda i: (i, 0)),
                  pl.BlockSpec((T, D), lambda i: (i, 8)), pl.BlockSpec((T, D), lambda i: (i, 9)),
                  pl.BlockSpec((M, D), lambda i: (0, 0)), pl.BlockSpec((M, D), lambda i: (0, 1)),
                  pl.BlockSpec(memory_space=pl.ANY)],
        out_specs=[pl.BlockSpec((M, 2 * D), lambda i: (0, 0)), pl.BlockSpec((T, 2 * D), lambda i: (i, 4))],
        out_shape=[jax.ShapeDtypeStruct((M, 2 * D), F32), jax.ShapeDtypeStruct(dz.shape, dz.dtype)],
        input_output_aliases={5: 1},
        compiler_params=_cparams(("arbitrary",)),
    )(dpx, z, z, kv, kv, dz)


def _col_sum(a):
    S, C = a.shape
    T = min(1024, S)

    def body(a_ref, o_ref):
        @pl.when(pl.program_id(0) == 0)
        def _():
            o_ref[...] = jnp.zeros_like(o_ref)

        o_ref[0:1, :] += jnp.sum(a_ref[...], axis=0, keepdims=True)

    return pl.pallas_call(
        body, name="col_sum", grid=(S // T,),
        in_specs=[pl.BlockSpec((T, C), lambda i: (i, 0))], out_specs=pl.BlockSpec((8, C), lambda i: (0, 0)),
        out_shape=jax.ShapeDtypeStruct((8, C), F32), compiler_params=_cparams(("arbitrary",)),
    )(a)


def _gain_grad(dn, xin, rstd):
    R, D = dn.shape

    def body(dn_ref, x_ref, r_ref, o_ref):
        o_ref[...] = jnp.zeros_like(o_ref)
        o_ref[0:1, :] = jnp.sum(dn_ref[...] * x_ref[...] * r_ref[...], axis=0, keepdims=True)

    return pl.pallas_call(
        body, name="gain_grad", out_shape=jax.ShapeDtypeStruct((8, D), F32),
        compiler_params=_cparams(),
    )(dn, xin, rstd)


def _merge_fwd(z, yc, ym, yx, D):
    S = z.shape[0]
    T = min(TOK_TILE, S)

    def body(g0, g1, g2, yc_ref, ym_ref, yx_ref, o_ref):
        o_ref[...] = (_sig(g0[...].astype(F32)) * yc_ref[...].astype(F32)
                      + _sig(g1[...].astype(F32)) * ym_ref[...].astype(F32)
                      + _sig(g2[...].astype(F32)) * yx_ref[...].astype(F32)).astype(BF16)

    tok = pl.BlockSpec((T, D), lambda i: (i, 0))
    return pl.pallas_call(
        body, name="merge_fwd", grid=(S // T,),
        in_specs=[pl.BlockSpec((T, D), lambda i: (i, 10)), pl.BlockSpec((T, D), lambda i: (i, 11)),
                  pl.BlockSpec((T, D), lambda i: (i, 12)), tok, tok, tok],
        out_specs=tok, out_shape=jax.ShapeDtypeStruct((S, D), BF16),
        compiler_params=_cparams(("parallel",)),
    )(z, z, z, yc, ym, yx)


def _merge_bwd(dmg, z, yc, ym, yx, D):
    S = z.shape[0]
    T = min(TOK_TILE, S)

    def body(dm_ref, g_ref, yc_ref, ym_ref, yx_ref, dyc_ref, dym_ref, dyx_ref, dg_ref):
        j = pl.program_id(1)
        for jj, (y_ref, dy_ref) in enumerate(((yc_ref, dyc_ref), (ym_ref, dym_ref), (yx_ref, dyx_ref))):
            @pl.when(j == jj)
            def _():
                sg = _sig(g_ref[...].astype(F32))
                dm = dm_ref[...].astype(F32)
                dy_ref[...] = (dm * sg).astype(BF16)
                dg_ref[...] = (dm * y_ref[...].astype(F32) * sg * (1.0 - sg)).astype(BF16)

    tok = pl.BlockSpec((T, D), lambda i, j: (i, 0))
    return pl.pallas_call(
        body, name="merge_bwd", grid=(S // T, 3),
        in_specs=[tok, pl.BlockSpec((T, D), lambda i, j: (i, 10 + j)), tok, tok, tok],
        out_specs=[tok, tok, tok, pl.BlockSpec((T, D), lambda i, j: (i, 10 + j))],
        out_shape=[jax.ShapeDtypeStruct((S, D), BF16)] * 3 + [jax.ShapeDtypeStruct((S, 13 * D), BF16)],
        compiler_params=_cparams(("arbitrary", "arbitrary")),
    )(dmg, z, yc, ym, yx)


def _post_loss(r, x, tgt, g_post):
    S, D = r.shape
    T = min(TOK_TILE, S)

    def body(r_ref, x_ref, t_ref, g_ref, dy_ref, dr_ref, dg_ref, loss_ref):
        i = pl.program_id(0)

        @pl.when(i == 0)
        def _():
            dg_ref[...] = jnp.zeros_like(dg_ref)
            loss_ref[...] = jnp.zeros_like(loss_ref)

        rv = r_ref[...]
        g = g_ref[...]
        rs = lax.rsqrt(jnp.mean(rv * rv, axis=-1, keepdims=True) + EPS)
        nrm = rv * rs
        e = x_ref[...] + nrm * g - t_ref[...]
        part = jnp.sum(jnp.sum(e * e, axis=-1, keepdims=True), axis=0, keepdims=True) * (0.5 / D)
        loss_ref[0:1, 0:1] += part
        dy = e * (1.0 / D)
        dy_ref[...] = dy
        dg_ref[0:1, :] += jnp.sum(dy * nrm, axis=0, keepdims=True)
        dn = dy * g
        dr_ref[...] = (rs * (dn - nrm * jnp.mean(dn * nrm, axis=-1, keepdims=True))).astype(BF16)

    tok = pl.BlockSpec((T, D), lambda i: (i, 0))
    return pl.pallas_call(
        body, name="post_loss", grid=(S // T,),
        in_specs=[tok, tok, tok, pl.BlockSpec((1, D), lambda i: (0, 0))],
        out_specs=[tok, tok, pl.BlockSpec((8, D), lambda i: (0, 0)), pl.BlockSpec((8, 128), lambda i: (0, 0))],
        out_shape=[jax.ShapeDtypeStruct((S, D), F32), jax.ShapeDtypeStruct((S, D), BF16),
                   jax.ShapeDtypeStruct((8, D), F32), jax.ShapeDtypeStruct((8, 128), F32)],
        compiler_params=_cparams(("arbitrary",)),
    )(r, x, tgt, g_post)


def _prenorm_bwd(dh, x, rstd, dy, g):
    S, D = x.shape
    T = min(TOK_TILE, S)

    def body(dh_ref, x_ref, r_ref, dy_ref, g_ref, dx_ref, dg_ref):
        i = pl.program_id(0)

        @pl.when(i == 0)
        def _():
            dg_ref[...] = jnp.zeros_like(dg_ref)

        rs = r_ref[...]
        xn = x_ref[...] * rs
        dhv = dh_ref[...]
        dg_ref[0:1, :] += jnp.sum(dhv * xn, axis=0, keepdims=True)
        dxn = dhv * g_ref[...]
        dx_ref[...] = dy_ref[...] + rs * (dxn - xn * jnp.mean(dxn * xn, axis=-1, keepdims=True))

    tok = pl.BlockSpec((T, D), lambda i: (i, 0))
    return pl.pallas_call(
        body, name="prenorm_bwd", grid=(S // T,),
        in_specs=[tok, tok, pl.BlockSpec((T, 1), lambda i: (i, 0)), tok, pl.BlockSpec((1, D), lambda i: (0, 0))],
        out_specs=[tok, pl.BlockSpec((8, D), lambda i: (0, 0))],
        out_shape=[jax.ShapeDtypeStruct((S, D), F32), jax.ShapeDtypeStruct((8, D), F32)],
        compiler_params=_cparams(("arbitrary",)),
    )(dh, x, rstd, dy, g)


def _conv_bwd1(dpc, z, u1, g_ln, b_ln, dz, D):
    S = z.shape[0]
    T = min(TOK_TILE, S)

    def body(dpc_ref, zc_ref, u1_ref, gln_ref, bln_ref, dz_any, du1_ref, st_ref, dzc_ref):
        del dz_any
        i = pl.program_id(0)

        @pl.when(i == 0)
        def _():
            st_ref[...] = jnp.zeros_like(st_ref)

        u1v = u1_ref[...]
        mu = jnp.mean(u1v, axis=-1, keepdims=True)
        xc = u1v - mu
        rs = lax.rsqrt(jnp.mean(xc * xc, axis=-1, keepdims=True) + EPS)
        xh = xc * rs
        g = gln_ref[...]
        ln = xh * g + bln_ref[...]
        zc = zc_ref[...].astype(F32)
        dpc_ = dpc_ref[...].astype(F32)
        dzc_ref[...] = (dpc_ * _silu(ln) * _dsilu(zc)).astype(BF16)
        dln = dpc_ * _silu(zc) * _dsilu(ln)
        st_ref[0:1, :] += jnp.sum(dln * xh, axis=0, keepdims=True)
        st_ref[1:2, :] += jnp.sum(dln, axis=0, keepdims=True)
        dxh = dln * g
        du1 = rs * (dxh - jnp.mean(dxh, axis=-1, keepdims=True) - xh * jnp.mean(dxh * xh, axis=-1, keepdims=True))
        du1_ref[...] = du1
        st_ref[2:3, :] += jnp.sum(du1, axis=0, keepdims=True)

    tok = pl.BlockSpec((T, D), lambda i: (i, 0))
    vec = pl.BlockSpec((1, D), lambda i: (0, 0))
    return pl.pallas_call(
        body, name="conv_bwd1", grid=(S // T,),
        in_specs=[tok, pl.BlockSpec((T, D), lambda i: (i, 2)), tok, vec, vec, pl.BlockSpec(memory_space=pl.ANY)],
        out_specs=[tok, pl.BlockSpec((8, D), lambda i: (0, 0)), pl.BlockSpec((T, D), lambda i: (i, 2))],
        out_shape=[jax.ShapeDtypeStruct((S, D), F32), jax.ShapeDtypeStruct((8, D), F32),
                   jax.ShapeDtypeStruct(dz.shape, dz.dtype)],
        input_output_aliases={5: 2},
        compiler_params=_cparams(("arbitrary",)),
    )(dpc, z, u1, g_ln, b_ln, dz)


def _conv_bwd2(du1, z, w_dw, dz, D):
    S = z.shape[0]
    T = min(TOK_TILE, S)
    hb = T // HALO
    last = S // HALO - 1

    n_steps = S // T

    def body(du_ref, dun_ref, a_ref, b_ref, ah_ref, bh_ref, w_ref, dz_any, dw_ref, dab_ref,
             dus, u0s, shd, shu, dwp):
        del dz_any
        i = pl.program_id(0)

        @pl.when(i == 0)
        def _():
            dwp[...] = jnp.zeros_like(dwp)

        dus[pl.ds(0, T), :] = du_ref[...]
        dus[pl.ds(T, HALO), :] = jnp.where(i < n_steps - 1, dun_ref[...], 0.0)
        dus[pl.ds(T + HALO, 8), :] = jnp.zeros((8, D), F32)
        u0s[pl.ds(HALO, T), :] = a_ref[...].astype(F32) * _sig(b_ref[...].astype(F32))
        halo = ah_ref[...].astype(F32) * _sig(bh_ref[...].astype(F32))
        u0s[pl.ds(0, HALO), :] = jnp.where(i > 0, halo, 0.0)
        u0s[pl.ds(T + HALO, 8), :] = jnp.zeros((8, D), F32)
        for c0 in range(0, D, LANE):
            cs = pl.ds(c0, LANE)
            _shift_strip(dus, shd, cs, T + HALO)
            _shift_strip(u0s, shu, cs, T + HALO)
            for r0 in range(0, T, ROW_BLK):
                rows = pl.ds(r0, ROW_BLK)
                du0 = jnp.zeros((ROW_BLK, LANE), F32)
                for j in range(CONV_W):
                    du0 = du0 + w_ref[pl.ds(j, 1), cs] * _tap(dus, shd, cs, CONV_W - 1 - j, r0)
                a = a_ref[rows, cs].astype(F32)
                sb = _sig(b_ref[rows, cs].astype(F32))
                dab_ref[rows, cs] = (du0 * sb).astype(BF16)
                dab_ref[rows, pl.ds(D + c0, LANE)] = (du0 * a * sb * (1.0 - sb)).astype(BF16)
                d = dus[rows, cs]
                for j in range(CONV_W):
                    prod = d * _tap(u0s, shu, cs, HALO - (CONV_W - 1) + j, r0)
                    part = prod[0:8]
                    for q in range(8, ROW_BLK, 8):
                        part = part + prod[q:q + 8]
                    dwp[pl.ds(8 * j, 8), cs] += part

        @pl.when(i == n_steps - 1)
        def _():
            dw_ref[...] = jnp.zeros_like(dw_ref)
            for j in range(CONV_W):
                dw_ref[pl.ds(j, 1), :] = jnp.sum(dwp[pl.ds(8 * j, 8), :], axis=0, keepdims=True)

    tok = pl.BlockSpec((T, D), lambda i: (i, 0))
    nxt = lambda i: (jnp.minimum((i + 1) * hb, last), 0)
    prev = lambda i: (jnp.maximum(i * hb - 1, 0), 0)
    prev_b = lambda i: (jnp.maximum(i * hb - 1, 0), 1)
    return pl.pallas_call(
        body, name="conv_bwd2", grid=(S // T,),
        in_specs=[tok, pl.BlockSpec((HALO, D), nxt), tok, pl.BlockSpec((T, D), lambda i: (i, 1)),
                  pl.BlockSpec((HALO, D), prev), pl.BlockSpec((HALO, D), prev_b),
                  pl.BlockSpec((32, D), lambda i: (0, 0)), pl.BlockSpec(memory_space=pl.ANY)],
        out_specs=[pl.BlockSpec((32, D), lambda i: (0, 0)), pl.BlockSpec((T, 2 * D), lambda i: (i, 0))],
        out_shape=[jax.ShapeDtypeStruct((32, D), F32), jax.ShapeDtypeStruct(dz.shape, dz.dtype)],
        input_output_aliases={7: 1},
        scratch_shapes=[pltpu.VMEM((T + HALO + 8, D), F32), pltpu.VMEM((T + HALO + 8, D), F32),
                        pltpu.VMEM((8, T + HALO, LANE), F32), pltpu.VMEM((8, T + HALO, LANE), F32),
                        pltpu.VMEM((8 * 32, D), F32)],
        compiler_params=_cparams(("arbitrary",)),
    )(du1, du1, z, z, z, z, w_dw, dz)


def _qk_bwd1(dqks, z, w_qk, D):
    S = z.shape[0]
    T = min(TOK_TILE, S)
    hb = T // HALO
    scale = (D // N_HEADS) ** -0.5
    W2 = 2 * D

    n_steps = S // T

    def body(d_ref, p0_ref, p1_ref, h0_ref, h1_ref, w_ref, dc_ref, dw_ref, ps, sh, dwp):
        i = pl.program_id(0)

        @pl.when(i == 0)
        def _():
            dwp[...] = jnp.zeros_like(dwp)

        _qk_fill(ps, p0_ref, p1_ref, h0_ref, h1_ref, T, D)
        for c0 in range(0, W2, LANE):
            cs = pl.ds(c0, LANE)
            _shift_strip(ps, sh, cs, T + HALO, _QK_RES)
            for r0 in range(0, T, ROW_BLK):
                rows = pl.ds(r0, ROW_BLK)
                dc = d_ref[rows, cs].astype(F32) * _dsilu(_qk_conv(ps, sh, w_ref, cs, r0))
                if c0 >= D:
                    dc = dc * scale
                dc_ref[rows, cs] = dc.astype(BF16)
                for j in range(QK_W):
                    prod = dc * _tap(ps, sh, cs, HALO - (QK_W - 1) + j, r0)
                    part = prod[0:8]
                    for q in range(8, ROW_BLK, 8):
                        part = part + prod[q:q + 8]
                    dwp[pl.ds(8 * j, 8), cs] += part

        @pl.when(i == n_steps - 1)
        def _():
            dw_ref[...] = jnp.zeros_like(dw_ref)
            for j in range(QK_W):
                dw_ref[pl.ds(j, 1), :] = jnp.sum(dwp[pl.ds(8 * j, 8), :], axis=0, keepdims=True)

    prev0 = lambda i: (jnp.maximum(i * hb - 1, 0), 3)
    prev1 = lambda i: (jnp.maximum(i * hb - 1, 0), 4)
    return pl.pallas_call(
        body, name="qk_bwd1", grid=(S // T,),
        in_specs=[pl.BlockSpec((T, W2), lambda i: (i, 0)),
                  pl.BlockSpec((T, D), lambda i: (i, 3)), pl.BlockSpec((T, D), lambda i: (i, 4)),
                  pl.BlockSpec((HALO, D), prev0), pl.BlockSpec((HALO, D), prev1),
                  pl.BlockSpec((8, W2), lambda i: (0, 0))],
        out_specs=[pl.BlockSpec((T, W2), lambda i: (i, 0)), pl.BlockSpec((8, W2), lambda i: (0, 0))],
        out_shape=[jax.ShapeDtypeStruct((S, W2), BF16), jax.ShapeDtypeStruct((8, W2), F32)],
        scratch_shapes=[pltpu.VMEM((T + HALO + 8, W2), F32), pltpu.VMEM((8, T + HALO, LANE), F32),
                        pltpu.VMEM((8 * QK_W, W2), F32)],
        compiler_params=_cparams(("arbitrary",)),
    )(dqks, z, z, z, z, w_qk)


def _qk_bwd2(dc, w_qk, dz, D):
    S = dc.shape[0]
    T = min(TOK_TILE, S)
    hb = T // HALO
    last = S // HALO - 1

    n_steps = S // T

    def body(d_ref, dn_ref, w_ref, dz_any, o_ref, ds, sh):
        del dz_any
        i = pl.program_id(0)
        ds[pl.ds(0, T), :] = d_ref[...].astype(F32)
        ds[pl.ds(T, HALO), :] = jnp.where(i < n_steps - 1, dn_ref[...].astype(F32), 0.0)
        ds[pl.ds(T + HALO, 8), :] = jnp.zeros((8, D), F32)
        for c0 in range(0, D, LANE):
            cs = pl.ds(c0, LANE)
            _shift_strip(ds, sh, cs, T + HALO, range(1, QK_W))
            for r0 in range(0, T, ROW_BLK):
                acc = jnp.zeros((ROW_BLK, LANE), F32)
                for j in range(QK_W):
                    acc = acc + w_ref[pl.ds(j, 1), cs] * _tap(ds, sh, cs, QK_W - 1 - j, r0)
                o_ref[pl.ds(r0, ROW_BLK), cs] = acc.astype(BF16)

    return pl.pallas_call(
        body, name="qk_bwd2", grid=(S // T, 2),
        in_specs=[pl.BlockSpec((T, D), lambda i, j: (i, j)),
                  pl.BlockSpec((HALO, D), lambda i, j: (jnp.minimum((i + 1) * hb, last), j)),
                  pl.BlockSpec((8, D), lambda i, j: (0, j)), pl.BlockSpec(memory_space=pl.ANY)],
        out_specs=pl.BlockSpec((T, D), lambda i, j: (i, 3 + j)),
        out_shape=jax.ShapeDtypeStruct(dz.shape, dz.dtype),
        input_output_aliases={3: 0},
        scratch_shapes=[pltpu.VMEM((T + HALO + 8, D), F32), pltpu.VMEM((8, T + HALO, LANE), F32)],
        compiler_params=_cparams(("arbitrary", "arbitrary")),
    )(dc, dc, w_qk, dz)


TILE_BUDGET = 12 * 1024 * 1024


def _tile_2d(R, C, elem_bytes):
    cands = [t for t in range(BF16_ROWS, R + 1, BF16_ROWS) if R % t == 0 and t * C * elem_bytes <= TILE_BUDGET]
    if cands:
        return max(cands), C
    if R * C * elem_bytes <= TILE_BUDGET or C % LANE:
        return R, C
    cands = [t for t in range(LANE, C + 1, LANE) if C % t == 0 and t * R * elem_bytes <= TILE_BUDGET]
    return R, (max(cands) if cands else LANE)


def _sum_parts(parts, name):
    n, R, C = parts.shape
    tr, tc = _tile_2d(R, C, n * parts.dtype.itemsize + 4)

    def body(p_ref, o_ref):
        g = p_ref[0].astype(F32)
        for s in range(1, n):
            g = g + p_ref[s].astype(F32)
        o_ref[...] = g

    return pl.pallas_call(
        body, name=name, grid=(R // tr, C // tc),
        in_specs=[pl.BlockSpec((n, tr, tc), lambda i, j: (0, i, j))],
        out_specs=pl.BlockSpec((tr, tc), lambda i, j: (i, j)),
        out_shape=jax.ShapeDtypeStruct((R, C), F32),
        compiler_params=_cparams(("parallel", "parallel")),
    )(parts)


def _adamw(parts, w, m, v, name):
    R, C = w.shape
    n = parts.shape[0]
    tr, tc = _tile_2d(R, C, 4 * 7 + n * parts.dtype.itemsize)
    c1 = 1.0 / (1.0 - ADAM_B1 ** ADAM_STEP)
    c2 = 1.0 / (1.0 - ADAM_B2 ** ADAM_STEP)

    def body(p_ref, w_ref, m_ref, v_ref, g_ref, d_ref, nm_ref, nv_ref):
        g = p_ref[0].astype(F32)
        for s in range(1, n):
            g = g + p_ref[s].astype(F32)
        g_ref[...] = g
        mn = ADAM_B1 * m_ref[...] + (1.0 - ADAM_B1) * g
        vn = ADAM_B2 * v_ref[...] + (1.0 - ADAM_B2) * (g * g)
        nm_ref[...] = mn
        nv_ref[...] = vn
        d_ref[...] = -ADAM_LR * ((mn * c1) / (jnp.sqrt(vn * c2) + ADAM_EPS) + ADAM_WD * w_ref[...])

    blk = pl.BlockSpec((tr, tc), lambda i, j: (i, j))
    return pl.pallas_call(
        body, name=name, grid=(R // tr, C // tc),
        in_specs=[pl.BlockSpec((n, tr, tc), lambda i, j: (0, i, j)), blk, blk, blk],
        out_specs=[blk, blk, blk, blk],
        out_shape=[jax.ShapeDtypeStruct((R, C), F32)] * 4,
        compiler_params=_cparams(("parallel", "parallel")),
    )(parts, w, m, v)


def _exchange(operands, scatter, name):
    n = len(operands)

    def body(*refs):
        copies = _xchg_copies(refs[:n], refs[n:2 * n], *refs[2 * n:], scatter)
        for cp in copies:
            cp.start()
        for cp in copies:
            cp.wait()

    hbm = pl.BlockSpec(memory_space=pltpu.HBM)
    return pl.pallas_call(
        body, name=name, in_specs=[hbm] * n, out_specs=[hbm] * n, out_shape=_xchg_out_shapes(operands, scatter),
        scratch_shapes=_xchg_sems(n), compiler_params=pltpu.CompilerParams(has_side_effects=True),
    )(*operands)


N_CHIP = 4


class _CoreGather:
    def __init__(self, core):
        self.core = core

    def out_shape(self, blk):
        return jax.ShapeDtypeStruct((N_CHIP,) + blk.shape, blk.dtype)

    @staticmethod
    def sems():
        return [pltpu.SemaphoreType.DMA((7,)), pltpu.SemaphoreType.DMA((7,)), pltpu.SemaphoreType.DMA]

    def _parts(self, src, out, send_sems, recv_sems, loc_sem):
        x, y, c = lax.axis_index("x"), lax.axis_index("y"), lax.axis_index("c")
        me, sibling = (x, y, c), (x, y, 1 - c)
        chips = [(1 - x, y), (x, 1 - y), (1 - x, 1 - y)]

        def copy(k, chip, to, from_src=False):
            slot = out.at[2 * chip[0] + chip[1]]
            return pltpu.make_async_remote_copy(
                src_ref=src if from_src else slot, dst_ref=slot, send_sem=send_sems.at[k], recv_sem=recv_sems.at[k],
                device_id=to, device_id_type=pl.DeviceIdType.MESH)

        mine = pltpu.make_async_copy(src, out.at[2 * x + y], loc_sem)
        first = [copy(0, (x, y), sibling, True)] + [copy(1 + j, (x, y), (*ch, c), True) for j, ch in enumerate(chips)]
        landed = [copy(1 + j, ch, me) for j, ch in enumerate(chips)]
        passed = [copy(4 + j, ch, sibling) for j, ch in enumerate(chips)]
        from_sibling = [copy(0, (x, y), me)] + [copy(4 + j, ch, me) for j, ch in enumerate(chips)]
        return c == self.core, mine, first, landed, passed, from_sibling

    def start(self, *refs):
        holder, mine, first, _, _, _ = self._parts(*refs)

        @pl.when(holder)
        def _():
            mine.start()
            for cp in first:
                cp.start()

    def forward(self, *refs):
        holder, _, _, landed, passed, _ = self._parts(*refs)

        @pl.when(holder)
        def _():
            for got, cp in zip(landed, passed):
                got.wait_recv()
                cp.start()

    def finish(self, *refs):
        holder, mine, first, _, passed, from_sibling = self._parts(*refs)

        @pl.when(holder)
        def _():
            for cp in first + passed:
                cp.wait_send()
            mine.wait()

        @pl.when(jnp.logical_not(holder))
        def _():
            for cp in from_sibling:
                cp.wait_recv()


def _core_gather(blk, core, name):
    g = _CoreGather(core)

    def body(*refs):
        g.start(*refs)
        g.forward(*refs)
        g.finish(*refs)

    hbm = pl.BlockSpec(memory_space=pltpu.HBM)
    return pl.pallas_call(
        body, name=name, in_specs=[hbm], out_specs=hbm, out_shape=g.out_shape(blk), scratch_shapes=g.sems(),
        compiler_params=pltpu.CompilerParams(has_side_effects=True),
    )(blk)


IN_TILE = 512


def _tile_index(tiles):
    jumps = [(i, tiles[i] - tiles[i - 1] - 1) for i in range(1, len(tiles)) if tiles[i] != tiles[i - 1] + 1]

    def f(t):
        j = t + tiles[0]
        for i, gap in jumps:
            j = j + jnp.where(t >= i, gap, 0)
        return j

    return f


def _in_proj_phase(h, w_rows, tiles, z_prev, name, gather=None, xchg=None):
    S, D = h.shape
    nt = len(tiles)
    assert w_rows.shape == (nt * IN_TILE, D)
    tm = _fit(S, 2048)
    grid = (nt, S // tm)
    col = _tile_index(tiles)
    n_in = 2 + (z_prev is not None)
    g_ops = [gather[0]] if gather else []
    x_ops = list(xchg[0]) if xchg else []
    n_g, n_x = len(g_ops), len(x_ops)

    def body(*refs):
        h_ref, w_ref = refs[0], refs[1]
        g_src = refs[n_in:n_in + n_g]
        x_src = refs[n_in + n_g:n_in + n_g + n_x]
        o = n_in + n_g + n_x
        z_ref = refs[o]
        g_dst = refs[o + 1:o + 1 + n_g]
        x_dst = refs[o + 1 + n_g:o + 1 + n_g + n_x]
        sems = refs[o + 1 + n_g + n_x:]
        g_sems, x_sems = sems[:3 * n_g], sems[3 * n_g:]
        t, i = pl.program_id(0), pl.program_id(1)
        first = (t == 0) & (i == 0)
        last = (t == nt - 1) & (i == grid[1] - 1)

        @pl.when(first)
        def _():
            if gather:
                gather[1].start(g_src[0], g_dst[0], *g_sems)
            if xchg:
                for cp in _xchg_copies(x_src, x_dst, *x_sems, xchg[1]):
                    cp.start()

        z_ref[...] = _dot_nt(h_ref[...], w_ref[...]).astype(z_ref.dtype)

        if gather:
            @pl.when((t == (3 * nt) // 4) & (i == 0))
            def _():
                gather[1].forward(g_src[0], g_dst[0], *g_sems)

        @pl.when(last)
        def _():
            if gather:
                gather[1].finish(g_src[0], g_dst[0], *g_sems)
            if xchg:
                for cp in _xchg_copies(x_src, x_dst, *x_sems, xchg[1]):
                    cp.wait()

    hbm = pl.BlockSpec(memory_space=pltpu.HBM)
    in_specs = [pl.BlockSpec((tm, D), lambda t, i: (i, 0)), pl.BlockSpec((IN_TILE, D), lambda t, i: (t, 0))]
    args = [h, w_rows]
    aliases = {}
    if z_prev is not None:
        in_specs.append(pl.BlockSpec(memory_space=pl.ANY))
        args.append(z_prev)
        aliases = {2: 0}
    in_specs += [hbm] * (n_g + n_x)
    args += g_ops + x_ops
    out_specs = [pl.BlockSpec((tm, IN_TILE), lambda t, i: (i, col(t)))] + [hbm] * (n_g + n_x)
    out_shape = [jax.ShapeDtypeStruct((S, 13 * D), BF16)]
    scratch = []
    if gather:
        out_shape.append(gather[1].out_shape(gather[0]))
        scratch += gather[1].sems()
    if xchg:
        out_shape += _xchg_out_shapes(x_ops, xchg[1])
        scratch += _xchg_sems(n_x)
    comm = bool(gather or xchg)
    res = pl.pallas_call(
        body, name=name, grid=grid, in_specs=in_specs, out_specs=out_specs, out_shape=out_shape,
        input_output_aliases=aliases, scratch_shapes=scratch,
        compiler_params=pltpu.CompilerParams(vmem_limit_bytes=VMEM_LIMIT, has_side_effects=comm,
                                             dimension_semantics=("arbitrary", "arbitrary")),
    )(*args)
    z = res[0]
    g_out = res[1] if gather else None
    x_out = list(res[1 + n_g:]) if xchg else None
    return z, g_out, x_out


LATE = ('w_conv_out', 'w_ml_out', 'w_xa_out', 'w_out', 'w_mem_kv', 'w_qk_conv', 'w_dw')
ROW_SHARDED = ('w_conv_out', 'w_ml_out', 'w_xa_out', 'w_out')
COL_SHARDED = ('w_in', 'w_mem_kv', 'w_qk_conv', 'w_dw')


def _cols_to_slabs(g):
    R, C8 = g.shape
    return g.reshape(R, N_DEV, C8 // N_DEV).transpose(1, 0, 2)


def _slabs_to_cols(a):
    n, R, C = a.shape
    return a.transpose(1, 0, 2).reshape(R, n * C)


def _assemble(name, slabs):
    if name in ROW_SHARDED:
        return slabs.reshape(slabs.shape[0] * slabs.shape[1], slabs.shape[2])
    return _slabs_to_cols(slabs)


def _to_slabs(name, g, dtype):
    if name in ROW_SHARDED:
        return g.reshape(N_DEV, g.shape[0] // N_DEV, g.shape[1]).astype(dtype)
    return _cols_to_slabs(g).astype(dtype)


def _w_in_rows(D):
    n = (13 * D + N_IF) // N_DEV
    gate_dev, gate_row = (8 * D) // n, (8 * D) % n
    assert gate_row + N_IF <= n
    height = -(-(n + BF16_ROWS - 1) // BF16_ROWS) * BF16_ROWS

    def first(p):
        return n * p - jnp.where(n * p >= 8 * D + N_IF, N_IF, 0)

    def start(p):
        return jnp.minimum((first(p) // BF16_ROWS) * BF16_ROWS, 13 * D - height)

    return n, gate_dev, gate_row, height, first, start


def _w_in_tiles(D):
    n, gate_dev, *_ = _w_in_rows(D)

    def rows(s):
        f = n * s - (N_IF if n * s >= 8 * D + N_IF else 0)
        return f, f + n - (N_IF if s == gate_dev else 0)

    tiles = range(13 * D // IN_TILE)
    inside = lambda j, s: rows(s)[0] <= j * IN_TILE and (j + 1) * IN_TILE <= rows(s)[1]
    early = [j for j in tiles if any(inside(j, s) for s in range(0, N_DEV, 2))]
    return early, [j for j in tiles if j not in early], rows


def _w_rows_of_tiles(tiles, blocks, D):
    n, gate_dev, gate_row, *_ = _w_in_rows(D)
    _, _, rows = _w_in_tiles(D)
    pieces = []
    for j in tiles:
        r0, r1 = j * IN_TILE, (j + 1) * IN_TILE
        for s in range(N_DEV):
            f, e = rows(s)
            for a, b, shift in ((f, min(e, 8 * D), 0), (max(f, 8 * D), e, N_IF if s == gate_dev else 0)):
                lo, hi = max(a, r0), min(b, r1)
                if lo < hi:
                    pieces.append(blocks[s][lo - f + shift:hi - f + shift])
    merged = jnp.concatenate(pieces, axis=0)
    assert merged.shape[0] == len(tiles) * IN_TILE
    return merged


def _interleave_tiles(parts):
    where = {j: (rows, i) for rows, tiles in parts for i, j in enumerate(tiles)}
    out, j, n_tiles = [], 0, len(where)
    while j < n_tiles:
        rows, i = where[j]
        k = 1
        while j + k in where and where[j + k][0] is rows and where[j + k][1] == i + k:
            k += 1
        out.append(rows[i * IN_TILE:(i + k) * IN_TILE])
        j += k
    return jnp.concatenate(out, axis=0)


def _local_step(x, mem, tgt, g_pre, w_early, w_rest, wifT, b_if, b_dw, g_ln, b_ln, g_head, g_mem, g_post, late, dist):
    S, D = x.shape
    L = min(ML_CHUNK, S)
    NC = S // L
    bif_row = jnp.zeros((1, IF_PAD), F32).at[0, :N_IF].set(b_if)
    bif_col = jnp.zeros((16, 1), F32).at[:N_IF, 0].set(b_if)
    early, rest, _ = _w_in_tiles(D)

    h, rstd = _rmsnorm_fwd(x, g_pre, "prenorm_fwd")
    if dist:
        my_block, rows_of_rest = w_rest
        z, north, _ = _in_proj_phase(h, w_early, early, None, "in_proj_early", gather=(my_block, _CoreGather(1)))
        w_rest = rows_of_rest(north)
        z, _, gathered = _in_proj_phase(h, w_rest, rest, z, "in_proj_rest", xchg=([late[n] for n in LATE], False))
        full = {n: _assemble(n, a) for n, a in zip(LATE, gathered)}
    else:
        z, _, _ = _in_proj_phase(h, w_early, early, None, "in_proj_early")
        z, _, _ = _in_proj_phase(h, w_rest, rest, z, "in_proj_rest")
        full = late
    wpT = _interleave_tiles([(w_early, early), (w_rest, rest)])
    w_co, w_mo, w_xo, w_out, w_kv = (full[n] for n in LATE[:5])
    w_qk = jnp.zeros((8, 2 * D), F32).at[:QK_W].set(full['w_qk_conv'])
    w_dw = jnp.zeros((32, D), F32).at[:CONV_W].set(full['w_dw'])
    gcol = _mm(h, wifT, mode="nt", out_dtype=F32, tm=1024, tn=IF_PAD, tk=2048, name="gates_col")
    grow = _mm(wifT[:16], h, mode="nt", out_dtype=F32, tm=16, tn=1024, tk=2048, name="gates_row")
    grow = grow.reshape(16, NC, L).transpose(1, 0, 2)

    u1, pc = _conv_fwd(z, w_dw, b_dw, g_ln, b_ln, D)
    qks = _qk_fwd(z, w_qk, D)
    hm, cst, nst = _mlstm_fwd(qks, z, gcol, grow, bif_row, bif_col, D)
    pm = _ml_post_fwd(hm, z, g_head, D)
    memn, rstd_mem = _rmsnorm_fwd(mem, g_mem, "memnorm_fwd")
    kv = _mm(memn, w_kv, mode="nn", out_dtype=BF16, tm=256, tn=1024, tk=2048, name="mem_kv")
    px = _xattn_fwd(z, kv, D)

    proj = functools.partial(_mm, mode="nn", tm=1024, tn=1024, tk=2048)
    yc = proj(pc, w_co, out_dtype=BF16, name="conv_out")
    ym = proj(pm, w_mo, out_dtype=BF16, name="ml_out")
    yx = proj(px, w_xo, out_dtype=BF16, name="xa_out")
    mg = _merge_fwd(z, yc, ym, yx, D)
    r = proj(mg, w_out, out_dtype=F32, name="out_proj")
    dy, dr, dg_post, loss = _post_loss(r, x, tgt, g_post)

    dgrad = functools.partial(_mm, mode="nt", out_dtype=BF16, tm=1024, tn=1024, tk=2048)
    wgrad = functools.partial(_mm, mode="tn", out_dtype=F32, tm=1024, tn=1024, tk=2048)
    dmg = dgrad(dr, w_out, name="out_proj_dx")
    dw_out = wgrad(mg, dr, name="out_proj_dw")
    dyc, dym, dyx, dz = _merge_bwd(dmg, z, yc, ym, yx, D)
    dpc = dgrad(dyc, w_co, name="conv_out_dx")
    dw_co = wgrad(pc, dyc, name="conv_out_dw")
    dpm = dgrad(dym, w_mo, name="ml_out_dx")
    dw_mo = wgrad(pm, dym, name="ml_out_dw")
    dpx = dgrad(dyx, w_xo, name="xa_out_dx")
    dw_xo = wgrad(px, dyx, name="xa_out_dw")

    dkv, dz = _xattn_bwd(dpx, z, kv, dz, D)
    dkv_b = dkv.astype(BF16)
    dw_kv = wgrad(memn, dkv_b, name="mem_kv_dw")
    dmemn = _mm(dkv_b, w_kv, mode="nt", out_dtype=F32, tm=256, tn=1024, tk=2048, name="mem_kv_dx")
    dg_mem = _gain_grad(dmemn, mem, rstd_mem)

    du1, cstats, dz = _conv_bwd1(dpc, z, u1, g_ln, b_ln, dz, D)
    dw_dw, dz = _conv_bwd2(du1, z, w_dw, dz, D)

    dhm, dg_head, dz = _ml_post_bwd(dpm, hm, z, g_head, dz, D)
    dqks, dgates, dz = _mlstm_bwd(qks, z, gcol, grow, bif_row, bif_col, hm, dhm, cst, nst, dz, D)
    dc, dw_qk = _qk_bwd1(dqks, z, w_qk, D)
    dz = _qk_bwd2(dc, w_qk, dz, D)

    g = dict(w_conv_out=dw_co, w_ml_out=dw_mo, w_xa_out=dw_xo, w_out=dw_out, w_mem_kv=dw_kv,
             w_qk_conv=dw_qk[:QK_W], w_dw=dw_dw[:CONV_W])
    dgates_b = dgates.astype(BF16)
    in_proj_dw = functools.partial(_mm, dz, h, mode="tn", out_dtype=BF16 if dist else F32, tm=1024, tn=2048, tk=2048,
                                   name="in_proj_dw")
    in_proj_dx = functools.partial(_mm, dz, wpT, mode="nn", out_dtype=F32, tm=512, tn=2048, tk=2048,
                                   name="in_proj_dx")
    parts = {}
    if dist:
        send = [_to_slabs(n, g[n], BF16 if n in LATE[:5] else F32) for n in LATE]
        dwpT, recv = in_proj_dw(side=(send, True))
        parts.update(zip(LATE, recv))
        *_, height, _, start = _w_in_rows(D)
        dh, recv = in_proj_dx(side=([dwpT], _RowWindows(start, height)))
        parts['w_in'] = recv[0]
    else:
        dwpT = in_proj_dw()
        dh = in_proj_dx()
    dwifT = _mm(dgates_b, h, mode="tn", out_dtype=F32, tm=IF_PAD, tn=2048, tk=2048, name="gates_dw")
    if not dist:
        g['w_in'] = jnp.concatenate([dwpT[:8 * D], dwifT[:N_IF], dwpT[8 * D:]], axis=0).T
    dh = _mm(dgates_b, wifT, mode="nn", out_dtype=F32, tm=1024, tn=1024, tk=IF_PAD, name="gates_dx", acc_in=dh)
    dx, dg_pre = _prenorm_bwd(dh, x, rstd, dy, g_pre)

    db_if = _col_sum(dgates)[0, :N_IF]
    g.update(g_pre=dg_pre[0], b_if=db_if, b_dw=cstats[2], g_ln=cstats[0], b_ln=cstats[1], g_ml_head=dg_head[0],
             g_mem=dg_mem[0], g_post=dg_post[0], w_in_gates=dwifT[:N_IF])
    return loss[0, 0], dx, g, parts


WEIGHTS = ('g_pre', 'w_in', 'b_if', 'w_qk_conv', 'w_dw', 'b_dw', 'g_ln', 'b_ln', 'w_conv_out', 'g_ml_head',
           'w_ml_out', 'g_mem', 'w_mem_kv', 'w_xa_out', 'w_out', 'g_post')
VECTORS = ('g_pre', 'b_dw', 'g_ln', 'b_ln', 'g_ml_head', 'g_mem', 'g_post')


def kernel(x, mem, g_pre, w_in, b_if, w_qk_conv, w_dw, b_dw, g_ln, b_ln, w_conv_out, g_ml_head, w_ml_out, g_mem, w_mem_kv, w_xa_out, w_out, g_post, loss_target, m_g_pre, m_w_in, m_b_if, m_w_qk_conv, m_w_dw, m_b_dw, m_g_ln, m_b_ln, m_w_conv_out, m_g_ml_head, m_w_ml_out, m_g_mem, m_w_mem_kv, m_w_xa_out, m_w_out, m_g_post, v_g_pre, v_w_in, v_b_if, v_w_qk_conv, v_w_dw, v_b_dw, v_g_ln, v_b_ln, v_w_conv_out, v_g_ml_head, v_w_ml_out, v_g_mem, v_w_mem_kv, v_w_xa_out, v_w_out, v_g_post):
    S, D = x.shape[1], x.shape[2]
    w = dict(g_pre=g_pre, w_in=w_in, b_if=b_if, w_qk_conv=w_qk_conv, w_dw=w_dw, b_dw=b_dw, g_ln=g_ln, b_ln=b_ln,
             w_conv_out=w_conv_out, g_ml_head=g_ml_head, w_ml_out=w_ml_out, g_mem=g_mem, w_mem_kv=w_mem_kv,
             w_xa_out=w_xa_out, w_out=w_out, g_post=g_post)
    mom = dict(g_pre=m_g_pre, w_in=m_w_in, b_if=m_b_if, w_qk_conv=m_w_qk_conv, w_dw=m_w_dw, b_dw=m_b_dw,
               g_ln=m_g_ln, b_ln=m_b_ln, w_conv_out=m_w_conv_out, g_ml_head=m_g_ml_head, w_ml_out=m_w_ml_out,
               g_mem=m_g_mem, w_mem_kv=m_w_mem_kv, w_xa_out=m_w_xa_out, w_out=m_w_out, g_post=m_g_post)
    var = dict(g_pre=v_g_pre, w_in=v_w_in, b_if=v_b_if, w_qk_conv=v_w_qk_conv, w_dw=v_w_dw, b_dw=v_b_dw,
               g_ln=v_g_ln, b_ln=v_b_ln, w_conv_out=v_w_conv_out, g_ml_head=v_g_ml_head, w_ml_out=v_w_ml_out,
               g_mem=v_g_mem, w_mem_kv=v_w_mem_kv, w_xa_out=v_w_xa_out, w_out=v_w_out, g_post=v_g_post)

    n_rows, gate_dev, gate_row, height, first, start = _w_in_rows(D)
    me = 4 * lax.axis_index("x") + 2 * lax.axis_index("y") + lax.axis_index("c")

    my_block = w_in.T.astype(BF16)
    south = _core_gather(my_block, 0, "gather_w_in_south")
    early, rest, _ = _w_in_tiles(D)
    south_blocks = {2 * q: south[q] for q in range(N_CHIP)}
    assert gate_dev % 2 == 0
    w_early = _w_rows_of_tiles(early, south_blocks, D)
    gates = south_blocks[gate_dev][gate_row:gate_row + N_IF]
    wifT = jnp.zeros((IF_PAD, D), BF16).at[:N_IF].set(gates)

    def rows_of_rest(north):
        blocks = dict(south_blocks)
        blocks.update({2 * q + 1: north[q] for q in range(N_CHIP)})
        return _w_rows_of_tiles(rest, blocks, D)

    late = {n: w[n].astype(BF16) if n in LATE[:5] else w[n] for n in LATE}
    row = lambda a: a.reshape(1, D)

    loss, dx, g, parts = _local_step(
        x[0], mem[0], loss_target[0], row(g_pre), w_early, (my_block, rows_of_rest), wifT, b_if, row(b_dw),
        row(g_ln), row(b_ln), row(g_ml_head), row(g_mem), row(g_post), late, True)

    pad = lambda a: jnp.zeros((D,), F32).at[:N_IF].set(a)
    vec = jnp.concatenate([jnp.stack([g[n] for n in VECTORS] + [pad(g['b_if'])]), g['w_in_gates']])
    parts['vec'] = _exchange([vec], False, "gather_vector_grads")[0]

    out = {}
    for n in LATE:
        out[n] = _adamw(parts[n], w[n], mom[n], var[n], "adamw_" + n)
    zeros8 = jnp.zeros((N_IF, D), F32)
    stack = lambda d: jnp.concatenate([jnp.stack([d[n] for n in VECTORS] + [pad(d['b_if'])]), zeros8])
    vres = _adamw(parts['vec'], stack(w), stack(mom), stack(var), "adamw_vectors")
    for i, n in enumerate(VECTORS):
        out[n] = [r[i] for r in vres]
    out['b_if'] = [r[len(VECTORS), :N_IF] for r in vres]
    gate_grad = vres[0][len(VECTORS) + 1:]

    win = _sum_parts(parts['w_in'], "sum_w_in_grads")
    seg = lax.dynamic_slice(win, (first(me) - start(me), 0), (n_rows, D))
    with_gates = jnp.concatenate([seg[:gate_row], gate_grad, seg[gate_row:n_rows - N_IF]], axis=0)
    g_in = jnp.where(me == gate_dev, with_gates, seg)
    out['w_in'] = [r.T for r in _adamw(g_in[None], w_in.T, m_w_in.T, v_w_in.T, "adamw_w_in")]

    loss = lax.psum(loss, ("x", "y", "c"))
    res = [loss, dx.reshape(1, S, D)]
    for k in range(4):
        res += [out[n][k] for n in WEIGHTS]
    return tuple(res)
```

```python
import functools
import math

import jax
import jax.numpy as jnp
from jax import lax
from jax.experimental import pallas as pl
from jax.experimental.pallas import tpu as pltpu

F32 = jnp.float32
BF16 = jnp.bfloat16

N_DEV = 8
N_HEADS = 4
CONV_W = 31
QK_W = 4
N_IF = 2 * N_HEADS
IF_PAD = 128
EPS = 1e-6
NEG = -1e30

ADAM_LR = 0.001
ADAM_B1 = 0.9
ADAM_B2 = 0.999
ADAM_EPS = 1e-08
ADAM_WD = 0.01
ADAM_STEP = 10

TOK_TILE = 256
ML_CHUNK = 256
HALO = 32
BF16_ROWS = 16
VMEM_LIMIT = 56 * 1024 * 1024


def _cparams(sem=None):
    kw = dict(vmem_limit_bytes=VMEM_LIMIT)
    if sem is not None:
        kw["dimension_semantics"] = sem
    return pltpu.CompilerParams(**kw)


def _sig(x):
    return jax.nn.sigmoid(x)


def _silu(x):
    return x * _sig(x)


def _dsilu(x):
    s = _sig(x)
    return s * (1.0 + x * (1.0 - s))


def _logsig(x):
    return jnp.minimum(x, 0.0) - jnp.log(1.0 + jnp.exp(-jnp.abs(x)))


def _dot(a, b, dims):
    return lax.dot_general(a, b, (dims, ((), ())), preferred_element_type=F32)


def _dot_nn(a, b):
    return _dot(a, b, ((1,), (0,)))


def _dot_nt(a, b):
    return _dot(a, b, ((1,), (1,)))


def _dot_tn(a, b):
    return _dot(a, b, ((0,), (0,)))


def _split3(a):
    hi = a.astype(BF16)
    r1 = a - hi.astype(F32)
    mid = r1.astype(BF16)
    lo = (r1 - mid.astype(F32)).astype(BF16)
    return hi, mid, lo


def _tri_left(tri, a):
    hi, mid, lo = _split3(a)
    return _dot_nn(tri, hi) + _dot_nn(tri, mid) + _dot_nn(tri, lo)


def _tri_right(a, tri):
    hi, mid, lo = _split3(a)
    return _dot_nn(hi, tri) + _dot_nn(mid, tri) + _dot_nn(lo, tri)


def _fit(n, t):
    if n <= t:
        return n
    return max(c for c in range(128, t + 1, 128) if n % c == 0)


def _peer(k):
    x, y, c = lax.axis_index("x"), lax.axis_index("y"), lax.axis_index("c")
    px = 1 - x if (k >> 2) & 1 else x
    py = 1 - y if (k >> 1) & 1 else y
    pc = 1 - c if k & 1 else c
    return (px, py, pc), 4 * px + 2 * py + pc


class _RowWindows:
    def __init__(self, start, height):
        self.start, self.height = start, height


def _xchg_copies(srcs, dsts, send_sems, recv_sems, loc_sems, scatter):
    _, me = _peer(0)

    def piece(src, p):
        if isinstance(scatter, _RowWindows):
            return src.at[pl.ds(pl.multiple_of(scatter.start(p), BF16_ROWS), scatter.height), :]
        return src.at[p] if scatter else src

    copies = []
    for t, (src, dst) in enumerate(zip(srcs, dsts)):
        copies.append(pltpu.make_async_copy(piece(src, me), dst.at[me], loc_sems.at[t]))
        for k in range(1, N_DEV):
            dev, p = _peer(k)
            s = t * (N_DEV - 1) + k - 1
            copies.append(pltpu.make_async_remote_copy(
                src_ref=piece(src, p), dst_ref=dst.at[me],
                send_sem=send_sems.at[s], recv_sem=recv_sems.at[s],
                device_id=dev, device_id_type=pl.DeviceIdType.MESH))
    return copies


def _xchg_out_shapes(operands, scatter):
    def one(a):
        if isinstance(scatter, _RowWindows):
            return (scatter.height, a.shape[1])
        return tuple(a.shape[1:] if scatter else a.shape)
    return [jax.ShapeDtypeStruct((N_DEV,) + one(a), a.dtype) for a in operands]


def _xchg_sems(n):
    return [pltpu.SemaphoreType.DMA((n * (N_DEV - 1),)), pltpu.SemaphoreType.DMA((n * (N_DEV - 1),)),
            pltpu.SemaphoreType.DMA((n,))]


def _mm(a, b, *, mode, out_dtype, tm, tn, tk, name, n_outer=False, acc_in=None, side=None):
    if mode == "nn":
        (M, K), (K2, N) = a.shape, b.shape
    elif mode == "nt":
        (M, K), (N, K2) = a.shape, b.shape
    else:
        (K, M), (K2, N) = a.shape, b.shape
    assert K == K2, (a.shape, b.shape, mode)
    tm, tn, tk = _fit(M, tm), _fit(N, tn), _fit(K, tk)
    assert M % tm == 0 and N % tn == 0 and K % tk == 0, (a.shape, b.shape, tm, tn, tk)
    nk = K // tk
    if n_outer:
        grid = (N // tn, M // tm, nk)
        ij = lambda g0, g1: (g1, g0)
    else:
        grid = (M // tm, N // tn, nk)
        ij = lambda g0, g1: (g0, g1)

    if mode == "nn":
        a_spec = pl.BlockSpec((tm, tk), lambda g0, g1, k: (ij(g0, g1)[0], k))
        b_spec = pl.BlockSpec((tk, tn), lambda g0, g1, k: (k, ij(g0, g1)[1]))
        dot = _dot_nn
    elif mode == "nt":
        a_spec = pl.BlockSpec((tm, tk), lambda g0, g1, k: (ij(g0, g1)[0], k))
        b_spec = pl.BlockSpec((tn, tk), lambda g0, g1, k: (ij(g0, g1)[1], k))
        dot = _dot_nt
    else:
        a_spec = pl.BlockSpec((tk, tm), lambda g0, g1, k: (k, ij(g0, g1)[0]))
        b_spec = pl.BlockSpec((tk, tn), lambda g0, g1, k: (k, ij(g0, g1)[1]))
        dot = _dot_tn
    o_spec = pl.BlockSpec((tm, tn), lambda g0, g1, k: ij(g0, g1))
    has_acc = acc_in is not None
    n_side = len(side[0]) if side is not None else 0
    scatter = side[1] if side is not None else False
    n_in = 2 + int(has_acc)

    def body(*refs):
        a_ref, b_ref = refs[0], refs[1]
        c_ref = refs[2] if has_acc else None
        srcs = refs[n_in:n_in + n_side]
        o_ref = refs[n_in + n_side]
        dsts = refs[n_in + n_side + 1:n_in + 2 * n_side + 1]
        scratch = refs[n_in + 2 * n_side + 1:]
        k = pl.program_id(2)
        if n_side:
            sems = scratch[-3:]
            first = (pl.program_id(0) == 0) & (pl.program_id(1) == 0) & (k == 0)
            last = (pl.program_id(0) == grid[0] - 1) & (pl.program_id(1) == grid[1] - 1) & (k == nk - 1)

            @pl.when(first)
            def _():
                for cp in _xchg_copies(srcs, dsts, *sems, scatter):
                    cp.start()

        p = dot(a_ref[...], b_ref[...])

        def finish(r):
            if has_acc:
                r = r + c_ref[...].astype(F32)
            o_ref[...] = r.astype(out_dtype)

        if nk == 1:
            finish(p)
        else:
            acc = scratch[0]

            @pl.when(k == 0)
            def _():
                acc[...] = p

            @pl.when((k > 0) & (k < nk - 1))
            def _():
                acc[...] += p

            @pl.when(k == nk - 1)
            def _():
                finish(acc[...] + p)

        if n_side:
            @pl.when(last)
            def _():
                for cp in _xchg_copies(srcs, dsts, *sems, scatter):
                    cp.wait()

    hbm = pl.BlockSpec(memory_space=pltpu.HBM)
    in_specs = [a_spec, b_spec]
    args = [a, b]
    if has_acc:
        in_specs.append(o_spec)
        args.append(acc_in)
    out_specs = [o_spec]
    out_shape = [jax.ShapeDtypeStruct((M, N), out_dtype)]
    scratch_shapes = [pltpu.VMEM((tm, tn), F32)] if nk > 1 else []
    if n_side:
        in_specs += [hbm] * n_side
        args += list(side[0])
        out_specs += [hbm] * n_side
        out_shape += _xchg_out_shapes(side[0], scatter)
        scratch_shapes += _xchg_sems(n_side)
        cp = pltpu.CompilerParams(vmem_limit_bytes=VMEM_LIMIT, has_side_effects=True,
                                  dimension_semantics=("arbitrary", "arbitrary", "arbitrary"))
    else:
        cp = _cparams(("parallel", "parallel", "arbitrary"))
    res = pl.pallas_call(
        body, name=name, grid=grid, in_specs=in_specs, out_specs=out_specs, out_shape=out_shape,
        scratch_shapes=scratch_shapes, compiler_params=cp,
    )(*args)
    return (res[0], list(res[1:])) if n_side else res[0]


def _rmsnorm_fwd(x, g, name):
    S, D = x.shape
    T = min(TOK_TILE, S)

    def body(x_ref, g_ref, h_ref, r_ref):
        xv = x_ref[...]
        r = lax.rsqrt(jnp.mean(xv * xv, axis=-1, keepdims=True) + EPS)
        h_ref[...] = (xv * r * g_ref[...]).astype(BF16)
        r_ref[...] = r

    return pl.pallas_call(
        body, name=name, grid=(S // T,),
        in_specs=[pl.BlockSpec((T, D), lambda i: (i, 0)), pl.BlockSpec((1, D), lambda i: (0, 0))],
        out_specs=[pl.BlockSpec((T, D), lambda i: (i, 0)), pl.BlockSpec((T, 1), lambda i: (i, 0))],
        out_shape=[jax.ShapeDtypeStruct((S, D), BF16), jax.ShapeDtypeStruct((S, 1), F32)],
        compiler_params=_cparams(("parallel",)),
    )(x, g)


LANE = 128
ROW_BLK = 64


def _shift_strip(src, sh, cs, nr, residues=range(1, 8)):
    for r in residues:
        sh[r] = src[pl.ds(r, nr), cs]


def _tap(src, sh, cs, o, r0):
    r = o % 8
    if r == 0:
        return src[pl.ds(o + r0, ROW_BLK), cs]
    return sh[r, pl.ds(o - r + r0, ROW_BLK), :]


def _conv_fwd(z, w_dw, b_dw, g_ln, b_ln, D):
    S = z.shape[0]
    T = min(TOK_TILE, S)
    hb = T // HALO

    def body(a_ref, b_ref, zc_ref, ah_ref, bh_ref, w_ref, bdw_ref, gln_ref, bln_ref, u1_ref, pc_ref, u0s, sh):
        i = pl.program_id(0)
        a = a_ref[...].astype(F32)
        b = b_ref[...].astype(F32)
        u0s[pl.ds(HALO, T), :] = a * _sig(b)
        halo = ah_ref[...].astype(F32) * _sig(bh_ref[...].astype(F32))
        u0s[pl.ds(0, HALO), :] = jnp.where(i > 0, halo, 0.0)
        u0s[pl.ds(T + HALO, 8), :] = jnp.zeros((8, D), F32)
        for c0 in range(0, D, LANE):
            cs = pl.ds(c0, LANE)
            _shift_strip(u0s, sh, cs, T + HALO)
            for r0 in range(0, T, ROW_BLK):
                acc = jnp.broadcast_to(bdw_ref[:, cs], (ROW_BLK, LANE))
                for j in range(CONV_W):
                    acc = acc + w_ref[pl.ds(j, 1), cs] * _tap(u0s, sh, cs, HALO - (CONV_W - 1) + j, r0)
                u1_ref[pl.ds(r0, ROW_BLK), cs] = acc
        acc = u1_ref[...]
        mu = jnp.mean(acc, axis=-1, keepdims=True)
        xc = acc - mu
        var = jnp.mean(xc * xc, axis=-1, keepdims=True)
        ln = xc * lax.rsqrt(var + EPS) * gln_ref[...] + bln_ref[...]
        pc_ref[...] = (_silu(ln) * _silu(zc_ref[...].astype(F32))).astype(BF16)

    prev = lambda i: (jnp.maximum(i * hb - 1, 0), 0)
    prev_b = lambda i: (jnp.maximum(i * hb - 1, 0), D // D)
    vec = pl.BlockSpec((1, D), lambda i: (0, 0))
    return pl.pallas_call(
        body, name="conv_fwd", grid=(S // T,),
        in_specs=[pl.BlockSpec((T, D), lambda i: (i, 0)), pl.BlockSpec((T, D), lambda i: (i, 1)),
                  pl.BlockSpec((T, D), lambda i: (i, 2)),
                  pl.BlockSpec((HALO, D), prev), pl.BlockSpec((HALO, D), prev_b),
                  pl.BlockSpec((32, D), lambda i: (0, 0)), vec, vec, vec],
        out_specs=[pl.BlockSpec((T, D), lambda i: (i, 0)), pl.BlockSpec((T, D), lambda i: (i, 0))],
        out_shape=[jax.ShapeDtypeStruct((S, D), F32), jax.ShapeDtypeStruct((S, D), BF16)],
        scratch_shapes=[pltpu.VMEM((T + HALO + 8, D), F32), pltpu.VMEM((8, T + HALO, LANE), F32)],
        compiler_params=_cparams(("parallel",)),
    )(z, z, z, z, z, w_dw, b_dw, g_ln, b_ln)


_QK_RES = sorted({(HALO - (QK_W - 1) + j) % 8 for j in range(QK_W)} - {0})


def _qk_fill(ps, p0_ref, p1_ref, h0_ref, h1_ref, T, D):
    i = pl.program_id(0)
    ps[pl.ds(HALO, T), pl.ds(0, D)] = p0_ref[...].astype(F32)
    ps[pl.ds(HALO, T), pl.ds(D, D)] = p1_ref[...].astype(F32)
    ps[pl.ds(0, HALO), pl.ds(0, D)] = jnp.where(i > 0, h0_ref[...].astype(F32), 0.0)
    ps[pl.ds(0, HALO), pl.ds(D, D)] = jnp.where(i > 0, h1_ref[...].astype(F32), 0.0)
    ps[pl.ds(T + HALO, 8), :] = jnp.zeros((8, 2 * D), F32)


def _qk_conv(ps, sh, w_ref, cs, r0):
    acc = jnp.zeros((ROW_BLK, LANE), F32)
    for j in range(QK_W):
        acc = acc + w_ref[pl.ds(j, 1), cs] * _tap(ps, sh, cs, HALO - (QK_W - 1) + j, r0)
    return acc


def _qk_fwd(z, w_qk, D):
    S = z.shape[0]
    T = min(TOK_TILE, S)
    hb = T // HALO
    scale = (D // N_HEADS) ** -0.5
    W2 = 2 * D

    def body(p0_ref, p1_ref, h0_ref, h1_ref, w_ref, o_ref, ps, sh):
        _qk_fill(ps, p0_ref, p1_ref, h0_ref, h1_ref, T, D)
        for c0 in range(0, W2, LANE):
            cs = pl.ds(c0, LANE)
            _shift_strip(ps, sh, cs, T + HALO, _QK_RES)
            for r0 in range(0, T, ROW_BLK):
                qk = _silu(_qk_conv(ps, sh, w_ref, cs, r0))
                o_ref[pl.ds(r0, ROW_BLK), cs] = (qk * scale if c0 >= D else qk).astype(BF16)

    prev0 = lambda i: (jnp.maximum(i * hb - 1, 0), 3)
    prev1 = lambda i: (jnp.maximum(i * hb - 1, 0), 4)
    return pl.pallas_call(
        body, name="qk_fwd", grid=(S // T,),
        in_specs=[pl.BlockSpec((T, D), lambda i: (i, 3)), pl.BlockSpec((T, D), lambda i: (i, 4)),
                  pl.BlockSpec((HALO, D), prev0), pl.BlockSpec((HALO, D), prev1),
                  pl.BlockSpec((8, W2), lambda i: (0, 0))],
        out_specs=pl.BlockSpec((T, W2), lambda i: (i, 0)),
        out_shape=jax.ShapeDtypeStruct((S, W2), BF16),
        scratch_shapes=[pltpu.VMEM((T + HALO + 8, W2), F32), pltpu.VMEM((8, T + HALO, LANE), F32)],
        compiler_params=_cparams(("parallel",)),
    )(z, z, z, z, w_qk)


def _ml_gates(gc_ref, gr_ref, bifr_ref, bifc_ref, L):
    gc = gc_ref[...] + bifr_ref[...]
    gr = gr_ref[...] + bifc_ref[...]
    ii = lax.broadcasted_iota(jnp.int32, (L, L), 0)
    jj = lax.broadcasted_iota(jnp.int32, (L, L), 1)
    tri = (jj <= ii).astype(BF16)
    triu = (ii <= jj).astype(BF16)
    bcol_all = _tri_left(tri, _logsig(gc))
    brow_all = _tri_right(_logsig(gr), triu)
    return gc, gr, bcol_all, brow_all, ii, jj


def _ml_intra(q, k, b_c, b_r, li_r, m, n_row, ii, jj):
    d = b_c - b_r + li_r
    dm = jnp.where(jj <= ii, d, NEG)
    inter = b_c + m
    m_row = jnp.maximum(inter, jnp.max(dm, axis=1, keepdims=True))
    w_intra = jnp.exp(dm - m_row)
    w_inter = jnp.exp(inter - m_row)
    qk = _dot_nt(q, k)
    s = qk * w_intra
    qn = jnp.sum(q.astype(F32) * n_row, axis=-1, keepdims=True)
    den = jnp.sum(s, axis=-1, keepdims=True) + w_inter * qn
    hdiv = jnp.maximum(jnp.abs(den), jnp.exp(-m_row))
    return qk, w_intra, w_inter, s, qn, den, hdiv, m_row


def _ml_state(b_c, li_c, m, L):
    b_last = b_c[L - 1:L, :]
    g_c = b_last - b_c + li_c
    m_new = jnp.maximum(b_last + m, jnp.max(g_c, axis=0, keepdims=True))
    decay = jnp.exp(b_last + m - m_new)
    wk = jnp.exp(g_c - m_new)
    return m_new, decay, wk


def _mlstm_fwd(qks, z, gcol, grow, bif_row, bif_col, D):
    S = qks.shape[0]
    L = min(ML_CHUNK, S)
    NC = S // L
    H = N_HEADS
    dh = D // H

    def body(q_ref, k_ref, v_ref, gc_ref, gr_ref, bifr_ref, bifc_ref, hm_ref, cst_ref, nst_ref, C, NM):
        c = pl.program_id(0)

        @pl.when(c == 0)
        def _():
            C[...] = jnp.zeros_like(C)
            NM[...] = jnp.zeros_like(NM)

        gc, gr, bcol_all, brow_all, ii, jj = _ml_gates(gc_ref, gr_ref.at[0], bifr_ref, bifc_ref, L)
        for h in range(H):
            hs = pl.ds(h * dh, dh)
            q = q_ref[:, hs]
            k = k_ref[:, hs]
            v = v_ref[:, hs]
            b_c = bcol_all[:, H + h:H + h + 1]
            li_c = gc[:, h:h + 1]
            b_r = brow_all[H + h:H + h + 1, :]
            li_r = gr[h:h + 1, :]
            n_row = NM[h, 0:1, :]
            m = NM[h, 1:2, 0:1]
            Cb = C[h].astype(BF16)
            cst_ref[0, h] = Cb
            nst_ref[0, h] = NM[h]
            qk, w_intra, w_inter, s, qn, den, hdiv, m_row = _ml_intra(q, k, b_c, b_r, li_r, m, n_row, ii, jj)
            num = _dot_nn(s.astype(BF16), v) + w_inter * _dot_nn(q, Cb)
            hm_ref[:, hs] = num / hdiv
            m_new, decay, wk = _ml_state(b_c, li_c, m, L)
            wkk = wk * k.astype(F32)
            C[h] = decay * C[h] + _dot_tn(wkk.astype(BF16), v)
            NM[h, 0:1, :] = decay * n_row + jnp.sum(wkk, axis=0, keepdims=True)
            NM[h, 1:2, :] = jnp.broadcast_to(m_new, (1, dh))

    return pl.pallas_call(
        body, name="mlstm_fwd", grid=(NC,),
        in_specs=[pl.BlockSpec((L, D), lambda c: (c, 0)), pl.BlockSpec((L, D), lambda c: (c, 1)),
                  pl.BlockSpec((L, D), lambda c: (c, 5)),
                  pl.BlockSpec((L, IF_PAD), lambda c: (c, 0)),
                  pl.BlockSpec((1, 16, L), lambda c: (c, 0, 0)),
                  pl.BlockSpec((1, IF_PAD), lambda c: (0, 0)), pl.BlockSpec((16, 1), lambda c: (0, 0))],
        out_specs=[pl.BlockSpec((L, D), lambda c: (c, 0)),
                   pl.BlockSpec((1, H, dh, dh), lambda c: (c, 0, 0, 0)),
                   pl.BlockSpec((1, H, 8, dh), lambda c: (c, 0, 0, 0))],
        out_shape=[jax.ShapeDtypeStruct((S, D), F32), jax.ShapeDtypeStruct((NC, H, dh, dh), BF16),
                   jax.ShapeDtypeStruct((NC, H, 8, dh), F32)],
        scratch_shapes=[pltpu.VMEM((H, dh, dh), F32), pltpu.VMEM((H, 8, dh), F32)],
        compiler_params=_cparams(("arbitrary",)),
    )(qks, qks, z, gcol, grow, bif_row, bif_col)


def _mlstm_bwd(qks, z, gcol, grow, bif_row, bif_col, hm, dhm, cst, nst, dz, D):
    S = qks.shape[0]
    L = min(ML_CHUNK, S)
    NC = S // L
    H = N_HEADS
    dh = D // H

    def body(q_ref, k_ref, v_ref, gc_ref, gr_ref, bifr_ref, bifc_ref, hm_ref, dhm_ref, cst_ref, nst_ref, dz_any,
             dqk_ref, dg_ref, dv_ref, dC, dN):
        del dz_any
        c = pl.program_id(0)

        @pl.when(c == 0)
        def _():
            dC[...] = jnp.zeros_like(dC)
            dN[...] = jnp.zeros_like(dN)

        gc, gr, bcol_all, brow_all, ii, jj = _ml_gates(gc_ref, gr_ref.at[0], bifr_ref, bifc_ref, L)
        eye = ii == jj
        lane = lax.broadcasted_iota(jnp.int32, (L, IF_PAD), 1)
        db_all = jnp.zeros((L, IF_PAD), F32)
        dli_all = jnp.zeros((L, IF_PAD), F32)
        last_row = lax.broadcasted_iota(jnp.int32, (L, 1), 0) == L - 1
        for h in range(H):
            hs = pl.ds(h * dh, dh)
            q = q_ref[:, hs]
            k = k_ref[:, hs]
            v = v_ref[:, hs]
            qf = q.astype(F32)
            kf = k.astype(F32)
            b_c = bcol_all[:, H + h:H + h + 1]
            li_c = gc[:, h:h + 1]
            b_r = brow_all[H + h:H + h + 1, :]
            li_r = gr[h:h + 1, :]
            n_row = nst_ref[0, h, 0:1, :]
            m = nst_ref[0, h, 1:2, 0:1]
            Cb = cst_ref[0, h]
            qk, w_intra, w_inter, s, qn, den, hdiv, m_row = _ml_intra(q, k, b_c, b_r, li_r, m, n_row, ii, jj)
            dh_ = dhm_ref[:, hs]
            hh = hm_ref[:, hs]
            dnum = dh_ / hdiv
            dhdiv = -jnp.sum(dh_ * hh, axis=-1, keepdims=True) / hdiv
            dden = jnp.where(jnp.abs(den) > jnp.exp(-m_row), dhdiv * jnp.sign(den), 0.0)
            dnum_b = dnum.astype(BF16)
            ds = _dot_nt(dnum_b, v) + dden
            s_b = s.astype(BF16)
            dv = _dot_tn(s_b, dnum_b)
            dnC = _dot_nt(dnum_b, Cb)
            dwi = dden * w_inter
            dw_inter = jnp.sum(qf * dnC, axis=-1, keepdims=True) + dden * qn
            dq = w_inter * dnC + dwi * n_row
            dC_out = _dot_tn((w_inter * qf).astype(BF16), dnum_b)
            dn_out = jnp.sum(dwi * qf, axis=0, keepdims=True)
            dqk = (ds * w_intra).astype(BF16)
            dq = dq + _dot_nn(dqk, k)
            dk = _dot_tn(dqk, q)
            dd = ds * s
            rs_c = jnp.sum(dd, axis=1, keepdims=True)
            cs_r = jnp.sum(dd, axis=0, keepdims=True)
            cs_c = jnp.sum(jnp.where(eye, cs_r, 0.0), axis=1, keepdims=True)
            dinter = dw_inter * w_inter
            m_new, decay, wk = _ml_state(b_c, li_c, m, L)
            dCn = dC[h]
            dCn_b = dCn.astype(BF16)
            dn_new = dN[h, 0:1, :]
            ddecay = (jnp.sum(jnp.sum(dCn * Cb.astype(F32), axis=1, keepdims=True), axis=0, keepdims=True)
                      + jnp.sum(dn_new * n_row, axis=1, keepdims=True))
            dwkk = _dot_nt(v, dCn_b) + dn_new
            wkk_b = (wk * kf).astype(BF16)
            dv = dv + _dot_nn(wkk_b, dCn_b)
            dk = dk + wk * dwkk
            dg = jnp.sum(dwkk * kf, axis=-1, keepdims=True) * wk
            db_last = ddecay * decay + jnp.sum(dg, axis=0, keepdims=True)
            dC[h] = decay * dCn + dC_out
            dN[h, 0:1, :] = decay * dn_new + dn_out
            db_c = rs_c - cs_c + dinter - dg + jnp.where(last_row, db_last, 0.0)
            dli_c = cs_c + dg
            db_all = jnp.where(lane == H + h, db_c, db_all)
            dli_all = jnp.where(lane == h, dli_c, dli_all)
            dqk_ref[:, hs] = dq.astype(BF16)
            dqk_ref[:, pl.ds(D + h * dh, dh)] = dk.astype(BF16)
            dv_ref[:, hs] = dv.astype(BF16)
        triu = (ii <= jj).astype(BF16)
        dlf = _tri_left(triu, db_all)
        dg_ref[...] = dli_all + dlf * (1.0 - _sig(gc))

    r = lambda c: NC - 1 - c
    n_in = 12
    return pl.pallas_call(
        body, name="mlstm_bwd", grid=(NC,),
        in_specs=[pl.BlockSpec((L, D), lambda c: (r(c), 0)), pl.BlockSpec((L, D), lambda c: (r(c), 1)),
                  pl.BlockSpec((L, D), lambda c: (r(c), 5)),
                  pl.BlockSpec((L, IF_PAD), lambda c: (r(c), 0)),
                  pl.BlockSpec((1, 16, L), lambda c: (r(c), 0, 0)),
                  pl.BlockSpec((1, IF_PAD), lambda c: (0, 0)), pl.BlockSpec((16, 1), lambda c: (0, 0)),
                  pl.BlockSpec((L, D), lambda c: (r(c), 0)), pl.BlockSpec((L, D), lambda c: (r(c), 0)),
                  pl.BlockSpec((1, H, dh, dh), lambda c: (r(c), 0, 0, 0)),
                  pl.BlockSpec((1, H, 8, dh), lambda c: (r(c), 0, 0, 0)),
                  pl.BlockSpec(memory_space=pl.ANY)],
        out_specs=[pl.BlockSpec((L, 2 * D), lambda c: (r(c), 0)),
                   pl.BlockSpec((L, IF_PAD), lambda c: (r(c), 0)),
                   pl.BlockSpec((L, D), lambda c: (r(c), 5))],
        out_shape=[jax.ShapeDtypeStruct((S, 2 * D), BF16),
                   jax.ShapeDtypeStruct((S, IF_PAD), F32), jax.ShapeDtypeStruct(dz.shape, dz.dtype)],
        input_output_aliases={n_in - 1: 2},
        scratch_shapes=[pltpu.VMEM((H, dh, dh), F32), pltpu.VMEM((H, 8, dh), F32)],
        compiler_params=_cparams(("arbitrary",)),
    )(qks, qks, z, gcol, grow, bif_row, bif_col, hm, dhm, cst, nst, dz)


def _ml_post_fwd(hm, z, g_head, D):
    S = hm.shape[0]
    T = min(TOK_TILE, S)
    dh = D // N_HEADS

    def body(hm_ref, o_ref, zm_ref, g_ref, pm_ref):
        for h in range(N_HEADS):
            hs = pl.ds(h * dh, dh)
            hg = _sig(o_ref[:, hs].astype(F32)) * hm_ref[:, hs]
            rs = lax.rsqrt(jnp.mean(hg * hg, axis=-1, keepdims=True) + EPS)
            pm_ref[:, hs] = (hg * rs * g_ref[:, hs] * _silu(zm_ref[:, hs].astype(F32))).astype(BF16)

    return pl.pallas_call(
        body, name="ml_post_fwd", grid=(S // T,),
        in_specs=[pl.BlockSpec((T, D), lambda i: (i, 0)), pl.BlockSpec((T, D), lambda i: (i, 6)),
                  pl.BlockSpec((T, D), lambda i: (i, 7)), pl.BlockSpec((1, D), lambda i: (0, 0))],
        out_specs=pl.BlockSpec((T, D), lambda i: (i, 0)),
        out_shape=jax.ShapeDtypeStruct((S, D), BF16),
        compiler_params=_cparams(("parallel",)),
    )(hm, z, z, g_head)


def _ml_post_bwd(dpm, hm, z, g_head, dz, D):
    S = hm.shape[0]
    T = min(TOK_TILE, S)
    dh = D // N_HEADS

    def body(dpm_ref, hm_ref, o_ref, zm_ref, g_ref, dz_any, dhm_ref, dg_ref, dozm_ref):
        del dz_any
        do_ref = dozm_ref.at[:, pl.ds(0, D)]
        dzm_ref = dozm_ref.at[:, pl.ds(D, D)]
        i = pl.program_id(0)

        @pl.when(i == 0)
        def _():
            dg_ref[...] = jnp.zeros_like(dg_ref)

        for h in range(N_HEADS):
            hs = pl.ds(h * dh, dh)
            o = o_ref[:, hs].astype(F32)
            zm = zm_ref[:, hs].astype(F32)
            hmv = hm_ref[:, hs]
            dpm_ = dpm_ref[:, hs].astype(F32)
            g = g_ref[:, hs]
            so = _sig(o)
            hg = so * hmv
            rs = lax.rsqrt(jnp.mean(hg * hg, axis=-1, keepdims=True) + EPS)
            xn = hg * rs
            dzm_ref[:, hs] = (dpm_ * xn * g * _dsilu(zm)).astype(BF16)
            dhn = dpm_ * _silu(zm)
            dg_ref[0:1, hs] += jnp.sum(dhn * xn, axis=0, keepdims=True)
            dxn = dhn * g
            dhg = rs * (dxn - xn * jnp.mean(dxn * xn, axis=-1, keepdims=True))
            do_ref[:, hs] = (dhg * hmv * so * (1.0 - so)).astype(BF16)
            dhm_ref[:, hs] = dhg * so

    return pl.pallas_call(
        body, name="ml_post_bwd", grid=(S // T,),
        in_specs=[pl.BlockSpec((T, D), lambda i: (i, 0)), pl.BlockSpec((T, D), lambda i: (i, 0)),
                  pl.BlockSpec((T, D), lambda i: (i, 6)), pl.BlockSpec((T, D), lambda i: (i, 7)),
                  pl.BlockSpec((1, D), lambda i: (0, 0)), pl.BlockSpec(memory_space=pl.ANY)],
        out_specs=[pl.BlockSpec((T, D), lambda i: (i, 0)), pl.BlockSpec((8, D), lambda i: (0, 0)),
                   pl.BlockSpec((T, 2 * D), lambda i: (i, 3))],
        out_shape=[jax.ShapeDtypeStruct((S, D), F32), jax.ShapeDtypeStruct((8, D), F32),
                   jax.ShapeDtypeStruct(dz.shape, dz.dtype)],
        input_output_aliases={5: 2},
        compiler_params=_cparams(("arbitrary",)),
    )(dpm, hm, z, z, g_head, dz)


def _xattn_probs(q, km_h, scale):
    s = _dot_nt(q, km_h) * scale
    e = jnp.exp(s - jnp.max(s, axis=-1, keepdims=True))
    return e / jnp.sum(e, axis=-1, keepdims=True)


def _xattn_fwd(z, kv, D):
    S = z.shape[0]
    M = kv.shape[0]
    T = min(TOK_TILE, S)
    dh = D // N_HEADS
    scale = dh ** -0.5

    def body(q_ref, zx_ref, km_ref, vm_ref, px_ref):
        for h in range(N_HEADS):
            hs = pl.ds(h * dh, dh)
            p = _xattn_probs(q_ref[:, hs], km_ref[:, hs], scale)
            o = _dot_nn(p.astype(BF16), vm_ref[:, hs])
            px_ref[:, hs] = (o * _silu(zx_ref[:, hs].astype(F32))).astype(BF16)

    return pl.pallas_call(
        body, name="xattn_fwd", grid=(S // T,),
        in_specs=[pl.BlockSpec((T, D), lambda i: (i, 8)), pl.BlockSpec((T, D), lambda i: (i, 9)),
                  pl.BlockSpec((M, D), lambda i: (0, 0)), pl.BlockSpec((M, D), lambda i: (0, 1))],
        out_specs=pl.BlockSpec((T, D), lambda i: (i, 0)),
        out_shape=jax.ShapeDtypeStruct((S, D), BF16),
        compiler_params=_cparams(("parallel",)),
    )(z, z, kv, kv)


def _xattn_bwd(dpx, z, kv, dz, D):
    S = z.shape[0]
    M = kv.shape[0]
    T = min(TOK_TILE, S)
    dh = D // N_HEADS
    scale = dh ** -0.5

    def body(dpx_ref, q_ref, zx_ref, km_ref, vm_ref, dz_any, dkv_ref, dqz_ref):
        del dz_any
        i = pl.program_id(0)

        @pl.when(i == 0)
        def _():
            dkv_ref[...] = jnp.zeros_like(dkv_ref)

        for h in range(N_HEADS):
            hs = pl.ds(h * dh, dh)
            q = q_ref[:, hs]
            zx = zx_ref[:, hs].astype(F32)
            dpx_ = dpx_ref[:, hs].astype(F32)
            p = _xattn_probs(q, km_ref[:, hs], scale)
            p_b = p.astype(BF16)
            o = _dot_nn(p_b, vm_ref[:, hs])
            dqz_ref[:, pl.ds(D + h * dh, dh)] = (dpx_ * o * _dsilu(zx)).astype(BF16)
            do = (dpx_ * _silu(zx)).astype(BF16)
            dp = _dot_nt(do, vm_ref[:, hs])
            dsc = (p * (dp - jnp.sum(p * dp, axis=-1, keepdims=True)) * scale).astype(BF16)
            dqz_ref[:, hs] = _dot_nn(dsc, km_ref[:, hs]).astype(BF16)
            dkv_ref[:, hs] += _dot_tn(dsc, q)
            dkv_ref[:, pl.ds(D + h * dh, dh)] += _dot_tn(p_b, do)

    return pl.pallas_call(
        body, name="xattn_bwd", grid=(S // T,),
        in_specs=[pl.BlockSpec((T, D), lambda i: (i, 0)),
                  pl.BlockSpec((T, D), lambda i: (i, 8)), pl.BlockSpec((T, D), lambda i: (i, 9)),
                  pl.BlockSpec((M, D), lambda i: (0, 0)), pl.BlockSpec((M, D), lambda i: (0, 1)),
                  pl.BlockSpec(memory_space=pl.ANY)],
        out_specs=[pl.BlockSpec((M, 2 * D), lambda i: (0, 0)), pl.BlockSpec((T, 2 * D), lambda i: (i, 4))],
        out_shape=[jax.ShapeDtypeStruct((M, 2 * D), F32), jax.ShapeDtypeStruct(dz.shape, dz.dtype)],
        input_output_aliases={5: 1},
        compiler_params=_cparams(("arbitrary",)),
    )(dpx, z, z, kv, kv, dz)


def _col_sum(a):
    S, C = a.shape
    T = min(1024, S)

    def body(a_ref, o_ref):
        @pl.when(pl.program_id(0) == 0)
        def _():
            o_ref[...] = jnp.zeros_like(o_ref)

        o_ref[0:1, :] += jnp.sum(a_ref[...], axis=0, keepdims=True)

    return pl.pallas_call(
        body, name="col_sum", grid=(S // T,),
        in_specs=[pl.BlockSpec((T, C), lambda i: (i, 0))], out_specs=pl.BlockSpec((8, C), lambda i: (0, 0)),
        out_shape=jax.ShapeDtypeStruct((8, C), F32), compiler_params=_cparams(("arbitrary",)),
    )(a)


def _gain_grad(dn, xin, rstd):
    R, D = dn.shape

    def body(dn_ref, x_ref, r_ref, o_ref):
        o_ref[...] = jnp.zeros_like(o_ref)
        o_ref[0:1, :] = jnp.sum(dn_ref[...] * x_ref[...] * r_ref[...], axis=0, keepdims=True)

    return pl.pallas_call(
        body, name="gain_grad", out_shape=jax.ShapeDtypeStruct((8, D), F32),
        compiler_params=_cparams(),
    )(dn, xin, rstd)


def _merge_fwd(z, yc, ym, yx, D):
    S = z.shape[0]
    T = min(TOK_TILE, S)

    def body(g0, g1, g2, yc_ref, ym_ref, yx_ref, o_ref):
        o_ref[...] = (_sig(g0[...].astype(F32)) * yc_ref[...].astype(F32)
                      + _sig(g1[...].astype(F32)) * ym_ref[...].astype(F32)
                      + _sig(g2[...].astype(F32)) * yx_ref[...].astype(F32)).astype(BF16)

    tok = pl.BlockSpec((T, D), lambda i: (i, 0))
    return pl.pallas_call(
        body, name="merge_fwd", grid=(S // T,),
        in_specs=[pl.BlockSpec((T, D), lambda i: (i, 10)), pl.BlockSpec((T, D), lambda i: (i, 11)),
                  pl.BlockSpec((T, D), lambda i: (i, 12)), tok, tok, tok],
        out_specs=tok, out_shape=jax.ShapeDtypeStruct((S, D), BF16),
        compiler_params=_cparams(("parallel",)),
    )(z, z, z, yc, ym, yx)


def _merge_bwd(dmg, z, yc, ym, yx, D):
    S = z.shape[0]
    T = min(TOK_TILE, S)

    def body(dm_ref, g_ref, yc_ref, ym_ref, yx_ref, dyc_ref, dym_ref, dyx_ref, dg_ref):
        j = pl.program_id(1)
        for jj, (y_ref, dy_ref) in enumerate(((yc_ref, dyc_ref), (ym_ref, dym_ref), (yx_ref, dyx_ref))):
            @pl.when(j == jj)
            def _():
                sg = _sig(g_ref[...].astype(F32))
                dm = dm_ref[...].astype(F32)
                dy_ref[...] = (dm * sg).astype(BF16)
                dg_ref[...] = (dm * y_ref[...].astype(F32) * sg * (1.0 - sg)).astype(BF16)

    tok = pl.BlockSpec((T, D), lambda i, j: (i, 0))
    return pl.pallas_call(
        body, name="merge_bwd", grid=(S // T, 3),
        in_specs=[tok, pl.BlockSpec((T, D), lambda i, j: (i, 10 + j)), tok, tok, tok],
        out_specs=[tok, tok, tok, pl.BlockSpec((T, D), lambda i, j: (i, 10 + j))],
        out_shape=[jax.ShapeDtypeStruct((S, D), BF16)] * 3 + [jax.ShapeDtypeStruct((S, 13 * D), BF16)],
        compiler_params=_cparams(("arbitrary", "arbitrary")),
    )(dmg, z, yc, ym, yx)


def _post_loss(r, x, tgt, g_post):
    S, D = r.shape
    T = min(TOK_TILE, S)

    def body(r_ref, x_ref, t_ref, g_ref, dy_ref, dr_ref, dg_ref, loss_ref):
        i = pl.program_id(0)

        @pl.when(i == 0)
        def _():
            dg_ref[...] = jnp.zeros_like(dg_ref)
            loss_ref[...] = jnp.zeros_like(loss_ref)

        rv = r_ref[...]
        g = g_ref[...]
        rs = lax.rsqrt(jnp.mean(rv * rv, axis=-1, keepdims=True) + EPS)
        nrm = rv * rs
        e = x_ref[...] + nrm * g - t_ref[...]
        part = jnp.sum(jnp.sum(e * e, axis=-1, keepdims=True), axis=0, keepdims=True) * (0.5 / D)
        loss_ref[0:1, 0:1] += part
        dy = e * (1.0 / D)
        dy_ref[...] = dy
        dg_ref[0:1, :] += jnp.sum(dy * nrm, axis=0, keepdims=True)
        dn = dy * g
        dr_ref[...] = (rs * (dn - nrm * jnp.mean(dn * nrm, axis=-1, keepdims=True))).astype(BF16)

    tok = pl.BlockSpec((T, D), lambda i: (i, 0))
    return pl.pallas_call(
        body, name="post_loss", grid=(S // T,),
        in_specs=[tok, tok, tok, pl.BlockSpec((1, D), lambda i: (0, 0))],
        out_specs=[tok, tok, pl.BlockSpec((8, D), lambda i: (0, 0)), pl.BlockSpec((8, 128), lambda i: (0, 0))],
        out_shape=[jax.ShapeDtypeStruct((S, D), F32), jax.ShapeDtypeStruct((S, D), BF16),
                   jax.ShapeDtypeStruct((8, D), F32), jax.ShapeDtypeStruct((8, 128), F32)],
        compiler_params=_cparams(("arbitrary",)),
    )(r, x, tgt, g_post)


def _prenorm_bwd(dh, x, rstd, dy, g):
    S, D = x.shape
    T = min(TOK_TILE, S)

    def body(dh_ref, x_ref, r_ref, dy_ref, g_ref, dx_ref, dg_ref):
        i = pl.program_id(0)

        @pl.when(i == 0)
        def _():
            dg_ref[...] = jnp.zeros_like(dg_ref)

        rs = r_ref[...]
        xn = x_ref[...] * rs
        dhv = dh_ref[...]
        dg_ref[0:1, :] += jnp.sum(dhv * xn, axis=0, keepdims=True)
        dxn = dhv * g_ref[...]
        dx_ref[...] = dy_ref[...] + rs * (dxn - xn * jnp.mean(dxn * xn, axis=-1, keepdims=True))

    tok = pl.BlockSpec((T, D), lambda i: (i, 0))
    return pl.pallas_call(
        body, name="prenorm_bwd", grid=(S // T,),
        in_specs=[tok, tok, pl.BlockSpec((T, 1), lambda i: (i, 0)), tok, pl.BlockSpec((1, D), lambda i: (0, 0))],
        out_specs=[tok, pl.BlockSpec((8, D), lambda i: (0, 0))],
        out_shape=[jax.ShapeDtypeStruct((S, D), F32), jax.ShapeDtypeStruct((8, D), F32)],
        compiler_params=_cparams(("arbitrary",)),
    )(dh, x, rstd, dy, g)


def _conv_bwd1(dpc, z, u1, g_ln, b_ln, dz, D):
    S = z.shape[0]
    T = min(TOK_TILE, S)

    def body(dpc_ref, zc_ref, u1_ref, gln_ref, bln_ref, dz_any, du1_ref, st_ref, dzc_ref):
        del dz_any
        i = pl.program_id(0)

        @pl.when(i == 0)
        def _():
            st_ref[...] = jnp.zeros_like(st_ref)

        u1v = u1_ref[...]
        mu = jnp.mean(u1v, axis=-1, keepdims=True)
        xc = u1v - mu
        rs = lax.rsqrt(jnp.mean(xc * xc, axis=-1, keepdims=True) + EPS)
        xh = xc * rs
        g = gln_ref[...]
        ln = xh * g + bln_ref[...]
        zc = zc_ref[...].astype(F32)
        dpc_ = dpc_ref[...].astype(F32)
        dzc_ref[...] = (dpc_ * _silu(ln) * _dsilu(zc)).astype(BF16)
        dln = dpc_ * _silu(zc) * _dsilu(ln)
        st_ref[0:1, :] += jnp.sum(dln * xh, axis=0, keepdims=True)
        st_ref[1:2, :] += jnp.sum(dln, axis=0, keepdims=True)
        dxh = dln * g
        du1 = rs * (dxh - jnp.mean(dxh, axis=-1, keepdims=True) - xh * jnp.mean(dxh * xh, axis=-1, keepdims=True))
        du1_ref[...] = du1
        st_ref[2:3, :] += jnp.sum(du1, axis=0, keepdims=True)

    tok = pl.BlockSpec((T, D), lambda i: (i, 0))
    vec = pl.BlockSpec((1, D), lambda i: (0, 0))
    return pl.pallas_call(
        body, name="conv_bwd1", grid=(S // T,),
        in_specs=[tok, pl.BlockSpec((T, D), lambda i: (i, 2)), tok, vec, vec, pl.BlockSpec(memory_space=pl.ANY)],
        out_specs=[tok, pl.BlockSpec((8, D), lambda i: (0, 0)), pl.BlockSpec((T, D), lambda i: (i, 2))],
        out_shape=[jax.ShapeDtypeStruct((S, D), F32), jax.ShapeDtypeStruct((8, D), F32),
                   jax.ShapeDtypeStruct(dz.shape, dz.dtype)],
        input_output_aliases={5: 2},
        compiler_params=_cparams(("arbitrary",)),
    )(dpc, z, u1, g_ln, b_ln, dz)


def _conv_bwd2(du1, z, w_dw, dz, D):
    S = z.shape[0]
    T = min(TOK_TILE, S)
    hb = T // HALO
    last = S // HALO - 1

    n_steps = S // T

    def body(du_ref, dun_ref, a_ref, b_ref, ah_ref, bh_ref, w_ref, dz_any, dw_ref, dab_ref,
             dus, u0s, shd, shu, dwp):
        del dz_any
        i = pl.program_id(0)

        @pl.when(i == 0)
        def _():
            dwp[...] = jnp.zeros_like(dwp)

        dus[pl.ds(0, T), :] = du_ref[...]
        dus[pl.ds(T, HALO), :] = jnp.where(i < n_steps - 1, dun_ref[...], 0.0)
        dus[pl.ds(T + HALO, 8), :] = jnp.zeros((8, D), F32)
        u0s[pl.ds(HALO, T), :] = a_ref[...].astype(F32) * _sig(b_ref[...].astype(F32))
        halo = ah_ref[...].astype(F32) * _sig(bh_ref[...].astype(F32))
        u0s[pl.ds(0, HALO), :] = jnp.where(i > 0, halo, 0.0)
        u0s[pl.ds(T + HALO, 8), :] = jnp.zeros((8, D), F32)
        for c0 in range(0, D, LANE):
            cs = pl.ds(c0, LANE)
            _shift_strip(dus, shd, cs, T + HALO)
            _shift_strip(u0s, shu, cs, T + HALO)
            for r0 in range(0, T, ROW_BLK):
                rows = pl.ds(r0, ROW_BLK)
                du0 = jnp.zeros((ROW_BLK, LANE), F32)
                for j in range(CONV_W):
                    du0 = du0 + w_ref[pl.ds(j, 1), cs] * _tap(dus, shd, cs, CONV_W - 1 - j, r0)
                a = a_ref[rows, cs].astype(F32)
                sb = _sig(b_ref[rows, cs].astype(F32))
                dab_ref[rows, cs] = (du0 * sb).astype(BF16)
                dab_ref[rows, pl.ds(D + c0, LANE)] = (du0 * a * sb * (1.0 - sb)).astype(BF16)
                d = dus[rows, cs]
                for j in range(CONV_W):
                    prod = d * _tap(u0s, shu, cs, HALO - (CONV_W - 1) + j, r0)
                    part = prod[0:8]
                    for q in range(8, ROW_BLK, 8):
                        part = part + prod[q:q + 8]
                    dwp[pl.ds(8 * j, 8), cs] += part

        @pl.when(i == n_steps - 1)
        def _():
            dw_ref[...] = jnp.zeros_like(dw_ref)
            for j in range(CONV_W):
                dw_ref[pl.ds(j, 1), :] = jnp.sum(dwp[pl.ds(8 * j, 8), :], axis=0, keepdims=True)

    tok = pl.BlockSpec((T, D), lambda i: (i, 0))
    nxt = lambda i: (jnp.minimum((i + 1) * hb, last), 0)
    prev = lambda i: (jnp.maximum(i * hb - 1, 0), 0)
    prev_b = lambda i: (jnp.maximum(i * hb - 1, 0), 1)
    return pl.pallas_call(
        body, name="conv_bwd2", grid=(S // T,),
        in_specs=[tok, pl.BlockSpec((HALO, D), nxt), tok, pl.BlockSpec((T, D), lambda i: (i, 1)),
                  pl.BlockSpec((HALO, D), prev), pl.BlockSpec((HALO, D), prev_b),
                  pl.BlockSpec((32, D), lambda i: (0, 0)), pl.BlockSpec(memory_space=pl.ANY)],
        out_specs=[pl.BlockSpec((32, D), lambda i: (0, 0)), pl.BlockSpec((T, 2 * D), lambda i: (i, 0))],
        out_shape=[jax.ShapeDtypeStruct((32, D), F32), jax.ShapeDtypeStruct(dz.shape, dz.dtype)],
        input_output_aliases={7: 1},
        scratch_shapes=[pltpu.VMEM((T + HALO + 8, D), F32), pltpu.VMEM((T + HALO + 8, D), F32),
                        pltpu.VMEM((8, T + HALO, LANE), F32), pltpu.VMEM((8, T + HALO, LANE), F32),
                        pltpu.VMEM((8 * 32, D), F32)],
        compiler_params=_cparams(("arbitrary",)),
    )(du1, du1, z, z, z, z, w_dw, dz)


def _qk_bwd1(dqks, z, w_qk, D):
    S = z.shape[0]
    T = min(TOK_TILE, S)
    hb = T // HALO
    scale = (D // N_HEADS) ** -0.5
    W2 = 2 * D

    n_steps = S // T

    def body(d_ref, p0_ref, p1_ref, h0_ref, h1_ref, w_ref, dc_ref, dw_ref, ps, sh, dwp):
        i = pl.program_id(0)

        @pl.when(i == 0)
        def _():
            dwp[...] = jnp.zeros_like(dwp)

        _qk_fill(ps, p0_ref, p1_ref, h0_ref, h1_ref, T, D)
        for c0 in range(0, W2, LANE):
            cs = pl.ds(c0, LANE)
            _shift_strip(ps, sh, cs, T + HALO, _QK_RES)
            for r0 in range(0, T, ROW_BLK):
                rows = pl.ds(r0, ROW_BLK)
                dc = d_ref[rows, cs].astype(F32) * _dsilu(_qk_conv(ps, sh, w_ref, cs, r0))
                if c0 >= D:
                    dc = dc * scale
                dc_ref[rows, cs] = dc.astype(BF16)
                for j in range(QK_W):
                    prod = dc * _tap(ps, sh, cs, HALO - (QK_W - 1) + j, r0)
                    part = prod[0:8]
                    for q in range(8, ROW_BLK, 8):
                        part = part + prod[q:q + 8]
                    dwp[pl.ds(8 * j, 8), cs] += part

        @pl.when(i == n_steps - 1)
        def _():
            dw_ref[...] = jnp.zeros_like(dw_ref)
            for j in range(QK_W):
                dw_ref[pl.ds(j, 1), :] = jnp.sum(dwp[pl.ds(8 * j, 8), :], axis=0, keepdims=True)

    prev0 = lambda i: (jnp.maximum(i * hb - 1, 0), 3)
    prev1 = lambda i: (jnp.maximum(i * hb - 1, 0), 4)
    return pl.pallas_call(
        body, name="qk_bwd1", grid=(S // T,),
        in_specs=[pl.BlockSpec((T, W2), lambda i: (i, 0)),
                  pl.BlockSpec((T, D), lambda i: (i, 3)), pl.BlockSpec((T, D), lambda i: (i, 4)),
                  pl.BlockSpec((HALO, D), prev0), pl.BlockSpec((HALO, D), prev1),
                  pl.BlockSpec((8, W2), lambda i: (0, 0))],
        out_specs=[pl.BlockSpec((T, W2), lambda i: (i, 0)), pl.BlockSpec((8, W2), lambda i: (0, 0))],
        out_shape=[jax.ShapeDtypeStruct((S, W2), BF16), jax.ShapeDtypeStruct((8, W2), F32)],
        scratch_shapes=[pltpu.VMEM((T + HALO + 8, W2), F32), pltpu.VMEM((8, T + HALO, LANE), F32),
                        pltpu.VMEM((8 * QK_W, W2), F32)],
        compiler_params=_cparams(("arbitrary",)),
    )(dqks, z, z, z, z, w_qk)


def _qk_bwd2(dc, w_qk, dz, D):
    S = dc.shape[0]
    T = min(TOK_TILE, S)
    hb = T // HALO
    last = S // HALO - 1

    n_steps = S // T

    def body(d_ref, dn_ref, w_ref, dz_any, o_ref, ds, sh):
        del dz_any
        i = pl.program_id(0)
        ds[pl.ds(0, T), :] = d_ref[...].astype(F32)
        ds[pl.ds(T, HALO), :] = jnp.where(i < n_steps - 1, dn_ref[...].astype(F32), 0.0)
        ds[pl.ds(T + HALO, 8), :] = jnp.zeros((8, D), F32)
        for c0 in range(0, D, LANE):
            cs = pl.ds(c0, LANE)
            _shift_strip(ds, sh, cs, T + HALO, range(1, QK_W))
            for r0 in range(0, T, ROW_BLK):
                acc = jnp.zeros((ROW_BLK, LANE), F32)
                for j in range(QK_W):
                    acc = acc + w_ref[pl.ds(j, 1), cs] * _tap(ds, sh, cs, QK_W - 1 - j, r0)
                o_ref[pl.ds(r0, ROW_BLK), cs] = acc.astype(BF16)

    return pl.pallas_call(
        body, name="qk_bwd2", grid=(S // T, 2),
        in_specs=[pl.BlockSpec((T, D), lambda i, j: (i, j)),
                  pl.BlockSpec((HALO, D), lambda i, j: (jnp.minimum((i + 1) * hb, last), j)),
                  pl.BlockSpec((8, D), lambda i, j: (0, j)), pl.BlockSpec(memory_space=pl.ANY)],
        out_specs=pl.BlockSpec((T, D), lambda i, j: (i, 3 + j)),
        out_shape=jax.ShapeDtypeStruct(dz.shape, dz.dtype),
        input_output_aliases={3: 0},
        scratch_shapes=[pltpu.VMEM((T + HALO + 8, D), F32), pltpu.VMEM((8, T + HALO, LANE), F32)],
        compiler_params=_cparams(("arbitrary", "arbitrary")),
    )(dc, dc, w_qk, dz)


TILE_BUDGET = 12 * 1024 * 1024


def _tile_2d(R, C, elem_bytes):
    cands = [t for t in range(BF16_ROWS, R + 1, BF16_ROWS) if R % t == 0 and t * C * elem_bytes <= TILE_BUDGET]
    if cands:
        return max(cands), C
    if R * C * elem_bytes <= TILE_BUDGET or C % LANE:
        return R, C
    cands = [t for t in range(LANE, C + 1, LANE) if C % t == 0 and t * R * elem_bytes <= TILE_BUDGET]
    return R, (max(cands) if cands else LANE)


def _sum_parts(parts, name):
    n, R, C = parts.shape
    tr, tc = _tile_2d(R, C, n * parts.dtype.itemsize + 4)

    def body(p_ref, o_ref):
        g = p_ref[0].astype(F32)
        for s in range(1, n):
            g = g + p_ref[s].astype(F32)
        o_ref[...] = g

    return pl.pallas_call(
        body, name=name, grid=(R // tr, C // tc),
        in_specs=[pl.BlockSpec((n, tr, tc), lambda i, j: (0, i, j))],
        out_specs=pl.BlockSpec((tr, tc), lambda i, j: (i, j)),
        out_shape=jax.ShapeDtypeStruct((R, C), F32),
        compiler_params=_cparams(("parallel", "parallel")),
    )(parts)


def _adamw(parts, w, m, v, name):
    R, C = w.shape
    n = parts.shape[0]
    tr, tc = _tile_2d(R, C, 4 * 7 + n * parts.dtype.itemsize)
    c1 = 1.0 / (1.0 - ADAM_B1 ** ADAM_STEP)
    c2 = 1.0 / (1.0 - ADAM_B2 ** ADAM_STEP)

    def body(p_ref, w_ref, m_ref, v_ref, g_ref, d_ref, nm_ref, nv_ref):
        g = p_ref[0].astype(F32)
        for s in range(1, n):
            g = g + p_ref[s].astype(F32)
        g_ref[...] = g
        mn = ADAM_B1 * m_ref[...] + (1.0 - ADAM_B1) * g
        vn = ADAM_B2 * v_ref[...] + (1.0 - ADAM_B2) * (g * g)
        nm_ref[...] = mn
        nv_ref[...] = vn
        d_ref[...] = -ADAM_LR * ((mn * c1) / (jnp.sqrt(vn * c2) + ADAM_EPS) + ADAM_WD * w_ref[...])

    blk = pl.BlockSpec((tr, tc), lambda i, j: (i, j))
    return pl.pallas_call(
        body, name=name, grid=(R // tr, C // tc),
        in_specs=[pl.BlockSpec((n, tr, tc), lambda i, j: (0, i, j)), blk, blk, blk],
        out_specs=[blk, blk, blk, blk],
        out_shape=[jax.ShapeDtypeStruct((R, C), F32)] * 4,
        compiler_params=_cparams(("parallel", "parallel")),
    )(parts, w, m, v)


def _exchange(operands, scatter, name):
    n = len(operands)

    def body(*refs):
        copies = _xchg_copies(refs[:n], refs[n:2 * n], *refs[2 * n:], scatter)
        for cp in copies:
            cp.start()
        for cp in copies:
            cp.wait()

    hbm = pl.BlockSpec(memory_space=pltpu.HBM)
    return pl.pallas_call(
        body, name=name, in_specs=[hbm] * n, out_specs=[hbm] * n, out_shape=_xchg_out_shapes(operands, scatter),
        scratch_shapes=_xchg_sems(n), compiler_params=pltpu.CompilerParams(has_side_effects=True),
    )(*operands)


N_CHIP = 4


class _CoreGather:
    def __init__(self, core):
        self.core = core

    def out_shape(self, blk):
        return jax.ShapeDtypeStruct((N_CHIP,) + blk.shape, blk.dtype)

    @staticmethod
    def sems():
        return [pltpu.SemaphoreType.DMA((7,)), pltpu.SemaphoreType.DMA((7,)), pltpu.SemaphoreType.DMA]

    def _parts(self, src, out, send_sems, recv_sems, loc_sem):
        x, y, c = lax.axis_index("x"), lax.axis_index("y"), lax.axis_index("c")
        me, sibling = (x, y, c), (x, y, 1 - c)
        chips = [(1 - x, y), (x, 1 - y), (1 - x, 1 - y)]

        def copy(k, chip, to, from_src=False):
            slot = out.at[2 * chip[0] + chip[1]]
            return pltpu.make_async_remote_copy(
                src_ref=src if from_src else slot, dst_ref=slot, send_sem=send_sems.at[k], recv_sem=recv_sems.at[k],
                device_id=to, device_id_type=pl.DeviceIdType.MESH)

        mine = pltpu.make_async_copy(src, out.at[2 * x + y], loc_sem)
        first = [copy(0, (x, y), sibling, True)] + [copy(1 + j, (x, y), (*ch, c), True) for j, ch in enumerate(chips)]
        landed = [copy(1 + j, ch, me) for j, ch in enumerate(chips)]
        passed = [copy(4 + j, ch, sibling) for j, ch in enumerate(chips)]
        from_sibling = [copy(0, (x, y), me)] + [copy(4 + j, ch, me) for j, ch in enumerate(chips)]
        return c == self.core, mine, first, landed, passed, from_sibling

    def start(self, *refs):
        holder, mine, first, _, _, _ = self._parts(*refs)

        @pl.when(holder)
        def _():
            mine.start()
            for cp in first:
                cp.start()

    def forward(self, *refs):
        holder, _, _, landed, passed, _ = self._parts(*refs)

        @pl.when(holder)
        def _():
            for got, cp in zip(landed, passed):
                got.wait_recv()
                cp.start()

    def finish(self, *refs):
        holder, mine, first, _, passed, from_sibling = self._parts(*refs)

        @pl.when(holder)
        def _():
            for cp in first + passed:
                cp.wait_send()
            mine.wait()

        @pl.when(jnp.logical_not(holder))
        def _():
            for cp in from_sibling:
                cp.wait_recv()


def _core_gather(blk, core, name):
    g = _CoreGather(core)

    def body(*refs):
        g.start(*refs)
        g.forward(*refs)
        g.finish(*refs)

    hbm = pl.BlockSpec(memory_space=pltpu.HBM)
    return pl.pallas_call(
        body, name=name, in_specs=[hbm], out_specs=hbm, out_shape=g.out_shape(blk), scratch_shapes=g.sems(),
        compiler_params=pltpu.CompilerParams(has_side_effects=True),
    )(blk)


IN_TILE = 512


def _tile_index(tiles):
    jumps = [(i, tiles[i] - tiles[i - 1] - 1) for i in range(1, len(tiles)) if tiles[i] != tiles[i - 1] + 1]

    def f(t):
        j = t + tiles[0]
        for i, gap in jumps:
            j = j + jnp.where(t >= i, gap, 0)
        return j

    return f


def _in_proj_phase(h, w_rows, tiles, z_prev, name, gather=None, xchg=None):
    S, D = h.shape
    nt = len(tiles)
    assert w_rows.shape == (nt * IN_TILE, D)
    tm = _fit(S, 2048)
    grid = (nt, S // tm)
    col = _tile_index(tiles)
    n_in = 2 + (z_prev is not None)
    g_ops = [gather[0]] if gather else []
    x_ops = list(xchg[0]) if xchg else []
    n_g, n_x = len(g_ops), len(x_ops)

    def body(*refs):
        h_ref, w_ref = refs[0], refs[1]
        g_src = refs[n_in:n_in + n_g]
        x_src = refs[n_in + n_g:n_in + n_g + n_x]
        o = n_in + n_g + n_x
        z_ref = refs[o]
        g_dst = refs[o + 1:o + 1 + n_g]
        x_dst = refs[o + 1 + n_g:o + 1 + n_g + n_x]
        sems = refs[o + 1 + n_g + n_x:]
        g_sems, x_sems = sems[:3 * n_g], sems[3 * n_g:]
        t, i = pl.program_id(0), pl.program_id(1)
        first = (t == 0) & (i == 0)
        last = (t == nt - 1) & (i == grid[1] - 1)

        @pl.when(first)
        def _():
            if gather:
                gather[1].start(g_src[0], g_dst[0], *g_sems)
            if xchg:
                for cp in _xchg_copies(x_src, x_dst, *x_sems, xchg[1]):
                    cp.start()

        z_ref[...] = _dot_nt(h_ref[...], w_ref[...]).astype(z_ref.dtype)

        if gather:
            @pl.when((t == (3 * nt) // 4) & (i == 0))
            def _():
                gather[1].forward(g_src[0], g_dst[0], *g_sems)

        @pl.when(last)
        def _():
            if gather:
                gather[1].finish(g_src[0], g_dst[0], *g_sems)
            if xchg:
                for cp in _xchg_copies(x_src, x_dst, *x_sems, xchg[1]):
                    cp.wait()

    hbm = pl.BlockSpec(memory_space=pltpu.HBM)
    in_specs = [pl.BlockSpec((tm, D), lambda t, i: (i, 0)), pl.BlockSpec((IN_TILE, D), lambda t, i: (t, 0))]
    args = [h, w_rows]
    aliases = {}
    if z_prev is not None:
        in_specs.append(pl.BlockSpec(memory_space=pl.ANY))
        args.append(z_prev)
        aliases = {2: 0}
    in_specs += [hbm] * (n_g + n_x)
    args += g_ops + x_ops
    out_specs = [pl.BlockSpec((tm, IN_TILE), lambda t, i: (i, col(t)))] + [hbm] * (n_g + n_x)
    out_shape = [jax.ShapeDtypeStruct((S, 13 * D), BF16)]
    scratch = []
    if gather:
        out_shape.append(gather[1].out_shape(gather[0]))
        scratch += gather[1].sems()
    if xchg:
        out_shape += _xchg_out_shapes(x_ops, xchg[1])
        scratch += _xchg_sems(n_x)
    comm = bool(gather or xchg)
    res = pl.pallas_call(
        body, name=name, grid=grid, in_specs=in_specs, out_specs=out_specs, out_shape=out_shape,
        input_output_aliases=aliases, scratch_shapes=scratch,
        compiler_params=pltpu.CompilerParams(vmem_limit_bytes=VMEM_LIMIT, has_side_effects=comm,
                                             dimension_semantics=("arbitrary", "arbitrary")),
    )(*args)
    z = res[0]
    g_out = res[1] if gather else None
    x_out = list(res[1 + n_g:]) if xchg else None
    return z, g_out, x_out


LATE = ('w_conv_out', 'w_ml_out', 'w_xa_out', 'w_out', 'w_mem_kv', 'w_qk_conv', 'w_dw')
ROW_SHARDED = ('w_conv_out', 'w_ml_out', 'w_xa_out', 'w_out')
COL_SHARDED = ('w_in', 'w_mem_kv', 'w_qk_conv', 'w_dw')


def _cols_to_slabs(g):
    R, C8 = g.shape
    return g.reshape(R, N_DEV, C8 // N_DEV).transpose(1, 0, 2)


def _slabs_to_cols(a):
    n, R, C = a.shape
    return a.transpose(1, 0, 2).reshape(R, n * C)


def _assemble(name, slabs):
    if name in ROW_SHARDED:
        return slabs.reshape(slabs.shape[0] * slabs.shape[1], slabs.shape[2])
    return _slabs_to_cols(slabs)


def _to_slabs(name, g, dtype):
    if name in ROW_SHARDED:
        return g.reshape(N_DEV, g.shape[0] // N_DEV, g.shape[1]).astype(dtype)
    return _cols_to_slabs(g).astype(dtype)


def _w_in_rows(D):
    n = (13 * D + N_IF) // N_DEV
    gate_dev, gate_row = (8 * D) // n, (8 * D) % n
    assert gate_row + N_IF <= n
    height = -(-(n + BF16_ROWS - 1) // BF16_ROWS) * BF16_ROWS

    def first(p):
        return n * p - jnp.where(n * p >= 8 * D + N_IF, N_IF, 0)

    def start(p):
        return jnp.minimum((first(p) // BF16_ROWS) * BF16_ROWS, 13 * D - height)

    return n, gate_dev, gate_row, height, first, start


def _w_in_tiles(D):
    n, gate_dev, *_ = _w_in_rows(D)

    def rows(s):
        f = n * s - (N_IF if n * s >= 8 * D + N_IF else 0)
        return f, f + n - (N_IF if s == gate_dev else 0)

    tiles = range(13 * D // IN_TILE)
    inside = lambda j, s: rows(s)[0] <= j * IN_TILE and (j + 1) * IN_TILE <= rows(s)[1]
    early = [j for j in tiles if any(inside(j, s) for s in range(0, N_DEV, 2))]
    return early, [j for j in tiles if j not in early], rows


def _w_window_height(D):
    n = (13 * D + N_IF) // N_DEV
    return -(-(n + IN_TILE - 1) // IN_TILE) * IN_TILE


def _w_window(block_t, me, D):
    n, gate_dev, gate_row, _, first, _ = _w_in_rows(D)
    no_gates = jnp.concatenate([block_t[:gate_row], block_t[gate_row + N_IF:], jnp.zeros((N_IF, D), block_t.dtype)])
    mine = jnp.where(me == gate_dev, no_gates, block_t)
    off = first(me) % IN_TILE
    return lax.dynamic_update_slice(jnp.zeros((_w_window_height(D), D), block_t.dtype), mine, (off, 0))


def _w_rows_of_tiles(tiles, windows, D):
    _, _, rows = _w_in_tiles(D)
    runs = []
    for j in tiles:
        own = tuple(s for s in range(N_DEV) if rows(s)[0] < (j + 1) * IN_TILE and j * IN_TILE < rows(s)[1])
        if runs and runs[-1][0] == own and runs[-1][1] + runs[-1][2] == j:
            runs[-1] = (own, runs[-1][1], runs[-1][2] + 1)
        else:
            runs.append((own, j, 1))
    pieces = []
    for own, j, k in runs:
        cut = [windows[s][(j - rows(s)[0] // IN_TILE) * IN_TILE:(j + k - rows(s)[0] // IN_TILE) * IN_TILE] for s in own]
        pieces.append(functools.reduce(jnp.add, cut))
    return jnp.concatenate(pieces, axis=0)


def _interleave_tiles(parts):
    where = {j: (rows, i) for rows, tiles in parts for i, j in enumerate(tiles)}
    out, j, n_tiles = [], 0, len(where)
    while j < n_tiles:
        rows, i = where[j]
        k = 1
        while j + k in where and where[j + k][0] is rows and where[j + k][1] == i + k:
            k += 1
        out.append(rows[i * IN_TILE:(i + k) * IN_TILE])
        j += k
    return jnp.concatenate(out, axis=0)


def _local_step(x, mem, tgt, g_pre, w_early, w_rest, wifT, b_if, b_dw, g_ln, b_ln, g_head, g_mem, g_post, late, dist):
    S, D = x.shape
    L = min(ML_CHUNK, S)
    NC = S // L
    bif_row = jnp.zeros((1, IF_PAD), F32).at[0, :N_IF].set(b_if)
    bif_col = jnp.zeros((16, 1), F32).at[:N_IF, 0].set(b_if)
    early, rest, _ = _w_in_tiles(D)

    h, rstd = _rmsnorm_fwd(x, g_pre, "prenorm_fwd")
    if dist:
        my_block, rows_of_rest = w_rest
        z, north, _ = _in_proj_phase(h, w_early, early, None, "in_proj_early", gather=(my_block, _CoreGather(1)))
        w_rest = rows_of_rest(north)
        z, _, gathered = _in_proj_phase(h, w_rest, rest, z, "in_proj_rest",
                                        xchg=([late[n] for n in LATE] + [wifT], False))
        full = {n: _assemble(n, a) for n, a in zip(LATE, gathered)}
        wifT = jnp.zeros((IF_PAD, D), BF16).at[:N_IF].set(gathered[-1][_w_in_rows(D)[1], :N_IF])
    else:
        z, _, _ = _in_proj_phase(h, w_early, early, None, "in_proj_early")
        z, _, _ = _in_proj_phase(h, w_rest, rest, z, "in_proj_rest")
        full = late
    wpT = _interleave_tiles([(w_early, early), (w_rest, rest)])
    w_co, w_mo, w_xo, w_out, w_kv = (full[n] for n in LATE[:5])
    w_qk = jnp.zeros((8, 2 * D), F32).at[:QK_W].set(full['w_qk_conv'])
    w_dw = jnp.zeros((32, D), F32).at[:CONV_W].set(full['w_dw'])
    gcol = _mm(h, wifT, mode="nt", out_dtype=F32, tm=1024, tn=IF_PAD, tk=2048, name="gates_col")
    grow = _mm(wifT[:16], h, mode="nt", out_dtype=F32, tm=16, tn=1024, tk=2048, name="gates_row")
    grow = grow.reshape(16, NC, L).transpose(1, 0, 2)

    u1, pc = _conv_fwd(z, w_dw, b_dw, g_ln, b_ln, D)
    qks = _qk_fwd(z, w_qk, D)
    hm, cst, nst = _mlstm_fwd(qks, z, gcol, grow, bif_row, bif_col, D)
    pm = _ml_post_fwd(hm, z, g_head, D)
    memn, rstd_mem = _rmsnorm_fwd(mem, g_mem, "memnorm_fwd")
    kv = _mm(memn, w_kv, mode="nn", out_dtype=BF16, tm=256, tn=1024, tk=2048, name="mem_kv")
    px = _xattn_fwd(z, kv, D)

    proj = functools.partial(_mm, mode="nn", tm=1024, tn=1024, tk=2048)
    yc = proj(pc, w_co, out_dtype=BF16, name="conv_out")
    ym = proj(pm, w_mo, out_dtype=BF16, name="ml_out")
    yx = proj(px, w_xo, out_dtype=BF16, name="xa_out")
    mg = _merge_fwd(z, yc, ym, yx, D)
    r = proj(mg, w_out, out_dtype=F32, name="out_proj")
    dy, dr, dg_post, loss = _post_loss(r, x, tgt, g_post)

    dgrad = functools.partial(_mm, mode="nt", out_dtype=BF16, tm=1024, tn=1024, tk=2048)
    wgrad = functools.partial(_mm, mode="tn", out_dtype=F32, tm=1024, tn=1024, tk=2048)
    dmg = dgrad(dr, w_out, name="out_proj_dx")
    dw_out = wgrad(mg, dr, name="out_proj_dw")
    dyc, dym, dyx, dz = _merge_bwd(dmg, z, yc, ym, yx, D)
    dpc = dgrad(dyc, w_co, name="conv_out_dx")
    dw_co = wgrad(pc, dyc, name="conv_out_dw")
    dpm = dgrad(dym, w_mo, name="ml_out_dx")
    dw_mo = wgrad(pm, dym, name="ml_out_dw")
    dpx = dgrad(dyx, w_xo, name="xa_out_dx")
    dw_xo = wgrad(px, dyx, name="xa_out_dw")

    dkv, dz = _xattn_bwd(dpx, z, kv, dz, D)
    dkv_b = dkv.astype(BF16)
    dw_kv = wgrad(memn, dkv_b, name="mem_kv_dw")
    dmemn = _mm(dkv_b, w_kv, mode="nt", out_dtype=F32, tm=256, tn=1024, tk=2048, name="mem_kv_dx")
    dg_mem = _gain_grad(dmemn, mem, rstd_mem)

    du1, cstats, dz = _conv_bwd1(dpc, z, u1, g_ln, b_ln, dz, D)
    dw_dw, dz = _conv_bwd2(du1, z, w_dw, dz, D)

    dhm, dg_head, dz = _ml_post_bwd(dpm, hm, z, g_head, dz, D)
    dqks, dgates, dz = _mlstm_bwd(qks, z, gcol, grow, bif_row, bif_col, hm, dhm, cst, nst, dz, D)
    dc, dw_qk = _qk_bwd1(dqks, z, w_qk, D)
    dz = _qk_bwd2(dc, w_qk, dz, D)

    g = dict(w_conv_out=dw_co, w_ml_out=dw_mo, w_xa_out=dw_xo, w_out=dw_out, w_mem_kv=dw_kv,
             w_qk_conv=dw_qk[:QK_W], w_dw=dw_dw[:CONV_W])
    dgates_b = dgates.astype(BF16)
    in_proj_dw = functools.partial(_mm, dz, h, mode="tn", out_dtype=BF16 if dist else F32, tm=1024, tn=2048, tk=2048,
                                   name="in_proj_dw")
    in_proj_dx = functools.partial(_mm, dz, wpT, mode="nn", out_dtype=F32, tm=512, tn=2048, tk=2048,
                                   name="in_proj_dx")
    parts = {}
    if dist:
        send = [_to_slabs(n, g[n], BF16 if n in LATE[:5] else F32) for n in LATE]
        dwpT, recv = in_proj_dw(side=(send, True))
        parts.update(zip(LATE, recv))
        *_, height, _, start = _w_in_rows(D)
        dh, recv = in_proj_dx(side=([dwpT], _RowWindows(start, height)))
        parts['w_in'] = recv[0]
    else:
        dwpT = in_proj_dw()
        dh = in_proj_dx()
    dwifT = _mm(dgates_b, h, mode="tn", out_dtype=F32, tm=IF_PAD, tn=2048, tk=2048, name="gates_dw")
    if not dist:
        g['w_in'] = jnp.concatenate([dwpT[:8 * D], dwifT[:N_IF], dwpT[8 * D:]], axis=0).T
    dh = _mm(dgates_b, wifT, mode="nn", out_dtype=F32, tm=1024, tn=1024, tk=IF_PAD, name="gates_dx", acc_in=dh)
    dx, dg_pre = _prenorm_bwd(dh, x, rstd, dy, g_pre)

    db_if = _col_sum(dgates)[0, :N_IF]
    g.update(g_pre=dg_pre[0], b_if=db_if, b_dw=cstats[2], g_ln=cstats[0], b_ln=cstats[1], g_ml_head=dg_head[0],
             g_mem=dg_mem[0], g_post=dg_post[0], w_in_gates=dwifT[:N_IF])
    return loss[0, 0], dx, g, parts


WEIGHTS = ('g_pre', 'w_in', 'b_if', 'w_qk_conv', 'w_dw', 'b_dw', 'g_ln', 'b_ln', 'w_conv_out', 'g_ml_head',
           'w_ml_out', 'g_mem', 'w_mem_kv', 'w_xa_out', 'w_out', 'g_post')
VECTORS = ('g_pre', 'b_dw', 'g_ln', 'b_ln', 'g_ml_head', 'g_mem', 'g_post')


def kernel(x, mem, g_pre, w_in, b_if, w_qk_conv, w_dw, b_dw, g_ln, b_ln, w_conv_out, g_ml_head, w_ml_out, g_mem, w_mem_kv, w_xa_out, w_out, g_post, loss_target, m_g_pre, m_w_in, m_b_if, m_w_qk_conv, m_w_dw, m_b_dw, m_g_ln, m_b_ln, m_w_conv_out, m_g_ml_head, m_w_ml_out, m_g_mem, m_w_mem_kv, m_w_xa_out, m_w_out, m_g_post, v_g_pre, v_w_in, v_b_if, v_w_qk_conv, v_w_dw, v_b_dw, v_g_ln, v_b_ln, v_w_conv_out, v_g_ml_head, v_w_ml_out, v_g_mem, v_w_mem_kv, v_w_xa_out, v_w_out, v_g_post):
    S, D = x.shape[1], x.shape[2]
    w = dict(g_pre=g_pre, w_in=w_in, b_if=b_if, w_qk_conv=w_qk_conv, w_dw=w_dw, b_dw=b_dw, g_ln=g_ln, b_ln=b_ln,
             w_conv_out=w_conv_out, g_ml_head=g_ml_head, w_ml_out=w_ml_out, g_mem=g_mem, w_mem_kv=w_mem_kv,
             w_xa_out=w_xa_out, w_out=w_out, g_post=g_post)
    mom = dict(g_pre=m_g_pre, w_in=m_w_in, b_if=m_b_if, w_qk_conv=m_w_qk_conv, w_dw=m_w_dw, b_dw=m_b_dw,
               g_ln=m_g_ln, b_ln=m_b_ln, w_conv_out=m_w_conv_out, g_ml_head=m_g_ml_head, w_ml_out=m_w_ml_out,
               g_mem=m_g_mem, w_mem_kv=m_w_mem_kv, w_xa_out=m_w_xa_out, w_out=m_w_out, g_post=m_g_post)
    var = dict(g_pre=v_g_pre, w_in=v_w_in, b_if=v_b_if, w_qk_conv=v_w_qk_conv, w_dw=v_w_dw, b_dw=v_b_dw,
               g_ln=v_g_ln, b_ln=v_b_ln, w_conv_out=v_w_conv_out, g_ml_head=v_g_ml_head, w_ml_out=v_w_ml_out,
               g_mem=v_g_mem, w_mem_kv=v_w_mem_kv, w_xa_out=v_w_xa_out, w_out=v_w_out, g_post=v_g_post)

    n_rows, gate_dev, gate_row, height, first, start = _w_in_rows(D)
    me = 4 * lax.axis_index("x") + 2 * lax.axis_index("y") + lax.axis_index("c")

    my_block = w_in.T.astype(BF16)
    my_window = _w_window(my_block, me, D)
    south = _core_gather(my_window, 0, "gather_w_in_south")
    early, rest, _ = _w_in_tiles(D)
    south_windows = {2 * q: south[q] for q in range(N_CHIP)}
    w_early = _w_rows_of_tiles(early, south_windows, D)
    my_gates = jnp.zeros((BF16_ROWS, D), BF16).at[:N_IF].set(my_block[gate_row:gate_row + N_IF])

    def rows_of_rest(north):
        windows = dict(south_windows)
        windows.update({2 * q + 1: north[q] for q in range(N_CHIP)})
        return _w_rows_of_tiles(rest, windows, D)

    late = {n: w[n].astype(BF16) if n in LATE[:5] else w[n] for n in LATE}
    row = lambda a: a.reshape(1, D)

    loss, dx, g, parts = _local_step(
        x[0], mem[0], loss_target[0], row(g_pre), w_early, (my_window, rows_of_rest), my_gates, b_if, row(b_dw),
        row(g_ln), row(b_ln), row(g_ml_head), row(g_mem), row(g_post), late, True)

    pad = lambda a: jnp.zeros((D,), F32).at[:N_IF].set(a)
    vec = jnp.concatenate([jnp.stack([g[n] for n in VECTORS] + [pad(g['b_if'])]), g['w_in_gates']])
    parts['vec'] = _exchange([vec], False, "gather_vector_grads")[0]

    out = {}
    for n in LATE:
        out[n] = _adamw(parts[n], w[n], mom[n], var[n], "adamw_" + n)
    zeros8 = jnp.zeros((N_IF, D), F32)
    stack = lambda d: jnp.concatenate([jnp.stack([d[n] for n in VECTORS] + [pad(d['b_if'])]), zeros8])
    vres = _adamw(parts['vec'], stack(w), stack(mom), stack(var), "adamw_vectors")
    for i, n in enumerate(VECTORS):
        out[n] = [r[i] for r in vres]
    out['b_if'] = [r[len(VECTORS), :N_IF] for r in vres]
    gate_grad = vres[0][len(VECTORS) + 1:]

    win = _sum_parts(parts['w_in'], "sum_w_in_grads")
    seg = lax.dynamic_slice(win, (first(me) - start(me), 0), (n_rows, D))
    with_gates = jnp.concatenate([seg[:gate_row], gate_grad, seg[gate_row:n_rows - N_IF]], axis=0)
    g_in = jnp.where(me == gate_dev, with_gates, seg)
    out['w_in'] = [r.T for r in _adamw(g_in[None], w_in.T, m_w_in.T, v_w_in.T, "adamw_w_in")]

    loss = lax.psum(loss, ("x", "y", "c"))
    res = [loss, dx.reshape(1, S, D)]
    for k in range(4):
        res += [out[n][k] for n in WEIGHTS]
    return tuple(res)
```

```python
import functools
import math

import jax
import jax.numpy as jnp
from jax import lax
from jax.experimental import pallas as pl
from jax.experimental.pallas import tpu as pltpu

F32 = jnp.float32
BF16 = jnp.bfloat16

N_DEV = 8
N_HEADS = 4
CONV_W = 31
QK_W = 4
N_IF = 2 * N_HEADS
IF_PAD = 128
EPS = 1e-6
NEG = -1e30

ADAM_LR = 0.001
ADAM_B1 = 0.9
ADAM_B2 = 0.999
ADAM_EPS = 1e-08
ADAM_WD = 0.01
ADAM_STEP = 10

TOK_TILE = 256
ML_CHUNK = 256
HALO = 32
BF16_ROWS = 16
VMEM_LIMIT = 56 * 1024 * 1024


def _cparams(sem=None):
    kw = dict(vmem_limit_bytes=VMEM_LIMIT)
    if sem is not None:
        kw["dimension_semantics"] = sem
    return pltpu.CompilerParams(**kw)


def _sig(x):
    return jax.nn.sigmoid(x)


def _silu(x):
    return x * _sig(x)


def _dsilu(x):
    s = _sig(x)
    return s * (1.0 + x * (1.0 - s))


def _logsig(x):
    return jnp.minimum(x, 0.0) - jnp.log(1.0 + jnp.exp(-jnp.abs(x)))


def _dot(a, b, dims):
    return lax.dot_general(a, b, (dims, ((), ())), preferred_element_type=F32)


def _dot_nn(a, b):
    return _dot(a, b, ((1,), (0,)))


def _dot_nt(a, b):
    return _dot(a, b, ((1,), (1,)))


def _dot_tn(a, b):
    return _dot(a, b, ((0,), (0,)))


def _split3(a):
    hi = a.astype(BF16)
    r1 = a - hi.astype(F32)
    mid = r1.astype(BF16)
    lo = (r1 - mid.astype(F32)).astype(BF16)
    return hi, mid, lo


def _tri_left(tri, a):
    hi, mid, lo = _split3(a)
    return _dot_nn(tri, hi) + _dot_nn(tri, mid) + _dot_nn(tri, lo)


def _tri_right(a, tri):
    hi, mid, lo = _split3(a)
    return _dot_nn(hi, tri) + _dot_nn(mid, tri) + _dot_nn(lo, tri)


def _fit(n, t):
    if n <= t:
        return n
    return max(c for c in range(128, t + 1, 128) if n % c == 0)


def _peer(k):
    x, y, c = lax.axis_index("x"), lax.axis_index("y"), lax.axis_index("c")
    px = 1 - x if (k >> 2) & 1 else x
    py = 1 - y if (k >> 1) & 1 else y
    pc = 1 - c if k & 1 else c
    return (px, py, pc), 4 * px + 2 * py + pc


class _RowWindows:
    def __init__(self, start, height):
        self.start, self.height = start, height


def _xchg_copies(srcs, dsts, send_sems, recv_sems, loc_sems, scatter):
    _, me = _peer(0)

    def piece(src, p):
        if isinstance(scatter, _RowWindows):
            return src.at[pl.ds(pl.multiple_of(scatter.start(p), BF16_ROWS), scatter.height), :]
        return src.at[p] if scatter else src

    copies = []
    for t, (src, dst) in enumerate(zip(srcs, dsts)):
        copies.append(pltpu.make_async_copy(piece(src, me), dst.at[me], loc_sems.at[t]))
        for k in range(1, N_DEV):
            dev, p = _peer(k)
            s = t * (N_DEV - 1) + k - 1
            copies.append(pltpu.make_async_remote_copy(
                src_ref=piece(src, p), dst_ref=dst.at[me],
                send_sem=send_sems.at[s], recv_sem=recv_sems.at[s],
                device_id=dev, device_id_type=pl.DeviceIdType.MESH))
    return copies


def _xchg_out_shapes(operands, scatter):
    def one(a):
        if isinstance(scatter, _RowWindows):
            return (scatter.height, a.shape[1])
        return tuple(a.shape[1:] if scatter else a.shape)
    return [jax.ShapeDtypeStruct((N_DEV,) + one(a), a.dtype) for a in operands]


def _xchg_sems(n):
    return [pltpu.SemaphoreType.DMA((n * (N_DEV - 1),)), pltpu.SemaphoreType.DMA((n * (N_DEV - 1),)),
            pltpu.SemaphoreType.DMA((n,))]


def _mm(a, b, *, mode, out_dtype, tm, tn, tk, name, n_outer=False, acc_in=None, side=None):
    if mode == "nn":
        (M, K), (K2, N) = a.shape, b.shape
    elif mode == "nt":
        (M, K), (N, K2) = a.shape, b.shape
    else:
        (K, M), (K2, N) = a.shape, b.shape
    assert K == K2, (a.shape, b.shape, mode)
    tm, tn, tk = _fit(M, tm), _fit(N, tn), _fit(K, tk)
    assert M % tm == 0 and N % tn == 0 and K % tk == 0, (a.shape, b.shape, tm, tn, tk)
    nk = K // tk
    if n_outer:
        grid = (N // tn, M // tm, nk)
        ij = lambda g0, g1: (g1, g0)
    else:
        grid = (M // tm, N // tn, nk)
        ij = lambda g0, g1: (g0, g1)

    if mode == "nn":
        a_spec = pl.BlockSpec((tm, tk), lambda g0, g1, k: (ij(g0, g1)[0], k))
        b_spec = pl.BlockSpec((tk, tn), lambda g0, g1, k: (k, ij(g0, g1)[1]))
        dot = _dot_nn
    elif mode == "nt":
        a_spec = pl.BlockSpec((tm, tk), lambda g0, g1, k: (ij(g0, g1)[0], k))
        b_spec = pl.BlockSpec((tn, tk), lambda g0, g1, k: (ij(g0, g1)[1], k))
        dot = _dot_nt
    else:
        a_spec = pl.BlockSpec((tk, tm), lambda g0, g1, k: (k, ij(g0, g1)[0]))
        b_spec = pl.BlockSpec((tk, tn), lambda g0, g1, k: (k, ij(g0, g1)[1]))
        dot = _dot_tn
    o_spec = pl.BlockSpec((tm, tn), lambda g0, g1, k: ij(g0, g1))
    has_acc = acc_in is not None
    n_side = len(side[0]) if side is not None else 0
    scatter = side[1] if side is not None else False
    n_in = 2 + int(has_acc)

    def body(*refs):
        a_ref, b_ref = refs[0], refs[1]
        c_ref = refs[2] if has_acc else None
        srcs = refs[n_in:n_in + n_side]
        o_ref = refs[n_in + n_side]
        dsts = refs[n_in + n_side + 1:n_in + 2 * n_side + 1]
        scratch = refs[n_in + 2 * n_side + 1:]
        k = pl.program_id(2)
        if n_side:
            sems = scratch[-3:]
            first = (pl.program_id(0) == 0) & (pl.program_id(1) == 0) & (k == 0)
            last = (pl.program_id(0) == grid[0] - 1) & (pl.program_id(1) == grid[1] - 1) & (k == nk - 1)

            @pl.when(first)
            def _():
                for cp in _xchg_copies(srcs, dsts, *sems, scatter):
                    cp.start()

        p = dot(a_ref[...], b_ref[...])

        def finish(r):
            if has_acc:
                r = r + c_ref[...].astype(F32)
            o_ref[...] = r.astype(out_dtype)

        if nk == 1:
            finish(p)
        else:
            acc = scratch[0]

            @pl.when(k == 0)
            def _():
                acc[...] = p

            @pl.when((k > 0) & (k < nk - 1))
            def _():
                acc[...] += p

            @pl.when(k == nk - 1)
            def _():
                finish(acc[...] + p)

        if n_side:
            @pl.when(last)
            def _():
                for cp in _xchg_copies(srcs, dsts, *sems, scatter):
                    cp.wait()

    hbm = pl.BlockSpec(memory_space=pltpu.HBM)
    in_specs = [a_spec, b_spec]
    args = [a, b]
    if has_acc:
        in_specs.append(o_spec)
        args.append(acc_in)
    out_specs = [o_spec]
    out_shape = [jax.ShapeDtypeStruct((M, N), out_dtype)]
    scratch_shapes = [pltpu.VMEM((tm, tn), F32)] if nk > 1 else []
    if n_side:
        in_specs += [hbm] * n_side
        args += list(side[0])
        out_specs += [hbm] * n_side
        out_shape += _xchg_out_shapes(side[0], scatter)
        scratch_shapes += _xchg_sems(n_side)
        cp = pltpu.CompilerParams(vmem_limit_bytes=VMEM_LIMIT, has_side_effects=True,
                                  dimension_semantics=("arbitrary", "arbitrary", "arbitrary"))
    else:
        cp = _cparams(("parallel", "parallel", "arbitrary"))
    res = pl.pallas_call(
        body, name=name, grid=grid, in_specs=in_specs, out_specs=out_specs, out_shape=out_shape,
        scratch_shapes=scratch_shapes, compiler_params=cp,
    )(*args)
    return (res[0], list(res[1:])) if n_side else res[0]


def _rmsnorm_fwd(x, g, name):
    S, D = x.shape
    T = min(TOK_TILE, S)

    def body(x_ref, g_ref, h_ref, r_ref):
        xv = x_ref[...]
        r = lax.rsqrt(jnp.mean(xv * xv, axis=-1, keepdims=True) + EPS)
        h_ref[...] = (xv * r * g_ref[...]).astype(BF16)
        r_ref[...] = r

    return pl.pallas_call(
        body, name=name, grid=(S // T,),
        in_specs=[pl.BlockSpec((T, D), lambda i: (i, 0)), pl.BlockSpec((1, D), lambda i: (0, 0))],
        out_specs=[pl.BlockSpec((T, D), lambda i: (i, 0)), pl.BlockSpec((T, 1), lambda i: (i, 0))],
        out_shape=[jax.ShapeDtypeStruct((S, D), BF16), jax.ShapeDtypeStruct((S, 1), F32)],
        compiler_params=_cparams(("parallel",)),
    )(x, g)


LANE = 128
ROW_BLK = 64


def _shift_strip(src, sh, cs, nr, residues=range(1, 8)):
    for r in residues:
        sh[r] = src[pl.ds(r, nr), cs]


def _tap(src, sh, cs, o, r0):
    r = o % 8
    if r == 0:
        return src[pl.ds(o + r0, ROW_BLK), cs]
    return sh[r, pl.ds(o - r + r0, ROW_BLK), :]


def _conv_fwd(z, w_dw, b_dw, g_ln, b_ln, D):
    S = z.shape[0]
    T = min(TOK_TILE, S)
    hb = T // HALO

    def body(a_ref, b_ref, zc_ref, ah_ref, bh_ref, w_ref, bdw_ref, gln_ref, bln_ref, u1_ref, pc_ref, u0s, sh):
        i = pl.program_id(0)
        a = a_ref[...].astype(F32)
        b = b_ref[...].astype(F32)
        u0s[pl.ds(HALO, T), :] = a * _sig(b)
        halo = ah_ref[...].astype(F32) * _sig(bh_ref[...].astype(F32))
        u0s[pl.ds(0, HALO), :] = jnp.where(i > 0, halo, 0.0)
        u0s[pl.ds(T + HALO, 8), :] = jnp.zeros((8, D), F32)
        for c0 in range(0, D, LANE):
            cs = pl.ds(c0, LANE)
            _shift_strip(u0s, sh, cs, T + HALO)
            for r0 in range(0, T, ROW_BLK):
                acc = jnp.broadcast_to(bdw_ref[:, cs], (ROW_BLK, LANE))
                for j in range(CONV_W):
                    acc = acc + w_ref[pl.ds(j, 1), cs] * _tap(u0s, sh, cs, HALO - (CONV_W - 1) + j, r0)
                u1_ref[pl.ds(r0, ROW_BLK), cs] = acc
        acc = u1_ref[...]
        mu = jnp.mean(acc, axis=-1, keepdims=True)
        xc = acc - mu
        var = jnp.mean(xc * xc, axis=-1, keepdims=True)
        ln = xc * lax.rsqrt(var + EPS) * gln_ref[...] + bln_ref[...]
        pc_ref[...] = (_silu(ln) * _silu(zc_ref[...].astype(F32))).astype(BF16)

    prev = lambda i: (jnp.maximum(i * hb - 1, 0), 0)
    prev_b = lambda i: (jnp.maximum(i * hb - 1, 0), D // D)
    vec = pl.BlockSpec((1, D), lambda i: (0, 0))
    return pl.pallas_call(
        body, name="conv_fwd", grid=(S // T,),
        in_specs=[pl.BlockSpec((T, D), lambda i: (i, 0)), pl.BlockSpec((T, D), lambda i: (i, 1)),
                  pl.BlockSpec((T, D), lambda i: (i, 2)),
                  pl.BlockSpec((HALO, D), prev), pl.BlockSpec((HALO, D), prev_b),
                  pl.BlockSpec((32, D), lambda i: (0, 0)), vec, vec, vec],
        out_specs=[pl.BlockSpec((T, D), lambda i: (i, 0)), pl.BlockSpec((T, D), lambda i: (i, 0))],
        out_shape=[jax.ShapeDtypeStruct((S, D), F32), jax.ShapeDtypeStruct((S, D), BF16)],
        scratch_shapes=[pltpu.VMEM((T + HALO + 8, D), F32), pltpu.VMEM((8, T + HALO, LANE), F32)],
        compiler_params=_cparams(("parallel",)),
    )(z, z, z, z, z, w_dw, b_dw, g_ln, b_ln)


_QK_RES = sorted({(HALO - (QK_W - 1) + j) % 8 for j in range(QK_W)} - {0})


def _qk_fill(ps, p0_ref, p1_ref, h0_ref, h1_ref, T, D):
    i = pl.program_id(0)
    ps[pl.ds(HALO, T), pl.ds(0, D)] = p0_ref[...].astype(F32)
    ps[pl.ds(HALO, T), pl.ds(D, D)] = p1_ref[...].astype(F32)
    ps[pl.ds(0, HALO), pl.ds(0, D)] = jnp.where(i > 0, h0_ref[...].astype(F32), 0.0)
    ps[pl.ds(0, HALO), pl.ds(D, D)] = jnp.where(i > 0, h1_ref[...].astype(F32), 0.0)
    ps[pl.ds(T + HALO, 8), :] = jnp.zeros((8, 2 * D), F32)


def _qk_conv(ps, sh, w_ref, cs, r0):
    acc = jnp.zeros((ROW_BLK, LANE), F32)
    for j in range(QK_W):
        acc = acc + w_ref[pl.ds(j, 1), cs] * _tap(ps, sh, cs, HALO - (QK_W - 1) + j, r0)
    return acc


def _qk_fwd(z, w_qk, D):
    S = z.shape[0]
    T = min(TOK_TILE, S)
    hb = T // HALO
    scale = (D // N_HEADS) ** -0.5
    W2 = 2 * D

    def body(p0_ref, p1_ref, h0_ref, h1_ref, w_ref, o_ref, ps, sh):
        _qk_fill(ps, p0_ref, p1_ref, h0_ref, h1_ref, T, D)
        for c0 in range(0, W2, LANE):
            cs = pl.ds(c0, LANE)
            _shift_strip(ps, sh, cs, T + HALO, _QK_RES)
            for r0 in range(0, T, ROW_BLK):
                qk = _silu(_qk_conv(ps, sh, w_ref, cs, r0))
                o_ref[pl.ds(r0, ROW_BLK), cs] = (qk * scale if c0 >= D else qk).astype(BF16)

    prev0 = lambda i: (jnp.maximum(i * hb - 1, 0), 3)
    prev1 = lambda i: (jnp.maximum(i * hb - 1, 0), 4)
    return pl.pallas_call(
        body, name="qk_fwd", grid=(S // T,),
        in_specs=[pl.BlockSpec((T, D), lambda i: (i, 3)), pl.BlockSpec((T, D), lambda i: (i, 4)),
                  pl.BlockSpec((HALO, D), prev0), pl.BlockSpec((HALO, D), prev1),
                  pl.BlockSpec((8, W2), lambda i: (0, 0))],
        out_specs=pl.BlockSpec((T, W2), lambda i: (i, 0)),
        out_shape=jax.ShapeDtypeStruct((S, W2), BF16),
        scratch_shapes=[pltpu.VMEM((T + HALO + 8, W2), F32), pltpu.VMEM((8, T + HALO, LANE), F32)],
        compiler_params=_cparams(("parallel",)),
    )(z, z, z, z, w_qk)


def _ml_gates(gc_ref, gr_ref, bifr_ref, bifc_ref, L):
    gc = gc_ref[...] + bifr_ref[...]
    gr = gr_ref[...] + bifc_ref[...]
    ii = lax.broadcasted_iota(jnp.int32, (L, L), 0)
    jj = lax.broadcasted_iota(jnp.int32, (L, L), 1)
    tri = (jj <= ii).astype(BF16)
    triu = (ii <= jj).astype(BF16)
    bcol_all = _tri_left(tri, _logsig(gc))
    brow_all = _tri_right(_logsig(gr), triu)
    return gc, gr, bcol_all, brow_all, ii, jj


def _ml_intra(q, k, b_c, b_r, li_r, m, n_row, ii, jj):
    d = b_c - b_r + li_r
    dm = jnp.where(jj <= ii, d, NEG)
    inter = b_c + m
    m_row = jnp.maximum(inter, jnp.max(dm, axis=1, keepdims=True))
    w_intra = jnp.exp(dm - m_row)
    w_inter = jnp.exp(inter - m_row)
    qk = _dot_nt(q, k)
    s = qk * w_intra
    qn = jnp.sum(q.astype(F32) * n_row, axis=-1, keepdims=True)
    den = jnp.sum(s, axis=-1, keepdims=True) + w_inter * qn
    hdiv = jnp.maximum(jnp.abs(den), jnp.exp(-m_row))
    return qk, w_intra, w_inter, s, qn, den, hdiv, m_row


def _ml_state(b_c, li_c, m, L):
    b_last = b_c[L - 1:L, :]
    g_c = b_last - b_c + li_c
    m_new = jnp.maximum(b_last + m, jnp.max(g_c, axis=0, keepdims=True))
    decay = jnp.exp(b_last + m - m_new)
    wk = jnp.exp(g_c - m_new)
    return m_new, decay, wk


def _mlstm_fwd(qks, z, gcol, grow, bif_row, bif_col, D):
    S = qks.shape[0]
    L = min(ML_CHUNK, S)
    NC = S // L
    H = N_HEADS
    dh = D // H

    def body(q_ref, k_ref, v_ref, gc_ref, gr_ref, bifr_ref, bifc_ref, hm_ref, cst_ref, nst_ref, C, NM):
        c = pl.program_id(0)

        @pl.when(c == 0)
        def _():
            C[...] = jnp.zeros_like(C)
            NM[...] = jnp.zeros_like(NM)

        gc, gr, bcol_all, brow_all, ii, jj = _ml_gates(gc_ref, gr_ref.at[0], bifr_ref, bifc_ref, L)
        for h in range(H):
            hs = pl.ds(h * dh, dh)
            q = q_ref[:, hs]
            k = k_ref[:, hs]
            v = v_ref[:, hs]
            b_c = bcol_all[:, H + h:H + h + 1]
            li_c = gc[:, h:h + 1]
            b_r = brow_all[H + h:H + h + 1, :]
            li_r = gr[h:h + 1, :]
            n_row = NM[h, 0:1, :]
            m = NM[h, 1:2, 0:1]
            Cb = C[h].astype(BF16)
            cst_ref[0, h] = Cb
            nst_ref[0, h] = NM[h]
            qk, w_intra, w_inter, s, qn, den, hdiv, m_row = _ml_intra(q, k, b_c, b_r, li_r, m, n_row, ii, jj)
            num = _dot_nn(s.astype(BF16), v) + w_inter * _dot_nn(q, Cb)
            hm_ref[:, hs] = num / hdiv
            m_new, decay, wk = _ml_state(b_c, li_c, m, L)
            wkk = wk * k.astype(F32)
            C[h] = decay * C[h] + _dot_tn(wkk.astype(BF16), v)
            NM[h, 0:1, :] = decay * n_row + jnp.sum(wkk, axis=0, keepdims=True)
            NM[h, 1:2, :] = jnp.broadcast_to(m_new, (1, dh))

    return pl.pallas_call(
        body, name="mlstm_fwd", grid=(NC,),
        in_specs=[pl.BlockSpec((L, D), lambda c: (c, 0)), pl.BlockSpec((L, D), lambda c: (c, 1)),
                  pl.BlockSpec((L, D), lambda c: (c, 5)),
                  pl.BlockSpec((L, IF_PAD), lambda c: (c, 0)),
                  pl.BlockSpec((1, 16, L), lambda c: (c, 0, 0)),
                  pl.BlockSpec((1, IF_PAD), lambda c: (0, 0)), pl.BlockSpec((16, 1), lambda c: (0, 0))],
        out_specs=[pl.BlockSpec((L, D), lambda c: (c, 0)),
                   pl.BlockSpec((1, H, dh, dh), lambda c: (c, 0, 0, 0)),
                   pl.BlockSpec((1, H, 8, dh), lambda c: (c, 0, 0, 0))],
        out_shape=[jax.ShapeDtypeStruct((S, D), F32), jax.ShapeDtypeStruct((NC, H, dh, dh), BF16),
                   jax.ShapeDtypeStruct((NC, H, 8, dh), F32)],
        scratch_shapes=[pltpu.VMEM((H, dh, dh), F32), pltpu.VMEM((H, 8, dh), F32)],
        compiler_params=_cparams(("arbitrary",)),
    )(qks, qks, z, gcol, grow, bif_row, bif_col)


def _mlstm_bwd(qks, z, gcol, grow, bif_row, bif_col, hm, dhm, cst, nst, dz, D):
    S = qks.shape[0]
    L = min(ML_CHUNK, S)
    NC = S // L
    H = N_HEADS
    dh = D // H

    def body(q_ref, k_ref, v_ref, gc_ref, gr_ref, bifr_ref, bifc_ref, hm_ref, dhm_ref, cst_ref, nst_ref, dz_any,
             dqk_ref, dg_ref, dv_ref, dC, dN):
        del dz_any
        c = pl.program_id(0)

        @pl.when(c == 0)
        def _():
            dC[...] = jnp.zeros_like(dC)
            dN[...] = jnp.zeros_like(dN)

        gc, gr, bcol_all, brow_all, ii, jj = _ml_gates(gc_ref, gr_ref.at[0], bifr_ref, bifc_ref, L)
        eye = ii == jj
        lane = lax.broadcasted_iota(jnp.int32, (L, IF_PAD), 1)
        db_all = jnp.zeros((L, IF_PAD), F32)
        dli_all = jnp.zeros((L, IF_PAD), F32)
        last_row = lax.broadcasted_iota(jnp.int32, (L, 1), 0) == L - 1
        for h in range(H):
            hs = pl.ds(h * dh, dh)
            q = q_ref[:, hs]
            k = k_ref[:, hs]
            v = v_ref[:, hs]
            qf = q.astype(F32)
            kf = k.astype(F32)
            b_c = bcol_all[:, H + h:H + h + 1]
            li_c = gc[:, h:h + 1]
            b_r = brow_all[H + h:H + h + 1, :]
            li_r = gr[h:h + 1, :]
            n_row = nst_ref[0, h, 0:1, :]
            m = nst_ref[0, h, 1:2, 0:1]
            Cb = cst_ref[0, h]
            qk, w_intra, w_inter, s, qn, den, hdiv, m_row = _ml_intra(q, k, b_c, b_r, li_r, m, n_row, ii, jj)
            dh_ = dhm_ref[:, hs]
            hh = hm_ref[:, hs]
            dnum = dh_ / hdiv
            dhdiv = -jnp.sum(dh_ * hh, axis=-1, keepdims=True) / hdiv
            dden = jnp.where(jnp.abs(den) > jnp.exp(-m_row), dhdiv * jnp.sign(den), 0.0)
            dnum_b = dnum.astype(BF16)
            ds = _dot_nt(dnum_b, v) + dden
            s_b = s.astype(BF16)
            dv = _dot_tn(s_b, dnum_b)
            dnC = _dot_nt(dnum_b, Cb)
            dwi = dden * w_inter
            dw_inter = jnp.sum(qf * dnC, axis=-1, keepdims=True) + dden * qn
            dq = w_inter * dnC + dwi * n_row
            dC_out = _dot_tn((w_inter * qf).astype(BF16), dnum_b)
            dn_out = jnp.sum(dwi * qf, axis=0, keepdims=True)
            dqk = (ds * w_intra).astype(BF16)
            dq = dq + _dot_nn(dqk, k)
            dk = _dot_tn(dqk, q)
            dd = ds * s
            rs_c = jnp.sum(dd, axis=1, keepdims=True)
            cs_r = jnp.sum(dd, axis=0, keepdims=True)
            cs_c = jnp.sum(jnp.where(eye, cs_r, 0.0), axis=1, keepdims=True)
            dinter = dw_inter * w_inter
            m_new, decay, wk = _ml_state(b_c, li_c, m, L)
            dCn = dC[h]
            dCn_b = dCn.astype(BF16)
            dn_new = dN[h, 0:1, :]
            ddecay = (jnp.sum(jnp.sum(dCn * Cb.astype(F32), axis=1, keepdims=True), axis=0, keepdims=True)
                      + jnp.sum(dn_new * n_row, axis=1, keepdims=True))
            dwkk = _dot_nt(v, dCn_b) + dn_new
            wkk_b = (wk * kf).astype(BF16)
            dv = dv + _dot_nn(wkk_b, dCn_b)
            dk = dk + wk * dwkk
            dg = jnp.sum(dwkk * kf, axis=-1, keepdims=True) * wk
            db_last = ddecay * decay + jnp.sum(dg, axis=0, keepdims=True)
            dC[h] = decay * dCn + dC_out
            dN[h, 0:1, :] = decay * dn_new + dn_out
            db_c = rs_c - cs_c + dinter - dg + jnp.where(last_row, db_last, 0.0)
            dli_c = cs_c + dg
            db_all = jnp.where(lane == H + h, db_c, db_all)
            dli_all = jnp.where(lane == h, dli_c, dli_all)
            dqk_ref[:, hs] = dq.astype(BF16)
            dqk_ref[:, pl.ds(D + h * dh, dh)] = dk.astype(BF16)
            dv_ref[:, hs] = dv.astype(BF16)
        triu = (ii <= jj).astype(BF16)
        dlf = _tri_left(triu, db_all)
        dg_ref[...] = dli_all + dlf * (1.0 - _sig(gc))

    r = lambda c: NC - 1 - c
    n_in = 12
    return pl.pallas_call(
        body, name="mlstm_bwd", grid=(NC,),
        in_specs=[pl.BlockSpec((L, D), lambda c: (r(c), 0)), pl.BlockSpec((L, D), lambda c: (r(c), 1)),
                  pl.BlockSpec((L, D), lambda c: (r(c), 5)),
                  pl.BlockSpec((L, IF_PAD), lambda c: (r(c), 0)),
                  pl.BlockSpec((1, 16, L), lambda c: (r(c), 0, 0)),
                  pl.BlockSpec((1, IF_PAD), lambda c: (0, 0)), pl.BlockSpec((16, 1), lambda c: (0, 0)),
                  pl.BlockSpec((L, D), lambda c: (r(c), 0)), pl.BlockSpec((L, D), lambda c: (r(c), 0)),
                  pl.BlockSpec((1, H, dh, dh), lambda c: (r(c), 0, 0, 0)),
                  pl.BlockSpec((1, H, 8, dh), lambda c: (r(c), 0, 0, 0)),
                  pl.BlockSpec(memory_space=pl.ANY)],
        out_specs=[pl.BlockSpec((L, 2 * D), lambda c: (r(c), 0)),
                   pl.BlockSpec((L, IF_PAD), lambda c: (r(c), 0)),
                   pl.BlockSpec((L, D), lambda c: (r(c), 5))],
        out_shape=[jax.ShapeDtypeStruct((S, 2 * D), BF16),
                   jax.ShapeDtypeStruct((S, IF_PAD), F32), jax.ShapeDtypeStruct(dz.shape, dz.dtype)],
        input_output_aliases={n_in - 1: 2},
        scratch_shapes=[pltpu.VMEM((H, dh, dh), F32), pltpu.VMEM((H, 8, dh), F32)],
        compiler_params=_cparams(("arbitrary",)),
    )(qks, qks, z, gcol, grow, bif_row, bif_col, hm, dhm, cst, nst, dz)


def _ml_post_fwd(hm, z, g_head, D):
    S = hm.shape[0]
    T = min(TOK_TILE, S)
    dh = D // N_HEADS

    def body(hm_ref, o_ref, zm_ref, g_ref, pm_ref):
        for h in range(N_HEADS):
            hs = pl.ds(h * dh, dh)
            hg = _sig(o_ref[:, hs].astype(F32)) * hm_ref[:, hs]
            rs = lax.rsqrt(jnp.mean(hg * hg, axis=-1, keepdims=True) + EPS)
            pm_ref[:, hs] = (hg * rs * g_ref[:, hs] * _silu(zm_ref[:, hs].astype(F32))).astype(BF16)

    return pl.pallas_call(
        body, name="ml_post_fwd", grid=(S // T,),
        in_specs=[pl.BlockSpec((T, D), lambda i: (i, 0)), pl.BlockSpec((T, D), lambda i: (i, 6)),
                  pl.BlockSpec((T, D), lambda i: (i, 7)), pl.BlockSpec((1, D), lambda i: (0, 0))],
        out_specs=pl.BlockSpec((T, D), lambda i: (i, 0)),
        out_shape=jax.ShapeDtypeStruct((S, D), BF16),
        compiler_params=_cparams(("parallel",)),
    )(hm, z, z, g_head)


def _ml_post_bwd(dpm, hm, z, g_head, dz, D):
    S = hm.shape[0]
    T = min(TOK_TILE, S)
    dh = D // N_HEADS

    def body(dpm_ref, hm_ref, o_ref, zm_ref, g_ref, dz_any, dhm_ref, dg_ref, dozm_ref):
        del dz_any
        do_ref = dozm_ref.at[:, pl.ds(0, D)]
        dzm_ref = dozm_ref.at[:, pl.ds(D, D)]
        i = pl.program_id(0)

        @pl.when(i == 0)
        def _():
            dg_ref[...] = jnp.zeros_like(dg_ref)

        for h in range(N_HEADS):
            hs = pl.ds(h * dh, dh)
            o = o_ref[:, hs].astype(F32)
            zm = zm_ref[:, hs].astype(F32)
            hmv = hm_ref[:, hs]
            dpm_ = dpm_ref[:, hs].astype(F32)
            g = g_ref[:, hs]
            so = _sig(o)
            hg = so * hmv
            rs = lax.rsqrt(jnp.mean(hg * hg, axis=-1, keepdims=True) + EPS)
            xn = hg * rs
            dzm_ref[:, hs] = (dpm_ * xn * g * _dsilu(zm)).astype(BF16)
            dhn = dpm_ * _silu(zm)
            dg_ref[0:1, hs] += jnp.sum(dhn * xn, axis=0, keepdims=True)
            dxn = dhn * g
            dhg = rs * (dxn - xn * jnp.mean(dxn * xn, axis=-1, keepdims=True))
            do_ref[:, hs] = (dhg * hmv * so * (1.0 - so)).astype(BF16)
            dhm_ref[:, hs] = dhg * so

    return pl.pallas_call(
        body, name="ml_post_bwd", grid=(S // T,),
        in_specs=[pl.BlockSpec((T, D), lambda i: (i, 0)), pl.BlockSpec((T, D), lambda i: (i, 0)),
                  pl.BlockSpec((T, D), lambda i: (i, 6)), pl.BlockSpec((T, D), lambda i: (i, 7)),
                  pl.BlockSpec((1, D), lambda i: (0, 0)), pl.BlockSpec(memory_space=pl.ANY)],
        out_specs=[pl.BlockSpec((T, D), lambda i: (i, 0)), pl.BlockSpec((8, D), lambda i: (0, 0)),
                   pl.BlockSpec((T, 2 * D), lambda i: (i, 3))],
        out_shape=[jax.ShapeDtypeStruct((S, D), F32), jax.ShapeDtypeStruct((8, D), F32),
                   jax.ShapeDtypeStruct(dz.shape, dz.dtype)],
        input_output_aliases={5: 2},
        compiler_params=_cparams(("arbitrary",)),
    )(dpm, hm, z, z, g_head, dz)


def _xattn_probs(q, km_h, scale):
    s = _dot_nt(q, km_h) * scale
    e = jnp.exp(s - jnp.max(s, axis=-1, keepdims=True))
    return e / jnp.sum(e, axis=-1, keepdims=True)


def _xattn_fwd(z, kv, D):
    S = z.shape[0]
    M = kv.shape[0]
    T = min(TOK_TILE, S)
    dh = D // N_HEADS
    scale = dh ** -0.5

    def body(q_ref, zx_ref, km_ref, vm_ref, px_ref):
        for h in range(N_HEADS):
            hs = pl.ds(h * dh, dh)
            p = _xattn_probs(q_ref[:, hs], km_ref[:, hs], scale)
            o = _dot_nn(p.astype(BF16), vm_ref[:, hs])
            px_ref[:, hs] = (o * _silu(zx_ref[:, hs].astype(F32))).astype(BF16)

    return pl.pallas_call(
        body, name="xattn_fwd", grid=(S // T,),
        in_specs=[pl.BlockSpec((T, D), lambda i: (i, 8)), pl.BlockSpec((T, D), lambda i: (i, 9)),
                  pl.BlockSpec((M, D), lambda i: (0, 0)), pl.BlockSpec((M, D), lambda i: (0, 1))],
        out_specs=pl.BlockSpec((T, D), lambda i: (i, 0)),
        out_shape=jax.ShapeDtypeStruct((S, D), BF16),
        compiler_params=_cparams(("parallel",)),
    )(z, z, kv, kv)


def _xattn_bwd(dpx, z, kv, dz, D):
    S = z.shape[0]
    M = kv.shape[0]
    T = min(TOK_TILE, S)
    dh = D // N_HEADS
    scale = dh ** -0.5

    def body(dpx_ref, q_ref, zx_ref, km_ref, vm_ref, dz_any, dkv_ref, dqz_ref):
        del dz_any
        i = pl.program_id(0)

        @pl.when(i == 0)
        def _():
            dkv_ref[...] = jnp.zeros_like(dkv_ref)

        for h in range(N_HEADS):
            hs = pl.ds(h * dh, dh)
            q = q_ref[:, hs]
            zx = zx_ref[:, hs].astype(F32)
            dpx_ = dpx_ref[:, hs].astype(F32)
            p = _xattn_probs(q, km_ref[:, hs], scale)
            p_b = p.astype(BF16)
            o = _dot_nn(p_b, vm_ref[:, hs])
            dqz_ref[:, pl.ds(D + h * dh, dh)] = (dpx_ * o * _dsilu(zx)).astype(BF16)
            do = (dpx_ * _silu(zx)).astype(BF16)
            dp = _dot_nt(do, vm_ref[:, hs])
            dsc = (p * (dp - jnp.sum(p * dp, axis=-1, keepdims=True)) * scale).astype(BF16)
            dqz_ref[:, hs] = _dot_nn(dsc, km_ref[:, hs]).astype(BF16)
            dkv_ref[:, hs] += _dot_tn(dsc, q)
            dkv_ref[:, pl.ds(D + h * dh, dh)] += _dot_tn(p_b, do)

    return pl.pallas_call(
        body, name="xattn_bwd", grid=(S // T,),
        in_specs=[pl.BlockSpec((T, D), lambda i: (i, 0)),
                  pl.BlockSpec((T, D), lambda i: (i, 8)), pl.BlockSpec((T, D), lambda i: (i, 9)),
                  pl.BlockSpec((M, D), lambda i: (0, 0)), pl.BlockSpec((M, D), lambda i: (0, 1)),
                  pl.BlockSpec(memory_space=pl.ANY)],
        out_specs=[pl.BlockSpec((M, 2 * D), lambda i: (0, 0)), pl.BlockSpec((T, 2 * D), lambda i: (i, 4))],
        out_shape=[jax.ShapeDtypeStruct((M, 2 * D), F32), jax.ShapeDtypeStruct(dz.shape, dz.dtype)],
        input_output_aliases={5: 1},
        compiler_params=_cparams(("arbitrary",)),
    )(dpx, z, z, kv, kv, dz)


def _col_sum(a):
    S, C = a.shape
    T = min(1024, S)

    def body(a_ref, o_ref):
        @pl.when(pl.program_id(0) == 0)
        def _():
            o_ref[...] = jnp.zeros_like(o_ref)

        o_ref[0:1, :] += jnp.sum(a_ref[...], axis=0, keepdims=True)

    return pl.pallas_call(
        body, name="col_sum", grid=(S // T,),
        in_specs=[pl.BlockSpec((T, C), lambda i: (i, 0))], out_specs=pl.BlockSpec((8, C), lambda i: (0, 0)),
        out_shape=jax.ShapeDtypeStruct((8, C), F32), compiler_params=_cparams(("arbitrary",)),
    )(a)


def _gain_grad(dn, xin, rstd):
    R, D = dn.shape

    def body(dn_ref, x_ref, r_ref, o_ref):
        o_ref[...] = jnp.zeros_like(o_ref)
        o_ref[0:1, :] = jnp.sum(dn_ref[...] * x_ref[...] * r_ref[...], axis=0, keepdims=True)

    return pl.pallas_call(
        body, name="gain_grad", out_shape=jax.ShapeDtypeStruct((8, D), F32),
        compiler_params=_cparams(),
    )(dn, xin, rstd)


def _merge_fwd(z, yc, ym, yx, D):
    S = z.shape[0]
    T = min(TOK_TILE, S)

    def body(g0, g1, g2, yc_ref, ym_ref, yx_ref, o_ref):
        o_ref[...] = (_sig(g0[...].astype(F32)) * yc_ref[...].astype(F32)
                      + _sig(g1[...].astype(F32)) * ym_ref[...].astype(F32)
                      + _sig(g2[...].astype(F32)) * yx_ref[...].astype(F32)).astype(BF16)

    tok = pl.BlockSpec((T, D), lambda i: (i, 0))
    return pl.pallas_call(
        body, name="merge_fwd", grid=(S // T,),
        in_specs=[pl.BlockSpec((T, D), lambda i: (i, 10)), pl.BlockSpec((T, D), lambda i: (i, 11)),
                  pl.BlockSpec((T, D), lambda i: (i, 12)), tok, tok, tok],
        out_specs=tok, out_shape=jax.ShapeDtypeStruct((S, D), BF16),
        compiler_params=_cparams(("parallel",)),
    )(z, z, z, yc, ym, yx)


def _merge_bwd(dmg, z, yc, ym, yx, D):
    S = z.shape[0]
    T = min(TOK_TILE, S)

    def body(dm_ref, g_ref, yc_ref, ym_ref, yx_ref, dyc_ref, dym_ref, dyx_ref, dg_ref):
        j = pl.program_id(1)
        for jj, (y_ref, dy_ref) in enumerate(((yc_ref, dyc_ref), (ym_ref, dym_ref), (yx_ref, dyx_ref))):
            @pl.when(j == jj)
            def _():
                sg = _sig(g_ref[...].astype(F32))
                dm = dm_ref[...].astype(F32)
                dy_ref[...] = (dm * sg).astype(BF16)
                dg_ref[...] = (dm * y_ref[...].astype(F32) * sg * (1.0 - sg)).astype(BF16)

    tok = pl.BlockSpec((T, D), lambda i, j: (i, 0))
    return pl.pallas_call(
        body, name="merge_bwd", grid=(S // T, 3),
        in_specs=[tok, pl.BlockSpec((T, D), lambda i, j: (i, 10 + j)), tok, tok, tok],
        out_specs=[tok, tok, tok, pl.BlockSpec((T, D), lambda i, j: (i, 10 + j))],
        out_shape=[jax.ShapeDtypeStruct((S, D), BF16)] * 3 + [jax.ShapeDtypeStruct((S, 13 * D), BF16)],
        compiler_params=_cparams(("arbitrary", "arbitrary")),
    )(dmg, z, yc, ym, yx)


def _post_loss(r, x, tgt, g_post):
    S, D = r.shape
    T = min(TOK_TILE, S)

    def body(r_ref, x_ref, t_ref, g_ref, dy_ref, dr_ref, dg_ref, loss_ref):
        i = pl.program_id(0)

        @pl.when(i == 0)
        def _():
            dg_ref[...] = jnp.zeros_like(dg_ref)
            loss_ref[...] = jnp.zeros_like(loss_ref)

        rv = r_ref[...]
        g = g_ref[...]
        rs = lax.rsqrt(jnp.mean(rv * rv, axis=-1, keepdims=True) + EPS)
        nrm = rv * rs
        e = x_ref[...] + nrm * g - t_ref[...]
        part = jnp.sum(jnp.sum(e * e, axis=-1, keepdims=True), axis=0, keepdims=True) * (0.5 / D)
        loss_ref[0:1, 0:1] += part
        dy = e * (1.0 / D)
        dy_ref[...] = dy
        dg_ref[0:1, :] += jnp.sum(dy * nrm, axis=0, keepdims=True)
        dn = dy * g
        dr_ref[...] = (rs * (dn - nrm * jnp.mean(dn * nrm, axis=-1, keepdims=True))).astype(BF16)

    tok = pl.BlockSpec((T, D), lambda i: (i, 0))
    return pl.pallas_call(
        body, name="post_loss", grid=(S // T,),
        in_specs=[tok, tok, tok, pl.BlockSpec((1, D), lambda i: (0, 0))],
        out_specs=[tok, tok, pl.BlockSpec((8, D), lambda i: (0, 0)), pl.BlockSpec((8, 128), lambda i: (0, 0))],
        out_shape=[jax.ShapeDtypeStruct((S, D), F32), jax.ShapeDtypeStruct((S, D), BF16),
                   jax.ShapeDtypeStruct((8, D), F32), jax.ShapeDtypeStruct((8, 128), F32)],
        compiler_params=_cparams(("arbitrary",)),
    )(r, x, tgt, g_post)


def _conv_bwd1(dpc, z, u1, g_ln, b_ln, dz, D):
    S = z.shape[0]
    T = min(TOK_TILE, S)

    def body(dpc_ref, zc_ref, u1_ref, gln_ref, bln_ref, dz_any, du1_ref, st_ref, dzc_ref):
        del dz_any
        i = pl.program_id(0)

        @pl.when(i == 0)
        def _():
            st_ref[...] = jnp.zeros_like(st_ref)

        u1v = u1_ref[...]
        mu = jnp.mean(u1v, axis=-1, keepdims=True)
        xc = u1v - mu
        rs = lax.rsqrt(jnp.mean(xc * xc, axis=-1, keepdims=True) + EPS)
        xh = xc * rs
        g = gln_ref[...]
        ln = xh * g + bln_ref[...]
        zc = zc_ref[...].astype(F32)
        dpc_ = dpc_ref[...].astype(F32)
        dzc_ref[...] = (dpc_ * _silu(ln) * _dsilu(zc)).astype(BF16)
        dln = dpc_ * _silu(zc) * _dsilu(ln)
        st_ref[0:1, :] += jnp.sum(dln * xh, axis=0, keepdims=True)
        st_ref[1:2, :] += jnp.sum(dln, axis=0, keepdims=True)
        dxh = dln * g
        du1 = rs * (dxh - jnp.mean(dxh, axis=-1, keepdims=True) - xh * jnp.mean(dxh * xh, axis=-1, keepdims=True))
        du1_ref[...] = du1
        st_ref[2:3, :] += jnp.sum(du1, axis=0, keepdims=True)

    tok = pl.BlockSpec((T, D), lambda i: (i, 0))
    vec = pl.BlockSpec((1, D), lambda i: (0, 0))
    return pl.pallas_call(
        body, name="conv_bwd1", grid=(S // T,),
        in_specs=[tok, pl.BlockSpec((T, D), lambda i: (i, 2)), tok, vec, vec, pl.BlockSpec(memory_space=pl.ANY)],
        out_specs=[tok, pl.BlockSpec((8, D), lambda i: (0, 0)), pl.BlockSpec((T, D), lambda i: (i, 2))],
        out_shape=[jax.ShapeDtypeStruct((S, D), F32), jax.ShapeDtypeStruct((8, D), F32),
                   jax.ShapeDtypeStruct(dz.shape, dz.dtype)],
        input_output_aliases={5: 2},
        compiler_params=_cparams(("arbitrary",)),
    )(dpc, z, u1, g_ln, b_ln, dz)


def _conv_bwd2(du1, z, w_dw, dz, D):
    S = z.shape[0]
    T = min(TOK_TILE, S)
    hb = T // HALO
    last = S // HALO - 1

    n_steps = S // T

    def body(du_ref, dun_ref, a_ref, b_ref, ah_ref, bh_ref, w_ref, dz_any, dw_ref, dab_ref,
             dus, u0s, shd, shu, dwp):
        del dz_any
        i = pl.program_id(0)

        @pl.when(i == 0)
        def _():
            dwp[...] = jnp.zeros_like(dwp)

        dus[pl.ds(0, T), :] = du_ref[...]
        dus[pl.ds(T, HALO), :] = jnp.where(i < n_steps - 1, dun_ref[...], 0.0)
        dus[pl.ds(T + HALO, 8), :] = jnp.zeros((8, D), F32)
        u0s[pl.ds(HALO, T), :] = a_ref[...].astype(F32) * _sig(b_ref[...].astype(F32))
        halo = ah_ref[...].astype(F32) * _sig(bh_ref[...].astype(F32))
        u0s[pl.ds(0, HALO), :] = jnp.where(i > 0, halo, 0.0)
        u0s[pl.ds(T + HALO, 8), :] = jnp.zeros((8, D), F32)
        for c0 in range(0, D, LANE):
            cs = pl.ds(c0, LANE)
            _shift_strip(dus, shd, cs, T + HALO)
            _shift_strip(u0s, shu, cs, T + HALO)
            for r0 in range(0, T, ROW_BLK):
                rows = pl.ds(r0, ROW_BLK)
                du0 = jnp.zeros((ROW_BLK, LANE), F32)
                for j in range(CONV_W):
                    du0 = du0 + w_ref[pl.ds(j, 1), cs] * _tap(dus, shd, cs, CONV_W - 1 - j, r0)
                a = a_ref[rows, cs].astype(F32)
                sb = _sig(b_ref[rows, cs].astype(F32))
                dab_ref[rows, cs] = (du0 * sb).astype(BF16)
                dab_ref[rows, pl.ds(D + c0, LANE)] = (du0 * a * sb * (1.0 - sb)).astype(BF16)
                d = dus[rows, cs]
                for j in range(CONV_W):
                    prod = d * _tap(u0s, shu, cs, HALO - (CONV_W - 1) + j, r0)
                    part = prod[0:8]
                    for q in range(8, ROW_BLK, 8):
                        part = part + prod[q:q + 8]
                    dwp[pl.ds(8 * j, 8), cs] += part

        @pl.when(i == n_steps - 1)
        def _():
            dw_ref[...] = jnp.zeros_like(dw_ref)
            for j in range(CONV_W):
                dw_ref[pl.ds(j, 1), :] = jnp.sum(dwp[pl.ds(8 * j, 8), :], axis=0, keepdims=True)

    tok = pl.BlockSpec((T, D), lambda i: (i, 0))
    nxt = lambda i: (jnp.minimum((i + 1) * hb, last), 0)
    prev = lambda i: (jnp.maximum(i * hb - 1, 0), 0)
    prev_b = lambda i: (jnp.maximum(i * hb - 1, 0), 1)
    return pl.pallas_call(
        body, name="conv_bwd2", grid=(S // T,),
        in_specs=[tok, pl.BlockSpec((HALO, D), nxt), tok, pl.BlockSpec((T, D), lambda i: (i, 1)),
                  pl.BlockSpec((HALO, D), prev), pl.BlockSpec((HALO, D), prev_b),
                  pl.BlockSpec((32, D), lambda i: (0, 0)), pl.BlockSpec(memory_space=pl.ANY)],
        out_specs=[pl.BlockSpec((32, D), lambda i: (0, 0)), pl.BlockSpec((T, 2 * D), lambda i: (i, 0))],
        out_shape=[jax.ShapeDtypeStruct((32, D), F32), jax.ShapeDtypeStruct(dz.shape, dz.dtype)],
        input_output_aliases={7: 1},
        scratch_shapes=[pltpu.VMEM((T + HALO + 8, D), F32), pltpu.VMEM((T + HALO + 8, D), F32),
                        pltpu.VMEM((8, T + HALO, LANE), F32), pltpu.VMEM((8, T + HALO, LANE), F32),
                        pltpu.VMEM((8 * 32, D), F32)],
        compiler_params=_cparams(("arbitrary",)),
    )(du1, du1, z, z, z, z, w_dw, dz)


def _qk_bwd1(dqks, z, w_qk, D):
    S = z.shape[0]
    T = min(TOK_TILE, S)
    hb = T // HALO
    scale = (D // N_HEADS) ** -0.5
    W2 = 2 * D

    n_steps = S // T

    def body(d_ref, p0_ref, p1_ref, h0_ref, h1_ref, w_ref, dc_ref, dw_ref, ps, sh, dwp):
        i = pl.program_id(0)

        @pl.when(i == 0)
        def _():
            dwp[...] = jnp.zeros_like(dwp)

        _qk_fill(ps, p0_ref, p1_ref, h0_ref, h1_ref, T, D)
        for c0 in range(0, W2, LANE):
            cs = pl.ds(c0, LANE)
            _shift_strip(ps, sh, cs, T + HALO, _QK_RES)
            for r0 in range(0, T, ROW_BLK):
                rows = pl.ds(r0, ROW_BLK)
                dc = d_ref[rows, cs].astype(F32) * _dsilu(_qk_conv(ps, sh, w_ref, cs, r0))
                if c0 >= D:
                    dc = dc * scale
                dc_ref[rows, cs] = dc.astype(BF16)
                for j in range(QK_W):
                    prod = dc * _tap(ps, sh, cs, HALO - (QK_W - 1) + j, r0)
                    part = prod[0:8]
                    for q in range(8, ROW_BLK, 8):
                        part = part + prod[q:q + 8]
                    dwp[pl.ds(8 * j, 8), cs] += part

        @pl.when(i == n_steps - 1)
        def _():
            dw_ref[...] = jnp.zeros_like(dw_ref)
            for j in range(QK_W):
                dw_ref[pl.ds(j, 1), :] = jnp.sum(dwp[pl.ds(8 * j, 8), :], axis=0, keepdims=True)

    prev0 = lambda i: (jnp.maximum(i * hb - 1, 0), 3)
    prev1 = lambda i: (jnp.maximum(i * hb - 1, 0), 4)
    return pl.pallas_call(
        body, name="qk_bwd1", grid=(S // T,),
        in_specs=[pl.BlockSpec((T, W2), lambda i: (i, 0)),
                  pl.BlockSpec((T, D), lambda i: (i, 3)), pl.BlockSpec((T, D), lambda i: (i, 4)),
                  pl.BlockSpec((HALO, D), prev0), pl.BlockSpec((HALO, D), prev1),
                  pl.BlockSpec((8, W2), lambda i: (0, 0))],
        out_specs=[pl.BlockSpec((T, W2), lambda i: (i, 0)), pl.BlockSpec((8, W2), lambda i: (0, 0))],
        out_shape=[jax.ShapeDtypeStruct((S, W2), BF16), jax.ShapeDtypeStruct((8, W2), F32)],
        scratch_shapes=[pltpu.VMEM((T + HALO + 8, W2), F32), pltpu.VMEM((8, T + HALO, LANE), F32),
                        pltpu.VMEM((8 * QK_W, W2), F32)],
        compiler_params=_cparams(("arbitrary",)),
    )(dqks, z, z, z, z, w_qk)


def _qk_bwd2(dc, w_qk, dz, D):
    S = dc.shape[0]
    T = min(TOK_TILE, S)
    hb = T // HALO
    last = S // HALO - 1

    n_steps = S // T

    def body(d_ref, dn_ref, w_ref, dz_any, o_ref, ds, sh):
        del dz_any
        i = pl.program_id(0)
        ds[pl.ds(0, T), :] = d_ref[...].astype(F32)
        ds[pl.ds(T, HALO), :] = jnp.where(i < n_steps - 1, dn_ref[...].astype(F32), 0.0)
        ds[pl.ds(T + HALO, 8), :] = jnp.zeros((8, D), F32)
        for c0 in range(0, D, LANE):
            cs = pl.ds(c0, LANE)
            _shift_strip(ds, sh, cs, T + HALO, range(1, QK_W))
            for r0 in range(0, T, ROW_BLK):
                acc = jnp.zeros((ROW_BLK, LANE), F32)
                for j in range(QK_W):
                    acc = acc + w_ref[pl.ds(j, 1), cs] * _tap(ds, sh, cs, QK_W - 1 - j, r0)
                o_ref[pl.ds(r0, ROW_BLK), cs] = acc.astype(BF16)

    return pl.pallas_call(
        body, name="qk_bwd2", grid=(S // T, 2),
        in_specs=[pl.BlockSpec((T, D), lambda i, j: (i, j)),
                  pl.BlockSpec((HALO, D), lambda i, j: (jnp.minimum((i + 1) * hb, last), j)),
                  pl.BlockSpec((8, D), lambda i, j: (0, j)), pl.BlockSpec(memory_space=pl.ANY)],
        out_specs=pl.BlockSpec((T, D), lambda i, j: (i, 3 + j)),
        out_shape=jax.ShapeDtypeStruct(dz.shape, dz.dtype),
        input_output_aliases={3: 0},
        scratch_shapes=[pltpu.VMEM((T + HALO + 8, D), F32), pltpu.VMEM((8, T + HALO, LANE), F32)],
        compiler_params=_cparams(("arbitrary", "arbitrary")),
    )(dc, dc, w_qk, dz)


TILE_BUDGET = 12 * 1024 * 1024


def _tile_2d(R, C, elem_bytes):
    cands = [t for t in range(BF16_ROWS, R + 1, BF16_ROWS) if R % t == 0 and t * C * elem_bytes <= TILE_BUDGET]
    if cands:
        return max(cands), C
    if R * C * elem_bytes <= TILE_BUDGET or C % LANE:
        return R, C
    cands = [t for t in range(LANE, C + 1, LANE) if C % t == 0 and t * R * elem_bytes <= TILE_BUDGET]
    return R, (max(cands) if cands else LANE)


def _sum_parts(parts, name):
    n, R, C = parts.shape
    tr, tc = _tile_2d(R, C, n * parts.dtype.itemsize + 4)

    def body(p_ref, o_ref):
        g = p_ref[0].astype(F32)
        for s in range(1, n):
            g = g + p_ref[s].astype(F32)
        o_ref[...] = g

    return pl.pallas_call(
        body, name=name, grid=(R // tr, C // tc),
        in_specs=[pl.BlockSpec((n, tr, tc), lambda i, j: (0, i, j))],
        out_specs=pl.BlockSpec((tr, tc), lambda i, j: (i, j)),
        out_shape=jax.ShapeDtypeStruct((R, C), F32),
        compiler_params=_cparams(("parallel", "parallel")),
    )(parts)


def _adamw(parts, w, m, v, name):
    R, C = w.shape
    n = parts.shape[0]
    tr, tc = _tile_2d(R, C, 4 * 7 + n * parts.dtype.itemsize)
    c1 = 1.0 / (1.0 - ADAM_B1 ** ADAM_STEP)
    c2 = 1.0 / (1.0 - ADAM_B2 ** ADAM_STEP)

    def body(p_ref, w_ref, m_ref, v_ref, g_ref, d_ref, nm_ref, nv_ref):
        g = p_ref[0].astype(F32)
        for s in range(1, n):
            g = g + p_ref[s].astype(F32)
        g_ref[...] = g
        mn = ADAM_B1 * m_ref[...] + (1.0 - ADAM_B1) * g
        vn = ADAM_B2 * v_ref[...] + (1.0 - ADAM_B2) * (g * g)
        nm_ref[...] = mn
        nv_ref[...] = vn
        d_ref[...] = -ADAM_LR * ((mn * c1) / (jnp.sqrt(vn * c2) + ADAM_EPS) + ADAM_WD * w_ref[...])

    blk = pl.BlockSpec((tr, tc), lambda i, j: (i, j))
    return pl.pallas_call(
        body, name=name, grid=(R // tr, C // tc),
        in_specs=[pl.BlockSpec((n, tr, tc), lambda i, j: (0, i, j)), blk, blk, blk],
        out_specs=[blk, blk, blk, blk],
        out_shape=[jax.ShapeDtypeStruct((R, C), F32)] * 4,
        compiler_params=_cparams(("parallel", "parallel")),
    )(parts, w, m, v)


def _exchange(operands, scatter, name):
    n = len(operands)

    def body(*refs):
        copies = _xchg_copies(refs[:n], refs[n:2 * n], *refs[2 * n:], scatter)
        for cp in copies:
            cp.start()
        for cp in copies:
            cp.wait()

    hbm = pl.BlockSpec(memory_space=pltpu.HBM)
    return pl.pallas_call(
        body, name=name, in_specs=[hbm] * n, out_specs=[hbm] * n, out_shape=_xchg_out_shapes(operands, scatter),
        scratch_shapes=_xchg_sems(n), compiler_params=pltpu.CompilerParams(has_side_effects=True),
    )(*operands)


N_CHIP = 4


class _CoreGather:
    def __init__(self, core):
        self.core = core

    def out_shape(self, blk):
        return jax.ShapeDtypeStruct((N_CHIP,) + blk.shape, blk.dtype)

    @staticmethod
    def sems():
        return [pltpu.SemaphoreType.DMA((7,)), pltpu.SemaphoreType.DMA((7,)), pltpu.SemaphoreType.DMA]

    def _parts(self, src, out, send_sems, recv_sems, loc_sem):
        x, y, c = lax.axis_index("x"), lax.axis_index("y"), lax.axis_index("c")
        me, sibling = (x, y, c), (x, y, 1 - c)
        chips = [(1 - x, y), (x, 1 - y), (1 - x, 1 - y)]

        def copy(k, chip, to, from_src=False):
            slot = out.at[2 * chip[0] + chip[1]]
            return pltpu.make_async_remote_copy(
                src_ref=src if from_src else slot, dst_ref=slot, send_sem=send_sems.at[k], recv_sem=recv_sems.at[k],
                device_id=to, device_id_type=pl.DeviceIdType.MESH)

        mine = pltpu.make_async_copy(src, out.at[2 * x + y], loc_sem)
        first = [copy(0, (x, y), sibling, True)] + [copy(1 + j, (x, y), (*ch, c), True) for j, ch in enumerate(chips)]
        landed = [copy(1 + j, ch, me) for j, ch in enumerate(chips)]
        passed = [copy(4 + j, ch, sibling) for j, ch in enumerate(chips)]
        from_sibling = [copy(0, (x, y), me)] + [copy(4 + j, ch, me) for j, ch in enumerate(chips)]
        return c == self.core, mine, first, landed, passed, from_sibling

    def start(self, *refs):
        holder, mine, first, _, _, _ = self._parts(*refs)

        @pl.when(holder)
        def _():
            mine.start()
            for cp in first:
                cp.start()

    def forward(self, *refs):
        holder, _, _, landed, passed, _ = self._parts(*refs)

        @pl.when(holder)
        def _():
            for got, cp in zip(landed, passed):
                got.wait_recv()
                cp.start()

    def finish(self, *refs):
        holder, mine, first, _, passed, from_sibling = self._parts(*refs)

        @pl.when(holder)
        def _():
            for cp in first + passed:
                cp.wait_send()
            mine.wait()

        @pl.when(jnp.logical_not(holder))
        def _():
            for cp in from_sibling:
                cp.wait_recv()


def _gather2_copies(srcs, dsts, send_sems, recv_sems, loc_sems):
    x, y, c = lax.axis_index("x"), lax.axis_index("y"), lax.axis_index("c")
    me, sibling = (x, y, c), (x, y, 1 - c)
    chips = [(1 - x, y), (x, 1 - y), (1 - x, 1 - y)]
    mine, first, landed, passed, arriving = [], [], [], [], []
    for t, (src, dst) in enumerate(zip(srcs, dsts)):
        def copy(k, block, to, from_src=False, src=src, dst=dst, t=t):
            slot = dst.at[4 * block[0] + 2 * block[1] + block[2]]
            return pltpu.make_async_remote_copy(
                src_ref=src if from_src else slot, dst_ref=slot,
                send_sem=send_sems.at[7 * t + k], recv_sem=recv_sems.at[7 * t + k],
                device_id=to, device_id_type=pl.DeviceIdType.MESH)

        mine.append(pltpu.make_async_copy(src, dst.at[4 * x + 2 * y + c], loc_sems.at[t]))
        first += [copy(0, me, sibling, True)] + [copy(1 + j, me, (*ch, c), True) for j, ch in enumerate(chips)]
        landed += [copy(1 + j, (*ch, c), me) for j, ch in enumerate(chips)]
        passed += [copy(4 + j, (*ch, c), sibling) for j, ch in enumerate(chips)]
        arriving += [copy(0, sibling, me)] + [copy(4 + j, (*ch, 1 - c), me) for j, ch in enumerate(chips)]
    return mine, first, landed, passed, arriving


def _core_gather(blk, core, name):
    g = _CoreGather(core)

    def body(*refs):
        g.start(*refs)
        g.forward(*refs)
        g.finish(*refs)

    hbm = pl.BlockSpec(memory_space=pltpu.HBM)
    return pl.pallas_call(
        body, name=name, in_specs=[hbm], out_specs=hbm, out_shape=g.out_shape(blk), scratch_shapes=g.sems(),
        compiler_params=pltpu.CompilerParams(has_side_effects=True),
    )(blk)


IN_TILE = 512


def _tile_index(tiles):
    jumps = [(i, tiles[i] - tiles[i - 1] - 1) for i in range(1, len(tiles)) if tiles[i] != tiles[i - 1] + 1]

    def f(t):
        j = t + tiles[0]
        for i, gap in jumps:
            j = j + jnp.where(t >= i, gap, 0)
        return j

    return f


def _in_proj_phase(h, w_rows, tiles, z_prev, name, gather=None, xchg=None):
    S, D = h.shape
    nt = len(tiles)
    assert w_rows.shape == (nt * IN_TILE, D)
    tm = _fit(S, 2048)
    grid = (nt, S // tm)
    col = _tile_index(tiles)
    n_in = 2 + (z_prev is not None)
    g_ops = [gather[0]] if gather else []
    x_ops = list(xchg[0]) if xchg else []
    n_g, n_x = len(g_ops), len(x_ops)

    def body(*refs):
        h_ref, w_ref = refs[0], refs[1]
        g_src = refs[n_in:n_in + n_g]
        x_src = refs[n_in + n_g:n_in + n_g + n_x]
        o = n_in + n_g + n_x
        z_ref = refs[o]
        g_dst = refs[o + 1:o + 1 + n_g]
        x_dst = refs[o + 1 + n_g:o + 1 + n_g + n_x]
        sems = refs[o + 1 + n_g + n_x:]
        g_sems, x_sems = sems[:3 * n_g], sems[3 * n_g:]
        t, i = pl.program_id(0), pl.program_id(1)
        first = (t == 0) & (i == 0)
        last = (t == nt - 1) & (i == grid[1] - 1)

        @pl.when(first)
        def _():
            if gather:
                gather[1].start(g_src[0], g_dst[0], *g_sems)
            if xchg:
                mine, out, _, _, _ = _gather2_copies(x_src, x_dst, *x_sems)
                for cp in mine + out:
                    cp.start()

        z_ref[...] = _dot_nt(h_ref[...], w_ref[...]).astype(z_ref.dtype)

        @pl.when((t == (3 * nt) // 4) & (i == 0))
        def _():
            if gather:
                gather[1].forward(g_src[0], g_dst[0], *g_sems)
            if xchg:
                _, _, landed, passed, _ = _gather2_copies(x_src, x_dst, *x_sems)
                for got, cp in zip(landed, passed):
                    got.wait_recv()
                    cp.start()

        @pl.when(last)
        def _():
            if gather:
                gather[1].finish(g_src[0], g_dst[0], *g_sems)
            if xchg:
                mine, out, _, passed, arriving = _gather2_copies(x_src, x_dst, *x_sems)
                for cp in arriving:
                    cp.wait_recv()
                for cp in out + passed:
                    cp.wait_send()
                for cp in mine:
                    cp.wait()

    hbm = pl.BlockSpec(memory_space=pltpu.HBM)
    in_specs = [pl.BlockSpec((tm, D), lambda t, i: (i, 0)), pl.BlockSpec((IN_TILE, D), lambda t, i: (t, 0))]
    args = [h, w_rows]
    aliases = {}
    if z_prev is not None:
        in_specs.append(pl.BlockSpec(memory_space=pl.ANY))
        args.append(z_prev)
        aliases = {2: 0}
    in_specs += [hbm] * (n_g + n_x)
    args += g_ops + x_ops
    out_specs = [pl.BlockSpec((tm, IN_TILE), lambda t, i: (i, col(t)))] + [hbm] * (n_g + n_x)
    out_shape = [jax.ShapeDtypeStruct((S, 13 * D), BF16)]
    scratch = []
    if gather:
        out_shape.append(gather[1].out_shape(gather[0]))
        scratch += gather[1].sems()
    if xchg:
        out_shape += _xchg_out_shapes(x_ops, xchg[1])
        scratch += _xchg_sems(n_x)
    comm = bool(gather or xchg)
    res = pl.pallas_call(
        body, name=name, grid=grid, in_specs=in_specs, out_specs=out_specs, out_shape=out_shape,
        input_output_aliases=aliases, scratch_shapes=scratch,
        compiler_params=pltpu.CompilerParams(vmem_limit_bytes=VMEM_LIMIT, has_side_effects=comm,
                                             dimension_semantics=("arbitrary", "arbitrary")),
    )(*args)
    z = res[0]
    g_out = res[1] if gather else None
    x_out = list(res[1 + n_g:]) if xchg else None
    return z, g_out, x_out


def _in_proj_dx(dz, wpT, dgates, wifT, x, rstd, dy, g, side=None):
    S, K = dz.shape
    D = wpT.shape[1]
    tm, tk = _fit(S, 512), _fit(K, 2048)
    nm, nk = S // tm, K // tk
    n_side = len(side[0]) if side else 0

    rc = min(tm, 128)

    def body(*refs):
        a_ref, b_ref, dgt_ref, wif_ref, x_hbm, r_ref, dy_hbm, g_ref = refs[:8]
        srcs = refs[8:8 + n_side]
        dx_ref, dg_ref = refs[8 + n_side], refs[9 + n_side]
        dsts = refs[10 + n_side:10 + 2 * n_side]
        acc, xbuf, dybuf, ep_sems = refs[10 + 2 * n_side:14 + 2 * n_side]
        sems = refs[14 + 2 * n_side:]
        i, k = pl.program_id(0), pl.program_id(1)
        rows = pl.ds(pl.multiple_of(i * tm, tm), tm)
        fetch = [pltpu.make_async_copy(x_hbm.at[rows, :], xbuf, ep_sems.at[0]),
                 pltpu.make_async_copy(dy_hbm.at[rows, :], dybuf, ep_sems.at[1])]

        @pl.when((i == 0) & (k == 0))
        def _():
            dg_ref[...] = jnp.zeros_like(dg_ref)
            if n_side:
                for cp in _xchg_copies(srcs, dsts, *sems, side[1]):
                    cp.start()

        @pl.when(k == nk - 2)
        def _():
            for cp in fetch:
                cp.start()

        p = _dot_nn(a_ref[...], b_ref[...])

        @pl.when(k == 0)
        def _():
            acc[...] = p

        @pl.when((k > 0) & (k < nk - 1))
        def _():
            acc[...] += p

        @pl.when(k == nk - 1)
        def _():
            acc[...] += p + _dot_nn(dgt_ref[...], wif_ref[...])
            for cp in fetch:
                cp.wait()
            for r0 in range(0, tm, rc):
                rr = pl.ds(r0, rc)
                dh = acc[rr, :]
                rs = r_ref[rr, :]
                xn = xbuf[rr, :] * rs
                dg_ref[0:1, :] += jnp.sum(dh * xn, axis=0, keepdims=True)
                dxn = dh * g_ref[...]
                dx_ref[rr, :] = dybuf[rr, :] + rs * (dxn - xn * jnp.mean(dxn * xn, axis=-1, keepdims=True))

        if n_side:
            @pl.when((i == nm - 1) & (k == nk - 1))
            def _():
                for cp in _xchg_copies(srcs, dsts, *sems, side[1]):
                    cp.wait()

    assert nk > 1
    hbm = pl.BlockSpec(memory_space=pltpu.HBM)
    tok = pl.BlockSpec((tm, D), lambda i, k: (i, 0))
    in_specs = [pl.BlockSpec((tm, tk), lambda i, k: (i, k)), pl.BlockSpec((tk, D), lambda i, k: (k, 0)),
                pl.BlockSpec((tm, IF_PAD), lambda i, k: (i, 0)), pl.BlockSpec((IF_PAD, D), lambda i, k: (0, 0)),
                hbm, pl.BlockSpec((tm, 1), lambda i, k: (i, 0)), hbm, pl.BlockSpec((1, D), lambda i, k: (0, 0))]
    out_specs = [tok, pl.BlockSpec((8, D), lambda i, k: (0, 0))]
    out_shape = [jax.ShapeDtypeStruct((S, D), F32), jax.ShapeDtypeStruct((8, D), F32)]
    scratch = [pltpu.VMEM((tm, D), F32), pltpu.VMEM((tm, D), F32), pltpu.VMEM((tm, D), F32),
               pltpu.SemaphoreType.DMA((2,))]
    args = [dz, wpT, dgates, wifT, x, rstd, dy, g]
    if n_side:
        in_specs += [hbm] * n_side
        args += list(side[0])
        out_specs += [hbm] * n_side
        out_shape += _xchg_out_shapes(side[0], side[1])
        scratch += _xchg_sems(n_side)
    res = pl.pallas_call(
        body, name="in_proj_dx", grid=(nm, nk), in_specs=in_specs, out_specs=out_specs, out_shape=out_shape,
        scratch_shapes=scratch,
        compiler_params=pltpu.CompilerParams(vmem_limit_bytes=VMEM_LIMIT, has_side_effects=bool(n_side),
                                             dimension_semantics=("arbitrary", "arbitrary")),
    )(*args)
    return res[0], res[1], list(res[2:])


LATE = ('w_conv_out', 'w_ml_out', 'w_xa_out', 'w_out', 'w_mem_kv', 'w_qk_conv', 'w_dw')
ROW_SHARDED = ('w_conv_out', 'w_ml_out', 'w_xa_out', 'w_out')
COL_SHARDED = ('w_in', 'w_mem_kv', 'w_qk_conv', 'w_dw')


def _cols_to_slabs(g):
    R, C8 = g.shape
    return g.reshape(R, N_DEV, C8 // N_DEV).transpose(1, 0, 2)


def _slabs_to_cols(a):
    n, R, C = a.shape
    return a.transpose(1, 0, 2).reshape(R, n * C)


def _assemble(name, slabs):
    if name in ROW_SHARDED:
        return slabs.reshape(slabs.shape[0] * slabs.shape[1], slabs.shape[2])
    return _slabs_to_cols(slabs)


def _to_slabs(name, g, dtype):
    if name in ROW_SHARDED:
        return g.reshape(N_DEV, g.shape[0] // N_DEV, g.shape[1]).astype(dtype)
    return _cols_to_slabs(g).astype(dtype)


def _w_in_rows(D):
    n = (13 * D + N_IF) // N_DEV
    gate_dev, gate_row = (8 * D) // n, (8 * D) % n
    assert gate_row + N_IF <= n
    height = -(-(n + BF16_ROWS - 1) // BF16_ROWS) * BF16_ROWS

    def first(p):
        return n * p - jnp.where(n * p >= 8 * D + N_IF, N_IF, 0)

    def start(p):
        return jnp.minimum((first(p) // BF16_ROWS) * BF16_ROWS, 13 * D - height)

    return n, gate_dev, gate_row, height, first, start


def _w_in_tiles(D):
    n, gate_dev, *_ = _w_in_rows(D)

    def rows(s):
        f = n * s - (N_IF if n * s >= 8 * D + N_IF else 0)
        return f, f + n - (N_IF if s == gate_dev else 0)

    tiles = range(13 * D // IN_TILE)
    inside = lambda j, s: rows(s)[0] <= j * IN_TILE and (j + 1) * IN_TILE <= rows(s)[1]
    early = [j for j in tiles if any(inside(j, s) for s in range(0, N_DEV, 2))]
    return early, [j for j in tiles if j not in early], rows


def _w_window_height(D):
    n = (13 * D + N_IF) // N_DEV
    return -(-(n + IN_TILE - 1) // IN_TILE) * IN_TILE


def _w_window(block_t, me, D):
    n, gate_dev, gate_row, _, first, _ = _w_in_rows(D)
    no_gates = jnp.concatenate([block_t[:gate_row], block_t[gate_row + N_IF:], jnp.zeros((N_IF, D), block_t.dtype)])
    mine = jnp.where(me == gate_dev, no_gates, block_t)
    off = first(me) % IN_TILE
    return lax.dynamic_update_slice(jnp.zeros((_w_window_height(D), D), block_t.dtype), mine, (off, 0))


def _w_rows_of_tiles(tiles, windows, D):
    _, _, rows = _w_in_tiles(D)
    runs = []
    for j in tiles:
        own = tuple(s for s in range(N_DEV) if rows(s)[0] < (j + 1) * IN_TILE and j * IN_TILE < rows(s)[1])
        if runs and runs[-1][0] == own and runs[-1][1] + runs[-1][2] == j:
            runs[-1] = (own, runs[-1][1], runs[-1][2] + 1)
        else:
            runs.append((own, j, 1))
    pieces = []
    for own, j, k in runs:
        cut = [windows[s][(j - rows(s)[0] // IN_TILE) * IN_TILE:(j + k - rows(s)[0] // IN_TILE) * IN_TILE] for s in own]
        pieces.append(functools.reduce(jnp.add, cut))
    return jnp.concatenate(pieces, axis=0)


def _interleave_tiles(parts):
    where = {j: (rows, i) for rows, tiles in parts for i, j in enumerate(tiles)}
    out, j, n_tiles = [], 0, len(where)
    while j < n_tiles:
        rows, i = where[j]
        k = 1
        while j + k in where and where[j + k][0] is rows and where[j + k][1] == i + k:
            k += 1
        out.append(rows[i * IN_TILE:(i + k) * IN_TILE])
        j += k
    return jnp.concatenate(out, axis=0)


def _local_step(x, mem, tgt, g_pre, w_early, w_rest, wifT, b_if, b_dw, g_ln, b_ln, g_head, g_mem, g_post, late, dist):
    S, D = x.shape
    L = min(ML_CHUNK, S)
    NC = S // L
    bif_row = jnp.zeros((1, IF_PAD), F32).at[0, :N_IF].set(b_if)
    bif_col = jnp.zeros((16, 1), F32).at[:N_IF, 0].set(b_if)
    early, rest, _ = _w_in_tiles(D)

    h, rstd = _rmsnorm_fwd(x, g_pre, "prenorm_fwd")
    if dist:
        my_block, rows_of_rest = w_rest
        z, north, _ = _in_proj_phase(h, w_early, early, None, "in_proj_early", gather=(my_block, _CoreGather(1)))
        w_rest = rows_of_rest(north)
        z, _, gathered = _in_proj_phase(h, w_rest, rest, z, "in_proj_rest",
                                        xchg=([late[n] for n in LATE] + [wifT], False))
        full = {n: _assemble(n, a) for n, a in zip(LATE, gathered)}
        wifT = jnp.zeros((IF_PAD, D), BF16).at[:N_IF].set(gathered[-1][_w_in_rows(D)[1], :N_IF])
    else:
        z, _, _ = _in_proj_phase(h, w_early, early, None, "in_proj_early")
        z, _, _ = _in_proj_phase(h, w_rest, rest, z, "in_proj_rest")
        full = late
    wpT = _interleave_tiles([(w_early, early), (w_rest, rest)])
    w_co, w_mo, w_xo, w_out, w_kv = (full[n] for n in LATE[:5])
    w_qk = jnp.zeros((8, 2 * D), F32).at[:QK_W].set(full['w_qk_conv'])
    w_dw = jnp.zeros((32, D), F32).at[:CONV_W].set(full['w_dw'])
    gcol = _mm(h, wifT, mode="nt", out_dtype=F32, tm=1024, tn=IF_PAD, tk=2048, name="gates_col")
    grow = _mm(wifT[:16], h, mode="nt", out_dtype=F32, tm=16, tn=1024, tk=2048, name="gates_row")
    grow = grow.reshape(16, NC, L).transpose(1, 0, 2)

    u1, pc = _conv_fwd(z, w_dw, b_dw, g_ln, b_ln, D)
    qks = _qk_fwd(z, w_qk, D)
    hm, cst, nst = _mlstm_fwd(qks, z, gcol, grow, bif_row, bif_col, D)
    pm = _ml_post_fwd(hm, z, g_head, D)
    memn, rstd_mem = _rmsnorm_fwd(mem, g_mem, "memnorm_fwd")
    kv = _mm(memn, w_kv, mode="nn", out_dtype=BF16, tm=256, tn=1024, tk=2048, name="mem_kv")
    px = _xattn_fwd(z, kv, D)

    proj = functools.partial(_mm, mode="nn", tm=1024, tn=1024, tk=2048)
    yc = proj(pc, w_co, out_dtype=BF16, name="conv_out")
    ym = proj(pm, w_mo, out_dtype=BF16, name="ml_out")
    yx = proj(px, w_xo, out_dtype=BF16, name="xa_out")
    mg = _merge_fwd(z, yc, ym, yx, D)
    r = proj(mg, w_out, out_dtype=F32, name="out_proj")
    dy, dr, dg_post, loss = _post_loss(r, x, tgt, g_post)

    dgrad = functools.partial(_mm, mode="nt", out_dtype=BF16, tm=1024, tn=1024, tk=2048)
    wgrad = functools.partial(_mm, mode="tn", out_dtype=F32, tm=1024, tn=1024, tk=2048)
    dmg = dgrad(dr, w_out, name="out_proj_dx")
    dw_out = wgrad(mg, dr, name="out_proj_dw")
    dyc, dym, dyx, dz = _merge_bwd(dmg, z, yc, ym, yx, D)
    dpc = dgrad(dyc, w_co, name="conv_out_dx")
    dw_co = wgrad(pc, dyc, name="conv_out_dw")
    dpm = dgrad(dym, w_mo, name="ml_out_dx")
    dw_mo = wgrad(pm, dym, name="ml_out_dw")
    dpx = dgrad(dyx, w_xo, name="xa_out_dx")
    dw_xo = wgrad(px, dyx, name="xa_out_dw")

    dkv, dz = _xattn_bwd(dpx, z, kv, dz, D)
    dkv_b = dkv.astype(BF16)
    dw_kv = wgrad(memn, dkv_b, name="mem_kv_dw")
    dmemn = _mm(dkv_b, w_kv, mode="nt", out_dtype=F32, tm=256, tn=1024, tk=2048, name="mem_kv_dx")
    dg_mem = _gain_grad(dmemn, mem, rstd_mem)

    du1, cstats, dz = _conv_bwd1(dpc, z, u1, g_ln, b_ln, dz, D)
    dw_dw, dz = _conv_bwd2(du1, z, w_dw, dz, D)

    dhm, dg_head, dz = _ml_post_bwd(dpm, hm, z, g_head, dz, D)
    dqks, dgates, dz = _mlstm_bwd(qks, z, gcol, grow, bif_row, bif_col, hm, dhm, cst, nst, dz, D)
    dc, dw_qk = _qk_bwd1(dqks, z, w_qk, D)
    dz = _qk_bwd2(dc, w_qk, dz, D)

    g = dict(w_conv_out=dw_co, w_ml_out=dw_mo, w_xa_out=dw_xo, w_out=dw_out, w_mem_kv=dw_kv,
             w_qk_conv=dw_qk[:QK_W], w_dw=dw_dw[:CONV_W])
    dgates_b = dgates.astype(BF16)
    in_proj_dw = functools.partial(_mm, dz, h, mode="tn", out_dtype=BF16 if dist else F32, tm=1024, tn=2048, tk=2048,
                                   name="in_proj_dw")
    in_proj_dx = functools.partial(_in_proj_dx, dz, wpT, dgates_b, wifT, x, rstd, dy, g_pre)
    parts = {}
    if dist:
        send = [_to_slabs(n, g[n], BF16 if n in LATE[:5] else F32) for n in LATE]
        dwpT, recv = in_proj_dw(side=(send, True))
        parts.update(zip(LATE, recv))
        *_, height, _, start = _w_in_rows(D)
        dx, dg_pre, recv = in_proj_dx(side=([dwpT], _RowWindows(start, height)))
        parts['w_in'] = recv[0]
    else:
        dwpT = in_proj_dw()
        dx, dg_pre, _ = in_proj_dx()
    dwifT = _mm(dgates_b, h, mode="tn", out_dtype=F32, tm=IF_PAD, tn=2048, tk=2048, name="gates_dw")
    if not dist:
        g['w_in'] = jnp.concatenate([dwpT[:8 * D], dwifT[:N_IF], dwpT[8 * D:]], axis=0).T

    db_if = _col_sum(dgates)[0, :N_IF]
    g.update(g_pre=dg_pre[0], b_if=db_if, b_dw=cstats[2], g_ln=cstats[0], b_ln=cstats[1], g_ml_head=dg_head[0],
             g_mem=dg_mem[0], g_post=dg_post[0], w_in_gates=dwifT[:N_IF])
    return loss[0, 0], dx, g, parts


WEIGHTS = ('g_pre', 'w_in', 'b_if', 'w_qk_conv', 'w_dw', 'b_dw', 'g_ln', 'b_ln', 'w_conv_out', 'g_ml_head',
           'w_ml_out', 'g_mem', 'w_mem_kv', 'w_xa_out', 'w_out', 'g_post')
VECTORS = ('g_pre', 'b_dw', 'g_ln', 'b_ln', 'g_ml_head', 'g_mem', 'g_post')


def kernel(x, mem, g_pre, w_in, b_if, w_qk_conv, w_dw, b_dw, g_ln, b_ln, w_conv_out, g_ml_head, w_ml_out, g_mem, w_mem_kv, w_xa_out, w_out, g_post, loss_target, m_g_pre, m_w_in, m_b_if, m_w_qk_conv, m_w_dw, m_b_dw, m_g_ln, m_b_ln, m_w_conv_out, m_g_ml_head, m_w_ml_out, m_g_mem, m_w_mem_kv, m_w_xa_out, m_w_out, m_g_post, v_g_pre, v_w_in, v_b_if, v_w_qk_conv, v_w_dw, v_b_dw, v_g_ln, v_b_ln, v_w_conv_out, v_g_ml_head, v_w_ml_out, v_g_mem, v_w_mem_kv, v_w_xa_out, v_w_out, v_g_post):
    S, D = x.shape[1], x.shape[2]
    w = dict(g_pre=g_pre, w_in=w_in, b_if=b_if, w_qk_conv=w_qk_conv, w_dw=w_dw, b_dw=b_dw, g_ln=g_ln, b_ln=b_ln,
             w_conv_out=w_conv_out, g_ml_head=g_ml_head, w_ml_out=w_ml_out, g_mem=g_mem, w_mem_kv=w_mem_kv,
             w_xa_out=w_xa_out, w_out=w_out, g_post=g_post)
    mom = dict(g_pre=m_g_pre, w_in=m_w_in, b_if=m_b_if, w_qk_conv=m_w_qk_conv, w_dw=m_w_dw, b_dw=m_b_dw,
               g_ln=m_g_ln, b_ln=m_b_ln, w_conv_out=m_w_conv_out, g_ml_head=m_g_ml_head, w_ml_out=m_w_ml_out,
               g_mem=m_g_mem, w_mem_kv=m_w_mem_kv, w_xa_out=m_w_xa_out, w_out=m_w_out, g_post=m_g_post)
    var = dict(g_pre=v_g_pre, w_in=v_w_in, b_if=v_b_if, w_qk_conv=v_w_qk_conv, w_dw=v_w_dw, b_dw=v_b_dw,
               g_ln=v_g_ln, b_ln=v_b_ln, w_conv_out=v_w_conv_out, g_ml_head=v_g_ml_head, w_ml_out=v_w_ml_out,
               g_mem=v_g_mem, w_mem_kv=v_w_mem_kv, w_xa_out=v_w_xa_out, w_out=v_w_out, g_post=v_g_post)

    n_rows, gate_dev, gate_row, height, first, start = _w_in_rows(D)
    me = 4 * lax.axis_index("x") + 2 * lax.axis_index("y") + lax.axis_index("c")

    my_block = w_in.T.astype(BF16)
    my_window = _w_window(my_block, me, D)
    south = _core_gather(my_window, 0, "gather_w_in_south")
    early, rest, _ = _w_in_tiles(D)
    south_windows = {2 * q: south[q] for q in range(N_CHIP)}
    w_early = _w_rows_of_tiles(early, south_windows, D)
    my_gates = jnp.zeros((BF16_ROWS, D), BF16).at[:N_IF].set(my_block[gate_row:gate_row + N_IF])

    def rows_of_rest(north):
        windows = dict(south_windows)
        windows.update({2 * q + 1: north[q] for q in range(N_CHIP)})
        return _w_rows_of_tiles(rest, windows, D)

    late = {n: w[n].astype(BF16) if n in LATE[:5] else w[n] for n in LATE}
    row = lambda a: a.reshape(1, D)

    loss, dx, g, parts = _local_step(
        x[0], mem[0], loss_target[0], row(g_pre), w_early, (my_window, rows_of_rest), my_gates, b_if, row(b_dw),
        row(g_ln), row(b_ln), row(g_ml_head), row(g_mem), row(g_post), late, True)

    pad = lambda a: jnp.zeros((D,), F32).at[:N_IF].set(a)
    vec = jnp.concatenate([jnp.stack([g[n] for n in VECTORS] + [pad(g['b_if'])]), g['w_in_gates']])
    parts['vec'] = _exchange([vec], False, "gather_vector_grads")[0]

    out = {}
    for n in LATE:
        out[n] = _adamw(parts[n], w[n], mom[n], var[n], "adamw_" + n)
    zeros8 = jnp.zeros((N_IF, D), F32)
    stack = lambda d: jnp.concatenate([jnp.stack([d[n] for n in VECTORS] + [pad(d['b_if'])]), zeros8])
    vres = _adamw(parts['vec'], stack(w), stack(mom), stack(var), "adamw_vectors")
    for i, n in enumerate(VECTORS):
        out[n] = [r[i] for r in vres]
    out['b_if'] = [r[len(VECTORS), :N_IF] for r in vres]
    gate_grad = vres[0][len(VECTORS) + 1:]

    win = _sum_parts(parts['w_in'], "sum_w_in_grads")
    seg = lax.dynamic_slice(win, (first(me) - start(me), 0), (n_rows, D))
    with_gates = jnp.concatenate([seg[:gate_row], gate_grad, seg[gate_row:n_rows - N_IF]], axis=0)
    g_in = jnp.where(me == gate_dev, with_gates, seg)
    out['w_in'] = [r.T for r in _adamw(g_in[None], w_in.T, m_w_in.T, v_w_in.T, "adamw_w_in")]

    loss = lax.psum(loss, ("x", "y", "c"))
    res = [loss, dx.reshape(1, S, D)]
    for k in range(4):
        res += [out[n][k] for n in WEIGHTS]
    return tuple(res)
```

```python
import functools
import math

import jax
import jax.numpy as jnp
from jax import lax
from jax.experimental import pallas as pl
from jax.experimental.pallas import tpu as pltpu

F32 = jnp.float32
BF16 = jnp.bfloat16

N_DEV = 8
N_HEADS = 4
CONV_W = 31
QK_W = 4
N_IF = 2 * N_HEADS
IF_PAD = 128
EPS = 1e-6
NEG = -1e30

ADAM_LR = 0.001
ADAM_B1 = 0.9
ADAM_B2 = 0.999
ADAM_EPS = 1e-08
ADAM_WD = 0.01
ADAM_STEP = 10

TOK_TILE = 256
ML_CHUNK = 256
HALO = 32
BF16_ROWS = 16
VMEM_LIMIT = 56 * 1024 * 1024


def _cparams(sem=None):
    kw = dict(vmem_limit_bytes=VMEM_LIMIT)
    if sem is not None:
        kw["dimension_semantics"] = sem
    return pltpu.CompilerParams(**kw)


def _sig(x):
    return jax.nn.sigmoid(x)


def _silu(x):
    return x * _sig(x)


def _dsilu(x):
    s = _sig(x)
    return s * (1.0 + x * (1.0 - s))


def _logsig(x):
    return jnp.minimum(x, 0.0) - jnp.log(1.0 + jnp.exp(-jnp.abs(x)))


def _dot(a, b, dims):
    return lax.dot_general(a, b, (dims, ((), ())), preferred_element_type=F32)


def _dot_nn(a, b):
    return _dot(a, b, ((1,), (0,)))


def _dot_nt(a, b):
    return _dot(a, b, ((1,), (1,)))


def _dot_tn(a, b):
    return _dot(a, b, ((0,), (0,)))


def _split3(a):
    hi = a.astype(BF16)
    r1 = a - hi.astype(F32)
    mid = r1.astype(BF16)
    lo = (r1 - mid.astype(F32)).astype(BF16)
    return hi, mid, lo


def _tri_left(tri, a):
    hi, mid, lo = _split3(a)
    return _dot_nn(tri, hi) + _dot_nn(tri, mid) + _dot_nn(tri, lo)


def _tri_right(a, tri):
    hi, mid, lo = _split3(a)
    return _dot_nn(hi, tri) + _dot_nn(mid, tri) + _dot_nn(lo, tri)


def _fit(n, t):
    if n <= t:
        return n
    return max(c for c in range(128, t + 1, 128) if n % c == 0)


def _peer(k):
    x, y, c = lax.axis_index("x"), lax.axis_index("y"), lax.axis_index("c")
    px = 1 - x if (k >> 2) & 1 else x
    py = 1 - y if (k >> 1) & 1 else y
    pc = 1 - c if k & 1 else c
    return (px, py, pc), 4 * px + 2 * py + pc


class _RowWindows:
    def __init__(self, start, height):
        self.start, self.height = start, height


def _xchg_copies(srcs, dsts, send_sems, recv_sems, loc_sems, scatter):
    _, me = _peer(0)

    def piece(src, p):
        if isinstance(scatter, _RowWindows):
            return src.at[pl.ds(pl.multiple_of(scatter.start(p), BF16_ROWS), scatter.height), :]
        return src.at[p] if scatter else src

    copies = []
    for t, (src, dst) in enumerate(zip(srcs, dsts)):
        copies.append(pltpu.make_async_copy(piece(src, me), dst.at[me], loc_sems.at[t]))
        for k in range(1, N_DEV):
            dev, p = _peer(k)
            s = t * (N_DEV - 1) + k - 1
            copies.append(pltpu.make_async_remote_copy(
                src_ref=piece(src, p), dst_ref=dst.at[me],
                send_sem=send_sems.at[s], recv_sem=recv_sems.at[s],
                device_id=dev, device_id_type=pl.DeviceIdType.MESH))
    return copies


def _xchg_out_shapes(operands, scatter):
    def one(a):
        if isinstance(scatter, _RowWindows):
            return (scatter.height, a.shape[1])
        return tuple(a.shape[1:] if scatter else a.shape)
    return [jax.ShapeDtypeStruct((N_DEV,) + one(a), a.dtype) for a in operands]


def _xchg_sems(n):
    return [pltpu.SemaphoreType.DMA((n * (N_DEV - 1),)), pltpu.SemaphoreType.DMA((n * (N_DEV - 1),)),
            pltpu.SemaphoreType.DMA((n,))]


def _mm(a, b, *, mode, out_dtype, tm, tn, tk, name, n_outer=False, acc_in=None, side=None):
    if mode == "nn":
        (M, K), (K2, N) = a.shape, b.shape
    elif mode == "nt":
        (M, K), (N, K2) = a.shape, b.shape
    else:
        (K, M), (K2, N) = a.shape, b.shape
    assert K == K2, (a.shape, b.shape, mode)
    tm, tn, tk = _fit(M, tm), _fit(N, tn), _fit(K, tk)
    assert M % tm == 0 and N % tn == 0 and K % tk == 0, (a.shape, b.shape, tm, tn, tk)
    nk = K // tk
    if n_outer:
        grid = (N // tn, M // tm, nk)
        ij = lambda g0, g1: (g1, g0)
    else:
        grid = (M // tm, N // tn, nk)
        ij = lambda g0, g1: (g0, g1)

    if mode == "nn":
        a_spec = pl.BlockSpec((tm, tk), lambda g0, g1, k: (ij(g0, g1)[0], k))
        b_spec = pl.BlockSpec((tk, tn), lambda g0, g1, k: (k, ij(g0, g1)[1]))
        dot = _dot_nn
    elif mode == "nt":
        a_spec = pl.BlockSpec((tm, tk), lambda g0, g1, k: (ij(g0, g1)[0], k))
        b_spec = pl.BlockSpec((tn, tk), lambda g0, g1, k: (ij(g0, g1)[1], k))
        dot = _dot_nt
    else:
        a_spec = pl.BlockSpec((tk, tm), lambda g0, g1, k: (k, ij(g0, g1)[0]))
        b_spec = pl.BlockSpec((tk, tn), lambda g0, g1, k: (k, ij(g0, g1)[1]))
        dot = _dot_tn
    o_spec = pl.BlockSpec((tm, tn), lambda g0, g1, k: ij(g0, g1))
    has_acc = acc_in is not None
    n_side = len(side[0]) if side is not None else 0
    scatter = side[1] if side is not None else False
    n_in = 2 + int(has_acc)

    def body(*refs):
        a_ref, b_ref = refs[0], refs[1]
        c_ref = refs[2] if has_acc else None
        srcs = refs[n_in:n_in + n_side]
        o_ref = refs[n_in + n_side]
        dsts = refs[n_in + n_side + 1:n_in + 2 * n_side + 1]
        scratch = refs[n_in + 2 * n_side + 1:]
        k = pl.program_id(2)
        if n_side:
            sems = scratch[-3:]
            first = (pl.program_id(0) == 0) & (pl.program_id(1) == 0) & (k == 0)
            last = (pl.program_id(0) == grid[0] - 1) & (pl.program_id(1) == grid[1] - 1) & (k == nk - 1)

            @pl.when(first)
            def _():
                for cp in _xchg_copies(srcs, dsts, *sems, scatter):
                    cp.start()

        p = dot(a_ref[...], b_ref[...])

        def finish(r):
            if has_acc:
                r = r + c_ref[...].astype(F32)
            o_ref[...] = r.astype(out_dtype)

        if nk == 1:
            finish(p)
        else:
            acc = scratch[0]

            @pl.when(k == 0)
            def _():
                acc[...] = p

            @pl.when((k > 0) & (k < nk - 1))
            def _():
                acc[...] += p

            @pl.when(k == nk - 1)
            def _():
                finish(acc[...] + p)

        if n_side:
            @pl.when(last)
            def _():
                for cp in _xchg_copies(srcs, dsts, *sems, scatter):
                    cp.wait()

    hbm = pl.BlockSpec(memory_space=pltpu.HBM)
    in_specs = [a_spec, b_spec]
    args = [a, b]
    if has_acc:
        in_specs.append(o_spec)
        args.append(acc_in)
    out_specs = [o_spec]
    out_shape = [jax.ShapeDtypeStruct((M, N), out_dtype)]
    scratch_shapes = [pltpu.VMEM((tm, tn), F32)] if nk > 1 else []
    if n_side:
        in_specs += [hbm] * n_side
        args += list(side[0])
        out_specs += [hbm] * n_side
        out_shape += _xchg_out_shapes(side[0], scatter)
        scratch_shapes += _xchg_sems(n_side)
        cp = pltpu.CompilerParams(vmem_limit_bytes=VMEM_LIMIT, has_side_effects=True,
                                  dimension_semantics=("arbitrary", "arbitrary", "arbitrary"))
    else:
        cp = _cparams(("parallel", "parallel", "arbitrary"))
    res = pl.pallas_call(
        body, name=name, grid=grid, in_specs=in_specs, out_specs=out_specs, out_shape=out_shape,
        scratch_shapes=scratch_shapes, compiler_params=cp,
    )(*args)
    return (res[0], list(res[1:])) if n_side else res[0]


def _rmsnorm_fwd(x, g, name):
    S, D = x.shape
    T = min(TOK_TILE, S)

    def body(x_ref, g_ref, h_ref, r_ref):
        xv = x_ref[...]
        r = lax.rsqrt(jnp.mean(xv * xv, axis=-1, keepdims=True) + EPS)
        h_ref[...] = (xv * r * g_ref[...]).astype(BF16)
        r_ref[...] = r

    return pl.pallas_call(
        body, name=name, grid=(S // T,),
        in_specs=[pl.BlockSpec((T, D), lambda i: (i, 0)), pl.BlockSpec((1, D), lambda i: (0, 0))],
        out_specs=[pl.BlockSpec((T, D), lambda i: (i, 0)), pl.BlockSpec((T, 1), lambda i: (i, 0))],
        out_shape=[jax.ShapeDtypeStruct((S, D), BF16), jax.ShapeDtypeStruct((S, 1), F32)],
        compiler_params=_cparams(("parallel",)),
    )(x, g)


LANE = 128
ROW_BLK = 64


def _shift_strip(src, sh, cs, nr, residues=range(1, 8)):
    for r in residues:
        sh[r] = src[pl.ds(r, nr), cs]


def _tap(src, sh, cs, o, r0):
    r = o % 8
    if r == 0:
        return src[pl.ds(o + r0, ROW_BLK), cs]
    return sh[r, pl.ds(o - r + r0, ROW_BLK), :]


def _conv_fwd(z, w_dw, b_dw, g_ln, b_ln, D):
    S = z.shape[0]
    T = min(TOK_TILE, S)
    hb = T // HALO

    def body(a_ref, b_ref, zc_ref, ah_ref, bh_ref, w_ref, bdw_ref, gln_ref, bln_ref, u1_ref, pc_ref, u0s, sh):
        i = pl.program_id(0)
        a = a_ref[...].astype(F32)
        b = b_ref[...].astype(F32)
        u0s[pl.ds(HALO, T), :] = a * _sig(b)
        halo = ah_ref[...].astype(F32) * _sig(bh_ref[...].astype(F32))
        u0s[pl.ds(0, HALO), :] = jnp.where(i > 0, halo, 0.0)
        u0s[pl.ds(T + HALO, 8), :] = jnp.zeros((8, D), F32)
        for c0 in range(0, D, LANE):
            cs = pl.ds(c0, LANE)
            _shift_strip(u0s, sh, cs, T + HALO)
            for r0 in range(0, T, ROW_BLK):
                acc = jnp.broadcast_to(bdw_ref[:, cs], (ROW_BLK, LANE))
                for j in range(CONV_W):
                    acc = acc + w_ref[pl.ds(j, 1), cs] * _tap(u0s, sh, cs, HALO - (CONV_W - 1) + j, r0)
                u1_ref[pl.ds(r0, ROW_BLK), cs] = acc
        acc = u1_ref[...]
        mu = jnp.mean(acc, axis=-1, keepdims=True)
        xc = acc - mu
        var = jnp.mean(xc * xc, axis=-1, keepdims=True)
        ln = xc * lax.rsqrt(var + EPS) * gln_ref[...] + bln_ref[...]
        pc_ref[...] = (_silu(ln) * _silu(zc_ref[...].astype(F32))).astype(BF16)

    prev = lambda i: (jnp.maximum(i * hb - 1, 0), 0)
    prev_b = lambda i: (jnp.maximum(i * hb - 1, 0), D // D)
    vec = pl.BlockSpec((1, D), lambda i: (0, 0))
    return pl.pallas_call(
        body, name="conv_fwd", grid=(S // T,),
        in_specs=[pl.BlockSpec((T, D), lambda i: (i, 0)), pl.BlockSpec((T, D), lambda i: (i, 1)),
                  pl.BlockSpec((T, D), lambda i: (i, 2)),
                  pl.BlockSpec((HALO, D), prev), pl.BlockSpec((HALO, D), prev_b),
                  pl.BlockSpec((32, D), lambda i: (0, 0)), vec, vec, vec],
        out_specs=[pl.BlockSpec((T, D), lambda i: (i, 0)), pl.BlockSpec((T, D), lambda i: (i, 0))],
        out_shape=[jax.ShapeDtypeStruct((S, D), F32), jax.ShapeDtypeStruct((S, D), BF16)],
        scratch_shapes=[pltpu.VMEM((T + HALO + 8, D), F32), pltpu.VMEM((8, T + HALO, LANE), F32)],
        compiler_params=_cparams(("parallel",)),
    )(z, z, z, z, z, w_dw, b_dw, g_ln, b_ln)


_QK_RES = sorted({(HALO - (QK_W - 1) + j) % 8 for j in range(QK_W)} - {0})


def _qk_fill(ps, p0_ref, p1_ref, h0_ref, h1_ref, T, D):
    i = pl.program_id(0)
    ps[pl.ds(HALO, T), pl.ds(0, D)] = p0_ref[...].astype(F32)
    ps[pl.ds(HALO, T), pl.ds(D, D)] = p1_ref[...].astype(F32)
    ps[pl.ds(0, HALO), pl.ds(0, D)] = jnp.where(i > 0, h0_ref[...].astype(F32), 0.0)
    ps[pl.ds(0, HALO), pl.ds(D, D)] = jnp.where(i > 0, h1_ref[...].astype(F32), 0.0)
    ps[pl.ds(T + HALO, 8), :] = jnp.zeros((8, 2 * D), F32)


def _qk_conv(ps, sh, w_ref, cs, r0):
    acc = jnp.zeros((ROW_BLK, LANE), F32)
    for j in range(QK_W):
        acc = acc + w_ref[pl.ds(j, 1), cs] * _tap(ps, sh, cs, HALO - (QK_W - 1) + j, r0)
    return acc


def _qk_fwd(z, w_qk, D):
    S = z.shape[0]
    T = min(TOK_TILE, S)
    hb = T // HALO
    scale = (D // N_HEADS) ** -0.5
    W2 = 2 * D

    def body(p0_ref, p1_ref, h0_ref, h1_ref, w_ref, o_ref, ps, sh):
        _qk_fill(ps, p0_ref, p1_ref, h0_ref, h1_ref, T, D)
        for c0 in range(0, W2, LANE):
            cs = pl.ds(c0, LANE)
            _shift_strip(ps, sh, cs, T + HALO, _QK_RES)
            for r0 in range(0, T, ROW_BLK):
                qk = _silu(_qk_conv(ps, sh, w_ref, cs, r0))
                o_ref[pl.ds(r0, ROW_BLK), cs] = (qk * scale if c0 >= D else qk).astype(BF16)

    prev0 = lambda i: (jnp.maximum(i * hb - 1, 0), 3)
    prev1 = lambda i: (jnp.maximum(i * hb - 1, 0), 4)
    return pl.pallas_call(
        body, name="qk_fwd", grid=(S // T,),
        in_specs=[pl.BlockSpec((T, D), lambda i: (i, 3)), pl.BlockSpec((T, D), lambda i: (i, 4)),
                  pl.BlockSpec((HALO, D), prev0), pl.BlockSpec((HALO, D), prev1),
                  pl.BlockSpec((8, W2), lambda i: (0, 0))],
        out_specs=pl.BlockSpec((T, W2), lambda i: (i, 0)),
        out_shape=jax.ShapeDtypeStruct((S, W2), BF16),
        scratch_shapes=[pltpu.VMEM((T + HALO + 8, W2), F32), pltpu.VMEM((8, T + HALO, LANE), F32)],
        compiler_params=_cparams(("parallel",)),
    )(z, z, z, z, w_qk)


def _ml_gates(gc_ref, gr_ref, bifr_ref, bifc_ref, L):
    gc = gc_ref[...] + bifr_ref[...]
    gr = gr_ref[...] + bifc_ref[...]
    ii = lax.broadcasted_iota(jnp.int32, (L, L), 0)
    jj = lax.broadcasted_iota(jnp.int32, (L, L), 1)
    tri = (jj <= ii).astype(BF16)
    triu = (ii <= jj).astype(BF16)
    bcol_all = _tri_left(tri, _logsig(gc))
    brow_all = _tri_right(_logsig(gr), triu)
    return gc, gr, bcol_all, brow_all, ii, jj


def _ml_intra(q, k, b_c, b_r, li_r, m, n_row, ii, jj):
    d = b_c - b_r + li_r
    dm = jnp.where(jj <= ii, d, NEG)
    inter = b_c + m
    m_row = jnp.maximum(inter, jnp.max(dm, axis=1, keepdims=True))
    w_intra = jnp.exp(dm - m_row)
    w_inter = jnp.exp(inter - m_row)
    qk = _dot_nt(q, k)
    s = qk * w_intra
    qn = jnp.sum(q.astype(F32) * n_row, axis=-1, keepdims=True)
    den = jnp.sum(s, axis=-1, keepdims=True) + w_inter * qn
    hdiv = jnp.maximum(jnp.abs(den), jnp.exp(-m_row))
    return qk, w_intra, w_inter, s, qn, den, hdiv, m_row


def _ml_state(b_c, li_c, m, L):
    b_last = b_c[L - 1:L, :]
    g_c = b_last - b_c + li_c
    m_new = jnp.maximum(b_last + m, jnp.max(g_c, axis=0, keepdims=True))
    decay = jnp.exp(b_last + m - m_new)
    wk = jnp.exp(g_c - m_new)
    return m_new, decay, wk


def _mlstm_fwd(qks, z, gcol, grow, bif_row, bif_col, D):
    S = qks.shape[0]
    L = min(ML_CHUNK, S)
    NC = S // L
    H = N_HEADS
    dh = D // H

    def body(q_ref, k_ref, v_ref, gc_ref, gr_ref, bifr_ref, bifc_ref, hm_ref, cst_ref, nst_ref, C, NM):
        c = pl.program_id(0)

        @pl.when(c == 0)
        def _():
            C[...] = jnp.zeros_like(C)
            NM[...] = jnp.zeros_like(NM)

        gc, gr, bcol_all, brow_all, ii, jj = _ml_gates(gc_ref, gr_ref.at[0], bifr_ref, bifc_ref, L)
        for h in range(H):
            hs = pl.ds(h * dh, dh)
            q = q_ref[:, hs]
            k = k_ref[:, hs]
            v = v_ref[:, hs]
            b_c = bcol_all[:, H + h:H + h + 1]
            li_c = gc[:, h:h + 1]
            b_r = brow_all[H + h:H + h + 1, :]
            li_r = gr[h:h + 1, :]
            n_row = NM[h, 0:1, :]
            m = NM[h, 1:2, 0:1]
            Cb = C[h].astype(BF16)
            cst_ref[0, h] = Cb
            nst_ref[0, h] = NM[h]
            qk, w_intra, w_inter, s, qn, den, hdiv, m_row = _ml_intra(q, k, b_c, b_r, li_r, m, n_row, ii, jj)
            num = _dot_nn(s.astype(BF16), v) + w_inter * _dot_nn(q, Cb)
            hm_ref[:, hs] = num / hdiv
            m_new, decay, wk = _ml_state(b_c, li_c, m, L)
            wkk = wk * k.astype(F32)
            C[h] = decay * C[h] + _dot_tn(wkk.astype(BF16), v)
            NM[h, 0:1, :] = decay * n_row + jnp.sum(wkk, axis=0, keepdims=True)
            NM[h, 1:2, :] = jnp.broadcast_to(m_new, (1, dh))

    return pl.pallas_call(
        body, name="mlstm_fwd", grid=(NC,),
        in_specs=[pl.BlockSpec((L, D), lambda c: (c, 0)), pl.BlockSpec((L, D), lambda c: (c, 1)),
                  pl.BlockSpec((L, D), lambda c: (c, 5)),
                  pl.BlockSpec((L, IF_PAD), lambda c: (c, 0)),
                  pl.BlockSpec((1, 16, L), lambda c: (c, 0, 0)),
                  pl.BlockSpec((1, IF_PAD), lambda c: (0, 0)), pl.BlockSpec((16, 1), lambda c: (0, 0))],
        out_specs=[pl.BlockSpec((L, D), lambda c: (c, 0)),
                   pl.BlockSpec((1, H, dh, dh), lambda c: (c, 0, 0, 0)),
                   pl.BlockSpec((1, H, 8, dh), lambda c: (c, 0, 0, 0))],
        out_shape=[jax.ShapeDtypeStruct((S, D), F32), jax.ShapeDtypeStruct((NC, H, dh, dh), BF16),
                   jax.ShapeDtypeStruct((NC, H, 8, dh), F32)],
        scratch_shapes=[pltpu.VMEM((H, dh, dh), F32), pltpu.VMEM((H, 8, dh), F32)],
        compiler_params=_cparams(("arbitrary",)),
    )(qks, qks, z, gcol, grow, bif_row, bif_col)


def _mlstm_bwd(qks, z, gcol, grow, bif_row, bif_col, hm, dhm, cst, nst, dz, D):
    S = qks.shape[0]
    L = min(ML_CHUNK, S)
    NC = S // L
    H = N_HEADS
    dh = D // H

    def body(q_ref, k_ref, v_ref, gc_ref, gr_ref, bifr_ref, bifc_ref, hm_ref, dhm_ref, cst_ref, nst_ref, dz_any,
             dqk_ref, dg_ref, dv_ref, dC, dN):
        del dz_any
        c = pl.program_id(0)

        @pl.when(c == 0)
        def _():
            dC[...] = jnp.zeros_like(dC)
            dN[...] = jnp.zeros_like(dN)

        gc, gr, bcol_all, brow_all, ii, jj = _ml_gates(gc_ref, gr_ref.at[0], bifr_ref, bifc_ref, L)
        eye = ii == jj
        lane = lax.broadcasted_iota(jnp.int32, (L, IF_PAD), 1)
        db_all = jnp.zeros((L, IF_PAD), F32)
        dli_all = jnp.zeros((L, IF_PAD), F32)
        last_row = lax.broadcasted_iota(jnp.int32, (L, 1), 0) == L - 1
        for h in range(H):
            hs = pl.ds(h * dh, dh)
            q = q_ref[:, hs]
            k = k_ref[:, hs]
            v = v_ref[:, hs]
            qf = q.astype(F32)
            kf = k.astype(F32)
            b_c = bcol_all[:, H + h:H + h + 1]
            li_c = gc[:, h:h + 1]
            b_r = brow_all[H + h:H + h + 1, :]
            li_r = gr[h:h + 1, :]
            n_row = nst_ref[0, h, 0:1, :]
            m = nst_ref[0, h, 1:2, 0:1]
            Cb = cst_ref[0, h]
            qk, w_intra, w_inter, s, qn, den, hdiv, m_row = _ml_intra(q, k, b_c, b_r, li_r, m, n_row, ii, jj)
            dh_ = dhm_ref[:, hs]
            hh = hm_ref[:, hs]
            dnum = dh_ / hdiv
            dhdiv = -jnp.sum(dh_ * hh, axis=-1, keepdims=True) / hdiv
            dden = jnp.where(jnp.abs(den) > jnp.exp(-m_row), dhdiv * jnp.sign(den), 0.0)
            dnum_b = dnum.astype(BF16)
            ds = _dot_nt(dnum_b, v) + dden
            s_b = s.astype(BF16)
            dv = _dot_tn(s_b, dnum_b)
            dnC = _dot_nt(dnum_b, Cb)
            dwi = dden * w_inter
            dw_inter = jnp.sum(qf * dnC, axis=-1, keepdims=True) + dden * qn
            dq = w_inter * dnC + dwi * n_row
            dC_out = _dot_tn((w_inter * qf).astype(BF16), dnum_b)
            dn_out = jnp.sum(dwi * qf, axis=0, keepdims=True)
            dqk = (ds * w_intra).astype(BF16)
            dq = dq + _dot_nn(dqk, k)
            dk = _dot_tn(dqk, q)
            dd = ds * s
            rs_c = jnp.sum(dd, axis=1, keepdims=True)
            cs_r = jnp.sum(dd, axis=0, keepdims=True)
            cs_c = jnp.sum(jnp.where(eye, cs_r, 0.0), axis=1, keepdims=True)
            dinter = dw_inter * w_inter
            m_new, decay, wk = _ml_state(b_c, li_c, m, L)
            dCn = dC[h]
            dCn_b = dCn.astype(BF16)
            dn_new = dN[h, 0:1, :]
            ddecay = (jnp.sum(jnp.sum(dCn * Cb.astype(F32), axis=1, keepdims=True), axis=0, keepdims=True)
                      + jnp.sum(dn_new * n_row, axis=1, keepdims=True))
            dwkk = _dot_nt(v, dCn_b) + dn_new
            wkk_b = (wk * kf).astype(BF16)
            dv = dv + _dot_nn(wkk_b, dCn_b)
            dk = dk + wk * dwkk
            dg = jnp.sum(dwkk * kf, axis=-1, keepdims=True) * wk
            db_last = ddecay * decay + jnp.sum(dg, axis=0, keepdims=True)
            dC[h] = decay * dCn + dC_out
            dN[h, 0:1, :] = decay * dn_new + dn_out
            db_c = rs_c - cs_c + dinter - dg + jnp.where(last_row, db_last, 0.0)
            dli_c = cs_c + dg
            db_all = jnp.where(lane == H + h, db_c, db_all)
            dli_all = jnp.where(lane == h, dli_c, dli_all)
            dqk_ref[:, hs] = dq.astype(BF16)
            dqk_ref[:, pl.ds(D + h * dh, dh)] = dk.astype(BF16)
            dv_ref[:, hs] = dv.astype(BF16)
        triu = (ii <= jj).astype(BF16)
        dlf = _tri_left(triu, db_all)
        dg_ref[...] = dli_all + dlf * (1.0 - _sig(gc))

    r = lambda c: NC - 1 - c
    n_in = 12
    return pl.pallas_call(
        body, name="mlstm_bwd", grid=(NC,),
        in_specs=[pl.BlockSpec((L, D), lambda c: (r(c), 0)), pl.BlockSpec((L, D), lambda c: (r(c), 1)),
                  pl.BlockSpec((L, D), lambda c: (r(c), 5)),
                  pl.BlockSpec((L, IF_PAD), lambda c: (r(c), 0)),
                  pl.BlockSpec((1, 16, L), lambda c: (r(c), 0, 0)),
                  pl.BlockSpec((1, IF_PAD), lambda c: (0, 0)), pl.BlockSpec((16, 1), lambda c: (0, 0)),
                  pl.BlockSpec((L, D), lambda c: (r(c), 0)), pl.BlockSpec((L, D), lambda c: (r(c), 0)),
                  pl.BlockSpec((1, H, dh, dh), lambda c: (r(c), 0, 0, 0)),
                  pl.BlockSpec((1, H, 8, dh), lambda c: (r(c), 0, 0, 0)),
                  pl.BlockSpec(memory_space=pl.ANY)],
        out_specs=[pl.BlockSpec((L, 2 * D), lambda c: (r(c), 0)),
                   pl.BlockSpec((L, IF_PAD), lambda c: (r(c), 0)),
                   pl.BlockSpec((L, D), lambda c: (r(c), 5))],
        out_shape=[jax.ShapeDtypeStruct((S, 2 * D), BF16),
                   jax.ShapeDtypeStruct((S, IF_PAD), F32), jax.ShapeDtypeStruct(dz.shape, dz.dtype)],
        input_output_aliases={n_in - 1: 2},
        scratch_shapes=[pltpu.VMEM((H, dh, dh), F32), pltpu.VMEM((H, 8, dh), F32)],
        compiler_params=_cparams(("arbitrary",)),
    )(qks, qks, z, gcol, grow, bif_row, bif_col, hm, dhm, cst, nst, dz)


def _ml_post_fwd(hm, z, g_head, D):
    S = hm.shape[0]
    T = min(TOK_TILE, S)
    dh = D // N_HEADS

    def body(hm_ref, o_ref, zm_ref, g_ref, pm_ref):
        for h in range(N_HEADS):
            hs = pl.ds(h * dh, dh)
            hg = _sig(o_ref[:, hs].astype(F32)) * hm_ref[:, hs]
            rs = lax.rsqrt(jnp.mean(hg * hg, axis=-1, keepdims=True) + EPS)
            pm_ref[:, hs] = (hg * rs * g_ref[:, hs] * _silu(zm_ref[:, hs].astype(F32))).astype(BF16)

    return pl.pallas_call(
        body, name="ml_post_fwd", grid=(S // T,),
        in_specs=[pl.BlockSpec((T, D), lambda i: (i, 0)), pl.BlockSpec((T, D), lambda i: (i, 6)),
                  pl.BlockSpec((T, D), lambda i: (i, 7)), pl.BlockSpec((1, D), lambda i: (0, 0))],
        out_specs=pl.BlockSpec((T, D), lambda i: (i, 0)),
        out_shape=jax.ShapeDtypeStruct((S, D), BF16),
        compiler_params=_cparams(("parallel",)),
    )(hm, z, z, g_head)


def _ml_post_bwd(dpm, hm, z, g_head, dz, D):
    S = hm.shape[0]
    T = min(TOK_TILE, S)
    dh = D // N_HEADS

    def body(dpm_ref, hm_ref, o_ref, zm_ref, g_ref, dz_any, dhm_ref, dg_ref, dozm_ref):
        del dz_any
        do_ref = dozm_ref.at[:, pl.ds(0, D)]
        dzm_ref = dozm_ref.at[:, pl.ds(D, D)]
        i = pl.program_id(0)

        @pl.when(i == 0)
        def _():
            dg_ref[...] = jnp.zeros_like(dg_ref)

        for h in range(N_HEADS):
            hs = pl.ds(h * dh, dh)
            o = o_ref[:, hs].astype(F32)
            zm = zm_ref[:, hs].astype(F32)
            hmv = hm_ref[:, hs]
            dpm_ = dpm_ref[:, hs].astype(F32)
            g = g_ref[:, hs]
            so = _sig(o)
            hg = so * hmv
            rs = lax.rsqrt(jnp.mean(hg * hg, axis=-1, keepdims=True) + EPS)
            xn = hg * rs
            dzm_ref[:, hs] = (dpm_ * xn * g * _dsilu(zm)).astype(BF16)
            dhn = dpm_ * _silu(zm)
            dg_ref[0:1, hs] += jnp.sum(dhn * xn, axis=0, keepdims=True)
            dxn = dhn * g
            dhg = rs * (dxn - xn * jnp.mean(dxn * xn, axis=-1, keepdims=True))
            do_ref[:, hs] = (dhg * hmv * so * (1.0 - so)).astype(BF16)
            dhm_ref[:, hs] = dhg * so

    return pl.pallas_call(
        body, name="ml_post_bwd", grid=(S // T,),
        in_specs=[pl.BlockSpec((T, D), lambda i: (i, 0)), pl.BlockSpec((T, D), lambda i: (i, 0)),
                  pl.BlockSpec((T, D), lambda i: (i, 6)), pl.BlockSpec((T, D), lambda i: (i, 7)),
                  pl.BlockSpec((1, D), lambda i: (0, 0)), pl.BlockSpec(memory_space=pl.ANY)],
        out_specs=[pl.BlockSpec((T, D), lambda i: (i, 0)), pl.BlockSpec((8, D), lambda i: (0, 0)),
                   pl.BlockSpec((T, 2 * D), lambda i: (i, 3))],
        out_shape=[jax.ShapeDtypeStruct((S, D), F32), jax.ShapeDtypeStruct((8, D), F32),
                   jax.ShapeDtypeStruct(dz.shape, dz.dtype)],
        input_output_aliases={5: 2},
        compiler_params=_cparams(("arbitrary",)),
    )(dpm, hm, z, z, g_head, dz)


def _xattn_probs(q, km_h, scale):
    s = _dot_nt(q, km_h) * scale
    e = jnp.exp(s - jnp.max(s, axis=-1, keepdims=True))
    return e / jnp.sum(e, axis=-1, keepdims=True)


def _xattn_fwd(z, kv, D):
    S = z.shape[0]
    M = kv.shape[0]
    T = min(TOK_TILE, S)
    dh = D // N_HEADS
    scale = dh ** -0.5

    def body(q_ref, zx_ref, km_ref, vm_ref, px_ref):
        for h in range(N_HEADS):
            hs = pl.ds(h * dh, dh)
            p = _xattn_probs(q_ref[:, hs], km_ref[:, hs], scale)
            o = _dot_nn(p.astype(BF16), vm_ref[:, hs])
            px_ref[:, hs] = (o * _silu(zx_ref[:, hs].astype(F32))).astype(BF16)

    return pl.pallas_call(
        body, name="xattn_fwd", grid=(S // T,),
        in_specs=[pl.BlockSpec((T, D), lambda i: (i, 8)), pl.BlockSpec((T, D), lambda i: (i, 9)),
                  pl.BlockSpec((M, D), lambda i: (0, 0)), pl.BlockSpec((M, D), lambda i: (0, 1))],
        out_specs=pl.BlockSpec((T, D), lambda i: (i, 0)),
        out_shape=jax.ShapeDtypeStruct((S, D), BF16),
        compiler_params=_cparams(("parallel",)),
    )(z, z, kv, kv)


def _xattn_bwd(dpx, z, kv, dz, D):
    S = z.shape[0]
    M = kv.shape[0]
    T = min(TOK_TILE, S)
    dh = D // N_HEADS
    scale = dh ** -0.5

    def body(dpx_ref, q_ref, zx_ref, km_ref, vm_ref, dz_any, dkv_ref, dqz_ref):
        del dz_any
        i = pl.program_id(0)

        @pl.when(i == 0)
        def _():
            dkv_ref[...] = jnp.zeros_like(dkv_ref)

        for h in range(N_HEADS):
            hs = pl.ds(h * dh, dh)
            q = q_ref[:, hs]
            zx = zx_ref[:, hs].astype(F32)
            dpx_ = dpx_ref[:, hs].astype(F32)
            p = _xattn_probs(q, km_ref[:, hs], scale)
            p_b = p.astype(BF16)
            o = _dot_nn(p_b, vm_ref[:, hs])
            dqz_ref[:, pl.ds(D + h * dh, dh)] = (dpx_ * o * _dsilu(zx)).astype(BF16)
            do = (dpx_ * _silu(zx)).astype(BF16)
            dp = _dot_nt(do, vm_ref[:, hs])
            dsc = (p * (dp - jnp.sum(p * dp, axis=-1, keepdims=True)) * scale).astype(BF16)
            dqz_ref[:, hs] = _dot_nn(dsc, km_ref[:, hs]).astype(BF16)
            dkv_ref[:, hs] += _dot_tn(dsc, q)
            dkv_ref[:, pl.ds(D + h * dh, dh)] += _dot_tn(p_b, do)

    return pl.pallas_call(
        body, name="xattn_bwd", grid=(S // T,),
        in_specs=[pl.BlockSpec((T, D), lambda i: (i, 0)),
                  pl.BlockSpec((T, D), lambda i: (i, 8)), pl.BlockSpec((T, D), lambda i: (i, 9)),
                  pl.BlockSpec((M, D), lambda i: (0, 0)), pl.BlockSpec((M, D), lambda i: (0, 1)),
                  pl.BlockSpec(memory_space=pl.ANY)],
        out_specs=[pl.BlockSpec((M, 2 * D), lambda i: (0, 0)), pl.BlockSpec((T, 2 * D), lambda i: (i, 4))],
        out_shape=[jax.ShapeDtypeStruct((M, 2 * D), F32), jax.ShapeDtypeStruct(dz.shape, dz.dtype)],
        input_output_aliases={5: 1},
        compiler_params=_cparams(("arbitrary",)),
    )(dpx, z, z, kv, kv, dz)


def _col_sum(a):
    S, C = a.shape
    T = min(1024, S)

    def body(a_ref, o_ref):
        @pl.when(pl.program_id(0) == 0)
        def _():
            o_ref[...] = jnp.zeros_like(o_ref)

        o_ref[0:1, :] += jnp.sum(a_ref[...], axis=0, keepdims=True)

    return pl.pallas_call(
        body, name="col_sum", grid=(S // T,),
        in_specs=[pl.BlockSpec((T, C), lambda i: (i, 0))], out_specs=pl.BlockSpec((8, C), lambda i: (0, 0)),
        out_shape=jax.ShapeDtypeStruct((8, C), F32), compiler_params=_cparams(("arbitrary",)),
    )(a)


def _gain_grad(dn, xin, rstd):
    R, D = dn.shape

    def body(dn_ref, x_ref, r_ref, o_ref):
        o_ref[...] = jnp.zeros_like(o_ref)
        o_ref[0:1, :] = jnp.sum(dn_ref[...] * x_ref[...] * r_ref[...], axis=0, keepdims=True)

    return pl.pallas_call(
        body, name="gain_grad", out_shape=jax.ShapeDtypeStruct((8, D), F32),
        compiler_params=_cparams(),
    )(dn, xin, rstd)


def _merge_fwd(z, yc, ym, yx, D):
    S = z.shape[0]
    T = min(TOK_TILE, S)

    def body(g0, g1, g2, yc_ref, ym_ref, yx_ref, o_ref):
        o_ref[...] = (_sig(g0[...].astype(F32)) * yc_ref[...].astype(F32)
                      + _sig(g1[...].astype(F32)) * ym_ref[...].astype(F32)
                      + _sig(g2[...].astype(F32)) * yx_ref[...].astype(F32)).astype(BF16)

    tok = pl.BlockSpec((T, D), lambda i: (i, 0))
    return pl.pallas_call(
        body, name="merge_fwd", grid=(S // T,),
        in_specs=[pl.BlockSpec((T, D), lambda i: (i, 10)), pl.BlockSpec((T, D), lambda i: (i, 11)),
                  pl.BlockSpec((T, D), lambda i: (i, 12)), tok, tok, tok],
        out_specs=tok, out_shape=jax.ShapeDtypeStruct((S, D), BF16),
        compiler_params=_cparams(("parallel",)),
    )(z, z, z, yc, ym, yx)


def _merge_bwd(dmg, z, yc, ym, yx, D):
    S = z.shape[0]
    T = min(TOK_TILE, S)

    def body(dm_ref, g_ref, yc_ref, ym_ref, yx_ref, dyc_ref, dym_ref, dyx_ref, dg_ref):
        j = pl.program_id(1)
        for jj, (y_ref, dy_ref) in enumerate(((yc_ref, dyc_ref), (ym_ref, dym_ref), (yx_ref, dyx_ref))):
            @pl.when(j == jj)
            def _():
                sg = _sig(g_ref[...].astype(F32))
                dm = dm_ref[...].astype(F32)
                dy_ref[...] = (dm * sg).astype(BF16)
                dg_ref[...] = (dm * y_ref[...].astype(F32) * sg * (1.0 - sg)).astype(BF16)

    tok = pl.BlockSpec((T, D), lambda i, j: (i, 0))
    return pl.pallas_call(
        body, name="merge_bwd", grid=(S // T, 3),
        in_specs=[tok, pl.BlockSpec((T, D), lambda i, j: (i, 10 + j)), tok, tok, tok],
        out_specs=[tok, tok, tok, pl.BlockSpec((T, D), lambda i, j: (i, 10 + j))],
        out_shape=[jax.ShapeDtypeStruct((S, D), BF16)] * 3 + [jax.ShapeDtypeStruct((S, 13 * D), BF16)],
        compiler_params=_cparams(("arbitrary", "arbitrary")),
    )(dmg, z, yc, ym, yx)


def _post_loss(r, x, tgt, g_post):
    S, D = r.shape
    T = min(TOK_TILE, S)

    def body(r_ref, x_ref, t_ref, g_ref, dy_ref, dr_ref, dg_ref, loss_ref):
        i = pl.program_id(0)

        @pl.when(i == 0)
        def _():
            dg_ref[...] = jnp.zeros_like(dg_ref)
            loss_ref[...] = jnp.zeros_like(loss_ref)

        rv = r_ref[...]
        g = g_ref[...]
        rs = lax.rsqrt(jnp.mean(rv * rv, axis=-1, keepdims=True) + EPS)
        nrm = rv * rs
        e = x_ref[...] + nrm * g - t_ref[...]
        part = jnp.sum(jnp.sum(e * e, axis=-1, keepdims=True), axis=0, keepdims=True) * (0.5 / D)
        loss_ref[0:1, 0:1] += part
        dy = e * (1.0 / D)
        dy_ref[...] = dy
        dg_ref[0:1, :] += jnp.sum(dy * nrm, axis=0, keepdims=True)
        dn = dy * g
        dr_ref[...] = (rs * (dn - nrm * jnp.mean(dn * nrm, axis=-1, keepdims=True))).astype(BF16)

    tok = pl.BlockSpec((T, D), lambda i: (i, 0))
    return pl.pallas_call(
        body, name="post_loss", grid=(S // T,),
        in_specs=[tok, tok, tok, pl.BlockSpec((1, D), lambda i: (0, 0))],
        out_specs=[tok, tok, pl.BlockSpec((8, D), lambda i: (0, 0)), pl.BlockSpec((8, 128), lambda i: (0, 0))],
        out_shape=[jax.ShapeDtypeStruct((S, D), F32), jax.ShapeDtypeStruct((S, D), BF16),
                   jax.ShapeDtypeStruct((8, D), F32), jax.ShapeDtypeStruct((8, 128), F32)],
        compiler_params=_cparams(("arbitrary",)),
    )(r, x, tgt, g_post)


def _conv_bwd1(dpc, z, u1, g_ln, b_ln, dz, D):
    S = z.shape[0]
    T = min(TOK_TILE, S)

    def body(dpc_ref, zc_ref, u1_ref, gln_ref, bln_ref, dz_any, du1_ref, st_ref, dzc_ref):
        del dz_any
        i = pl.program_id(0)

        @pl.when(i == 0)
        def _():
            st_ref[...] = jnp.zeros_like(st_ref)

        u1v = u1_ref[...]
        mu = jnp.mean(u1v, axis=-1, keepdims=True)
        xc = u1v - mu
        rs = lax.rsqrt(jnp.mean(xc * xc, axis=-1, keepdims=True) + EPS)
        xh = xc * rs
        g = gln_ref[...]
        ln = xh * g + bln_ref[...]
        zc = zc_ref[...].astype(F32)
        dpc_ = dpc_ref[...].astype(F32)
        dzc_ref[...] = (dpc_ * _silu(ln) * _dsilu(zc)).astype(BF16)
        dln = dpc_ * _silu(zc) * _dsilu(ln)
        st_ref[0:1, :] += jnp.sum(dln * xh, axis=0, keepdims=True)
        st_ref[1:2, :] += jnp.sum(dln, axis=0, keepdims=True)
        dxh = dln * g
        du1 = rs * (dxh - jnp.mean(dxh, axis=-1, keepdims=True) - xh * jnp.mean(dxh * xh, axis=-1, keepdims=True))
        du1_ref[...] = du1
        st_ref[2:3, :] += jnp.sum(du1, axis=0, keepdims=True)

    tok = pl.BlockSpec((T, D), lambda i: (i, 0))
    vec = pl.BlockSpec((1, D), lambda i: (0, 0))
    return pl.pallas_call(
        body, name="conv_bwd1", grid=(S // T,),
        in_specs=[tok, pl.BlockSpec((T, D), lambda i: (i, 2)), tok, vec, vec, pl.BlockSpec(memory_space=pl.ANY)],
        out_specs=[tok, pl.BlockSpec((8, D), lambda i: (0, 0)), pl.BlockSpec((T, D), lambda i: (i, 2))],
        out_shape=[jax.ShapeDtypeStruct((S, D), F32), jax.ShapeDtypeStruct((8, D), F32),
                   jax.ShapeDtypeStruct(dz.shape, dz.dtype)],
        input_output_aliases={5: 2},
        compiler_params=_cparams(("arbitrary",)),
    )(dpc, z, u1, g_ln, b_ln, dz)


def _conv_bwd2(du1, z, w_dw, dz, D):
    S = z.shape[0]
    T = min(TOK_TILE, S)
    hb = T // HALO
    last = S // HALO - 1

    n_steps = S // T

    def body(du_ref, dun_ref, a_ref, b_ref, ah_ref, bh_ref, w_ref, dz_any, dw_ref, dab_ref,
             dus, u0s, shd, shu, dwp):
        del dz_any
        i = pl.program_id(0)

        @pl.when(i == 0)
        def _():
            dwp[...] = jnp.zeros_like(dwp)

        dus[pl.ds(0, T), :] = du_ref[...]
        dus[pl.ds(T, HALO), :] = jnp.where(i < n_steps - 1, dun_ref[...], 0.0)
        dus[pl.ds(T + HALO, 8), :] = jnp.zeros((8, D), F32)
        u0s[pl.ds(HALO, T), :] = a_ref[...].astype(F32) * _sig(b_ref[...].astype(F32))
        halo = ah_ref[...].astype(F32) * _sig(bh_ref[...].astype(F32))
        u0s[pl.ds(0, HALO), :] = jnp.where(i > 0, halo, 0.0)
        u0s[pl.ds(T + HALO, 8), :] = jnp.zeros((8, D), F32)
        for c0 in range(0, D, LANE):
            cs = pl.ds(c0, LANE)
            _shift_strip(dus, shd, cs, T + HALO)
            _shift_strip(u0s, shu, cs, T + HALO)
            for r0 in range(0, T, ROW_BLK):
                rows = pl.ds(r0, ROW_BLK)
                du0 = jnp.zeros((ROW_BLK, LANE), F32)
                for j in range(CONV_W):
                    du0 = du0 + w_ref[pl.ds(j, 1), cs] * _tap(dus, shd, cs, CONV_W - 1 - j, r0)
                a = a_ref[rows, cs].astype(F32)
                sb = _sig(b_ref[rows, cs].astype(F32))
                dab_ref[rows, cs] = (du0 * sb).astype(BF16)
                dab_ref[rows, pl.ds(D + c0, LANE)] = (du0 * a * sb * (1.0 - sb)).astype(BF16)
                d = dus[rows, cs]
                for j in range(CONV_W):
                    prod = d * _tap(u0s, shu, cs, HALO - (CONV_W - 1) + j, r0)
                    part = prod[0:8]
                    for q in range(8, ROW_BLK, 8):
                        part = part + prod[q:q + 8]
                    dwp[pl.ds(8 * j, 8), cs] += part

        @pl.when(i == n_steps - 1)
        def _():
            dw_ref[...] = jnp.zeros_like(dw_ref)
            for j in range(CONV_W):
                dw_ref[pl.ds(j, 1), :] = jnp.sum(dwp[pl.ds(8 * j, 8), :], axis=0, keepdims=True)

    tok = pl.BlockSpec((T, D), lambda i: (i, 0))
    nxt = lambda i: (jnp.minimum((i + 1) * hb, last), 0)
    prev = lambda i: (jnp.maximum(i * hb - 1, 0), 0)
    prev_b = lambda i: (jnp.maximum(i * hb - 1, 0), 1)
    return pl.pallas_call(
        body, name="conv_bwd2", grid=(S // T,),
        in_specs=[tok, pl.BlockSpec((HALO, D), nxt), tok, pl.BlockSpec((T, D), lambda i: (i, 1)),
                  pl.BlockSpec((HALO, D), prev), pl.BlockSpec((HALO, D), prev_b),
                  pl.BlockSpec((32, D), lambda i: (0, 0)), pl.BlockSpec(memory_space=pl.ANY)],
        out_specs=[pl.BlockSpec((32, D), lambda i: (0, 0)), pl.BlockSpec((T, 2 * D), lambda i: (i, 0))],
        out_shape=[jax.ShapeDtypeStruct((32, D), F32), jax.ShapeDtypeStruct(dz.shape, dz.dtype)],
        input_output_aliases={7: 1},
        scratch_shapes=[pltpu.VMEM((T + HALO + 8, D), F32), pltpu.VMEM((T + HALO + 8, D), F32),
                        pltpu.VMEM((8, T + HALO, LANE), F32), pltpu.VMEM((8, T + HALO, LANE), F32),
                        pltpu.VMEM((8 * 32, D), F32)],
        compiler_params=_cparams(("arbitrary",)),
    )(du1, du1, z, z, z, z, w_dw, dz)


def _qk_bwd1(dqks, z, w_qk, D):
    S = z.shape[0]
    T = min(TOK_TILE, S)
    hb = T // HALO
    scale = (D // N_HEADS) ** -0.5
    W2 = 2 * D

    n_steps = S // T

    def body(d_ref, p0_ref, p1_ref, h0_ref, h1_ref, w_ref, dc_ref, dw_ref, ps, sh, dwp):
        i = pl.program_id(0)

        @pl.when(i == 0)
        def _():
            dwp[...] = jnp.zeros_like(dwp)

        _qk_fill(ps, p0_ref, p1_ref, h0_ref, h1_ref, T, D)
        for c0 in range(0, W2, LANE):
            cs = pl.ds(c0, LANE)
            _shift_strip(ps, sh, cs, T + HALO, _QK_RES)
            for r0 in range(0, T, ROW_BLK):
                rows = pl.ds(r0, ROW_BLK)
                dc = d_ref[rows, cs].astype(F32) * _dsilu(_qk_conv(ps, sh, w_ref, cs, r0))
                if c0 >= D:
                    dc = dc * scale
                dc_ref[rows, cs] = dc.astype(BF16)
                for j in range(QK_W):
                    prod = dc * _tap(ps, sh, cs, HALO - (QK_W - 1) + j, r0)
                    part = prod[0:8]
                    for q in range(8, ROW_BLK, 8):
                        part = part + prod[q:q + 8]
                    dwp[pl.ds(8 * j, 8), cs] += part

        @pl.when(i == n_steps - 1)
        def _():
            dw_ref[...] = jnp.zeros_like(dw_ref)
            for j in range(QK_W):
                dw_ref[pl.ds(j, 1), :] = jnp.sum(dwp[pl.ds(8 * j, 8), :], axis=0, keepdims=True)

    prev0 = lambda i: (jnp.maximum(i * hb - 1, 0), 3)
    prev1 = lambda i: (jnp.maximum(i * hb - 1, 0), 4)
    return pl.pallas_call(
        body, name="qk_bwd1", grid=(S // T,),
        in_specs=[pl.BlockSpec((T, W2), lambda i: (i, 0)),
                  pl.BlockSpec((T, D), lambda i: (i, 3)), pl.BlockSpec((T, D), lambda i: (i, 4)),
                  pl.BlockSpec((HALO, D), prev0), pl.BlockSpec((HALO, D), prev1),
                  pl.BlockSpec((8, W2), lambda i: (0, 0))],
        out_specs=[pl.BlockSpec((T, W2), lambda i: (i, 0)), pl.BlockSpec((8, W2), lambda i: (0, 0))],
        out_shape=[jax.ShapeDtypeStruct((S, W2), BF16), jax.ShapeDtypeStruct((8, W2), F32)],
        scratch_shapes=[pltpu.VMEM((T + HALO + 8, W2), F32), pltpu.VMEM((8, T + HALO, LANE), F32),
                        pltpu.VMEM((8 * QK_W, W2), F32)],
        compiler_params=_cparams(("arbitrary",)),
    )(dqks, z, z, z, z, w_qk)


def _qk_bwd2(dc, w_qk, dz, D):
    S = dc.shape[0]
    T = min(TOK_TILE, S)
    hb = T // HALO
    last = S // HALO - 1

    n_steps = S // T

    def body(d_ref, dn_ref, w_ref, dz_any, o_ref, ds, sh):
        del dz_any
        i = pl.program_id(0)
        ds[pl.ds(0, T), :] = d_ref[...].astype(F32)
        ds[pl.ds(T, HALO), :] = jnp.where(i < n_steps - 1, dn_ref[...].astype(F32), 0.0)
        ds[pl.ds(T + HALO, 8), :] = jnp.zeros((8, D), F32)
        for c0 in range(0, D, LANE):
            cs = pl.ds(c0, LANE)
            _shift_strip(ds, sh, cs, T + HALO, range(1, QK_W))
            for r0 in range(0, T, ROW_BLK):
                acc = jnp.zeros((ROW_BLK, LANE), F32)
                for j in range(QK_W):
                    acc = acc + w_ref[pl.ds(j, 1), cs] * _tap(ds, sh, cs, QK_W - 1 - j, r0)
                o_ref[pl.ds(r0, ROW_BLK), cs] = acc.astype(BF16)

    return pl.pallas_call(
        body, name="qk_bwd2", grid=(S // T, 2),
        in_specs=[pl.BlockSpec((T, D), lambda i, j: (i, j)),
                  pl.BlockSpec((HALO, D), lambda i, j: (jnp.minimum((i + 1) * hb, last), j)),
                  pl.BlockSpec((8, D), lambda i, j: (0, j)), pl.BlockSpec(memory_space=pl.ANY)],
        out_specs=pl.BlockSpec((T, D), lambda i, j: (i, 3 + j)),
        out_shape=jax.ShapeDtypeStruct(dz.shape, dz.dtype),
        input_output_aliases={3: 0},
        scratch_shapes=[pltpu.VMEM((T + HALO + 8, D), F32), pltpu.VMEM((8, T + HALO, LANE), F32)],
        compiler_params=_cparams(("arbitrary", "arbitrary")),
    )(dc, dc, w_qk, dz)


TILE_BUDGET = 12 * 1024 * 1024


def _tile_2d(R, C, elem_bytes):
    cands = [t for t in range(BF16_ROWS, R + 1, BF16_ROWS) if R % t == 0 and t * C * elem_bytes <= TILE_BUDGET]
    if cands:
        return max(cands), C
    if R * C * elem_bytes <= TILE_BUDGET or C % LANE:
        return R, C
    cands = [t for t in range(LANE, C + 1, LANE) if C % t == 0 and t * R * elem_bytes <= TILE_BUDGET]
    return R, (max(cands) if cands else LANE)


def _sum_parts(parts, name):
    n, R, C = parts.shape
    tr, tc = _tile_2d(R, C, n * parts.dtype.itemsize + 4)

    def body(p_ref, o_ref):
        g = p_ref[0].astype(F32)
        for s in range(1, n):
            g = g + p_ref[s].astype(F32)
        o_ref[...] = g

    return pl.pallas_call(
        body, name=name, grid=(R // tr, C // tc),
        in_specs=[pl.BlockSpec((n, tr, tc), lambda i, j: (0, i, j))],
        out_specs=pl.BlockSpec((tr, tc), lambda i, j: (i, j)),
        out_shape=jax.ShapeDtypeStruct((R, C), F32),
        compiler_params=_cparams(("parallel", "parallel")),
    )(parts)


def _adamw(parts, w, m, v, name):
    R, C = w.shape
    n = parts.shape[0]
    tr, tc = _tile_2d(R, C, 4 * 7 + n * parts.dtype.itemsize)
    c1 = 1.0 / (1.0 - ADAM_B1 ** ADAM_STEP)
    c2 = 1.0 / (1.0 - ADAM_B2 ** ADAM_STEP)

    def body(p_ref, w_ref, m_ref, v_ref, g_ref, d_ref, nm_ref, nv_ref):
        g = p_ref[0].astype(F32)
        for s in range(1, n):
            g = g + p_ref[s].astype(F32)
        g_ref[...] = g
        mn = ADAM_B1 * m_ref[...] + (1.0 - ADAM_B1) * g
        vn = ADAM_B2 * v_ref[...] + (1.0 - ADAM_B2) * (g * g)
        nm_ref[...] = mn
        nv_ref[...] = vn
        d_ref[...] = -ADAM_LR * ((mn * c1) / (jnp.sqrt(vn * c2) + ADAM_EPS) + ADAM_WD * w_ref[...])

    blk = pl.BlockSpec((tr, tc), lambda i, j: (i, j))
    return pl.pallas_call(
        body, name=name, grid=(R // tr, C // tc),
        in_specs=[pl.BlockSpec((n, tr, tc), lambda i, j: (0, i, j)), blk, blk, blk],
        out_specs=[blk, blk, blk, blk],
        out_shape=[jax.ShapeDtypeStruct((R, C), F32)] * 4,
        compiler_params=_cparams(("parallel", "parallel")),
    )(parts, w, m, v)


def _exchange(operands, scatter, name):
    n = len(operands)

    def body(*refs):
        copies = _xchg_copies(refs[:n], refs[n:2 * n], *refs[2 * n:], scatter)
        for cp in copies:
            cp.start()
        for cp in copies:
            cp.wait()

    hbm = pl.BlockSpec(memory_space=pltpu.HBM)
    return pl.pallas_call(
        body, name=name, in_specs=[hbm] * n, out_specs=[hbm] * n, out_shape=_xchg_out_shapes(operands, scatter),
        scratch_shapes=_xchg_sems(n), compiler_params=pltpu.CompilerParams(has_side_effects=True),
    )(*operands)


N_CHIP = 4


class _CoreGather:
    def __init__(self, core):
        self.core = core

    def out_shape(self, blk):
        return jax.ShapeDtypeStruct((N_CHIP,) + blk.shape, blk.dtype)

    @staticmethod
    def sems():
        return [pltpu.SemaphoreType.DMA((7,)), pltpu.SemaphoreType.DMA((7,)), pltpu.SemaphoreType.DMA]

    def _parts(self, src, out, send_sems, recv_sems, loc_sem):
        x, y, c = lax.axis_index("x"), lax.axis_index("y"), lax.axis_index("c")
        me, sibling = (x, y, c), (x, y, 1 - c)
        chips = [(1 - x, y), (x, 1 - y), (1 - x, 1 - y)]

        def copy(k, chip, to, from_src=False):
            slot = out.at[2 * chip[0] + chip[1]]
            return pltpu.make_async_remote_copy(
                src_ref=src if from_src else slot, dst_ref=slot, send_sem=send_sems.at[k], recv_sem=recv_sems.at[k],
                device_id=to, device_id_type=pl.DeviceIdType.MESH)

        mine = pltpu.make_async_copy(src, out.at[2 * x + y], loc_sem)
        first = [copy(0, (x, y), sibling, True)] + [copy(1 + j, (x, y), (*ch, c), True) for j, ch in enumerate(chips)]
        landed = [copy(1 + j, ch, me) for j, ch in enumerate(chips)]
        passed = [copy(4 + j, ch, sibling) for j, ch in enumerate(chips)]
        from_sibling = [copy(0, (x, y), me)] + [copy(4 + j, ch, me) for j, ch in enumerate(chips)]
        return c == self.core, mine, first, landed, passed, from_sibling

    def start(self, *refs):
        holder, mine, first, _, _, _ = self._parts(*refs)

        @pl.when(holder)
        def _():
            mine.start()
            for cp in first:
                cp.start()

    def forward(self, *refs):
        holder, _, _, landed, passed, _ = self._parts(*refs)

        @pl.when(holder)
        def _():
            for got, cp in zip(landed, passed):
                got.wait_recv()
                cp.start()

    def finish(self, *refs):
        holder, mine, first, _, passed, from_sibling = self._parts(*refs)

        @pl.when(holder)
        def _():
            for cp in first + passed:
                cp.wait_send()
            mine.wait()

        @pl.when(jnp.logical_not(holder))
        def _():
            for cp in from_sibling:
                cp.wait_recv()


def _gather2_copies(srcs, dsts, send_sems, recv_sems, loc_sems):
    x, y, c = lax.axis_index("x"), lax.axis_index("y"), lax.axis_index("c")
    me, sibling = (x, y, c), (x, y, 1 - c)
    chips = [(1 - x, y), (x, 1 - y), (1 - x, 1 - y)]
    mine, first, landed, passed, arriving = [], [], [], [], []
    for t, (src, dst) in enumerate(zip(srcs, dsts)):
        def copy(k, block, to, from_src=False, src=src, dst=dst, t=t):
            slot = dst.at[4 * block[0] + 2 * block[1] + block[2]]
            return pltpu.make_async_remote_copy(
                src_ref=src if from_src else slot, dst_ref=slot,
                send_sem=send_sems.at[7 * t + k], recv_sem=recv_sems.at[7 * t + k],
                device_id=to, device_id_type=pl.DeviceIdType.MESH)

        mine.append(pltpu.make_async_copy(src, dst.at[4 * x + 2 * y + c], loc_sems.at[t]))
        first += [copy(0, me, sibling, True)] + [copy(1 + j, me, (*ch, c), True) for j, ch in enumerate(chips)]
        landed += [copy(1 + j, (*ch, c), me) for j, ch in enumerate(chips)]
        passed += [copy(4 + j, (*ch, c), sibling) for j, ch in enumerate(chips)]
        arriving += [copy(0, sibling, me)] + [copy(4 + j, (*ch, 1 - c), me) for j, ch in enumerate(chips)]
    return mine, first, landed, passed, arriving


def _core_gather(blk, core, name):
    g = _CoreGather(core)

    def body(*refs):
        g.start(*refs)
        g.forward(*refs)
        g.finish(*refs)

    hbm = pl.BlockSpec(memory_space=pltpu.HBM)
    return pl.pallas_call(
        body, name=name, in_specs=[hbm], out_specs=hbm, out_shape=g.out_shape(blk), scratch_shapes=g.sems(),
        compiler_params=pltpu.CompilerParams(has_side_effects=True),
    )(blk)


IN_TILE = 512


def _tile_index(tiles):
    jumps = [(i, tiles[i] - tiles[i - 1] - 1) for i in range(1, len(tiles)) if tiles[i] != tiles[i - 1] + 1]

    def f(t):
        j = t + tiles[0]
        for i, gap in jumps:
            j = j + jnp.where(t >= i, gap, 0)
        return j

    return f


def _in_proj_phase(h, w_rows, tiles, z_prev, name, gather=None, xchg=None):
    S, D = h.shape
    nt = len(tiles)
    assert w_rows.shape == (nt * IN_TILE, D)
    tm = _fit(S, 2048)
    ni = S // tm
    grid = (ni, nt)
    mid = (3 * ni * nt) // 4
    col = _tile_index(tiles)
    n_in = 2 + (z_prev is not None)
    g_ops = [gather[0]] if gather else []
    x_ops = list(xchg[0]) if xchg else []
    n_g, n_x = len(g_ops), len(x_ops)

    def body(*refs):
        h_ref, w_ref = refs[0], refs[1]
        g_src = refs[n_in:n_in + n_g]
        x_src = refs[n_in + n_g:n_in + n_g + n_x]
        o = n_in + n_g + n_x
        z_ref = refs[o]
        g_dst = refs[o + 1:o + 1 + n_g]
        x_dst = refs[o + 1 + n_g:o + 1 + n_g + n_x]
        sems = refs[o + 1 + n_g + n_x:]
        g_sems, x_sems = sems[:3 * n_g], sems[3 * n_g:]
        i, t = pl.program_id(0), pl.program_id(1)
        first = (t == 0) & (i == 0)
        last = (t == nt - 1) & (i == ni - 1)

        @pl.when(first)
        def _():
            if gather:
                gather[1].start(g_src[0], g_dst[0], *g_sems)
            if xchg:
                mine, out, _, _, _ = _gather2_copies(x_src, x_dst, *x_sems)
                for cp in mine + out:
                    cp.start()

        z_ref[...] = _dot_nt(h_ref[...], w_ref[...]).astype(z_ref.dtype)

        @pl.when((i == mid // nt) & (t == mid % nt))
        def _():
            if gather:
                gather[1].forward(g_src[0], g_dst[0], *g_sems)
            if xchg:
                _, _, landed, passed, _ = _gather2_copies(x_src, x_dst, *x_sems)
                for got, cp in zip(landed, passed):
                    got.wait_recv()
                    cp.start()

        @pl.when(last)
        def _():
            if gather:
                gather[1].finish(g_src[0], g_dst[0], *g_sems)
            if xchg:
                mine, out, _, passed, arriving = _gather2_copies(x_src, x_dst, *x_sems)
                for cp in arriving:
                    cp.wait_recv()
                for cp in out + passed:
                    cp.wait_send()
                for cp in mine:
                    cp.wait()

    hbm = pl.BlockSpec(memory_space=pltpu.HBM)
    in_specs = [pl.BlockSpec((tm, D), lambda i, t: (i, 0)), pl.BlockSpec((IN_TILE, D), lambda i, t: (t, 0))]
    args = [h, w_rows]
    aliases = {}
    if z_prev is not None:
        in_specs.append(pl.BlockSpec(memory_space=pl.ANY))
        args.append(z_prev)
        aliases = {2: 0}
    in_specs += [hbm] * (n_g + n_x)
    args += g_ops + x_ops
    out_specs = [pl.BlockSpec((tm, IN_TILE), lambda i, t: (i, col(t)))] + [hbm] * (n_g + n_x)
    out_shape = [jax.ShapeDtypeStruct((S, 13 * D), BF16)]
    scratch = []
    if gather:
        out_shape.append(gather[1].out_shape(gather[0]))
        scratch += gather[1].sems()
    if xchg:
        out_shape += _xchg_out_shapes(x_ops, xchg[1])
        scratch += _xchg_sems(n_x)
    comm = bool(gather or xchg)
    res = pl.pallas_call(
        body, name=name, grid=grid, in_specs=in_specs, out_specs=out_specs, out_shape=out_shape,
        input_output_aliases=aliases, scratch_shapes=scratch,
        compiler_params=pltpu.CompilerParams(vmem_limit_bytes=VMEM_LIMIT, has_side_effects=comm,
                                             dimension_semantics=("arbitrary", "arbitrary")),
    )(*args)
    z = res[0]
    g_out = res[1] if gather else None
    x_out = list(res[1 + n_g:]) if xchg else None
    return z, g_out, x_out


def _in_proj_dx(dz, wpT, dgates, wifT, x, rstd, dy, g, side=None):
    S, K = dz.shape
    D = wpT.shape[1]
    tm, tk = _fit(S, 512), _fit(K, 2048)
    nm, nk = S // tm, K // tk
    n_side = len(side[0]) if side else 0

    rc = min(tm, 128)

    def body(*refs):
        a_ref, b_ref, dgt_ref, wif_ref, x_hbm, r_ref, dy_hbm, g_ref = refs[:8]
        srcs = refs[8:8 + n_side]
        dx_ref, dg_ref = refs[8 + n_side], refs[9 + n_side]
        dsts = refs[10 + n_side:10 + 2 * n_side]
        acc, xbuf, dybuf, ep_sems = refs[10 + 2 * n_side:14 + 2 * n_side]
        sems = refs[14 + 2 * n_side:]
        i, k = pl.program_id(0), pl.program_id(1)
        rows = pl.ds(pl.multiple_of(i * tm, tm), tm)
        fetch = [pltpu.make_async_copy(x_hbm.at[rows, :], xbuf, ep_sems.at[0]),
                 pltpu.make_async_copy(dy_hbm.at[rows, :], dybuf, ep_sems.at[1])]

        @pl.when((i == 0) & (k == 0))
        def _():
            dg_ref[...] = jnp.zeros_like(dg_ref)
            if n_side:
                for cp in _xchg_copies(srcs, dsts, *sems, side[1]):
                    cp.start()

        @pl.when(k == nk - 2)
        def _():
            for cp in fetch:
                cp.start()

        p = _dot_nn(a_ref[...], b_ref[...])

        @pl.when(k == 0)
        def _():
            acc[...] = p

        @pl.when((k > 0) & (k < nk - 1))
        def _():
            acc[...] += p

        @pl.when(k == nk - 1)
        def _():
            acc[...] += p + _dot_nn(dgt_ref[...], wif_ref[...])
            for cp in fetch:
                cp.wait()
            for r0 in range(0, tm, rc):
                rr = pl.ds(r0, rc)
                dh = acc[rr, :]
                rs = r_ref[rr, :]
                xn = xbuf[rr, :] * rs
                dg_ref[0:1, :] += jnp.sum(dh * xn, axis=0, keepdims=True)
                dxn = dh * g_ref[...]
                dx_ref[rr, :] = dybuf[rr, :] + rs * (dxn - xn * jnp.mean(dxn * xn, axis=-1, keepdims=True))

        if n_side:
            @pl.when((i == nm - 1) & (k == nk - 1))
            def _():
                for cp in _xchg_copies(srcs, dsts, *sems, side[1]):
                    cp.wait()

    assert nk > 1
    hbm = pl.BlockSpec(memory_space=pltpu.HBM)
    tok = pl.BlockSpec((tm, D), lambda i, k: (i, 0))
    in_specs = [pl.BlockSpec((tm, tk), lambda i, k: (i, k)), pl.BlockSpec((tk, D), lambda i, k: (k, 0)),
                pl.BlockSpec((tm, IF_PAD), lambda i, k: (i, 0)), pl.BlockSpec((IF_PAD, D), lambda i, k: (0, 0)),
                hbm, pl.BlockSpec((tm, 1), lambda i, k: (i, 0)), hbm, pl.BlockSpec((1, D), lambda i, k: (0, 0))]
    out_specs = [tok, pl.BlockSpec((8, D), lambda i, k: (0, 0))]
    out_shape = [jax.ShapeDtypeStruct((S, D), F32), jax.ShapeDtypeStruct((8, D), F32)]
    scratch = [pltpu.VMEM((tm, D), F32), pltpu.VMEM((tm, D), F32), pltpu.VMEM((tm, D), F32),
               pltpu.SemaphoreType.DMA((2,))]
    args = [dz, wpT, dgates, wifT, x, rstd, dy, g]
    if n_side:
        in_specs += [hbm] * n_side
        args += list(side[0])
        out_specs += [hbm] * n_side
        out_shape += _xchg_out_shapes(side[0], side[1])
        scratch += _xchg_sems(n_side)
    res = pl.pallas_call(
        body, name="in_proj_dx", grid=(nm, nk), in_specs=in_specs, out_specs=out_specs, out_shape=out_shape,
        scratch_shapes=scratch,
        compiler_params=pltpu.CompilerParams(vmem_limit_bytes=VMEM_LIMIT, has_side_effects=bool(n_side),
                                             dimension_semantics=("arbitrary", "arbitrary")),
    )(*args)
    return res[0], res[1], list(res[2:])


LATE = ('w_conv_out', 'w_ml_out', 'w_xa_out', 'w_out', 'w_mem_kv', 'w_qk_conv', 'w_dw')
ROW_SHARDED = ('w_conv_out', 'w_ml_out', 'w_xa_out', 'w_out')
COL_SHARDED = ('w_in', 'w_mem_kv', 'w_qk_conv', 'w_dw')


def _cols_to_slabs(g):
    R, C8 = g.shape
    return g.reshape(R, N_DEV, C8 // N_DEV).transpose(1, 0, 2)


def _slabs_to_cols(a):
    n, R, C = a.shape
    return a.transpose(1, 0, 2).reshape(R, n * C)


def _assemble(name, slabs):
    if name in ROW_SHARDED:
        return slabs.reshape(slabs.shape[0] * slabs.shape[1], slabs.shape[2])
    return _slabs_to_cols(slabs)


def _to_slabs(name, g, dtype):
    if name in ROW_SHARDED:
        return g.reshape(N_DEV, g.shape[0] // N_DEV, g.shape[1]).astype(dtype)
    return _cols_to_slabs(g).astype(dtype)


def _w_in_rows(D):
    n = (13 * D + N_IF) // N_DEV
    gate_dev, gate_row = (8 * D) // n, (8 * D) % n
    assert gate_row + N_IF <= n
    height = -(-(n + BF16_ROWS - 1) // BF16_ROWS) * BF16_ROWS

    def first(p):
        return n * p - jnp.where(n * p >= 8 * D + N_IF, N_IF, 0)

    def start(p):
        return jnp.minimum((first(p) // BF16_ROWS) * BF16_ROWS, 13 * D - height)

    return n, gate_dev, gate_row, height, first, start


def _w_in_tiles(D):
    n, gate_dev, *_ = _w_in_rows(D)

    def rows(s):
        f = n * s - (N_IF if n * s >= 8 * D + N_IF else 0)
        return f, f + n - (N_IF if s == gate_dev else 0)

    tiles = range(13 * D // IN_TILE)
    inside = lambda j, s: rows(s)[0] <= j * IN_TILE and (j + 1) * IN_TILE <= rows(s)[1]
    early = [j for j in tiles if any(inside(j, s) for s in range(0, N_DEV, 2))]
    return early, [j for j in tiles if j not in early], rows


def _w_window_height(D):
    n = (13 * D + N_IF) // N_DEV
    return -(-(n + IN_TILE - 1) // IN_TILE) * IN_TILE


def _w_window(block_t, me, D):
    n, gate_dev, gate_row, _, first, _ = _w_in_rows(D)
    no_gates = jnp.concatenate([block_t[:gate_row], block_t[gate_row + N_IF:], jnp.zeros((N_IF, D), block_t.dtype)])
    mine = jnp.where(me == gate_dev, no_gates, block_t)
    off = first(me) % IN_TILE
    return lax.dynamic_update_slice(jnp.zeros((_w_window_height(D), D), block_t.dtype), mine, (off, 0))


def _w_rows_of_tiles(tiles, windows, D):
    _, _, rows = _w_in_tiles(D)
    runs = []
    for j in tiles:
        own = tuple(s for s in range(N_DEV) if rows(s)[0] < (j + 1) * IN_TILE and j * IN_TILE < rows(s)[1])
        if runs and runs[-1][0] == own and runs[-1][1] + runs[-1][2] == j:
            runs[-1] = (own, runs[-1][1], runs[-1][2] + 1)
        else:
            runs.append((own, j, 1))
    pieces = []
    for own, j, k in runs:
        cut = [windows[s][(j - rows(s)[0] // IN_TILE) * IN_TILE:(j + k - rows(s)[0] // IN_TILE) * IN_TILE] for s in own]
        pieces.append(functools.reduce(jnp.add, cut))
    return jnp.concatenate(pieces, axis=0)


def _interleave_tiles(parts):
    where = {j: (rows, i) for rows, tiles in parts for i, j in enumerate(tiles)}
    out, j, n_tiles = [], 0, len(where)
    while j < n_tiles:
        rows, i = where[j]
        k = 1
        while j + k in where and where[j + k][0] is rows and where[j + k][1] == i + k:
            k += 1
        out.append(rows[i * IN_TILE:(i + k) * IN_TILE])
        j += k
    return jnp.concatenate(out, axis=0)


def _local_step(x, mem, tgt, g_pre, w_early, w_rest, wifT, b_if, b_dw, g_ln, b_ln, g_head, g_mem, g_post, late, dist):
    S, D = x.shape
    L = min(ML_CHUNK, S)
    NC = S // L
    bif_row = jnp.zeros((1, IF_PAD), F32).at[0, :N_IF].set(b_if)
    bif_col = jnp.zeros((16, 1), F32).at[:N_IF, 0].set(b_if)
    early, rest, _ = _w_in_tiles(D)

    h, rstd = _rmsnorm_fwd(x, g_pre, "prenorm_fwd")
    if dist:
        my_block, rows_of_rest = w_rest
        z, north, _ = _in_proj_phase(h, w_early, early, None, "in_proj_early", gather=(my_block, _CoreGather(1)))
        w_rest = rows_of_rest(north)
        z, _, gathered = _in_proj_phase(h, w_rest, rest, z, "in_proj_rest",
                                        xchg=([late[n] for n in LATE] + [wifT], False))
        full = {n: _assemble(n, a) for n, a in zip(LATE, gathered)}
        wifT = jnp.zeros((IF_PAD, D), BF16).at[:N_IF].set(gathered[-1][_w_in_rows(D)[1], :N_IF])
    else:
        z, _, _ = _in_proj_phase(h, w_early, early, None, "in_proj_early")
        z, _, _ = _in_proj_phase(h, w_rest, rest, z, "in_proj_rest")
        full = late
    wpT = _interleave_tiles([(w_early, early), (w_rest, rest)])
    w_co, w_mo, w_xo, w_out, w_kv = (full[n] for n in LATE[:5])
    w_qk = jnp.zeros((8, 2 * D), F32).at[:QK_W].set(full['w_qk_conv'])
    w_dw = jnp.zeros((32, D), F32).at[:CONV_W].set(full['w_dw'])
    gcol = _mm(h, wifT, mode="nt", out_dtype=F32, tm=1024, tn=IF_PAD, tk=2048, name="gates_col")
    grow = _mm(wifT[:16], h, mode="nt", out_dtype=F32, tm=16, tn=1024, tk=2048, name="gates_row")
    grow = grow.reshape(16, NC, L).transpose(1, 0, 2)

    u1, pc = _conv_fwd(z, w_dw, b_dw, g_ln, b_ln, D)
    qks = _qk_fwd(z, w_qk, D)
    hm, cst, nst = _mlstm_fwd(qks, z, gcol, grow, bif_row, bif_col, D)
    pm = _ml_post_fwd(hm, z, g_head, D)
    memn, rstd_mem = _rmsnorm_fwd(mem, g_mem, "memnorm_fwd")
    kv = _mm(memn, w_kv, mode="nn", out_dtype=BF16, tm=256, tn=1024, tk=2048, name="mem_kv")
    px = _xattn_fwd(z, kv, D)

    proj = functools.partial(_mm, mode="nn", tm=1024, tn=1024, tk=2048)
    yc = proj(pc, w_co, out_dtype=BF16, name="conv_out")
    ym = proj(pm, w_mo, out_dtype=BF16, name="ml_out")
    yx = proj(px, w_xo, out_dtype=BF16, name="xa_out")
    mg = _merge_fwd(z, yc, ym, yx, D)
    r = proj(mg, w_out, out_dtype=F32, name="out_proj")
    dy, dr, dg_post, loss = _post_loss(r, x, tgt, g_post)

    dgrad = functools.partial(_mm, mode="nt", out_dtype=BF16, tm=1024, tn=1024, tk=2048)
    wgrad = functools.partial(_mm, mode="tn", out_dtype=F32, tm=1024, tn=1024, tk=2048)
    dmg = dgrad(dr, w_out, name="out_proj_dx")
    dw_out = wgrad(mg, dr, name="out_proj_dw")
    dyc, dym, dyx, dz = _merge_bwd(dmg, z, yc, ym, yx, D)
    dpc = dgrad(dyc, w_co, name="conv_out_dx")
    dw_co = wgrad(pc, dyc, name="conv_out_dw")
    dpm = dgrad(dym, w_mo, name="ml_out_dx")
    dw_mo = wgrad(pm, dym, name="ml_out_dw")
    dpx = dgrad(dyx, w_xo, name="xa_out_dx")
    dw_xo = wgrad(px, dyx, name="xa_out_dw")

    dkv, dz = _xattn_bwd(dpx, z, kv, dz, D)
    dkv_b = dkv.astype(BF16)
    dw_kv = wgrad(memn, dkv_b, name="mem_kv_dw")
    dmemn = _mm(dkv_b, w_kv, mode="nt", out_dtype=F32, tm=256, tn=1024, tk=2048, name="mem_kv_dx")
    dg_mem = _gain_grad(dmemn, mem, rstd_mem)

    du1, cstats, dz = _conv_bwd1(dpc, z, u1, g_ln, b_ln, dz, D)
    dw_dw, dz = _conv_bwd2(du1, z, w_dw, dz, D)

    dhm, dg_head, dz = _ml_post_bwd(dpm, hm, z, g_head, dz, D)
    dqks, dgates, dz = _mlstm_bwd(qks, z, gcol, grow, bif_row, bif_col, hm, dhm, cst, nst, dz, D)
    dc, dw_qk = _qk_bwd1(dqks, z, w_qk, D)
    dz = _qk_bwd2(dc, w_qk, dz, D)

    g = dict(w_conv_out=dw_co, w_ml_out=dw_mo, w_xa_out=dw_xo, w_out=dw_out, w_mem_kv=dw_kv,
             w_qk_conv=dw_qk[:QK_W], w_dw=dw_dw[:CONV_W])
    dgates_b = dgates.astype(BF16)
    in_proj_dw = functools.partial(_mm, dz, h, mode="tn", out_dtype=BF16 if dist else F32, tm=1024, tn=2048, tk=2048,
                                   name="in_proj_dw")
    in_proj_dx = functools.partial(_in_proj_dx, dz, wpT, dgates_b, wifT, x, rstd, dy, g_pre)
    parts = {}
    if dist:
        send = [_to_slabs(n, g[n], BF16 if n in LATE[:5] else F32) for n in LATE]
        dwpT, recv = in_proj_dw(side=(send, True))
        parts.update(zip(LATE, recv))
        *_, height, _, start = _w_in_rows(D)
        dx, dg_pre, recv = in_proj_dx(side=([dwpT], _RowWindows(start, height)))
        parts['w_in'] = recv[0]
    else:
        dwpT = in_proj_dw()
        dx, dg_pre, _ = in_proj_dx()
    dwifT = _mm(dgates_b, h, mode="tn", out_dtype=F32, tm=IF_PAD, tn=2048, tk=2048, name="gates_dw")
    if not dist:
        g['w_in'] = jnp.concatenate([dwpT[:8 * D], dwifT[:N_IF], dwpT[8 * D:]], axis=0).T

    db_if = _col_sum(dgates)[0, :N_IF]
    g.update(g_pre=dg_pre[0], b_if=db_if, b_dw=cstats[2], g_ln=cstats[0], b_ln=cstats[1], g_ml_head=dg_head[0],
             g_mem=dg_mem[0], g_post=dg_post[0], w_in_gates=dwifT[:N_IF])
    return loss[0, 0], dx, g, parts


WEIGHTS = ('g_pre', 'w_in', 'b_if', 'w_qk_conv', 'w_dw', 'b_dw', 'g_ln', 'b_ln', 'w_conv_out', 'g_ml_head',
           'w_ml_out', 'g_mem', 'w_mem_kv', 'w_xa_out', 'w_out', 'g_post')
VECTORS = ('g_pre', 'b_dw', 'g_ln', 'b_ln', 'g_ml_head', 'g_mem', 'g_post')


def kernel(x, mem, g_pre, w_in, b_if, w_qk_conv, w_dw, b_dw, g_ln, b_ln, w_conv_out, g_ml_head, w_ml_out, g_mem, w_mem_kv, w_xa_out, w_out, g_post, loss_target, m_g_pre, m_w_in, m_b_if, m_w_qk_conv, m_w_dw, m_b_dw, m_g_ln, m_b_ln, m_w_conv_out, m_g_ml_head, m_w_ml_out, m_g_mem, m_w_mem_kv, m_w_xa_out, m_w_out, m_g_post, v_g_pre, v_w_in, v_b_if, v_w_qk_conv, v_w_dw, v_b_dw, v_g_ln, v_b_ln, v_w_conv_out, v_g_ml_head, v_w_ml_out, v_g_mem, v_w_mem_kv, v_w_xa_out, v_w_out, v_g_post):
    S, D = x.shape[1], x.shape[2]
    w = dict(g_pre=g_pre, w_in=w_in, b_if=b_if, w_qk_conv=w_qk_conv, w_dw=w_dw, b_dw=b_dw, g_ln=g_ln, b_ln=b_ln,
             w_conv_out=w_conv_out, g_ml_head=g_ml_head, w_ml_out=w_ml_out, g_mem=g_mem, w_mem_kv=w_mem_kv,
             w_xa_out=w_xa_out, w_out=w_out, g_post=g_post)
    mom = dict(g_pre=m_g_pre, w_in=m_w_in, b_if=m_b_if, w_qk_conv=m_w_qk_conv, w_dw=m_w_dw, b_dw=m_b_dw,
               g_ln=m_g_ln, b_ln=m_b_ln, w_conv_out=m_w_conv_out, g_ml_head=m_g_ml_head, w_ml_out=m_w_ml_out,
               g_mem=m_g_mem, w_mem_kv=m_w_mem_kv, w_xa_out=m_w_xa_out, w_out=m_w_out, g_post=m_g_post)
    var = dict(g_pre=v_g_pre, w_in=v_w_in, b_if=v_b_if, w_qk_conv=v_w_qk_conv, w_dw=v_w_dw, b_dw=v_b_dw,
               g_ln=v_g_ln, b_ln=v_b_ln, w_conv_out=v_w_conv_out, g_ml_head=v_g_ml_head, w_ml_out=v_w_ml_out,
               g_mem=v_g_mem, w_mem_kv=v_w_mem_kv, w_xa_out=v_w_xa_out, w_out=v_w_out, g_post=v_g_post)

    n_rows, gate_dev, gate_row, height, first, start = _w_in_rows(D)
    me = 4 * lax.axis_index("x") + 2 * lax.axis_index("y") + lax.axis_index("c")

    my_block = w_in.T.astype(BF16)
    my_window = _w_window(my_block, me, D)
    south = _core_gather(my_window, 0, "gather_w_in_south")
    early, rest, _ = _w_in_tiles(D)
    south_windows = {2 * q: south[q] for q in range(N_CHIP)}
    w_early = _w_rows_of_tiles(early, south_windows, D)
    my_gates = jnp.zeros((BF16_ROWS, D), BF16).at[:N_IF].set(my_block[gate_row:gate_row + N_IF])

    def rows_of_rest(north):
        windows = dict(south_windows)
        windows.update({2 * q + 1: north[q] for q in range(N_CHIP)})
        return _w_rows_of_tiles(rest, windows, D)

    late = {n: w[n].astype(BF16) if n in LATE[:5] else w[n] for n in LATE}
    row = lambda a: a.reshape(1, D)

    loss, dx, g, parts = _local_step(
        x[0], mem[0], loss_target[0], row(g_pre), w_early, (my_window, rows_of_rest), my_gates, b_if, row(b_dw),
        row(g_ln), row(b_ln), row(g_ml_head), row(g_mem), row(g_post), late, True)

    pad = lambda a: jnp.zeros((D,), F32).at[:N_IF].set(a)
    vec = jnp.concatenate([jnp.stack([g[n] for n in VECTORS] + [pad(g['b_if'])]), g['w_in_gates']])
    parts['vec'] = _exchange([vec], False, "gather_vector_grads")[0]

    out = {}
    for n in LATE:
        out[n] = _adamw(parts[n], w[n], mom[n], var[n], "adamw_" + n)
    zeros8 = jnp.zeros((N_IF, D), F32)
    stack = lambda d: jnp.concatenate([jnp.stack([d[n] for n in VECTORS] + [pad(d['b_if'])]), zeros8])
    vres = _adamw(parts['vec'], stack(w), stack(mom), stack(var), "adamw_vectors")
    for i, n in enumerate(VECTORS):
        out[n] = [r[i] for r in vres]
    out['b_if'] = [r[len(VECTORS), :N_IF] for r in vres]
    gate_grad = vres[0][len(VECTORS) + 1:]

    win = _sum_parts(parts['w_in'], "sum_w_in_grads")
    seg = lax.dynamic_slice(win, (first(me) - start(me), 0), (n_rows, D))
    with_gates = jnp.concatenate([seg[:gate_row], gate_grad, seg[gate_row:n_rows - N_IF]], axis=0)
    g_in = jnp.where(me == gate_dev, with_gates, seg)
    out['w_in'] = [r.T for r in _adamw(g_in[None], w_in.T, m_w_in.T, v_w_in.T, "adamw_w_in")]

    loss = lax.psum(loss, ("x", "y", "c"))
    res = [loss, dx.reshape(1, S, D)]
    for k in range(4):
        res += [out[n][k] for n in WEIGHTS]
    return tuple(res)
```

```python
import functools
import math

import jax
import jax.numpy as jnp
from jax import lax
from jax.experimental import pallas as pl
from jax.experimental.pallas import tpu as pltpu

F32 = jnp.float32
BF16 = jnp.bfloat16

N_DEV = 8
N_HEADS = 4
CONV_W = 31
QK_W = 4
N_IF = 2 * N_HEADS
IF_PAD = 128
EPS = 1e-6
NEG = -1e30

ADAM_LR = 0.001
ADAM_B1 = 0.9
ADAM_B2 = 0.999
ADAM_EPS = 1e-08
ADAM_WD = 0.01
ADAM_STEP = 10

TOK_TILE = 512
ML_CHUNK = 256
HALO = 32
BF16_ROWS = 16
VMEM_LIMIT = 56 * 1024 * 1024


def _cparams(sem=None):
    kw = dict(vmem_limit_bytes=VMEM_LIMIT)
    if sem is not None:
        kw["dimension_semantics"] = sem
    return pltpu.CompilerParams(**kw)


def _sig(x):
    return jax.nn.sigmoid(x)


def _silu(x):
    return x * _sig(x)


def _dsilu(x):
    s = _sig(x)
    return s * (1.0 + x * (1.0 - s))


def _logsig(x):
    return jnp.minimum(x, 0.0) - jnp.log(1.0 + jnp.exp(-jnp.abs(x)))


def _dot(a, b, dims):
    return lax.dot_general(a, b, (dims, ((), ())), preferred_element_type=F32)


def _dot_nn(a, b):
    return _dot(a, b, ((1,), (0,)))


def _dot_nt(a, b):
    return _dot(a, b, ((1,), (1,)))


def _dot_tn(a, b):
    return _dot(a, b, ((0,), (0,)))


def _split3(a):
    hi = a.astype(BF16)
    r1 = a - hi.astype(F32)
    mid = r1.astype(BF16)
    lo = (r1 - mid.astype(F32)).astype(BF16)
    return hi, mid, lo


def _tri_left(tri, a):
    hi, mid, lo = _split3(a)
    return _dot_nn(tri, hi) + _dot_nn(tri, mid) + _dot_nn(tri, lo)


def _tri_right(a, tri):
    hi, mid, lo = _split3(a)
    return _dot_nn(hi, tri) + _dot_nn(mid, tri) + _dot_nn(lo, tri)


def _fit(n, t):
    if n <= t:
        return n
    return max(c for c in range(128, t + 1, 128) if n % c == 0)


def _peer(k):
    x, y, c = lax.axis_index("x"), lax.axis_index("y"), lax.axis_index("c")
    px = 1 - x if (k >> 2) & 1 else x
    py = 1 - y if (k >> 1) & 1 else y
    pc = 1 - c if k & 1 else c
    return (px, py, pc), 4 * px + 2 * py + pc


class _RowWindows:
    def __init__(self, start, height):
        self.start, self.height = start, height


def _xchg_copies(srcs, dsts, send_sems, recv_sems, loc_sems, scatter):
    _, me = _peer(0)

    def piece(src, p):
        if isinstance(scatter, _RowWindows):
            return src.at[pl.ds(pl.multiple_of(scatter.start(p), BF16_ROWS), scatter.height), :]
        return src.at[p] if scatter else src

    copies = []
    for t, (src, dst) in enumerate(zip(srcs, dsts)):
        copies.append(pltpu.make_async_copy(piece(src, me), dst.at[me], loc_sems.at[t]))
        for k in range(1, N_DEV):
            dev, p = _peer(k)
            s = t * (N_DEV - 1) + k - 1
            copies.append(pltpu.make_async_remote_copy(
                src_ref=piece(src, p), dst_ref=dst.at[me],
                send_sem=send_sems.at[s], recv_sem=recv_sems.at[s],
                device_id=dev, device_id_type=pl.DeviceIdType.MESH))
    return copies


def _xchg_out_shapes(operands, scatter):
    def one(a):
        if isinstance(scatter, _RowWindows):
            return (scatter.height, a.shape[1])
        return tuple(a.shape[1:] if scatter else a.shape)
    return [jax.ShapeDtypeStruct((N_DEV,) + one(a), a.dtype) for a in operands]


def _xchg_sems(n):
    return [pltpu.SemaphoreType.DMA((n * (N_DEV - 1),)), pltpu.SemaphoreType.DMA((n * (N_DEV - 1),)),
            pltpu.SemaphoreType.DMA((n,))]


def _mm(a, b, *, mode, out_dtype, tm, tn, tk, name, n_outer=False, acc_in=None, side=None):
    if mode == "nn":
        (M, K), (K2, N) = a.shape, b.shape
    elif mode == "nt":
        (M, K), (N, K2) = a.shape, b.shape
    else:
        (K, M), (K2, N) = a.shape, b.shape
    assert K == K2, (a.shape, b.shape, mode)
    tm, tn, tk = _fit(M, tm), _fit(N, tn), _fit(K, tk)
    assert M % tm == 0 and N % tn == 0 and K % tk == 0, (a.shape, b.shape, tm, tn, tk)
    nk = K // tk
    if n_outer:
        grid = (N // tn, M // tm, nk)
        ij = lambda g0, g1: (g1, g0)
    else:
        grid = (M // tm, N // tn, nk)
        ij = lambda g0, g1: (g0, g1)

    if mode == "nn":
        a_spec = pl.BlockSpec((tm, tk), lambda g0, g1, k: (ij(g0, g1)[0], k))
        b_spec = pl.BlockSpec((tk, tn), lambda g0, g1, k: (k, ij(g0, g1)[1]))
        dot = _dot_nn
    elif mode == "nt":
        a_spec = pl.BlockSpec((tm, tk), lambda g0, g1, k: (ij(g0, g1)[0], k))
        b_spec = pl.BlockSpec((tn, tk), lambda g0, g1, k: (ij(g0, g1)[1], k))
        dot = _dot_nt
    else:
        a_spec = pl.BlockSpec((tk, tm), lambda g0, g1, k: (k, ij(g0, g1)[0]))
        b_spec = pl.BlockSpec((tk, tn), lambda g0, g1, k: (k, ij(g0, g1)[1]))
        dot = _dot_tn
    o_spec = pl.BlockSpec((tm, tn), lambda g0, g1, k: ij(g0, g1))
    has_acc = acc_in is not None
    n_side = len(side[0]) if side is not None else 0
    scatter = side[1] if side is not None else False
    n_in = 2 + int(has_acc)

    def body(*refs):
        a_ref, b_ref = refs[0], refs[1]
        c_ref = refs[2] if has_acc else None
        srcs = refs[n_in:n_in + n_side]
        o_ref = refs[n_in + n_side]
        dsts = refs[n_in + n_side + 1:n_in + 2 * n_side + 1]
        scratch = refs[n_in + 2 * n_side + 1:]
        k = pl.program_id(2)
        if n_side:
            sems = scratch[-3:]
            first = (pl.program_id(0) == 0) & (pl.program_id(1) == 0) & (k == 0)
            last = (pl.program_id(0) == grid[0] - 1) & (pl.program_id(1) == grid[1] - 1) & (k == nk - 1)

            @pl.when(first)
            def _():
                for cp in _xchg_copies(srcs, dsts, *sems, scatter):
                    cp.start()

        p = dot(a_ref[...], b_ref[...])

        def finish(r):
            if has_acc:
                r = r + c_ref[...].astype(F32)
            o_ref[...] = r.astype(out_dtype)

        if nk == 1:
            finish(p)
        else:
            acc = scratch[0]

            @pl.when(k == 0)
            def _():
                acc[...] = p

            @pl.when((k > 0) & (k < nk - 1))
            def _():
                acc[...] += p

            @pl.when(k == nk - 1)
            def _():
                finish(acc[...] + p)

        if n_side:
            @pl.when(last)
            def _():
                for cp in _xchg_copies(srcs, dsts, *sems, scatter):
                    cp.wait()

    hbm = pl.BlockSpec(memory_space=pltpu.HBM)
    in_specs = [a_spec, b_spec]
    args = [a, b]
    if has_acc:
        in_specs.append(o_spec)
        args.append(acc_in)
    out_specs = [o_spec]
    out_shape = [jax.ShapeDtypeStruct((M, N), out_dtype)]
    scratch_shapes = [pltpu.VMEM((tm, tn), F32)] if nk > 1 else []
    if n_side:
        in_specs += [hbm] * n_side
        args += list(side[0])
        out_specs += [hbm] * n_side
        out_shape += _xchg_out_shapes(side[0], scatter)
        scratch_shapes += _xchg_sems(n_side)
        cp = pltpu.CompilerParams(vmem_limit_bytes=VMEM_LIMIT, has_side_effects=True,
                                  dimension_semantics=("arbitrary", "arbitrary", "arbitrary"))
    else:
        cp = _cparams(("parallel", "parallel", "arbitrary"))
    res = pl.pallas_call(
        body, name=name, grid=grid, in_specs=in_specs, out_specs=out_specs, out_shape=out_shape,
        scratch_shapes=scratch_shapes, compiler_params=cp,
    )(*args)
    return (res[0], list(res[1:])) if n_side else res[0]


def _rmsnorm_fwd(x, g, name):
    S, D = x.shape
    T = min(TOK_TILE, S)

    def body(x_ref, g_ref, h_ref, r_ref):
        xv = x_ref[...]
        r = lax.rsqrt(jnp.mean(xv * xv, axis=-1, keepdims=True) + EPS)
        h_ref[...] = (xv * r * g_ref[...]).astype(BF16)
        r_ref[...] = r

    return pl.pallas_call(
        body, name=name, grid=(S // T,),
        in_specs=[pl.BlockSpec((T, D), lambda i: (i, 0)), pl.BlockSpec((1, D), lambda i: (0, 0))],
        out_specs=[pl.BlockSpec((T, D), lambda i: (i, 0)), pl.BlockSpec((T, 1), lambda i: (i, 0))],
        out_shape=[jax.ShapeDtypeStruct((S, D), BF16), jax.ShapeDtypeStruct((S, 1), F32)],
        compiler_params=_cparams(("parallel",)),
    )(x, g)


LANE = 128
ROW_BLK = 64


def _shift_strip(src, sh, cs, nr, residues=range(1, 8)):
    for r in residues:
        sh[r] = src[pl.ds(r, nr), cs]


def _tap(src, sh, cs, o, r0):
    r = o % 8
    if r == 0:
        return src[pl.ds(o + r0, ROW_BLK), cs]
    return sh[r, pl.ds(o - r + r0, ROW_BLK), :]


def _conv_fwd(z, w_dw, b_dw, g_ln, b_ln, D):
    S = z.shape[0]
    T = min(TOK_TILE, S)
    hb = T // HALO

    def body(a_ref, b_ref, zc_ref, ah_ref, bh_ref, w_ref, bdw_ref, gln_ref, bln_ref, u1_ref, pc_ref, u0s, sh):
        i = pl.program_id(0)
        a = a_ref[...].astype(F32)
        b = b_ref[...].astype(F32)
        u0s[pl.ds(HALO, T), :] = a * _sig(b)
        halo = ah_ref[...].astype(F32) * _sig(bh_ref[...].astype(F32))
        u0s[pl.ds(0, HALO), :] = jnp.where(i > 0, halo, 0.0)
        u0s[pl.ds(T + HALO, 8), :] = jnp.zeros((8, D), F32)
        for c0 in range(0, D, LANE):
            cs = pl.ds(c0, LANE)
            _shift_strip(u0s, sh, cs, T + HALO)
            for r0 in range(0, T, ROW_BLK):
                acc = jnp.broadcast_to(bdw_ref[:, cs], (ROW_BLK, LANE))
                for j in range(CONV_W):
                    acc = acc + w_ref[pl.ds(j, 1), cs] * _tap(u0s, sh, cs, HALO - (CONV_W - 1) + j, r0)
                u1_ref[pl.ds(r0, ROW_BLK), cs] = acc
        acc = u1_ref[...]
        mu = jnp.mean(acc, axis=-1, keepdims=True)
        xc = acc - mu
        var = jnp.mean(xc * xc, axis=-1, keepdims=True)
        ln = xc * lax.rsqrt(var + EPS) * gln_ref[...] + bln_ref[...]
        pc_ref[...] = (_silu(ln) * _silu(zc_ref[...].astype(F32))).astype(BF16)

    prev = lambda i: (jnp.maximum(i * hb - 1, 0), 0)
    prev_b = lambda i: (jnp.maximum(i * hb - 1, 0), D // D)
    vec = pl.BlockSpec((1, D), lambda i: (0, 0))
    return pl.pallas_call(
        body, name="conv_fwd", grid=(S // T,),
        in_specs=[pl.BlockSpec((T, D), lambda i: (i, 0)), pl.BlockSpec((T, D), lambda i: (i, 1)),
                  pl.BlockSpec((T, D), lambda i: (i, 2)),
                  pl.BlockSpec((HALO, D), prev), pl.BlockSpec((HALO, D), prev_b),
                  pl.BlockSpec((32, D), lambda i: (0, 0)), vec, vec, vec],
        out_specs=[pl.BlockSpec((T, D), lambda i: (i, 0)), pl.BlockSpec((T, D), lambda i: (i, 0))],
        out_shape=[jax.ShapeDtypeStruct((S, D), F32), jax.ShapeDtypeStruct((S, D), BF16)],
        scratch_shapes=[pltpu.VMEM((T + HALO + 8, D), F32), pltpu.VMEM((8, T + HALO, LANE), F32)],
        compiler_params=_cparams(("parallel",)),
    )(z, z, z, z, z, w_dw, b_dw, g_ln, b_ln)


_QK_RES = sorted({(HALO - (QK_W - 1) + j) % 8 for j in range(QK_W)} - {0})


def _qk_fill(ps, p0_ref, p1_ref, h0_ref, h1_ref, T, D):
    i = pl.program_id(0)
    ps[pl.ds(HALO, T), pl.ds(0, D)] = p0_ref[...].astype(F32)
    ps[pl.ds(HALO, T), pl.ds(D, D)] = p1_ref[...].astype(F32)
    ps[pl.ds(0, HALO), pl.ds(0, D)] = jnp.where(i > 0, h0_ref[...].astype(F32), 0.0)
    ps[pl.ds(0, HALO), pl.ds(D, D)] = jnp.where(i > 0, h1_ref[...].astype(F32), 0.0)
    ps[pl.ds(T + HALO, 8), :] = jnp.zeros((8, 2 * D), F32)


def _qk_conv(ps, sh, w_ref, cs, r0):
    acc = jnp.zeros((ROW_BLK, LANE), F32)
    for j in range(QK_W):
        acc = acc + w_ref[pl.ds(j, 1), cs] * _tap(ps, sh, cs, HALO - (QK_W - 1) + j, r0)
    return acc


def _qk_fwd(z, w_qk, D):
    S = z.shape[0]
    T = min(TOK_TILE, S)
    hb = T // HALO
    scale = (D // N_HEADS) ** -0.5
    W2 = 2 * D

    def body(p0_ref, p1_ref, h0_ref, h1_ref, w_ref, o_ref, ps, sh):
        _qk_fill(ps, p0_ref, p1_ref, h0_ref, h1_ref, T, D)
        for c0 in range(0, W2, LANE):
            cs = pl.ds(c0, LANE)
            _shift_strip(ps, sh, cs, T + HALO, _QK_RES)
            for r0 in range(0, T, ROW_BLK):
                qk = _silu(_qk_conv(ps, sh, w_ref, cs, r0))
                o_ref[pl.ds(r0, ROW_BLK), cs] = (qk * scale if c0 >= D else qk).astype(BF16)

    prev0 = lambda i: (jnp.maximum(i * hb - 1, 0), 3)
    prev1 = lambda i: (jnp.maximum(i * hb - 1, 0), 4)
    return pl.pallas_call(
        body, name="qk_fwd", grid=(S // T,),
        in_specs=[pl.BlockSpec((T, D), lambda i: (i, 3)), pl.BlockSpec((T, D), lambda i: (i, 4)),
                  pl.BlockSpec((HALO, D), prev0), pl.BlockSpec((HALO, D), prev1),
                  pl.BlockSpec((8, W2), lambda i: (0, 0))],
        out_specs=pl.BlockSpec((T, W2), lambda i: (i, 0)),
        out_shape=jax.ShapeDtypeStruct((S, W2), BF16),
        scratch_shapes=[pltpu.VMEM((T + HALO + 8, W2), F32), pltpu.VMEM((8, T + HALO, LANE), F32)],
        compiler_params=_cparams(("parallel",)),
    )(z, z, z, z, w_qk)


def _ml_gates(gc_ref, gr_ref, bifr_ref, bifc_ref, L):
    gc = gc_ref[...] + bifr_ref[...]
    gr = gr_ref[...] + bifc_ref[...]
    ii = lax.broadcasted_iota(jnp.int32, (L, L), 0)
    jj = lax.broadcasted_iota(jnp.int32, (L, L), 1)
    tri = (jj <= ii).astype(BF16)
    triu = (ii <= jj).astype(BF16)
    bcol_all = _tri_left(tri, _logsig(gc))
    brow_all = _tri_right(_logsig(gr), triu)
    return gc, gr, bcol_all, brow_all, ii, jj


def _ml_intra(q, k, b_c, b_r, li_r, m, n_row, ii, jj):
    d = b_c - b_r + li_r
    dm = jnp.where(jj <= ii, d, NEG)
    inter = b_c + m
    m_row = jnp.maximum(inter, jnp.max(dm, axis=1, keepdims=True))
    w_intra = jnp.exp(dm - m_row)
    w_inter = jnp.exp(inter - m_row)
    qk = _dot_nt(q, k)
    s = qk * w_intra
    qn = jnp.sum(q.astype(F32) * n_row, axis=-1, keepdims=True)
    den = jnp.sum(s, axis=-1, keepdims=True) + w_inter * qn
    hdiv = jnp.maximum(jnp.abs(den), jnp.exp(-m_row))
    return qk, w_intra, w_inter, s, qn, den, hdiv, m_row


def _ml_state(b_c, li_c, m, L):
    b_last = b_c[L - 1:L, :]
    g_c = b_last - b_c + li_c
    m_new = jnp.maximum(b_last + m, jnp.max(g_c, axis=0, keepdims=True))
    decay = jnp.exp(b_last + m - m_new)
    wk = jnp.exp(g_c - m_new)
    return m_new, decay, wk


def _mlstm_fwd(qks, z, gcol, grow, bif_row, bif_col, D):
    S = qks.shape[0]
    L = min(ML_CHUNK, S)
    NC = S // L
    H = N_HEADS
    dh = D // H

    def body(q_ref, k_ref, v_ref, gc_ref, gr_ref, bifr_ref, bifc_ref, hm_ref, cst_ref, nst_ref, C, NM):
        c = pl.program_id(0)

        @pl.when(c == 0)
        def _():
            C[...] = jnp.zeros_like(C)
            NM[...] = jnp.zeros_like(NM)

        gc, gr, bcol_all, brow_all, ii, jj = _ml_gates(gc_ref, gr_ref.at[0], bifr_ref, bifc_ref, L)
        for h in range(H):
            hs = pl.ds(h * dh, dh)
            q = q_ref[:, hs]
            k = k_ref[:, hs]
            v = v_ref[:, hs]
            b_c = bcol_all[:, H + h:H + h + 1]
            li_c = gc[:, h:h + 1]
            b_r = brow_all[H + h:H + h + 1, :]
            li_r = gr[h:h + 1, :]
            n_row = NM[h, 0:1, :]
            m = NM[h, 1:2, 0:1]
            Cb = C[h].astype(BF16)
            cst_ref[0, h] = Cb
            nst_ref[0, h] = NM[h]
            qk, w_intra, w_inter, s, qn, den, hdiv, m_row = _ml_intra(q, k, b_c, b_r, li_r, m, n_row, ii, jj)
            num = _dot_nn(s.astype(BF16), v) + w_inter * _dot_nn(q, Cb)
            hm_ref[:, hs] = num / hdiv
            m_new, decay, wk = _ml_state(b_c, li_c, m, L)
            wkk = wk * k.astype(F32)
            C[h] = decay * C[h] + _dot_tn(wkk.astype(BF16), v)
            NM[h, 0:1, :] = decay * n_row + jnp.sum(wkk, axis=0, keepdims=True)
            NM[h, 1:2, :] = jnp.broadcast_to(m_new, (1, dh))

    return pl.pallas_call(
        body, name="mlstm_fwd", grid=(NC,),
        in_specs=[pl.BlockSpec((L, D), lambda c: (c, 0)), pl.BlockSpec((L, D), lambda c: (c, 1)),
                  pl.BlockSpec((L, D), lambda c: (c, 5)),
                  pl.BlockSpec((L, IF_PAD), lambda c: (c, 0)),
                  pl.BlockSpec((1, 16, L), lambda c: (c, 0, 0)),
                  pl.BlockSpec((1, IF_PAD), lambda c: (0, 0)), pl.BlockSpec((16, 1), lambda c: (0, 0))],
        out_specs=[pl.BlockSpec((L, D), lambda c: (c, 0)),
                   pl.BlockSpec((1, H, dh, dh), lambda c: (c, 0, 0, 0)),
                   pl.BlockSpec((1, H, 8, dh), lambda c: (c, 0, 0, 0))],
        out_shape=[jax.ShapeDtypeStruct((S, D), F32), jax.ShapeDtypeStruct((NC, H, dh, dh), BF16),
                   jax.ShapeDtypeStruct((NC, H, 8, dh), F32)],
        scratch_shapes=[pltpu.VMEM((H, dh, dh), F32), pltpu.VMEM((H, 8, dh), F32)],
        compiler_params=_cparams(("arbitrary",)),
    )(qks, qks, z, gcol, grow, bif_row, bif_col)


def _mlstm_bwd(qks, z, gcol, grow, bif_row, bif_col, hm, dhm, cst, nst, dz, D):
    S = qks.shape[0]
    L = min(ML_CHUNK, S)
    NC = S // L
    H = N_HEADS
    dh = D // H

    def body(q_ref, k_ref, v_ref, gc_ref, gr_ref, bifr_ref, bifc_ref, hm_ref, dhm_ref, cst_ref, nst_ref, dz_any,
             dqk_ref, dg_ref, dv_ref, dC, dN):
        del dz_any
        c = pl.program_id(0)

        @pl.when(c == 0)
        def _():
            dC[...] = jnp.zeros_like(dC)
            dN[...] = jnp.zeros_like(dN)

        gc, gr, bcol_all, brow_all, ii, jj = _ml_gates(gc_ref, gr_ref.at[0], bifr_ref, bifc_ref, L)
        eye = ii == jj
        lane = lax.broadcasted_iota(jnp.int32, (L, IF_PAD), 1)
        db_all = jnp.zeros((L, IF_PAD), F32)
        dli_all = jnp.zeros((L, IF_PAD), F32)
        last_row = lax.broadcasted_iota(jnp.int32, (L, 1), 0) == L - 1
        for h in range(H):
            hs = pl.ds(h * dh, dh)
            q = q_ref[:, hs]
            k = k_ref[:, hs]
            v = v_ref[:, hs]
            qf = q.astype(F32)
            kf = k.astype(F32)
            b_c = bcol_all[:, H + h:H + h + 1]
            li_c = gc[:, h:h + 1]
            b_r = brow_all[H + h:H + h + 1, :]
            li_r = gr[h:h + 1, :]
            n_row = nst_ref[0, h, 0:1, :]
            m = nst_ref[0, h, 1:2, 0:1]
            Cb = cst_ref[0, h]
            qk, w_intra, w_inter, s, qn, den, hdiv, m_row = _ml_intra(q, k, b_c, b_r, li_r, m, n_row, ii, jj)
            dh_ = dhm_ref[:, hs]
            hh = hm_ref[:, hs]
            dnum = dh_ / hdiv
            dhdiv = -jnp.sum(dh_ * hh, axis=-1, keepdims=True) / hdiv
            dden = jnp.where(jnp.abs(den) > jnp.exp(-m_row), dhdiv * jnp.sign(den), 0.0)
            dnum_b = dnum.astype(BF16)
            ds = _dot_nt(dnum_b, v) + dden
            s_b = s.astype(BF16)
            dv = _dot_tn(s_b, dnum_b)
            dnC = _dot_nt(dnum_b, Cb)
            dwi = dden * w_inter
            dw_inter = jnp.sum(qf * dnC, axis=-1, keepdims=True) + dden * qn
            dq = w_inter * dnC + dwi * n_row
            dC_out = _dot_tn((w_inter * qf).astype(BF16), dnum_b)
            dn_out = jnp.sum(dwi * qf, axis=0, keepdims=True)
            dqk = (ds * w_intra).astype(BF16)
            dq = dq + _dot_nn(dqk, k)
            dk = _dot_tn(dqk, q)
            dd = ds * s
            rs_c = jnp.sum(dd, axis=1, keepdims=True)
            cs_r = jnp.sum(dd, axis=0, keepdims=True)
            cs_c = jnp.sum(jnp.where(eye, cs_r, 0.0), axis=1, keepdims=True)
            dinter = dw_inter * w_inter
            m_new, decay, wk = _ml_state(b_c, li_c, m, L)
            dCn = dC[h]
            dCn_b = dCn.astype(BF16)
            dn_new = dN[h, 0:1, :]
            ddecay = (jnp.sum(jnp.sum(dCn * Cb.astype(F32), axis=1, keepdims=True), axis=0, keepdims=True)
                      + jnp.sum(dn_new * n_row, axis=1, keepdims=True))
            dwkk = _dot_nt(v, dCn_b) + dn_new
            wkk_b = (wk * kf).astype(BF16)
            dv = dv + _dot_nn(wkk_b, dCn_b)
            dk = dk + wk * dwkk
            dg = jnp.sum(dwkk * kf, axis=-1, keepdims=True) * wk
            db_last = ddecay * decay + jnp.sum(dg, axis=0, keepdims=True)
            dC[h] = decay * dCn + dC_out
            dN[h, 0:1, :] = decay * dn_new + dn_out
            db_c = rs_c - cs_c + dinter - dg + jnp.where(last_row, db_last, 0.0)
            dli_c = cs_c + dg
            db_all = jnp.where(lane == H + h, db_c, db_all)
            dli_all = jnp.where(lane == h, dli_c, dli_all)
            dqk_ref[:, hs] = dq.astype(BF16)
            dqk_ref[:, pl.ds(D + h * dh, dh)] = dk.astype(BF16)
            dv_ref[:, hs] = dv.astype(BF16)
        triu = (ii <= jj).astype(BF16)
        dlf = _tri_left(triu, db_all)
        dg_ref[...] = dli_all + dlf * (1.0 - _sig(gc))

    r = lambda c: NC - 1 - c
    n_in = 12
    return pl.pallas_call(
        body, name="mlstm_bwd", grid=(NC,),
        in_specs=[pl.BlockSpec((L, D), lambda c: (r(c), 0)), pl.BlockSpec((L, D), lambda c: (r(c), 1)),
                  pl.BlockSpec((L, D), lambda c: (r(c), 5)),
                  pl.BlockSpec((L, IF_PAD), lambda c: (r(c), 0)),
                  pl.BlockSpec((1, 16, L), lambda c: (r(c), 0, 0)),
                  pl.BlockSpec((1, IF_PAD), lambda c: (0, 0)), pl.BlockSpec((16, 1), lambda c: (0, 0)),
                  pl.BlockSpec((L, D), lambda c: (r(c), 0)), pl.BlockSpec((L, D), lambda c: (r(c), 0)),
                  pl.BlockSpec((1, H, dh, dh), lambda c: (r(c), 0, 0, 0)),
                  pl.BlockSpec((1, H, 8, dh), lambda c: (r(c), 0, 0, 0)),
                  pl.BlockSpec(memory_space=pl.ANY)],
        out_specs=[pl.BlockSpec((L, 2 * D), lambda c: (r(c), 0)),
                   pl.BlockSpec((L, IF_PAD), lambda c: (r(c), 0)),
                   pl.BlockSpec((L, D), lambda c: (r(c), 5))],
        out_shape=[jax.ShapeDtypeStruct((S, 2 * D), BF16),
                   jax.ShapeDtypeStruct((S, IF_PAD), F32), jax.ShapeDtypeStruct(dz.shape, dz.dtype)],
        input_output_aliases={n_in - 1: 2},
        scratch_shapes=[pltpu.VMEM((H, dh, dh), F32), pltpu.VMEM((H, 8, dh), F32)],
        compiler_params=_cparams(("arbitrary",)),
    )(qks, qks, z, gcol, grow, bif_row, bif_col, hm, dhm, cst, nst, dz)


def _ml_post_fwd(hm, z, g_head, D):
    S = hm.shape[0]
    T = min(TOK_TILE, S)
    dh = D // N_HEADS

    def body(hm_ref, o_ref, zm_ref, g_ref, pm_ref):
        for h in range(N_HEADS):
            hs = pl.ds(h * dh, dh)
            hg = _sig(o_ref[:, hs].astype(F32)) * hm_ref[:, hs]
            rs = lax.rsqrt(jnp.mean(hg * hg, axis=-1, keepdims=True) + EPS)
            pm_ref[:, hs] = (hg * rs * g_ref[:, hs] * _silu(zm_ref[:, hs].astype(F32))).astype(BF16)

    return pl.pallas_call(
        body, name="ml_post_fwd", grid=(S // T,),
        in_specs=[pl.BlockSpec((T, D), lambda i: (i, 0)), pl.BlockSpec((T, D), lambda i: (i, 6)),
                  pl.BlockSpec((T, D), lambda i: (i, 7)), pl.BlockSpec((1, D), lambda i: (0, 0))],
        out_specs=pl.BlockSpec((T, D), lambda i: (i, 0)),
        out_shape=jax.ShapeDtypeStruct((S, D), BF16),
        compiler_params=_cparams(("parallel",)),
    )(hm, z, z, g_head)


def _ml_post_bwd(dpm, hm, z, g_head, dz, D):
    S = hm.shape[0]
    T = min(TOK_TILE, S)
    dh = D // N_HEADS

    def body(dpm_ref, hm_ref, o_ref, zm_ref, g_ref, dz_any, dhm_ref, dg_ref, dozm_ref):
        del dz_any
        do_ref = dozm_ref.at[:, pl.ds(0, D)]
        dzm_ref = dozm_ref.at[:, pl.ds(D, D)]
        i = pl.program_id(0)

        @pl.when(i == 0)
        def _():
            dg_ref[...] = jnp.zeros_like(dg_ref)

        for h in range(N_HEADS):
            hs = pl.ds(h * dh, dh)
            o = o_ref[:, hs].astype(F32)
            zm = zm_ref[:, hs].astype(F32)
            hmv = hm_ref[:, hs]
            dpm_ = dpm_ref[:, hs].astype(F32)
            g = g_ref[:, hs]
            so = _sig(o)
            hg = so * hmv
            rs = lax.rsqrt(jnp.mean(hg * hg, axis=-1, keepdims=True) + EPS)
            xn = hg * rs
            dzm_ref[:, hs] = (dpm_ * xn * g * _dsilu(zm)).astype(BF16)
            dhn = dpm_ * _silu(zm)
            dg_ref[0:1, hs] += jnp.sum(dhn * xn, axis=0, keepdims=True)
            dxn = dhn * g
            dhg = rs * (dxn - xn * jnp.mean(dxn * xn, axis=-1, keepdims=True))
            do_ref[:, hs] = (dhg * hmv * so * (1.0 - so)).astype(BF16)
            dhm_ref[:, hs] = dhg * so

    return pl.pallas_call(
        body, name="ml_post_bwd", grid=(S // T,),
        in_specs=[pl.BlockSpec((T, D), lambda i: (i, 0)), pl.BlockSpec((T, D), lambda i: (i, 0)),
                  pl.BlockSpec((T, D), lambda i: (i, 6)), pl.BlockSpec((T, D), lambda i: (i, 7)),
                  pl.BlockSpec((1, D), lambda i: (0, 0)), pl.BlockSpec(memory_space=pl.ANY)],
        out_specs=[pl.BlockSpec((T, D), lambda i: (i, 0)), pl.BlockSpec((8, D), lambda i: (0, 0)),
                   pl.BlockSpec((T, 2 * D), lambda i: (i, 3))],
        out_shape=[jax.ShapeDtypeStruct((S, D), F32), jax.ShapeDtypeStruct((8, D), F32),
                   jax.ShapeDtypeStruct(dz.shape, dz.dtype)],
        input_output_aliases={5: 2},
        compiler_params=_cparams(("arbitrary",)),
    )(dpm, hm, z, z, g_head, dz)


def _xattn_probs(q, km_h, scale):
    s = _dot_nt(q, km_h) * scale
    e = jnp.exp(s - jnp.max(s, axis=-1, keepdims=True))
    return e / jnp.sum(e, axis=-1, keepdims=True)


def _xattn_fwd(z, kv, D):
    S = z.shape[0]
    M = kv.shape[0]
    T = min(TOK_TILE, S)
    dh = D // N_HEADS
    scale = dh ** -0.5

    def body(q_ref, zx_ref, km_ref, vm_ref, px_ref):
        for h in range(N_HEADS):
            hs = pl.ds(h * dh, dh)
            p = _xattn_probs(q_ref[:, hs], km_ref[:, hs], scale)
            o = _dot_nn(p.astype(BF16), vm_ref[:, hs])
            px_ref[:, hs] = (o * _silu(zx_ref[:, hs].astype(F32))).astype(BF16)

    return pl.pallas_call(
        body, name="xattn_fwd", grid=(S // T,),
        in_specs=[pl.BlockSpec((T, D), lambda i: (i, 8)), pl.BlockSpec((T, D), lambda i: (i, 9)),
                  pl.BlockSpec((M, D), lambda i: (0, 0)), pl.BlockSpec((M, D), lambda i: (0, 1))],
        out_specs=pl.BlockSpec((T, D), lambda i: (i, 0)),
        out_shape=jax.ShapeDtypeStruct((S, D), BF16),
        compiler_params=_cparams(("parallel",)),
    )(z, z, kv, kv)


def _xattn_bwd(dpx, z, kv, dz, D):
    S = z.shape[0]
    M = kv.shape[0]
    T = min(TOK_TILE, S)
    dh = D // N_HEADS
    scale = dh ** -0.5

    def body(dpx_ref, q_ref, zx_ref, km_ref, vm_ref, dz_any, dkv_ref, dqz_ref):
        del dz_any
        i = pl.program_id(0)

        @pl.when(i == 0)
        def _():
            dkv_ref[...] = jnp.zeros_like(dkv_ref)

        for h in range(N_HEADS):
            hs = pl.ds(h * dh, dh)
            q = q_ref[:, hs]
            zx = zx_ref[:, hs].astype(F32)
            dpx_ = dpx_ref[:, hs].astype(F32)
            p = _xattn_probs(q, km_ref[:, hs], scale)
            p_b = p.astype(BF16)
            o = _dot_nn(p_b, vm_ref[:, hs])
            dqz_ref[:, pl.ds(D + h * dh, dh)] = (dpx_ * o * _dsilu(zx)).astype(BF16)
            do = (dpx_ * _silu(zx)).astype(BF16)
            dp = _dot_nt(do, vm_ref[:, hs])
            dsc = (p * (dp - jnp.sum(p * dp, axis=-1, keepdims=True)) * scale).astype(BF16)
            dqz_ref[:, hs] = _dot_nn(dsc, km_ref[:, hs]).astype(BF16)
            dkv_ref[:, hs] += _dot_tn(dsc, q)
            dkv_ref[:, pl.ds(D + h * dh, dh)] += _dot_tn(p_b, do)

    return pl.pallas_call(
        body, name="xattn_bwd", grid=(S // T,),
        in_specs=[pl.BlockSpec((T, D), lambda i: (i, 0)),
                  pl.BlockSpec((T, D), lambda i: (i, 8)), pl.BlockSpec((T, D), lambda i: (i, 9)),
                  pl.BlockSpec((M, D), lambda i: (0, 0)), pl.BlockSpec((M, D), lambda i: (0, 1)),
                  pl.BlockSpec(memory_space=pl.ANY)],
        out_specs=[pl.BlockSpec((M, 2 * D), lambda i: (0, 0)), pl.BlockSpec((T, 2 * D), lambda i: (i, 4))],
        out_shape=[jax.ShapeDtypeStruct((M, 2 * D), F32), jax.ShapeDtypeStruct(dz.shape, dz.dtype)],
        input_output_aliases={5: 1},
        compiler_params=_cparams(("arbitrary",)),
    )(dpx, z, z, kv, kv, dz)


def _col_sum(a):
    S, C = a.shape
    T = min(1024, S)

    def body(a_ref, o_ref):
        @pl.when(pl.program_id(0) == 0)
        def _():
            o_ref[...] = jnp.zeros_like(o_ref)

        o_ref[0:1, :] += jnp.sum(a_ref[...], axis=0, keepdims=True)

    return pl.pallas_call(
        body, name="col_sum", grid=(S // T,),
        in_specs=[pl.BlockSpec((T, C), lambda i: (i, 0))], out_specs=pl.BlockSpec((8, C), lambda i: (0, 0)),
        out_shape=jax.ShapeDtypeStruct((8, C), F32), compiler_params=_cparams(("arbitrary",)),
    )(a)


def _gain_grad(dn, xin, rstd):
    R, D = dn.shape

    def body(dn_ref, x_ref, r_ref, o_ref):
        o_ref[...] = jnp.zeros_like(o_ref)
        o_ref[0:1, :] = jnp.sum(dn_ref[...] * x_ref[...] * r_ref[...], axis=0, keepdims=True)

    return pl.pallas_call(
        body, name="gain_grad", out_shape=jax.ShapeDtypeStruct((8, D), F32),
        compiler_params=_cparams(),
    )(dn, xin, rstd)


def _merge_fwd(z, yc, ym, yx, D):
    S = z.shape[0]
    T = min(TOK_TILE, S)

    def body(g0, g1, g2, yc_ref, ym_ref, yx_ref, o_ref):
        o_ref[...] = (_sig(g0[...].astype(F32)) * yc_ref[...].astype(F32)
                      + _sig(g1[...].astype(F32)) * ym_ref[...].astype(F32)
                      + _sig(g2[...].astype(F32)) * yx_ref[...].astype(F32)).astype(BF16)

    tok = pl.BlockSpec((T, D), lambda i: (i, 0))
    return pl.pallas_call(
        body, name="merge_fwd", grid=(S // T,),
        in_specs=[pl.BlockSpec((T, D), lambda i: (i, 10)), pl.BlockSpec((T, D), lambda i: (i, 11)),
                  pl.BlockSpec((T, D), lambda i: (i, 12)), tok, tok, tok],
        out_specs=tok, out_shape=jax.ShapeDtypeStruct((S, D), BF16),
        compiler_params=_cparams(("parallel",)),
    )(z, z, z, yc, ym, yx)


def _merge_bwd(dmg, z, yc, ym, yx, D):
    S = z.shape[0]
    T = min(TOK_TILE // 2, S)

    def body(dm_ref, g_ref, yc_ref, ym_ref, yx_ref, dyc_ref, dym_ref, dyx_ref, dg_ref):
        j = pl.program_id(1)
        for jj, (y_ref, dy_ref) in enumerate(((yc_ref, dyc_ref), (ym_ref, dym_ref), (yx_ref, dyx_ref))):
            @pl.when(j == jj)
            def _():
                sg = _sig(g_ref[...].astype(F32))
                dm = dm_ref[...].astype(F32)
                dy_ref[...] = (dm * sg).astype(BF16)
                dg_ref[...] = (dm * y_ref[...].astype(F32) * sg * (1.0 - sg)).astype(BF16)

    tok = pl.BlockSpec((T, D), lambda i, j: (i, 0))
    return pl.pallas_call(
        body, name="merge_bwd", grid=(S // T, 3),
        in_specs=[tok, pl.BlockSpec((T, D), lambda i, j: (i, 10 + j)), tok, tok, tok],
        out_specs=[tok, tok, tok, pl.BlockSpec((T, D), lambda i, j: (i, 10 + j))],
        out_shape=[jax.ShapeDtypeStruct((S, D), BF16)] * 3 + [jax.ShapeDtypeStruct((S, 13 * D), BF16)],
        compiler_params=_cparams(("arbitrary", "arbitrary")),
    )(dmg, z, yc, ym, yx)


def _post_loss(r, x, tgt, g_post):
    S, D = r.shape
    T = min(TOK_TILE, S)

    def body(r_ref, x_ref, t_ref, g_ref, dy_ref, dr_ref, dg_ref, loss_ref):
        i = pl.program_id(0)

        @pl.when(i == 0)
        def _():
            dg_ref[...] = jnp.zeros_like(dg_ref)
            loss_ref[...] = jnp.zeros_like(loss_ref)

        rv = r_ref[...]
        g = g_ref[...]
        rs = lax.rsqrt(jnp.mean(rv * rv, axis=-1, keepdims=True) + EPS)
        nrm = rv * rs
        e = x_ref[...] + nrm * g - t_ref[...]
        part = jnp.sum(jnp.sum(e * e, axis=-1, keepdims=True), axis=0, keepdims=True) * (0.5 / D)
        loss_ref[0:1, 0:1] += part
        dy = e * (1.0 / D)
        dy_ref[...] = dy
        dg_ref[0:1, :] += jnp.sum(dy * nrm, axis=0, keepdims=True)
        dn = dy * g
        dr_ref[...] = (rs * (dn - nrm * jnp.mean(dn * nrm, axis=-1, keepdims=True))).astype(BF16)

    tok = pl.BlockSpec((T, D), lambda i: (i, 0))
    return pl.pallas_call(
        body, name="post_loss", grid=(S // T,),
        in_specs=[tok, tok, tok, pl.BlockSpec((1, D), lambda i: (0, 0))],
        out_specs=[tok, tok, pl.BlockSpec((8, D), lambda i: (0, 0)), pl.BlockSpec((8, 128), lambda i: (0, 0))],
        out_shape=[jax.ShapeDtypeStruct((S, D), F32), jax.ShapeDtypeStruct((S, D), BF16),
                   jax.ShapeDtypeStruct((8, D), F32), jax.ShapeDtypeStruct((8, 128), F32)],
        compiler_params=_cparams(("arbitrary",)),
    )(r, x, tgt, g_post)


def _conv_bwd1(dpc, z, u1, g_ln, b_ln, dz, D):
    S = z.shape[0]
    T = min(TOK_TILE // 2, S)

    def body(dpc_ref, zc_ref, u1_ref, gln_ref, bln_ref, dz_any, du1_ref, st_ref, dzc_ref):
        del dz_any
        i = pl.program_id(0)

        @pl.when(i == 0)
        def _():
            st_ref[...] = jnp.zeros_like(st_ref)

        u1v = u1_ref[...]
        mu = jnp.mean(u1v, axis=-1, keepdims=True)
        xc = u1v - mu
        rs = lax.rsqrt(jnp.mean(xc * xc, axis=-1, keepdims=True) + EPS)
        xh = xc * rs
        g = gln_ref[...]
        ln = xh * g + bln_ref[...]
        zc = zc_ref[...].astype(F32)
        dpc_ = dpc_ref[...].astype(F32)
        dzc_ref[...] = (dpc_ * _silu(ln) * _dsilu(zc)).astype(BF16)
        dln = dpc_ * _silu(zc) * _dsilu(ln)
        st_ref[0:1, :] += jnp.sum(dln * xh, axis=0, keepdims=True)
        st_ref[1:2, :] += jnp.sum(dln, axis=0, keepdims=True)
        dxh = dln * g
        du1 = rs * (dxh - jnp.mean(dxh, axis=-1, keepdims=True) - xh * jnp.mean(dxh * xh, axis=-1, keepdims=True))
        du1_ref[...] = du1
        st_ref[2:3, :] += jnp.sum(du1, axis=0, keepdims=True)

    tok = pl.BlockSpec((T, D), lambda i: (i, 0))
    vec = pl.BlockSpec((1, D), lambda i: (0, 0))
    return pl.pallas_call(
        body, name="conv_bwd1", grid=(S // T,),
        in_specs=[tok, pl.BlockSpec((T, D), lambda i: (i, 2)), tok, vec, vec, pl.BlockSpec(memory_space=pl.ANY)],
        out_specs=[tok, pl.BlockSpec((8, D), lambda i: (0, 0)), pl.BlockSpec((T, D), lambda i: (i, 2))],
        out_shape=[jax.ShapeDtypeStruct((S, D), F32), jax.ShapeDtypeStruct((8, D), F32),
                   jax.ShapeDtypeStruct(dz.shape, dz.dtype)],
        input_output_aliases={5: 2},
        compiler_params=_cparams(("arbitrary",)),
    )(dpc, z, u1, g_ln, b_ln, dz)


def _conv_bwd2(du1, z, w_dw, dz, D):
    S = z.shape[0]
    T = min(TOK_TILE, S)
    hb = T // HALO
    last = S // HALO - 1

    n_steps = S // T

    def body(du_ref, dun_ref, a_ref, b_ref, ah_ref, bh_ref, w_ref, dz_any, dw_ref, dab_ref,
             dus, u0s, shd, shu, dwp):
        del dz_any
        i = pl.program_id(0)

        @pl.when(i == 0)
        def _():
            dwp[...] = jnp.zeros_like(dwp)

        dus[pl.ds(0, T), :] = du_ref[...]
        dus[pl.ds(T, HALO), :] = jnp.where(i < n_steps - 1, dun_ref[...], 0.0)
        dus[pl.ds(T + HALO, 8), :] = jnp.zeros((8, D), F32)
        u0s[pl.ds(HALO, T), :] = a_ref[...].astype(F32) * _sig(b_ref[...].astype(F32))
        halo = ah_ref[...].astype(F32) * _sig(bh_ref[...].astype(F32))
        u0s[pl.ds(0, HALO), :] = jnp.where(i > 0, halo, 0.0)
        u0s[pl.ds(T + HALO, 8), :] = jnp.zeros((8, D), F32)
        for c0 in range(0, D, LANE):
            cs = pl.ds(c0, LANE)
            _shift_strip(dus, shd, cs, T + HALO)
            _shift_strip(u0s, shu, cs, T + HALO)
            for r0 in range(0, T, ROW_BLK):
                rows = pl.ds(r0, ROW_BLK)
                du0 = jnp.zeros((ROW_BLK, LANE), F32)
                for j in range(CONV_W):
                    du0 = du0 + w_ref[pl.ds(j, 1), cs] * _tap(dus, shd, cs, CONV_W - 1 - j, r0)
                a = a_ref[rows, cs].astype(F32)
                sb = _sig(b_ref[rows, cs].astype(F32))
                dab_ref[rows, cs] = (du0 * sb).astype(BF16)
                dab_ref[rows, pl.ds(D + c0, LANE)] = (du0 * a * sb * (1.0 - sb)).astype(BF16)
                d = dus[rows, cs]
                for j in range(CONV_W):
                    prod = d * _tap(u0s, shu, cs, HALO - (CONV_W - 1) + j, r0)
                    part = prod[0:8]
                    for q in range(8, ROW_BLK, 8):
                        part = part + prod[q:q + 8]
                    dwp[pl.ds(8 * j, 8), cs] += part

        @pl.when(i == n_steps - 1)
        def _():
            dw_ref[...] = jnp.zeros_like(dw_ref)
            for j in range(CONV_W):
                dw_ref[pl.ds(j, 1), :] = jnp.sum(dwp[pl.ds(8 * j, 8), :], axis=0, keepdims=True)

    tok = pl.BlockSpec((T, D), lambda i: (i, 0))
    nxt = lambda i: (jnp.minimum((i + 1) * hb, last), 0)
    prev = lambda i: (jnp.maximum(i * hb - 1, 0), 0)
    prev_b = lambda i: (jnp.maximum(i * hb - 1, 0), 1)
    return pl.pallas_call(
        body, name="conv_bwd2", grid=(S // T,),
        in_specs=[tok, pl.BlockSpec((HALO, D), nxt), tok, pl.BlockSpec((T, D), lambda i: (i, 1)),
                  pl.BlockSpec((HALO, D), prev), pl.BlockSpec((HALO, D), prev_b),
                  pl.BlockSpec((32, D), lambda i: (0, 0)), pl.BlockSpec(memory_space=pl.ANY)],
        out_specs=[pl.BlockSpec((32, D), lambda i: (0, 0)), pl.BlockSpec((T, 2 * D), lambda i: (i, 0))],
        out_shape=[jax.ShapeDtypeStruct((32, D), F32), jax.ShapeDtypeStruct(dz.shape, dz.dtype)],
        input_output_aliases={7: 1},
        scratch_shapes=[pltpu.VMEM((T + HALO + 8, D), F32), pltpu.VMEM((T + HALO + 8, D), F32),
                        pltpu.VMEM((8, T + HALO, LANE), F32), pltpu.VMEM((8, T + HALO, LANE), F32),
                        pltpu.VMEM((8 * 32, D), F32)],
        compiler_params=_cparams(("arbitrary",)),
    )(du1, du1, z, z, z, z, w_dw, dz)


def _qk_bwd1(dqks, z, w_qk, D):
    S = z.shape[0]
    T = min(TOK_TILE, S)
    hb = T // HALO
    scale = (D // N_HEADS) ** -0.5
    W2 = 2 * D

    n_steps = S // T

    def body(d_ref, p0_ref, p1_ref, h0_ref, h1_ref, w_ref, dc_ref, dw_ref, ps, sh, dwp):
        i = pl.program_id(0)

        @pl.when(i == 0)
        def _():
            dwp[...] = jnp.zeros_like(dwp)

        _qk_fill(ps, p0_ref, p1_ref, h0_ref, h1_ref, T, D)
        for c0 in range(0, W2, LANE):
            cs = pl.ds(c0, LANE)
            _shift_strip(ps, sh, cs, T + HALO, _QK_RES)
            for r0 in range(0, T, ROW_BLK):
                rows = pl.ds(r0, ROW_BLK)
                dc = d_ref[rows, cs].astype(F32) * _dsilu(_qk_conv(ps, sh, w_ref, cs, r0))
                if c0 >= D:
                    dc = dc * scale
                dc_ref[rows, cs] = dc.astype(BF16)
                for j in range(QK_W):
                    prod = dc * _tap(ps, sh, cs, HALO - (QK_W - 1) + j, r0)
                    part = prod[0:8]
                    for q in range(8, ROW_BLK, 8):
                        part = part + prod[q:q + 8]
                    dwp[pl.ds(8 * j, 8), cs] += part

        @pl.when(i == n_steps - 1)
        def _():
            dw_ref[...] = jnp.zeros_like(dw_ref)
            for j in range(QK_W):
                dw_ref[pl.ds(j, 1), :] = jnp.sum(dwp[pl.ds(8 * j, 8), :], axis=0, keepdims=True)

    prev0 = lambda i: (jnp.maximum(i * hb - 1, 0), 3)
    prev1 = lambda i: (jnp.maximum(i * hb - 1, 0), 4)
    return pl.pallas_call(
        body, name="qk_bwd1", grid=(S // T,),
        in_specs=[pl.BlockSpec((T, W2), lambda i: (i, 0)),
                  pl.BlockSpec((T, D), lambda i: (i, 3)), pl.BlockSpec((T, D), lambda i: (i, 4)),
                  pl.BlockSpec((HALO, D), prev0), pl.BlockSpec((HALO, D), prev1),
                  pl.BlockSpec((8, W2), lambda i: (0, 0))],
        out_specs=[pl.BlockSpec((T, W2), lambda i: (i, 0)), pl.BlockSpec((8, W2), lambda i: (0, 0))],
        out_shape=[jax.ShapeDtypeStruct((S, W2), BF16), jax.ShapeDtypeStruct((8, W2), F32)],
        scratch_shapes=[pltpu.VMEM((T + HALO + 8, W2), F32), pltpu.VMEM((8, T + HALO, LANE), F32),
                        pltpu.VMEM((8 * QK_W, W2), F32)],
        compiler_params=_cparams(("arbitrary",)),
    )(dqks, z, z, z, z, w_qk)


def _qk_bwd2(dc, w_qk, dz, D):
    S = dc.shape[0]
    T = min(TOK_TILE, S)
    hb = T // HALO
    last = S // HALO - 1

    n_steps = S // T

    def body(d_ref, dn_ref, w_ref, dz_any, o_ref, ds, sh):
        del dz_any
        i = pl.program_id(0)
        ds[pl.ds(0, T), :] = d_ref[...].astype(F32)
        ds[pl.ds(T, HALO), :] = jnp.where(i < n_steps - 1, dn_ref[...].astype(F32), 0.0)
        ds[pl.ds(T + HALO, 8), :] = jnp.zeros((8, D), F32)
        for c0 in range(0, D, LANE):
            cs = pl.ds(c0, LANE)
            _shift_strip(ds, sh, cs, T + HALO, range(1, QK_W))
            for r0 in range(0, T, ROW_BLK):
                acc = jnp.zeros((ROW_BLK, LANE), F32)
                for j in range(QK_W):
                    acc = acc + w_ref[pl.ds(j, 1), cs] * _tap(ds, sh, cs, QK_W - 1 - j, r0)
                o_ref[pl.ds(r0, ROW_BLK), cs] = acc.astype(BF16)

    return pl.pallas_call(
        body, name="qk_bwd2", grid=(S // T, 2),
        in_specs=[pl.BlockSpec((T, D), lambda i, j: (i, j)),
                  pl.BlockSpec((HALO, D), lambda i, j: (jnp.minimum((i + 1) * hb, last), j)),
                  pl.BlockSpec((8, D), lambda i, j: (0, j)), pl.BlockSpec(memory_space=pl.ANY)],
        out_specs=pl.BlockSpec((T, D), lambda i, j: (i, 3 + j)),
        out_shape=jax.ShapeDtypeStruct(dz.shape, dz.dtype),
        input_output_aliases={3: 0},
        scratch_shapes=[pltpu.VMEM((T + HALO + 8, D), F32), pltpu.VMEM((8, T + HALO, LANE), F32)],
        compiler_params=_cparams(("arbitrary", "arbitrary")),
    )(dc, dc, w_qk, dz)


TILE_BUDGET = 12 * 1024 * 1024


def _tile_2d(R, C, elem_bytes):
    cands = [t for t in range(BF16_ROWS, R + 1, BF16_ROWS) if R % t == 0 and t * C * elem_bytes <= TILE_BUDGET]
    if cands:
        return max(cands), C
    if R * C * elem_bytes <= TILE_BUDGET or C % LANE:
        return R, C
    cands = [t for t in range(LANE, C + 1, LANE) if C % t == 0 and t * R * elem_bytes <= TILE_BUDGET]
    return R, (max(cands) if cands else LANE)


def _sum_parts(parts, name):
    n, R, C = parts.shape
    tr, tc = _tile_2d(R, C, n * parts.dtype.itemsize + 4)

    def body(p_ref, o_ref):
        g = p_ref[0].astype(F32)
        for s in range(1, n):
            g = g + p_ref[s].astype(F32)
        o_ref[...] = g

    return pl.pallas_call(
        body, name=name, grid=(R // tr, C // tc),
        in_specs=[pl.BlockSpec((n, tr, tc), lambda i, j: (0, i, j))],
        out_specs=pl.BlockSpec((tr, tc), lambda i, j: (i, j)),
        out_shape=jax.ShapeDtypeStruct((R, C), F32),
        compiler_params=_cparams(("parallel", "parallel")),
    )(parts)


def _adamw(parts, w, m, v, name):
    R, C = w.shape
    n = parts.shape[0]
    tr, tc = _tile_2d(R, C, 4 * 7 + n * parts.dtype.itemsize)
    c1 = 1.0 / (1.0 - ADAM_B1 ** ADAM_STEP)
    c2 = 1.0 / (1.0 - ADAM_B2 ** ADAM_STEP)

    def body(p_ref, w_ref, m_ref, v_ref, g_ref, d_ref, nm_ref, nv_ref):
        g = p_ref[0].astype(F32)
        for s in range(1, n):
            g = g + p_ref[s].astype(F32)
        g_ref[...] = g
        mn = ADAM_B1 * m_ref[...] + (1.0 - ADAM_B1) * g
        vn = ADAM_B2 * v_ref[...] + (1.0 - ADAM_B2) * (g * g)
        nm_ref[...] = mn
        nv_ref[...] = vn
        d_ref[...] = -ADAM_LR * ((mn * c1) / (jnp.sqrt(vn * c2) + ADAM_EPS) + ADAM_WD * w_ref[...])

    blk = pl.BlockSpec((tr, tc), lambda i, j: (i, j))
    return pl.pallas_call(
        body, name=name, grid=(R // tr, C // tc),
        in_specs=[pl.BlockSpec((n, tr, tc), lambda i, j: (0, i, j)), blk, blk, blk],
        out_specs=[blk, blk, blk, blk],
        out_shape=[jax.ShapeDtypeStruct((R, C), F32)] * 4,
        compiler_params=_cparams(("parallel", "parallel")),
    )(parts, w, m, v)


def _exchange(operands, scatter, name):
    n = len(operands)

    def body(*refs):
        copies = _xchg_copies(refs[:n], refs[n:2 * n], *refs[2 * n:], scatter)
        for cp in copies:
            cp.start()
        for cp in copies:
            cp.wait()

    hbm = pl.BlockSpec(memory_space=pltpu.HBM)
    return pl.pallas_call(
        body, name=name, in_specs=[hbm] * n, out_specs=[hbm] * n, out_shape=_xchg_out_shapes(operands, scatter),
        scratch_shapes=_xchg_sems(n), compiler_params=pltpu.CompilerParams(has_side_effects=True),
    )(*operands)


N_CHIP = 4


class _CoreGather:
    def __init__(self, core):
        self.core = core

    def out_shape(self, blk):
        return jax.ShapeDtypeStruct((N_CHIP,) + blk.shape, blk.dtype)

    @staticmethod
    def sems():
        return [pltpu.SemaphoreType.DMA((7,)), pltpu.SemaphoreType.DMA((7,)), pltpu.SemaphoreType.DMA]

    def _parts(self, src, out, send_sems, recv_sems, loc_sem):
        x, y, c = lax.axis_index("x"), lax.axis_index("y"), lax.axis_index("c")
        me, sibling = (x, y, c), (x, y, 1 - c)
        chips = [(1 - x, y), (x, 1 - y), (1 - x, 1 - y)]

        def copy(k, chip, to, from_src=False):
            slot = out.at[2 * chip[0] + chip[1]]
            return pltpu.make_async_remote_copy(
                src_ref=src if from_src else slot, dst_ref=slot, send_sem=send_sems.at[k], recv_sem=recv_sems.at[k],
                device_id=to, device_id_type=pl.DeviceIdType.MESH)

        mine = pltpu.make_async_copy(src, out.at[2 * x + y], loc_sem)
        first = [copy(0, (x, y), sibling, True)] + [copy(1 + j, (x, y), (*ch, c), True) for j, ch in enumerate(chips)]
        landed = [copy(1 + j, ch, me) for j, ch in enumerate(chips)]
        passed = [copy(4 + j, ch, sibling) for j, ch in enumerate(chips)]
        from_sibling = [copy(0, (x, y), me)] + [copy(4 + j, ch, me) for j, ch in enumerate(chips)]
        return c == self.core, mine, first, landed, passed, from_sibling

    def start(self, *refs):
        holder, mine, first, _, _, _ = self._parts(*refs)

        @pl.when(holder)
        def _():
            mine.start()
            for cp in first:
                cp.start()

    def forward(self, *refs):
        holder, _, _, landed, passed, _ = self._parts(*refs)

        @pl.when(holder)
        def _():
            for got, cp in zip(landed, passed):
                got.wait_recv()
                cp.start()

    def finish(self, *refs):
        holder, mine, first, _, passed, from_sibling = self._parts(*refs)

        @pl.when(holder)
        def _():
            for cp in first + passed:
                cp.wait_send()
            mine.wait()

        @pl.when(jnp.logical_not(holder))
        def _():
            for cp in from_sibling:
                cp.wait_recv()


def _gather2_copies(srcs, dsts, send_sems, recv_sems, loc_sems):
    x, y, c = lax.axis_index("x"), lax.axis_index("y"), lax.axis_index("c")
    me, sibling = (x, y, c), (x, y, 1 - c)
    chips = [(1 - x, y), (x, 1 - y), (1 - x, 1 - y)]
    mine, first, landed, passed, arriving = [], [], [], [], []
    for t, (src, dst) in enumerate(zip(srcs, dsts)):
        def copy(k, block, to, from_src=False, src=src, dst=dst, t=t):
            slot = dst.at[4 * block[0] + 2 * block[1] + block[2]]
            return pltpu.make_async_remote_copy(
                src_ref=src if from_src else slot, dst_ref=slot,
                send_sem=send_sems.at[7 * t + k], recv_sem=recv_sems.at[7 * t + k],
                device_id=to, device_id_type=pl.DeviceIdType.MESH)

        mine.append(pltpu.make_async_copy(src, dst.at[4 * x + 2 * y + c], loc_sems.at[t]))
        first += [copy(0, me, sibling, True)] + [copy(1 + j, me, (*ch, c), True) for j, ch in enumerate(chips)]
        landed += [copy(1 + j, (*ch, c), me) for j, ch in enumerate(chips)]
        passed += [copy(4 + j, (*ch, c), sibling) for j, ch in enumerate(chips)]
        arriving += [copy(0, sibling, me)] + [copy(4 + j, (*ch, 1 - c), me) for j, ch in enumerate(chips)]
    return mine, first, landed, passed, arriving


def _core_gather(blk, core, name):
    g = _CoreGather(core)

    def body(*refs):
        g.start(*refs)
        g.forward(*refs)
        g.finish(*refs)

    hbm = pl.BlockSpec(memory_space=pltpu.HBM)
    return pl.pallas_call(
        body, name=name, in_specs=[hbm], out_specs=hbm, out_shape=g.out_shape(blk), scratch_shapes=g.sems(),
        compiler_params=pltpu.CompilerParams(has_side_effects=True),
    )(blk)


IN_TILE = 512


def _tile_index(tiles):
    jumps = [(i, tiles[i] - tiles[i - 1] - 1) for i in range(1, len(tiles)) if tiles[i] != tiles[i - 1] + 1]

    def f(t):
        j = t + tiles[0]
        for i, gap in jumps:
            j = j + jnp.where(t >= i, gap, 0)
        return j

    return f


def _in_proj_phase(h, w_rows, tiles, z_prev, name, gather=None, xchg=None):
    S, D = h.shape
    nt = len(tiles)
    assert w_rows.shape == (nt * IN_TILE, D)
    tm = _fit(S, 2048)
    ni = S // tm
    grid = (ni, nt)
    mid = (3 * ni * nt) // 4
    col = _tile_index(tiles)
    n_in = 2 + (z_prev is not None)
    g_ops = [gather[0]] if gather else []
    x_ops = list(xchg[0]) if xchg else []
    n_g, n_x = len(g_ops), len(x_ops)

    def body(*refs):
        h_ref, w_ref = refs[0], refs[1]
        g_src = refs[n_in:n_in + n_g]
        x_src = refs[n_in + n_g:n_in + n_g + n_x]
        o = n_in + n_g + n_x
        z_ref = refs[o]
        g_dst = refs[o + 1:o + 1 + n_g]
        x_dst = refs[o + 1 + n_g:o + 1 + n_g + n_x]
        sems = refs[o + 1 + n_g + n_x:]
        g_sems, x_sems = sems[:3 * n_g], sems[3 * n_g:]
        i, t = pl.program_id(0), pl.program_id(1)
        first = (t == 0) & (i == 0)
        last = (t == nt - 1) & (i == ni - 1)

        @pl.when(first)
        def _():
            if gather:
                gather[1].start(g_src[0], g_dst[0], *g_sems)
            if xchg:
                mine, out, _, _, _ = _gather2_copies(x_src, x_dst, *x_sems)
                for cp in mine + out:
                    cp.start()

        z_ref[...] = _dot_nt(h_ref[...], w_ref[...]).astype(z_ref.dtype)

        @pl.when((i == mid // nt) & (t == mid % nt))
        def _():
            if gather:
                gather[1].forward(g_src[0], g_dst[0], *g_sems)
            if xchg:
                _, _, landed, passed, _ = _gather2_copies(x_src, x_dst, *x_sems)
                for got, cp in zip(landed, passed):
                    got.wait_recv()
                    cp.start()

        @pl.when(last)
        def _():
            if gather:
                gather[1].finish(g_src[0], g_dst[0], *g_sems)
            if xchg:
                mine, out, _, passed, arriving = _gather2_copies(x_src, x_dst, *x_sems)
                for cp in arriving:
                    cp.wait_recv()
                for cp in out + passed:
                    cp.wait_send()
                for cp in mine:
                    cp.wait()

    hbm = pl.BlockSpec(memory_space=pltpu.HBM)
    in_specs = [pl.BlockSpec((tm, D), lambda i, t: (i, 0)), pl.BlockSpec((IN_TILE, D), lambda i, t: (t, 0))]
    args = [h, w_rows]
    aliases = {}
    if z_prev is not None:
        in_specs.append(pl.BlockSpec(memory_space=pl.ANY))
        args.append(z_prev)
        aliases = {2: 0}
    in_specs += [hbm] * (n_g + n_x)
    args += g_ops + x_ops
    out_specs = [pl.BlockSpec((tm, IN_TILE), lambda i, t: (i, col(t)))] + [hbm] * (n_g + n_x)
    out_shape = [jax.ShapeDtypeStruct((S, 13 * D), BF16)]
    scratch = []
    if gather:
        out_shape.append(gather[1].out_shape(gather[0]))
        scratch += gather[1].sems()
    if xchg:
        out_shape += _xchg_out_shapes(x_ops, xchg[1])
        scratch += _xchg_sems(n_x)
    comm = bool(gather or xchg)
    res = pl.pallas_call(
        body, name=name, grid=grid, in_specs=in_specs, out_specs=out_specs, out_shape=out_shape,
        input_output_aliases=aliases, scratch_shapes=scratch,
        compiler_params=pltpu.CompilerParams(vmem_limit_bytes=VMEM_LIMIT, has_side_effects=comm,
                                             dimension_semantics=("arbitrary", "arbitrary")),
    )(*args)
    z = res[0]
    g_out = res[1] if gather else None
    x_out = list(res[1 + n_g:]) if xchg else None
    return z, g_out, x_out


def _in_proj_dx(dz, wpT, dgates, wifT, x, rstd, dy, g, side=None):
    S, K = dz.shape
    D = wpT.shape[1]
    tm, tk = _fit(S, 512), _fit(K, 2048)
    nm, nk = S // tm, K // tk
    n_side = len(side[0]) if side else 0

    rc = min(tm, 128)

    def body(*refs):
        a_ref, b_ref, dgt_ref, wif_ref, x_hbm, r_ref, dy_hbm, g_ref = refs[:8]
        srcs = refs[8:8 + n_side]
        dx_ref, dg_ref = refs[8 + n_side], refs[9 + n_side]
        dsts = refs[10 + n_side:10 + 2 * n_side]
        acc, xbuf, dybuf, ep_sems = refs[10 + 2 * n_side:14 + 2 * n_side]
        sems = refs[14 + 2 * n_side:]
        i, k = pl.program_id(0), pl.program_id(1)
        rows = pl.ds(pl.multiple_of(i * tm, tm), tm)
        fetch = [pltpu.make_async_copy(x_hbm.at[rows, :], xbuf, ep_sems.at[0]),
                 pltpu.make_async_copy(dy_hbm.at[rows, :], dybuf, ep_sems.at[1])]

        @pl.when((i == 0) & (k == 0))
        def _():
            dg_ref[...] = jnp.zeros_like(dg_ref)
            if n_side:
                for cp in _xchg_copies(srcs, dsts, *sems, side[1]):
                    cp.start()

        @pl.when(k == nk - 2)
        def _():
            for cp in fetch:
                cp.start()

        p = _dot_nn(a_ref[...], b_ref[...])

        @pl.when(k == 0)
        def _():
            acc[...] = p

        @pl.when((k > 0) & (k < nk - 1))
        def _():
            acc[...] += p

        @pl.when(k == nk - 1)
        def _():
            acc[...] += p + _dot_nn(dgt_ref[...], wif_ref[...])
            for cp in fetch:
                cp.wait()
            for r0 in range(0, tm, rc):
                rr = pl.ds(r0, rc)
                dh = acc[rr, :]
                rs = r_ref[rr, :]
                xn = xbuf[rr, :] * rs
                dg_ref[0:1, :] += jnp.sum(dh * xn, axis=0, keepdims=True)
                dxn = dh * g_ref[...]
                dx_ref[rr, :] = dybuf[rr, :] + rs * (dxn - xn * jnp.mean(dxn * xn, axis=-1, keepdims=True))

        if n_side:
            @pl.when((i == nm - 1) & (k == nk - 1))
            def _():
                for cp in _xchg_copies(srcs, dsts, *sems, side[1]):
                    cp.wait()

    assert nk > 1
    hbm = pl.BlockSpec(memory_space=pltpu.HBM)
    tok = pl.BlockSpec((tm, D), lambda i, k: (i, 0))
    in_specs = [pl.BlockSpec((tm, tk), lambda i, k: (i, k)), pl.BlockSpec((tk, D), lambda i, k: (k, 0)),
                pl.BlockSpec((tm, IF_PAD), lambda i, k: (i, 0)), pl.BlockSpec((IF_PAD, D), lambda i, k: (0, 0)),
                hbm, pl.BlockSpec((tm, 1), lambda i, k: (i, 0)), hbm, pl.BlockSpec((1, D), lambda i, k: (0, 0))]
    out_specs = [tok, pl.BlockSpec((8, D), lambda i, k: (0, 0))]
    out_shape = [jax.ShapeDtypeStruct((S, D), F32), jax.ShapeDtypeStruct((8, D), F32)]
    scratch = [pltpu.VMEM((tm, D), F32), pltpu.VMEM((tm, D), F32), pltpu.VMEM((tm, D), F32),
               pltpu.SemaphoreType.DMA((2,))]
    args = [dz, wpT, dgates, wifT, x, rstd, dy, g]
    if n_side:
        in_specs += [hbm] * n_side
        args += list(side[0])
        out_specs += [hbm] * n_side
        out_shape += _xchg_out_shapes(side[0], side[1])
        scratch += _xchg_sems(n_side)
    res = pl.pallas_call(
        body, name="in_proj_dx", grid=(nm, nk), in_specs=in_specs, out_specs=out_specs, out_shape=out_shape,
        scratch_shapes=scratch,
        compiler_params=pltpu.CompilerParams(vmem_limit_bytes=VMEM_LIMIT, has_side_effects=bool(n_side),
                                             dimension_semantics=("arbitrary", "arbitrary")),
    )(*args)
    return res[0], res[1], list(res[2:])


LATE = ('w_conv_out', 'w_ml_out', 'w_xa_out', 'w_out', 'w_mem_kv', 'w_qk_conv', 'w_dw')
ROW_SHARDED = ('w_conv_out', 'w_ml_out', 'w_xa_out', 'w_out')
COL_SHARDED = ('w_in', 'w_mem_kv', 'w_qk_conv', 'w_dw')


def _cols_to_slabs(g):
    R, C8 = g.shape
    return g.reshape(R, N_DEV, C8 // N_DEV).transpose(1, 0, 2)


def _slabs_to_cols(a):
    n, R, C = a.shape
    return a.transpose(1, 0, 2).reshape(R, n * C)


def _assemble(name, slabs):
    if name in ROW_SHARDED:
        return slabs.reshape(slabs.shape[0] * slabs.shape[1], slabs.shape[2])
    return _slabs_to_cols(slabs)


def _to_slabs(name, g, dtype):
    if name in ROW_SHARDED:
        return g.reshape(N_DEV, g.shape[0] // N_DEV, g.shape[1]).astype(dtype)
    return _cols_to_slabs(g).astype(dtype)


def _w_in_rows(D):
    n = (13 * D + N_IF) // N_DEV
    gate_dev, gate_row = (8 * D) // n, (8 * D) % n
    assert gate_row + N_IF <= n
    height = -(-(n + BF16_ROWS - 1) // BF16_ROWS) * BF16_ROWS

    def first(p):
        return n * p - jnp.where(n * p >= 8 * D + N_IF, N_IF, 0)

    def start(p):
        return jnp.minimum((first(p) // BF16_ROWS) * BF16_ROWS, 13 * D - height)

    return n, gate_dev, gate_row, height, first, start


def _w_in_tiles(D):
    n, gate_dev, *_ = _w_in_rows(D)

    def rows(s):
        f = n * s - (N_IF if n * s >= 8 * D + N_IF else 0)
        return f, f + n - (N_IF if s == gate_dev else 0)

    tiles = range(13 * D // IN_TILE)
    inside = lambda j, s: rows(s)[0] <= j * IN_TILE and (j + 1) * IN_TILE <= rows(s)[1]
    early = [j for j in tiles if any(inside(j, s) for s in range(0, N_DEV, 2))]
    return early, [j for j in tiles if j not in early], rows


def _w_window_height(D):
    n = (13 * D + N_IF) // N_DEV
    return -(-(n + IN_TILE - 1) // IN_TILE) * IN_TILE


def _w_window(block_t, me, D):
    n, gate_dev, gate_row, _, first, _ = _w_in_rows(D)
    no_gates = jnp.concatenate([block_t[:gate_row], block_t[gate_row + N_IF:], jnp.zeros((N_IF, D), block_t.dtype)])
    mine = jnp.where(me == gate_dev, no_gates, block_t)
    off = first(me) % IN_TILE
    return lax.dynamic_update_slice(jnp.zeros((_w_window_height(D), D), block_t.dtype), mine, (off, 0))


def _w_rows_of_tiles(tiles, windows, D):
    _, _, rows = _w_in_tiles(D)
    runs = []
    for j in tiles:
        own = tuple(s for s in range(N_DEV) if rows(s)[0] < (j + 1) * IN_TILE and j * IN_TILE < rows(s)[1])
        if runs and runs[-1][0] == own and runs[-1][1] + runs[-1][2] == j:
            runs[-1] = (own, runs[-1][1], runs[-1][2] + 1)
        else:
            runs.append((own, j, 1))
    pieces = []
    for own, j, k in runs:
        cut = [windows[s][(j - rows(s)[0] // IN_TILE) * IN_TILE:(j + k - rows(s)[0] // IN_TILE) * IN_TILE] for s in own]
        pieces.append(functools.reduce(jnp.add, cut))
    return jnp.concatenate(pieces, axis=0)


def _interleave_tiles(parts):
    where = {j: (rows, i) for rows, tiles in parts for i, j in enumerate(tiles)}
    out, j, n_tiles = [], 0, len(where)
    while j < n_tiles:
        rows, i = where[j]
        k = 1
        while j + k in where and where[j + k][0] is rows and where[j + k][1] == i + k:
            k += 1
        out.append(rows[i * IN_TILE:(i + k) * IN_TILE])
        j += k
    return jnp.concatenate(out, axis=0)


def _local_step(x, mem, tgt, g_pre, w_early, w_rest, wifT, b_if, b_dw, g_ln, b_ln, g_head, g_mem, g_post, late, dist):
    S, D = x.shape
    L = min(ML_CHUNK, S)
    NC = S // L
    bif_row = jnp.zeros((1, IF_PAD), F32).at[0, :N_IF].set(b_if)
    bif_col = jnp.zeros((16, 1), F32).at[:N_IF, 0].set(b_if)
    early, rest, _ = _w_in_tiles(D)

    h, rstd = _rmsnorm_fwd(x, g_pre, "prenorm_fwd")
    if dist:
        my_block, rows_of_rest = w_rest
        z, north, _ = _in_proj_phase(h, w_early, early, None, "in_proj_early", gather=(my_block, _CoreGather(1)))
        w_rest = rows_of_rest(north)
        z, _, gathered = _in_proj_phase(h, w_rest, rest, z, "in_proj_rest",
                                        xchg=([late[n] for n in LATE] + [wifT], False))
        full = {n: _assemble(n, a) for n, a in zip(LATE, gathered)}
        wifT = jnp.zeros((IF_PAD, D), BF16).at[:N_IF].set(gathered[-1][_w_in_rows(D)[1], :N_IF])
    else:
        z, _, _ = _in_proj_phase(h, w_early, early, None, "in_proj_early")
        z, _, _ = _in_proj_phase(h, w_rest, rest, z, "in_proj_rest")
        full = late
    wpT = _interleave_tiles([(w_early, early), (w_rest, rest)])
    w_co, w_mo, w_xo, w_out, w_kv = (full[n] for n in LATE[:5])
    w_qk = jnp.zeros((8, 2 * D), F32).at[:QK_W].set(full['w_qk_conv'])
    w_dw = jnp.zeros((32, D), F32).at[:CONV_W].set(full['w_dw'])
    gcol = _mm(h, wifT, mode="nt", out_dtype=F32, tm=1024, tn=IF_PAD, tk=2048, name="gates_col")
    grow = _mm(wifT[:16], h, mode="nt", out_dtype=F32, tm=16, tn=1024, tk=2048, name="gates_row")
    grow = grow.reshape(16, NC, L).transpose(1, 0, 2)

    u1, pc = _conv_fwd(z, w_dw, b_dw, g_ln, b_ln, D)
    qks = _qk_fwd(z, w_qk, D)
    hm, cst, nst = _mlstm_fwd(qks, z, gcol, grow, bif_row, bif_col, D)
    pm = _ml_post_fwd(hm, z, g_head, D)
    memn, rstd_mem = _rmsnorm_fwd(mem, g_mem, "memnorm_fwd")
    kv = _mm(memn, w_kv, mode="nn", out_dtype=BF16, tm=256, tn=1024, tk=2048, name="mem_kv")
    px = _xattn_fwd(z, kv, D)

    proj = functools.partial(_mm, mode="nn", tm=1024, tn=1024, tk=2048)
    yc = proj(pc, w_co, out_dtype=BF16, name="conv_out")
    ym = proj(pm, w_mo, out_dtype=BF16, name="ml_out")
    yx = proj(px, w_xo, out_dtype=BF16, name="xa_out")
    mg = _merge_fwd(z, yc, ym, yx, D)
    r = proj(mg, w_out, out_dtype=F32, name="out_proj")
    dy, dr, dg_post, loss = _post_loss(r, x, tgt, g_post)

    dgrad = functools.partial(_mm, mode="nt", out_dtype=BF16, tm=1024, tn=1024, tk=2048)
    wgrad = functools.partial(_mm, mode="tn", out_dtype=F32, tm=1024, tn=1024, tk=2048)
    dmg = dgrad(dr, w_out, name="out_proj_dx")
    dw_out = wgrad(mg, dr, name="out_proj_dw")
    dyc, dym, dyx, dz = _merge_bwd(dmg, z, yc, ym, yx, D)
    dpc = dgrad(dyc, w_co, name="conv_out_dx")
    dw_co = wgrad(pc, dyc, name="conv_out_dw")
    dpm = dgrad(dym, w_mo, name="ml_out_dx")
    dw_mo = wgrad(pm, dym, name="ml_out_dw")
    dpx = dgrad(dyx, w_xo, name="xa_out_dx")
    dw_xo = wgrad(px, dyx, name="xa_out_dw")

    dkv, dz = _xattn_bwd(dpx, z, kv, dz, D)
    dkv_b = dkv.astype(BF16)
    dw_kv = wgrad(memn, dkv_b, name="mem_kv_dw")
    dmemn = _mm(dkv_b, w_kv, mode="nt", out_dtype=F32, tm=256, tn=1024, tk=2048, name="mem_kv_dx")
    dg_mem = _gain_grad(dmemn, mem, rstd_mem)

    du1, cstats, dz = _conv_bwd1(dpc, z, u1, g_ln, b_ln, dz, D)
    dw_dw, dz = _conv_bwd2(du1, z, w_dw, dz, D)

    dhm, dg_head, dz = _ml_post_bwd(dpm, hm, z, g_head, dz, D)
    dqks, dgates, dz = _mlstm_bwd(qks, z, gcol, grow, bif_row, bif_col, hm, dhm, cst, nst, dz, D)
    dc, dw_qk = _qk_bwd1(dqks, z, w_qk, D)
    dz = _qk_bwd2(dc, w_qk, dz, D)

    g = dict(w_conv_out=dw_co, w_ml_out=dw_mo, w_xa_out=dw_xo, w_out=dw_out, w_mem_kv=dw_kv,
             w_qk_conv=dw_qk[:QK_W], w_dw=dw_dw[:CONV_W])
    dgates_b = dgates.astype(BF16)
    in_proj_dw = functools.partial(_mm, dz, h, mode="tn", out_dtype=BF16 if dist else F32, tm=1024, tn=2048, tk=2048,
                                   name="in_proj_dw")
    in_proj_dx = functools.partial(_in_proj_dx, dz, wpT, dgates_b, wifT, x, rstd, dy, g_pre)
    parts = {}
    if dist:
        send = [_to_slabs(n, g[n], BF16 if n in LATE[:5] else F32) for n in LATE]
        dwpT, recv = in_proj_dw(side=(send, True))
        parts.update(zip(LATE, recv))
        *_, height, _, start = _w_in_rows(D)
        dx, dg_pre, recv = in_proj_dx(side=([dwpT], _RowWindows(start, height)))
        parts['w_in'] = recv[0]
    else:
        dwpT = in_proj_dw()
        dx, dg_pre, _ = in_proj_dx()
    dwifT = _mm(dgates_b, h, mode="tn", out_dtype=F32, tm=IF_PAD, tn=2048, tk=2048, name="gates_dw")
    if not dist:
        g['w_in'] = jnp.concatenate([dwpT[:8 * D], dwifT[:N_IF], dwpT[8 * D:]], axis=0).T

    db_if = _col_sum(dgates)[0, :N_IF]
    g.update(g_pre=dg_pre[0], b_if=db_if, b_dw=cstats[2], g_ln=cstats[0], b_ln=cstats[1], g_ml_head=dg_head[0],
             g_mem=dg_mem[0], g_post=dg_post[0], w_in_gates=dwifT[:N_IF])
    return loss[0, 0], dx, g, parts


WEIGHTS = ('g_pre', 'w_in', 'b_if', 'w_qk_conv', 'w_dw', 'b_dw', 'g_ln', 'b_ln', 'w_conv_out', 'g_ml_head',
           'w_ml_out', 'g_mem', 'w_mem_kv', 'w_xa_out', 'w_out', 'g_post')
VECTORS = ('g_pre', 'b_dw', 'g_ln', 'b_ln', 'g_ml_head', 'g_mem', 'g_post')


def kernel(x, mem, g_pre, w_in, b_if, w_qk_conv, w_dw, b_dw, g_ln, b_ln, w_conv_out, g_ml_head, w_ml_out, g_mem, w_mem_kv, w_xa_out, w_out, g_post, loss_target, m_g_pre, m_w_in, m_b_if, m_w_qk_conv, m_w_dw, m_b_dw, m_g_ln, m_b_ln, m_w_conv_out, m_g_ml_head, m_w_ml_out, m_g_mem, m_w_mem_kv, m_w_xa_out, m_w_out, m_g_post, v_g_pre, v_w_in, v_b_if, v_w_qk_conv, v_w_dw, v_b_dw, v_g_ln, v_b_ln, v_w_conv_out, v_g_ml_head, v_w_ml_out, v_g_mem, v_w_mem_kv, v_w_xa_out, v_w_out, v_g_post):
    S, D = x.shape[1], x.shape[2]
    w = dict(g_pre=g_pre, w_in=w_in, b_if=b_if, w_qk_conv=w_qk_conv, w_dw=w_dw, b_dw=b_dw, g_ln=g_ln, b_ln=b_ln,
             w_conv_out=w_conv_out, g_ml_head=g_ml_head, w_ml_out=w_ml_out, g_mem=g_mem, w_mem_kv=w_mem_kv,
             w_xa_out=w_xa_out, w_out=w_out, g_post=g_post)
    mom = dict(g_pre=m_g_pre, w_in=m_w_in, b_if=m_b_if, w_qk_conv=m_w_qk_conv, w_dw=m_w_dw, b_dw=m_b_dw,
               g_ln=m_g_ln, b_ln=m_b_ln, w_conv_out=m_w_conv_out, g_ml_head=m_g_ml_head, w_ml_out=m_w_ml_out,
               g_mem=m_g_mem, w_mem_kv=m_w_mem_kv, w_xa_out=m_w_xa_out, w_out=m_w_out, g_post=m_g_post)
    var = dict(g_pre=v_g_pre, w_in=v_w_in, b_if=v_b_if, w_qk_conv=v_w_qk_conv, w_dw=v_w_dw, b_dw=v_b_dw,
               g_ln=v_g_ln, b_ln=v_b_ln, w_conv_out=v_w_conv_out, g_ml_head=v_g_ml_head, w_ml_out=v_w_ml_out,
               g_mem=v_g_mem, w_mem_kv=v_w_mem_kv, w_xa_out=v_w_xa_out, w_out=v_w_out, g_post=v_g_post)

    n_rows, gate_dev, gate_row, height, first, start = _w_in_rows(D)
    me = 4 * lax.axis_index("x") + 2 * lax.axis_index("y") + lax.axis_index("c")

    my_block = w_in.T.astype(BF16)
    my_window = _w_window(my_block, me, D)
    south = _core_gather(my_window, 0, "gather_w_in_south")
    early, rest, _ = _w_in_tiles(D)
    south_windows = {2 * q: south[q] for q in range(N_CHIP)}
    w_early = _w_rows_of_tiles(early, south_windows, D)
    my_gates = jnp.zeros((BF16_ROWS, D), BF16).at[:N_IF].set(my_block[gate_row:gate_row + N_IF])

    def rows_of_rest(north):
        windows = dict(south_windows)
        windows.update({2 * q + 1: north[q] for q in range(N_CHIP)})
        return _w_rows_of_tiles(rest, windows, D)

    late = {n: w[n].astype(BF16) if n in LATE[:5] else w[n] for n in LATE}
    row = lambda a: a.reshape(1, D)

    loss, dx, g, parts = _local_step(
        x[0], mem[0], loss_target[0], row(g_pre), w_early, (my_window, rows_of_rest), my_gates, b_if, row(b_dw),
        row(g_ln), row(b_ln), row(g_ml_head), row(g_mem), row(g_post), late, True)

    pad = lambda a: jnp.zeros((D,), F32).at[:N_IF].set(a)
    vec = jnp.concatenate([jnp.stack([g[n] for n in VECTORS] + [pad(g['b_if'])]), g['w_in_gates']])
    parts['vec'] = _exchange([vec], False, "gather_vector_grads")[0]

    out = {}
    for n in LATE:
        out[n] = _adamw(parts[n], w[n], mom[n], var[n], "adamw_" + n)
    zeros8 = jnp.zeros((N_IF, D), F32)
    stack = lambda d: jnp.concatenate([jnp.stack([d[n] for n in VECTORS] + [pad(d['b_if'])]), zeros8])
    vres = _adamw(parts['vec'], stack(w), stack(mom), stack(var), "adamw_vectors")
    for i, n in enumerate(VECTORS):
        out[n] = [r[i] for r in vres]
    out['b_if'] = [r[len(VECTORS), :N_IF] for r in vres]
    gate_grad = vres[0][len(VECTORS) + 1:]

    win = _sum_parts(parts['w_in'], "sum_w_in_grads")
    seg = lax.dynamic_slice(win, (first(me) - start(me), 0), (n_rows, D))
    with_gates = jnp.concatenate([seg[:gate_row], gate_grad, seg[gate_row:n_rows - N_IF]], axis=0)
    g_in = jnp.where(me == gate_dev, with_gates, seg)
    out['w_in'] = [r.T for r in _adamw(g_in[None], w_in.T, m_w_in.T, v_w_in.T, "adamw_w_in")]

    loss = lax.psum(loss, ("x", "y", "c"))
    res = [loss, dx.reshape(1, S, D)]
    for k in range(4):
        res += [out[n][k] for n in WEIGHTS]
    return tuple(res)
```

```python
import functools
import math

import jax
import jax.numpy as jnp
from jax import lax
from jax.experimental import pallas as pl
from jax.experimental.pallas import tpu as pltpu

F32 = jnp.float32
BF16 = jnp.bfloat16

N_DEV = 8
N_HEADS = 4
CONV_W = 31
QK_W = 4
N_IF = 2 * N_HEADS
IF_PAD = 128
EPS = 1e-6
NEG = -1e30

ADAM_LR = 0.001
ADAM_B1 = 0.9
ADAM_B2 = 0.999
ADAM_EPS = 1e-08
ADAM_WD = 0.01
ADAM_STEP = 10

TOK_TILE = 512
ML_CHUNK = 256
HALO = 32
BF16_ROWS = 16
VMEM_LIMIT = 56 * 1024 * 1024


def _cparams(sem=None):
    kw = dict(vmem_limit_bytes=VMEM_LIMIT)
    if sem is not None:
        kw["dimension_semantics"] = sem
    return pltpu.CompilerParams(**kw)


def _sig(x):
    return jax.nn.sigmoid(x)


def _silu(x):
    return x * _sig(x)


def _dsilu(x):
    s = _sig(x)
    return s * (1.0 + x * (1.0 - s))


def _logsig(x):
    return jnp.minimum(x, 0.0) - jnp.log(1.0 + jnp.exp(-jnp.abs(x)))


def _dot(a, b, dims):
    return lax.dot_general(a, b, (dims, ((), ())), preferred_element_type=F32)


def _dot_nn(a, b):
    return _dot(a, b, ((1,), (0,)))


def _dot_nt(a, b):
    return _dot(a, b, ((1,), (1,)))


def _dot_tn(a, b):
    return _dot(a, b, ((0,), (0,)))


def _split3(a):
    hi = a.astype(BF16)
    r1 = a - hi.astype(F32)
    mid = r1.astype(BF16)
    lo = (r1 - mid.astype(F32)).astype(BF16)
    return hi, mid, lo


def _tri_left(tri, a):
    hi, mid, lo = _split3(a)
    return _dot_nn(tri, hi) + _dot_nn(tri, mid) + _dot_nn(tri, lo)


def _tri_right(a, tri):
    hi, mid, lo = _split3(a)
    return _dot_nn(hi, tri) + _dot_nn(mid, tri) + _dot_nn(lo, tri)


def _fit(n, t):
    if n <= t:
        return n
    return max(c for c in range(128, t + 1, 128) if n % c == 0)


def _peer(k):
    x, y, c = lax.axis_index("x"), lax.axis_index("y"), lax.axis_index("c")
    px = 1 - x if (k >> 2) & 1 else x
    py = 1 - y if (k >> 1) & 1 else y
    pc = 1 - c if k & 1 else c
    return (px, py, pc), 4 * px + 2 * py + pc


class _RowWindows:
    def __init__(self, start, height):
        self.start, self.height = start, height


def _xchg_copies(srcs, dsts, send_sems, recv_sems, loc_sems, scatter):
    _, me = _peer(0)

    def piece(src, p):
        if isinstance(scatter, _RowWindows):
            return src.at[pl.ds(pl.multiple_of(scatter.start(p), BF16_ROWS), scatter.height), :]
        return src.at[p] if scatter else src

    copies = []
    for t, (src, dst) in enumerate(zip(srcs, dsts)):
        copies.append(pltpu.make_async_copy(piece(src, me), dst.at[me], loc_sems.at[t]))
        for k in range(1, N_DEV):
            dev, p = _peer(k)
            s = t * (N_DEV - 1) + k - 1
            copies.append(pltpu.make_async_remote_copy(
                src_ref=piece(src, p), dst_ref=dst.at[me],
                send_sem=send_sems.at[s], recv_sem=recv_sems.at[s],
                device_id=dev, device_id_type=pl.DeviceIdType.MESH))
    return copies


def _xchg_out_shapes(operands, scatter):
    def one(a):
        if isinstance(scatter, _RowWindows):
            return (scatter.height, a.shape[1])
        return tuple(a.shape[1:] if scatter else a.shape)
    return [jax.ShapeDtypeStruct((N_DEV,) + one(a), a.dtype) for a in operands]


def _xchg_sems(n):
    return [pltpu.SemaphoreType.DMA((n * (N_DEV - 1),)), pltpu.SemaphoreType.DMA((n * (N_DEV - 1),)),
            pltpu.SemaphoreType.DMA((n,))]


def _mm(a, b, *, mode, out_dtype, tm, tn, tk, name, n_outer=False, acc_in=None, side=None):
    if mode == "nn":
        (M, K), (K2, N) = a.shape, b.shape
    elif mode == "nt":
        (M, K), (N, K2) = a.shape, b.shape
    else:
        (K, M), (K2, N) = a.shape, b.shape
    assert K == K2, (a.shape, b.shape, mode)
    tm, tn, tk = _fit(M, tm), _fit(N, tn), _fit(K, tk)
    assert M % tm == 0 and N % tn == 0 and K % tk == 0, (a.shape, b.shape, tm, tn, tk)
    nk = K // tk
    if n_outer:
        grid = (N // tn, M // tm, nk)
        ij = lambda g0, g1: (g1, g0)
    else:
        grid = (M // tm, N // tn, nk)
        ij = lambda g0, g1: (g0, g1)

    if mode == "nn":
        a_spec = pl.BlockSpec((tm, tk), lambda g0, g1, k: (ij(g0, g1)[0], k))
        b_spec = pl.BlockSpec((tk, tn), lambda g0, g1, k: (k, ij(g0, g1)[1]))
        dot = _dot_nn
    elif mode == "nt":
        a_spec = pl.BlockSpec((tm, tk), lambda g0, g1, k: (ij(g0, g1)[0], k))
        b_spec = pl.BlockSpec((tn, tk), lambda g0, g1, k: (ij(g0, g1)[1], k))
        dot = _dot_nt
    else:
        a_spec = pl.BlockSpec((tk, tm), lambda g0, g1, k: (k, ij(g0, g1)[0]))
        b_spec = pl.BlockSpec((tk, tn), lambda g0, g1, k: (k, ij(g0, g1)[1]))
        dot = _dot_tn
    o_spec = pl.BlockSpec((tm, tn), lambda g0, g1, k: ij(g0, g1))
    has_acc = acc_in is not None
    n_side = len(side[0]) if side is not None else 0
    scatter = side[1] if side is not None else False
    n_in = 2 + int(has_acc)

    def body(*refs):
        a_ref, b_ref = refs[0], refs[1]
        c_ref = refs[2] if has_acc else None
        srcs = refs[n_in:n_in + n_side]
        o_ref = refs[n_in + n_side]
        dsts = refs[n_in + n_side + 1:n_in + 2 * n_side + 1]
        scratch = refs[n_in + 2 * n_side + 1:]
        k = pl.program_id(2)
        if n_side:
            sems = scratch[-3:]
            first = (pl.program_id(0) == 0) & (pl.program_id(1) == 0) & (k == 0)
            last = (pl.program_id(0) == grid[0] - 1) & (pl.program_id(1) == grid[1] - 1) & (k == nk - 1)

            @pl.when(first)
            def _():
                for cp in _xchg_copies(srcs, dsts, *sems, scatter):
                    cp.start()

        p = dot(a_ref[...], b_ref[...])

        def finish(r):
            if has_acc:
                r = r + c_ref[...].astype(F32)
            o_ref[...] = r.astype(out_dtype)

        if nk == 1:
            finish(p)
        else:
            acc = scratch[0]

            @pl.when(k == 0)
            def _():
                acc[...] = p

            @pl.when((k > 0) & (k < nk - 1))
            def _():
                acc[...] += p

            @pl.when(k == nk - 1)
            def _():
                finish(acc[...] + p)

        if n_side:
            @pl.when(last)
            def _():
                for cp in _xchg_copies(srcs, dsts, *sems, scatter):
                    cp.wait()

    hbm = pl.BlockSpec(memory_space=pltpu.HBM)
    in_specs = [a_spec, b_spec]
    args = [a, b]
    if has_acc:
        in_specs.append(o_spec)
        args.append(acc_in)
    out_specs = [o_spec]
    out_shape = [jax.ShapeDtypeStruct((M, N), out_dtype)]
    scratch_shapes = [pltpu.VMEM((tm, tn), F32)] if nk > 1 else []
    if n_side:
        in_specs += [hbm] * n_side
        args += list(side[0])
        out_specs += [hbm] * n_side
        out_shape += _xchg_out_shapes(side[0], scatter)
        scratch_shapes += _xchg_sems(n_side)
        cp = pltpu.CompilerParams(vmem_limit_bytes=VMEM_LIMIT, has_side_effects=True,
                                  dimension_semantics=("arbitrary", "arbitrary", "arbitrary"))
    else:
        cp = _cparams(("parallel", "parallel", "arbitrary"))
    res = pl.pallas_call(
        body, name=name, grid=grid, in_specs=in_specs, out_specs=out_specs, out_shape=out_shape,
        scratch_shapes=scratch_shapes, compiler_params=cp,
    )(*args)
    return (res[0], list(res[1:])) if n_side else res[0]


def _rmsnorm_fwd(x, g, name):
    S, D = x.shape
    T = min(TOK_TILE, S)

    def body(x_ref, g_ref, h_ref, r_ref):
        xv = x_ref[...]
        r = lax.rsqrt(jnp.mean(xv * xv, axis=-1, keepdims=True) + EPS)
        h_ref[...] = (xv * r * g_ref[...]).astype(BF16)
        r_ref[...] = r

    return pl.pallas_call(
        body, name=name, grid=(S // T,),
        in_specs=[pl.BlockSpec((T, D), lambda i: (i, 0)), pl.BlockSpec((1, D), lambda i: (0, 0))],
        out_specs=[pl.BlockSpec((T, D), lambda i: (i, 0)), pl.BlockSpec((T, 1), lambda i: (i, 0))],
        out_shape=[jax.ShapeDtypeStruct((S, D), BF16), jax.ShapeDtypeStruct((S, 1), F32)],
        compiler_params=_cparams(("parallel",)),
    )(x, g)


LANE = 128
ROW_BLK = 64


def _shift_strip(src, sh, cs, nr, residues=range(1, 8)):
    for r in residues:
        sh[r] = src[pl.ds(r, nr), cs]


def _tap(src, sh, cs, o, r0):
    r = o % 8
    if r == 0:
        return src[pl.ds(o + r0, ROW_BLK), cs]
    return sh[r, pl.ds(o - r + r0, ROW_BLK), :]


def _conv_fwd(z, w_dw, b_dw, g_ln, b_ln, D):
    S = z.shape[0]
    T = min(TOK_TILE, S)
    hb = T // HALO

    def body(a_ref, b_ref, zc_ref, ah_ref, bh_ref, w_ref, bdw_ref, gln_ref, bln_ref, u1_ref, pc_ref, u0s, sh):
        i = pl.program_id(0)
        a = a_ref[...].astype(F32)
        b = b_ref[...].astype(F32)
        u0s[pl.ds(HALO, T), :] = a * _sig(b)
        halo = ah_ref[...].astype(F32) * _sig(bh_ref[...].astype(F32))
        u0s[pl.ds(0, HALO), :] = jnp.where(i > 0, halo, 0.0)
        u0s[pl.ds(T + HALO, 8), :] = jnp.zeros((8, D), F32)
        for c0 in range(0, D, LANE):
            cs = pl.ds(c0, LANE)
            _shift_strip(u0s, sh, cs, T + HALO)
            for r0 in range(0, T, ROW_BLK):
                acc = jnp.broadcast_to(bdw_ref[:, cs], (ROW_BLK, LANE))
                for j in range(CONV_W):
                    acc = acc + w_ref[pl.ds(j, 1), cs] * _tap(u0s, sh, cs, HALO - (CONV_W - 1) + j, r0)
                u1_ref[pl.ds(r0, ROW_BLK), cs] = acc
        acc = u1_ref[...]
        mu = jnp.mean(acc, axis=-1, keepdims=True)
        xc = acc - mu
        var = jnp.mean(xc * xc, axis=-1, keepdims=True)
        ln = xc * lax.rsqrt(var + EPS) * gln_ref[...] + bln_ref[...]
        pc_ref[...] = (_silu(ln) * _silu(zc_ref[...].astype(F32))).astype(BF16)

    prev = lambda i: (jnp.maximum(i * hb - 1, 0), 0)
    prev_b = lambda i: (jnp.maximum(i * hb - 1, 0), D // D)
    vec = pl.BlockSpec((1, D), lambda i: (0, 0))
    return pl.pallas_call(
        body, name="conv_fwd", grid=(S // T,),
        in_specs=[pl.BlockSpec((T, D), lambda i: (i, 0)), pl.BlockSpec((T, D), lambda i: (i, 1)),
                  pl.BlockSpec((T, D), lambda i: (i, 2)),
                  pl.BlockSpec((HALO, D), prev), pl.BlockSpec((HALO, D), prev_b),
                  pl.BlockSpec((32, D), lambda i: (0, 0)), vec, vec, vec],
        out_specs=[pl.BlockSpec((T, D), lambda i: (i, 0)), pl.BlockSpec((T, D), lambda i: (i, 0))],
        out_shape=[jax.ShapeDtypeStruct((S, D), F32), jax.ShapeDtypeStruct((S, D), BF16)],
        scratch_shapes=[pltpu.VMEM((T + HALO + 8, D), F32), pltpu.VMEM((8, T + HALO, LANE), F32)],
        compiler_params=_cparams(("parallel",)),
    )(z, z, z, z, z, w_dw, b_dw, g_ln, b_ln)


_QK_RES = sorted({(HALO - (QK_W - 1) + j) % 8 for j in range(QK_W)} - {0})


def _qk_fill(ps, p0_ref, p1_ref, h0_ref, h1_ref, T, D):
    i = pl.program_id(0)
    ps[pl.ds(HALO, T), pl.ds(0, D)] = p0_ref[...].astype(F32)
    ps[pl.ds(HALO, T), pl.ds(D, D)] = p1_ref[...].astype(F32)
    ps[pl.ds(0, HALO), pl.ds(0, D)] = jnp.where(i > 0, h0_ref[...].astype(F32), 0.0)
    ps[pl.ds(0, HALO), pl.ds(D, D)] = jnp.where(i > 0, h1_ref[...].astype(F32), 0.0)
    ps[pl.ds(T + HALO, 8), :] = jnp.zeros((8, 2 * D), F32)


def _qk_conv(ps, sh, w_ref, cs, r0):
    acc = jnp.zeros((ROW_BLK, LANE), F32)
    for j in range(QK_W):
        acc = acc + w_ref[pl.ds(j, 1), cs] * _tap(ps, sh, cs, HALO - (QK_W - 1) + j, r0)
    return acc


def _qk_fwd(z, w_qk, D):
    S = z.shape[0]
    T = min(TOK_TILE, S)
    hb = T // HALO
    scale = (D // N_HEADS) ** -0.5
    W2 = 2 * D

    def body(p0_ref, p1_ref, h0_ref, h1_ref, w_ref, o_ref, ps, sh):
        _qk_fill(ps, p0_ref, p1_ref, h0_ref, h1_ref, T, D)
        for c0 in range(0, W2, LANE):
            cs = pl.ds(c0, LANE)
            _shift_strip(ps, sh, cs, T + HALO, _QK_RES)
            for r0 in range(0, T, ROW_BLK):
                qk = _silu(_qk_conv(ps, sh, w_ref, cs, r0))
                o_ref[pl.ds(r0, ROW_BLK), cs] = (qk * scale if c0 >= D else qk).astype(BF16)

    prev0 = lambda i: (jnp.maximum(i * hb - 1, 0), 3)
    prev1 = lambda i: (jnp.maximum(i * hb - 1, 0), 4)
    return pl.pallas_call(
        body, name="qk_fwd", grid=(S // T,),
        in_specs=[pl.BlockSpec((T, D), lambda i: (i, 3)), pl.BlockSpec((T, D), lambda i: (i, 4)),
                  pl.BlockSpec((HALO, D), prev0), pl.BlockSpec((HALO, D), prev1),
                  pl.BlockSpec((8, W2), lambda i: (0, 0))],
        out_specs=pl.BlockSpec((T, W2), lambda i: (i, 0)),
        out_shape=jax.ShapeDtypeStruct((S, W2), BF16),
        scratch_shapes=[pltpu.VMEM((T + HALO + 8, W2), F32), pltpu.VMEM((8, T + HALO, LANE), F32)],
        compiler_params=_cparams(("parallel",)),
    )(z, z, z, z, w_qk)


def _ml_gates(gc_ref, gr_ref, bifr_ref, bifc_ref, L):
    gc = gc_ref[...] + bifr_ref[...]
    gr = gr_ref[...] + bifc_ref[...]
    ii = lax.broadcasted_iota(jnp.int32, (L, L), 0)
    jj = lax.broadcasted_iota(jnp.int32, (L, L), 1)
    tri = (jj <= ii).astype(BF16)
    triu = (ii <= jj).astype(BF16)
    bcol_all = _tri_left(tri, _logsig(gc))
    brow_all = _tri_right(_logsig(gr), triu)
    return gc, gr, bcol_all, brow_all, ii, jj


def _ml_intra(q, k, b_c, b_r, li_r, m, n_row, ii, jj):
    d = b_c - b_r + li_r
    dm = jnp.where(jj <= ii, d, NEG)
    inter = b_c + m
    m_row = jnp.maximum(inter, jnp.max(dm, axis=1, keepdims=True))
    w_intra = jnp.exp(dm - m_row)
    w_inter = jnp.exp(inter - m_row)
    qk = _dot_nt(q, k)
    s = qk * w_intra
    qn = jnp.sum(q.astype(F32) * n_row, axis=-1, keepdims=True)
    den = jnp.sum(s, axis=-1, keepdims=True) + w_inter * qn
    hdiv = jnp.maximum(jnp.abs(den), jnp.exp(-m_row))
    return qk, w_intra, w_inter, s, qn, den, hdiv, m_row


def _ml_state(b_c, li_c, m, L):
    b_last = b_c[L - 1:L, :]
    g_c = b_last - b_c + li_c
    m_new = jnp.maximum(b_last + m, jnp.max(g_c, axis=0, keepdims=True))
    decay = jnp.exp(b_last + m - m_new)
    wk = jnp.exp(g_c - m_new)
    return m_new, decay, wk


def _mlstm_fwd(qks, z, gcol, grow, bif_row, bif_col, D):
    S = qks.shape[0]
    L = min(ML_CHUNK, S)
    NC = S // L
    H = N_HEADS
    dh = D // H

    def body(q_ref, k_ref, v_ref, gc_ref, gr_ref, bifr_ref, bifc_ref, hm_ref, cst_ref, nst_ref, C, NM):
        c = pl.program_id(0)

        @pl.when(c == 0)
        def _():
            C[...] = jnp.zeros_like(C)
            NM[...] = jnp.zeros_like(NM)

        gc, gr, bcol_all, brow_all, ii, jj = _ml_gates(gc_ref, gr_ref.at[0], bifr_ref, bifc_ref, L)
        for h in range(H):
            hs = pl.ds(h * dh, dh)
            q = q_ref[:, hs]
            k = k_ref[:, hs]
            v = v_ref[:, hs]
            b_c = bcol_all[:, H + h:H + h + 1]
            li_c = gc[:, h:h + 1]
            b_r = brow_all[H + h:H + h + 1, :]
            li_r = gr[h:h + 1, :]
            n_row = NM[h, 0:1, :]
            m = NM[h, 1:2, 0:1]
            Cb = C[h].astype(BF16)
            cst_ref[0, h] = Cb
            nst_ref[0, h] = NM[h]
            qk, w_intra, w_inter, s, qn, den, hdiv, m_row = _ml_intra(q, k, b_c, b_r, li_r, m, n_row, ii, jj)
            num = _dot_nn(s.astype(BF16), v) + w_inter * _dot_nn(q, Cb)
            hm_ref[:, hs] = num / hdiv
            m_new, decay, wk = _ml_state(b_c, li_c, m, L)
            wkk = wk * k.astype(F32)
            C[h] = decay * C[h] + _dot_tn(wkk.astype(BF16), v)
            NM[h, 0:1, :] = decay * n_row + jnp.sum(wkk, axis=0, keepdims=True)
            NM[h, 1:2, :] = jnp.broadcast_to(m_new, (1, dh))

    return pl.pallas_call(
        body, name="mlstm_fwd", grid=(NC,),
        in_specs=[pl.BlockSpec((L, D), lambda c: (c, 0)), pl.BlockSpec((L, D), lambda c: (c, 1)),
                  pl.BlockSpec((L, D), lambda c: (c, 5)),
                  pl.BlockSpec((L, IF_PAD), lambda c: (c, 0)),
                  pl.BlockSpec((1, 16, L), lambda c: (c, 0, 0)),
                  pl.BlockSpec((1, IF_PAD), lambda c: (0, 0)), pl.BlockSpec((16, 1), lambda c: (0, 0))],
        out_specs=[pl.BlockSpec((L, D), lambda c: (c, 0)),
                   pl.BlockSpec((1, H, dh, dh), lambda c: (c, 0, 0, 0)),
                   pl.BlockSpec((1, H, 8, dh), lambda c: (c, 0, 0, 0))],
        out_shape=[jax.ShapeDtypeStruct((S, D), F32), jax.ShapeDtypeStruct((NC, H, dh, dh), BF16),
                   jax.ShapeDtypeStruct((NC, H, 8, dh), F32)],
        scratch_shapes=[pltpu.VMEM((H, dh, dh), F32), pltpu.VMEM((H, 8, dh), F32)],
        compiler_params=_cparams(("arbitrary",)),
    )(qks, qks, z, gcol, grow, bif_row, bif_col)


def _mlstm_bwd(qks, z, gcol, grow, bif_row, bif_col, hm, dhm, cst, nst, dz, D):
    S = qks.shape[0]
    L = min(ML_CHUNK, S)
    NC = S // L
    H = N_HEADS
    dh = D // H

    def body(q_ref, k_ref, v_ref, gc_ref, gr_ref, bifr_ref, bifc_ref, hm_ref, dhm_ref, cst_ref, nst_ref, dz_any,
             dqk_ref, dg_ref, dv_ref, dC, dN):
        del dz_any
        c = pl.program_id(0)

        @pl.when(c == 0)
        def _():
            dC[...] = jnp.zeros_like(dC)
            dN[...] = jnp.zeros_like(dN)

        gc, gr, bcol_all, brow_all, ii, jj = _ml_gates(gc_ref, gr_ref.at[0], bifr_ref, bifc_ref, L)
        eye = ii == jj
        lane = lax.broadcasted_iota(jnp.int32, (L, IF_PAD), 1)
        db_all = jnp.zeros((L, IF_PAD), F32)
        dli_all = jnp.zeros((L, IF_PAD), F32)
        last_row = lax.broadcasted_iota(jnp.int32, (L, 1), 0) == L - 1
        for h in range(H):
            hs = pl.ds(h * dh, dh)
            q = q_ref[:, hs]
            k = k_ref[:, hs]
            v = v_ref[:, hs]
            qf = q.astype(F32)
            kf = k.astype(F32)
            b_c = bcol_all[:, H + h:H + h + 1]
            li_c = gc[:, h:h + 1]
            b_r = brow_all[H + h:H + h + 1, :]
            li_r = gr[h:h + 1, :]
            n_row = nst_ref[0, h, 0:1, :]
            m = nst_ref[0, h, 1:2, 0:1]
            Cb = cst_ref[0, h]
            qk, w_intra, w_inter, s, qn, den, hdiv, m_row = _ml_intra(q, k, b_c, b_r, li_r, m, n_row, ii, jj)
            dh_ = dhm_ref[:, hs]
            hh = hm_ref[:, hs]
            dnum = dh_ / hdiv
            dhdiv = -jnp.sum(dh_ * hh, axis=-1, keepdims=True) / hdiv
            dden = jnp.where(jnp.abs(den) > jnp.exp(-m_row), dhdiv * jnp.sign(den), 0.0)
            dnum_b = dnum.astype(BF16)
            ds = _dot_nt(dnum_b, v) + dden
            s_b = s.astype(BF16)
            dv = _dot_tn(s_b, dnum_b)
            dnC = _dot_nt(dnum_b, Cb)
            dwi = dden * w_inter
            dw_inter = jnp.sum(qf * dnC, axis=-1, keepdims=True) + dden * qn
            dq = w_inter * dnC + dwi * n_row
            dC_out = _dot_tn((w_inter * qf).astype(BF16), dnum_b)
            dn_out = jnp.sum(dwi * qf, axis=0, keepdims=True)
            dqk = (ds * w_intra).astype(BF16)
            dq = dq + _dot_nn(dqk, k)
            dk = _dot_tn(dqk, q)
            dd = ds * s
            rs_c = jnp.sum(dd, axis=1, keepdims=True)
            cs_r = jnp.sum(dd, axis=0, keepdims=True)
            cs_c = jnp.sum(jnp.where(eye, cs_r, 0.0), axis=1, keepdims=True)
            dinter = dw_inter * w_inter
            m_new, decay, wk = _ml_state(b_c, li_c, m, L)
            dCn = dC[h]
            dCn_b = dCn.astype(BF16)
            dn_new = dN[h, 0:1, :]
            ddecay = (jnp.sum(jnp.sum(dCn * Cb.astype(F32), axis=1, keepdims=True), axis=0, keepdims=True)
                      + jnp.sum(dn_new * n_row, axis=1, keepdims=True))
            dwkk = _dot_nt(v, dCn_b) + dn_new
            wkk_b = (wk * kf).astype(BF16)
            dv = dv + _dot_nn(wkk_b, dCn_b)
            dk = dk + wk * dwkk
            dg = jnp.sum(dwkk * kf, axis=-1, keepdims=True) * wk
            db_last = ddecay * decay + jnp.sum(dg, axis=0, keepdims=True)
            dC[h] = decay * dCn + dC_out
            dN[h, 0:1, :] = decay * dn_new + dn_out
            db_c = rs_c - cs_c + dinter - dg + jnp.where(last_row, db_last, 0.0)
            dli_c = cs_c + dg
            db_all = jnp.where(lane == H + h, db_c, db_all)
            dli_all = jnp.where(lane == h, dli_c, dli_all)
            dqk_ref[:, hs] = dq.astype(BF16)
            dqk_ref[:, pl.ds(D + h * dh, dh)] = dk.astype(BF16)
            dv_ref[:, hs] = dv.astype(BF16)
        triu = (ii <= jj).astype(BF16)
        dlf = _tri_left(triu, db_all)
        dg_ref[...] = dli_all + dlf * (1.0 - _sig(gc))

    r = lambda c: NC - 1 - c
    n_in = 12
    return pl.pallas_call(
        body, name="mlstm_bwd", grid=(NC,),
        in_specs=[pl.BlockSpec((L, D), lambda c: (r(c), 0)), pl.BlockSpec((L, D), lambda c: (r(c), 1)),
                  pl.BlockSpec((L, D), lambda c: (r(c), 5)),
                  pl.BlockSpec((L, IF_PAD), lambda c: (r(c), 0)),
                  pl.BlockSpec((1, 16, L), lambda c: (r(c), 0, 0)),
                  pl.BlockSpec((1, IF_PAD), lambda c: (0, 0)), pl.BlockSpec((16, 1), lambda c: (0, 0)),
                  pl.BlockSpec((L, D), lambda c: (r(c), 0)), pl.BlockSpec((L, D), lambda c: (r(c), 0)),
                  pl.BlockSpec((1, H, dh, dh), lambda c: (r(c), 0, 0, 0)),
                  pl.BlockSpec((1, H, 8, dh), lambda c: (r(c), 0, 0, 0)),
                  pl.BlockSpec(memory_space=pl.ANY)],
        out_specs=[pl.BlockSpec((L, 2 * D), lambda c: (r(c), 0)),
                   pl.BlockSpec((L, IF_PAD), lambda c: (r(c), 0)),
                   pl.BlockSpec((L, D), lambda c: (r(c), 5))],
        out_shape=[jax.ShapeDtypeStruct((S, 2 * D), BF16),
                   jax.ShapeDtypeStruct((S, IF_PAD), F32), jax.ShapeDtypeStruct(dz.shape, dz.dtype)],
        input_output_aliases={n_in - 1: 2},
        scratch_shapes=[pltpu.VMEM((H, dh, dh), F32), pltpu.VMEM((H, 8, dh), F32)],
        compiler_params=_cparams(("arbitrary",)),
    )(qks, qks, z, gcol, grow, bif_row, bif_col, hm, dhm, cst, nst, dz)


def _ml_post_fwd(hm, z, g_head, D):
    S = hm.shape[0]
    T = min(TOK_TILE, S)
    dh = D // N_HEADS

    def body(hm_ref, o_ref, zm_ref, g_ref, pm_ref):
        for h in range(N_HEADS):
            hs = pl.ds(h * dh, dh)
            hg = _sig(o_ref[:, hs].astype(F32)) * hm_ref[:, hs]
            rs = lax.rsqrt(jnp.mean(hg * hg, axis=-1, keepdims=True) + EPS)
            pm_ref[:, hs] = (hg * rs * g_ref[:, hs] * _silu(zm_ref[:, hs].astype(F32))).astype(BF16)

    return pl.pallas_call(
        body, name="ml_post_fwd", grid=(S // T,),
        in_specs=[pl.BlockSpec((T, D), lambda i: (i, 0)), pl.BlockSpec((T, D), lambda i: (i, 6)),
                  pl.BlockSpec((T, D), lambda i: (i, 7)), pl.BlockSpec((1, D), lambda i: (0, 0))],
        out_specs=pl.BlockSpec((T, D), lambda i: (i, 0)),
        out_shape=jax.ShapeDtypeStruct((S, D), BF16),
        compiler_params=_cparams(("parallel",)),
    )(hm, z, z, g_head)


def _ml_post_bwd(dpm, hm, z, g_head, dz, D):
    S = hm.shape[0]
    T = min(TOK_TILE, S)
    dh = D // N_HEADS

    def body(dpm_ref, hm_ref, o_ref, zm_ref, g_ref, dz_any, dhm_ref, dg_ref, dozm_ref):
        del dz_any
        do_ref = dozm_ref.at[:, pl.ds(0, D)]
        dzm_ref = dozm_ref.at[:, pl.ds(D, D)]
        i = pl.program_id(0)

        @pl.when(i == 0)
        def _():
            dg_ref[...] = jnp.zeros_like(dg_ref)

        for h in range(N_HEADS):
            hs = pl.ds(h * dh, dh)
            o = o_ref[:, hs].astype(F32)
            zm = zm_ref[:, hs].astype(F32)
            hmv = hm_ref[:, hs]
            dpm_ = dpm_ref[:, hs].astype(F32)
            g = g_ref[:, hs]
            so = _sig(o)
            hg = so * hmv
            rs = lax.rsqrt(jnp.mean(hg * hg, axis=-1, keepdims=True) + EPS)
            xn = hg * rs
            dzm_ref[:, hs] = (dpm_ * xn * g * _dsilu(zm)).astype(BF16)
            dhn = dpm_ * _silu(zm)
            dg_ref[0:1, hs] += jnp.sum(dhn * xn, axis=0, keepdims=True)
            dxn = dhn * g
            dhg = rs * (dxn - xn * jnp.mean(dxn * xn, axis=-1, keepdims=True))
            do_ref[:, hs] = (dhg * hmv * so * (1.0 - so)).astype(BF16)
            dhm_ref[:, hs] = dhg * so

    return pl.pallas_call(
        body, name="ml_post_bwd", grid=(S // T,),
        in_specs=[pl.BlockSpec((T, D), lambda i: (i, 0)), pl.BlockSpec((T, D), lambda i: (i, 0)),
                  pl.BlockSpec((T, D), lambda i: (i, 6)), pl.BlockSpec((T, D), lambda i: (i, 7)),
                  pl.BlockSpec((1, D), lambda i: (0, 0)), pl.BlockSpec(memory_space=pl.ANY)],
        out_specs=[pl.BlockSpec((T, D), lambda i: (i, 0)), pl.BlockSpec((8, D), lambda i: (0, 0)),
                   pl.BlockSpec((T, 2 * D), lambda i: (i, 3))],
        out_shape=[jax.ShapeDtypeStruct((S, D), F32), jax.ShapeDtypeStruct((8, D), F32),
                   jax.ShapeDtypeStruct(dz.shape, dz.dtype)],
        input_output_aliases={5: 2},
        compiler_params=_cparams(("arbitrary",)),
    )(dpm, hm, z, z, g_head, dz)


def _xattn_probs(q, km_h, scale):
    s = _dot_nt(q, km_h) * scale
    e = jnp.exp(s - jnp.max(s, axis=-1, keepdims=True))
    return e / jnp.sum(e, axis=-1, keepdims=True)


def _xattn_fwd(z, kv, D):
    S = z.shape[0]
    M = kv.shape[0]
    T = min(TOK_TILE, S)
    dh = D // N_HEADS
    scale = dh ** -0.5

    def body(q_ref, zx_ref, km_ref, vm_ref, px_ref):
        for h in range(N_HEADS):
            hs = pl.ds(h * dh, dh)
            p = _xattn_probs(q_ref[:, hs], km_ref[:, hs], scale)
            o = _dot_nn(p.astype(BF16), vm_ref[:, hs])
            px_ref[:, hs] = (o * _silu(zx_ref[:, hs].astype(F32))).astype(BF16)

    return pl.pallas_call(
        body, name="xattn_fwd", grid=(S // T,),
        in_specs=[pl.BlockSpec((T, D), lambda i: (i, 8)), pl.BlockSpec((T, D), lambda i: (i, 9)),
                  pl.BlockSpec((M, D), lambda i: (0, 0)), pl.BlockSpec((M, D), lambda i: (0, 1))],
        out_specs=pl.BlockSpec((T, D), lambda i: (i, 0)),
        out_shape=jax.ShapeDtypeStruct((S, D), BF16),
        compiler_params=_cparams(("parallel",)),
    )(z, z, kv, kv)


def _xattn_bwd(dpx, z, kv, dz, D):
    S = z.shape[0]
    M = kv.shape[0]
    T = min(TOK_TILE, S)
    dh = D // N_HEADS
    scale = dh ** -0.5

    def body(dpx_ref, q_ref, zx_ref, km_ref, vm_ref, dz_any, dkv_ref, dqz_ref):
        del dz_any
        i = pl.program_id(0)

        @pl.when(i == 0)
        def _():
            dkv_ref[...] = jnp.zeros_like(dkv_ref)

        for h in range(N_HEADS):
            hs = pl.ds(h * dh, dh)
            q = q_ref[:, hs]
            zx = zx_ref[:, hs].astype(F32)
            dpx_ = dpx_ref[:, hs].astype(F32)
            p = _xattn_probs(q, km_ref[:, hs], scale)
            p_b = p.astype(BF16)
            o = _dot_nn(p_b, vm_ref[:, hs])
            dqz_ref[:, pl.ds(D + h * dh, dh)] = (dpx_ * o * _dsilu(zx)).astype(BF16)
            do = (dpx_ * _silu(zx)).astype(BF16)
            dp = _dot_nt(do, vm_ref[:, hs])
            dsc = (p * (dp - jnp.sum(p * dp, axis=-1, keepdims=True)) * scale).astype(BF16)
            dqz_ref[:, hs] = _dot_nn(dsc, km_ref[:, hs]).astype(BF16)
            dkv_ref[:, hs] += _dot_tn(dsc, q)
            dkv_ref[:, pl.ds(D + h * dh, dh)] += _dot_tn(p_b, do)

    return pl.pallas_call(
        body, name="xattn_bwd", grid=(S // T,),
        in_specs=[pl.BlockSpec((T, D), lambda i: (i, 0)),
                  pl.BlockSpec((T, D), lambda i: (i, 8)), pl.BlockSpec((T, D), lambda i: (i, 9)),
                  pl.BlockSpec((M, D), lambda i: (0, 0)), pl.BlockSpec((M, D), lambda i: (0, 1)),
                  pl.BlockSpec(memory_space=pl.ANY)],
        out_specs=[pl.BlockSpec((M, 2 * D), lambda i: (0, 0)), pl.BlockSpec((T, 2 * D), lambda i: (i, 4))],
        out_shape=[jax.ShapeDtypeStruct((M, 2 * D), F32), jax.ShapeDtypeStruct(dz.shape, dz.dtype)],
        input_output_aliases={5: 1},
        compiler_params=_cparams(("arbitrary",)),
    )(dpx, z, z, kv, kv, dz)


def _col_sum(a):
    S, C = a.shape
    T = min(1024, S)

    def body(a_ref, o_ref):
        @pl.when(pl.program_id(0) == 0)
        def _():
            o_ref[...] = jnp.zeros_like(o_ref)

        o_ref[0:1, :] += jnp.sum(a_ref[...], axis=0, keepdims=True)

    return pl.pallas_call(
        body, name="col_sum", grid=(S // T,),
        in_specs=[pl.BlockSpec((T, C), lambda i: (i, 0))], out_specs=pl.BlockSpec((8, C), lambda i: (0, 0)),
        out_shape=jax.ShapeDtypeStruct((8, C), F32), compiler_params=_cparams(("arbitrary",)),
    )(a)


def _gain_grad(dn, xin, rstd):
    R, D = dn.shape

    def body(dn_ref, x_ref, r_ref, o_ref):
        o_ref[...] = jnp.zeros_like(o_ref)
        o_ref[0:1, :] = jnp.sum(dn_ref[...] * x_ref[...] * r_ref[...], axis=0, keepdims=True)

    return pl.pallas_call(
        body, name="gain_grad", out_shape=jax.ShapeDtypeStruct((8, D), F32),
        compiler_params=_cparams(),
    )(dn, xin, rstd)


def _merge_fwd(z, yc, ym, yx, D):
    S = z.shape[0]
    T = min(TOK_TILE, S)

    def body(g0, g1, g2, yc_ref, ym_ref, yx_ref, o_ref):
        o_ref[...] = (_sig(g0[...].astype(F32)) * yc_ref[...].astype(F32)
                      + _sig(g1[...].astype(F32)) * ym_ref[...].astype(F32)
                      + _sig(g2[...].astype(F32)) * yx_ref[...].astype(F32)).astype(BF16)

    tok = pl.BlockSpec((T, D), lambda i: (i, 0))
    return pl.pallas_call(
        body, name="merge_fwd", grid=(S // T,),
        in_specs=[pl.BlockSpec((T, D), lambda i: (i, 10)), pl.BlockSpec((T, D), lambda i: (i, 11)),
                  pl.BlockSpec((T, D), lambda i: (i, 12)), tok, tok, tok],
        out_specs=tok, out_shape=jax.ShapeDtypeStruct((S, D), BF16),
        compiler_params=_cparams(("parallel",)),
    )(z, z, z, yc, ym, yx)


def _merge_bwd(dmg, z, yc, ym, yx, D):
    S = z.shape[0]
    T = min(TOK_TILE // 2, S)

    def body(dm_ref, g_ref, yc_ref, ym_ref, yx_ref, dyc_ref, dym_ref, dyx_ref, dg_ref):
        j = pl.program_id(1)
        for jj, (y_ref, dy_ref) in enumerate(((yc_ref, dyc_ref), (ym_ref, dym_ref), (yx_ref, dyx_ref))):
            @pl.when(j == jj)
            def _():
                sg = _sig(g_ref[...].astype(F32))
                dm = dm_ref[...].astype(F32)
                dy_ref[...] = (dm * sg).astype(BF16)
                dg_ref[...] = (dm * y_ref[...].astype(F32) * sg * (1.0 - sg)).astype(BF16)

    tok = pl.BlockSpec((T, D), lambda i, j: (i, 0))
    return pl.pallas_call(
        body, name="merge_bwd", grid=(S // T, 3),
        in_specs=[tok, pl.BlockSpec((T, D), lambda i, j: (i, 10 + j)), tok, tok, tok],
        out_specs=[tok, tok, tok, pl.BlockSpec((T, D), lambda i, j: (i, 10 + j))],
        out_shape=[jax.ShapeDtypeStruct((S, D), BF16)] * 3 + [jax.ShapeDtypeStruct((S, 13 * D), BF16)],
        compiler_params=_cparams(("arbitrary", "arbitrary")),
    )(dmg, z, yc, ym, yx)


def _post_loss(r, x, tgt, g_post):
    S, D = r.shape
    T = min(TOK_TILE, S)

    def body(r_ref, x_ref, t_ref, g_ref, dy_ref, dr_ref, dg_ref, loss_ref):
        i = pl.program_id(0)

        @pl.when(i == 0)
        def _():
            dg_ref[...] = jnp.zeros_like(dg_ref)
            loss_ref[...] = jnp.zeros_like(loss_ref)

        rv = r_ref[...]
        g = g_ref[...]
        rs = lax.rsqrt(jnp.mean(rv * rv, axis=-1, keepdims=True) + EPS)
        nrm = rv * rs
        e = x_ref[...] + nrm * g - t_ref[...]
        part = jnp.sum(jnp.sum(e * e, axis=-1, keepdims=True), axis=0, keepdims=True) * (0.5 / D)
        loss_ref[0:1, 0:1] += part
        dy = e * (1.0 / D)
        dy_ref[...] = dy
        dg_ref[0:1, :] += jnp.sum(dy * nrm, axis=0, keepdims=True)
        dn = dy * g
        dr_ref[...] = (rs * (dn - nrm * jnp.mean(dn * nrm, axis=-1, keepdims=True))).astype(BF16)

    tok = pl.BlockSpec((T, D), lambda i: (i, 0))
    return pl.pallas_call(
        body, name="post_loss", grid=(S // T,),
        in_specs=[tok, tok, tok, pl.BlockSpec((1, D), lambda i: (0, 0))],
        out_specs=[tok, tok, pl.BlockSpec((8, D), lambda i: (0, 0)), pl.BlockSpec((8, 128), lambda i: (0, 0))],
        out_shape=[jax.ShapeDtypeStruct((S, D), F32), jax.ShapeDtypeStruct((S, D), BF16),
                   jax.ShapeDtypeStruct((8, D), F32), jax.ShapeDtypeStruct((8, 128), F32)],
        compiler_params=_cparams(("arbitrary",)),
    )(r, x, tgt, g_post)


def _conv_bwd1(dpc, z, u1, g_ln, b_ln, dz, D):
    S = z.shape[0]
    T = min(TOK_TILE // 2, S)

    def body(dpc_ref, zc_ref, u1_ref, gln_ref, bln_ref, dz_any, du1_ref, st_ref, dzc_ref):
        del dz_any
        i = pl.program_id(0)

        @pl.when(i == 0)
        def _():
            st_ref[...] = jnp.zeros_like(st_ref)

        u1v = u1_ref[...]
        mu = jnp.mean(u1v, axis=-1, keepdims=True)
        xc = u1v - mu
        rs = lax.rsqrt(jnp.mean(xc * xc, axis=-1, keepdims=True) + EPS)
        xh = xc * rs
        g = gln_ref[...]
        ln = xh * g + bln_ref[...]
        zc = zc_ref[...].astype(F32)
        dpc_ = dpc_ref[...].astype(F32)
        dzc_ref[...] = (dpc_ * _silu(ln) * _dsilu(zc)).astype(BF16)
        dln = dpc_ * _silu(zc) * _dsilu(ln)
        st_ref[0:1, :] += jnp.sum(dln * xh, axis=0, keepdims=True)
        st_ref[1:2, :] += jnp.sum(dln, axis=0, keepdims=True)
        dxh = dln * g
        du1 = rs * (dxh - jnp.mean(dxh, axis=-1, keepdims=True) - xh * jnp.mean(dxh * xh, axis=-1, keepdims=True))
        du1_ref[...] = du1
        st_ref[2:3, :] += jnp.sum(du1, axis=0, keepdims=True)

    tok = pl.BlockSpec((T, D), lambda i: (i, 0))
    vec = pl.BlockSpec((1, D), lambda i: (0, 0))
    return pl.pallas_call(
        body, name="conv_bwd1", grid=(S // T,),
        in_specs=[tok, pl.BlockSpec((T, D), lambda i: (i, 2)), tok, vec, vec, pl.BlockSpec(memory_space=pl.ANY)],
        out_specs=[tok, pl.BlockSpec((8, D), lambda i: (0, 0)), pl.BlockSpec((T, D), lambda i: (i, 2))],
        out_shape=[jax.ShapeDtypeStruct((S, D), F32), jax.ShapeDtypeStruct((8, D), F32),
                   jax.ShapeDtypeStruct(dz.shape, dz.dtype)],
        input_output_aliases={5: 2},
        compiler_params=_cparams(("arbitrary",)),
    )(dpc, z, u1, g_ln, b_ln, dz)


def _conv_bwd2(du1, z, w_dw, dz, D):
    S = z.shape[0]
    T = min(TOK_TILE, S)
    hb = T // HALO
    last = S // HALO - 1

    n_steps = S // T

    def body(du_ref, dun_ref, a_ref, b_ref, ah_ref, bh_ref, w_ref, dz_any, dw_ref, dab_ref,
             dus, u0s, shd, shu, dwp):
        del dz_any
        i = pl.program_id(0)

        @pl.when(i == 0)
        def _():
            dwp[...] = jnp.zeros_like(dwp)

        dus[pl.ds(0, T), :] = du_ref[...]
        dus[pl.ds(T, HALO), :] = jnp.where(i < n_steps - 1, dun_ref[...], 0.0)
        dus[pl.ds(T + HALO, 8), :] = jnp.zeros((8, D), F32)
        u0s[pl.ds(HALO, T), :] = a_ref[...].astype(F32) * _sig(b_ref[...].astype(F32))
        halo = ah_ref[...].astype(F32) * _sig(bh_ref[...].astype(F32))
        u0s[pl.ds(0, HALO), :] = jnp.where(i > 0, halo, 0.0)
        u0s[pl.ds(T + HALO, 8), :] = jnp.zeros((8, D), F32)
        for c0 in range(0, D, LANE):
            cs = pl.ds(c0, LANE)
            _shift_strip(dus, shd, cs, T + HALO)
            _shift_strip(u0s, shu, cs, T + HALO)
            for r0 in range(0, T, ROW_BLK):
                rows = pl.ds(r0, ROW_BLK)
                du0 = jnp.zeros((ROW_BLK, LANE), F32)
                for j in range(CONV_W):
                    du0 = du0 + w_ref[pl.ds(j, 1), cs] * _tap(dus, shd, cs, CONV_W - 1 - j, r0)
                a = a_ref[rows, cs].astype(F32)
                sb = _sig(b_ref[rows, cs].astype(F32))
                dab_ref[rows, cs] = (du0 * sb).astype(BF16)
                dab_ref[rows, pl.ds(D + c0, LANE)] = (du0 * a * sb * (1.0 - sb)).astype(BF16)
                d = dus[rows, cs]
                for j in range(CONV_W):
                    prod = d * _tap(u0s, shu, cs, HALO - (CONV_W - 1) + j, r0)
                    part = prod[0:8]
                    for q in range(8, ROW_BLK, 8):
                        part = part + prod[q:q + 8]
                    dwp[pl.ds(8 * j, 8), cs] += part

        @pl.when(i == n_steps - 1)
        def _():
            dw_ref[...] = jnp.zeros_like(dw_ref)
            for j in range(CONV_W):
                dw_ref[pl.ds(j, 1), :] = jnp.sum(dwp[pl.ds(8 * j, 8), :], axis=0, keepdims=True)

    tok = pl.BlockSpec((T, D), lambda i: (i, 0))
    nxt = lambda i: (jnp.minimum((i + 1) * hb, last), 0)
    prev = lambda i: (jnp.maximum(i * hb - 1, 0), 0)
    prev_b = lambda i: (jnp.maximum(i * hb - 1, 0), 1)
    return pl.pallas_call(
        body, name="conv_bwd2", grid=(S // T,),
        in_specs=[tok, pl.BlockSpec((HALO, D), nxt), tok, pl.BlockSpec((T, D), lambda i: (i, 1)),
                  pl.BlockSpec((HALO, D), prev), pl.BlockSpec((HALO, D), prev_b),
                  pl.BlockSpec((32, D), lambda i: (0, 0)), pl.BlockSpec(memory_space=pl.ANY)],
        out_specs=[pl.BlockSpec((32, D), lambda i: (0, 0)), pl.BlockSpec((T, 2 * D), lambda i: (i, 0))],
        out_shape=[jax.ShapeDtypeStruct((32, D), F32), jax.ShapeDtypeStruct(dz.shape, dz.dtype)],
        input_output_aliases={7: 1},
        scratch_shapes=[pltpu.VMEM((T + HALO + 8, D), F32), pltpu.VMEM((T + HALO + 8, D), F32),
                        pltpu.VMEM((8, T + HALO, LANE), F32), pltpu.VMEM((8, T + HALO, LANE), F32),
                        pltpu.VMEM((8 * 32, D), F32)],
        compiler_params=_cparams(("arbitrary",)),
    )(du1, du1, z, z, z, z, w_dw, dz)


def _qk_bwd1(dqks, z, w_qk, D):
    S = z.shape[0]
    T = min(TOK_TILE, S)
    hb = T // HALO
    scale = (D // N_HEADS) ** -0.5
    W2 = 2 * D

    n_steps = S // T

    def body(d_ref, p0_ref, p1_ref, h0_ref, h1_ref, w_ref, dc_ref, dw_ref, ps, sh, dwp):
        i = pl.program_id(0)

        @pl.when(i == 0)
        def _():
            dwp[...] = jnp.zeros_like(dwp)

        _qk_fill(ps, p0_ref, p1_ref, h0_ref, h1_ref, T, D)
        for c0 in range(0, W2, LANE):
            cs = pl.ds(c0, LANE)
            _shift_strip(ps, sh, cs, T + HALO, _QK_RES)
            for r0 in range(0, T, ROW_BLK):
                rows = pl.ds(r0, ROW_BLK)
                dc = d_ref[rows, cs].astype(F32) * _dsilu(_qk_conv(ps, sh, w_ref, cs, r0))
                if c0 >= D:
                    dc = dc * scale
                dc_ref[rows, cs] = dc.astype(BF16)
                for j in range(QK_W):
                    prod = dc * _tap(ps, sh, cs, HALO - (QK_W - 1) + j, r0)
                    part = prod[0:8]
                    for q in range(8, ROW_BLK, 8):
                        part = part + prod[q:q + 8]
                    dwp[pl.ds(8 * j, 8), cs] += part

        @pl.when(i == n_steps - 1)
        def _():
            dw_ref[...] = jnp.zeros_like(dw_ref)
            for j in range(QK_W):
                dw_ref[pl.ds(j, 1), :] = jnp.sum(dwp[pl.ds(8 * j, 8), :], axis=0, keepdims=True)

    prev0 = lambda i: (jnp.maximum(i * hb - 1, 0), 3)
    prev1 = lambda i: (jnp.maximum(i * hb - 1, 0), 4)
    return pl.pallas_call(
        body, name="qk_bwd1", grid=(S // T,),
        in_specs=[pl.BlockSpec((T, W2), lambda i: (i, 0)),
                  pl.BlockSpec((T, D), lambda i: (i, 3)), pl.BlockSpec((T, D), lambda i: (i, 4)),
                  pl.BlockSpec((HALO, D), prev0), pl.BlockSpec((HALO, D), prev1),
                  pl.BlockSpec((8, W2), lambda i: (0, 0))],
        out_specs=[pl.BlockSpec((T, W2), lambda i: (i, 0)), pl.BlockSpec((8, W2), lambda i: (0, 0))],
        out_shape=[jax.ShapeDtypeStruct((S, W2), BF16), jax.ShapeDtypeStruct((8, W2), F32)],
        scratch_shapes=[pltpu.VMEM((T + HALO + 8, W2), F32), pltpu.VMEM((8, T + HALO, LANE), F32),
                        pltpu.VMEM((8 * QK_W, W2), F32)],
        compiler_params=_cparams(("arbitrary",)),
    )(dqks, z, z, z, z, w_qk)


def _qk_bwd2(dc, w_qk, dz, D):
    S = dc.shape[0]
    T = min(TOK_TILE, S)
    hb = T // HALO
    last = S // HALO - 1

    n_steps = S // T

    def body(d_ref, dn_ref, w_ref, dz_any, o_ref, ds, sh):
        del dz_any
        i = pl.program_id(0)
        ds[pl.ds(0, T), :] = d_ref[...].astype(F32)
        ds[pl.ds(T, HALO), :] = jnp.where(i < n_steps - 1, dn_ref[...].astype(F32), 0.0)
        ds[pl.ds(T + HALO, 8), :] = jnp.zeros((8, D), F32)
        for c0 in range(0, D, LANE):
            cs = pl.ds(c0, LANE)
            _shift_strip(ds, sh, cs, T + HALO, range(1, QK_W))
            for r0 in range(0, T, ROW_BLK):
                acc = jnp.zeros((ROW_BLK, LANE), F32)
                for j in range(QK_W):
                    acc = acc + w_ref[pl.ds(j, 1), cs] * _tap(ds, sh, cs, QK_W - 1 - j, r0)
                o_ref[pl.ds(r0, ROW_BLK), cs] = acc.astype(BF16)

    return pl.pallas_call(
        body, name="qk_bwd2", grid=(S // T, 2),
        in_specs=[pl.BlockSpec((T, D), lambda i, j: (i, j)),
                  pl.BlockSpec((HALO, D), lambda i, j: (jnp.minimum((i + 1) * hb, last), j)),
                  pl.BlockSpec((8, D), lambda i, j: (0, j)), pl.BlockSpec(memory_space=pl.ANY)],
        out_specs=pl.BlockSpec((T, D), lambda i, j: (i, 3 + j)),
        out_shape=jax.ShapeDtypeStruct(dz.shape, dz.dtype),
        input_output_aliases={3: 0},
        scratch_shapes=[pltpu.VMEM((T + HALO + 8, D), F32), pltpu.VMEM((8, T + HALO, LANE), F32)],
        compiler_params=_cparams(("arbitrary", "arbitrary")),
    )(dc, dc, w_qk, dz)


TILE_BUDGET = 12 * 1024 * 1024


def _tile_2d(R, C, elem_bytes):
    cands = [t for t in range(BF16_ROWS, R + 1, BF16_ROWS) if R % t == 0 and t * C * elem_bytes <= TILE_BUDGET]
    if cands:
        return max(cands), C
    if R * C * elem_bytes <= TILE_BUDGET or C % LANE:
        return R, C
    cands = [t for t in range(LANE, C + 1, LANE) if C % t == 0 and t * R * elem_bytes <= TILE_BUDGET]
    return R, (max(cands) if cands else LANE)


def _sum_parts(parts, name):
    n, R, C = parts.shape
    tr, tc = _tile_2d(R, C, n * parts.dtype.itemsize + 4)

    def body(p_ref, o_ref):
        g = p_ref[0].astype(F32)
        for s in range(1, n):
            g = g + p_ref[s].astype(F32)
        o_ref[...] = g

    return pl.pallas_call(
        body, name=name, grid=(R // tr, C // tc),
        in_specs=[pl.BlockSpec((n, tr, tc), lambda i, j: (0, i, j))],
        out_specs=pl.BlockSpec((tr, tc), lambda i, j: (i, j)),
        out_shape=jax.ShapeDtypeStruct((R, C), F32),
        compiler_params=_cparams(("parallel", "parallel")),
    )(parts)


def _adamw(parts, w, m, v, name):
    R, C = w.shape
    n = parts.shape[0]
    tr, tc = _tile_2d(R, C, 4 * 7 + n * parts.dtype.itemsize)
    c1 = 1.0 / (1.0 - ADAM_B1 ** ADAM_STEP)
    c2 = 1.0 / (1.0 - ADAM_B2 ** ADAM_STEP)

    def body(p_ref, w_ref, m_ref, v_ref, g_ref, d_ref, nm_ref, nv_ref):
        g = p_ref[0].astype(F32)
        for s in range(1, n):
            g = g + p_ref[s].astype(F32)
        g_ref[...] = g
        mn = ADAM_B1 * m_ref[...] + (1.0 - ADAM_B1) * g
        vn = ADAM_B2 * v_ref[...] + (1.0 - ADAM_B2) * (g * g)
        nm_ref[...] = mn
        nv_ref[...] = vn
        d_ref[...] = -ADAM_LR * ((mn * c1) / (jnp.sqrt(vn * c2) + ADAM_EPS) + ADAM_WD * w_ref[...])

    blk = pl.BlockSpec((tr, tc), lambda i, j: (i, j))
    return pl.pallas_call(
        body, name=name, grid=(R // tr, C // tc),
        in_specs=[pl.BlockSpec((n, tr, tc), lambda i, j: (0, i, j)), blk, blk, blk],
        out_specs=[blk, blk, blk, blk],
        out_shape=[jax.ShapeDtypeStruct((R, C), F32)] * 4,
        compiler_params=_cparams(("parallel", "parallel")),
    )(parts, w, m, v)


def _exchange(operands, scatter, name):
    n = len(operands)

    def body(*refs):
        copies = _xchg_copies(refs[:n], refs[n:2 * n], *refs[2 * n:], scatter)
        for cp in copies:
            cp.start()
        for cp in copies:
            cp.wait()

    hbm = pl.BlockSpec(memory_space=pltpu.HBM)
    return pl.pallas_call(
        body, name=name, in_specs=[hbm] * n, out_specs=[hbm] * n, out_shape=_xchg_out_shapes(operands, scatter),
        scratch_shapes=_xchg_sems(n), compiler_params=pltpu.CompilerParams(has_side_effects=True),
    )(*operands)


N_CHIP = 4


class _CoreGather:
    def __init__(self, core):
        self.core = core

    def out_shape(self, blk):
        return jax.ShapeDtypeStruct((N_CHIP,) + blk.shape, blk.dtype)

    @staticmethod
    def sems():
        return [pltpu.SemaphoreType.DMA((7,)), pltpu.SemaphoreType.DMA((7,)), pltpu.SemaphoreType.DMA]

    def _parts(self, src, out, send_sems, recv_sems, loc_sem):
        x, y, c = lax.axis_index("x"), lax.axis_index("y"), lax.axis_index("c")
        me, sibling = (x, y, c), (x, y, 1 - c)
        chips = [(1 - x, y), (x, 1 - y), (1 - x, 1 - y)]

        def copy(k, chip, to, from_src=False):
            slot = out.at[2 * chip[0] + chip[1]]
            return pltpu.make_async_remote_copy(
                src_ref=src if from_src else slot, dst_ref=slot, send_sem=send_sems.at[k], recv_sem=recv_sems.at[k],
                device_id=to, device_id_type=pl.DeviceIdType.MESH)

        mine = pltpu.make_async_copy(src, out.at[2 * x + y], loc_sem)
        first = [copy(0, (x, y), sibling, True)] + [copy(1 + j, (x, y), (*ch, c), True) for j, ch in enumerate(chips)]
        landed = [copy(1 + j, ch, me) for j, ch in enumerate(chips)]
        passed = [copy(4 + j, ch, sibling) for j, ch in enumerate(chips)]
        from_sibling = [copy(0, (x, y), me)] + [copy(4 + j, ch, me) for j, ch in enumerate(chips)]
        return c == self.core, mine, first, landed, passed, from_sibling

    def start(self, *refs):
        holder, mine, first, _, _, _ = self._parts(*refs)

        @pl.when(holder)
        def _():
            mine.start()
            for cp in first:
                cp.start()

    def forward(self, *refs):
        holder, _, _, landed, passed, _ = self._parts(*refs)

        @pl.when(holder)
        def _():
            for got, cp in zip(landed, passed):
                got.wait_recv()
                cp.start()

    def finish(self, *refs):
        holder, mine, first, _, passed, from_sibling = self._parts(*refs)

        @pl.when(holder)
        def _():
            for cp in first + passed:
                cp.wait_send()
            mine.wait()

        @pl.when(jnp.logical_not(holder))
        def _():
            for cp in from_sibling:
                cp.wait_recv()


def _gather2_copies(srcs, dsts, send_sems, recv_sems, loc_sems):
    x, y, c = lax.axis_index("x"), lax.axis_index("y"), lax.axis_index("c")
    me, sibling = (x, y, c), (x, y, 1 - c)
    chips = [(1 - x, y), (x, 1 - y), (1 - x, 1 - y)]
    mine, first, landed, passed, arriving = [], [], [], [], []
    for t, (src, dst) in enumerate(zip(srcs, dsts)):
        def copy(k, block, to, from_src=False, src=src, dst=dst, t=t):
            slot = dst.at[4 * block[0] + 2 * block[1] + block[2]]
            return pltpu.make_async_remote_copy(
                src_ref=src if from_src else slot, dst_ref=slot,
                send_sem=send_sems.at[7 * t + k], recv_sem=recv_sems.at[7 * t + k],
                device_id=to, device_id_type=pl.DeviceIdType.MESH)

        mine.append(pltpu.make_async_copy(src, dst.at[4 * x + 2 * y + c], loc_sems.at[t]))
        first += [copy(0, me, sibling, True)] + [copy(1 + j, me, (*ch, c), True) for j, ch in enumerate(chips)]
        landed += [copy(1 + j, (*ch, c), me) for j, ch in enumerate(chips)]
        passed += [copy(4 + j, (*ch, c), sibling) for j, ch in enumerate(chips)]
        arriving += [copy(0, sibling, me)] + [copy(4 + j, (*ch, 1 - c), me) for j, ch in enumerate(chips)]
    return mine, first, landed, passed, arriving


def _core_gather(blk, core, name):
    g = _CoreGather(core)

    def body(*refs):
        g.start(*refs)
        g.forward(*refs)
        g.finish(*refs)

    hbm = pl.BlockSpec(memory_space=pltpu.HBM)
    return pl.pallas_call(
        body, name=name, in_specs=[hbm], out_specs=hbm, out_shape=g.out_shape(blk), scratch_shapes=g.sems(),
        compiler_params=pltpu.CompilerParams(has_side_effects=True),
    )(blk)


IN_TILE = 512


def _tile_index(tiles):
    jumps = [(i, tiles[i] - tiles[i - 1] - 1) for i in range(1, len(tiles)) if tiles[i] != tiles[i - 1] + 1]

    def f(t):
        j = t + tiles[0]
        for i, gap in jumps:
            j = j + jnp.where(t >= i, gap, 0)
        return j

    return f


def _in_proj_phase(h, w_rows, tiles, z_prev, name, gather=None, xchg=None):
    S, D = h.shape
    nt = len(tiles)
    assert w_rows.shape == (nt * IN_TILE, D)
    tm = _fit(S, 2048)
    ni = S // tm
    grid = (ni, nt)
    mid = (3 * ni * nt) // 4
    col = _tile_index(tiles)
    n_in = 2 + (z_prev is not None)
    gather = list(gather or [])
    g_ops = [blk for blk, _ in gather]
    x_ops = list(xchg[0]) if xchg else []
    n_g, n_x = len(g_ops), len(x_ops)

    def body(*refs):
        h_ref, w_ref = refs[0], refs[1]
        g_src = refs[n_in:n_in + n_g]
        x_src = refs[n_in + n_g:n_in + n_g + n_x]
        o = n_in + n_g + n_x
        z_ref = refs[o]
        g_dst = refs[o + 1:o + 1 + n_g]
        x_dst = refs[o + 1 + n_g:o + 1 + n_g + n_x]
        sems = refs[o + 1 + n_g + n_x:]
        g_sems, x_sems = sems[:3 * n_g], sems[3 * n_g:]
        i, t = pl.program_id(0), pl.program_id(1)
        first = (t == 0) & (i == 0)
        last = (t == nt - 1) & (i == ni - 1)

        def each_gather(step):
            for n, (_, how) in enumerate(gather):
                getattr(how, step)(g_src[n], g_dst[n], *g_sems[3 * n:3 * n + 3])

        @pl.when(first)
        def _():
            each_gather("start")
            if xchg:
                mine, out, _, _, _ = _gather2_copies(x_src, x_dst, *x_sems)
                for cp in mine + out:
                    cp.start()

        z_ref[...] = _dot_nt(h_ref[...], w_ref[...]).astype(z_ref.dtype)

        @pl.when((i == mid // nt) & (t == mid % nt))
        def _():
            each_gather("forward")
            if xchg:
                _, _, landed, passed, _ = _gather2_copies(x_src, x_dst, *x_sems)
                for got, cp in zip(landed, passed):
                    got.wait_recv()
                    cp.start()

        @pl.when(last)
        def _():
            each_gather("finish")
            if xchg:
                mine, out, _, passed, arriving = _gather2_copies(x_src, x_dst, *x_sems)
                for cp in arriving:
                    cp.wait_recv()
                for cp in out + passed:
                    cp.wait_send()
                for cp in mine:
                    cp.wait()

    hbm = pl.BlockSpec(memory_space=pltpu.HBM)
    in_specs = [pl.BlockSpec((tm, D), lambda i, t: (i, 0)), pl.BlockSpec((IN_TILE, D), lambda i, t: (t, 0))]
    args = [h, w_rows]
    aliases = {}
    if z_prev is not None:
        in_specs.append(pl.BlockSpec(memory_space=pl.ANY))
        args.append(z_prev)
        aliases = {2: 0}
    in_specs += [hbm] * (n_g + n_x)
    args += g_ops + x_ops
    out_specs = [pl.BlockSpec((tm, IN_TILE), lambda i, t: (i, col(t)))] + [hbm] * (n_g + n_x)
    out_shape = [jax.ShapeDtypeStruct((S, 13 * D), BF16)]
    scratch = []
    for blk, how in gather:
        out_shape.append(how.out_shape(blk))
        scratch += how.sems()
    if xchg:
        out_shape += _xchg_out_shapes(x_ops, xchg[1])
        scratch += _xchg_sems(n_x)
    comm = bool(gather or xchg)
    res = pl.pallas_call(
        body, name=name, grid=grid, in_specs=in_specs, out_specs=out_specs, out_shape=out_shape,
        input_output_aliases=aliases, scratch_shapes=scratch,
        compiler_params=pltpu.CompilerParams(vmem_limit_bytes=VMEM_LIMIT, has_side_effects=comm,
                                             dimension_semantics=("arbitrary", "arbitrary")),
    )(*args)
    z = res[0]
    g_out = list(res[1:1 + n_g])
    x_out = list(res[1 + n_g:]) if xchg else None
    return z, g_out, x_out


def _in_proj_dx(dz, wpT, dgates, wifT, x, rstd, dy, g, side=None):
    S, K = dz.shape
    D = wpT.shape[1]
    tm, tk = _fit(S, 512), _fit(K, 2048)
    nm, nk = S // tm, K // tk
    n_side = len(side[0]) if side else 0

    rc = min(tm, 128)

    def body(*refs):
        a_ref, b_ref, dgt_ref, wif_ref, x_hbm, r_ref, dy_hbm, g_ref = refs[:8]
        srcs = refs[8:8 + n_side]
        dx_ref, dg_ref = refs[8 + n_side], refs[9 + n_side]
        dsts = refs[10 + n_side:10 + 2 * n_side]
        acc, xbuf, dybuf, ep_sems = refs[10 + 2 * n_side:14 + 2 * n_side]
        sems = refs[14 + 2 * n_side:]
        i, k = pl.program_id(0), pl.program_id(1)
        rows = pl.ds(pl.multiple_of(i * tm, tm), tm)
        fetch = [pltpu.make_async_copy(x_hbm.at[rows, :], xbuf, ep_sems.at[0]),
                 pltpu.make_async_copy(dy_hbm.at[rows, :], dybuf, ep_sems.at[1])]

        @pl.when((i == 0) & (k == 0))
        def _():
            dg_ref[...] = jnp.zeros_like(dg_ref)
            if n_side:
                for cp in _xchg_copies(srcs, dsts, *sems, side[1]):
                    cp.start()

        @pl.when(k == nk - 2)
        def _():
            for cp in fetch:
                cp.start()

        p = _dot_nn(a_ref[...], b_ref[...])

        @pl.when(k == 0)
        def _():
            acc[...] = p

        @pl.when((k > 0) & (k < nk - 1))
        def _():
            acc[...] += p

        @pl.when(k == nk - 1)
        def _():
            acc[...] += p + _dot_nn(dgt_ref[...], wif_ref[...])
            for cp in fetch:
                cp.wait()
            for r0 in range(0, tm, rc):
                rr = pl.ds(r0, rc)
                dh = acc[rr, :]
                rs = r_ref[rr, :]
                xn = xbuf[rr, :] * rs
                dg_ref[0:1, :] += jnp.sum(dh * xn, axis=0, keepdims=True)
                dxn = dh * g_ref[...]
                dx_ref[rr, :] = dybuf[rr, :] + rs * (dxn - xn * jnp.mean(dxn * xn, axis=-1, keepdims=True))

        if n_side:
            @pl.when((i == nm - 1) & (k == nk - 1))
            def _():
                for cp in _xchg_copies(srcs, dsts, *sems, side[1]):
                    cp.wait()

    assert nk > 1
    hbm = pl.BlockSpec(memory_space=pltpu.HBM)
    tok = pl.BlockSpec((tm, D), lambda i, k: (i, 0))
    in_specs = [pl.BlockSpec((tm, tk), lambda i, k: (i, k)), pl.BlockSpec((tk, D), lambda i, k: (k, 0)),
                pl.BlockSpec((tm, IF_PAD), lambda i, k: (i, 0)), pl.BlockSpec((IF_PAD, D), lambda i, k: (0, 0)),
                hbm, pl.BlockSpec((tm, 1), lambda i, k: (i, 0)), hbm, pl.BlockSpec((1, D), lambda i, k: (0, 0))]
    out_specs = [tok, pl.BlockSpec((8, D), lambda i, k: (0, 0))]
    out_shape = [jax.ShapeDtypeStruct((S, D), F32), jax.ShapeDtypeStruct((8, D), F32)]
    scratch = [pltpu.VMEM((tm, D), F32), pltpu.VMEM((tm, D), F32), pltpu.VMEM((tm, D), F32),
               pltpu.SemaphoreType.DMA((2,))]
    args = [dz, wpT, dgates, wifT, x, rstd, dy, g]
    if n_side:
        in_specs += [hbm] * n_side
        args += list(side[0])
        out_specs += [hbm] * n_side
        out_shape += _xchg_out_shapes(side[0], side[1])
        scratch += _xchg_sems(n_side)
    res = pl.pallas_call(
        body, name="in_proj_dx", grid=(nm, nk), in_specs=in_specs, out_specs=out_specs, out_shape=out_shape,
        scratch_shapes=scratch,
        compiler_params=pltpu.CompilerParams(vmem_limit_bytes=VMEM_LIMIT, has_side_effects=bool(n_side),
                                             dimension_semantics=("arbitrary", "arbitrary")),
    )(*args)
    return res[0], res[1], list(res[2:])


LATE = ('w_conv_out', 'w_ml_out', 'w_xa_out', 'w_out', 'w_mem_kv', 'w_qk_conv', 'w_dw')
ROW_SHARDED = ('w_conv_out', 'w_ml_out', 'w_xa_out', 'w_out')
COL_SHARDED = ('w_in', 'w_mem_kv', 'w_qk_conv', 'w_dw')


def _cols_to_slabs(g):
    R, C8 = g.shape
    return g.reshape(R, N_DEV, C8 // N_DEV).transpose(1, 0, 2)


def _slabs_to_cols(a):
    n, R, C = a.shape
    return a.transpose(1, 0, 2).reshape(R, n * C)


def _assemble(name, slabs):
    if name in ROW_SHARDED:
        return slabs.reshape(slabs.shape[0] * slabs.shape[1], slabs.shape[2])
    return _slabs_to_cols(slabs)


def _to_slabs(name, g, dtype):
    if name in ROW_SHARDED:
        return g.reshape(N_DEV, g.shape[0] // N_DEV, g.shape[1]).astype(dtype)
    return _cols_to_slabs(g).astype(dtype)


def _w_in_rows(D):
    n = (13 * D + N_IF) // N_DEV
    gate_dev, gate_row = (8 * D) // n, (8 * D) % n
    assert gate_row + N_IF <= n
    height = -(-(n + BF16_ROWS - 1) // BF16_ROWS) * BF16_ROWS

    def first(p):
        return n * p - jnp.where(n * p >= 8 * D + N_IF, N_IF, 0)

    def start(p):
        return jnp.minimum((first(p) // BF16_ROWS) * BF16_ROWS, 13 * D - height)

    return n, gate_dev, gate_row, height, first, start


def _w_in_tiles(D):
    n, gate_dev, *_ = _w_in_rows(D)

    def rows(s):
        f = n * s - (N_IF if n * s >= 8 * D + N_IF else 0)
        return f, f + n - (N_IF if s == gate_dev else 0)

    tiles = range(13 * D // IN_TILE)
    inside = lambda j, s: rows(s)[0] <= j * IN_TILE and (j + 1) * IN_TILE <= rows(s)[1]
    early = [j for j in tiles if any(inside(j, s) for s in range(0, N_DEV, 2))]
    return early, [j for j in tiles if j not in early], rows


def _w_window_height(D):
    n = (13 * D + N_IF) // N_DEV
    return -(-(n + IN_TILE - 1) // IN_TILE) * IN_TILE


def _w_window(block_t, me, D):
    n, gate_dev, gate_row, _, first, _ = _w_in_rows(D)
    no_gates = jnp.concatenate([block_t[:gate_row], block_t[gate_row + N_IF:], jnp.zeros((N_IF, D), block_t.dtype)])
    mine = jnp.where(me == gate_dev, no_gates, block_t)
    off = first(me) % IN_TILE
    return lax.dynamic_update_slice(jnp.zeros((_w_window_height(D), D), block_t.dtype), mine, (off, 0))


def _w_window_parts(window, me, D):
    *_, first, _ = _w_in_rows(D)
    n_win = _w_window_height(D) // IN_TILE
    lead = jnp.where(first(me) % IN_TILE == 0, 0, 1)
    core = lax.dynamic_slice(window, (lead * IN_TILE, 0), ((n_win - 2) * IN_TILE, D))
    low = jnp.where(lead == 0, window[(n_win - 2) * IN_TILE:(n_win - 1) * IN_TILE], window[:IN_TILE])
    return core, jnp.concatenate([low, window[(n_win - 1) * IN_TILE:]], axis=0)


def _w_segments(s, D, window=None, core=None, edges=None):
    _, _, rows = _w_in_tiles(D)
    n_win = _w_window_height(D) // IN_TILE
    t0, lead = rows(s)[0] // IN_TILE, 0 if rows(s)[0] % IN_TILE == 0 else 1
    segs = []
    if window is not None:
        segs.append((t0, window))
    if core is not None:
        segs.append((t0 + lead, core))
    if edges is not None:
        segs += [(t0 + (n_win - 2 if lead == 0 else 0), edges[:IN_TILE]), (t0 + n_win - 1, edges[IN_TILE:])]
    return segs


def _w_rows_of_tiles(tiles, segments, D):
    _, _, rows = _w_in_tiles(D)

    def find(s, j):
        for t0, arr in segments[s]:
            if t0 <= j < t0 + arr.shape[0] // IN_TILE:
                return arr, j - t0
        raise KeyError((s, j))

    runs = []
    for j in tiles:
        own = [s for s in range(N_DEV) if rows(s)[0] < (j + 1) * IN_TILE and j * IN_TILE < rows(s)[1]]
        got = [find(s, j) for s in own]
        last = runs[-1] if runs else None
        if (last and len(got) == 1 and len(last[0]) == 1 and last[2] == j and last[0][0][0] is got[0][0]
                and last[0][0][1] + last[1] == got[0][1]):
            last[1], last[2] = last[1] + 1, j + 1
        else:
            runs.append([got, 1, j + 1])
    out = [functools.reduce(jnp.add, [arr[i * IN_TILE:(i + k) * IN_TILE] for arr, i in got]) for got, k, _ in runs]
    return jnp.concatenate(out, axis=0)


def _interleave_tiles(parts):
    where = {j: (rows, i) for rows, tiles in parts for i, j in enumerate(tiles)}
    out, j, n_tiles = [], 0, len(where)
    while j < n_tiles:
        rows, i = where[j]
        k = 1
        while j + k in where and where[j + k][0] is rows and where[j + k][1] == i + k:
            k += 1
        out.append(rows[i * IN_TILE:(i + k) * IN_TILE])
        j += k
    return jnp.concatenate(out, axis=0)


def _local_step(x, mem, tgt, g_pre, w_early, w_rest, wifT, b_if, b_dw, g_ln, b_ln, g_head, g_mem, g_post, late, dist):
    S, D = x.shape
    L = min(ML_CHUNK, S)
    NC = S // L
    bif_row = jnp.zeros((1, IF_PAD), F32).at[0, :N_IF].set(b_if)
    bif_col = jnp.zeros((16, 1), F32).at[:N_IF, 0].set(b_if)
    early, rest, _ = _w_in_tiles(D)

    h, rstd = _rmsnorm_fwd(x, g_pre, "prenorm_fwd")
    if dist:
        my_window, my_edges, rows_of_rest = w_rest
        z, (north, south_edges), _ = _in_proj_phase(
            h, w_early, early, None, "in_proj_early",
            gather=[(my_window, _CoreGather(1)), (my_edges, _CoreGather(0))])
        w_rest = rows_of_rest(north, south_edges)
        z, _, gathered = _in_proj_phase(h, w_rest, rest, z, "in_proj_rest",
                                        xchg=([late[n] for n in LATE] + [wifT], False))
        full = {n: _assemble(n, a) for n, a in zip(LATE, gathered)}
        wifT = jnp.zeros((IF_PAD, D), BF16).at[:N_IF].set(gathered[-1][_w_in_rows(D)[1], :N_IF])
    else:
        z, _, _ = _in_proj_phase(h, w_early, early, None, "in_proj_early")
        z, _, _ = _in_proj_phase(h, w_rest, rest, z, "in_proj_rest")
        full = late
    wpT = _interleave_tiles([(w_early, early), (w_rest, rest)])
    w_co, w_mo, w_xo, w_out, w_kv = (full[n] for n in LATE[:5])
    w_qk = jnp.zeros((8, 2 * D), F32).at[:QK_W].set(full['w_qk_conv'])
    w_dw = jnp.zeros((32, D), F32).at[:CONV_W].set(full['w_dw'])
    gcol = _mm(h, wifT, mode="nt", out_dtype=F32, tm=1024, tn=IF_PAD, tk=2048, name="gates_col")
    grow = _mm(wifT[:16], h, mode="nt", out_dtype=F32, tm=16, tn=1024, tk=2048, name="gates_row")
    grow = grow.reshape(16, NC, L).transpose(1, 0, 2)

    u1, pc = _conv_fwd(z, w_dw, b_dw, g_ln, b_ln, D)
    qks = _qk_fwd(z, w_qk, D)
    hm, cst, nst = _mlstm_fwd(qks, z, gcol, grow, bif_row, bif_col, D)
    pm = _ml_post_fwd(hm, z, g_head, D)
    memn, rstd_mem = _rmsnorm_fwd(mem, g_mem, "memnorm_fwd")
    kv = _mm(memn, w_kv, mode="nn", out_dtype=BF16, tm=256, tn=1024, tk=2048, name="mem_kv")
    px = _xattn_fwd(z, kv, D)

    proj = functools.partial(_mm, mode="nn", tm=1024, tn=1024, tk=2048)
    yc = proj(pc, w_co, out_dtype=BF16, name="conv_out")
    ym = proj(pm, w_mo, out_dtype=BF16, name="ml_out")
    yx = proj(px, w_xo, out_dtype=BF16, name="xa_out")
    mg = _merge_fwd(z, yc, ym, yx, D)
    r = proj(mg, w_out, out_dtype=F32, name="out_proj")
    dy, dr, dg_post, loss = _post_loss(r, x, tgt, g_post)

    dgrad = functools.partial(_mm, mode="nt", out_dtype=BF16, tm=1024, tn=1024, tk=2048)
    wgrad = functools.partial(_mm, mode="tn", out_dtype=F32, tm=1024, tn=1024, tk=2048)
    dmg = dgrad(dr, w_out, name="out_proj_dx")
    dw_out = wgrad(mg, dr, name="out_proj_dw")
    dyc, dym, dyx, dz = _merge_bwd(dmg, z, yc, ym, yx, D)
    dpc = dgrad(dyc, w_co, name="conv_out_dx")
    dw_co = wgrad(pc, dyc, name="conv_out_dw")
    dpm = dgrad(dym, w_mo, name="ml_out_dx")
    dw_mo = wgrad(pm, dym, name="ml_out_dw")
    dpx = dgrad(dyx, w_xo, name="xa_out_dx")
    dw_xo = wgrad(px, dyx, name="xa_out_dw")

    dkv, dz = _xattn_bwd(dpx, z, kv, dz, D)
    dkv_b = dkv.astype(BF16)
    dw_kv = wgrad(memn, dkv_b, name="mem_kv_dw")
    dmemn = _mm(dkv_b, w_kv, mode="nt", out_dtype=F32, tm=256, tn=1024, tk=2048, name="mem_kv_dx")
    dg_mem = _gain_grad(dmemn, mem, rstd_mem)

    du1, cstats, dz = _conv_bwd1(dpc, z, u1, g_ln, b_ln, dz, D)
    dw_dw, dz = _conv_bwd2(du1, z, w_dw, dz, D)

    dhm, dg_head, dz = _ml_post_bwd(dpm, hm, z, g_head, dz, D)
    dqks, dgates, dz = _mlstm_bwd(qks, z, gcol, grow, bif_row, bif_col, hm, dhm, cst, nst, dz, D)
    dc, dw_qk = _qk_bwd1(dqks, z, w_qk, D)
    dz = _qk_bwd2(dc, w_qk, dz, D)

    g = dict(w_conv_out=dw_co, w_ml_out=dw_mo, w_xa_out=dw_xo, w_out=dw_out, w_mem_kv=dw_kv,
             w_qk_conv=dw_qk[:QK_W], w_dw=dw_dw[:CONV_W])
    dgates_b = dgates.astype(BF16)
    in_proj_dw = functools.partial(_mm, dz, h, mode="tn", out_dtype=BF16 if dist else F32, tm=1024, tn=2048, tk=2048,
                                   name="in_proj_dw")
    in_proj_dx = functools.partial(_in_proj_dx, dz, wpT, dgates_b, wifT, x, rstd, dy, g_pre)
    parts = {}
    if dist:
        send = [_to_slabs(n, g[n], BF16 if n in LATE[:5] else F32) for n in LATE]
        dwpT, recv = in_proj_dw(side=(send, True))
        parts.update(zip(LATE, recv))
        *_, height, _, start = _w_in_rows(D)
        dx, dg_pre, recv = in_proj_dx(side=([dwpT], _RowWindows(start, height)))
        parts['w_in'] = recv[0]
    else:
        dwpT = in_proj_dw()
        dx, dg_pre, _ = in_proj_dx()
    dwifT = _mm(dgates_b, h, mode="tn", out_dtype=F32, tm=IF_PAD, tn=2048, tk=2048, name="gates_dw")
    if not dist:
        g['w_in'] = jnp.concatenate([dwpT[:8 * D], dwifT[:N_IF], dwpT[8 * D:]], axis=0).T

    db_if = _col_sum(dgates)[0, :N_IF]
    g.update(g_pre=dg_pre[0], b_if=db_if, b_dw=cstats[2], g_ln=cstats[0], b_ln=cstats[1], g_ml_head=dg_head[0],
             g_mem=dg_mem[0], g_post=dg_post[0], w_in_gates=dwifT[:N_IF])
    return loss[0, 0], dx, g, parts


WEIGHTS = ('g_pre', 'w_in', 'b_if', 'w_qk_conv', 'w_dw', 'b_dw', 'g_ln', 'b_ln', 'w_conv_out', 'g_ml_head',
           'w_ml_out', 'g_mem', 'w_mem_kv', 'w_xa_out', 'w_out', 'g_post')
VECTORS = ('g_pre', 'b_dw', 'g_ln', 'b_ln', 'g_ml_head', 'g_mem', 'g_post')


def kernel(x, mem, g_pre, w_in, b_if, w_qk_conv, w_dw, b_dw, g_ln, b_ln, w_conv_out, g_ml_head, w_ml_out, g_mem, w_mem_kv, w_xa_out, w_out, g_post, loss_target, m_g_pre, m_w_in, m_b_if, m_w_qk_conv, m_w_dw, m_b_dw, m_g_ln, m_b_ln, m_w_conv_out, m_g_ml_head, m_w_ml_out, m_g_mem, m_w_mem_kv, m_w_xa_out, m_w_out, m_g_post, v_g_pre, v_w_in, v_b_if, v_w_qk_conv, v_w_dw, v_b_dw, v_g_ln, v_b_ln, v_w_conv_out, v_g_ml_head, v_w_ml_out, v_g_mem, v_w_mem_kv, v_w_xa_out, v_w_out, v_g_post):
    S, D = x.shape[1], x.shape[2]
    w = dict(g_pre=g_pre, w_in=w_in, b_if=b_if, w_qk_conv=w_qk_conv, w_dw=w_dw, b_dw=b_dw, g_ln=g_ln, b_ln=b_ln,
             w_conv_out=w_conv_out, g_ml_head=g_ml_head, w_ml_out=w_ml_out, g_mem=g_mem, w_mem_kv=w_mem_kv,
             w_xa_out=w_xa_out, w_out=w_out, g_post=g_post)
    mom = dict(g_pre=m_g_pre, w_in=m_w_in, b_if=m_b_if, w_qk_conv=m_w_qk_conv, w_dw=m_w_dw, b_dw=m_b_dw,
               g_ln=m_g_ln, b_ln=m_b_ln, w_conv_out=m_w_conv_out, g_ml_head=m_g_ml_head, w_ml_out=m_w_ml_out,
               g_mem=m_g_mem, w_mem_kv=m_w_mem_kv, w_xa_out=m_w_xa_out, w_out=m_w_out, g_post=m_g_post)
    var = dict(g_pre=v_g_pre, w_in=v_w_in, b_if=v_b_if, w_qk_conv=v_w_qk_conv, w_dw=v_w_dw, b_dw=v_b_dw,
               g_ln=v_g_ln, b_ln=v_b_ln, w_conv_out=v_w_conv_out, g_ml_head=v_g_ml_head, w_ml_out=v_w_ml_out,
               g_mem=v_g_mem, w_mem_kv=v_w_mem_kv, w_xa_out=v_w_xa_out, w_out=v_w_out, g_post=v_g_post)

    n_rows, gate_dev, gate_row, height, first, start = _w_in_rows(D)
    me = 4 * lax.axis_index("x") + 2 * lax.axis_index("y") + lax.axis_index("c")

    my_block = w_in.T.astype(BF16)
    my_window = _w_window(my_block, me, D)
    my_core, my_edges = _w_window_parts(my_window, me, D)
    south = _core_gather(my_core, 0, "gather_w_in_south")
    early, rest, _ = _w_in_tiles(D)
    w_early = _w_rows_of_tiles(early, {2 * q: _w_segments(2 * q, D, core=south[q]) for q in range(N_CHIP)}, D)
    my_gates = jnp.zeros((BF16_ROWS, D), BF16).at[:N_IF].set(my_block[gate_row:gate_row + N_IF])

    def rows_of_rest(north, south_edges):
        segments = {2 * q: _w_segments(2 * q, D, core=south[q], edges=south_edges[q]) for q in range(N_CHIP)}
        segments.update({2 * q + 1: _w_segments(2 * q + 1, D, window=north[q]) for q in range(N_CHIP)})
        return _w_rows_of_tiles(rest, segments, D)

    late = {n: w[n].astype(BF16) if n in LATE[:5] else w[n] for n in LATE}
    row = lambda a: a.reshape(1, D)

    loss, dx, g, parts = _local_step(
        x[0], mem[0], loss_target[0], row(g_pre), w_early, (my_window, my_edges, rows_of_rest), my_gates, b_if,
        row(b_dw), row(g_ln), row(b_ln), row(g_ml_head), row(g_mem), row(g_post), late, True)

    pad = lambda a: jnp.zeros((D,), F32).at[:N_IF].set(a)
    vec = jnp.concatenate([jnp.stack([g[n] for n in VECTORS] + [pad(g['b_if'])]), g['w_in_gates']])
    parts['vec'] = _exchange([vec], False, "gather_vector_grads")[0]

    out = {}
    for n in LATE:
        out[n] = _adamw(parts[n], w[n], mom[n], var[n], "adamw_" + n)
    zeros8 = jnp.zeros((N_IF, D), F32)
    stack = lambda d: jnp.concatenate([jnp.stack([d[n] for n in VECTORS] + [pad(d['b_if'])]), zeros8])
    vres = _adamw(parts['vec'], stack(w), stack(mom), stack(var), "adamw_vectors")
    for i, n in enumerate(VECTORS):
        out[n] = [r[i] for r in vres]
    out['b_if'] = [r[len(VECTORS), :N_IF] for r in vres]
    gate_grad = vres[0][len(VECTORS) + 1:]

    win = _sum_parts(parts['w_in'], "sum_w_in_grads")
    seg = lax.dynamic_slice(win, (first(me) - start(me), 0), (n_rows, D))
    with_gates = jnp.concatenate([seg[:gate_row], gate_grad, seg[gate_row:n_rows - N_IF]], axis=0)
    g_in = jnp.where(me == gate_dev, with_gates, seg)
    out['w_in'] = [r.T for r in _adamw(g_in[None], w_in.T, m_w_in.T, v_w_in.T, "adamw_w_in")]

    loss = lax.psum(loss, ("x", "y", "c"))
    res = [loss, dx.reshape(1, S, D)]
    for k in range(4):
        res += [out[n][k] for n in WEIGHTS]
    return tuple(res)
```

```python
import functools
import math

import jax
import jax.numpy as jnp
from jax import lax
from jax.experimental import pallas as pl
from jax.experimental.pallas import tpu as pltpu

F32 = jnp.float32
BF16 = jnp.bfloat16

N_DEV = 8
N_HEADS = 4
CONV_W = 31
QK_W = 4
N_IF = 2 * N_HEADS
IF_PAD = 128
EPS = 1e-6
NEG = -1e30

ADAM_LR = 0.001
ADAM_B1 = 0.9
ADAM_B2 = 0.999
ADAM_EPS = 1e-08
ADAM_WD = 0.01
ADAM_STEP = 10

TOK_TILE = 512
ML_CHUNK = 256
HALO = 32
BF16_ROWS = 16
VMEM_LIMIT = 56 * 1024 * 1024


def _cparams(sem=None):
    kw = dict(vmem_limit_bytes=VMEM_LIMIT)
    if sem is not None:
        kw["dimension_semantics"] = sem
    return pltpu.CompilerParams(**kw)


def _sig(x):
    return jax.nn.sigmoid(x)


def _silu(x):
    return x * _sig(x)


def _dsilu(x):
    s = _sig(x)
    return s * (1.0 + x * (1.0 - s))


def _logsig(x):
    return jnp.minimum(x, 0.0) - jnp.log(1.0 + jnp.exp(-jnp.abs(x)))


def _dot(a, b, dims):
    return lax.dot_general(a, b, (dims, ((), ())), preferred_element_type=F32)


def _dot_nn(a, b):
    return _dot(a, b, ((1,), (0,)))


def _dot_nt(a, b):
    return _dot(a, b, ((1,), (1,)))


def _dot_tn(a, b):
    return _dot(a, b, ((0,), (0,)))


def _split3(a):
    hi = a.astype(BF16)
    r1 = a - hi.astype(F32)
    mid = r1.astype(BF16)
    lo = (r1 - mid.astype(F32)).astype(BF16)
    return hi, mid, lo


def _tri_left(tri, a):
    hi, mid, lo = _split3(a)
    return _dot_nn(tri, hi) + _dot_nn(tri, mid) + _dot_nn(tri, lo)


def _tri_right(a, tri):
    hi, mid, lo = _split3(a)
    return _dot_nn(hi, tri) + _dot_nn(mid, tri) + _dot_nn(lo, tri)


def _fit(n, t):
    if n <= t:
        return n
    return max(c for c in range(128, t + 1, 128) if n % c == 0)


def _peer(k):
    x, y, c = lax.axis_index("x"), lax.axis_index("y"), lax.axis_index("c")
    px = 1 - x if (k >> 2) & 1 else x
    py = 1 - y if (k >> 1) & 1 else y
    pc = 1 - c if k & 1 else c
    return (px, py, pc), 4 * px + 2 * py + pc


class _RowWindows:
    def __init__(self, start, height):
        self.start, self.height = start, height


def _xchg_copies(srcs, dsts, send_sems, recv_sems, loc_sems, scatter):
    _, me = _peer(0)

    def piece(src, p):
        if isinstance(scatter, _RowWindows):
            return src.at[pl.ds(pl.multiple_of(scatter.start(p), BF16_ROWS), scatter.height), :]
        return src.at[p] if scatter else src

    copies = []
    for t, (src, dst) in enumerate(zip(srcs, dsts)):
        copies.append(pltpu.make_async_copy(piece(src, me), dst.at[me], loc_sems.at[t]))
        for k in range(1, N_DEV):
            dev, p = _peer(k)
            s = t * (N_DEV - 1) + k - 1
            copies.append(pltpu.make_async_remote_copy(
                src_ref=piece(src, p), dst_ref=dst.at[me],
                send_sem=send_sems.at[s], recv_sem=recv_sems.at[s],
                device_id=dev, device_id_type=pl.DeviceIdType.MESH))
    return copies


def _xchg_out_shapes(operands, scatter):
    def one(a):
        if isinstance(scatter, _RowWindows):
            return (scatter.height, a.shape[1])
        return tuple(a.shape[1:] if scatter else a.shape)
    return [jax.ShapeDtypeStruct((N_DEV,) + one(a), a.dtype) for a in operands]


def _xchg_sems(n):
    return [pltpu.SemaphoreType.DMA((n * (N_DEV - 1),)), pltpu.SemaphoreType.DMA((n * (N_DEV - 1),)),
            pltpu.SemaphoreType.DMA((n,))]


def _mm(a, b, *, mode, out_dtype, tm, tn, tk, name, n_outer=False, acc_in=None, side=None):
    if mode == "nn":
        (M, K), (K2, N) = a.shape, b.shape
    elif mode == "nt":
        (M, K), (N, K2) = a.shape, b.shape
    else:
        (K, M), (K2, N) = a.shape, b.shape
    assert K == K2, (a.shape, b.shape, mode)
    tm, tn, tk = _fit(M, tm), _fit(N, tn), _fit(K, tk)
    assert M % tm == 0 and N % tn == 0 and K % tk == 0, (a.shape, b.shape, tm, tn, tk)
    nk = K // tk
    if n_outer:
        grid = (N // tn, M // tm, nk)
        ij = lambda g0, g1: (g1, g0)
    else:
        grid = (M // tm, N // tn, nk)
        ij = lambda g0, g1: (g0, g1)

    if mode == "nn":
        a_spec = pl.BlockSpec((tm, tk), lambda g0, g1, k: (ij(g0, g1)[0], k))
        b_spec = pl.BlockSpec((tk, tn), lambda g0, g1, k: (k, ij(g0, g1)[1]))
        dot = _dot_nn
    elif mode == "nt":
        a_spec = pl.BlockSpec((tm, tk), lambda g0, g1, k: (ij(g0, g1)[0], k))
        b_spec = pl.BlockSpec((tn, tk), lambda g0, g1, k: (ij(g0, g1)[1], k))
        dot = _dot_nt
    else:
        a_spec = pl.BlockSpec((tk, tm), lambda g0, g1, k: (k, ij(g0, g1)[0]))
        b_spec = pl.BlockSpec((tk, tn), lambda g0, g1, k: (k, ij(g0, g1)[1]))
        dot = _dot_tn
    o_spec = pl.BlockSpec((tm, tn), lambda g0, g1, k: ij(g0, g1))
    has_acc = acc_in is not None
    n_side = len(side[0]) if side is not None else 0
    scatter = side[1] if side is not None else False
    n_in = 2 + int(has_acc)

    def body(*refs):
        a_ref, b_ref = refs[0], refs[1]
        c_ref = refs[2] if has_acc else None
        srcs = refs[n_in:n_in + n_side]
        o_ref = refs[n_in + n_side]
        dsts = refs[n_in + n_side + 1:n_in + 2 * n_side + 1]
        scratch = refs[n_in + 2 * n_side + 1:]
        k = pl.program_id(2)
        if n_side:
            sems = scratch[-3:]
            first = (pl.program_id(0) == 0) & (pl.program_id(1) == 0) & (k == 0)
            last = (pl.program_id(0) == grid[0] - 1) & (pl.program_id(1) == grid[1] - 1) & (k == nk - 1)

            @pl.when(first)
            def _():
                for cp in _xchg_copies(srcs, dsts, *sems, scatter):
                    cp.start()

        p = dot(a_ref[...], b_ref[...])

        def finish(r):
            if has_acc:
                r = r + c_ref[...].astype(F32)
            o_ref[...] = r.astype(out_dtype)

        if nk == 1:
            finish(p)
        else:
            acc = scratch[0]

            @pl.when(k == 0)
            def _():
                acc[...] = p

            @pl.when((k > 0) & (k < nk - 1))
            def _():
                acc[...] += p

            @pl.when(k == nk - 1)
            def _():
                finish(acc[...] + p)

        if n_side:
            @pl.when(last)
            def _():
                for cp in _xchg_copies(srcs, dsts, *sems, scatter):
                    cp.wait()

    hbm = pl.BlockSpec(memory_space=pltpu.HBM)
    in_specs = [a_spec, b_spec]
    args = [a, b]
    if has_acc:
        in_specs.append(o_spec)
        args.append(acc_in)
    out_specs = [o_spec]
    out_shape = [jax.ShapeDtypeStruct((M, N), out_dtype)]
    scratch_shapes = [pltpu.VMEM((tm, tn), F32)] if nk > 1 else []
    if n_side:
        in_specs += [hbm] * n_side
        args += list(side[0])
        out_specs += [hbm] * n_side
        out_shape += _xchg_out_shapes(side[0], scatter)
        scratch_shapes += _xchg_sems(n_side)
        cp = pltpu.CompilerParams(vmem_limit_bytes=VMEM_LIMIT, has_side_effects=True,
                                  dimension_semantics=("arbitrary", "arbitrary", "arbitrary"))
    else:
        cp = _cparams(("parallel", "parallel", "arbitrary"))
    res = pl.pallas_call(
        body, name=name, grid=grid, in_specs=in_specs, out_specs=out_specs, out_shape=out_shape,
        scratch_shapes=scratch_shapes, compiler_params=cp,
    )(*args)
    return (res[0], list(res[1:])) if n_side else res[0]


def _rmsnorm_fwd(x, g, name):
    S, D = x.shape
    T = min(TOK_TILE, S)

    def body(x_ref, g_ref, h_ref, r_ref):
        xv = x_ref[...]
        r = lax.rsqrt(jnp.mean(xv * xv, axis=-1, keepdims=True) + EPS)
        h_ref[...] = (xv * r * g_ref[...]).astype(BF16)
        r_ref[...] = r

    return pl.pallas_call(
        body, name=name, grid=(S // T,),
        in_specs=[pl.BlockSpec((T, D), lambda i: (i, 0)), pl.BlockSpec((1, D), lambda i: (0, 0))],
        out_specs=[pl.BlockSpec((T, D), lambda i: (i, 0)), pl.BlockSpec((T, 1), lambda i: (i, 0))],
        out_shape=[jax.ShapeDtypeStruct((S, D), BF16), jax.ShapeDtypeStruct((S, 1), F32)],
        compiler_params=_cparams(("parallel",)),
    )(x, g)


LANE = 128
ROW_BLK = 64


def _shift_strip(src, sh, cs, nr, residues=range(1, 8)):
    for r in residues:
        sh[r] = src[pl.ds(r, nr), cs]


def _tap(src, sh, cs, o, r0):
    r = o % 8
    if r == 0:
        return src[pl.ds(o + r0, ROW_BLK), cs]
    return sh[r, pl.ds(o - r + r0, ROW_BLK), :]


def _conv_fwd(z, w_dw, b_dw, g_ln, b_ln, D):
    S = z.shape[0]
    T = min(TOK_TILE, S)
    hb = T // HALO

    def body(a_ref, b_ref, zc_ref, ah_ref, bh_ref, w_ref, bdw_ref, gln_ref, bln_ref, u1_ref, pc_ref, u0s, sh):
        i = pl.program_id(0)
        a = a_ref[...].astype(F32)
        b = b_ref[...].astype(F32)
        u0s[pl.ds(HALO, T), :] = a * _sig(b)
        halo = ah_ref[...].astype(F32) * _sig(bh_ref[...].astype(F32))
        u0s[pl.ds(0, HALO), :] = jnp.where(i > 0, halo, 0.0)
        u0s[pl.ds(T + HALO, 8), :] = jnp.zeros((8, D), F32)
        for c0 in range(0, D, LANE):
            cs = pl.ds(c0, LANE)
            _shift_strip(u0s, sh, cs, T + HALO)
            for r0 in range(0, T, ROW_BLK):
                acc = jnp.broadcast_to(bdw_ref[:, cs], (ROW_BLK, LANE))
                for j in range(CONV_W):
                    acc = acc + w_ref[pl.ds(j, 1), cs] * _tap(u0s, sh, cs, HALO - (CONV_W - 1) + j, r0)
                u1_ref[pl.ds(r0, ROW_BLK), cs] = acc
        acc = u1_ref[...]
        mu = jnp.mean(acc, axis=-1, keepdims=True)
        xc = acc - mu
        var = jnp.mean(xc * xc, axis=-1, keepdims=True)
        ln = xc * lax.rsqrt(var + EPS) * gln_ref[...] + bln_ref[...]
        pc_ref[...] = (_silu(ln) * _silu(zc_ref[...].astype(F32))).astype(BF16)

    prev = lambda i: (jnp.maximum(i * hb - 1, 0), 0)
    prev_b = lambda i: (jnp.maximum(i * hb - 1, 0), D // D)
    vec = pl.BlockSpec((1, D), lambda i: (0, 0))
    return pl.pallas_call(
        body, name="conv_fwd", grid=(S // T,),
        in_specs=[pl.BlockSpec((T, D), lambda i: (i, 0)), pl.BlockSpec((T, D), lambda i: (i, 1)),
                  pl.BlockSpec((T, D), lambda i: (i, 2)),
                  pl.BlockSpec((HALO, D), prev), pl.BlockSpec((HALO, D), prev_b),
                  pl.BlockSpec((32, D), lambda i: (0, 0)), vec, vec, vec],
        out_specs=[pl.BlockSpec((T, D), lambda i: (i, 0)), pl.BlockSpec((T, D), lambda i: (i, 0))],
        out_shape=[jax.ShapeDtypeStruct((S, D), F32), jax.ShapeDtypeStruct((S, D), BF16)],
        scratch_shapes=[pltpu.VMEM((T + HALO + 8, D), F32), pltpu.VMEM((8, T + HALO, LANE), F32)],
        compiler_params=_cparams(("parallel",)),
    )(z, z, z, z, z, w_dw, b_dw, g_ln, b_ln)


_QK_RES = sorted({(HALO - (QK_W - 1) + j) % 8 for j in range(QK_W)} - {0})


def _qk_fill(ps, p0_ref, p1_ref, h0_ref, h1_ref, T, D):
    i = pl.program_id(0)
    ps[pl.ds(HALO, T), pl.ds(0, D)] = p0_ref[...].astype(F32)
    ps[pl.ds(HALO, T), pl.ds(D, D)] = p1_ref[...].astype(F32)
    ps[pl.ds(0, HALO), pl.ds(0, D)] = jnp.where(i > 0, h0_ref[...].astype(F32), 0.0)
    ps[pl.ds(0, HALO), pl.ds(D, D)] = jnp.where(i > 0, h1_ref[...].astype(F32), 0.0)
    ps[pl.ds(T + HALO, 8), :] = jnp.zeros((8, 2 * D), F32)


def _qk_conv(ps, sh, w_ref, cs, r0):
    acc = jnp.zeros((ROW_BLK, LANE), F32)
    for j in range(QK_W):
        acc = acc + w_ref[pl.ds(j, 1), cs] * _tap(ps, sh, cs, HALO - (QK_W - 1) + j, r0)
    return acc


def _qk_fwd(z, w_qk, D):
    S = z.shape[0]
    T = min(TOK_TILE, S)
    hb = T // HALO
    scale = (D // N_HEADS) ** -0.5
    W2 = 2 * D

    def body(p0_ref, p1_ref, h0_ref, h1_ref, w_ref, o_ref, ps, sh):
        _qk_fill(ps, p0_ref, p1_ref, h0_ref, h1_ref, T, D)
        for c0 in range(0, W2, LANE):
            cs = pl.ds(c0, LANE)
            _shift_strip(ps, sh, cs, T + HALO, _QK_RES)
            for r0 in range(0, T, ROW_BLK):
                qk = _silu(_qk_conv(ps, sh, w_ref, cs, r0))
                o_ref[pl.ds(r0, ROW_BLK), cs] = (qk * scale if c0 >= D else qk).astype(BF16)

    prev0 = lambda i: (jnp.maximum(i * hb - 1, 0), 3)
    prev1 = lambda i: (jnp.maximum(i * hb - 1, 0), 4)
    return pl.pallas_call(
        body, name="qk_fwd", grid=(S // T,),
        in_specs=[pl.BlockSpec((T, D), lambda i: (i, 3)), pl.BlockSpec((T, D), lambda i: (i, 4)),
                  pl.BlockSpec((HALO, D), prev0), pl.BlockSpec((HALO, D), prev1),
                  pl.BlockSpec((8, W2), lambda i: (0, 0))],
        out_specs=pl.BlockSpec((T, W2), lambda i: (i, 0)),
        out_shape=jax.ShapeDtypeStruct((S, W2), BF16),
        scratch_shapes=[pltpu.VMEM((T + HALO + 8, W2), F32), pltpu.VMEM((8, T + HALO, LANE), F32)],
        compiler_params=_cparams(("parallel",)),
    )(z, z, z, z, w_qk)


def _ml_gates(gc_ref, gr_ref, bifr_ref, bifc_ref, L):
    gc = gc_ref[...] + bifr_ref[...]
    gr = gr_ref[...] + bifc_ref[...]
    ii = lax.broadcasted_iota(jnp.int32, (L, L), 0)
    jj = lax.broadcasted_iota(jnp.int32, (L, L), 1)
    tri = (jj <= ii).astype(BF16)
    triu = (ii <= jj).astype(BF16)
    bcol_all = _tri_left(tri, _logsig(gc))
    brow_all = _tri_right(_logsig(gr), triu)
    return gc, gr, bcol_all, brow_all, ii, jj


def _ml_intra(q, k, b_c, b_r, li_r, m, n_row, ii, jj):
    d = b_c - b_r + li_r
    dm = jnp.where(jj <= ii, d, NEG)
    inter = b_c + m
    m_row = jnp.maximum(inter, jnp.max(dm, axis=1, keepdims=True))
    w_intra = jnp.exp(dm - m_row)
    w_inter = jnp.exp(inter - m_row)
    qk = _dot_nt(q, k)
    s = qk * w_intra
    qn = jnp.sum(q.astype(F32) * n_row, axis=-1, keepdims=True)
    den = jnp.sum(s, axis=-1, keepdims=True) + w_inter * qn
    hdiv = jnp.maximum(jnp.abs(den), jnp.exp(-m_row))
    return qk, w_intra, w_inter, s, qn, den, hdiv, m_row


def _ml_state(b_c, li_c, m, L):
    b_last = b_c[L - 1:L, :]
    g_c = b_last - b_c + li_c
    m_new = jnp.maximum(b_last + m, jnp.max(g_c, axis=0, keepdims=True))
    decay = jnp.exp(b_last + m - m_new)
    wk = jnp.exp(g_c - m_new)
    return m_new, decay, wk


def _mlstm_fwd(qks, z, gcol, grow, bif_row, bif_col, D):
    S = qks.shape[0]
    L = min(ML_CHUNK, S)
    NC = S // L
    H = N_HEADS
    dh = D // H

    def body(q_ref, k_ref, v_ref, gc_ref, gr_ref, bifr_ref, bifc_ref, hm_ref, cst_ref, nst_ref, C, NM):
        c = pl.program_id(0)

        @pl.when(c == 0)
        def _():
            C[...] = jnp.zeros_like(C)
            NM[...] = jnp.zeros_like(NM)

        gc, gr, bcol_all, brow_all, ii, jj = _ml_gates(gc_ref, gr_ref.at[0], bifr_ref, bifc_ref, L)
        for h in range(H):
            hs = pl.ds(h * dh, dh)
            q = q_ref[:, hs]
            k = k_ref[:, hs]
            v = v_ref[:, hs]
            b_c = bcol_all[:, H + h:H + h + 1]
            li_c = gc[:, h:h + 1]
            b_r = brow_all[H + h:H + h + 1, :]
            li_r = gr[h:h + 1, :]
            n_row = NM[h, 0:1, :]
            m = NM[h, 1:2, 0:1]
            Cb = C[h].astype(BF16)
            cst_ref[0, h] = Cb
            nst_ref[0, h] = NM[h]
            qk, w_intra, w_inter, s, qn, den, hdiv, m_row = _ml_intra(q, k, b_c, b_r, li_r, m, n_row, ii, jj)
            num = _dot_nn(s.astype(BF16), v) + w_inter * _dot_nn(q, Cb)
            hm_ref[:, hs] = num / hdiv
            m_new, decay, wk = _ml_state(b_c, li_c, m, L)
            wkk = wk * k.astype(F32)
            C[h] = decay * C[h] + _dot_tn(wkk.astype(BF16), v)
            NM[h, 0:1, :] = decay * n_row + jnp.sum(wkk, axis=0, keepdims=True)
            NM[h, 1:2, :] = jnp.broadcast_to(m_new, (1, dh))

    return pl.pallas_call(
        body, name="mlstm_fwd", grid=(NC,),
        in_specs=[pl.BlockSpec((L, D), lambda c: (c, 0)), pl.BlockSpec((L, D), lambda c: (c, 1)),
                  pl.BlockSpec((L, D), lambda c: (c, 5)),
                  pl.BlockSpec((L, IF_PAD), lambda c: (c, 0)),
                  pl.BlockSpec((1, 16, L), lambda c: (c, 0, 0)),
                  pl.BlockSpec((1, IF_PAD), lambda c: (0, 0)), pl.BlockSpec((16, 1), lambda c: (0, 0))],
        out_specs=[pl.BlockSpec((L, D), lambda c: (c, 0)),
                   pl.BlockSpec((1, H, dh, dh), lambda c: (c, 0, 0, 0)),
                   pl.BlockSpec((1, H, 8, dh), lambda c: (c, 0, 0, 0))],
        out_shape=[jax.ShapeDtypeStruct((S, D), F32), jax.ShapeDtypeStruct((NC, H, dh, dh), BF16),
                   jax.ShapeDtypeStruct((NC, H, 8, dh), F32)],
        scratch_shapes=[pltpu.VMEM((H, dh, dh), F32), pltpu.VMEM((H, 8, dh), F32)],
        compiler_params=_cparams(("arbitrary",)),
    )(qks, qks, z, gcol, grow, bif_row, bif_col)


def _mlstm_bwd(qks, z, gcol, grow, bif_row, bif_col, hm, dhm, cst, nst, dz, D):
    S = qks.shape[0]
    L = min(ML_CHUNK, S)
    NC = S // L
    H = N_HEADS
    dh = D // H

    def body(q_ref, k_ref, v_ref, gc_ref, gr_ref, bifr_ref, bifc_ref, hm_ref, dhm_ref, cst_ref, nst_ref, dz_any,
             dqk_ref, dg_ref, dv_ref, dC, dN):
        del dz_any
        c = pl.program_id(0)

        @pl.when(c == 0)
        def _():
            dC[...] = jnp.zeros_like(dC)
            dN[...] = jnp.zeros_like(dN)

        gc, gr, bcol_all, brow_all, ii, jj = _ml_gates(gc_ref, gr_ref.at[0], bifr_ref, bifc_ref, L)
        eye = ii == jj
        lane = lax.broadcasted_iota(jnp.int32, (L, IF_PAD), 1)
        db_all = jnp.zeros((L, IF_PAD), F32)
        dli_all = jnp.zeros((L, IF_PAD), F32)
        last_row = lax.broadcasted_iota(jnp.int32, (L, 1), 0) == L - 1
        for h in range(H):
            hs = pl.ds(h * dh, dh)
            q = q_ref[:, hs]
            k = k_ref[:, hs]
            v = v_ref[:, hs]
            qf = q.astype(F32)
            kf = k.astype(F32)
            b_c = bcol_all[:, H + h:H + h + 1]
            li_c = gc[:, h:h + 1]
            b_r = brow_all[H + h:H + h + 1, :]
            li_r = gr[h:h + 1, :]
            n_row = nst_ref[0, h, 0:1, :]
            m = nst_ref[0, h, 1:2, 0:1]
            Cb = cst_ref[0, h]
            qk, w_intra, w_inter, s, qn, den, hdiv, m_row = _ml_intra(q, k, b_c, b_r, li_r, m, n_row, ii, jj)
            dh_ = dhm_ref[:, hs]
            hh = hm_ref[:, hs]
            dnum = dh_ / hdiv
            dhdiv = -jnp.sum(dh_ * hh, axis=-1, keepdims=True) / hdiv
            dden = jnp.where(jnp.abs(den) > jnp.exp(-m_row), dhdiv * jnp.sign(den), 0.0)
            dnum_b = dnum.astype(BF16)
            ds = _dot_nt(dnum_b, v) + dden
            s_b = s.astype(BF16)
            dv = _dot_tn(s_b, dnum_b)
            dnC = _dot_nt(dnum_b, Cb)
            dwi = dden * w_inter
            dw_inter = jnp.sum(qf * dnC, axis=-1, keepdims=True) + dden * qn
            dq = w_inter * dnC + dwi * n_row
            dC_out = _dot_tn((w_inter * qf).astype(BF16), dnum_b)
            dn_out = jnp.sum(dwi * qf, axis=0, keepdims=True)
            dqk = (ds * w_intra).astype(BF16)
            dq = dq + _dot_nn(dqk, k)
            dk = _dot_tn(dqk, q)
            dd = ds * s
            rs_c = jnp.sum(dd, axis=1, keepdims=True)
            cs_r = jnp.sum(dd, axis=0, keepdims=True)
            cs_c = jnp.sum(jnp.where(eye, cs_r, 0.0), axis=1, keepdims=True)
            dinter = dw_inter * w_inter
            m_new, decay, wk = _ml_state(b_c, li_c, m, L)
            dCn = dC[h]
            dCn_b = dCn.astype(BF16)
            dn_new = dN[h, 0:1, :]
            ddecay = (jnp.sum(jnp.sum(dCn * Cb.astype(F32), axis=1, keepdims=True), axis=0, keepdims=True)
                      + jnp.sum(dn_new * n_row, axis=1, keepdims=True))
            dwkk = _dot_nt(v, dCn_b) + dn_new
            wkk_b = (wk * kf).astype(BF16)
            dv = dv + _dot_nn(wkk_b, dCn_b)
            dk = dk + wk * dwkk
            dg = jnp.sum(dwkk * kf, axis=-1, keepdims=True) * wk
            db_last = ddecay * decay + jnp.sum(dg, axis=0, keepdims=True)
            dC[h] = decay * dCn + dC_out
            dN[h, 0:1, :] = decay * dn_new + dn_out
            db_c = rs_c - cs_c + dinter - dg + jnp.where(last_row, db_last, 0.0)
            dli_c = cs_c + dg
            db_all = jnp.where(lane == H + h, db_c, db_all)
            dli_all = jnp.where(lane == h, dli_c, dli_all)
            dqk_ref[:, hs] = dq.astype(BF16)
            dqk_ref[:, pl.ds(D + h * dh, dh)] = dk.astype(BF16)
            dv_ref[:, hs] = dv.astype(BF16)
        triu = (ii <= jj).astype(BF16)
        dlf = _tri_left(triu, db_all)
        dg_ref[...] = dli_all + dlf * (1.0 - _sig(gc))

    r = lambda c: NC - 1 - c
    n_in = 12
    return pl.pallas_call(
        body, name="mlstm_bwd", grid=(NC,),
        in_specs=[pl.BlockSpec((L, D), lambda c: (r(c), 0)), pl.BlockSpec((L, D), lambda c: (r(c), 1)),
                  pl.BlockSpec((L, D), lambda c: (r(c), 5)),
                  pl.BlockSpec((L, IF_PAD), lambda c: (r(c), 0)),
                  pl.BlockSpec((1, 16, L), lambda c: (r(c), 0, 0)),
                  pl.BlockSpec((1, IF_PAD), lambda c: (0, 0)), pl.BlockSpec((16, 1), lambda c: (0, 0)),
                  pl.BlockSpec((L, D), lambda c: (r(c), 0)), pl.BlockSpec((L, D), lambda c: (r(c), 0)),
                  pl.BlockSpec((1, H, dh, dh), lambda c: (r(c), 0, 0, 0)),
                  pl.BlockSpec((1, H, 8, dh), lambda c: (r(c), 0, 0, 0)),
                  pl.BlockSpec(memory_space=pl.ANY)],
        out_specs=[pl.BlockSpec((L, 2 * D), lambda c: (r(c), 0)),
                   pl.BlockSpec((L, IF_PAD), lambda c: (r(c), 0)),
                   pl.BlockSpec((L, D), lambda c: (r(c), 5))],
        out_shape=[jax.ShapeDtypeStruct((S, 2 * D), BF16),
                   jax.ShapeDtypeStruct((S, IF_PAD), F32), jax.ShapeDtypeStruct(dz.shape, dz.dtype)],
        input_output_aliases={n_in - 1: 2},
        scratch_shapes=[pltpu.VMEM((H, dh, dh), F32), pltpu.VMEM((H, 8, dh), F32)],
        compiler_params=_cparams(("arbitrary",)),
    )(qks, qks, z, gcol, grow, bif_row, bif_col, hm, dhm, cst, nst, dz)


def _ml_post_fwd(hm, z, g_head, D):
    S = hm.shape[0]
    T = min(TOK_TILE, S)
    dh = D // N_HEADS

    def body(hm_ref, o_ref, zm_ref, g_ref, pm_ref):
        for h in range(N_HEADS):
            hs = pl.ds(h * dh, dh)
            hg = _sig(o_ref[:, hs].astype(F32)) * hm_ref[:, hs]
            rs = lax.rsqrt(jnp.mean(hg * hg, axis=-1, keepdims=True) + EPS)
            pm_ref[:, hs] = (hg * rs * g_ref[:, hs] * _silu(zm_ref[:, hs].astype(F32))).astype(BF16)

    return pl.pallas_call(
        body, name="ml_post_fwd", grid=(S // T,),
        in_specs=[pl.BlockSpec((T, D), lambda i: (i, 0)), pl.BlockSpec((T, D), lambda i: (i, 6)),
                  pl.BlockSpec((T, D), lambda i: (i, 7)), pl.BlockSpec((1, D), lambda i: (0, 0))],
        out_specs=pl.BlockSpec((T, D), lambda i: (i, 0)),
        out_shape=jax.ShapeDtypeStruct((S, D), BF16),
        compiler_params=_cparams(("parallel",)),
    )(hm, z, z, g_head)


def _ml_post_bwd(dpm, hm, z, g_head, dz, D):
    S = hm.shape[0]
    T = min(TOK_TILE, S)
    dh = D // N_HEADS

    def body(dpm_ref, hm_ref, o_ref, zm_ref, g_ref, dz_any, dhm_ref, dg_ref, dozm_ref):
        del dz_any
        do_ref = dozm_ref.at[:, pl.ds(0, D)]
        dzm_ref = dozm_ref.at[:, pl.ds(D, D)]
        i = pl.program_id(0)

        @pl.when(i == 0)
        def _():
            dg_ref[...] = jnp.zeros_like(dg_ref)

        for h in range(N_HEADS):
            hs = pl.ds(h * dh, dh)
            o = o_ref[:, hs].astype(F32)
            zm = zm_ref[:, hs].astype(F32)
            hmv = hm_ref[:, hs]
            dpm_ = dpm_ref[:, hs].astype(F32)
            g = g_ref[:, hs]
            so = _sig(o)
            hg = so * hmv
            rs = lax.rsqrt(jnp.mean(hg * hg, axis=-1, keepdims=True) + EPS)
            xn = hg * rs
            dzm_ref[:, hs] = (dpm_ * xn * g * _dsilu(zm)).astype(BF16)
            dhn = dpm_ * _silu(zm)
            dg_ref[0:1, hs] += jnp.sum(dhn * xn, axis=0, keepdims=True)
            dxn = dhn * g
            dhg = rs * (dxn - xn * jnp.mean(dxn * xn, axis=-1, keepdims=True))
            do_ref[:, hs] = (dhg * hmv * so * (1.0 - so)).astype(BF16)
            dhm_ref[:, hs] = dhg * so

    return pl.pallas_call(
        body, name="ml_post_bwd", grid=(S // T,),
        in_specs=[pl.BlockSpec((T, D), lambda i: (i, 0)), pl.BlockSpec((T, D), lambda i: (i, 0)),
                  pl.BlockSpec((T, D), lambda i: (i, 6)), pl.BlockSpec((T, D), lambda i: (i, 7)),
                  pl.BlockSpec((1, D), lambda i: (0, 0)), pl.BlockSpec(memory_space=pl.ANY)],
        out_specs=[pl.BlockSpec((T, D), lambda i: (i, 0)), pl.BlockSpec((8, D), lambda i: (0, 0)),
                   pl.BlockSpec((T, 2 * D), lambda i: (i, 3))],
        out_shape=[jax.ShapeDtypeStruct((S, D), F32), jax.ShapeDtypeStruct((8, D), F32),
                   jax.ShapeDtypeStruct(dz.shape, dz.dtype)],
        input_output_aliases={5: 2},
        compiler_params=_cparams(("arbitrary",)),
    )(dpm, hm, z, z, g_head, dz)


def _xattn_probs(q, km_h, scale):
    s = _dot_nt(q, km_h) * scale
    e = jnp.exp(s - jnp.max(s, axis=-1, keepdims=True))
    return e / jnp.sum(e, axis=-1, keepdims=True)


def _xattn_fwd(z, kv, D):
    S = z.shape[0]
    M = kv.shape[0]
    T = min(TOK_TILE, S)
    dh = D // N_HEADS
    scale = dh ** -0.5

    def body(q_ref, zx_ref, km_ref, vm_ref, px_ref):
        for h in range(N_HEADS):
            hs = pl.ds(h * dh, dh)
            p = _xattn_probs(q_ref[:, hs], km_ref[:, hs], scale)
            o = _dot_nn(p.astype(BF16), vm_ref[:, hs])
            px_ref[:, hs] = (o * _silu(zx_ref[:, hs].astype(F32))).astype(BF16)

    return pl.pallas_call(
        body, name="xattn_fwd", grid=(S // T,),
        in_specs=[pl.BlockSpec((T, D), lambda i: (i, 8)), pl.BlockSpec((T, D), lambda i: (i, 9)),
                  pl.BlockSpec((M, D), lambda i: (0, 0)), pl.BlockSpec((M, D), lambda i: (0, 1))],
        out_specs=pl.BlockSpec((T, D), lambda i: (i, 0)),
        out_shape=jax.ShapeDtypeStruct((S, D), BF16),
        compiler_params=_cparams(("parallel",)),
    )(z, z, kv, kv)


def _xattn_bwd(dpx, z, kv, dz, D):
    S = z.shape[0]
    M = kv.shape[0]
    T = min(TOK_TILE, S)
    dh = D // N_HEADS
    scale = dh ** -0.5

    def body(dpx_ref, q_ref, zx_ref, km_ref, vm_ref, dz_any, dkv_ref, dqz_ref):
        del dz_any
        i = pl.program_id(0)

        @pl.when(i == 0)
        def _():
            dkv_ref[...] = jnp.zeros_like(dkv_ref)

        for h in range(N_HEADS):
            hs = pl.ds(h * dh, dh)
            q = q_ref[:, hs]
            zx = zx_ref[:, hs].astype(F32)
            dpx_ = dpx_ref[:, hs].astype(F32)
            p = _xattn_probs(q, km_ref[:, hs], scale)
            p_b = p.astype(BF16)
            o = _dot_nn(p_b, vm_ref[:, hs])
            dqz_ref[:, pl.ds(D + h * dh, dh)] = (dpx_ * o * _dsilu(zx)).astype(BF16)
            do = (dpx_ * _silu(zx)).astype(BF16)
            dp = _dot_nt(do, vm_ref[:, hs])
            dsc = (p * (dp - jnp.sum(p * dp, axis=-1, keepdims=True)) * scale).astype(BF16)
            dqz_ref[:, hs] = _dot_nn(dsc, km_ref[:, hs]).astype(BF16)
            dkv_ref[:, hs] += _dot_tn(dsc, q)
            dkv_ref[:, pl.ds(D + h * dh, dh)] += _dot_tn(p_b, do)

    return pl.pallas_call(
        body, name="xattn_bwd", grid=(S // T,),
        in_specs=[pl.BlockSpec((T, D), lambda i: (i, 0)),
                  pl.BlockSpec((T, D), lambda i: (i, 8)), pl.BlockSpec((T, D), lambda i: (i, 9)),
                  pl.BlockSpec((M, D), lambda i: (0, 0)), pl.BlockSpec((M, D), lambda i: (0, 1)),
                  pl.BlockSpec(memory_space=pl.ANY)],
        out_specs=[pl.BlockSpec((M, 2 * D), lambda i: (0, 0)), pl.BlockSpec((T, 2 * D), lambda i: (i, 4))],
        out_shape=[jax.ShapeDtypeStruct((M, 2 * D), F32), jax.ShapeDtypeStruct(dz.shape, dz.dtype)],
        input_output_aliases={5: 1},
        compiler_params=_cparams(("arbitrary",)),
    )(dpx, z, z, kv, kv, dz)


def _col_sum(a):
    S, C = a.shape
    T = min(1024, S)

    def body(a_ref, o_ref):
        @pl.when(pl.program_id(0) == 0)
        def _():
            o_ref[...] = jnp.zeros_like(o_ref)

        o_ref[0:1, :] += jnp.sum(a_ref[...], axis=0, keepdims=True)

    return pl.pallas_call(
        body, name="col_sum", grid=(S // T,),
        in_specs=[pl.BlockSpec((T, C), lambda i: (i, 0))], out_specs=pl.BlockSpec((8, C), lambda i: (0, 0)),
        out_shape=jax.ShapeDtypeStruct((8, C), F32), compiler_params=_cparams(("arbitrary",)),
    )(a)


def _gain_grad(dn, xin, rstd):
    R, D = dn.shape

    def body(dn_ref, x_ref, r_ref, o_ref):
        o_ref[...] = jnp.zeros_like(o_ref)
        o_ref[0:1, :] = jnp.sum(dn_ref[...] * x_ref[...] * r_ref[...], axis=0, keepdims=True)

    return pl.pallas_call(
        body, name="gain_grad", out_shape=jax.ShapeDtypeStruct((8, D), F32),
        compiler_params=_cparams(),
    )(dn, xin, rstd)


def _merge_fwd(z, yc, ym, yx, D):
    S = z.shape[0]
    T = min(TOK_TILE, S)

    def body(g0, g1, g2, yc_ref, ym_ref, yx_ref, o_ref):
        o_ref[...] = (_sig(g0[...].astype(F32)) * yc_ref[...].astype(F32)
                      + _sig(g1[...].astype(F32)) * ym_ref[...].astype(F32)
                      + _sig(g2[...].astype(F32)) * yx_ref[...].astype(F32)).astype(BF16)

    tok = pl.BlockSpec((T, D), lambda i: (i, 0))
    return pl.pallas_call(
        body, name="merge_fwd", grid=(S // T,),
        in_specs=[pl.BlockSpec((T, D), lambda i: (i, 10)), pl.BlockSpec((T, D), lambda i: (i, 11)),
                  pl.BlockSpec((T, D), lambda i: (i, 12)), tok, tok, tok],
        out_specs=tok, out_shape=jax.ShapeDtypeStruct((S, D), BF16),
        compiler_params=_cparams(("parallel",)),
    )(z, z, z, yc, ym, yx)


def _merge_bwd(dmg, z, yc, ym, yx, D):
    S = z.shape[0]
    T = min(TOK_TILE // 2, S)

    def body(dm_ref, g_ref, yc_ref, ym_ref, yx_ref, dyc_ref, dym_ref, dyx_ref, dg_ref):
        j = pl.program_id(1)
        for jj, (y_ref, dy_ref) in enumerate(((yc_ref, dyc_ref), (ym_ref, dym_ref), (yx_ref, dyx_ref))):
            @pl.when(j == jj)
            def _():
                sg = _sig(g_ref[...].astype(F32))
                dm = dm_ref[...].astype(F32)
                dy_ref[...] = (dm * sg).astype(BF16)
                dg_ref[...] = (dm * y_ref[...].astype(F32) * sg * (1.0 - sg)).astype(BF16)

    tok = pl.BlockSpec((T, D), lambda i, j: (i, 0))
    return pl.pallas_call(
        body, name="merge_bwd", grid=(S // T, 3),
        in_specs=[tok, pl.BlockSpec((T, D), lambda i, j: (i, 10 + j)), tok, tok, tok],
        out_specs=[tok, tok, tok, pl.BlockSpec((T, D), lambda i, j: (i, 10 + j))],
        out_shape=[jax.ShapeDtypeStruct((S, D), BF16)] * 3 + [jax.ShapeDtypeStruct((S, 13 * D), BF16)],
        compiler_params=_cparams(("arbitrary", "arbitrary")),
    )(dmg, z, yc, ym, yx)


def _post_loss(r, x, tgt, g_post):
    S, D = r.shape
    T = min(TOK_TILE, S)

    def body(r_ref, x_ref, t_ref, g_ref, dy_ref, dr_ref, dg_ref, loss_ref):
        i = pl.program_id(0)

        @pl.when(i == 0)
        def _():
            dg_ref[...] = jnp.zeros_like(dg_ref)
            loss_ref[...] = jnp.zeros_like(loss_ref)

        rv = r_ref[...]
        g = g_ref[...]
        rs = lax.rsqrt(jnp.mean(rv * rv, axis=-1, keepdims=True) + EPS)
        nrm = rv * rs
        e = x_ref[...] + nrm * g - t_ref[...]
        part = jnp.sum(jnp.sum(e * e, axis=-1, keepdims=True), axis=0, keepdims=True) * (0.5 / D)
        loss_ref[0:1, 0:1] += part
        dy = e * (1.0 / D)
        dy_ref[...] = dy
        dg_ref[0:1, :] += jnp.sum(dy * nrm, axis=0, keepdims=True)
        dn = dy * g
        dr_ref[...] = (rs * (dn - nrm * jnp.mean(dn * nrm, axis=-1, keepdims=True))).astype(BF16)

    tok = pl.BlockSpec((T, D), lambda i: (i, 0))
    return pl.pallas_call(
        body, name="post_loss", grid=(S // T,),
        in_specs=[tok, tok, tok, pl.BlockSpec((1, D), lambda i: (0, 0))],
        out_specs=[tok, tok, pl.BlockSpec((8, D), lambda i: (0, 0)), pl.BlockSpec((8, 128), lambda i: (0, 0))],
        out_shape=[jax.ShapeDtypeStruct((S, D), F32), jax.ShapeDtypeStruct((S, D), BF16),
                   jax.ShapeDtypeStruct((8, D), F32), jax.ShapeDtypeStruct((8, 128), F32)],
        compiler_params=_cparams(("arbitrary",)),
    )(r, x, tgt, g_post)


def _conv_bwd1(dpc, z, u1, g_ln, b_ln, dz, D):
    S = z.shape[0]
    T = min(TOK_TILE // 2, S)

    def body(dpc_ref, zc_ref, u1_ref, gln_ref, bln_ref, dz_any, du1_ref, st_ref, dzc_ref):
        del dz_any
        i = pl.program_id(0)

        @pl.when(i == 0)
        def _():
            st_ref[...] = jnp.zeros_like(st_ref)

        u1v = u1_ref[...]
        mu = jnp.mean(u1v, axis=-1, keepdims=True)
        xc = u1v - mu
        rs = lax.rsqrt(jnp.mean(xc * xc, axis=-1, keepdims=True) + EPS)
        xh = xc * rs
        g = gln_ref[...]
        ln = xh * g + bln_ref[...]
        zc = zc_ref[...].astype(F32)
        dpc_ = dpc_ref[...].astype(F32)
        dzc_ref[...] = (dpc_ * _silu(ln) * _dsilu(zc)).astype(BF16)
        dln = dpc_ * _silu(zc) * _dsilu(ln)
        st_ref[0:1, :] += jnp.sum(dln * xh, axis=0, keepdims=True)
        st_ref[1:2, :] += jnp.sum(dln, axis=0, keepdims=True)
        dxh = dln * g
        du1 = rs * (dxh - jnp.mean(dxh, axis=-1, keepdims=True) - xh * jnp.mean(dxh * xh, axis=-1, keepdims=True))
        du1_ref[...] = du1
        st_ref[2:3, :] += jnp.sum(du1, axis=0, keepdims=True)

    tok = pl.BlockSpec((T, D), lambda i: (i, 0))
    vec = pl.BlockSpec((1, D), lambda i: (0, 0))
    return pl.pallas_call(
        body, name="conv_bwd1", grid=(S // T,),
        in_specs=[tok, pl.BlockSpec((T, D), lambda i: (i, 2)), tok, vec, vec, pl.BlockSpec(memory_space=pl.ANY)],
        out_specs=[tok, pl.BlockSpec((8, D), lambda i: (0, 0)), pl.BlockSpec((T, D), lambda i: (i, 2))],
        out_shape=[jax.ShapeDtypeStruct((S, D), F32), jax.ShapeDtypeStruct((8, D), F32),
                   jax.ShapeDtypeStruct(dz.shape, dz.dtype)],
        input_output_aliases={5: 2},
        compiler_params=_cparams(("arbitrary",)),
    )(dpc, z, u1, g_ln, b_ln, dz)


def _conv_bwd2(du1, z, w_dw, dz, D):
    S = z.shape[0]
    T = min(TOK_TILE, S)
    hb = T // HALO
    last = S // HALO - 1

    n_steps = S // T

    def body(du_ref, dun_ref, a_ref, b_ref, ah_ref, bh_ref, w_ref, dz_any, dw_ref, dab_ref,
             dus, u0s, shd, shu, dwp):
        del dz_any
        i = pl.program_id(0)

        @pl.when(i == 0)
        def _():
            dwp[...] = jnp.zeros_like(dwp)

        dus[pl.ds(0, T), :] = du_ref[...]
        dus[pl.ds(T, HALO), :] = jnp.where(i < n_steps - 1, dun_ref[...], 0.0)
        dus[pl.ds(T + HALO, 8), :] = jnp.zeros((8, D), F32)
        u0s[pl.ds(HALO, T), :] = a_ref[...].astype(F32) * _sig(b_ref[...].astype(F32))
        halo = ah_ref[...].astype(F32) * _sig(bh_ref[...].astype(F32))
        u0s[pl.ds(0, HALO), :] = jnp.where(i > 0, halo, 0.0)
        u0s[pl.ds(T + HALO, 8), :] = jnp.zeros((8, D), F32)
        for c0 in range(0, D, LANE):
            cs = pl.ds(c0, LANE)
            _shift_strip(dus, shd, cs, T + HALO)
            _shift_strip(u0s, shu, cs, T + HALO)
            for r0 in range(0, T, ROW_BLK):
                rows = pl.ds(r0, ROW_BLK)
                du0 = jnp.zeros((ROW_BLK, LANE), F32)
                for j in range(CONV_W):
                    du0 = du0 + w_ref[pl.ds(j, 1), cs] * _tap(dus, shd, cs, CONV_W - 1 - j, r0)
                a = a_ref[rows, cs].astype(F32)
                sb = _sig(b_ref[rows, cs].astype(F32))
                dab_ref[rows, cs] = (du0 * sb).astype(BF16)
                dab_ref[rows, pl.ds(D + c0, LANE)] = (du0 * a * sb * (1.0 - sb)).astype(BF16)
                d = dus[rows, cs]
                for j in range(CONV_W):
                    prod = d * _tap(u0s, shu, cs, HALO - (CONV_W - 1) + j, r0)
                    part = prod[0:8]
                    for q in range(8, ROW_BLK, 8):
                        part = part + prod[q:q + 8]
                    dwp[pl.ds(8 * j, 8), cs] += part

        @pl.when(i == n_steps - 1)
        def _():
            dw_ref[...] = jnp.zeros_like(dw_ref)
            for j in range(CONV_W):
                dw_ref[pl.ds(j, 1), :] = jnp.sum(dwp[pl.ds(8 * j, 8), :], axis=0, keepdims=True)

    tok = pl.BlockSpec((T, D), lambda i: (i, 0))
    nxt = lambda i: (jnp.minimum((i + 1) * hb, last), 0)
    prev = lambda i: (jnp.maximum(i * hb - 1, 0), 0)
    prev_b = lambda i: (jnp.maximum(i * hb - 1, 0), 1)
    return pl.pallas_call(
        body, name="conv_bwd2", grid=(S // T,),
        in_specs=[tok, pl.BlockSpec((HALO, D), nxt), tok, pl.BlockSpec((T, D), lambda i: (i, 1)),
                  pl.BlockSpec((HALO, D), prev), pl.BlockSpec((HALO, D), prev_b),
                  pl.BlockSpec((32, D), lambda i: (0, 0)), pl.BlockSpec(memory_space=pl.ANY)],
        out_specs=[pl.BlockSpec((32, D), lambda i: (0, 0)), pl.BlockSpec((T, 2 * D), lambda i: (i, 0))],
        out_shape=[jax.ShapeDtypeStruct((32, D), F32), jax.ShapeDtypeStruct(dz.shape, dz.dtype)],
        input_output_aliases={7: 1},
        scratch_shapes=[pltpu.VMEM((T + HALO + 8, D), F32), pltpu.VMEM((T + HALO + 8, D), F32),
                        pltpu.VMEM((8, T + HALO, LANE), F32), pltpu.VMEM((8, T + HALO, LANE), F32),
                        pltpu.VMEM((8 * 32, D), F32)],
        compiler_params=_cparams(("arbitrary",)),
    )(du1, du1, z, z, z, z, w_dw, dz)


def _qk_bwd1(dqks, z, w_qk, D):
    S = z.shape[0]
    T = min(TOK_TILE, S)
    hb = T // HALO
    scale = (D // N_HEADS) ** -0.5
    W2 = 2 * D

    n_steps = S // T

    def body(d_ref, p0_ref, p1_ref, h0_ref, h1_ref, w_ref, dc_ref, dw_ref, ps, sh, dwp):
        i = pl.program_id(0)

        @pl.when(i == 0)
        def _():
            dwp[...] = jnp.zeros_like(dwp)

        _qk_fill(ps, p0_ref, p1_ref, h0_ref, h1_ref, T, D)
        for c0 in range(0, W2, LANE):
            cs = pl.ds(c0, LANE)
            _shift_strip(ps, sh, cs, T + HALO, _QK_RES)
            for r0 in range(0, T, ROW_BLK):
                rows = pl.ds(r0, ROW_BLK)
                dc = d_ref[rows, cs].astype(F32) * _dsilu(_qk_conv(ps, sh, w_ref, cs, r0))
                if c0 >= D:
                    dc = dc * scale
                dc_ref[rows, cs] = dc.astype(BF16)
                for j in range(QK_W):
                    prod = dc * _tap(ps, sh, cs, HALO - (QK_W - 1) + j, r0)
                    part = prod[0:8]
                    for q in range(8, ROW_BLK, 8):
                        part = part + prod[q:q + 8]
                    dwp[pl.ds(8 * j, 8), cs] += part

        @pl.when(i == n_steps - 1)
        def _():
            dw_ref[...] = jnp.zeros_like(dw_ref)
            for j in range(QK_W):
                dw_ref[pl.ds(j, 1), :] = jnp.sum(dwp[pl.ds(8 * j, 8), :], axis=0, keepdims=True)

    prev0 = lambda i: (jnp.maximum(i * hb - 1, 0), 3)
    prev1 = lambda i: (jnp.maximum(i * hb - 1, 0), 4)
    return pl.pallas_call(
        body, name="qk_bwd1", grid=(S // T,),
        in_specs=[pl.BlockSpec((T, W2), lambda i: (i, 0)),
                  pl.BlockSpec((T, D), lambda i: (i, 3)), pl.BlockSpec((T, D), lambda i: (i, 4)),
                  pl.BlockSpec((HALO, D), prev0), pl.BlockSpec((HALO, D), prev1),
                  pl.BlockSpec((8, W2), lambda i: (0, 0))],
        out_specs=[pl.BlockSpec((T, W2), lambda i: (i, 0)), pl.BlockSpec((8, W2), lambda i: (0, 0))],
        out_shape=[jax.ShapeDtypeStruct((S, W2), BF16), jax.ShapeDtypeStruct((8, W2), F32)],
        scratch_shapes=[pltpu.VMEM((T + HALO + 8, W2), F32), pltpu.VMEM((8, T + HALO, LANE), F32),
                        pltpu.VMEM((8 * QK_W, W2), F32)],
        compiler_params=_cparams(("arbitrary",)),
    )(dqks, z, z, z, z, w_qk)


def _qk_bwd2(dc, w_qk, dz, D):
    S = dc.shape[0]
    T = min(TOK_TILE, S)
    hb = T // HALO
    last = S // HALO - 1

    n_steps = S // T

    def body(d_ref, dn_ref, w_ref, dz_any, o_ref, ds, sh):
        del dz_any
        i = pl.program_id(0)
        ds[pl.ds(0, T), :] = d_ref[...].astype(F32)
        ds[pl.ds(T, HALO), :] = jnp.where(i < n_steps - 1, dn_ref[...].astype(F32), 0.0)
        ds[pl.ds(T + HALO, 8), :] = jnp.zeros((8, D), F32)
        for c0 in range(0, D, LANE):
            cs = pl.ds(c0, LANE)
            _shift_strip(ds, sh, cs, T + HALO, range(1, QK_W))
            for r0 in range(0, T, ROW_BLK):
                acc = jnp.zeros((ROW_BLK, LANE), F32)
                for j in range(QK_W):
                    acc = acc + w_ref[pl.ds(j, 1), cs] * _tap(ds, sh, cs, QK_W - 1 - j, r0)
                o_ref[pl.ds(r0, ROW_BLK), cs] = acc.astype(BF16)

    return pl.pallas_call(
        body, name="qk_bwd2", grid=(S // T, 2),
        in_specs=[pl.BlockSpec((T, D), lambda i, j: (i, j)),
                  pl.BlockSpec((HALO, D), lambda i, j: (jnp.minimum((i + 1) * hb, last), j)),
                  pl.BlockSpec((8, D), lambda i, j: (0, j)), pl.BlockSpec(memory_space=pl.ANY)],
        out_specs=pl.BlockSpec((T, D), lambda i, j: (i, 3 + j)),
        out_shape=jax.ShapeDtypeStruct(dz.shape, dz.dtype),
        input_output_aliases={3: 0},
        scratch_shapes=[pltpu.VMEM((T + HALO + 8, D), F32), pltpu.VMEM((8, T + HALO, LANE), F32)],
        compiler_params=_cparams(("arbitrary", "arbitrary")),
    )(dc, dc, w_qk, dz)


TILE_BUDGET = 12 * 1024 * 1024


def _tile_2d(R, C, elem_bytes):
    cands = [t for t in range(BF16_ROWS, R + 1, BF16_ROWS) if R % t == 0 and t * C * elem_bytes <= TILE_BUDGET]
    if cands:
        return max(cands), C
    if R * C * elem_bytes <= TILE_BUDGET or C % LANE:
        return R, C
    cands = [t for t in range(LANE, C + 1, LANE) if C % t == 0 and t * R * elem_bytes <= TILE_BUDGET]
    return R, (max(cands) if cands else LANE)


def _sum_parts(parts, name):
    n, R, C = parts.shape
    tr, tc = _tile_2d(R, C, n * parts.dtype.itemsize + 4)

    def body(p_ref, o_ref):
        g = p_ref[0].astype(F32)
        for s in range(1, n):
            g = g + p_ref[s].astype(F32)
        o_ref[...] = g

    return pl.pallas_call(
        body, name=name, grid=(R // tr, C // tc),
        in_specs=[pl.BlockSpec((n, tr, tc), lambda i, j: (0, i, j))],
        out_specs=pl.BlockSpec((tr, tc), lambda i, j: (i, j)),
        out_shape=jax.ShapeDtypeStruct((R, C), F32),
        compiler_params=_cparams(("parallel", "parallel")),
    )(parts)


def _adamw(parts, w, m, v, name):
    R, C = w.shape
    n = parts.shape[0]
    tr, tc = _tile_2d(R, C, 4 * 7 + n * parts.dtype.itemsize)
    c1 = 1.0 / (1.0 - ADAM_B1 ** ADAM_STEP)
    c2 = 1.0 / (1.0 - ADAM_B2 ** ADAM_STEP)

    def body(p_ref, w_ref, m_ref, v_ref, g_ref, d_ref, nm_ref, nv_ref):
        g = p_ref[0].astype(F32)
        for s in range(1, n):
            g = g + p_ref[s].astype(F32)
        g_ref[...] = g
        mn = ADAM_B1 * m_ref[...] + (1.0 - ADAM_B1) * g
        vn = ADAM_B2 * v_ref[...] + (1.0 - ADAM_B2) * (g * g)
        nm_ref[...] = mn
        nv_ref[...] = vn
        d_ref[...] = -ADAM_LR * ((mn * c1) / (jnp.sqrt(vn * c2) + ADAM_EPS) + ADAM_WD * w_ref[...])

    blk = pl.BlockSpec((tr, tc), lambda i, j: (i, j))
    return pl.pallas_call(
        body, name=name, grid=(R // tr, C // tc),
        in_specs=[pl.BlockSpec((n, tr, tc), lambda i, j: (0, i, j)), blk, blk, blk],
        out_specs=[blk, blk, blk, blk],
        out_shape=[jax.ShapeDtypeStruct((R, C), F32)] * 4,
        compiler_params=_cparams(("parallel", "parallel")),
    )(parts, w, m, v)


def _exchange(operands, scatter, name):
    n = len(operands)

    def body(*refs):
        copies = _xchg_copies(refs[:n], refs[n:2 * n], *refs[2 * n:], scatter)
        for cp in copies:
            cp.start()
        for cp in copies:
            cp.wait()

    hbm = pl.BlockSpec(memory_space=pltpu.HBM)
    return pl.pallas_call(
        body, name=name, in_specs=[hbm] * n, out_specs=[hbm] * n, out_shape=_xchg_out_shapes(operands, scatter),
        scratch_shapes=_xchg_sems(n), compiler_params=pltpu.CompilerParams(has_side_effects=True),
    )(*operands)


N_CHIP = 4


class _CoreGather:
    def __init__(self, core):
        self.core = core

    def out_shape(self, blk):
        return jax.ShapeDtypeStruct((N_CHIP,) + blk.shape, blk.dtype)

    @staticmethod
    def sems():
        return [pltpu.SemaphoreType.DMA((7,)), pltpu.SemaphoreType.DMA((7,)), pltpu.SemaphoreType.DMA]

    def _parts(self, src, out, send_sems, recv_sems, loc_sem):
        x, y, c = lax.axis_index("x"), lax.axis_index("y"), lax.axis_index("c")
        me, sibling = (x, y, c), (x, y, 1 - c)
        chips = [(1 - x, y), (x, 1 - y), (1 - x, 1 - y)]

        def copy(k, chip, to, from_src=False):
            slot = out.at[2 * chip[0] + chip[1]]
            return pltpu.make_async_remote_copy(
                src_ref=src if from_src else slot, dst_ref=slot, send_sem=send_sems.at[k], recv_sem=recv_sems.at[k],
                device_id=to, device_id_type=pl.DeviceIdType.MESH)

        mine = pltpu.make_async_copy(src, out.at[2 * x + y], loc_sem)
        first = [copy(0, (x, y), sibling, True)] + [copy(1 + j, (x, y), (*ch, c), True) for j, ch in enumerate(chips)]
        landed = [copy(1 + j, ch, me) for j, ch in enumerate(chips)]
        passed = [copy(4 + j, ch, sibling) for j, ch in enumerate(chips)]
        from_sibling = [copy(0, (x, y), me)] + [copy(4 + j, ch, me) for j, ch in enumerate(chips)]
        return c == self.core, mine, first, landed, passed, from_sibling

    def start(self, *refs):
        holder, mine, first, _, _, _ = self._parts(*refs)

        @pl.when(holder)
        def _():
            mine.start()
            for cp in first:
                cp.start()

    def forward(self, *refs):
        holder, _, _, landed, passed, _ = self._parts(*refs)

        @pl.when(holder)
        def _():
            for got, cp in zip(landed, passed):
                got.wait_recv()
                cp.start()

    def finish(self, *refs):
        holder, mine, first, _, passed, from_sibling = self._parts(*refs)

        @pl.when(holder)
        def _():
            for cp in first + passed:
                cp.wait_send()
            mine.wait()

        @pl.when(jnp.logical_not(holder))
        def _():
            for cp in from_sibling:
                cp.wait_recv()


def _gather2_copies(srcs, dsts, send_sems, recv_sems, loc_sems):
    x, y, c = lax.axis_index("x"), lax.axis_index("y"), lax.axis_index("c")
    me, sibling = (x, y, c), (x, y, 1 - c)
    chips = [(1 - x, y), (x, 1 - y), (1 - x, 1 - y)]
    mine, first, landed, passed, arriving = [], [], [], [], []
    for t, (src, dst) in enumerate(zip(srcs, dsts)):
        def copy(k, block, to, from_src=False, src=src, dst=dst, t=t):
            slot = dst.at[4 * block[0] + 2 * block[1] + block[2]]
            return pltpu.make_async_remote_copy(
                src_ref=src if from_src else slot, dst_ref=slot,
                send_sem=send_sems.at[7 * t + k], recv_sem=recv_sems.at[7 * t + k],
                device_id=to, device_id_type=pl.DeviceIdType.MESH)

        mine.append(pltpu.make_async_copy(src, dst.at[4 * x + 2 * y + c], loc_sems.at[t]))
        first += [copy(0, me, sibling, True)] + [copy(1 + j, me, (*ch, c), True) for j, ch in enumerate(chips)]
        landed += [copy(1 + j, (*ch, c), me) for j, ch in enumerate(chips)]
        passed += [copy(4 + j, (*ch, c), sibling) for j, ch in enumerate(chips)]
        arriving += [copy(0, sibling, me)] + [copy(4 + j, (*ch, 1 - c), me) for j, ch in enumerate(chips)]
    return mine, first, landed, passed, arriving


def _core_gather(blk, core, name):
    g = _CoreGather(core)

    def body(*refs):
        g.start(*refs)
        g.forward(*refs)
        g.finish(*refs)

    hbm = pl.BlockSpec(memory_space=pltpu.HBM)
    return pl.pallas_call(
        body, name=name, in_specs=[hbm], out_specs=hbm, out_shape=g.out_shape(blk), scratch_shapes=g.sems(),
        compiler_params=pltpu.CompilerParams(has_side_effects=True),
    )(blk)


IN_TILE = 512


def _tile_index(tiles):
    jumps = [(i, tiles[i] - tiles[i - 1] - 1) for i in range(1, len(tiles)) if tiles[i] != tiles[i - 1] + 1]

    def f(t):
        j = t + tiles[0]
        for i, gap in jumps:
            j = j + jnp.where(t >= i, gap, 0)
        return j

    return f


def _in_proj_phase(h, w_rows, tiles, z_prev, name, gather=None, xchg=None):
    S, D = h.shape
    nt = len(tiles)
    assert w_rows.shape == (nt * IN_TILE, D)
    tm = _fit(S, 2048)
    ni = S // tm
    grid = (ni, nt)
    mid = (7 * ni * nt) // 8
    col = _tile_index(tiles)
    n_in = 2 + (z_prev is not None)
    gather = list(gather or [])
    g_ops = [blk for blk, _ in gather]
    x_ops = list(xchg[0]) if xchg else []
    n_g, n_x = len(g_ops), len(x_ops)

    def body(*refs):
        h_ref, w_ref = refs[0], refs[1]
        g_src = refs[n_in:n_in + n_g]
        x_src = refs[n_in + n_g:n_in + n_g + n_x]
        o = n_in + n_g + n_x
        z_ref = refs[o]
        g_dst = refs[o + 1:o + 1 + n_g]
        x_dst = refs[o + 1 + n_g:o + 1 + n_g + n_x]
        sems = refs[o + 1 + n_g + n_x:]
        g_sems, x_sems = sems[:3 * n_g], sems[3 * n_g:]
        i, t = pl.program_id(0), pl.program_id(1)
        first = (t == 0) & (i == 0)
        last = (t == nt - 1) & (i == ni - 1)

        def each_gather(step):
            for n, (_, how) in enumerate(gather):
                getattr(how, step)(g_src[n], g_dst[n], *g_sems[3 * n:3 * n + 3])

        @pl.when(first)
        def _():
            each_gather("start")
            if xchg:
                mine, out, _, _, _ = _gather2_copies(x_src, x_dst, *x_sems)
                for cp in mine + out:
                    cp.start()

        z_ref[...] = _dot_nt(h_ref[...], w_ref[...]).astype(z_ref.dtype)

        @pl.when((i == mid // nt) & (t == mid % nt))
        def _():
            each_gather("forward")
            if xchg:
                _, _, landed, passed, _ = _gather2_copies(x_src, x_dst, *x_sems)
                for got, cp in zip(landed, passed):
                    got.wait_recv()
                    cp.start()

        @pl.when(last)
        def _():
            each_gather("finish")
            if xchg:
                mine, out, _, passed, arriving = _gather2_copies(x_src, x_dst, *x_sems)
                for cp in arriving:
                    cp.wait_recv()
                for cp in out + passed:
                    cp.wait_send()
                for cp in mine:
                    cp.wait()

    hbm = pl.BlockSpec(memory_space=pltpu.HBM)
    in_specs = [pl.BlockSpec((tm, D), lambda i, t: (i, 0)), pl.BlockSpec((IN_TILE, D), lambda i, t: (t, 0))]
    args = [h, w_rows]
    aliases = {}
    if z_prev is not None:
        in_specs.append(pl.BlockSpec(memory_space=pl.ANY))
        args.append(z_prev)
        aliases = {2: 0}
    in_specs += [hbm] * (n_g + n_x)
    args += g_ops + x_ops
    out_specs = [pl.BlockSpec((tm, IN_TILE), lambda i, t: (i, col(t)))] + [hbm] * (n_g + n_x)
    out_shape = [jax.ShapeDtypeStruct((S, 13 * D), BF16)]
    scratch = []
    for blk, how in gather:
        out_shape.append(how.out_shape(blk))
        scratch += how.sems()
    if xchg:
        out_shape += _xchg_out_shapes(x_ops, xchg[1])
        scratch += _xchg_sems(n_x)
    comm = bool(gather or xchg)
    res = pl.pallas_call(
        body, name=name, grid=grid, in_specs=in_specs, out_specs=out_specs, out_shape=out_shape,
        input_output_aliases=aliases, scratch_shapes=scratch,
        compiler_params=pltpu.CompilerParams(vmem_limit_bytes=VMEM_LIMIT, has_side_effects=comm,
                                             dimension_semantics=("arbitrary", "arbitrary")),
    )(*args)
    z = res[0]
    g_out = list(res[1:1 + n_g])
    x_out = list(res[1 + n_g:]) if xchg else None
    return z, g_out, x_out


def _in_proj_dx(dz, wpT, dgates, wifT, x, rstd, dy, g, side=None):
    S, K = dz.shape
    D = wpT.shape[1]
    tm, tk = _fit(S, 512), _fit(K, 2048)
    nm, nk = S // tm, K // tk
    n_side = len(side[0]) if side else 0

    rc = min(tm, 128)

    def body(*refs):
        a_ref, b_ref, dgt_ref, wif_ref, x_hbm, r_ref, dy_hbm, g_ref = refs[:8]
        srcs = refs[8:8 + n_side]
        dx_ref, dg_ref = refs[8 + n_side], refs[9 + n_side]
        dsts = refs[10 + n_side:10 + 2 * n_side]
        acc, xbuf, dybuf, ep_sems = refs[10 + 2 * n_side:14 + 2 * n_side]
        sems = refs[14 + 2 * n_side:]
        i, k = pl.program_id(0), pl.program_id(1)
        rows = pl.ds(pl.multiple_of(i * tm, tm), tm)
        fetch = [pltpu.make_async_copy(x_hbm.at[rows, :], xbuf, ep_sems.at[0]),
                 pltpu.make_async_copy(dy_hbm.at[rows, :], dybuf, ep_sems.at[1])]

        @pl.when((i == 0) & (k == 0))
        def _():
            dg_ref[...] = jnp.zeros_like(dg_ref)
            if n_side:
                for cp in _xchg_copies(srcs, dsts, *sems, side[1]):
                    cp.start()

        @pl.when(k == nk - 2)
        def _():
            for cp in fetch:
                cp.start()

        p = _dot_nn(a_ref[...], b_ref[...])

        @pl.when(k == 0)
        def _():
            acc[...] = p

        @pl.when((k > 0) & (k < nk - 1))
        def _():
            acc[...] += p

        @pl.when(k == nk - 1)
        def _():
            acc[...] += p + _dot_nn(dgt_ref[...], wif_ref[...])
            for cp in fetch:
                cp.wait()
            for r0 in range(0, tm, rc):
                rr = pl.ds(r0, rc)
                dh = acc[rr, :]
                rs = r_ref[rr, :]
                xn = xbuf[rr, :] * rs
                dg_ref[0:1, :] += jnp.sum(dh * xn, axis=0, keepdims=True)
                dxn = dh * g_ref[...]
                dx_ref[rr, :] = dybuf[rr, :] + rs * (dxn - xn * jnp.mean(dxn * xn, axis=-1, keepdims=True))

        if n_side:
            @pl.when((i == nm - 1) & (k == nk - 1))
            def _():
                for cp in _xchg_copies(srcs, dsts, *sems, side[1]):
                    cp.wait()

    assert nk > 1
    hbm = pl.BlockSpec(memory_space=pltpu.HBM)
    tok = pl.BlockSpec((tm, D), lambda i, k: (i, 0))
    in_specs = [pl.BlockSpec((tm, tk), lambda i, k: (i, k)), pl.BlockSpec((tk, D), lambda i, k: (k, 0)),
                pl.BlockSpec((tm, IF_PAD), lambda i, k: (i, 0)), pl.BlockSpec((IF_PAD, D), lambda i, k: (0, 0)),
                hbm, pl.BlockSpec((tm, 1), lambda i, k: (i, 0)), hbm, pl.BlockSpec((1, D), lambda i, k: (0, 0))]
    out_specs = [tok, pl.BlockSpec((8, D), lambda i, k: (0, 0))]
    out_shape = [jax.ShapeDtypeStruct((S, D), F32), jax.ShapeDtypeStruct((8, D), F32)]
    scratch = [pltpu.VMEM((tm, D), F32), pltpu.VMEM((tm, D), F32), pltpu.VMEM((tm, D), F32),
               pltpu.SemaphoreType.DMA((2,))]
    args = [dz, wpT, dgates, wifT, x, rstd, dy, g]
    if n_side:
        in_specs += [hbm] * n_side
        args += list(side[0])
        out_specs += [hbm] * n_side
        out_shape += _xchg_out_shapes(side[0], side[1])
        scratch += _xchg_sems(n_side)
    res = pl.pallas_call(
        body, name="in_proj_dx", grid=(nm, nk), in_specs=in_specs, out_specs=out_specs, out_shape=out_shape,
        scratch_shapes=scratch,
        compiler_params=pltpu.CompilerParams(vmem_limit_bytes=VMEM_LIMIT, has_side_effects=bool(n_side),
                                             dimension_semantics=("arbitrary", "arbitrary")),
    )(*args)
    return res[0], res[1], list(res[2:])


LATE = ('w_conv_out', 'w_ml_out', 'w_xa_out', 'w_out', 'w_mem_kv', 'w_qk_conv', 'w_dw')
ROW_SHARDED = ('w_conv_out', 'w_ml_out', 'w_xa_out', 'w_out')
COL_SHARDED = ('w_in', 'w_mem_kv', 'w_qk_conv', 'w_dw')


def _cols_to_slabs(g):
    R, C8 = g.shape
    return g.reshape(R, N_DEV, C8 // N_DEV).transpose(1, 0, 2)


def _slabs_to_cols(a):
    n, R, C = a.shape
    return a.transpose(1, 0, 2).reshape(R, n * C)


def _assemble(name, slabs):
    if name in ROW_SHARDED:
        return slabs.reshape(slabs.shape[0] * slabs.shape[1], slabs.shape[2])
    return _slabs_to_cols(slabs)


def _to_slabs(name, g, dtype):
    if name in ROW_SHARDED:
        return g.reshape(N_DEV, g.shape[0] // N_DEV, g.shape[1]).astype(dtype)
    return _cols_to_slabs(g).astype(dtype)


def _w_in_rows(D):
    n = (13 * D + N_IF) // N_DEV
    gate_dev, gate_row = (8 * D) // n, (8 * D) % n
    assert gate_row + N_IF <= n
    height = -(-(n + BF16_ROWS - 1) // BF16_ROWS) * BF16_ROWS

    def first(p):
        return n * p - jnp.where(n * p >= 8 * D + N_IF, N_IF, 0)

    def start(p):
        return jnp.minimum((first(p) // BF16_ROWS) * BF16_ROWS, 13 * D - height)

    return n, gate_dev, gate_row, height, first, start


def _w_in_tiles(D):
    n, gate_dev, *_ = _w_in_rows(D)

    def rows(s):
        f = n * s - (N_IF if n * s >= 8 * D + N_IF else 0)
        return f, f + n - (N_IF if s == gate_dev else 0)

    tiles = range(13 * D // IN_TILE)
    inside = lambda j, s: rows(s)[0] <= j * IN_TILE and (j + 1) * IN_TILE <= rows(s)[1]
    early = [j for j in tiles if any(inside(j, s) for s in range(0, N_DEV, 2))]
    return early, [j for j in tiles if j not in early], rows


def _w_window_height(D):
    n = (13 * D + N_IF) // N_DEV
    return -(-(n + IN_TILE - 1) // IN_TILE) * IN_TILE


def _w_window(block_t, me, D):
    n, gate_dev, gate_row, _, first, _ = _w_in_rows(D)
    no_gates = jnp.concatenate([block_t[:gate_row], block_t[gate_row + N_IF:], jnp.zeros((N_IF, D), block_t.dtype)])
    mine = jnp.where(me == gate_dev, no_gates, block_t)
    off = first(me) % IN_TILE
    return lax.dynamic_update_slice(jnp.zeros((_w_window_height(D), D), block_t.dtype), mine, (off, 0))


def _w_window_parts(window, me, D):
    *_, first, _ = _w_in_rows(D)
    n_win = _w_window_height(D) // IN_TILE
    lead = jnp.where(first(me) % IN_TILE == 0, 0, 1)
    core = lax.dynamic_slice(window, (lead * IN_TILE, 0), ((n_win - 2) * IN_TILE, D))
    low = jnp.where(lead == 0, window[(n_win - 2) * IN_TILE:(n_win - 1) * IN_TILE], window[:IN_TILE])
    return core, jnp.concatenate([low, window[(n_win - 1) * IN_TILE:]], axis=0)


def _w_segments(s, D, window=None, core=None, edges=None):
    _, _, rows = _w_in_tiles(D)
    n_win = _w_window_height(D) // IN_TILE
    t0, lead = rows(s)[0] // IN_TILE, 0 if rows(s)[0] % IN_TILE == 0 else 1
    segs = []
    if window is not None:
        segs.append((t0, window))
    if core is not None:
        segs.append((t0 + lead, core))
    if edges is not None:
        segs += [(t0 + (n_win - 2 if lead == 0 else 0), edges[:IN_TILE]), (t0 + n_win - 1, edges[IN_TILE:])]
    return segs


def _w_rows_of_tiles(tiles, segments, D):
    _, _, rows = _w_in_tiles(D)

    def find(s, j):
        for t0, arr in segments[s]:
            if t0 <= j < t0 + arr.shape[0] // IN_TILE:
                return arr, j - t0
        raise KeyError((s, j))

    runs = []
    for j in tiles:
        own = [s for s in range(N_DEV) if rows(s)[0] < (j + 1) * IN_TILE and j * IN_TILE < rows(s)[1]]
        got = [find(s, j) for s in own]
        last = runs[-1] if runs else None
        if (last and len(got) == 1 and len(last[0]) == 1 and last[2] == j and last[0][0][0] is got[0][0]
                and last[0][0][1] + last[1] == got[0][1]):
            last[1], last[2] = last[1] + 1, j + 1
        else:
            runs.append([got, 1, j + 1])
    out = [functools.reduce(jnp.add, [arr[i * IN_TILE:(i + k) * IN_TILE] for arr, i in got]) for got, k, _ in runs]
    return jnp.concatenate(out, axis=0)


def _interleave_tiles(parts):
    where = {j: (rows, i) for rows, tiles in parts for i, j in enumerate(tiles)}
    out, j, n_tiles = [], 0, len(where)
    while j < n_tiles:
        rows, i = where[j]
        k = 1
        while j + k in where and where[j + k][0] is rows and where[j + k][1] == i + k:
            k += 1
        out.append(rows[i * IN_TILE:(i + k) * IN_TILE])
        j += k
    return jnp.concatenate(out, axis=0)


def _local_step(x, mem, tgt, g_pre, w_early, w_rest, wifT, b_if, b_dw, g_ln, b_ln, g_head, g_mem, g_post, late, dist):
    S, D = x.shape
    L = min(ML_CHUNK, S)
    NC = S // L
    bif_row = jnp.zeros((1, IF_PAD), F32).at[0, :N_IF].set(b_if)
    bif_col = jnp.zeros((16, 1), F32).at[:N_IF, 0].set(b_if)
    early, rest, _ = _w_in_tiles(D)

    h, rstd = _rmsnorm_fwd(x, g_pre, "prenorm_fwd")
    if dist:
        my_window, my_edges, rows_of_rest = w_rest
        z, (north, south_edges), _ = _in_proj_phase(
            h, w_early, early, None, "in_proj_early",
            gather=[(my_window, _CoreGather(1)), (my_edges, _CoreGather(0))])
        w_rest = rows_of_rest(north, south_edges)
        z, _, gathered = _in_proj_phase(h, w_rest, rest, z, "in_proj_rest",
                                        xchg=([late[n] for n in LATE] + [wifT], False))
        full = {n: _assemble(n, a) for n, a in zip(LATE, gathered)}
        wifT = jnp.zeros((IF_PAD, D), BF16).at[:N_IF].set(gathered[-1][_w_in_rows(D)[1], :N_IF])
    else:
        z, _, _ = _in_proj_phase(h, w_early, early, None, "in_proj_early")
        z, _, _ = _in_proj_phase(h, w_rest, rest, z, "in_proj_rest")
        full = late
    wpT = _interleave_tiles([(w_early, early), (w_rest, rest)])
    w_co, w_mo, w_xo, w_out, w_kv = (full[n] for n in LATE[:5])
    w_qk = jnp.zeros((8, 2 * D), F32).at[:QK_W].set(full['w_qk_conv'])
    w_dw = jnp.zeros((32, D), F32).at[:CONV_W].set(full['w_dw'])
    gcol = _mm(h, wifT, mode="nt", out_dtype=F32, tm=1024, tn=IF_PAD, tk=2048, name="gates_col")
    grow = _mm(wifT[:16], h, mode="nt", out_dtype=F32, tm=16, tn=1024, tk=2048, name="gates_row")
    grow = grow.reshape(16, NC, L).transpose(1, 0, 2)

    u1, pc = _conv_fwd(z, w_dw, b_dw, g_ln, b_ln, D)
    qks = _qk_fwd(z, w_qk, D)
    hm, cst, nst = _mlstm_fwd(qks, z, gcol, grow, bif_row, bif_col, D)
    pm = _ml_post_fwd(hm, z, g_head, D)
    memn, rstd_mem = _rmsnorm_fwd(mem, g_mem, "memnorm_fwd")
    kv = _mm(memn, w_kv, mode="nn", out_dtype=BF16, tm=256, tn=1024, tk=2048, name="mem_kv")
    px = _xattn_fwd(z, kv, D)

    proj = functools.partial(_mm, mode="nn", tm=1024, tn=1024, tk=2048)
    yc = proj(pc, w_co, out_dtype=BF16, name="conv_out")
    ym = proj(pm, w_mo, out_dtype=BF16, name="ml_out")
    yx = proj(px, w_xo, out_dtype=BF16, name="xa_out")
    mg = _merge_fwd(z, yc, ym, yx, D)
    r = proj(mg, w_out, out_dtype=F32, name="out_proj")
    dy, dr, dg_post, loss = _post_loss(r, x, tgt, g_post)

    dgrad = functools.partial(_mm, mode="nt", out_dtype=BF16, tm=1024, tn=1024, tk=2048)
    wgrad = functools.partial(_mm, mode="tn", out_dtype=F32, tm=1024, tn=1024, tk=2048)
    dmg = dgrad(dr, w_out, name="out_proj_dx")
    dw_out = wgrad(mg, dr, name="out_proj_dw")
    dyc, dym, dyx, dz = _merge_bwd(dmg, z, yc, ym, yx, D)
    dpc = dgrad(dyc, w_co, name="conv_out_dx")
    dw_co = wgrad(pc, dyc, name="conv_out_dw")
    dpm = dgrad(dym, w_mo, name="ml_out_dx")
    dw_mo = wgrad(pm, dym, name="ml_out_dw")
    dpx = dgrad(dyx, w_xo, name="xa_out_dx")
    dw_xo = wgrad(px, dyx, name="xa_out_dw")

    dkv, dz = _xattn_bwd(dpx, z, kv, dz, D)
    dkv_b = dkv.astype(BF16)
    dw_kv = wgrad(memn, dkv_b, name="mem_kv_dw")
    dmemn = _mm(dkv_b, w_kv, mode="nt", out_dtype=F32, tm=256, tn=1024, tk=2048, name="mem_kv_dx")
    dg_mem = _gain_grad(dmemn, mem, rstd_mem)

    du1, cstats, dz = _conv_bwd1(dpc, z, u1, g_ln, b_ln, dz, D)
    dw_dw, dz = _conv_bwd2(du1, z, w_dw, dz, D)

    dhm, dg_head, dz = _ml_post_bwd(dpm, hm, z, g_head, dz, D)
    dqks, dgates, dz = _mlstm_bwd(qks, z, gcol, grow, bif_row, bif_col, hm, dhm, cst, nst, dz, D)
    dc, dw_qk = _qk_bwd1(dqks, z, w_qk, D)
    dz = _qk_bwd2(dc, w_qk, dz, D)

    g = dict(w_conv_out=dw_co, w_ml_out=dw_mo, w_xa_out=dw_xo, w_out=dw_out, w_mem_kv=dw_kv,
             w_qk_conv=dw_qk[:QK_W], w_dw=dw_dw[:CONV_W])
    dgates_b = dgates.astype(BF16)
    in_proj_dw = functools.partial(_mm, dz, h, mode="tn", out_dtype=BF16 if dist else F32, tm=1024, tn=2048, tk=2048,
                                   name="in_proj_dw")
    in_proj_dx = functools.partial(_in_proj_dx, dz, wpT, dgates_b, wifT, x, rstd, dy, g_pre)
    parts = {}
    if dist:
        send = [_to_slabs(n, g[n], BF16 if n in LATE[:5] else F32) for n in LATE]
        dwpT, recv = in_proj_dw(side=(send, True))
        parts.update(zip(LATE, recv))
        *_, height, _, start = _w_in_rows(D)
        dx, dg_pre, recv = in_proj_dx(side=([dwpT], _RowWindows(start, height)))
        parts['w_in'] = recv[0]
    else:
        dwpT = in_proj_dw()
        dx, dg_pre, _ = in_proj_dx()
    dwifT = _mm(dgates_b, h, mode="tn", out_dtype=F32, tm=IF_PAD, tn=2048, tk=2048, name="gates_dw")
    if not dist:
        g['w_in'] = jnp.concatenate([dwpT[:8 * D], dwifT[:N_IF], dwpT[8 * D:]], axis=0).T

    db_if = _col_sum(dgates)[0, :N_IF]
    g.update(g_pre=dg_pre[0], b_if=db_if, b_dw=cstats[2], g_ln=cstats[0], b_ln=cstats[1], g_ml_head=dg_head[0],
             g_mem=dg_mem[0], g_post=dg_post[0], w_in_gates=dwifT[:N_IF])
    return loss[0, 0], dx, g, parts


WEIGHTS = ('g_pre', 'w_in', 'b_if', 'w_qk_conv', 'w_dw', 'b_dw', 'g_ln', 'b_ln', 'w_conv_out', 'g_ml_head',
           'w_ml_out', 'g_mem', 'w_mem_kv', 'w_xa_out', 'w_out', 'g_post')
VECTORS = ('g_pre', 'b_dw', 'g_ln', 'b_ln', 'g_ml_head', 'g_mem', 'g_post')


def kernel(x, mem, g_pre, w_in, b_if, w_qk_conv, w_dw, b_dw, g_ln, b_ln, w_conv_out, g_ml_head, w_ml_out, g_mem, w_mem_kv, w_xa_out, w_out, g_post, loss_target, m_g_pre, m_w_in, m_b_if, m_w_qk_conv, m_w_dw, m_b_dw, m_g_ln, m_b_ln, m_w_conv_out, m_g_ml_head, m_w_ml_out, m_g_mem, m_w_mem_kv, m_w_xa_out, m_w_out, m_g_post, v_g_pre, v_w_in, v_b_if, v_w_qk_conv, v_w_dw, v_b_dw, v_g_ln, v_b_ln, v_w_conv_out, v_g_ml_head, v_w_ml_out, v_g_mem, v_w_mem_kv, v_w_xa_out, v_w_out, v_g_post):
    S, D = x.shape[1], x.shape[2]
    w = dict(g_pre=g_pre, w_in=w_in, b_if=b_if, w_qk_conv=w_qk_conv, w_dw=w_dw, b_dw=b_dw, g_ln=g_ln, b_ln=b_ln,
             w_conv_out=w_conv_out, g_ml_head=g_ml_head, w_ml_out=w_ml_out, g_mem=g_mem, w_mem_kv=w_mem_kv,
             w_xa_out=w_xa_out, w_out=w_out, g_post=g_post)
    mom = dict(g_pre=m_g_pre, w_in=m_w_in, b_if=m_b_if, w_qk_conv=m_w_qk_conv, w_dw=m_w_dw, b_dw=m_b_dw,
               g_ln=m_g_ln, b_ln=m_b_ln, w_conv_out=m_w_conv_out, g_ml_head=m_g_ml_head, w_ml_out=m_w_ml_out,
               g_mem=m_g_mem, w_mem_kv=m_w_mem_kv, w_xa_out=m_w_xa_out, w_out=m_w_out, g_post=m_g_post)
    var = dict(g_pre=v_g_pre, w_in=v_w_in, b_if=v_b_if, w_qk_conv=v_w_qk_conv, w_dw=v_w_dw, b_dw=v_b_dw,
               g_ln=v_g_ln, b_ln=v_b_ln, w_conv_out=v_w_conv_out, g_ml_head=v_g_ml_head, w_ml_out=v_w_ml_out,
               g_mem=v_g_mem, w_mem_kv=v_w_mem_kv, w_xa_out=v_w_xa_out, w_out=v_w_out, g_post=v_g_post)

    n_rows, gate_dev, gate_row, height, first, start = _w_in_rows(D)
    me = 4 * lax.axis_index("x") + 2 * lax.axis_index("y") + lax.axis_index("c")

    my_block = w_in.T.astype(BF16)
    my_window = _w_window(my_block, me, D)
    my_core, my_edges = _w_window_parts(my_window, me, D)
    south = _core_gather(my_core, 0, "gather_w_in_south")
    early, rest, _ = _w_in_tiles(D)
    w_early = _w_rows_of_tiles(early, {2 * q: _w_segments(2 * q, D, core=south[q]) for q in range(N_CHIP)}, D)
    my_gates = jnp.zeros((BF16_ROWS, D), BF16).at[:N_IF].set(my_block[gate_row:gate_row + N_IF])

    def rows_of_rest(north, south_edges):
        segments = {2 * q: _w_segments(2 * q, D, core=south[q], edges=south_edges[q]) for q in range(N_CHIP)}
        segments.update({2 * q + 1: _w_segments(2 * q + 1, D, window=north[q]) for q in range(N_CHIP)})
        return _w_rows_of_tiles(rest, segments, D)

    late = {n: w[n].astype(BF16) if n in LATE[:5] else w[n] for n in LATE}
    row = lambda a: a.reshape(1, D)

    loss, dx, g, parts = _local_step(
        x[0], mem[0], loss_target[0], row(g_pre), w_early, (my_window, my_edges, rows_of_rest), my_gates, b_if,
        row(b_dw), row(g_ln), row(b_ln), row(g_ml_head), row(g_mem), row(g_post), late, True)

    pad = lambda a: jnp.zeros((D,), F32).at[:N_IF].set(a)
    vec = jnp.concatenate([jnp.stack([g[n] for n in VECTORS] + [pad(g['b_if'])]), g['w_in_gates']])
    parts['vec'] = _exchange([vec], False, "gather_vector_grads")[0]

    out = {}
    for n in LATE:
        out[n] = _adamw(parts[n], w[n], mom[n], var[n], "adamw_" + n)
    zeros8 = jnp.zeros((N_IF, D), F32)
    stack = lambda d: jnp.concatenate([jnp.stack([d[n] for n in VECTORS] + [pad(d['b_if'])]), zeros8])
    vres = _adamw(parts['vec'], stack(w), stack(mom), stack(var), "adamw_vectors")
    for i, n in enumerate(VECTORS):
        out[n] = [r[i] for r in vres]
    out['b_if'] = [r[len(VECTORS), :N_IF] for r in vres]
    gate_grad = vres[0][len(VECTORS) + 1:]

    win = _sum_parts(parts['w_in'], "sum_w_in_grads")
    seg = lax.dynamic_slice(win, (first(me) - start(me), 0), (n_rows, D))
    with_gates = jnp.concatenate([seg[:gate_row], gate_grad, seg[gate_row:n_rows - N_IF]], axis=0)
    g_in = jnp.where(me == gate_dev, with_gates, seg)
    out['w_in'] = [r.T for r in _adamw(g_in[None], w_in.T, m_w_in.T, v_w_in.T, "adamw_w_in")]

    loss = lax.psum(loss, ("x", "y", "c"))
    res = [loss, dx.reshape(1, S, D)]
    for k in range(4):
        res += [out[n][k] for n in WEIGHTS]
    return tuple(res)
```

```python
import functools
import math

import jax
import jax.numpy as jnp
from jax import lax
from jax.experimental import pallas as pl
from jax.experimental.pallas import tpu as pltpu

F32 = jnp.float32
BF16 = jnp.bfloat16

N_DEV = 8
N_HEADS = 4
CONV_W = 31
QK_W = 4
N_IF = 2 * N_HEADS
IF_PAD = 128
EPS = 1e-6
NEG = -1e30

ADAM_LR = 0.001
ADAM_B1 = 0.9
ADAM_B2 = 0.999
ADAM_EPS = 1e-08
ADAM_WD = 0.01
ADAM_STEP = 10

TOK_TILE = 512
ML_CHUNK = 256
HALO = 32
BF16_ROWS = 16
VMEM_LIMIT = 56 * 1024 * 1024


def _cparams(sem=None):
    kw = dict(vmem_limit_bytes=VMEM_LIMIT)
    if sem is not None:
        kw["dimension_semantics"] = sem
    return pltpu.CompilerParams(**kw)


def _sig(x):
    return jax.nn.sigmoid(x)


def _silu(x):
    return x * _sig(x)


def _dsilu(x):
    s = _sig(x)
    return s * (1.0 + x * (1.0 - s))


def _logsig(x):
    return jnp.minimum(x, 0.0) - jnp.log(1.0 + jnp.exp(-jnp.abs(x)))


def _dot(a, b, dims):
    return lax.dot_general(a, b, (dims, ((), ())), preferred_element_type=F32)


def _dot_nn(a, b):
    return _dot(a, b, ((1,), (0,)))


def _dot_nt(a, b):
    return _dot(a, b, ((1,), (1,)))


def _dot_tn(a, b):
    return _dot(a, b, ((0,), (0,)))


def _split3(a):
    hi = a.astype(BF16)
    r1 = a - hi.astype(F32)
    mid = r1.astype(BF16)
    lo = (r1 - mid.astype(F32)).astype(BF16)
    return hi, mid, lo


def _tri_left(tri, a):
    hi, mid, lo = _split3(a)
    return _dot_nn(tri, hi) + _dot_nn(tri, mid) + _dot_nn(tri, lo)


def _tri_right(a, tri):
    hi, mid, lo = _split3(a)
    return _dot_nn(hi, tri) + _dot_nn(mid, tri) + _dot_nn(lo, tri)


def _fit(n, t):
    if n <= t:
        return n
    return max(c for c in range(128, t + 1, 128) if n % c == 0)


def _peer(k):
    x, y, c = lax.axis_index("x"), lax.axis_index("y"), lax.axis_index("c")
    px = 1 - x if (k >> 2) & 1 else x
    py = 1 - y if (k >> 1) & 1 else y
    pc = 1 - c if k & 1 else c
    return (px, py, pc), 4 * px + 2 * py + pc


class _RowWindows:
    def __init__(self, start, height):
        self.start, self.height = start, height


def _xchg_copies(srcs, dsts, send_sems, recv_sems, loc_sems, scatter):
    _, me = _peer(0)

    def piece(src, p):
        if isinstance(scatter, _RowWindows):
            return src.at[pl.ds(pl.multiple_of(scatter.start(p), BF16_ROWS), scatter.height), :]
        return src.at[p] if scatter else src

    copies = []
    for t, (src, dst) in enumerate(zip(srcs, dsts)):
        copies.append(pltpu.make_async_copy(piece(src, me), dst.at[me], loc_sems.at[t]))
        for k in range(1, N_DEV):
            dev, p = _peer(k)
            s = t * (N_DEV - 1) + k - 1
            copies.append(pltpu.make_async_remote_copy(
                src_ref=piece(src, p), dst_ref=dst.at[me],
                send_sem=send_sems.at[s], recv_sem=recv_sems.at[s],
                device_id=dev, device_id_type=pl.DeviceIdType.MESH))
    return copies


def _xchg_out_shapes(operands, scatter):
    def one(a):
        if isinstance(scatter, _RowWindows):
            return (scatter.height, a.shape[1])
        return tuple(a.shape[1:] if scatter else a.shape)
    return [jax.ShapeDtypeStruct((N_DEV,) + one(a), a.dtype) for a in operands]


def _xchg_sems(n):
    return [pltpu.SemaphoreType.DMA((n * (N_DEV - 1),)), pltpu.SemaphoreType.DMA((n * (N_DEV - 1),)),
            pltpu.SemaphoreType.DMA((n,))]


def _mm(a, b, *, mode, out_dtype, tm, tn, tk, name, n_outer=False, acc_in=None, side=None):
    if mode == "nn":
        (M, K), (K2, N) = a.shape, b.shape
    elif mode == "nt":
        (M, K), (N, K2) = a.shape, b.shape
    else:
        (K, M), (K2, N) = a.shape, b.shape
    assert K == K2, (a.shape, b.shape, mode)
    tm, tn, tk = _fit(M, tm), _fit(N, tn), _fit(K, tk)
    assert M % tm == 0 and N % tn == 0 and K % tk == 0, (a.shape, b.shape, tm, tn, tk)
    nk = K // tk
    if n_outer:
        grid = (N // tn, M // tm, nk)
        ij = lambda g0, g1: (g1, g0)
    else:
        grid = (M // tm, N // tn, nk)
        ij = lambda g0, g1: (g0, g1)

    if mode == "nn":
        a_spec = pl.BlockSpec((tm, tk), lambda g0, g1, k: (ij(g0, g1)[0], k))
        b_spec = pl.BlockSpec((tk, tn), lambda g0, g1, k: (k, ij(g0, g1)[1]))
        dot = _dot_nn
    elif mode == "nt":
        a_spec = pl.BlockSpec((tm, tk), lambda g0, g1, k: (ij(g0, g1)[0], k))
        b_spec = pl.BlockSpec((tn, tk), lambda g0, g1, k: (ij(g0, g1)[1], k))
        dot = _dot_nt
    else:
        a_spec = pl.BlockSpec((tk, tm), lambda g0, g1, k: (k, ij(g0, g1)[0]))
        b_spec = pl.BlockSpec((tk, tn), lambda g0, g1, k: (k, ij(g0, g1)[1]))
        dot = _dot_tn
    o_spec = pl.BlockSpec((tm, tn), lambda g0, g1, k: ij(g0, g1))
    has_acc = acc_in is not None
    n_side = len(side[0]) if side is not None else 0
    scatter = side[1] if side is not None else False
    n_in = 2 + int(has_acc)

    def body(*refs):
        a_ref, b_ref = refs[0], refs[1]
        c_ref = refs[2] if has_acc else None
        srcs = refs[n_in:n_in + n_side]
        o_ref = refs[n_in + n_side]
        dsts = refs[n_in + n_side + 1:n_in + 2 * n_side + 1]
        scratch = refs[n_in + 2 * n_side + 1:]
        k = pl.program_id(2)
        if n_side:
            sems = scratch[-3:]
            first = (pl.program_id(0) == 0) & (pl.program_id(1) == 0) & (k == 0)
            last = (pl.program_id(0) == grid[0] - 1) & (pl.program_id(1) == grid[1] - 1) & (k == nk - 1)

            @pl.when(first)
            def _():
                for cp in _xchg_copies(srcs, dsts, *sems, scatter):
                    cp.start()

        p = dot(a_ref[...], b_ref[...])

        def finish(r):
            if has_acc:
                r = r + c_ref[...].astype(F32)
            o_ref[...] = r.astype(out_dtype)

        if nk == 1:
            finish(p)
        else:
            acc = scratch[0]

            @pl.when(k == 0)
            def _():
                acc[...] = p

            @pl.when((k > 0) & (k < nk - 1))
            def _():
                acc[...] += p

            @pl.when(k == nk - 1)
            def _():
                finish(acc[...] + p)

        if n_side:
            @pl.when(last)
            def _():
                for cp in _xchg_copies(srcs, dsts, *sems, scatter):
                    cp.wait()

    hbm = pl.BlockSpec(memory_space=pltpu.HBM)
    in_specs = [a_spec, b_spec]
    args = [a, b]
    if has_acc:
        in_specs.append(o_spec)
        args.append(acc_in)
    out_specs = [o_spec]
    out_shape = [jax.ShapeDtypeStruct((M, N), out_dtype)]
    scratch_shapes = [pltpu.VMEM((tm, tn), F32)] if nk > 1 else []
    if n_side:
        in_specs += [hbm] * n_side
        args += list(side[0])
        out_specs += [hbm] * n_side
        out_shape += _xchg_out_shapes(side[0], scatter)
        scratch_shapes += _xchg_sems(n_side)
        cp = pltpu.CompilerParams(vmem_limit_bytes=VMEM_LIMIT, has_side_effects=True,
                                  dimension_semantics=("arbitrary", "arbitrary", "arbitrary"))
    else:
        cp = _cparams(("parallel", "parallel", "arbitrary"))
    res = pl.pallas_call(
        body, name=name, grid=grid, in_specs=in_specs, out_specs=out_specs, out_shape=out_shape,
        scratch_shapes=scratch_shapes, compiler_params=cp,
    )(*args)
    return (res[0], list(res[1:])) if n_side else res[0]


def _rmsnorm_fwd(x, g, name):
    S, D = x.shape
    T = min(TOK_TILE, S)

    def body(x_ref, g_ref, h_ref, r_ref):
        xv = x_ref[...]
        r = lax.rsqrt(jnp.mean(xv * xv, axis=-1, keepdims=True) + EPS)
        h_ref[...] = (xv * r * g_ref[...]).astype(BF16)
        r_ref[...] = r

    return pl.pallas_call(
        body, name=name, grid=(S // T,),
        in_specs=[pl.BlockSpec((T, D), lambda i: (i, 0)), pl.BlockSpec((1, D), lambda i: (0, 0))],
        out_specs=[pl.BlockSpec((T, D), lambda i: (i, 0)), pl.BlockSpec((T, 1), lambda i: (i, 0))],
        out_shape=[jax.ShapeDtypeStruct((S, D), BF16), jax.ShapeDtypeStruct((S, 1), F32)],
        compiler_params=_cparams(("parallel",)),
    )(x, g)


LANE = 128
ROW_BLK = 64


def _shift_strip(src, sh, cs, nr, residues=range(1, 8)):
    for r in residues:
        sh[r] = src[pl.ds(r, nr), cs]


def _tap(src, sh, cs, o, r0):
    r = o % 8
    if r == 0:
        return src[pl.ds(o + r0, ROW_BLK), cs]
    return sh[r, pl.ds(o - r + r0, ROW_BLK), :]


def _conv_fwd(z, w_dw, b_dw, g_ln, b_ln, D):
    S = z.shape[0]
    T = min(TOK_TILE, S)
    hb = T // HALO

    def body(a_ref, b_ref, zc_ref, ah_ref, bh_ref, w_ref, bdw_ref, gln_ref, bln_ref, u1_ref, pc_ref, u0s, sh):
        i = pl.program_id(0)
        a = a_ref[...].astype(F32)
        b = b_ref[...].astype(F32)
        u0s[pl.ds(HALO, T), :] = a * _sig(b)
        halo = ah_ref[...].astype(F32) * _sig(bh_ref[...].astype(F32))
        u0s[pl.ds(0, HALO), :] = jnp.where(i > 0, halo, 0.0)
        u0s[pl.ds(T + HALO, 8), :] = jnp.zeros((8, D), F32)
        for c0 in range(0, D, LANE):
            cs = pl.ds(c0, LANE)
            _shift_strip(u0s, sh, cs, T + HALO)
            for r0 in range(0, T, ROW_BLK):
                acc = jnp.broadcast_to(bdw_ref[:, cs], (ROW_BLK, LANE))
                for j in range(CONV_W):
                    acc = acc + w_ref[pl.ds(j, 1), cs] * _tap(u0s, sh, cs, HALO - (CONV_W - 1) + j, r0)
                u1_ref[pl.ds(r0, ROW_BLK), cs] = acc
        acc = u1_ref[...]
        mu = jnp.mean(acc, axis=-1, keepdims=True)
        xc = acc - mu
        var = jnp.mean(xc * xc, axis=-1, keepdims=True)
        ln = xc * lax.rsqrt(var + EPS) * gln_ref[...] + bln_ref[...]
        pc_ref[...] = (_silu(ln) * _silu(zc_ref[...].astype(F32))).astype(BF16)

    prev = lambda i: (jnp.maximum(i * hb - 1, 0), 0)
    prev_b = lambda i: (jnp.maximum(i * hb - 1, 0), D // D)
    vec = pl.BlockSpec((1, D), lambda i: (0, 0))
    return pl.pallas_call(
        body, name="conv_fwd", grid=(S // T,),
        in_specs=[pl.BlockSpec((T, D), lambda i: (i, 0)), pl.BlockSpec((T, D), lambda i: (i, 1)),
                  pl.BlockSpec((T, D), lambda i: (i, 2)),
                  pl.BlockSpec((HALO, D), prev), pl.BlockSpec((HALO, D), prev_b),
                  pl.BlockSpec((32, D), lambda i: (0, 0)), vec, vec, vec],
        out_specs=[pl.BlockSpec((T, D), lambda i: (i, 0)), pl.BlockSpec((T, D), lambda i: (i, 0))],
        out_shape=[jax.ShapeDtypeStruct((S, D), F32), jax.ShapeDtypeStruct((S, D), BF16)],
        scratch_shapes=[pltpu.VMEM((T + HALO + 8, D), F32), pltpu.VMEM((8, T + HALO, LANE), F32)],
        compiler_params=_cparams(("parallel",)),
    )(z, z, z, z, z, w_dw, b_dw, g_ln, b_ln)


_QK_RES = sorted({(HALO - (QK_W - 1) + j) % 8 for j in range(QK_W)} - {0})


def _qk_fill(ps, p0_ref, p1_ref, h0_ref, h1_ref, T, D):
    i = pl.program_id(0)
    ps[pl.ds(HALO, T), pl.ds(0, D)] = p0_ref[...].astype(F32)
    ps[pl.ds(HALO, T), pl.ds(D, D)] = p1_ref[...].astype(F32)
    ps[pl.ds(0, HALO), pl.ds(0, D)] = jnp.where(i > 0, h0_ref[...].astype(F32), 0.0)
    ps[pl.ds(0, HALO), pl.ds(D, D)] = jnp.where(i > 0, h1_ref[...].astype(F32), 0.0)
    ps[pl.ds(T + HALO, 8), :] = jnp.zeros((8, 2 * D), F32)


def _qk_conv(ps, sh, w_ref, cs, r0):
    acc = jnp.zeros((ROW_BLK, LANE), F32)
    for j in range(QK_W):
        acc = acc + w_ref[pl.ds(j, 1), cs] * _tap(ps, sh, cs, HALO - (QK_W - 1) + j, r0)
    return acc


def _qk_fwd(z, w_qk, D):
    S = z.shape[0]
    T = min(TOK_TILE, S)
    hb = T // HALO
    scale = (D // N_HEADS) ** -0.5
    W2 = 2 * D

    def body(p0_ref, p1_ref, h0_ref, h1_ref, w_ref, o_ref, ps, sh):
        _qk_fill(ps, p0_ref, p1_ref, h0_ref, h1_ref, T, D)
        for c0 in range(0, W2, LANE):
            cs = pl.ds(c0, LANE)
            _shift_strip(ps, sh, cs, T + HALO, _QK_RES)
            for r0 in range(0, T, ROW_BLK):
                qk = _silu(_qk_conv(ps, sh, w_ref, cs, r0))
                o_ref[pl.ds(r0, ROW_BLK), cs] = (qk * scale if c0 >= D else qk).astype(BF16)

    prev0 = lambda i: (jnp.maximum(i * hb - 1, 0), 3)
    prev1 = lambda i: (jnp.maximum(i * hb - 1, 0), 4)
    return pl.pallas_call(
        body, name="qk_fwd", grid=(S // T,),
        in_specs=[pl.BlockSpec((T, D), lambda i: (i, 3)), pl.BlockSpec((T, D), lambda i: (i, 4)),
                  pl.BlockSpec((HALO, D), prev0), pl.BlockSpec((HALO, D), prev1),
                  pl.BlockSpec((8, W2), lambda i: (0, 0))],
        out_specs=pl.BlockSpec((T, W2), lambda i: (i, 0)),
        out_shape=jax.ShapeDtypeStruct((S, W2), BF16),
        scratch_shapes=[pltpu.VMEM((T + HALO + 8, W2), F32), pltpu.VMEM((8, T + HALO, LANE), F32)],
        compiler_params=_cparams(("parallel",)),
    )(z, z, z, z, w_qk)


def _ml_gates(gc_ref, gr_ref, bifr_ref, bifc_ref, L):
    gc = gc_ref[...] + bifr_ref[...]
    gr = gr_ref[...] + bifc_ref[...]
    ii = lax.broadcasted_iota(jnp.int32, (L, L), 0)
    jj = lax.broadcasted_iota(jnp.int32, (L, L), 1)
    tri = (jj <= ii).astype(BF16)
    triu = (ii <= jj).astype(BF16)
    bcol_all = _tri_left(tri, _logsig(gc))
    brow_all = _tri_right(_logsig(gr), triu)
    return gc, gr, bcol_all, brow_all, ii, jj


def _ml_intra(q, k, b_c, b_r, li_r, m, n_row, ii, jj):
    d = b_c - b_r + li_r
    dm = jnp.where(jj <= ii, d, NEG)
    inter = b_c + m
    m_row = jnp.maximum(inter, jnp.max(dm, axis=1, keepdims=True))
    w_intra = jnp.exp(dm - m_row)
    w_inter = jnp.exp(inter - m_row)
    qk = _dot_nt(q, k)
    s = qk * w_intra
    qn = jnp.sum(q.astype(F32) * n_row, axis=-1, keepdims=True)
    den = jnp.sum(s, axis=-1, keepdims=True) + w_inter * qn
    hdiv = jnp.maximum(jnp.abs(den), jnp.exp(-m_row))
    return qk, w_intra, w_inter, s, qn, den, hdiv, m_row


def _ml_state(b_c, li_c, m, L):
    b_last = b_c[L - 1:L, :]
    g_c = b_last - b_c + li_c
    m_new = jnp.maximum(b_last + m, jnp.max(g_c, axis=0, keepdims=True))
    decay = jnp.exp(b_last + m - m_new)
    wk = jnp.exp(g_c - m_new)
    return m_new, decay, wk


def _mlstm_fwd(qks, z, gcol, grow, bif_row, bif_col, D):
    S = qks.shape[0]
    L = min(ML_CHUNK, S)
    NC = S // L
    H = N_HEADS
    dh = D // H

    def body(q_ref, k_ref, v_ref, gc_ref, gr_ref, bifr_ref, bifc_ref, hm_ref, cst_ref, nst_ref, C, NM):
        c = pl.program_id(0)

        @pl.when(c == 0)
        def _():
            C[...] = jnp.zeros_like(C)
            NM[...] = jnp.zeros_like(NM)

        gc, gr, bcol_all, brow_all, ii, jj = _ml_gates(gc_ref, gr_ref.at[0], bifr_ref, bifc_ref, L)
        for h in range(H):
            hs = pl.ds(h * dh, dh)
            q = q_ref[:, hs]
            k = k_ref[:, hs]
            v = v_ref[:, hs]
            b_c = bcol_all[:, H + h:H + h + 1]
            li_c = gc[:, h:h + 1]
            b_r = brow_all[H + h:H + h + 1, :]
            li_r = gr[h:h + 1, :]
            n_row = NM[h, 0:1, :]
            m = NM[h, 1:2, 0:1]
            Cb = C[h].astype(BF16)
            cst_ref[0, h] = Cb
            nst_ref[0, h] = NM[h]
            qk, w_intra, w_inter, s, qn, den, hdiv, m_row = _ml_intra(q, k, b_c, b_r, li_r, m, n_row, ii, jj)
            num = _dot_nn(s.astype(BF16), v) + w_inter * _dot_nn(q, Cb)
            hm_ref[:, hs] = num / hdiv
            m_new, decay, wk = _ml_state(b_c, li_c, m, L)
            wkk = wk * k.astype(F32)
            C[h] = decay * C[h] + _dot_tn(wkk.astype(BF16), v)
            NM[h, 0:1, :] = decay * n_row + jnp.sum(wkk, axis=0, keepdims=True)
            NM[h, 1:2, :] = jnp.broadcast_to(m_new, (1, dh))

    return pl.pallas_call(
        body, name="mlstm_fwd", grid=(NC,),
        in_specs=[pl.BlockSpec((L, D), lambda c: (c, 0)), pl.BlockSpec((L, D), lambda c: (c, 1)),
                  pl.BlockSpec((L, D), lambda c: (c, 5)),
                  pl.BlockSpec((L, IF_PAD), lambda c: (c, 0)),
                  pl.BlockSpec((1, 16, L), lambda c: (c, 0, 0)),
                  pl.BlockSpec((1, IF_PAD), lambda c: (0, 0)), pl.BlockSpec((16, 1), lambda c: (0, 0))],
        out_specs=[pl.BlockSpec((L, D), lambda c: (c, 0)),
                   pl.BlockSpec((1, H, dh, dh), lambda c: (c, 0, 0, 0)),
                   pl.BlockSpec((1, H, 8, dh), lambda c: (c, 0, 0, 0))],
        out_shape=[jax.ShapeDtypeStruct((S, D), F32), jax.ShapeDtypeStruct((NC, H, dh, dh), BF16),
                   jax.ShapeDtypeStruct((NC, H, 8, dh), F32)],
        scratch_shapes=[pltpu.VMEM((H, dh, dh), F32), pltpu.VMEM((H, 8, dh), F32)],
        compiler_params=_cparams(("arbitrary",)),
    )(qks, qks, z, gcol, grow, bif_row, bif_col)


def _mlstm_bwd(qks, z, gcol, grow, bif_row, bif_col, hm, dhm, cst, nst, dz, D):
    S = qks.shape[0]
    L = min(ML_CHUNK, S)
    NC = S // L
    H = N_HEADS
    dh = D // H

    def body(q_ref, k_ref, v_ref, gc_ref, gr_ref, bifr_ref, bifc_ref, hm_ref, dhm_ref, cst_ref, nst_ref, dz_any,
             dqk_ref, dg_ref, dv_ref, dC, dN):
        del dz_any
        c = pl.program_id(0)

        @pl.when(c == 0)
        def _():
            dC[...] = jnp.zeros_like(dC)
            dN[...] = jnp.zeros_like(dN)

        gc, gr, bcol_all, brow_all, ii, jj = _ml_gates(gc_ref, gr_ref.at[0], bifr_ref, bifc_ref, L)
        eye = ii == jj
        lane = lax.broadcasted_iota(jnp.int32, (L, IF_PAD), 1)
        db_all = jnp.zeros((L, IF_PAD), F32)
        dli_all = jnp.zeros((L, IF_PAD), F32)
        last_row = lax.broadcasted_iota(jnp.int32, (L, 1), 0) == L - 1
        for h in range(H):
            hs = pl.ds(h * dh, dh)
            q = q_ref[:, hs]
            k = k_ref[:, hs]
            v = v_ref[:, hs]
            qf = q.astype(F32)
            kf = k.astype(F32)
            b_c = bcol_all[:, H + h:H + h + 1]
            li_c = gc[:, h:h + 1]
            b_r = brow_all[H + h:H + h + 1, :]
            li_r = gr[h:h + 1, :]
            n_row = nst_ref[0, h, 0:1, :]
            m = nst_ref[0, h, 1:2, 0:1]
            Cb = cst_ref[0, h]
            qk, w_intra, w_inter, s, qn, den, hdiv, m_row = _ml_intra(q, k, b_c, b_r, li_r, m, n_row, ii, jj)
            dh_ = dhm_ref[:, hs]
            hh = hm_ref[:, hs]
            dnum = dh_ / hdiv
            dhdiv = -jnp.sum(dh_ * hh, axis=-1, keepdims=True) / hdiv
            dden = jnp.where(jnp.abs(den) > jnp.exp(-m_row), dhdiv * jnp.sign(den), 0.0)
            dnum_b = dnum.astype(BF16)
            ds = _dot_nt(dnum_b, v) + dden
            s_b = s.astype(BF16)
            dv = _dot_tn(s_b, dnum_b)
            dnC = _dot_nt(dnum_b, Cb)
            dwi = dden * w_inter
            dw_inter = jnp.sum(qf * dnC, axis=-1, keepdims=True) + dden * qn
            dq = w_inter * dnC + dwi * n_row
            dC_out = _dot_tn((w_inter * qf).astype(BF16), dnum_b)
            dn_out = jnp.sum(dwi * qf, axis=0, keepdims=True)
            dqk = (ds * w_intra).astype(BF16)
            dq = dq + _dot_nn(dqk, k)
            dk = _dot_tn(dqk, q)
            dd = ds * s
            rs_c = jnp.sum(dd, axis=1, keepdims=True)
            cs_r = jnp.sum(dd, axis=0, keepdims=True)
            cs_c = jnp.sum(jnp.where(eye, cs_r, 0.0), axis=1, keepdims=True)
            dinter = dw_inter * w_inter
            m_new, decay, wk = _ml_state(b_c, li_c, m, L)
            dCn = dC[h]
            dCn_b = dCn.astype(BF16)
            dn_new = dN[h, 0:1, :]
            ddecay = (jnp.sum(jnp.sum(dCn * Cb.astype(F32), axis=1, keepdims=True), axis=0, keepdims=True)
                      + jnp.sum(dn_new * n_row, axis=1, keepdims=True))
            dwkk = _dot_nt(v, dCn_b) + dn_new
            wkk_b = (wk * kf).astype(BF16)
            dv = dv + _dot_nn(wkk_b, dCn_b)
            dk = dk + wk * dwkk
            dg = jnp.sum(dwkk * kf, axis=-1, keepdims=True) * wk
            db_last = ddecay * decay + jnp.sum(dg, axis=0, keepdims=True)
            dC[h] = decay * dCn + dC_out
            dN[h, 0:1, :] = decay * dn_new + dn_out
            db_c = rs_c - cs_c + dinter - dg + jnp.where(last_row, db_last, 0.0)
            dli_c = cs_c + dg
            db_all = jnp.where(lane == H + h, db_c, db_all)
            dli_all = jnp.where(lane == h, dli_c, dli_all)
            dqk_ref[:, hs] = dq.astype(BF16)
            dqk_ref[:, pl.ds(D + h * dh, dh)] = dk.astype(BF16)
            dv_ref[:, hs] = dv.astype(BF16)
        triu = (ii <= jj).astype(BF16)
        dlf = _tri_left(triu, db_all)
        dg_ref[...] = dli_all + dlf * (1.0 - _sig(gc))

    r = lambda c: NC - 1 - c
    n_in = 12
    return pl.pallas_call(
        body, name="mlstm_bwd", grid=(NC,),
        in_specs=[pl.BlockSpec((L, D), lambda c: (r(c), 0)), pl.BlockSpec((L, D), lambda c: (r(c), 1)),
                  pl.BlockSpec((L, D), lambda c: (r(c), 5)),
                  pl.BlockSpec((L, IF_PAD), lambda c: (r(c), 0)),
                  pl.BlockSpec((1, 16, L), lambda c: (r(c), 0, 0)),
                  pl.BlockSpec((1, IF_PAD), lambda c: (0, 0)), pl.BlockSpec((16, 1), lambda c: (0, 0)),
                  pl.BlockSpec((L, D), lambda c: (r(c), 0)), pl.BlockSpec((L, D), lambda c: (r(c), 0)),
                  pl.BlockSpec((1, H, dh, dh), lambda c: (r(c), 0, 0, 0)),
                  pl.BlockSpec((1, H, 8, dh), lambda c: (r(c), 0, 0, 0)),
                  pl.BlockSpec(memory_space=pl.ANY)],
        out_specs=[pl.BlockSpec((L, 2 * D), lambda c: (r(c), 0)),
                   pl.BlockSpec((L, IF_PAD), lambda c: (r(c), 0)),
                   pl.BlockSpec((L, D), lambda c: (r(c), 5))],
        out_shape=[jax.ShapeDtypeStruct((S, 2 * D), BF16),
                   jax.ShapeDtypeStruct((S, IF_PAD), F32), jax.ShapeDtypeStruct(dz.shape, dz.dtype)],
        input_output_aliases={n_in - 1: 2},
        scratch_shapes=[pltpu.VMEM((H, dh, dh), F32), pltpu.VMEM((H, 8, dh), F32)],
        compiler_params=_cparams(("arbitrary",)),
    )(qks, qks, z, gcol, grow, bif_row, bif_col, hm, dhm, cst, nst, dz)


def _ml_post_fwd(hm, z, g_head, D):
    S = hm.shape[0]
    T = min(TOK_TILE, S)
    dh = D // N_HEADS

    def body(hm_ref, o_ref, zm_ref, g_ref, pm_ref):
        for h in range(N_HEADS):
            hs = pl.ds(h * dh, dh)
            hg = _sig(o_ref[:, hs].astype(F32)) * hm_ref[:, hs]
            rs = lax.rsqrt(jnp.mean(hg * hg, axis=-1, keepdims=True) + EPS)
            pm_ref[:, hs] = (hg * rs * g_ref[:, hs] * _silu(zm_ref[:, hs].astype(F32))).astype(BF16)

    return pl.pallas_call(
        body, name="ml_post_fwd", grid=(S // T,),
        in_specs=[pl.BlockSpec((T, D), lambda i: (i, 0)), pl.BlockSpec((T, D), lambda i: (i, 6)),
                  pl.BlockSpec((T, D), lambda i: (i, 7)), pl.BlockSpec((1, D), lambda i: (0, 0))],
        out_specs=pl.BlockSpec((T, D), lambda i: (i, 0)),
        out_shape=jax.ShapeDtypeStruct((S, D), BF16),
        compiler_params=_cparams(("parallel",)),
    )(hm, z, z, g_head)


def _ml_post_bwd(dpm, hm, z, g_head, dz, D):
    S = hm.shape[0]
    T = min(TOK_TILE, S)
    dh = D // N_HEADS

    def body(dpm_ref, hm_ref, o_ref, zm_ref, g_ref, dz_any, dhm_ref, dg_ref, dozm_ref):
        del dz_any
        do_ref = dozm_ref.at[:, pl.ds(0, D)]
        dzm_ref = dozm_ref.at[:, pl.ds(D, D)]
        i = pl.program_id(0)

        @pl.when(i == 0)
        def _():
            dg_ref[...] = jnp.zeros_like(dg_ref)

        for h in range(N_HEADS):
            hs = pl.ds(h * dh, dh)
            o = o_ref[:, hs].astype(F32)
            zm = zm_ref[:, hs].astype(F32)
            hmv = hm_ref[:, hs]
            dpm_ = dpm_ref[:, hs].astype(F32)
            g = g_ref[:, hs]
            so = _sig(o)
            hg = so * hmv
            rs = lax.rsqrt(jnp.mean(hg * hg, axis=-1, keepdims=True) + EPS)
            xn = hg * rs
            dzm_ref[:, hs] = (dpm_ * xn * g * _dsilu(zm)).astype(BF16)
            dhn = dpm_ * _silu(zm)
            dg_ref[0:1, hs] += jnp.sum(dhn * xn, axis=0, keepdims=True)
            dxn = dhn * g
            dhg = rs * (dxn - xn * jnp.mean(dxn * xn, axis=-1, keepdims=True))
            do_ref[:, hs] = (dhg * hmv * so * (1.0 - so)).astype(BF16)
            dhm_ref[:, hs] = dhg * so

    return pl.pallas_call(
        body, name="ml_post_bwd", grid=(S // T,),
        in_specs=[pl.BlockSpec((T, D), lambda i: (i, 0)), pl.BlockSpec((T, D), lambda i: (i, 0)),
                  pl.BlockSpec((T, D), lambda i: (i, 6)), pl.BlockSpec((T, D), lambda i: (i, 7)),
                  pl.BlockSpec((1, D), lambda i: (0, 0)), pl.BlockSpec(memory_space=pl.ANY)],
        out_specs=[pl.BlockSpec((T, D), lambda i: (i, 0)), pl.BlockSpec((8, D), lambda i: (0, 0)),
                   pl.BlockSpec((T, 2 * D), lambda i: (i, 3))],
        out_shape=[jax.ShapeDtypeStruct((S, D), F32), jax.ShapeDtypeStruct((8, D), F32),
                   jax.ShapeDtypeStruct(dz.shape, dz.dtype)],
        input_output_aliases={5: 2},
        compiler_params=_cparams(("arbitrary",)),
    )(dpm, hm, z, z, g_head, dz)


def _xattn_probs(q, km_h, scale):
    s = _dot_nt(q, km_h) * scale
    e = jnp.exp(s - jnp.max(s, axis=-1, keepdims=True))
    return e / jnp.sum(e, axis=-1, keepdims=True)


def _xattn_fwd(z, kv, D):
    S = z.shape[0]
    M = kv.shape[0]
    T = min(TOK_TILE, S)
    dh = D // N_HEADS
    scale = dh ** -0.5

    def body(q_ref, zx_ref, km_ref, vm_ref, px_ref):
        for h in range(N_HEADS):
            hs = pl.ds(h * dh, dh)
            p = _xattn_probs(q_ref[:, hs], km_ref[:, hs], scale)
            o = _dot_nn(p.astype(BF16), vm_ref[:, hs])
            px_ref[:, hs] = (o * _silu(zx_ref[:, hs].astype(F32))).astype(BF16)

    return pl.pallas_call(
        body, name="xattn_fwd", grid=(S // T,),
        in_specs=[pl.BlockSpec((T, D), lambda i: (i, 8)), pl.BlockSpec((T, D), lambda i: (i, 9)),
                  pl.BlockSpec((M, D), lambda i: (0, 0)), pl.BlockSpec((M, D), lambda i: (0, 1))],
        out_specs=pl.BlockSpec((T, D), lambda i: (i, 0)),
        out_shape=jax.ShapeDtypeStruct((S, D), BF16),
        compiler_params=_cparams(("parallel",)),
    )(z, z, kv, kv)


def _xattn_bwd(dpx, z, kv, dz, D):
    S = z.shape[0]
    M = kv.shape[0]
    T = min(TOK_TILE, S)
    dh = D // N_HEADS
    scale = dh ** -0.5

    def body(dpx_ref, q_ref, zx_ref, km_ref, vm_ref, dz_any, dkv_ref, dqz_ref):
        del dz_any
        i = pl.program_id(0)

        @pl.when(i == 0)
        def _():
            dkv_ref[...] = jnp.zeros_like(dkv_ref)

        for h in range(N_HEADS):
            hs = pl.ds(h * dh, dh)
            q = q_ref[:, hs]
            zx = zx_ref[:, hs].astype(F32)
            dpx_ = dpx_ref[:, hs].astype(F32)
            p = _xattn_probs(q, km_ref[:, hs], scale)
            p_b = p.astype(BF16)
            o = _dot_nn(p_b, vm_ref[:, hs])
            dqz_ref[:, pl.ds(D + h * dh, dh)] = (dpx_ * o * _dsilu(zx)).astype(BF16)
            do = (dpx_ * _silu(zx)).astype(BF16)
            dp = _dot_nt(do, vm_ref[:, hs])
            dsc = (p * (dp - jnp.sum(p * dp, axis=-1, keepdims=True)) * scale).astype(BF16)
            dqz_ref[:, hs] = _dot_nn(dsc, km_ref[:, hs]).astype(BF16)
            dkv_ref[:, hs] += _dot_tn(dsc, q)
            dkv_ref[:, pl.ds(D + h * dh, dh)] += _dot_tn(p_b, do)

    return pl.pallas_call(
        body, name="xattn_bwd", grid=(S // T,),
        in_specs=[pl.BlockSpec((T, D), lambda i: (i, 0)),
                  pl.BlockSpec((T, D), lambda i: (i, 8)), pl.BlockSpec((T, D), lambda i: (i, 9)),
                  pl.BlockSpec((M, D), lambda i: (0, 0)), pl.BlockSpec((M, D), lambda i: (0, 1)),
                  pl.BlockSpec(memory_space=pl.ANY)],
        out_specs=[pl.BlockSpec((M, 2 * D), lambda i: (0, 0)), pl.BlockSpec((T, 2 * D), lambda i: (i, 4))],
        out_shape=[jax.ShapeDtypeStruct((M, 2 * D), F32), jax.ShapeDtypeStruct(dz.shape, dz.dtype)],
        input_output_aliases={5: 1},
        compiler_params=_cparams(("arbitrary",)),
    )(dpx, z, z, kv, kv, dz)


def _col_sum(a):
    S, C = a.shape
    T = min(1024, S)

    def body(a_ref, o_ref):
        @pl.when(pl.program_id(0) == 0)
        def _():
            o_ref[...] = jnp.zeros_like(o_ref)

        o_ref[0:1, :] += jnp.sum(a_ref[...], axis=0, keepdims=True)

    return pl.pallas_call(
        body, name="col_sum", grid=(S // T,),
        in_specs=[pl.BlockSpec((T, C), lambda i: (i, 0))], out_specs=pl.BlockSpec((8, C), lambda i: (0, 0)),
        out_shape=jax.ShapeDtypeStruct((8, C), F32), compiler_params=_cparams(("arbitrary",)),
    )(a)


def _gain_grad(dn, xin, rstd):
    R, D = dn.shape

    def body(dn_ref, x_ref, r_ref, o_ref):
        o_ref[...] = jnp.zeros_like(o_ref)
        o_ref[0:1, :] = jnp.sum(dn_ref[...] * x_ref[...] * r_ref[...], axis=0, keepdims=True)

    return pl.pallas_call(
        body, name="gain_grad", out_shape=jax.ShapeDtypeStruct((8, D), F32),
        compiler_params=_cparams(),
    )(dn, xin, rstd)


def _merge_fwd(z, yc, ym, yx, D):
    S = z.shape[0]
    T = min(TOK_TILE, S)

    def body(g0, g1, g2, yc_ref, ym_ref, yx_ref, o_ref):
        o_ref[...] = (_sig(g0[...].astype(F32)) * yc_ref[...].astype(F32)
                      + _sig(g1[...].astype(F32)) * ym_ref[...].astype(F32)
                      + _sig(g2[...].astype(F32)) * yx_ref[...].astype(F32)).astype(BF16)

    tok = pl.BlockSpec((T, D), lambda i: (i, 0))
    return pl.pallas_call(
        body, name="merge_fwd", grid=(S // T,),
        in_specs=[pl.BlockSpec((T, D), lambda i: (i, 10)), pl.BlockSpec((T, D), lambda i: (i, 11)),
                  pl.BlockSpec((T, D), lambda i: (i, 12)), tok, tok, tok],
        out_specs=tok, out_shape=jax.ShapeDtypeStruct((S, D), BF16),
        compiler_params=_cparams(("parallel",)),
    )(z, z, z, yc, ym, yx)


def _merge_bwd(dmg, z, yc, ym, yx, D):
    S = z.shape[0]
    T = min(TOK_TILE // 2, S)

    def body(dm_ref, g_ref, yc_ref, ym_ref, yx_ref, dyc_ref, dym_ref, dyx_ref, dg_ref):
        j = pl.program_id(1)
        for jj, (y_ref, dy_ref) in enumerate(((yc_ref, dyc_ref), (ym_ref, dym_ref), (yx_ref, dyx_ref))):
            @pl.when(j == jj)
            def _():
                sg = _sig(g_ref[...].astype(F32))
                dm = dm_ref[...].astype(F32)
                dy_ref[...] = (dm * sg).astype(BF16)
                dg_ref[...] = (dm * y_ref[...].astype(F32) * sg * (1.0 - sg)).astype(BF16)

    tok = pl.BlockSpec((T, D), lambda i, j: (i, 0))
    return pl.pallas_call(
        body, name="merge_bwd", grid=(S // T, 3),
        in_specs=[tok, pl.BlockSpec((T, D), lambda i, j: (i, 10 + j)), tok, tok, tok],
        out_specs=[tok, tok, tok, pl.BlockSpec((T, D), lambda i, j: (i, 10 + j))],
        out_shape=[jax.ShapeDtypeStruct((S, D), BF16)] * 3 + [jax.ShapeDtypeStruct((S, 13 * D), BF16)],
        compiler_params=_cparams(("arbitrary", "arbitrary")),
    )(dmg, z, yc, ym, yx)


def _post_loss(r, x, tgt, g_post):
    S, D = r.shape
    T = min(TOK_TILE, S)

    def body(r_ref, x_ref, t_ref, g_ref, dy_ref, dr_ref, dg_ref, loss_ref):
        i = pl.program_id(0)

        @pl.when(i == 0)
        def _():
            dg_ref[...] = jnp.zeros_like(dg_ref)
            loss_ref[...] = jnp.zeros_like(loss_ref)

        rv = r_ref[...]
        g = g_ref[...]
        rs = lax.rsqrt(jnp.mean(rv * rv, axis=-1, keepdims=True) + EPS)
        nrm = rv * rs
        e = x_ref[...] + nrm * g - t_ref[...]
        part = jnp.sum(jnp.sum(e * e, axis=-1, keepdims=True), axis=0, keepdims=True) * (0.5 / D)
        loss_ref[0:1, 0:1] += part
        dy = e * (1.0 / D)
        dy_ref[...] = dy
        dg_ref[0:1, :] += jnp.sum(dy * nrm, axis=0, keepdims=True)
        dn = dy * g
        dr_ref[...] = (rs * (dn - nrm * jnp.mean(dn * nrm, axis=-1, keepdims=True))).astype(BF16)

    tok = pl.BlockSpec((T, D), lambda i: (i, 0))
    return pl.pallas_call(
        body, name="post_loss", grid=(S // T,),
        in_specs=[tok, tok, tok, pl.BlockSpec((1, D), lambda i: (0, 0))],
        out_specs=[tok, tok, pl.BlockSpec((8, D), lambda i: (0, 0)), pl.BlockSpec((8, 128), lambda i: (0, 0))],
        out_shape=[jax.ShapeDtypeStruct((S, D), F32), jax.ShapeDtypeStruct((S, D), BF16),
                   jax.ShapeDtypeStruct((8, D), F32), jax.ShapeDtypeStruct((8, 128), F32)],
        compiler_params=_cparams(("arbitrary",)),
    )(r, x, tgt, g_post)


def _conv_bwd1(dpc, z, u1, g_ln, b_ln, dz, D):
    S = z.shape[0]
    T = min(TOK_TILE // 2, S)

    def body(dpc_ref, zc_ref, u1_ref, gln_ref, bln_ref, dz_any, du1_ref, st_ref, dzc_ref):
        del dz_any
        i = pl.program_id(0)

        @pl.when(i == 0)
        def _():
            st_ref[...] = jnp.zeros_like(st_ref)

        u1v = u1_ref[...]
        mu = jnp.mean(u1v, axis=-1, keepdims=True)
        xc = u1v - mu
        rs = lax.rsqrt(jnp.mean(xc * xc, axis=-1, keepdims=True) + EPS)
        xh = xc * rs
        g = gln_ref[...]
        ln = xh * g + bln_ref[...]
        zc = zc_ref[...].astype(F32)
        dpc_ = dpc_ref[...].astype(F32)
        dzc_ref[...] = (dpc_ * _silu(ln) * _dsilu(zc)).astype(BF16)
        dln = dpc_ * _silu(zc) * _dsilu(ln)
        st_ref[0:1, :] += jnp.sum(dln * xh, axis=0, keepdims=True)
        st_ref[1:2, :] += jnp.sum(dln, axis=0, keepdims=True)
        dxh = dln * g
        du1 = rs * (dxh - jnp.mean(dxh, axis=-1, keepdims=True) - xh * jnp.mean(dxh * xh, axis=-1, keepdims=True))
        du1_ref[...] = du1
        st_ref[2:3, :] += jnp.sum(du1, axis=0, keepdims=True)

    tok = pl.BlockSpec((T, D), lambda i: (i, 0))
    vec = pl.BlockSpec((1, D), lambda i: (0, 0))
    return pl.pallas_call(
        body, name="conv_bwd1", grid=(S // T,),
        in_specs=[tok, pl.BlockSpec((T, D), lambda i: (i, 2)), tok, vec, vec, pl.BlockSpec(memory_space=pl.ANY)],
        out_specs=[tok, pl.BlockSpec((8, D), lambda i: (0, 0)), pl.BlockSpec((T, D), lambda i: (i, 2))],
        out_shape=[jax.ShapeDtypeStruct((S, D), F32), jax.ShapeDtypeStruct((8, D), F32),
                   jax.ShapeDtypeStruct(dz.shape, dz.dtype)],
        input_output_aliases={5: 2},
        compiler_params=_cparams(("arbitrary",)),
    )(dpc, z, u1, g_ln, b_ln, dz)


def _conv_bwd2(du1, z, w_dw, dz, D):
    S = z.shape[0]
    T = min(TOK_TILE, S)
    hb = T // HALO
    last = S // HALO - 1

    n_steps = S // T

    def body(du_ref, dun_ref, a_ref, b_ref, ah_ref, bh_ref, w_ref, dz_any, dw_ref, dab_ref,
             dus, u0s, shd, shu, dwp):
        del dz_any
        i = pl.program_id(0)

        @pl.when(i == 0)
        def _():
            dwp[...] = jnp.zeros_like(dwp)

        dus[pl.ds(0, T), :] = du_ref[...]
        dus[pl.ds(T, HALO), :] = jnp.where(i < n_steps - 1, dun_ref[...], 0.0)
        dus[pl.ds(T + HALO, 8), :] = jnp.zeros((8, D), F32)
        u0s[pl.ds(HALO, T), :] = a_ref[...].astype(F32) * _sig(b_ref[...].astype(F32))
        halo = ah_ref[...].astype(F32) * _sig(bh_ref[...].astype(F32))
        u0s[pl.ds(0, HALO), :] = jnp.where(i > 0, halo, 0.0)
        u0s[pl.ds(T + HALO, 8), :] = jnp.zeros((8, D), F32)
        for c0 in range(0, D, LANE):
            cs = pl.ds(c0, LANE)
            _shift_strip(dus, shd, cs, T + HALO)
            _shift_strip(u0s, shu, cs, T + HALO)
            for r0 in range(0, T, ROW_BLK):
                rows = pl.ds(r0, ROW_BLK)
                du0 = jnp.zeros((ROW_BLK, LANE), F32)
                for j in range(CONV_W):
                    du0 = du0 + w_ref[pl.ds(j, 1), cs] * _tap(dus, shd, cs, CONV_W - 1 - j, r0)
                a = a_ref[rows, cs].astype(F32)
                sb = _sig(b_ref[rows, cs].astype(F32))
                dab_ref[rows, cs] = (du0 * sb).astype(BF16)
                dab_ref[rows, pl.ds(D + c0, LANE)] = (du0 * a * sb * (1.0 - sb)).astype(BF16)
                d = dus[rows, cs]
                for j in range(CONV_W):
                    prod = d * _tap(u0s, shu, cs, HALO - (CONV_W - 1) + j, r0)
                    part = prod[0:8]
                    for q in range(8, ROW_BLK, 8):
                        part = part + prod[q:q + 8]
                    dwp[pl.ds(8 * j, 8), cs] += part

        @pl.when(i == n_steps - 1)
        def _():
            dw_ref[...] = jnp.zeros_like(dw_ref)
            for j in range(CONV_W):
                dw_ref[pl.ds(j, 1), :] = jnp.sum(dwp[pl.ds(8 * j, 8), :], axis=0, keepdims=True)

    tok = pl.BlockSpec((T, D), lambda i: (i, 0))
    nxt = lambda i: (jnp.minimum((i + 1) * hb, last), 0)
    prev = lambda i: (jnp.maximum(i * hb - 1, 0), 0)
    prev_b = lambda i: (jnp.maximum(i * hb - 1, 0), 1)
    return pl.pallas_call(
        body, name="conv_bwd2", grid=(S // T,),
        in_specs=[tok, pl.BlockSpec((HALO, D), nxt), tok, pl.BlockSpec((T, D), lambda i: (i, 1)),
                  pl.BlockSpec((HALO, D), prev), pl.BlockSpec((HALO, D), prev_b),
                  pl.BlockSpec((32, D), lambda i: (0, 0)), pl.BlockSpec(memory_space=pl.ANY)],
        out_specs=[pl.BlockSpec((32, D), lambda i: (0, 0)), pl.BlockSpec((T, 2 * D), lambda i: (i, 0))],
        out_shape=[jax.ShapeDtypeStruct((32, D), F32), jax.ShapeDtypeStruct(dz.shape, dz.dtype)],
        input_output_aliases={7: 1},
        scratch_shapes=[pltpu.VMEM((T + HALO + 8, D), F32), pltpu.VMEM((T + HALO + 8, D), F32),
                        pltpu.VMEM((8, T + HALO, LANE), F32), pltpu.VMEM((8, T + HALO, LANE), F32),
                        pltpu.VMEM((8 * 32, D), F32)],
        compiler_params=_cparams(("arbitrary",)),
    )(du1, du1, z, z, z, z, w_dw, dz)


def _qk_bwd1(dqks, z, w_qk, D):
    S = z.shape[0]
    T = min(TOK_TILE, S)
    hb = T // HALO
    scale = (D // N_HEADS) ** -0.5
    W2 = 2 * D

    n_steps = S // T

    def body(d_ref, p0_ref, p1_ref, h0_ref, h1_ref, w_ref, dc_ref, dw_ref, ps, sh, dwp):
        i = pl.program_id(0)

        @pl.when(i == 0)
        def _():
            dwp[...] = jnp.zeros_like(dwp)

        _qk_fill(ps, p0_ref, p1_ref, h0_ref, h1_ref, T, D)
        for c0 in range(0, W2, LANE):
            cs = pl.ds(c0, LANE)
            _shift_strip(ps, sh, cs, T + HALO, _QK_RES)
            for r0 in range(0, T, ROW_BLK):
                rows = pl.ds(r0, ROW_BLK)
                dc = d_ref[rows, cs].astype(F32) * _dsilu(_qk_conv(ps, sh, w_ref, cs, r0))
                if c0 >= D:
                    dc = dc * scale
                dc_ref[rows, cs] = dc.astype(BF16)
                for j in range(QK_W):
                    prod = dc * _tap(ps, sh, cs, HALO - (QK_W - 1) + j, r0)
                    part = prod[0:8]
                    for q in range(8, ROW_BLK, 8):
                        part = part + prod[q:q + 8]
                    dwp[pl.ds(8 * j, 8), cs] += part

        @pl.when(i == n_steps - 1)
        def _():
            dw_ref[...] = jnp.zeros_like(dw_ref)
            for j in range(QK_W):
                dw_ref[pl.ds(j, 1), :] = jnp.sum(dwp[pl.ds(8 * j, 8), :], axis=0, keepdims=True)

    prev0 = lambda i: (jnp.maximum(i * hb - 1, 0), 3)
    prev1 = lambda i: (jnp.maximum(i * hb - 1, 0), 4)
    return pl.pallas_call(
        body, name="qk_bwd1", grid=(S // T,),
        in_specs=[pl.BlockSpec((T, W2), lambda i: (i, 0)),
                  pl.BlockSpec((T, D), lambda i: (i, 3)), pl.BlockSpec((T, D), lambda i: (i, 4)),
                  pl.BlockSpec((HALO, D), prev0), pl.BlockSpec((HALO, D), prev1),
                  pl.BlockSpec((8, W2), lambda i: (0, 0))],
        out_specs=[pl.BlockSpec((T, W2), lambda i: (i, 0)), pl.BlockSpec((8, W2), lambda i: (0, 0))],
        out_shape=[jax.ShapeDtypeStruct((S, W2), BF16), jax.ShapeDtypeStruct((8, W2), F32)],
        scratch_shapes=[pltpu.VMEM((T + HALO + 8, W2), F32), pltpu.VMEM((8, T + HALO, LANE), F32),
                        pltpu.VMEM((8 * QK_W, W2), F32)],
        compiler_params=_cparams(("arbitrary",)),
    )(dqks, z, z, z, z, w_qk)


def _qk_bwd2(dc, w_qk, dz, D):
    S = dc.shape[0]
    T = min(TOK_TILE, S)
    hb = T // HALO
    last = S // HALO - 1

    n_steps = S // T

    def body(d_ref, dn_ref, w_ref, dz_any, o_ref, ds, sh):
        del dz_any
        i = pl.program_id(0)
        ds[pl.ds(0, T), :] = d_ref[...].astype(F32)
        ds[pl.ds(T, HALO), :] = jnp.where(i < n_steps - 1, dn_ref[...].astype(F32), 0.0)
        ds[pl.ds(T + HALO, 8), :] = jnp.zeros((8, D), F32)
        for c0 in range(0, D, LANE):
            cs = pl.ds(c0, LANE)
            _shift_strip(ds, sh, cs, T + HALO, range(1, QK_W))
            for r0 in range(0, T, ROW_BLK):
                acc = jnp.zeros((ROW_BLK, LANE), F32)
                for j in range(QK_W):
                    acc = acc + w_ref[pl.ds(j, 1), cs] * _tap(ds, sh, cs, QK_W - 1 - j, r0)
                o_ref[pl.ds(r0, ROW_BLK), cs] = acc.astype(BF16)

    return pl.pallas_call(
        body, name="qk_bwd2", grid=(S // T, 2),
        in_specs=[pl.BlockSpec((T, D), lambda i, j: (i, j)),
                  pl.BlockSpec((HALO, D), lambda i, j: (jnp.minimum((i + 1) * hb, last), j)),
                  pl.BlockSpec((8, D), lambda i, j: (0, j)), pl.BlockSpec(memory_space=pl.ANY)],
        out_specs=pl.BlockSpec((T, D), lambda i, j: (i, 3 + j)),
        out_shape=jax.ShapeDtypeStruct(dz.shape, dz.dtype),
        input_output_aliases={3: 0},
        scratch_shapes=[pltpu.VMEM((T + HALO + 8, D), F32), pltpu.VMEM((8, T + HALO, LANE), F32)],
        compiler_params=_cparams(("arbitrary", "arbitrary")),
    )(dc, dc, w_qk, dz)


TILE_BUDGET = 12 * 1024 * 1024


def _tile_2d(R, C, elem_bytes):
    cands = [t for t in range(BF16_ROWS, R + 1, BF16_ROWS) if R % t == 0 and t * C * elem_bytes <= TILE_BUDGET]
    if cands:
        return max(cands), C
    if R * C * elem_bytes <= TILE_BUDGET or C % LANE:
        return R, C
    cands = [t for t in range(LANE, C + 1, LANE) if C % t == 0 and t * R * elem_bytes <= TILE_BUDGET]
    return R, (max(cands) if cands else LANE)


def _sum_parts(parts, name):
    n, R, C = parts.shape
    tr, tc = _tile_2d(R, C, n * parts.dtype.itemsize + 4)

    def body(p_ref, o_ref):
        g = p_ref[0].astype(F32)
        for s in range(1, n):
            g = g + p_ref[s].astype(F32)
        o_ref[...] = g

    return pl.pallas_call(
        body, name=name, grid=(R // tr, C // tc),
        in_specs=[pl.BlockSpec((n, tr, tc), lambda i, j: (0, i, j))],
        out_specs=pl.BlockSpec((tr, tc), lambda i, j: (i, j)),
        out_shape=jax.ShapeDtypeStruct((R, C), F32),
        compiler_params=_cparams(("parallel", "parallel")),
    )(parts)


def _adamw(parts, w, m, v, name):
    R, C = w.shape
    n = parts.shape[0]
    tr, tc = _tile_2d(R, C, 4 * 7 + n * parts.dtype.itemsize)
    c1 = 1.0 / (1.0 - ADAM_B1 ** ADAM_STEP)
    c2 = 1.0 / (1.0 - ADAM_B2 ** ADAM_STEP)

    def body(p_ref, w_ref, m_ref, v_ref, g_ref, d_ref, nm_ref, nv_ref):
        g = p_ref[0].astype(F32)
        for s in range(1, n):
            g = g + p_ref[s].astype(F32)
        g_ref[...] = g
        mn = ADAM_B1 * m_ref[...] + (1.0 - ADAM_B1) * g
        vn = ADAM_B2 * v_ref[...] + (1.0 - ADAM_B2) * (g * g)
        nm_ref[...] = mn
        nv_ref[...] = vn
        d_ref[...] = -ADAM_LR * ((mn * c1) / (jnp.sqrt(vn * c2) + ADAM_EPS) + ADAM_WD * w_ref[...])

    blk = pl.BlockSpec((tr, tc), lambda i, j: (i, j))
    return pl.pallas_call(
        body, name=name, grid=(R // tr, C // tc),
        in_specs=[pl.BlockSpec((n, tr, tc), lambda i, j: (0, i, j)), blk, blk, blk],
        out_specs=[blk, blk, blk, blk],
        out_shape=[jax.ShapeDtypeStruct((R, C), F32)] * 4,
        compiler_params=_cparams(("parallel", "parallel")),
    )(parts, w, m, v)


def _exchange(operands, scatter, name):
    n = len(operands)

    def body(*refs):
        copies = _xchg_copies(refs[:n], refs[n:2 * n], *refs[2 * n:], scatter)
        for cp in copies:
            cp.start()
        for cp in copies:
            cp.wait()

    hbm = pl.BlockSpec(memory_space=pltpu.HBM)
    return pl.pallas_call(
        body, name=name, in_specs=[hbm] * n, out_specs=[hbm] * n, out_shape=_xchg_out_shapes(operands, scatter),
        scratch_shapes=_xchg_sems(n), compiler_params=pltpu.CompilerParams(has_side_effects=True),
    )(*operands)


N_CHIP = 4


class _CoreGather:
    def __init__(self, core):
        self.core = core

    def out_shape(self, blk):
        return jax.ShapeDtypeStruct((N_CHIP,) + blk.shape, blk.dtype)

    @staticmethod
    def sems():
        return [pltpu.SemaphoreType.DMA((7,)), pltpu.SemaphoreType.DMA((7,)), pltpu.SemaphoreType.DMA]

    def _parts(self, src, out, send_sems, recv_sems, loc_sem):
        x, y, c = lax.axis_index("x"), lax.axis_index("y"), lax.axis_index("c")
        me, sibling = (x, y, c), (x, y, 1 - c)
        chips = [(1 - x, y), (x, 1 - y), (1 - x, 1 - y)]

        def copy(k, chip, to, from_src=False):
            slot = out.at[2 * chip[0] + chip[1]]
            return pltpu.make_async_remote_copy(
                src_ref=src if from_src else slot, dst_ref=slot, send_sem=send_sems.at[k], recv_sem=recv_sems.at[k],
                device_id=to, device_id_type=pl.DeviceIdType.MESH)

        mine = pltpu.make_async_copy(src, out.at[2 * x + y], loc_sem)
        first = [copy(0, (x, y), sibling, True)] + [copy(1 + j, (x, y), (*ch, c), True) for j, ch in enumerate(chips)]
        landed = [copy(1 + j, ch, me) for j, ch in enumerate(chips)]
        passed = [copy(4 + j, ch, sibling) for j, ch in enumerate(chips)]
        from_sibling = [copy(0, (x, y), me)] + [copy(4 + j, ch, me) for j, ch in enumerate(chips)]
        return c == self.core, mine, first, landed, passed, from_sibling

    def start(self, *refs):
        holder, mine, first, _, _, _ = self._parts(*refs)

        @pl.when(holder)
        def _():
            mine.start()
            for cp in first:
                cp.start()

    def forward(self, *refs):
        holder, _, _, landed, passed, _ = self._parts(*refs)

        @pl.when(holder)
        def _():
            for got, cp in zip(landed, passed):
                got.wait_recv()
                cp.start()

    def finish(self, *refs):
        holder, mine, first, _, passed, from_sibling = self._parts(*refs)

        @pl.when(holder)
        def _():
            for cp in first + passed:
                cp.wait_send()
            mine.wait()

        @pl.when(jnp.logical_not(holder))
        def _():
            for cp in from_sibling:
                cp.wait_recv()


def _gather2_copies(srcs, dsts, send_sems, recv_sems, loc_sems):
    x, y, c = lax.axis_index("x"), lax.axis_index("y"), lax.axis_index("c")
    me, sibling = (x, y, c), (x, y, 1 - c)
    chips = [(1 - x, y), (x, 1 - y), (1 - x, 1 - y)]
    mine, first, landed, passed, arriving = [], [], [], [], []
    for t, (src, dst) in enumerate(zip(srcs, dsts)):
        def copy(k, block, to, from_src=False, src=src, dst=dst, t=t):
            slot = dst.at[4 * block[0] + 2 * block[1] + block[2]]
            return pltpu.make_async_remote_copy(
                src_ref=src if from_src else slot, dst_ref=slot,
                send_sem=send_sems.at[7 * t + k], recv_sem=recv_sems.at[7 * t + k],
                device_id=to, device_id_type=pl.DeviceIdType.MESH)

        mine.append(pltpu.make_async_copy(src, dst.at[4 * x + 2 * y + c], loc_sems.at[t]))
        first += [copy(0, me, sibling, True)] + [copy(1 + j, me, (*ch, c), True) for j, ch in enumerate(chips)]
        landed += [copy(1 + j, (*ch, c), me) for j, ch in enumerate(chips)]
        passed += [copy(4 + j, (*ch, c), sibling) for j, ch in enumerate(chips)]
        arriving += [copy(0, sibling, me)] + [copy(4 + j, (*ch, 1 - c), me) for j, ch in enumerate(chips)]
    return mine, first, landed, passed, arriving


def _core_gather(blk, core, name):
    g = _CoreGather(core)

    def body(*refs):
        g.start(*refs)
        g.forward(*refs)
        g.finish(*refs)

    hbm = pl.BlockSpec(memory_space=pltpu.HBM)
    return pl.pallas_call(
        body, name=name, in_specs=[hbm], out_specs=hbm, out_shape=g.out_shape(blk), scratch_shapes=g.sems(),
        compiler_params=pltpu.CompilerParams(has_side_effects=True),
    )(blk)


IN_TILE = 512
FIRST_CORE = 1


def _tile_index(tiles):
    jumps = [(i, tiles[i] - tiles[i - 1] - 1) for i in range(1, len(tiles)) if tiles[i] != tiles[i - 1] + 1]

    def f(t):
        j = t + tiles[0]
        for i, gap in jumps:
            j = j + jnp.where(t >= i, gap, 0)
        return j

    return f


def _in_proj_phase(h, w_rows, tiles, z_prev, name, gather=None, xchg=None):
    S, D = h.shape
    nt = len(tiles)
    assert w_rows.shape == (nt * IN_TILE, D)
    tm = _fit(S, 2048)
    ni = S // tm
    grid = (ni, nt)
    mid = (7 * ni * nt) // 8
    col = _tile_index(tiles)
    n_in = 2 + (z_prev is not None)
    gather = list(gather or [])
    g_ops = [blk for blk, _ in gather]
    x_ops = list(xchg[0]) if xchg else []
    n_g, n_x = len(g_ops), len(x_ops)

    def body(*refs):
        h_ref, w_ref = refs[0], refs[1]
        g_src = refs[n_in:n_in + n_g]
        x_src = refs[n_in + n_g:n_in + n_g + n_x]
        o = n_in + n_g + n_x
        z_ref = refs[o]
        g_dst = refs[o + 1:o + 1 + n_g]
        x_dst = refs[o + 1 + n_g:o + 1 + n_g + n_x]
        sems = refs[o + 1 + n_g + n_x:]
        g_sems, x_sems = sems[:3 * n_g], sems[3 * n_g:]
        i, t = pl.program_id(0), pl.program_id(1)
        first = (t == 0) & (i == 0)
        last = (t == nt - 1) & (i == ni - 1)

        def each_gather(step):
            for n, (_, how) in enumerate(gather):
                getattr(how, step)(g_src[n], g_dst[n], *g_sems[3 * n:3 * n + 3])

        @pl.when(first)
        def _():
            each_gather("start")
            if xchg:
                mine, out, _, _, _ = _gather2_copies(x_src, x_dst, *x_sems)
                for cp in mine + out:
                    cp.start()

        z_ref[...] = _dot_nt(h_ref[...], w_ref[...]).astype(z_ref.dtype)

        @pl.when((i == mid // nt) & (t == mid % nt))
        def _():
            each_gather("forward")
            if xchg:
                _, _, landed, passed, _ = _gather2_copies(x_src, x_dst, *x_sems)
                for got, cp in zip(landed, passed):
                    got.wait_recv()
                    cp.start()

        @pl.when(last)
        def _():
            each_gather("finish")
            if xchg:
                mine, out, _, passed, arriving = _gather2_copies(x_src, x_dst, *x_sems)
                for cp in arriving:
                    cp.wait_recv()
                for cp in out + passed:
                    cp.wait_send()
                for cp in mine:
                    cp.wait()

    hbm = pl.BlockSpec(memory_space=pltpu.HBM)
    in_specs = [pl.BlockSpec((tm, D), lambda i, t: (i, 0)), pl.BlockSpec((IN_TILE, D), lambda i, t: (t, 0))]
    args = [h, w_rows]
    aliases = {}
    if z_prev is not None:
        in_specs.append(pl.BlockSpec(memory_space=pl.ANY))
        args.append(z_prev)
        aliases = {2: 0}
    in_specs += [hbm] * (n_g + n_x)
    args += g_ops + x_ops
    out_specs = [pl.BlockSpec((tm, IN_TILE), lambda i, t: (i, col(t)))] + [hbm] * (n_g + n_x)
    out_shape = [jax.ShapeDtypeStruct((S, 13 * D), BF16)]
    scratch = []
    for blk, how in gather:
        out_shape.append(how.out_shape(blk))
        scratch += how.sems()
    if xchg:
        out_shape += _xchg_out_shapes(x_ops, xchg[1])
        scratch += _xchg_sems(n_x)
    comm = bool(gather or xchg)
    res = pl.pallas_call(
        body, name=name, grid=grid, in_specs=in_specs, out_specs=out_specs, out_shape=out_shape,
        input_output_aliases=aliases, scratch_shapes=scratch,
        compiler_params=pltpu.CompilerParams(vmem_limit_bytes=VMEM_LIMIT, has_side_effects=comm,
                                             dimension_semantics=("arbitrary", "arbitrary")),
    )(*args)
    z = res[0]
    g_out = list(res[1:1 + n_g])
    x_out = list(res[1 + n_g:]) if xchg else None
    return z, g_out, x_out


def _in_proj_dx(dz, wpT, dgates, wifT, x, rstd, dy, g, side=None):
    S, K = dz.shape
    D = wpT.shape[1]
    tm, tk = _fit(S, 512), _fit(K, 2048)
    nm, nk = S // tm, K // tk
    n_side = len(side[0]) if side else 0

    rc = min(tm, 128)

    def body(*refs):
        a_ref, b_ref, dgt_ref, wif_ref, x_hbm, r_ref, dy_hbm, g_ref = refs[:8]
        srcs = refs[8:8 + n_side]
        dx_ref, dg_ref = refs[8 + n_side], refs[9 + n_side]
        dsts = refs[10 + n_side:10 + 2 * n_side]
        acc, xbuf, dybuf, ep_sems = refs[10 + 2 * n_side:14 + 2 * n_side]
        sems = refs[14 + 2 * n_side:]
        i, k = pl.program_id(0), pl.program_id(1)
        rows = pl.ds(pl.multiple_of(i * tm, tm), tm)
        fetch = [pltpu.make_async_copy(x_hbm.at[rows, :], xbuf, ep_sems.at[0]),
                 pltpu.make_async_copy(dy_hbm.at[rows, :], dybuf, ep_sems.at[1])]

        @pl.when((i == 0) & (k == 0))
        def _():
            dg_ref[...] = jnp.zeros_like(dg_ref)
            if n_side:
                for cp in _xchg_copies(srcs, dsts, *sems, side[1]):
                    cp.start()

        @pl.when(k == nk - 2)
        def _():
            for cp in fetch:
                cp.start()

        p = _dot_nn(a_ref[...], b_ref[...])

        @pl.when(k == 0)
        def _():
            acc[...] = p

        @pl.when((k > 0) & (k < nk - 1))
        def _():
            acc[...] += p

        @pl.when(k == nk - 1)
        def _():
            acc[...] += p + _dot_nn(dgt_ref[...], wif_ref[...])
            for cp in fetch:
                cp.wait()
            for r0 in range(0, tm, rc):
                rr = pl.ds(r0, rc)
                dh = acc[rr, :]
                rs = r_ref[rr, :]
                xn = xbuf[rr, :] * rs
                dg_ref[0:1, :] += jnp.sum(dh * xn, axis=0, keepdims=True)
                dxn = dh * g_ref[...]
                dx_ref[rr, :] = dybuf[rr, :] + rs * (dxn - xn * jnp.mean(dxn * xn, axis=-1, keepdims=True))

        if n_side:
            @pl.when((i == nm - 1) & (k == nk - 1))
            def _():
                for cp in _xchg_copies(srcs, dsts, *sems, side[1]):
                    cp.wait()

    assert nk > 1
    hbm = pl.BlockSpec(memory_space=pltpu.HBM)
    tok = pl.BlockSpec((tm, D), lambda i, k: (i, 0))
    in_specs = [pl.BlockSpec((tm, tk), lambda i, k: (i, k)), pl.BlockSpec((tk, D), lambda i, k: (k, 0)),
                pl.BlockSpec((tm, IF_PAD), lambda i, k: (i, 0)), pl.BlockSpec((IF_PAD, D), lambda i, k: (0, 0)),
                hbm, pl.BlockSpec((tm, 1), lambda i, k: (i, 0)), hbm, pl.BlockSpec((1, D), lambda i, k: (0, 0))]
    out_specs = [tok, pl.BlockSpec((8, D), lambda i, k: (0, 0))]
    out_shape = [jax.ShapeDtypeStruct((S, D), F32), jax.ShapeDtypeStruct((8, D), F32)]
    scratch = [pltpu.VMEM((tm, D), F32), pltpu.VMEM((tm, D), F32), pltpu.VMEM((tm, D), F32),
               pltpu.SemaphoreType.DMA((2,))]
    args = [dz, wpT, dgates, wifT, x, rstd, dy, g]
    if n_side:
        in_specs += [hbm] * n_side
        args += list(side[0])
        out_specs += [hbm] * n_side
        out_shape += _xchg_out_shapes(side[0], side[1])
        scratch += _xchg_sems(n_side)
    res = pl.pallas_call(
        body, name="in_proj_dx", grid=(nm, nk), in_specs=in_specs, out_specs=out_specs, out_shape=out_shape,
        scratch_shapes=scratch,
        compiler_params=pltpu.CompilerParams(vmem_limit_bytes=VMEM_LIMIT, has_side_effects=bool(n_side),
                                             dimension_semantics=("arbitrary", "arbitrary")),
    )(*args)
    return res[0], res[1], list(res[2:])


LATE = ('w_conv_out', 'w_ml_out', 'w_xa_out', 'w_out', 'w_mem_kv', 'w_qk_conv', 'w_dw')
ROW_SHARDED = ('w_conv_out', 'w_ml_out', 'w_xa_out', 'w_out')
COL_SHARDED = ('w_in', 'w_mem_kv', 'w_qk_conv', 'w_dw')


def _cols_to_slabs(g):
    R, C8 = g.shape
    return g.reshape(R, N_DEV, C8 // N_DEV).transpose(1, 0, 2)


def _slabs_to_cols(a):
    n, R, C = a.shape
    return a.transpose(1, 0, 2).reshape(R, n * C)


def _assemble(name, slabs):
    if name in ROW_SHARDED:
        return slabs.reshape(slabs.shape[0] * slabs.shape[1], slabs.shape[2])
    return _slabs_to_cols(slabs)


def _to_slabs(name, g, dtype):
    if name in ROW_SHARDED:
        return g.reshape(N_DEV, g.shape[0] // N_DEV, g.shape[1]).astype(dtype)
    return _cols_to_slabs(g).astype(dtype)


def _w_in_rows(D):
    n = (13 * D + N_IF) // N_DEV
    gate_dev, gate_row = (8 * D) // n, (8 * D) % n
    assert gate_row + N_IF <= n
    height = -(-(n + BF16_ROWS - 1) // BF16_ROWS) * BF16_ROWS

    def first(p):
        return n * p - jnp.where(n * p >= 8 * D + N_IF, N_IF, 0)

    def start(p):
        return jnp.minimum((first(p) // BF16_ROWS) * BF16_ROWS, 13 * D - height)

    return n, gate_dev, gate_row, height, first, start


def _w_in_tiles(D):
    n, gate_dev, *_ = _w_in_rows(D)

    def rows(s):
        f = n * s - (N_IF if n * s >= 8 * D + N_IF else 0)
        return f, f + n - (N_IF if s == gate_dev else 0)

    tiles = range(13 * D // IN_TILE)
    inside = lambda j, s: rows(s)[0] <= j * IN_TILE and (j + 1) * IN_TILE <= rows(s)[1]
    early = [j for j in tiles if any(inside(j, s) for s in range(FIRST_CORE, N_DEV, 2))]
    return early, [j for j in tiles if j not in early], rows


def _w_window_height(D):
    n = (13 * D + N_IF) // N_DEV
    return -(-(n + IN_TILE - 1) // IN_TILE) * IN_TILE


def _w_window(block_t, me, D):
    n, gate_dev, gate_row, _, first, _ = _w_in_rows(D)
    no_gates = jnp.concatenate([block_t[:gate_row], block_t[gate_row + N_IF:], jnp.zeros((N_IF, D), block_t.dtype)])
    mine = jnp.where(me == gate_dev, no_gates, block_t)
    off = first(me) % IN_TILE
    return lax.dynamic_update_slice(jnp.zeros((_w_window_height(D), D), block_t.dtype), mine, (off, 0))


def _w_window_parts(window, me, D):
    *_, first, _ = _w_in_rows(D)
    n_win = _w_window_height(D) // IN_TILE
    lead = jnp.where(first(me) % IN_TILE == 0, 0, 1)
    core = lax.dynamic_slice(window, (lead * IN_TILE, 0), ((n_win - 2) * IN_TILE, D))
    low = jnp.where(lead == 0, window[(n_win - 2) * IN_TILE:(n_win - 1) * IN_TILE], window[:IN_TILE])
    return core, jnp.concatenate([low, window[(n_win - 1) * IN_TILE:]], axis=0)


def _w_segments(s, D, window=None, core=None, edges=None):
    _, _, rows = _w_in_tiles(D)
    n_win = _w_window_height(D) // IN_TILE
    t0, lead = rows(s)[0] // IN_TILE, 0 if rows(s)[0] % IN_TILE == 0 else 1
    segs = []
    if window is not None:
        segs.append((t0, window))
    if core is not None:
        segs.append((t0 + lead, core))
    if edges is not None:
        segs += [(t0 + (n_win - 2 if lead == 0 else 0), edges[:IN_TILE]), (t0 + n_win - 1, edges[IN_TILE:])]
    return segs


def _w_rows_of_tiles(tiles, segments, D):
    _, _, rows = _w_in_tiles(D)

    def find(s, j):
        for t0, arr in segments[s]:
            if t0 <= j < t0 + arr.shape[0] // IN_TILE:
                return arr, j - t0
        raise KeyError((s, j))

    runs = []
    for j in tiles:
        own = [s for s in range(N_DEV) if rows(s)[0] < (j + 1) * IN_TILE and j * IN_TILE < rows(s)[1]]
        got = [find(s, j) for s in own]
        last = runs[-1] if runs else None
        if (last and len(got) == 1 and len(last[0]) == 1 and last[2] == j and last[0][0][0] is got[0][0]
                and last[0][0][1] + last[1] == got[0][1]):
            last[1], last[2] = last[1] + 1, j + 1
        else:
            runs.append([got, 1, j + 1])
    out = [functools.reduce(jnp.add, [arr[i * IN_TILE:(i + k) * IN_TILE] for arr, i in got]) for got, k, _ in runs]
    return jnp.concatenate(out, axis=0)


def _interleave_tiles(parts):
    where = {j: (rows, i) for rows, tiles in parts for i, j in enumerate(tiles)}
    out, j, n_tiles = [], 0, len(where)
    while j < n_tiles:
        rows, i = where[j]
        k = 1
        while j + k in where and where[j + k][0] is rows and where[j + k][1] == i + k:
            k += 1
        out.append(rows[i * IN_TILE:(i + k) * IN_TILE])
        j += k
    return jnp.concatenate(out, axis=0)


def _local_step(x, mem, tgt, g_pre, w_early, w_rest, wifT, b_if, b_dw, g_ln, b_ln, g_head, g_mem, g_post, late, dist):
    S, D = x.shape
    L = min(ML_CHUNK, S)
    NC = S // L
    bif_row = jnp.zeros((1, IF_PAD), F32).at[0, :N_IF].set(b_if)
    bif_col = jnp.zeros((16, 1), F32).at[:N_IF, 0].set(b_if)
    early, rest, _ = _w_in_tiles(D)

    h, rstd = _rmsnorm_fwd(x, g_pre, "prenorm_fwd")
    if dist:
        my_window, my_edges, rows_of_rest = w_rest
        z, (north, south_edges), _ = _in_proj_phase(
            h, w_early, early, None, "in_proj_early",
            gather=[(my_window, _CoreGather(1 - FIRST_CORE)), (my_edges, _CoreGather(FIRST_CORE))])
        w_rest = rows_of_rest(north, south_edges)
        z, _, gathered = _in_proj_phase(h, w_rest, rest, z, "in_proj_rest",
                                        xchg=([late[n] for n in LATE] + [wifT], False))
        full = {n: _assemble(n, a) for n, a in zip(LATE, gathered)}
        wifT = jnp.zeros((IF_PAD, D), BF16).at[:N_IF].set(gathered[-1][_w_in_rows(D)[1], :N_IF])
    else:
        z, _, _ = _in_proj_phase(h, w_early, early, None, "in_proj_early")
        z, _, _ = _in_proj_phase(h, w_rest, rest, z, "in_proj_rest")
        full = late
    wpT = _interleave_tiles([(w_early, early), (w_rest, rest)])
    w_co, w_mo, w_xo, w_out, w_kv = (full[n] for n in LATE[:5])
    w_qk = jnp.zeros((8, 2 * D), F32).at[:QK_W].set(full['w_qk_conv'])
    w_dw = jnp.zeros((32, D), F32).at[:CONV_W].set(full['w_dw'])
    gcol = _mm(h, wifT, mode="nt", out_dtype=F32, tm=1024, tn=IF_PAD, tk=2048, name="gates_col")
    grow = _mm(wifT[:16], h, mode="nt", out_dtype=F32, tm=16, tn=1024, tk=2048, name="gates_row")
    grow = grow.reshape(16, NC, L).transpose(1, 0, 2)

    u1, pc = _conv_fwd(z, w_dw, b_dw, g_ln, b_ln, D)
    qks = _qk_fwd(z, w_qk, D)
    hm, cst, nst = _mlstm_fwd(qks, z, gcol, grow, bif_row, bif_col, D)
    pm = _ml_post_fwd(hm, z, g_head, D)
    memn, rstd_mem = _rmsnorm_fwd(mem, g_mem, "memnorm_fwd")
    kv = _mm(memn, w_kv, mode="nn", out_dtype=BF16, tm=256, tn=1024, tk=2048, name="mem_kv")
    px = _xattn_fwd(z, kv, D)

    proj = functools.partial(_mm, mode="nn", tm=1024, tn=1024, tk=2048)
    yc = proj(pc, w_co, out_dtype=BF16, name="conv_out")
    ym = proj(pm, w_mo, out_dtype=BF16, name="ml_out")
    yx = proj(px, w_xo, out_dtype=BF16, name="xa_out")
    mg = _merge_fwd(z, yc, ym, yx, D)
    r = proj(mg, w_out, out_dtype=F32, name="out_proj")
    dy, dr, dg_post, loss = _post_loss(r, x, tgt, g_post)

    dgrad = functools.partial(_mm, mode="nt", out_dtype=BF16, tm=1024, tn=1024, tk=2048)
    wgrad = functools.partial(_mm, mode="tn", out_dtype=F32, tm=1024, tn=1024, tk=2048)
    dmg = dgrad(dr, w_out, name="out_proj_dx")
    dw_out = wgrad(mg, dr, name="out_proj_dw")
    dyc, dym, dyx, dz = _merge_bwd(dmg, z, yc, ym, yx, D)
    dpc = dgrad(dyc, w_co, name="conv_out_dx")
    dw_co = wgrad(pc, dyc, name="conv_out_dw")
    dpm = dgrad(dym, w_mo, name="ml_out_dx")
    dw_mo = wgrad(pm, dym, name="ml_out_dw")
    dpx = dgrad(dyx, w_xo, name="xa_out_dx")
    dw_xo = wgrad(px, dyx, name="xa_out_dw")

    dkv, dz = _xattn_bwd(dpx, z, kv, dz, D)
    dkv_b = dkv.astype(BF16)
    dw_kv = wgrad(memn, dkv_b, name="mem_kv_dw")
    dmemn = _mm(dkv_b, w_kv, mode="nt", out_dtype=F32, tm=256, tn=1024, tk=2048, name="mem_kv_dx")
    dg_mem = _gain_grad(dmemn, mem, rstd_mem)

    du1, cstats, dz = _conv_bwd1(dpc, z, u1, g_ln, b_ln, dz, D)
    dw_dw, dz = _conv_bwd2(du1, z, w_dw, dz, D)

    dhm, dg_head, dz = _ml_post_bwd(dpm, hm, z, g_head, dz, D)
    dqks, dgates, dz = _mlstm_bwd(qks, z, gcol, grow, bif_row, bif_col, hm, dhm, cst, nst, dz, D)
    dc, dw_qk = _qk_bwd1(dqks, z, w_qk, D)
    dz = _qk_bwd2(dc, w_qk, dz, D)

    g = dict(w_conv_out=dw_co, w_ml_out=dw_mo, w_xa_out=dw_xo, w_out=dw_out, w_mem_kv=dw_kv,
             w_qk_conv=dw_qk[:QK_W], w_dw=dw_dw[:CONV_W])
    dgates_b = dgates.astype(BF16)
    in_proj_dw = functools.partial(_mm, dz, h, mode="tn", out_dtype=BF16 if dist else F32, tm=1024, tn=2048, tk=2048,
                                   name="in_proj_dw")
    in_proj_dx = functools.partial(_in_proj_dx, dz, wpT, dgates_b, wifT, x, rstd, dy, g_pre)
    parts = {}
    if dist:
        send = [_to_slabs(n, g[n], BF16 if n in LATE[:5] else F32) for n in LATE]
        dwpT, recv = in_proj_dw(side=(send, True))
        parts.update(zip(LATE, recv))
        *_, height, _, start = _w_in_rows(D)
        dx, dg_pre, recv = in_proj_dx(side=([dwpT], _RowWindows(start, height)))
        parts['w_in'] = recv[0]
    else:
        dwpT = in_proj_dw()
        dx, dg_pre, _ = in_proj_dx()
    dwifT = _mm(dgates_b, h, mode="tn", out_dtype=F32, tm=IF_PAD, tn=2048, tk=2048, name="gates_dw")
    if not dist:
        g['w_in'] = jnp.concatenate([dwpT[:8 * D], dwifT[:N_IF], dwpT[8 * D:]], axis=0).T

    db_if = _col_sum(dgates)[0, :N_IF]
    g.update(g_pre=dg_pre[0], b_if=db_if, b_dw=cstats[2], g_ln=cstats[0], b_ln=cstats[1], g_ml_head=dg_head[0],
             g_mem=dg_mem[0], g_post=dg_post[0], w_in_gates=dwifT[:N_IF])
    return loss[0, 0], dx, g, parts


WEIGHTS = ('g_pre', 'w_in', 'b_if', 'w_qk_conv', 'w_dw', 'b_dw', 'g_ln', 'b_ln', 'w_conv_out', 'g_ml_head',
           'w_ml_out', 'g_mem', 'w_mem_kv', 'w_xa_out', 'w_out', 'g_post')
VECTORS = ('g_pre', 'b_dw', 'g_ln', 'b_ln', 'g_ml_head', 'g_mem', 'g_post')


def kernel(x, mem, g_pre, w_in, b_if, w_qk_conv, w_dw, b_dw, g_ln, b_ln, w_conv_out, g_ml_head, w_ml_out, g_mem, w_mem_kv, w_xa_out, w_out, g_post, loss_target, m_g_pre, m_w_in, m_b_if, m_w_qk_conv, m_w_dw, m_b_dw, m_g_ln, m_b_ln, m_w_conv_out, m_g_ml_head, m_w_ml_out, m_g_mem, m_w_mem_kv, m_w_xa_out, m_w_out, m_g_post, v_g_pre, v_w_in, v_b_if, v_w_qk_conv, v_w_dw, v_b_dw, v_g_ln, v_b_ln, v_w_conv_out, v_g_ml_head, v_w_ml_out, v_g_mem, v_w_mem_kv, v_w_xa_out, v_w_out, v_g_post):
    S, D = x.shape[1], x.shape[2]
    w = dict(g_pre=g_pre, w_in=w_in, b_if=b_if, w_qk_conv=w_qk_conv, w_dw=w_dw, b_dw=b_dw, g_ln=g_ln, b_ln=b_ln,
             w_conv_out=w_conv_out, g_ml_head=g_ml_head, w_ml_out=w_ml_out, g_mem=g_mem, w_mem_kv=w_mem_kv,
             w_xa_out=w_xa_out, w_out=w_out, g_post=g_post)
    mom = dict(g_pre=m_g_pre, w_in=m_w_in, b_if=m_b_if, w_qk_conv=m_w_qk_conv, w_dw=m_w_dw, b_dw=m_b_dw,
               g_ln=m_g_ln, b_ln=m_b_ln, w_conv_out=m_w_conv_out, g_ml_head=m_g_ml_head, w_ml_out=m_w_ml_out,
               g_mem=m_g_mem, w_mem_kv=m_w_mem_kv, w_xa_out=m_w_xa_out, w_out=m_w_out, g_post=m_g_post)
    var = dict(g_pre=v_g_pre, w_in=v_w_in, b_if=v_b_if, w_qk_conv=v_w_qk_conv, w_dw=v_w_dw, b_dw=v_b_dw,
               g_ln=v_g_ln, b_ln=v_b_ln, w_conv_out=v_w_conv_out, g_ml_head=v_g_ml_head, w_ml_out=v_w_ml_out,
               g_mem=v_g_mem, w_mem_kv=v_w_mem_kv, w_xa_out=v_w_xa_out, w_out=v_w_out, g_post=v_g_post)

    n_rows, gate_dev, gate_row, height, first, start = _w_in_rows(D)
    me = 4 * lax.axis_index("x") + 2 * lax.axis_index("y") + lax.axis_index("c")

    my_block = w_in.T.astype(BF16)
    my_window = _w_window(my_block, me, D)
    my_core, my_edges = _w_window_parts(my_window, me, D)
    a, b = FIRST_CORE, 1 - FIRST_CORE
    south = _core_gather(my_core, a, "gather_w_in_first")
    early, rest, _ = _w_in_tiles(D)
    w_early = _w_rows_of_tiles(early, {2 * q + a: _w_segments(2 * q + a, D, core=south[q]) for q in range(N_CHIP)}, D)
    my_gates = jnp.zeros((BF16_ROWS, D), BF16).at[:N_IF].set(my_block[gate_row:gate_row + N_IF])

    def rows_of_rest(north, south_edges):
        segments = {2 * q + a: _w_segments(2 * q + a, D, core=south[q], edges=south_edges[q]) for q in range(N_CHIP)}
        segments.update({2 * q + b: _w_segments(2 * q + b, D, window=north[q]) for q in range(N_CHIP)})
        return _w_rows_of_tiles(rest, segments, D)

    late = {n: w[n].astype(BF16) if n in LATE[:5] else w[n] for n in LATE}
    row = lambda a: a.reshape(1, D)

    loss, dx, g, parts = _local_step(
        x[0], mem[0], loss_target[0], row(g_pre), w_early, (my_window, my_edges, rows_of_rest), my_gates, b_if,
        row(b_dw), row(g_ln), row(b_ln), row(g_ml_head), row(g_mem), row(g_post), late, True)

    pad = lambda a: jnp.zeros((D,), F32).at[:N_IF].set(a)
    vec = jnp.concatenate([jnp.stack([g[n] for n in VECTORS] + [pad(g['b_if'])]), g['w_in_gates']])
    parts['vec'] = _exchange([vec], False, "gather_vector_grads")[0]

    out = {}
    for n in LATE:
        out[n] = _adamw(parts[n], w[n], mom[n], var[n], "adamw_" + n)
    zeros8 = jnp.zeros((N_IF, D), F32)
    stack = lambda d: jnp.concatenate([jnp.stack([d[n] for n in VECTORS] + [pad(d['b_if'])]), zeros8])
    vres = _adamw(parts['vec'], stack(w), stack(mom), stack(var), "adamw_vectors")
    for i, n in enumerate(VECTORS):
        out[n] = [r[i] for r in vres]
    out['b_if'] = [r[len(VECTORS), :N_IF] for r in vres]
    gate_grad = vres[0][len(VECTORS) + 1:]

    win = _sum_parts(parts['w_in'], "sum_w_in_grads")
    seg = lax.dynamic_slice(win, (first(me) - start(me), 0), (n_rows, D))
    with_gates = jnp.concatenate([seg[:gate_row], gate_grad, seg[gate_row:n_rows - N_IF]], axis=0)
    g_in = jnp.where(me == gate_dev, with_gates, seg)
    out['w_in'] = [r.T for r in _adamw(g_in[None], w_in.T, m_w_in.T, v_w_in.T, "adamw_w_in")]

    loss = lax.psum(loss, ("x", "y", "c"))
    res = [loss, dx.reshape(1, S, D)]
    for k in range(4):
        res += [out[n][k] for n in WEIGHTS]
    return tuple(res)
```
